```python
import math
import jax, jax.numpy as jnp
from jax import lax
import numpy as np

D_MODEL = 1024
BATCH = 8
SEQ = 4096
DEPTH = 1

D_MIX = D_MODEL
D_HYENA = D_MIX // 2
D_RWKV = D_MIX - D_HYENA
N_DIRS = 2
HYENA_ORDER = 2
SHORT_CONV = 3
FILTER_EMB = 33
FILTER_BANDS = (FILTER_EMB - 1) // 2
FILTER_WIDTH = 64
FILTER_INNER = 2
DECAY_TARGET = 1e-2
FAST_DECAY_PCT = 0.3
SLOW_DECAY_PCT = 1.5
FILTER_NORM_EPS = 1e-6
RWKV_HEAD = 64
RWKV_HEADS = D_RWKV // RWKV_HEAD
DECAY_LORA = 32
ICLR_LORA = 32
GATE_LORA = 96
GN_EPS = 64e-5
HY_COLS = (HYENA_ORDER + 1) * D_HYENA
RW_COLS = 3 * D_RWKV + N_DIRS * (DECAY_LORA + ICLR_LORA) + GATE_LORA
N_IN = HY_COLS + RW_COLS
N_EXPERTS = 256
TOP_K = 8
N_GROUPS = 8
TOPK_GROUPS = 4
GROUP_SCORE_K = 2
D_EXPERT = 256
ROUTE_SCALE = 2.5
EXPERT_BLOCK = 128
NORM_EPS = 1e-6
N_MOD = 6

kernel_name = 'hybrid_hyena_rwkv7_moe_block'


def rmsnorm(x, g):
    xf = x.astype(jnp.float32)
    y = xf * lax.rsqrt(jnp.mean(xf * xf, axis=-1, keepdims=True) + NORM_EPS)
    return (y * g.astype(jnp.float32)).astype(x.dtype)


def centred_conv(u, w, b):
    half = SHORT_CONV // 2
    T = u.shape[1]
    up = jnp.pad(u, ((0, 0), (half, half), (0, 0)))
    return sum(up[:, j:j + T] * w[j] for j in range(SHORT_CONV)) + b


def centred_shift(u):
    T = u.shape[1]
    up = jnp.pad(u, ((0, 0), (1, 1), (0, 0)))
    return 0.5 * (up[:, :T] + up[:, 2:])


def hyena_filters(L, pw1, pb1, pw2, pb2, pw3, freq):
    f32 = jnp.float32
    pw1, pb1, pw2, pb2, pw3, freq = (t.astype(f32) for t in (pw1, pb1, pw2, pb2, pw3, freq))
    pos = jnp.arange(L, dtype=f32)
    t = pos / max(L - 1, 1)
    ang = (2.0 * math.pi / L) * pos[:, None] * jnp.linspace(1e-4, FILTER_BANDS - 1, FILTER_BANDS)[None]
    feats = jnp.concatenate([t[:, None], jnp.cos(ang), -jnp.sin(ang)], axis=-1)
    hdn = jnp.sin(freq * (feats @ pw1 + pb1))
    for i in range(FILTER_INNER):
        hdn = jnp.sin(freq * (hdn @ pw2[i] + pb2[i]))
    filt = (hdn @ pw3).reshape(L, HYENA_ORDER, N_DIRS, D_HYENA)
    deltas = jnp.abs(jnp.linspace(math.log(DECAY_TARGET) / SLOW_DECAY_PCT,
                                  math.log(DECAY_TARGET) / FAST_DECAY_PCT, D_HYENA))
    filt = filt * jnp.exp(-t[:, None] * deltas[None])[:, None, None, :]
    fwd, bwd = filt[:, :, 0], filt[:, :, 1]
    two_sided = jnp.concatenate([fwd, jnp.zeros_like(fwd[:1]), bwd[:0:-1]], axis=0)
    return two_sided * lax.rsqrt(jnp.sum(two_sided * two_sided, axis=0, keepdims=True) + FILTER_NORM_EPS)


def fft_long_conv(u, k, skip):
    L = u.shape[1]
    u_f = jnp.fft.rfft(u, n=2 * L, axis=1)
    k_f = jnp.fft.rfft(k, axis=0)
    y = jnp.fft.irfft(u_f * k_f[None], n=2 * L, axis=1)[:, :L]
    return y + u * skip


def hyena_mixer(p, conv_w, conv_b, pw1, pb1, pw2, pb2, pw3, freq, skip):
    f32 = jnp.float32
    u = centred_conv(p.astype(f32), conv_w.astype(f32), conv_b.astype(f32))
    streams = jnp.split(u, HYENA_ORDER + 1, axis=-1)
    k = hyena_filters(p.shape[1], pw1, pb1, pw2, pb2, pw3, freq)
    skip = skip.astype(f32)
    z = streams[0]
    for n in range(HYENA_ORDER):
        z = streams[n + 1] * fft_long_conv(z, k[:, n], skip[n])
    return z


def wkv_scan(r, w, k, v, kk, a, reverse):
    B, T, H, N = r.shape
    xs = tuple(jnp.moveaxis(t, 1, 0) for t in (r, w, k, v, -kk, kk * a))

    def step(S, inp):
        r_t, w_t, k_t, v_t, a_t, b_t = inp
        sa = jnp.einsum('bhvk,bhk->bhv', S, a_t)
        S = S * w_t[:, :, None, :] + sa[..., None] * b_t[:, :, None, :] + v_t[..., None] * k_t[:, :, None, :]
        return S, jnp.einsum('bhvk,bhk->bhv', S, r_t)

    S0 = jnp.zeros((B, H, N, N), jnp.float32)
    _, out = lax.scan(step, S0, xs, reverse=reverse)
    return jnp.moveaxis(out, 0, 1)


def rwkv_mixer(p, mu, w0, w_up, a0, a_up, g_up, k_k, k_a, r_k, ln_w, ln_b):
    f32 = jnp.float32
    B, T, _ = p.shape
    C, H, N = D_RWKV, RWKV_HEADS, RWKV_HEAD
    p = p.astype(f32)
    p = p + mu.astype(f32) * (centred_shift(p) - p)
    r, k, v = p[..., :C], p[..., C:2 * C], p[..., 2 * C:3 * C]
    col = 3 * C
    w_lo = p[..., col:col + N_DIRS * DECAY_LORA].reshape(B, T, N_DIRS, DECAY_LORA)
    col += N_DIRS * DECAY_LORA
    a_lo = p[..., col:col + N_DIRS * ICLR_LORA].reshape(B, T, N_DIRS, ICLR_LORA)
    col += N_DIRS * ICLR_LORA
    g_lo = p[..., col:col + GATE_LORA]
    w_log = -jax.nn.softplus(-(w0.astype(f32) + jnp.einsum('btdl,dlc->btdc', jnp.tanh(w_lo), w_up.astype(f32)))) - 0.5
    decay = jnp.exp(-jnp.exp(w_log))
    a = jax.nn.sigmoid(a0.astype(f32) + jnp.einsum('btdl,dlc->btdc', a_lo, a_up.astype(f32)))
    g = jax.nn.sigmoid(g_lo) @ g_up.astype(f32)
    heads = lambda t: t.reshape(B, T, H, N)
    kk = heads(k * k_k.astype(f32))
    kk = kk / jnp.maximum(jnp.sqrt(jnp.sum(kk * kk, axis=-1, keepdims=True)), 1e-12)
    k_dir = k[:, :, None, :] * (1.0 + (a - 1.0) * k_a.astype(f32))
    r_h, v_h = heads(r), heads(v)
    o_f = wkv_scan(r_h, heads(decay[:, :, 0]), heads(k_dir[:, :, 0]), v_h, kk, heads(a[:, :, 0]), False)
    o_b = wkv_scan(r_h, heads(decay[:, :, 1]), heads(k_dir[:, :, 1]), v_h, kk, heads(a[:, :, 1]), True)
    s = o_f + o_b
    mean = jnp.mean(s, axis=-1, keepdims=True)
    var = jnp.mean(jnp.square(s - mean), axis=-1, keepdims=True)
    s = ((s - mean) * lax.rsqrt(var + GN_EPS)).reshape(B, T, C) * ln_w.astype(f32) + ln_b.astype(f32)
    bonus = jnp.sum(heads(r * jnp.sum(k_dir, axis=2) * r_k.astype(f32)), axis=-1, keepdims=True) * v_h
    return (s + bonus.reshape(B, T, C)) * g


def swiglu(x, wg, wu, wd):
    return (jax.nn.silu(x @ wg) * (x @ wu)) @ wd


def route(h, router_w, router_bias):
    Ntok = h.shape[0]
    scores = jax.nn.sigmoid(h.astype(jnp.float32) @ router_w.astype(jnp.float32))
    biased = scores + router_bias.astype(jnp.float32)
    grp = biased.reshape(Ntok, N_GROUPS, N_EXPERTS // N_GROUPS)
    grp_score = jnp.sum(lax.top_k(grp, GROUP_SCORE_K)[0], axis=-1)
    _, gidx = lax.top_k(grp_score, TOPK_GROUPS)
    gmask = jnp.any(gidx[..., None] == jnp.arange(N_GROUPS), axis=1)
    emask = jnp.repeat(gmask, N_EXPERTS // N_GROUPS, axis=1)
    _, eidx = lax.top_k(jnp.where(emask, biased, -jnp.inf), TOP_K)
    wsel = jnp.take_along_axis(scores, eidx, axis=-1)
    wsel = wsel / jnp.sum(wsel, axis=-1, keepdims=True) * ROUTE_SCALE
    return eidx, wsel


def routed_experts(h, eidx, wsel, wg, wu, wd):
    Ntok, D = h.shape
    A = Ntok * TOP_K
    flat_e = eidx.reshape(-1).astype(jnp.int32)
    flat_w = wsel.reshape(-1).astype(h.dtype)
    order = jnp.argsort(flat_e)
    sorted_e = flat_e[order]
    counts = jnp.bincount(flat_e, length=N_EXPERTS).astype(jnp.int32)
    offsets = jnp.cumsum(counts) - counts
    padded = (counts + EXPERT_BLOCK - 1) // EXPERT_BLOCK * EXPERT_BLOCK
    pad_end = jnp.cumsum(padded)
    pad_off = pad_end - padded
    dest = pad_off[sorted_e] + jnp.arange(A, dtype=jnp.int32) - offsets[sorted_e]
    n_blocks = -(-(A + N_EXPERTS * (EXPERT_BLOCK - 1)) // EXPERT_BLOCK)
    P = n_blocks * EXPERT_BLOCK
    buf_tok = jnp.full((P,), Ntok, jnp.int32).at[dest].set((order // TOP_K).astype(jnp.int32))
    buf_w = jnp.zeros((P,), h.dtype).at[dest].set(flat_w[order])
    block_e = jnp.clip(jnp.searchsorted(pad_end, jnp.arange(n_blocks, dtype=jnp.int32) * EXPERT_BLOCK,
                                        side='right'), 0, N_EXPERTS - 1)
    h_pad = jnp.concatenate([h, jnp.zeros((1, D), h.dtype)], axis=0)

    def body(y, blk):
        tok, wt, e = blk
        out = swiglu(h_pad[tok], wg[e], wu[e], wd[e])
        return y.at[tok].add(out * wt[:, None]), None

    y0 = jnp.zeros((Ntok + 1, D), h.dtype)
    y, _ = lax.scan(body, y0, (buf_tok.reshape(n_blocks, EXPERT_BLOCK),
                               buf_w.reshape(n_blocks, EXPERT_BLOCK), block_e))
    return y[:Ntok]


def setup_inputs(seed: int = 0) -> dict:
    key = jax.random.key(seed)
    ks = iter(jax.random.split(key, 40))
    L = DEPTH

    def nrm(shape, scale):
        return scale * jax.random.normal(next(ks), shape, jnp.float32)

    def gain(shape):
        return 1.0 + 0.02 * jax.random.normal(next(ks), shape, jnp.float32)

    return {
        'x': nrm((BATCH, SEQ, D_MODEL), 1.0),
        'c': nrm((BATCH, D_MODEL), 1.0),
        'norm1_g': gain((L, D_MODEL)),
        'norm2_g': gain((L, D_MODEL)),
        'normf_g': gain((D_MODEL,)),
        'w_ada': nrm((L, D_MODEL, N_MOD * D_MODEL), 0.5 * D_MODEL ** -0.5),
        'b_ada': nrm((L, N_MOD * D_MODEL), 0.02),
        'w_in': nrm((L, D_MODEL, N_IN), D_MODEL ** -0.5),
        'w_out': nrm((L, D_MIX, D_MODEL), D_MIX ** -0.5),
        'hy_conv_w': nrm((L, SHORT_CONV, HY_COLS), SHORT_CONV ** -0.5),
        'hy_conv_b': nrm((L, HY_COLS), 0.02),
        'hy_pos_w1': nrm((L, FILTER_EMB, FILTER_WIDTH), FILTER_EMB ** -0.5),
        'hy_pos_b1': nrm((L, FILTER_WIDTH), 0.1),
        'hy_pos_w2': nrm((L, FILTER_INNER, FILTER_WIDTH, FILTER_WIDTH), FILTER_WIDTH ** -0.5),
        'hy_pos_b2': nrm((L, FILTER_INNER, FILTER_WIDTH), 0.1),
        'hy_pos_w3': nrm((L, FILTER_WIDTH, HYENA_ORDER * N_DIRS * D_HYENA), FILTER_WIDTH ** -0.5),
        'hy_sin_freq': gain((L, FILTER_WIDTH)),
        'hy_skip': nrm((L, HYENA_ORDER, D_HYENA), 1.0),
        'rw_mu': jax.random.uniform(next(ks), (L, RW_COLS), jnp.float32),
        'rw_w0': jax.random.uniform(next(ks), (L, N_DIRS, D_RWKV), jnp.float32, minval=-6.5, maxval=-1.5),
        'rw_w_up': nrm((L, N_DIRS, DECAY_LORA, D_RWKV), 0.1),
        'rw_a0': nrm((L, N_DIRS, D_RWKV), 0.1),
        'rw_a_up': nrm((L, N_DIRS, ICLR_LORA, D_RWKV), 0.1),
        'rw_g_up': nrm((L, GATE_LORA, D_RWKV), GATE_LORA ** -0.5),
        'rw_k_k': 0.85 + nrm((L, D_RWKV), 0.02),
        'rw_k_a': gain((L, D_RWKV)),
        'rw_r_k': nrm((L, D_RWKV), 0.1),
        'rw_ln_w': gain((L, D_RWKV)),
        'rw_ln_b': nrm((L, D_RWKV), 0.02),
        'router_w': nrm((L, D_MODEL, N_EXPERTS), D_MODEL ** -0.5),
        'router_bias': nrm((L, N_EXPERTS), 0.01),
        'exp_w_gate': nrm((L, N_EXPERTS, D_MODEL, D_EXPERT), D_MODEL ** -0.5),
        'exp_w_up': nrm((L, N_EXPERTS, D_MODEL, D_EXPERT), D_MODEL ** -0.5),
        'exp_w_down': nrm((L, N_EXPERTS, D_EXPERT, D_MODEL), D_EXPERT ** -0.5),
        'sh_w_gate': nrm((L, D_MODEL, D_EXPERT), D_MODEL ** -0.5),
        'sh_w_up': nrm((L, D_MODEL, D_EXPERT), D_MODEL ** -0.5),
        'sh_w_down': nrm((L, D_EXPERT, D_MODEL), D_EXPERT ** -0.5),
    }


def reference(x, c, norm1_g, norm2_g, normf_g, w_ada, b_ada, w_in, w_out,
              hy_conv_w, hy_conv_b, hy_pos_w1, hy_pos_b1, hy_pos_w2, hy_pos_b2, hy_pos_w3,
              hy_sin_freq, hy_skip, rw_mu, rw_w0, rw_w_up, rw_a0, rw_a_up, rw_g_up,
              rw_k_k, rw_k_a, rw_r_k, rw_ln_w, rw_ln_b, router_w, router_bias,
              exp_w_gate, exp_w_up, exp_w_down, sh_w_gate, sh_w_up, sh_w_down):
    B, T, D = x.shape
    for l in range(DEPTH):
        mod = jax.nn.silu(c) @ w_ada[l] + b_ada[l]
        sh1, sc1, g1, sh2, sc2, g2 = jnp.split(mod, N_MOD, axis=-1)
        h = rmsnorm(x, norm1_g[l]) * (1.0 + sc1[:, None, :]) + sh1[:, None, :]
        p = h @ w_in[l]
        y_hy = hyena_mixer(p[..., :HY_COLS], hy_conv_w[l], hy_conv_b[l], hy_pos_w1[l], hy_pos_b1[l],
                           hy_pos_w2[l], hy_pos_b2[l], hy_pos_w3[l], hy_sin_freq[l], hy_skip[l])
        y_rw = rwkv_mixer(p[..., HY_COLS:], rw_mu[l], rw_w0[l], rw_w_up[l], rw_a0[l], rw_a_up[l],
                          rw_g_up[l], rw_k_k[l], rw_k_a[l], rw_r_k[l], rw_ln_w[l], rw_ln_b[l])
        mix = jnp.concatenate([y_hy, y_rw], axis=-1).astype(x.dtype) @ w_out[l]
        x = x + g1[:, None, :] * mix
        h = rmsnorm(x, norm2_g[l]) * (1.0 + sc2[:, None, :]) + sh2[:, None, :]
        hf = h.reshape(B * T, D)
        eidx, wsel = route(hf, router_w[l], router_bias[l])
        ffn = swiglu(hf, sh_w_gate[l], sh_w_up[l], sh_w_down[l]) + \
            routed_experts(hf, eidx, wsel, exp_w_gate[l], exp_w_up[l], exp_w_down[l])
        x = x + g2[:, None, :] * ffn.reshape(B, T, D)
    return rmsnorm(x, normf_g)
```

```python
import functools
import math

import jax
import jax.numpy as jnp
import numpy as np
from jax import lax
from jax.experimental import pallas as pl
from jax.experimental.pallas import tpu as pltpu

F32 = jnp.float32
BF16 = jnp.bfloat16

LANES = 128
MXU_DIM = 256
VMEM_LIMIT = 56 * 1024 * 1024

D_HYENA = 512
D_RWKV = 512
HEAD = 64
N_HEADS = D_RWKV // HEAD
HYENA_ORDER = 2
FILTER_BANDS = 16
DECAY_TARGET = 1e-2
FAST_DECAY_PCT = 0.3
SLOW_DECAY_PCT = 1.5
FILTER_NORM_EPS = 1e-6
DECAY_LORA = 32
ICLR_LORA = 32
GATE_LORA = 96
GN_EPS = 64e-5
NORM_EPS = 1e-6
N_EXPERTS = 256
TOP_K = 8
N_GROUPS = 8
TOPK_GROUPS = 4
ROUTE_SCALE = 2.5
D_EXPERT = 256


def _params(*sem):
    return pltpu.CompilerParams(dimension_semantics=sem, vmem_limit_bytes=VMEM_LIMIT)


def _split2(a):
    hi = a.astype(BF16)
    lo = (a - hi.astype(F32)).astype(BF16)
    return hi, lo


def _dot(a, b):
    return jnp.dot(a, b, preferred_element_type=F32)


def _dot3(a, b):
    ah, al = _split2(a)
    bh, bl = _split2(b)
    return _dot(ah, bh) + (_dot(ah, bl) + _dot(al, bh))


def _dot_exact_rhs(a, b_bf16):
    ah, al = _split2(a)
    return _dot(ah, b_bf16) + _dot(al, b_bf16)


def _silu(x):
    return x * jax.nn.sigmoid(x)


def _mod_kernel(c_ref, w_ref, b_ref, o_ref):
    o_ref[...] = _dot3(_silu(c_ref[...]), w_ref[...]) + b_ref[...]


def _modulation(c, w_ada, b_ada):
    bsz, d = c.shape
    n = w_ada.shape[1]
    blk = 1024
    return pl.pallas_call(
        _mod_kernel,
        grid=(n // blk,),
        in_specs=[
            pl.BlockSpec((bsz, d), lambda j: (0, 0)),
            pl.BlockSpec((d, blk), lambda j: (0, j)),
            pl.BlockSpec((1, blk), lambda j: (0, j)),
        ],
        out_specs=pl.BlockSpec((bsz, blk), lambda j: (0, j)),
        out_shape=jax.ShapeDtypeStruct((bsz, n), F32),
        compiler_params=_params("arbitrary"),
        name="adaln_mod",
    )(c, w_ada, b_ada.reshape(1, n))


def _filter_kernel(band_ref, w1_ref, b1_ref, w2_ref, b2_ref, w3_ref, freq_ref, delta_ref,
                   k_ref, ss_ref, *, seq, rows):
    half = pl.program_id(0)
    i = pl.program_id(1)
    r = lax.broadcasted_iota(jnp.int32, (rows, LANES), 0) + i * rows
    pos = jnp.where(half == 0, r, seq - r).astype(F32)
    tt = pos / float(max(seq - 1, 1))
    lane = lax.broadcasted_iota(jnp.int32, (rows, LANES), 1)
    ang = pos * band_ref[...]
    feats = jnp.where(lane == 0, tt,
                      jnp.where(lane <= FILTER_BANDS, jnp.cos(ang),
                                jnp.where(lane <= 2 * FILTER_BANDS, -jnp.sin(ang), 0.0)))
    freq = freq_ref[...]
    hdn = jnp.sin(freq * (_dot3(feats, w1_ref[...]) + b1_ref[...]))
    for j in range(w2_ref.shape[0]):
        hdn = jnp.sin(freq * (_dot3(hdn, w2_ref[j]) + b2_ref[j]))
    filt = _dot3(hdn, w3_ref[...])
    filt = filt * jnp.exp(-tt[:, :1] * delta_ref[...])
    valid = jnp.logical_or(half == 0, r[:, :1] > 0)
    filt = jnp.where(valid, filt, 0.0)
    k_ref[...] = filt

    @pl.when(jnp.logical_and(half == 0, i == 0))
    def _():
        ss_ref[...] = jnp.zeros_like(ss_ref)

    ss_ref[...] += jnp.broadcast_to(jnp.sum(filt * filt, axis=0, keepdims=True), ss_ref.shape)


def _hyena_filters(seq, pw1, pb1, pw2, pb2, pw3, freq):
    width = pw1.shape[1]
    ncol = HYENA_ORDER * D_HYENA
    rows = min(seq, 512)
    bands = np.zeros((1, LANES), np.float64)
    lin = np.linspace(1e-4, FILTER_BANDS - 1, FILTER_BANDS)
    bands[0, 1:1 + FILTER_BANDS] = lin
    bands[0, 1 + FILTER_BANDS:1 + 2 * FILTER_BANDS] = lin
    bands = jnp.asarray(bands * (2.0 * math.pi / seq), F32)
    deltas = np.abs(np.linspace(math.log(DECAY_TARGET) / SLOW_DECAY_PCT,
                                math.log(DECAY_TARGET) / FAST_DECAY_PCT, D_HYENA))
    deltas = jnp.asarray(np.tile(deltas, HYENA_ORDER)[None], F32)
    w1 = jnp.zeros((LANES, width), F32).at[:pw1.shape[0]].set(pw1)
    w3 = pw3.reshape(width, HYENA_ORDER, 2, D_HYENA).transpose(2, 0, 1, 3).reshape(2, width, ncol)
    nt = seq // rows
    full = lambda *shape: pl.BlockSpec(shape, lambda h, i: (0,) * len(shape))
    return pl.pallas_call(
        functools.partial(_filter_kernel, seq=seq, rows=rows),
        grid=(2, nt),
        in_specs=[
            full(1, LANES), full(LANES, width), full(1, width),
            full(pw2.shape[0], width, width), full(pw2.shape[0], 1, width),
            pl.BlockSpec((None, width, ncol), lambda h, i: (h, 0, 0)),
            full(1, width), full(1, ncol),
        ],
        out_specs=[
            pl.BlockSpec((rows, ncol), lambda h, i: (h * nt + i, 0)),
            pl.BlockSpec((8, ncol), lambda h, i: (0, 0)),
        ],
        out_shape=[jax.ShapeDtypeStruct((2 * seq, ncol), F32),
                   jax.ShapeDtypeStruct((8, ncol), F32)],
        compiler_params=_params("arbitrary", "arbitrary"),
        name="hyena_filters",
    )(bands, w1, pb1.reshape(1, width), pw2, pb2.reshape(pw2.shape[0], 1, width), w3,
      freq.reshape(1, width), deltas)


N1 = LANES


def _dft_tables(seq):
    m = 2 * seq
    n2 = m // N1
    n2h = n2 // 2
    n1 = np.arange(N1)[:, None, None]
    f2 = np.arange(n2)[None, :, None]
    k2 = np.arange(n2)[None, None, :]
    th = 2.0 * np.pi * (n1 * f2 / m + (k2 * f2 % n2) / n2)
    fwd_a = np.concatenate([np.cos(th), -np.sin(th)], axis=1)
    tht = np.transpose(th, (0, 2, 1))
    inv_a = np.concatenate([np.cos(tht), -np.sin(tht)], axis=2)[:, :n2h] / m
    a = np.arange(N1)
    ph = 2.0 * np.pi * np.outer(a, a) / N1
    c, s = np.cos(ph), np.sin(ph)
    fwd_b = np.block([[c, s], [-s, c]])
    inv_b = np.block([[c, -s], [s, c]])
    cast = lambda t: jnp.asarray(t, BF16)
    return cast(fwd_a), cast(fwd_a[:, :, :n2h]), cast(inv_a), cast(fwd_b), cast(inv_b), n2, n2h


def _stage_a_fwd(x_ref, wa_ref, y_ref, n2, scale=None):
    def body(n1, carry):
        xs = x_ref[pl.ds(n1, wa_ref.shape[2], stride=N1), :]
        if scale is not None:
            xs = xs * scale
        a = _dot(wa_ref[n1], xs.astype(BF16))
        y_ref[pl.ds(n1, n2, stride=2 * N1), :] = a[:n2]
        y_ref[pl.ds(N1 + n1, n2, stride=2 * N1), :] = a[n2:]
        return carry
    lax.fori_loop(0, N1, body, 0)


def _filter_fft_kernel(k_ref, ss_ref, wa_ref, fb_ref, o_ref, y_ref, *, n2):
    scale = lax.rsqrt(ss_ref[0:1, :] + FILTER_NORM_EPS)
    _stage_a_fwd(k_ref, wa_ref, y_ref, n2, scale=scale)

    def body(f2, carry):
        off = pl.multiple_of(f2 * 2 * N1, 2 * N1)
        o_ref[f2] = _dot(fb_ref[...], y_ref[pl.ds(off, 2 * N1), :].astype(BF16))
        return carry
    lax.fori_loop(0, n2, body, 0)


def _filter_spectrum(k2, ss, seq):
    fwd_a, _, _, fwd_b, _, n2, _ = _dft_tables(seq)
    ncol = k2.shape[1]
    nblk = ncol // LANES
    return pl.pallas_call(
        functools.partial(_filter_fft_kernel, n2=n2),
        grid=(nblk,),
        in_specs=[
            pl.BlockSpec((2 * seq, LANES), lambda c: (0, c)),
            pl.BlockSpec((8, LANES), lambda c: (0, c)),
            pl.BlockSpec(fwd_a.shape, lambda c: (0, 0, 0)),
            pl.BlockSpec(fwd_b.shape, lambda c: (0, 0)),
        ],
        out_specs=pl.BlockSpec((None, n2, 2 * N1, LANES), lambda c: (c, 0, 0, 0)),
        out_shape=jax.ShapeDtypeStruct((nblk, n2, 2 * N1, LANES), F32),
        scratch_shapes=[pltpu.VMEM((n2 * 2 * N1, LANES), F32)],
        compiler_params=_params("arbitrary"),
        name="hyena_filter_fft",
    )(k2, ss, fwd_a, fwd_b)


def _conv_kernel(u_ref, g_ref, kh_ref, skip_ref, wa_ref, va_ref, fb_ref, ib_ref, o_ref, y_ref,
                 *, n2, n2h):
    _stage_a_fwd(u_ref, wa_ref, y_ref, n2)

    def mid(f2, carry):
        off = pl.multiple_of(f2 * 2 * N1, 2 * N1)
        z = _dot(fb_ref[...], y_ref[pl.ds(off, 2 * N1), :].astype(BF16))
        zr, zi = z[:N1], z[N1:]
        kh = kh_ref[f2]
        kr, ki = kh[:N1], kh[N1:]
        p = jnp.concatenate([zr * kr - zi * ki, zr * ki + zi * kr], axis=0)
        y_ref[pl.ds(off, 2 * N1), :] = _dot(ib_ref[...], p.astype(BF16))
        return carry
    lax.fori_loop(0, n2, mid, 0)

    skip = skip_ref[...]

    def last(n1, carry):
        g = jnp.concatenate([y_ref[pl.ds(n1, n2, stride=2 * N1), :],
                             y_ref[pl.ds(N1 + n1, n2, stride=2 * N1), :]], axis=0)
        conv = _dot(va_ref[n1], g.astype(BF16))
        rows = pl.ds(n1, n2h, stride=N1)
        u = u_ref[rows, :]
        o_ref[rows, :] = g_ref[rows, :] * (conv + u * skip)
        return carry
    lax.fori_loop(0, N1, last, 0)


def _long_conv_gate(u, u_col, gate, gate_col, khat, skip, order):
    bsz, seq, _ = u.shape
    ch = D_HYENA
    _, fwd_a, inv_a, fwd_b, inv_b, n2, n2h = _dft_tables(seq)
    nblk = ch // LANES
    const = lambda a: pl.BlockSpec(a.shape, lambda c, b: (0,) * a.ndim)
    at = lambda col: pl.BlockSpec((None, seq, LANES), lambda c, b: (b, 0, col // LANES + c))
    data = at(0)
    return pl.pallas_call(
        functools.partial(_conv_kernel, n2=n2, n2h=n2h),
        grid=(nblk, bsz),
        in_specs=[
            at(u_col), at(gate_col),
            pl.BlockSpec((None, n2, 2 * N1, LANES), lambda c, b: (order * nblk + c, 0, 0, 0)),
            pl.BlockSpec((1, LANES), lambda c, b: (0, c)),
            const(fwd_a), const(inv_a), const(fwd_b), const(inv_b),
        ],
        out_specs=data,
        out_shape=jax.ShapeDtypeStruct((bsz, seq, ch), F32),
        scratch_shapes=[pltpu.VMEM((n2 * 2 * N1, LANES), F32)],
        compiler_params=_params("arbitrary", "arbitrary"),
        name=f"hyena_conv{order}",
    )(u, gate, khat, skip[order].reshape(1, ch), fwd_a, inv_a, fwd_b, inv_b)


HALO = 8


def _shift_rows(p, k):
    return pltpu.roll(p, k % p.shape[0], axis=0)


def _proj_kernel(xp_ref, x_ref, xn_ref, mod_ref, g1_ref, why_ref, wrkv_ref, wlora_ref,
                 cw_ref, cb_ref, murkv_ref, mulora_ref, w0_ref, a0_ref, wwa_ref, gup_ref,
                 kk_ref, ka_ref, rk_ref, ones_ref,
                 uhy_ref, r_ref, k_ref, v_ref, kkn_ref, lw_ref, a_ref, g_ref, bonus_ref,
                 *, tt, nt):
    i = pl.program_id(1)
    xe = jnp.concatenate([xp_ref[...], x_ref[...], xn_ref[...]], axis=0)
    ms = jnp.mean(xe * xe, axis=-1, keepdims=True)
    h = xe * lax.rsqrt(ms + NORM_EPS) * g1_ref[...]
    h = h * (1.0 + mod_ref[1:2, :]) + mod_ref[0:1, :]
    row = lax.broadcasted_iota(jnp.int32, (tt + 2 * HALO, 1), 0)
    inside = jnp.logical_and(jnp.logical_or(row >= HALO, i > 0),
                             jnp.logical_or(row < tt + HALO, i < nt - 1))
    hb = jnp.where(inside, h, 0.0).astype(BF16)
    mid = slice(HALO, tt + HALO)

    p = _dot(hb, why_ref[...])
    u = (_shift_rows(p, 1) * cw_ref[0:1, :] + p * cw_ref[1:2, :]
         + _shift_rows(p, -1) * cw_ref[2:3, :] + cb_ref[...])
    uhy_ref[...] = u[mid]

    p = _dot(hb, wrkv_ref[...])
    p = p + murkv_ref[...] * (0.5 * (_shift_rows(p, 1) + _shift_rows(p, -1)) - p)
    p = p[mid]
    c = D_RWKV
    r, k, v = p[:, :c], p[:, c:2 * c], p[:, 2 * c:]
    r_ref[...] = r
    k_ref[...] = k
    v_ref[...] = v

    q = _dot(hb, wlora_ref[...])
    q = q + mulora_ref[...] * (0.5 * (_shift_rows(q, 1) + _shift_rows(q, -1)) - q)
    q = q[mid]
    wa = q[:, :LANES]
    lane = lax.broadcasted_iota(jnp.int32, wa.shape, 1)
    wa = jnp.where(lane < 2 * DECAY_LORA, jnp.tanh(wa), wa)
    up = _dot3(wa, wwa_ref[...])
    z = -(w0_ref[...] + up[:, :2 * c])
    softplus = jnp.maximum(z, 0.0) + jnp.log1p(jnp.exp(-jnp.abs(z)))
    lw_ref[...] = -jnp.exp(-softplus - 0.5)
    a = jax.nn.sigmoid(a0_ref[...] + up[:, 2 * c:])
    a_ref[...] = a
    g_ref[...] = _dot3(jax.nn.sigmoid(q[:, LANES:]), gup_ref[...])

    ones = ones_ref[...]
    kk = k * kk_ref[...]
    nrm = jnp.sqrt(_dot_exact_rhs(kk * kk, ones))
    kkn_ref[...] = kk / jnp.maximum(nrm, 1e-12)
    ka = ka_ref[...]
    ksum = k * (2.0 + (a[:, :c] + a[:, c:] - 2.0) * ka)
    bonus_ref[...] = _dot_exact_rhs(r * ksum * rk_ref[...], ones) * v


def _head_ones():
    hid = np.arange(D_RWKV) // HEAD
    return jnp.asarray(hid[:, None] == hid[None, :], BF16)


def _projection(x, mod, norm1_g, w_in, hy_conv_w, hy_conv_b, rw_mu, rw_w0, rw_w_up, rw_a0,
                rw_a_up, rw_g_up, rw_k_k, rw_k_a, rw_r_k, tt=256):
    bsz, seq, d = x.shape
    tt = min(tt, seq)
    nt = seq // tt
    c = D_RWKV
    hy = (HYENA_ORDER + 1) * D_HYENA
    nlora = 2 * LANES
    w_hy = w_in[:, :hy].astype(BF16)
    w_rkv = w_in[:, hy:hy + 3 * c].astype(BF16)
    w_lora = jnp.zeros((d, nlora), F32).at[:, :w_in.shape[1] - hy - 3 * c].set(w_in[:, hy + 3 * c:]).astype(BF16)
    mu_rkv = rw_mu[:3 * c].reshape(1, 3 * c)
    mu_lora = jnp.zeros((1, nlora), F32).at[0, :rw_mu.shape[0] - 3 * c].set(rw_mu[3 * c:])
    wwa = jnp.zeros((LANES, 4 * c), F32)
    for dd in range(2):
        wwa = wwa.at[dd * DECAY_LORA:(dd + 1) * DECAY_LORA, dd * c:(dd + 1) * c].set(rw_w_up[dd])
        wwa = wwa.at[2 * DECAY_LORA + dd * ICLR_LORA:2 * DECAY_LORA + (dd + 1) * ICLR_LORA,
                     2 * c + dd * c:2 * c + (dd + 1) * c].set(rw_a_up[dd])
    gup = jnp.zeros((LANES, c), F32).at[:GATE_LORA].set(rw_g_up)
    row = lambda a: a.reshape(1, -1)

    nb8 = seq // HALO
    tb = tt // HALO
    const = lambda a: pl.BlockSpec(a.shape, lambda b, i: (0,) * a.ndim)
    tile = lambda w: pl.BlockSpec((None, tt, w), lambda b, i: (b, i, 0))
    ins = [
        (x, pl.BlockSpec((None, HALO, d), lambda b, i: (b, jnp.maximum(i * tb - 1, 0), 0))),
        (x, pl.BlockSpec((None, tt, d), lambda b, i: (b, i, 0))),
        (x, pl.BlockSpec((None, HALO, d), lambda b, i: (b, jnp.minimum((i + 1) * tb, nb8 - 1), 0))),
        (mod, pl.BlockSpec((None,) + mod.shape[1:], lambda b, i: (b, 0, 0))),
    ]
    consts = [row(norm1_g), w_hy, w_rkv, w_lora, hy_conv_w, row(hy_conv_b), mu_rkv, mu_lora,
              row(rw_w0), row(rw_a0), wwa, gup, row(rw_k_k), row(rw_k_a), row(rw_r_k), _head_ones()]
    ins += [(a, const(a)) for a in consts]
    widths = [hy, c, c, c, c, 2 * c, 2 * c, c, c]
    return pl.pallas_call(
        functools.partial(_proj_kernel, tt=tt, nt=nt),
        grid=(bsz, nt),
        in_specs=[s for _, s in ins],
        out_specs=[tile(w) for w in widths],
        out_shape=[jax.ShapeDtypeStruct((bsz, seq, w), F32) for w in widths],
        compiler_params=_params("arbitrary", "arbitrary"),
        name="input_projection",
    )(*[a for a, _ in ins])


CHUNK = HEAD
GROUP = MXU_DIM // HEAD


def _nt(a, b):
    return lax.dot_general(a, b, (((1,), (1,)), ((), ())), preferred_element_type=F32)


def _tn(a, b):
    return lax.dot_general(a, b, (((0,), (0,)), ((), ())), preferred_element_type=F32)


def _wkv_direction(r, k, v, kk, lw, a, ka, s_ref, reverse):
    c = CHUNK
    ti = lax.broadcasted_iota(jnp.int32, (c, c), 0)
    si = lax.broadcasted_iota(jnp.int32, (c, c), 1)
    tri = (si >= ti) if reverse else (si <= ti)
    cum = _dot_exact_rhs_lhs(jnp.where(tri, 1.0, 0.0).astype(BF16), lw)
    tot = jnp.sum(lw, axis=0, keepdims=True)
    w_incl = jnp.exp(cum)
    w_prev = jnp.exp(cum - lw)
    w_inv = jnp.exp(-cum)
    w_end = jnp.exp(tot - cum)
    w_tot = jnp.exp(tot)
    kd = k * (1.0 + (a - 1.0) * ka)
    b = kk * a
    a_w = -kk * w_prev
    r_w = r * w_incl
    b_w = b * w_inv
    k_w = kd * w_inv
    b_e = b * w_end
    k_e = kd * w_end

    m = MXU_DIM
    ri = lax.broadcasted_iota(jnp.int32, (m, m), 0)
    ci = lax.broadcasted_iota(jnp.int32, (m, m), 1)
    head_mask = (ri // HEAD) == (ci // HEAD)
    tl, sl = ri % c, ci % c
    strict = (sl > tl) if reverse else (sl < tl)
    incl = (sl >= tl) if reverse else (sl <= tl)
    eye = jnp.where(ri == ci, 1.0, 0.0)

    def stack(xg):
        return jnp.where(head_mask, jnp.concatenate([xg] * GROUP, axis=0), 0.0).astype(BF16)

    outs = []
    for g in range(D_RWKV // m):
        sl_g = slice(g * m, (g + 1) * m)
        a_st, r_st, b_st, k_st = stack(a_w[:, sl_g]), stack(r_w[:, sl_g]), stack(b_w[:, sl_g]), stack(k_w[:, sl_g])
        be_st, ke_st, v_st = stack(b_e[:, sl_g]), stack(k_e[:, sl_g]), stack(v[:, sl_g])
        s = s_ref[g]
        sb = s.astype(BF16)
        m_ab = jnp.where(strict, _nt(a_st, b_st), 0.0)
        m_ak = jnp.where(strict, _nt(a_st, k_st), 0.0)
        m_rb = jnp.where(incl, _nt(r_st, b_st), 0.0)
        m_rk = jnp.where(incl, _nt(r_st, k_st), 0.0)
        rhs = _nt(a_st, sb) + _dot(m_ak.astype(BF16), v_st)
        t = eye + m_ab
        pw = m_ab
        for _ in range(int(math.log2(c)) - 1):
            pb = pw.astype(BF16)
            pw = _dot(pb, pb)
            t = t + _dot(t.astype(BF16), pw.astype(BF16))
        u = _dot(t.astype(BF16), rhs.astype(BF16))
        ub = u.astype(BF16)
        o_st = _nt(r_st, sb) + _dot(m_rb.astype(BF16), ub) + _dot(m_rk.astype(BF16), v_st)
        o = o_st[0:c]
        for hh in range(1, GROUP):
            o = o + o_st[hh * c:(hh + 1) * c]
        outs.append(o)
        s_ref[g] = s * w_tot[:, sl_g] + _tn(ub, be_st) + _tn(v_st, ke_st)
    return jnp.concatenate(outs, axis=1)


def _dot_exact_rhs_lhs(tri_bf16, x):
    xh, xl = _split2(x)
    return _dot(tri_bf16, xh) + _dot(tri_bf16, xl)


def _wkv_kernel(rf, kf, vf, kkf, lwf, af, rb, kb, vb, kkb, lwb, ab, ka_ref, of_ref, ob_ref, s_ref):
    @pl.when(pl.program_id(1) == 0)
    def _():
        s_ref[...] = jnp.zeros_like(s_ref)

    ka = ka_ref[...]
    of_ref[...] = _wkv_direction(rf[...], kf[...], vf[...], kkf[...], lwf[...], af[...], ka,
                                 s_ref.at[0], False)
    ob_ref[...] = _wkv_direction(rb[...], kb[...], vb[...], kkb[...], lwb[...], ab[...], ka,
                                 s_ref.at[1], True)


def _wkv(r, k, v, kk, lw, a, rw_k_a):
    bsz, seq, c = r.shape
    nc = seq // CHUNK
    fwd = lambda lane_blk: pl.BlockSpec((None, CHUNK, c), lambda b, j: (b, j, lane_blk))
    bwd = lambda lane_blk: pl.BlockSpec((None, CHUNK, c), lambda b, j: (b, nc - 1 - j, lane_blk))
    return pl.pallas_call(
        _wkv_kernel,
        grid=(bsz, nc),
        in_specs=[fwd(0)] * 4 + [fwd(0), fwd(0)] + [bwd(0)] * 4 + [bwd(1), bwd(1)]
        + [pl.BlockSpec((1, c), lambda b, j: (0, 0))],
        out_specs=[fwd(0), bwd(0)],
        out_shape=[jax.ShapeDtypeStruct((bsz, seq, c), F32)] * 2,
        scratch_shapes=[pltpu.VMEM((2, c // MXU_DIM, MXU_DIM, MXU_DIM), F32)],
        compiler_params=_params("arbitrary", "arbitrary"),
        name="wkv7_chunked",
    )(r, k, v, kk, lw, a, r, k, v, kk, lw, a, rw_k_a.reshape(1, c))


NEG_INF = float("-inf")


def _first_max(vals, idx, size):
    m = jnp.max(vals, axis=0, keepdims=True)
    i = jnp.min(jnp.where(vals == m, idx, size), axis=0, keepdims=True)
    return m, i


def _route(scores, biased):
    e, tt = scores.shape
    per = e // N_GROUPS
    rowl = lax.broadcasted_iota(jnp.int32, (per, tt), 0)
    gs = []
    for g in range(N_GROUPS):
        blk = biased[g * per:(g + 1) * per]
        m1, i1 = _first_max(blk, rowl, per)
        m2 = jnp.max(jnp.where(rowl == i1, NEG_INF, blk), axis=0, keepdims=True)
        gs.append(m1 + m2)
    cur = jnp.concatenate(gs, axis=0)
    growl = lax.broadcasted_iota(jnp.int32, (N_GROUPS, tt), 0)
    gsel = jnp.zeros((N_GROUPS, tt), F32)
    for _ in range(TOPK_GROUPS):
        _, ig = _first_max(cur, growl, N_GROUPS)
        hit = growl == ig
        gsel = jnp.where(hit, 1.0, gsel)
        cur = jnp.where(hit, NEG_INF, cur)
    emask = jnp.concatenate([jnp.broadcast_to(gsel[g:g + 1], (per, tt)) for g in range(N_GROUPS)], axis=0)
    masked = jnp.where(emask > 0.5, biased, NEG_INF)
    row = lax.broadcasted_iota(jnp.int32, (e, tt), 0)
    ids, ws = [], []
    for _ in range(TOP_K):
        _, ie = _first_max(masked, row, e)
        hit = row == ie
        ids.append(ie)
        ws.append(jnp.sum(jnp.where(hit, scores, 0.0), axis=0, keepdims=True))
        masked = jnp.where(hit, NEG_INF, masked)
    w = jnp.concatenate(ws, axis=0)
    w = w / jnp.sum(w, axis=0, keepdims=True) * ROUTE_SCALE
    return jnp.concatenate(ids, axis=0), w


def _mixout_kernel(x_ref, mod_ref, yhy_ref, of_ref, ob_ref, g_ref, bonus_ref, lnw_ref, lnb_ref,
                   ones_ref, wout_ref, g2n_ref, rwt_ref, bias_ref,
                   x1_ref, h2_ref, eid_ref, wsel_ref):
    ones = ones_ref[...]
    s = of_ref[...] + ob_ref[...]
    mean = _dot_exact_rhs(s, ones) * (1.0 / HEAD)
    dlt = s - mean
    var = _dot_exact_rhs(dlt * dlt, ones) * (1.0 / HEAD)
    sn = dlt * lax.rsqrt(var + GN_EPS) * lnw_ref[...] + lnb_ref[...]
    yrw = (sn + bonus_ref[...]) * g_ref[...]
    ch = yhy_ref.shape[-1]
    mix = _dot(yhy_ref[...].astype(BF16), wout_ref[:ch, :]) + _dot(yrw.astype(BF16), wout_ref[ch:, :])
    x1 = x_ref[...] + mod_ref[2:3, :] * mix
    x1_ref[...] = x1
    ms = jnp.mean(x1 * x1, axis=-1, keepdims=True)
    h2 = x1 * lax.rsqrt(ms + NORM_EPS) * g2n_ref[...]
    h2 = h2 * (1.0 + mod_ref[4:5, :]) + mod_ref[3:4, :]
    h2_ref[...] = h2
    rh, rl = _split2(rwt_ref[...])
    hh, hl = _split2(h2)
    logits = _nt(rh, hh) + (_nt(rh, hl) + _nt(rl, hh))
    scores = jax.nn.sigmoid(logits)
    ids, w = _route(scores, scores + bias_ref[...])
    eid_ref[...] = ids
    wsel_ref[...] = w


def _mix_out(x, mod, yhy, o_f, o_b, g, bonus, ln_w, ln_b, w_out, norm2_g, router_w, router_bias, tt=256):
    bsz, seq, d = x.shape
    tt = min(tt, seq)
    nt = seq // tt
    n = bsz * seq
    c = D_RWKV
    e = router_w.shape[1]
    row = lambda a: a.reshape(1, -1)
    consts = [row(ln_w), row(ln_b), _head_ones(), w_out.astype(BF16), row(norm2_g), router_w.T,
              jnp.broadcast_to(router_bias.reshape(e, 1), (e, tt))]
    const = lambda a: pl.BlockSpec(a.shape, lambda b, i: (0,) * a.ndim)
    tile = lambda w: pl.BlockSpec((None, tt, w), lambda b, i: (b, i, 0))
    flat = lambda rows, dt: jax.ShapeDtypeStruct((rows, n), dt)
    return pl.pallas_call(
        _mixout_kernel,
        grid=(bsz, nt),
        in_specs=[tile(d), pl.BlockSpec((None,) + mod.shape[1:], lambda b, i: (b, 0, 0))]
        + [tile(c)] * 5 + [const(a) for a in consts],
        out_specs=[tile(d), pl.BlockSpec((tt, d), lambda b, i: (b * nt + i, 0)),
                   pl.BlockSpec((TOP_K, tt), lambda b, i: (0, b * nt + i)),
                   pl.BlockSpec((TOP_K, tt), lambda b, i: (0, b * nt + i))],
        out_shape=[jax.ShapeDtypeStruct((bsz, seq, d), F32), jax.ShapeDtypeStruct((n, d), F32),
                   flat(TOP_K, jnp.int32), flat(TOP_K, F32)],
        compiler_params=_params("arbitrary", "arbitrary"),
        name="mix_out_router",
    )(x, mod, yhy, o_f, o_b, g, bonus, *consts)


BLK = 256
BLK_SHIFT = 8


def _multi_hot(eid, e):
    row = lax.broadcasted_iota(jnp.int32, (e, eid.shape[1]), 0)
    mh = jnp.zeros((e, eid.shape[1]), F32)
    for kk in range(TOP_K):
        mh = mh + jnp.where(row == eid[kk:kk + 1, :], 1.0, 0.0)
    return row, mh


def _lookup(row, eid, table):
    return jnp.concatenate(
        [jnp.sum(jnp.where(row == eid[kk:kk + 1, :], table, 0.0), axis=0, keepdims=True)
         for kk in range(TOP_K)], axis=0)


def _rank_kernel(eid_ref, rank_ref, cnt_ref, *, e):
    @pl.when(pl.program_id(0) == 0)
    def _():
        cnt_ref[...] = jnp.zeros_like(cnt_ref)

    eid = eid_ref[...]
    tt = eid.shape[1]
    row, mh = _multi_hot(eid, e)
    mhb = mh.astype(BF16)
    si = lax.broadcasted_iota(jnp.int32, (tt, tt), 0)
    ti = lax.broadcasted_iota(jnp.int32, (tt, tt), 1)
    earlier = _dot(mhb, jnp.where(si < ti, 1.0, 0.0).astype(BF16))
    cnt = cnt_ref[...]
    full = earlier + jnp.concatenate([cnt] * (tt // LANES), axis=1)
    rank_ref[...] = _lookup(row, eid, full).astype(jnp.int32)
    cnt_ref[...] = cnt + _dot(mhb, jnp.ones((tt, LANES), BF16))


def _expert_ranks(eid, e, tt=512):
    n = eid.shape[1]
    tt = min(tt, n)
    return pl.pallas_call(
        functools.partial(_rank_kernel, e=e),
        grid=(n // tt,),
        in_specs=[pl.BlockSpec((TOP_K, tt), lambda i: (0, i))],
        out_specs=[pl.BlockSpec((TOP_K, tt), lambda i: (0, i)),
                   pl.BlockSpec((e, LANES), lambda i: (0, 0))],
        out_shape=[jax.ShapeDtypeStruct((TOP_K, n), jnp.int32), jax.ShapeDtypeStruct((e, LANES), F32)],
        compiler_params=_params("arbitrary"),
        name="expert_ranks",
    )(eid)


def _block_offsets(cnt):
    e = cnt.shape[0]
    nblk = ((cnt.astype(jnp.int32) + (BLK - 1)) >> BLK_SHIFT).astype(F32)
    ri = lax.broadcasted_iota(jnp.int32, (e, e), 0)
    ci = lax.broadcasted_iota(jnp.int32, (e, e), 1)
    tril = jnp.where(ci <= ri, 1.0, 0.0).astype(BF16)
    nh, nl = _split2(nblk)
    return nblk, _dot(tril, nh) + _dot(tril, nl)


def _dest_kernel(cnt_ref, eid_ref, rank_ref, dest_ref):
    nblk, end = _block_offsets(cnt_ref[...])
    off = (end - nblk) * float(BLK)
    eid = eid_ref[...]
    tt = eid.shape[1]
    row = lax.broadcasted_iota(jnp.int32, (off.shape[0], tt), 0)
    table = jnp.concatenate([off] * (tt // LANES), axis=1)
    dest_ref[...] = _lookup(row, eid, table).astype(jnp.int32) + rank_ref[...]


def _destinations(cnt, eid, rank, tt=512):
    n = eid.shape[1]
    tt = min(tt, n)
    blk = pl.BlockSpec((TOP_K, tt), lambda i: (0, i))
    return pl.pallas_call(
        _dest_kernel,
        grid=(n // tt,),
        in_specs=[pl.BlockSpec(cnt.shape, lambda i: (0, 0)), blk, blk],
        out_specs=blk,
        out_shape=jax.ShapeDtypeStruct((TOP_K, n), jnp.int32),
        compiler_params=_params("arbitrary"),
        name="expert_destinations",
    )(cnt, eid, rank)


def _meta_kernel(cnt_ref, meta_ref, emeta_ref, *, nbp):
    cnt = cnt_ref[...]
    e = cnt.shape[0]
    nblk, end = _block_offsets(cnt)
    rep = lambda a, w: jnp.concatenate([a] * (w // LANES), axis=1)
    b = lax.broadcasted_iota(jnp.int32, (e, nbp), 1).astype(F32)
    blk_e = jnp.minimum(jnp.sum(jnp.where(rep(end, nbp) <= b, 1.0, 0.0), axis=0, keepdims=True), float(e - 1))
    row = lax.broadcasted_iota(jnp.int32, (e, nbp), 0).astype(F32)
    mine = row == blk_e
    left = rep(cnt + (end - nblk) * float(BLK), nbp) - b * float(BLK)
    nvalid = jnp.clip(jnp.sum(jnp.where(mine, left, 0.0), axis=0, keepdims=True), 0.0, float(BLK))
    nused = jnp.max(rep(end, nbp), axis=0, keepdims=True)
    meta_ref[...] = jnp.concatenate([blk_e, nvalid, nused, jnp.zeros((5, nbp), F32)], axis=0).astype(jnp.int32)
    eye = lax.broadcasted_iota(jnp.int32, (e, e), 0) == lax.broadcasted_iota(jnp.int32, (e, e), 1)
    to_row = lambda a: jnp.sum(jnp.where(eye, rep(a, e), 0.0), axis=0, keepdims=True)
    emeta_ref[...] = jnp.concatenate([to_row(end), to_row(cnt), jnp.zeros((6, e), F32)], axis=0).astype(jnp.int32)


def _block_meta(cnt, nb):
    e = cnt.shape[0]
    nbp = -(-nb // LANES) * LANES
    return pl.pallas_call(
        functools.partial(_meta_kernel, nbp=nbp),
        out_shape=[jax.ShapeDtypeStruct((8, nbp), jnp.int32), jax.ShapeDtypeStruct((8, e), jnp.int32)],
        compiler_params=pltpu.CompilerParams(vmem_limit_bytes=VMEM_LIMIT),
        name="expert_block_meta",
    )(cnt)


def _row_copy(src_ref, s, dst_ref, t, sem):
    return pltpu.make_async_copy(src_ref.at[pl.ds(s, 1), :], dst_ref.at[pl.ds(t, 1), :], sem)


def _dispatch_kernel(eend_ref, ecnt_ref, dest_ref, h_ref, xs_ref, zero_ref, zsem, sem, *, e, tt):
    @pl.when(pl.program_id(0) == 0)
    def _():
        zero_ref[...] = jnp.zeros_like(zero_ref)

        def tail(ex):
            start = pl.multiple_of((eend_ref[ex] - 1) * BLK, BLK)
            return pltpu.make_async_copy(zero_ref, xs_ref.at[pl.ds(start, BLK), :], zsem)

        def issue(ex, carry):
            @pl.when(ecnt_ref[ex] > 0)
            def _():
                tail(ex).start()
            return carry

        def drain(ex, carry):
            @pl.when(ecnt_ref[ex] > 0)
            def _():
                tail(ex).wait()
            return carry

        lax.fori_loop(0, e, issue, 0)
        lax.fori_loop(0, e, drain, 0)

    def issue_rows(t, carry):
        for kk in range(TOP_K):
            _row_copy(h_ref, t, xs_ref, dest_ref[kk, t], sem).start()
        return carry

    def drain_rows(t, carry):
        for kk in range(TOP_K):
            _row_copy(h_ref, t, xs_ref, dest_ref[kk, t], sem).wait()
        return carry

    lax.fori_loop(0, tt, issue_rows, 0)
    lax.fori_loop(0, tt, drain_rows, 0)


def _dispatch(eend, ecnt, dest, h2, nb, tt=256):
    n, d = h2.shape
    e = eend.shape[0]
    tt = min(tt, n)
    return pl.pallas_call(
        functools.partial(_dispatch_kernel, e=e, tt=tt),
        grid_spec=pltpu.PrefetchScalarGridSpec(
            num_scalar_prefetch=2,
            grid=(n // tt,),
            in_specs=[pl.BlockSpec((TOP_K, tt), lambda i, *_: (0, i), memory_space=pltpu.SMEM),
                      pl.BlockSpec((tt, d), lambda i, *_: (i, 0))],
            out_specs=pl.BlockSpec(memory_space=pl.ANY),
            scratch_shapes=[pltpu.VMEM((BLK, d), F32), pltpu.SemaphoreType.DMA(()),
                            pltpu.SemaphoreType.DMA(())],
        ),
        out_shape=jax.ShapeDtypeStruct((nb * BLK, d), F32),
        compiler_params=_params("arbitrary"),
        name="moe_dispatch",
    )(eend, ecnt, dest, h2)


def _experts_kernel(be_ref, nv_ref, nu_ref, x_ref, wg_ref, wu_ref, wd_ref, o_ref, wgb, wub, wdb):
    b = pl.program_id(0)
    used = b < nu_ref[0]

    @pl.when(used)
    def _():
        prev = be_ref[jnp.maximum(b - 1, 0)]

        @pl.when(jnp.logical_or(b == 0, be_ref[b] != prev))
        def _():
            wgb[...] = wg_ref[...].astype(BF16)
            wub[...] = wu_ref[...].astype(BF16)
            wdb[...] = wd_ref[...].astype(BF16)

        row = lax.broadcasted_iota(jnp.int32, (x_ref.shape[0], 1), 0)
        x = jnp.where(row < nv_ref[b], x_ref[...], 0.0).astype(BF16)
        act = _silu(_dot(x, wgb[...])) * _dot(x, wub[...])
        o_ref[...] = _dot(act.astype(BF16), wdb[...])

    @pl.when(jnp.logical_not(used))
    def _():
        o_ref[...] = jnp.zeros_like(o_ref)


def _experts(blk_e, nvalid, nused, xs, wg, wu, wd):
    p, d = xs.shape
    nb = p // BLK
    de = wg.shape[2]
    last = lambda b, be, nv, nu: jnp.minimum(b, nu[0] - 1)
    return pl.pallas_call(
        _experts_kernel,
        grid_spec=pltpu.PrefetchScalarGridSpec(
            num_scalar_prefetch=3,
            grid=(nb,),
            in_specs=[pl.BlockSpec((BLK, d), lambda b, be, nv, nu: (last(b, be, nv, nu), 0)),
                      pl.BlockSpec((None, d, de), lambda b, be, nv, nu: (be[last(b, be, nv, nu)], 0, 0)),
                      pl.BlockSpec((None, d, de), lambda b, be, nv, nu: (be[last(b, be, nv, nu)], 0, 0)),
                      pl.BlockSpec((None, de, d), lambda b, be, nv, nu: (be[last(b, be, nv, nu)], 0, 0))],
            out_specs=pl.BlockSpec((BLK, d), lambda b, be, nv, nu: (b, 0)),
            scratch_shapes=[pltpu.VMEM((d, de), BF16), pltpu.VMEM((d, de), BF16), pltpu.VMEM((de, d), BF16)],
        ),
        out_shape=jax.ShapeDtypeStruct((p, d), F32),
        compiler_params=_params("arbitrary"),
        name="moe_experts",
    )(blk_e, nvalid, nused, xs, wg, wu, wd)


def _combine_kernel(dest_ref, w_ref, x1_ref, h2_ref, mod_ref, ys_ref, sg_ref, su_ref, sd_ref, gf_ref,
                    sel_ref, o_ref, buf, sem, *, tt):
    def issue(t, carry):
        for kk in range(TOP_K):
            _row_copy(ys_ref, dest_ref[kk, t], buf.at[kk], t, sem).start()
        return carry

    def drain(t, carry):
        for kk in range(TOP_K):
            _row_copy(ys_ref, dest_ref[kk, t], buf.at[kk], t, sem).wait()
        return carry

    lax.fori_loop(0, tt, issue, 0)
    hb = h2_ref[...].astype(BF16)
    act = _silu(_dot(hb, sg_ref[...])) * _dot(hb, su_ref[...])
    ffn = _dot(act.astype(BF16), sd_ref[...])
    lax.fori_loop(0, tt, drain, 0)
    wh, wl = _split2(w_ref[...])
    for kk in range(TOP_K):
        sel = sel_ref[kk]
        ffn = ffn + buf[kk] * (_tn(wh, sel) + _tn(wl, sel))
    xo = x1_ref[...] + mod_ref[5:6, :] * ffn
    ms = jnp.mean(xo * xo, axis=-1, keepdims=True)
    o_ref[...] = xo * lax.rsqrt(ms + NORM_EPS) * gf_ref[...]


def _combine(dest, wsel, x1, h2, mod, ys, sh_wg, sh_wu, sh_wd, normf_g, tt=128):
    bsz, seq, d = x1.shape
    n = bsz * seq
    tt = min(tt, seq)
    per = seq // tt
    sel = jnp.asarray(np.broadcast_to(np.eye(TOP_K)[:, :, None], (TOP_K, TOP_K, d)), BF16)
    consts = [sh_wg.astype(BF16), sh_wu.astype(BF16), sh_wd.astype(BF16), normf_g.reshape(1, d), sel]
    const = lambda a: pl.BlockSpec(a.shape, lambda i: (0,) * a.ndim)
    rows = pl.BlockSpec((tt, d), lambda i: (i, 0))
    return pl.pallas_call(
        functools.partial(_combine_kernel, tt=tt),
        grid=(n // tt,),
        in_specs=[pl.BlockSpec((TOP_K, tt), lambda i: (0, i), memory_space=pltpu.SMEM),
                  pl.BlockSpec((TOP_K, tt), lambda i: (0, i)),
                  rows, rows,
                  pl.BlockSpec((None,) + mod.shape[1:], lambda i: (i // per, 0, 0)),
                  pl.BlockSpec(memory_space=pl.ANY)] + [const(a) for a in consts],
        out_specs=rows,
        out_shape=jax.ShapeDtypeStruct((n, d), F32),
        scratch_shapes=[pltpu.VMEM((TOP_K, tt, d), F32), pltpu.SemaphoreType.DMA(())],
        compiler_params=_params("arbitrary"),
        name="moe_combine",
    )(dest, wsel, x1.reshape(n, d), h2, mod, ys, *consts)


def _moe(x1, h2, mod, eid, wsel, exp_wg, exp_wu, exp_wd, sh_wg, sh_wu, sh_wd, normf_g):
    n = h2.shape[0]
    e = exp_wg.shape[0]
    nb = (n * TOP_K + e * (BLK - 1)) // BLK
    rank, cnt = _expert_ranks(eid, e)
    dest = _destinations(cnt, eid, rank)
    meta, emeta = _block_meta(cnt, nb)
    xs = _dispatch(emeta[0], emeta[1], dest, h2, nb)
    ys = _experts(meta[0, :nb], meta[1, :nb], meta[2, :1], xs, exp_wg, exp_wu, exp_wd)
    return _combine(dest, wsel, x1, h2, mod, ys, sh_wg, sh_wu, sh_wd, normf_g)


def kernel(x, c, norm1_g, norm2_g, normf_g, w_ada, b_ada, w_in, w_out, hy_conv_w, hy_conv_b, hy_pos_w1, hy_pos_b1, hy_pos_w2, hy_pos_b2, hy_pos_w3, hy_sin_freq, hy_skip, rw_mu, rw_w0, rw_w_up, rw_a0, rw_a_up, rw_g_up, rw_k_k, rw_k_a, rw_r_k, rw_ln_w, rw_ln_b, router_w, router_bias, exp_w_gate, exp_w_up, exp_w_down, sh_w_gate, sh_w_up, sh_w_down):
    bsz, seq, d = x.shape
    depth = w_ada.shape[0]
    assert depth == 1, "the final norm is fused into the last kernel of a single layer"
    for l in range(depth):
        mod = _modulation(c, w_ada[l], b_ada[l]).reshape(bsz, -1, d)
        uhy, r, k, v, kk, lw, a, g, bonus = _projection(
            x, mod, norm1_g[l], w_in[l], hy_conv_w[l], hy_conv_b[l], rw_mu[l], rw_w0[l], rw_w_up[l],
            rw_a0[l], rw_a_up[l], rw_g_up[l], rw_k_k[l], rw_k_a[l], rw_r_k[l])
        k2, ss = _hyena_filters(seq, hy_pos_w1[l], hy_pos_b1[l], hy_pos_w2[l], hy_pos_b2[l],
                                hy_pos_w3[l], hy_sin_freq[l])
        khat = _filter_spectrum(k2, ss, seq)
        z, z_col = uhy, 0
        for order in range(HYENA_ORDER):
            z = _long_conv_gate(z, z_col, uhy, (order + 1) * D_HYENA, khat, hy_skip[l], order)
            z_col = 0
        o_f, o_b = _wkv(r, k, v, kk, lw, a, rw_k_a[l])
        x1, h2, eid, wsel = _mix_out(x, mod, z, o_f, o_b, g, bonus, rw_ln_w[l], rw_ln_b[l], w_out[l],
                                     norm2_g[l], router_w[l], router_bias[l])
        x = _moe(x1, h2, mod, eid, wsel, exp_w_gate[l], exp_w_up[l], exp_w_down[l],
                 sh_w_gate[l], sh_w_up[l], sh_w_down[l], normf_g)
        x = x.reshape(bsz, seq, d)
    return x
```

```python
import functools
import math

import jax
import jax.numpy as jnp
import numpy as np
from jax import lax
from jax.experimental import pallas as pl
from jax.experimental.pallas import tpu as pltpu

F32 = jnp.float32
BF16 = jnp.bfloat16

LANES = 128
MXU_DIM = 256
VMEM_LIMIT = 56 * 1024 * 1024

D_HYENA = 512
D_RWKV = 512
HEAD = 64
N_HEADS = D_RWKV // HEAD
HYENA_ORDER = 2
FILTER_BANDS = 16
DECAY_TARGET = 1e-2
FAST_DECAY_PCT = 0.3
SLOW_DECAY_PCT = 1.5
FILTER_NORM_EPS = 1e-6
DECAY_LORA = 32
ICLR_LORA = 32
GATE_LORA = 96
GN_EPS = 64e-5
NORM_EPS = 1e-6
N_EXPERTS = 256
TOP_K = 8
N_GROUPS = 8
TOPK_GROUPS = 4
ROUTE_SCALE = 2.5
D_EXPERT = 256


def _params(*sem):
    return pltpu.CompilerParams(dimension_semantics=sem, vmem_limit_bytes=VMEM_LIMIT)


def _split2(a):
    hi = a.astype(BF16)
    lo = (a - hi.astype(F32)).astype(BF16)
    return hi, lo


def _dot(a, b):
    return jnp.dot(a, b, preferred_element_type=F32)


def _dot3(a, b):
    ah, al = _split2(a)
    bh, bl = _split2(b)
    return _dot(ah, bh) + (_dot(ah, bl) + _dot(al, bh))


def _dot_exact_rhs(a, b_bf16):
    ah, al = _split2(a)
    return _dot(ah, b_bf16) + _dot(al, b_bf16)


def _silu(x):
    return x * jax.nn.sigmoid(x)


def _mod_kernel(c_ref, w_ref, b_ref, o_ref):
    o_ref[...] = _dot3(_silu(c_ref[...]), w_ref[...]) + b_ref[...]


def _modulation(c, w_ada, b_ada):
    bsz, d = c.shape
    n = w_ada.shape[1]
    blk = 1024
    return pl.pallas_call(
        _mod_kernel,
        grid=(n // blk,),
        in_specs=[
            pl.BlockSpec((bsz, d), lambda j: (0, 0)),
            pl.BlockSpec((d, blk), lambda j: (0, j)),
            pl.BlockSpec((1, blk), lambda j: (0, j)),
        ],
        out_specs=pl.BlockSpec((bsz, blk), lambda j: (0, j)),
        out_shape=jax.ShapeDtypeStruct((bsz, n), F32),
        compiler_params=_params("arbitrary"),
        name="adaln_mod",
    )(c, w_ada, b_ada.reshape(1, n))


def _filter_kernel(band_ref, w1_ref, b1_ref, w2_ref, b2_ref, w3_ref, freq_ref, delta_ref,
                   k_ref, ss_ref, *, seq, rows):
    half = pl.program_id(0)
    i = pl.program_id(1)
    r = lax.broadcasted_iota(jnp.int32, (rows, LANES), 0) + i * rows
    pos = jnp.where(half == 0, r, seq - r).astype(F32)
    tt = pos / float(max(seq - 1, 1))
    lane = lax.broadcasted_iota(jnp.int32, (rows, LANES), 1)
    ang = pos * band_ref[...]
    feats = jnp.where(lane == 0, tt,
                      jnp.where(lane <= FILTER_BANDS, jnp.cos(ang),
                                jnp.where(lane <= 2 * FILTER_BANDS, -jnp.sin(ang), 0.0)))
    freq = freq_ref[...]
    hdn = jnp.sin(freq * (_dot3(feats, w1_ref[...]) + b1_ref[...]))
    for j in range(w2_ref.shape[0]):
        hdn = jnp.sin(freq * (_dot3(hdn, w2_ref[j]) + b2_ref[j]))
    filt = _dot3(hdn, w3_ref[...])
    filt = filt * jnp.exp(-tt[:, :1] * delta_ref[...])
    valid = jnp.logical_or(half == 0, r[:, :1] > 0)
    filt = jnp.where(valid, filt, 0.0)
    k_ref[...] = filt

    @pl.when(jnp.logical_and(half == 0, i == 0))
    def _():
        ss_ref[...] = jnp.zeros_like(ss_ref)

    ss_ref[...] += jnp.broadcast_to(jnp.sum(filt * filt, axis=0, keepdims=True), ss_ref.shape)


def _hyena_filters(seq, pw1, pb1, pw2, pb2, pw3, freq):
    width = pw1.shape[1]
    ncol = HYENA_ORDER * D_HYENA
    rows = min(seq, 512)
    bands = np.zeros((1, LANES), np.float64)
    lin = np.linspace(1e-4, FILTER_BANDS - 1, FILTER_BANDS)
    bands[0, 1:1 + FILTER_BANDS] = lin
    bands[0, 1 + FILTER_BANDS:1 + 2 * FILTER_BANDS] = lin
    bands = jnp.asarray(bands * (2.0 * math.pi / seq), F32)
    deltas = np.abs(np.linspace(math.log(DECAY_TARGET) / SLOW_DECAY_PCT,
                                math.log(DECAY_TARGET) / FAST_DECAY_PCT, D_HYENA))
    deltas = jnp.asarray(np.tile(deltas, HYENA_ORDER)[None], F32)
    w1 = jnp.zeros((LANES, width), F32).at[:pw1.shape[0]].set(pw1)
    w3 = pw3.reshape(width, HYENA_ORDER, 2, D_HYENA).transpose(2, 0, 1, 3).reshape(2, width, ncol)
    nt = seq // rows
    full = lambda *shape: pl.BlockSpec(shape, lambda h, i: (0,) * len(shape))
    return pl.pallas_call(
        functools.partial(_filter_kernel, seq=seq, rows=rows),
        grid=(2, nt),
        in_specs=[
            full(1, LANES), full(LANES, width), full(1, width),
            full(pw2.shape[0], width, width), full(pw2.shape[0], 1, width),
            pl.BlockSpec((None, width, ncol), lambda h, i: (h, 0, 0)),
            full(1, width), full(1, ncol),
        ],
        out_specs=[
            pl.BlockSpec((rows, ncol), lambda h, i: (h * nt + i, 0)),
            pl.BlockSpec((8, ncol), lambda h, i: (0, 0)),
        ],
        out_shape=[jax.ShapeDtypeStruct((2 * seq, ncol), F32),
                   jax.ShapeDtypeStruct((8, ncol), F32)],
        compiler_params=_params("arbitrary", "arbitrary"),
        name="hyena_filters",
    )(bands, w1, pb1.reshape(1, width), pw2, pb2.reshape(pw2.shape[0], 1, width), w3,
      freq.reshape(1, width), deltas)


N1 = LANES
UNROLL = 8


def _dft_tables(seq):
    m = 2 * seq
    n2 = m // N1
    n2h = n2 // 2
    n1 = np.arange(N1)[:, None, None]
    f2 = np.arange(n2)[None, :, None]
    k2 = np.arange(n2)[None, None, :]
    th = 2.0 * np.pi * (n1 * f2 / m + (k2 * f2 % n2) / n2)
    fwd_a = np.concatenate([np.cos(th), -np.sin(th)], axis=1)
    tht = np.transpose(th, (0, 2, 1))
    inv_a = np.concatenate([np.cos(tht), -np.sin(tht)], axis=2)[:, :n2h] / m
    a = np.arange(N1)
    ph = 2.0 * np.pi * np.outer(a, a) / N1
    c, s = np.cos(ph), np.sin(ph)
    fwd_b = np.block([[c, s], [-s, c]])
    inv_b = np.block([[c, -s], [s, c]])
    cast = lambda t: jnp.asarray(t, BF16)
    return cast(fwd_a), cast(fwd_a[:, :, :n2h]), cast(inv_a), cast(fwd_b), cast(inv_b), n2, n2h


def _stage_a_fwd(x_ref, wa_ref, y_ref, n2, scale=None):
    def body(i, carry):
        trips = [i * UNROLL + j for j in range(UNROLL)]
        xs = [x_ref[pl.ds(n1, wa_ref.shape[2], stride=N1), :] for n1 in trips]
        if scale is not None:
            xs = [x * scale for x in xs]
        prods = [_dot(wa_ref[n1], x.astype(BF16)) for n1, x in zip(trips, xs)]
        for n1, a in zip(trips, prods):
            y_ref[pl.ds(n1, n2, stride=2 * N1), :] = a[:n2]
            y_ref[pl.ds(N1 + n1, n2, stride=2 * N1), :] = a[n2:]
        return carry
    lax.fori_loop(0, N1 // UNROLL, body, 0)


def _filter_fft_kernel(k_ref, ss_ref, wa_ref, fb_ref, o_ref, y_ref, *, n2):
    scale = lax.rsqrt(ss_ref[0:1, :] + FILTER_NORM_EPS)
    _stage_a_fwd(k_ref, wa_ref, y_ref, n2, scale=scale)

    unr = min(UNROLL, n2)

    def body(i, carry):
        trips = [i * unr + j for j in range(unr)]
        ys = [y_ref[pl.ds(pl.multiple_of(f2 * 2 * N1, 2 * N1), 2 * N1), :].astype(BF16) for f2 in trips]
        for f2, y in zip(trips, ys):
            o_ref[f2] = _dot(fb_ref[...], y)
        return carry
    lax.fori_loop(0, n2 // unr, body, 0)


def _filter_spectrum(k2, ss, seq):
    fwd_a, _, _, fwd_b, _, n2, _ = _dft_tables(seq)
    ncol = k2.shape[1]
    nblk = ncol // LANES
    return pl.pallas_call(
        functools.partial(_filter_fft_kernel, n2=n2),
        grid=(nblk,),
        in_specs=[
            pl.BlockSpec((2 * seq, LANES), lambda c: (0, c)),
            pl.BlockSpec((8, LANES), lambda c: (0, c)),
            pl.BlockSpec(fwd_a.shape, lambda c: (0, 0, 0)),
            pl.BlockSpec(fwd_b.shape, lambda c: (0, 0)),
        ],
        out_specs=pl.BlockSpec((None, n2, 2 * N1, LANES), lambda c: (c, 0, 0, 0)),
        out_shape=jax.ShapeDtypeStruct((nblk, n2, 2 * N1, LANES), F32),
        scratch_shapes=[pltpu.VMEM((n2 * 2 * N1, LANES), F32)],
        compiler_params=_params("arbitrary"),
        name="hyena_filter_fft",
    )(k2, ss, fwd_a, fwd_b)


def _conv_kernel(u_ref, g_ref, kh_ref, skip_ref, wa_ref, va_ref, fb_ref, ib_ref, o_ref, y_ref,
                 *, n2, n2h):
    _stage_a_fwd(u_ref, wa_ref, y_ref, n2)

    unr = min(UNROLL, n2)

    def mid(i, carry):
        trips = [i * unr + j for j in range(unr)]
        offs = [pl.multiple_of(f2 * 2 * N1, 2 * N1) for f2 in trips]
        zs = [_dot(fb_ref[...], y_ref[pl.ds(off, 2 * N1), :].astype(BF16)) for off in offs]
        ps = []
        for f2, z in zip(trips, zs):
            zr, zi = z[:N1], z[N1:]
            kh = kh_ref[f2]
            kr, ki = kh[:N1], kh[N1:]
            ps.append(jnp.concatenate([zr * kr - zi * ki, zr * ki + zi * kr], axis=0).astype(BF16))
        gs = [_dot(ib_ref[...], p) for p in ps]
        for off, g in zip(offs, gs):
            y_ref[pl.ds(off, 2 * N1), :] = g
        return carry
    lax.fori_loop(0, n2 // unr, mid, 0)

    skip = skip_ref[...]

    def last(i, carry):
        trips = [i * UNROLL + j for j in range(UNROLL)]
        gs = [jnp.concatenate([y_ref[pl.ds(n1, n2, stride=2 * N1), :],
                               y_ref[pl.ds(N1 + n1, n2, stride=2 * N1), :]], axis=0).astype(BF16)
              for n1 in trips]
        convs = [_dot(va_ref[n1], g) for n1, g in zip(trips, gs)]
        for n1, conv in zip(trips, convs):
            rows = pl.ds(n1, n2h, stride=N1)
            u = u_ref[rows, :]
            o_ref[rows, :] = g_ref[rows, :] * (conv + u * skip)
        return carry
    lax.fori_loop(0, N1 // UNROLL, last, 0)


def _long_conv_gate(u, u_col, gate, gate_col, khat, skip, order):
    bsz, seq, _ = u.shape
    ch = D_HYENA
    _, fwd_a, inv_a, fwd_b, inv_b, n2, n2h = _dft_tables(seq)
    nblk = ch // LANES
    const = lambda a: pl.BlockSpec(a.shape, lambda c, b: (0,) * a.ndim)
    at = lambda col: pl.BlockSpec((None, seq, LANES), lambda c, b: (b, 0, col // LANES + c))
    data = at(0)
    return pl.pallas_call(
        functools.partial(_conv_kernel, n2=n2, n2h=n2h),
        grid=(nblk, bsz),
        in_specs=[
            at(u_col), at(gate_col),
            pl.BlockSpec((None, n2, 2 * N1, LANES), lambda c, b: (order * nblk + c, 0, 0, 0)),
            pl.BlockSpec((1, LANES), lambda c, b: (0, c)),
            const(fwd_a), const(inv_a), const(fwd_b), const(inv_b),
        ],
        out_specs=data,
        out_shape=jax.ShapeDtypeStruct((bsz, seq, ch), F32),
        scratch_shapes=[pltpu.VMEM((n2 * 2 * N1, LANES), F32)],
        compiler_params=_params("arbitrary", "arbitrary"),
        name=f"hyena_conv{order}",
    )(u, gate, khat, skip[order].reshape(1, ch), fwd_a, inv_a, fwd_b, inv_b)


HALO = 8


def _shift_rows(p, k):
    return pltpu.roll(p, k % p.shape[0], axis=0)


def _proj_kernel(xp_ref, x_ref, xn_ref, mod_ref, g1_ref, why_ref, wrkv_ref, wlora_ref,
                 cw_ref, cb_ref, murkv_ref, mulora_ref, w0_ref, a0_ref, wwa_ref, gup_ref,
                 kk_ref, ka_ref, rk_ref, ones_ref,
                 uhy_ref, r_ref, k_ref, v_ref, kkn_ref, lw_ref, a_ref, g_ref, bonus_ref,
                 *, tt, nt):
    i = pl.program_id(1)
    xe = jnp.concatenate([xp_ref[...], x_ref[...], xn_ref[...]], axis=0)
    ms = jnp.mean(xe * xe, axis=-1, keepdims=True)
    h = xe * lax.rsqrt(ms + NORM_EPS) * g1_ref[...]
    h = h * (1.0 + mod_ref[1:2, :]) + mod_ref[0:1, :]
    row = lax.broadcasted_iota(jnp.int32, (tt + 2 * HALO, 1), 0)
    inside = jnp.logical_and(jnp.logical_or(row >= HALO, i > 0),
                             jnp.logical_or(row < tt + HALO, i < nt - 1))
    hb = jnp.where(inside, h, 0.0).astype(BF16)
    mid = slice(HALO, tt + HALO)

    p = _dot(hb, why_ref[...])
    u = (_shift_rows(p, 1) * cw_ref[0:1, :] + p * cw_ref[1:2, :]
         + _shift_rows(p, -1) * cw_ref[2:3, :] + cb_ref[...])
    uhy_ref[...] = u[mid]

    p = _dot(hb, wrkv_ref[...])
    p = p + murkv_ref[...] * (0.5 * (_shift_rows(p, 1) + _shift_rows(p, -1)) - p)
    p = p[mid]
    c = D_RWKV
    r, k, v = p[:, :c], p[:, c:2 * c], p[:, 2 * c:]
    r_ref[...] = r
    k_ref[...] = k
    v_ref[...] = v

    q = _dot(hb, wlora_ref[...])
    q = q + mulora_ref[...] * (0.5 * (_shift_rows(q, 1) + _shift_rows(q, -1)) - q)
    q = q[mid]
    wa = q[:, :LANES]
    lane = lax.broadcasted_iota(jnp.int32, wa.shape, 1)
    wa = jnp.where(lane < 2 * DECAY_LORA, jnp.tanh(wa), wa)
    up = _dot3(wa, wwa_ref[...])
    z = -(w0_ref[...] + up[:, :2 * c])
    softplus = jnp.maximum(z, 0.0) + jnp.log1p(jnp.exp(-jnp.abs(z)))
    lw_ref[...] = -jnp.exp(-softplus - 0.5)
    a = jax.nn.sigmoid(a0_ref[...] + up[:, 2 * c:])
    a_ref[...] = a
    g_ref[...] = _dot3(jax.nn.sigmoid(q[:, LANES:]), gup_ref[...])

    ones = ones_ref[...]
    kk = k * kk_ref[...]
    nrm = jnp.sqrt(_dot_exact_rhs(kk * kk, ones))
    kkn_ref[...] = kk / jnp.maximum(nrm, 1e-12)
    ka = ka_ref[...]
    ksum = k * (2.0 + (a[:, :c] + a[:, c:] - 2.0) * ka)
    bonus_ref[...] = _dot_exact_rhs(r * ksum * rk_ref[...], ones) * v


def _head_ones():
    hid = np.arange(D_RWKV) // HEAD
    return jnp.asarray(hid[:, None] == hid[None, :], BF16)


def _projection(x, mod, norm1_g, w_in, hy_conv_w, hy_conv_b, rw_mu, rw_w0, rw_w_up, rw_a0,
                rw_a_up, rw_g_up, rw_k_k, rw_k_a, rw_r_k, tt=256):
    bsz, seq, d = x.shape
    tt = min(tt, seq)
    nt = seq // tt
    c = D_RWKV
    hy = (HYENA_ORDER + 1) * D_HYENA
    nlora = 2 * LANES
    w_hy = w_in[:, :hy].astype(BF16)
    w_rkv = w_in[:, hy:hy + 3 * c].astype(BF16)
    w_lora = jnp.zeros((d, nlora), F32).at[:, :w_in.shape[1] - hy - 3 * c].set(w_in[:, hy + 3 * c:]).astype(BF16)
    mu_rkv = rw_mu[:3 * c].reshape(1, 3 * c)
    mu_lora = jnp.zeros((1, nlora), F32).at[0, :rw_mu.shape[0] - 3 * c].set(rw_mu[3 * c:])
    wwa = jnp.zeros((LANES, 4 * c), F32)
    for dd in range(2):
        wwa = wwa.at[dd * DECAY_LORA:(dd + 1) * DECAY_LORA, dd * c:(dd + 1) * c].set(rw_w_up[dd])
        wwa = wwa.at[2 * DECAY_LORA + dd * ICLR_LORA:2 * DECAY_LORA + (dd + 1) * ICLR_LORA,
                     2 * c + dd * c:2 * c + (dd + 1) * c].set(rw_a_up[dd])
    gup = jnp.zeros((LANES, c), F32).at[:GATE_LORA].set(rw_g_up)
    row = lambda a: a.reshape(1, -1)

    nb8 = seq // HALO
    tb = tt // HALO
    const = lambda a: pl.BlockSpec(a.shape, lambda b, i: (0,) * a.ndim)
    tile = lambda w: pl.BlockSpec((None, tt, w), lambda b, i: (b, i, 0))
    ins = [
        (x, pl.BlockSpec((None, HALO, d), lambda b, i: (b, jnp.maximum(i * tb - 1, 0), 0))),
        (x, pl.BlockSpec((None, tt, d), lambda b, i: (b, i, 0))),
        (x, pl.BlockSpec((None, HALO, d), lambda b, i: (b, jnp.minimum((i + 1) * tb, nb8 - 1), 0))),
        (mod, pl.BlockSpec((None,) + mod.shape[1:], lambda b, i: (b, 0, 0))),
    ]
    consts = [row(norm1_g), w_hy, w_rkv, w_lora, hy_conv_w, row(hy_conv_b), mu_rkv, mu_lora,
              row(rw_w0), row(rw_a0), wwa, gup, row(rw_k_k), row(rw_k_a), row(rw_r_k), _head_ones()]
    ins += [(a, const(a)) for a in consts]
    widths = [hy, c, c, c, c, 2 * c, 2 * c, c, c]
    return pl.pallas_call(
        functools.partial(_proj_kernel, tt=tt, nt=nt),
        grid=(bsz, nt),
        in_specs=[s for _, s in ins],
        out_specs=[tile(w) for w in widths],
        out_shape=[jax.ShapeDtypeStruct((bsz, seq, w), F32) for w in widths],
        compiler_params=_params("arbitrary", "arbitrary"),
        name="input_projection",
    )(*[a for a, _ in ins])


CHUNK = HEAD
GROUP = MXU_DIM // HEAD


def _nt(a, b):
    return lax.dot_general(a, b, (((1,), (1,)), ((), ())), preferred_element_type=F32)


def _tn(a, b):
    return lax.dot_general(a, b, (((0,), (0,)), ((), ())), preferred_element_type=F32)


def _wkv_direction(r, k, v, kk, lw, a, ka, s_ref, reverse):
    c = CHUNK
    ti = lax.broadcasted_iota(jnp.int32, (c, c), 0)
    si = lax.broadcasted_iota(jnp.int32, (c, c), 1)
    tri = (si >= ti) if reverse else (si <= ti)
    cum = _dot_exact_rhs_lhs(jnp.where(tri, 1.0, 0.0).astype(BF16), lw)
    tot = jnp.sum(lw, axis=0, keepdims=True)
    w_incl = jnp.exp(cum)
    w_prev = jnp.exp(cum - lw)
    w_inv = jnp.exp(-cum)
    w_end = jnp.exp(tot - cum)
    w_tot = jnp.exp(tot)
    kd = k * (1.0 + (a - 1.0) * ka)
    b = kk * a
    a_w = -kk * w_prev
    r_w = r * w_incl
    b_w = b * w_inv
    k_w = kd * w_inv
    b_e = b * w_end
    k_e = kd * w_end

    m = MXU_DIM
    ri = lax.broadcasted_iota(jnp.int32, (m, m), 0)
    ci = lax.broadcasted_iota(jnp.int32, (m, m), 1)
    head_mask = (ri // HEAD) == (ci // HEAD)
    tl = lax.broadcasted_iota(jnp.int32, (c, m), 0)
    sl = lax.broadcasted_iota(jnp.int32, (c, m), 1) % c
    strict = (sl > tl) if reverse else (sl < tl)
    incl = (sl >= tl) if reverse else (sl <= tl)
    eye = jnp.where(sl == tl, 1.0, 0.0)
    both = lambda top, bot: jnp.concatenate([top, bot], axis=0)

    def stack(xg):
        xb = xg.astype(BF16)
        return jnp.where(head_mask, jnp.concatenate([xb] * GROUP, axis=0), jnp.zeros((), BF16))

    streams = []
    for g in range(D_RWKV // m):
        sl_g = slice(g * m, (g + 1) * m)
        streams.append(dict(
            ar=both(a_w[:, sl_g], r_w[:, sl_g]).astype(BF16),
            b_st=stack(b_w[:, sl_g]), k_st=stack(k_w[:, sl_g]), v_st=stack(v[:, sl_g]),
            v=v[:, sl_g], bk=both(b_e[:, sl_g], k_e[:, sl_g]).astype(BF16),
            w_tot=w_tot[:, sl_g], s_ref=s_ref.at[g],
            strict=strict, incl=incl, eye=eye, head_mask=head_mask, stack=stack))
    return streams


def _wkv_streams_step(streams):
    c = CHUNK
    both = lambda top, bot: jnp.concatenate([top, bot], axis=0)
    for st in streams:
        st["s"] = st["s_ref"][...]
        st["xb"] = _nt(st["ar"], st["b_st"])
        st["xk"] = _nt(st["ar"], st["k_st"])
        st["xs"] = _nt(st["ar"], st["s"].astype(BF16))
    for st in streams:
        m_ak = jnp.where(st["strict"], st["xk"][:c], 0.0)
        st["rhs"] = st["xs"][:c] + _dot(m_ak.astype(BF16), st["v_st"])
        st["pw"] = jnp.where(st["strict"], st["xb"][:c], 0.0)
        st["t"] = st["eye"] + st["pw"]
        st["p_st"] = st["stack"](st["pw"])
    for _ in range(int(math.log2(c)) - 1):
        for st in streams:
            st["pw"] = _dot(st["pw"].astype(BF16), st["p_st"])
            st["p_st"] = st["stack"](st["pw"])
        for st in streams:
            st["t"] = st["t"] + _dot(st["t"].astype(BF16), st["p_st"])
    for st in streams:
        st["u"] = _dot(st["t"].astype(BF16), st["stack"](st["rhs"]))
    outs = []
    for st in streams:
        m_rb = jnp.where(st["incl"], st["xb"][c:], 0.0)
        m_rk = jnp.where(st["incl"], st["xk"][c:], 0.0)
        outs.append(st["xs"][c:] + _dot(m_rb.astype(BF16), st["stack"](st["u"]))
                    + _dot(m_rk.astype(BF16), st["v_st"]))
        uv = both(st["u"], st["v"]).astype(BF16)
        st["s_ref"][...] = st["s"] * st["w_tot"] + jnp.where(st["head_mask"], _tn(uv, st["bk"]), 0.0)
    return outs


def _dot_exact_rhs_lhs(tri_bf16, x):
    xh, xl = _split2(x)
    return _dot(tri_bf16, xh) + _dot(tri_bf16, xl)


def _wkv_kernel(rf, kf, vf, kkf, lwf, af, rb, kb, vb, kkb, lwb, ab, ka_ref, of_ref, ob_ref, s_ref):
    @pl.when(pl.program_id(1) == 0)
    def _():
        s_ref[...] = jnp.zeros_like(s_ref)

    ka = ka_ref[...]
    fwd = _wkv_direction(rf[...], kf[...], vf[...], kkf[...], lwf[...], af[...], ka, s_ref.at[0], False)
    bwd = _wkv_direction(rb[...], kb[...], vb[...], kkb[...], lwb[...], ab[...], ka, s_ref.at[1], True)
    outs = _wkv_streams_step(fwd + bwd)
    of_ref[...] = jnp.concatenate(outs[:len(fwd)], axis=1)
    ob_ref[...] = jnp.concatenate(outs[len(fwd):], axis=1)


def _wkv(r, k, v, kk, lw, a, rw_k_a):
    bsz, seq, c = r.shape
    nc = seq // CHUNK
    fwd = lambda lane_blk: pl.BlockSpec((None, CHUNK, c), lambda b, j: (b, j, lane_blk))
    bwd = lambda lane_blk: pl.BlockSpec((None, CHUNK, c), lambda b, j: (b, nc - 1 - j, lane_blk))
    return pl.pallas_call(
        _wkv_kernel,
        grid=(bsz, nc),
        in_specs=[fwd(0)] * 4 + [fwd(0), fwd(0)] + [bwd(0)] * 4 + [bwd(1), bwd(1)]
        + [pl.BlockSpec((1, c), lambda b, j: (0, 0))],
        out_specs=[fwd(0), bwd(0)],
        out_shape=[jax.ShapeDtypeStruct((bsz, seq, c), F32)] * 2,
        scratch_shapes=[pltpu.VMEM((2, c // MXU_DIM, MXU_DIM, MXU_DIM), F32)],
        compiler_params=_params("arbitrary", "arbitrary"),
        name="wkv7_chunked",
    )(r, k, v, kk, lw, a, r, k, v, kk, lw, a, rw_k_a.reshape(1, c))


NEG_INF = float("-inf")


def _first_max(vals, idx, size):
    m = jnp.max(vals, axis=0, keepdims=True)
    i = jnp.min(jnp.where(vals == m, idx, size), axis=0, keepdims=True)
    return m, i


def _route(scores, biased):
    e, tt = scores.shape
    per = e // N_GROUPS
    rowl = lax.broadcasted_iota(jnp.int32, (per, tt), 0)
    gs = []
    for g in range(N_GROUPS):
        blk = biased[g * per:(g + 1) * per]
        m1, i1 = _first_max(blk, rowl, per)
        m2 = jnp.max(jnp.where(rowl == i1, NEG_INF, blk), axis=0, keepdims=True)
        gs.append(m1 + m2)
    cur = jnp.concatenate(gs, axis=0)
    growl = lax.broadcasted_iota(jnp.int32, (N_GROUPS, tt), 0)
    gsel = jnp.zeros((N_GROUPS, tt), F32)
    for _ in range(TOPK_GROUPS):
        _, ig = _first_max(cur, growl, N_GROUPS)
        hit = growl == ig
        gsel = jnp.where(hit, 1.0, gsel)
        cur = jnp.where(hit, NEG_INF, cur)
    emask = jnp.concatenate([jnp.broadcast_to(gsel[g:g + 1], (per, tt)) for g in range(N_GROUPS)], axis=0)
    masked = jnp.where(emask > 0.5, biased, NEG_INF)
    row = lax.broadcasted_iota(jnp.int32, (e, tt), 0)
    ids, ws = [], []
    for _ in range(TOP_K):
        _, ie = _first_max(masked, row, e)
        hit = row == ie
        ids.append(ie)
        ws.append(jnp.sum(jnp.where(hit, scores, 0.0), axis=0, keepdims=True))
        masked = jnp.where(hit, NEG_INF, masked)
    w = jnp.concatenate(ws, axis=0)
    w = w / jnp.sum(w, axis=0, keepdims=True) * ROUTE_SCALE
    return jnp.concatenate(ids, axis=0), w


def _mixout_kernel(x_ref, mod_ref, yhy_ref, of_ref, ob_ref, g_ref, bonus_ref, lnw_ref, lnb_ref,
                   ones_ref, wout_ref, g2n_ref, rwt_ref, bias_ref,
                   x1_ref, h2_ref, eid_ref, wsel_ref):
    ones = ones_ref[...]
    s = of_ref[...] + ob_ref[...]
    mean = _dot_exact_rhs(s, ones) * (1.0 / HEAD)
    dlt = s - mean
    var = _dot_exact_rhs(dlt * dlt, ones) * (1.0 / HEAD)
    sn = dlt * lax.rsqrt(var + GN_EPS) * lnw_ref[...] + lnb_ref[...]
    yrw = (sn + bonus_ref[...]) * g_ref[...]
    ch = yhy_ref.shape[-1]
    mix = _dot(yhy_ref[...].astype(BF16), wout_ref[:ch, :]) + _dot(yrw.astype(BF16), wout_ref[ch:, :])
    x1 = x_ref[...] + mod_ref[2:3, :] * mix
    x1_ref[...] = x1
    ms = jnp.mean(x1 * x1, axis=-1, keepdims=True)
    h2 = x1 * lax.rsqrt(ms + NORM_EPS) * g2n_ref[...]
    h2 = h2 * (1.0 + mod_ref[4:5, :]) + mod_ref[3:4, :]
    h2_ref[...] = h2
    rh, rl = _split2(rwt_ref[...])
    hh, hl = _split2(h2)
    logits = _nt(rh, hh) + (_nt(rh, hl) + _nt(rl, hh))
    scores = jax.nn.sigmoid(logits)
    ids, w = _route(scores, scores + bias_ref[...])
    eid_ref[...] = ids
    wsel_ref[...] = w


def _mix_out(x, mod, yhy, o_f, o_b, g, bonus, ln_w, ln_b, w_out, norm2_g, router_w, router_bias, tt=256):
    bsz, seq, d = x.shape
    tt = min(tt, seq)
    nt = seq // tt
    n = bsz * seq
    c = D_RWKV
    e = router_w.shape[1]
    row = lambda a: a.reshape(1, -1)
    consts = [row(ln_w), row(ln_b), _head_ones(), w_out.astype(BF16), row(norm2_g), router_w.T,
              jnp.broadcast_to(router_bias.reshape(e, 1), (e, tt))]
    const = lambda a: pl.BlockSpec(a.shape, lambda b, i: (0,) * a.ndim)
    tile = lambda w: pl.BlockSpec((None, tt, w), lambda b, i: (b, i, 0))
    flat = lambda rows, dt: jax.ShapeDtypeStruct((rows, n), dt)
    return pl.pallas_call(
        _mixout_kernel,
        grid=(bsz, nt),
        in_specs=[tile(d), pl.BlockSpec((None,) + mod.shape[1:], lambda b, i: (b, 0, 0))]
        + [tile(c)] * 5 + [const(a) for a in consts],
        out_specs=[tile(d), pl.BlockSpec((tt, d), lambda b, i: (b * nt + i, 0)),
                   pl.BlockSpec((TOP_K, tt), lambda b, i: (0, b * nt + i)),
                   pl.BlockSpec((TOP_K, tt), lambda b, i: (0, b * nt + i))],
        out_shape=[jax.ShapeDtypeStruct((bsz, seq, d), F32), jax.ShapeDtypeStruct((n, d), F32),
                   flat(TOP_K, jnp.int32), flat(TOP_K, F32)],
        compiler_params=_params("arbitrary", "arbitrary"),
        name="mix_out_router",
    )(x, mod, yhy, o_f, o_b, g, bonus, *consts)


BLK = 256
BLK_SHIFT = 8


def _multi_hot(eid, e):
    row = lax.broadcasted_iota(jnp.int32, (e, eid.shape[1]), 0)
    mh = jnp.zeros((e, eid.shape[1]), F32)
    for kk in range(TOP_K):
        mh = mh + jnp.where(row == eid[kk:kk + 1, :], 1.0, 0.0)
    return row, mh


def _lookup(row, eid, table):
    return jnp.concatenate(
        [jnp.sum(jnp.where(row == eid[kk:kk + 1, :], table, 0.0), axis=0, keepdims=True)
         for kk in range(TOP_K)], axis=0)


def _rank_kernel(eid_ref, rank_ref, cnt_ref, *, e):
    @pl.when(pl.program_id(0) == 0)
    def _():
        cnt_ref[...] = jnp.zeros_like(cnt_ref)

    eid = eid_ref[...]
    tt = eid.shape[1]
    row, mh = _multi_hot(eid, e)
    mhb = mh.astype(BF16)
    si = lax.broadcasted_iota(jnp.int32, (tt, tt), 0)
    ti = lax.broadcasted_iota(jnp.int32, (tt, tt), 1)
    earlier = _dot(mhb, jnp.where(si < ti, 1.0, 0.0).astype(BF16))
    cnt = cnt_ref[...]
    full = earlier + jnp.concatenate([cnt] * (tt // LANES), axis=1)
    rank_ref[...] = _lookup(row, eid, full).astype(jnp.int32)
    cnt_ref[...] = cnt + _dot(mhb, jnp.ones((tt, LANES), BF16))


def _expert_ranks(eid, e, tt=512):
    n = eid.shape[1]
    tt = min(tt, n)
    return pl.pallas_call(
        functools.partial(_rank_kernel, e=e),
        grid=(n // tt,),
        in_specs=[pl.BlockSpec((TOP_K, tt), lambda i: (0, i))],
        out_specs=[pl.BlockSpec((TOP_K, tt), lambda i: (0, i)),
                   pl.BlockSpec((e, LANES), lambda i: (0, 0))],
        out_shape=[jax.ShapeDtypeStruct((TOP_K, n), jnp.int32), jax.ShapeDtypeStruct((e, LANES), F32)],
        compiler_params=_params("arbitrary"),
        name="expert_ranks",
    )(eid)


def _block_offsets(cnt):
    e = cnt.shape[0]
    nblk = ((cnt.astype(jnp.int32) + (BLK - 1)) >> BLK_SHIFT).astype(F32)
    ri = lax.broadcasted_iota(jnp.int32, (e, e), 0)
    ci = lax.broadcasted_iota(jnp.int32, (e, e), 1)
    tril = jnp.where(ci <= ri, 1.0, 0.0).astype(BF16)
    nh, nl = _split2(nblk)
    return nblk, _dot(tril, nh) + _dot(tril, nl)


def _dest_kernel(cnt_ref, eid_ref, rank_ref, dest_ref):
    nblk, end = _block_offsets(cnt_ref[...])
    off = (end - nblk) * float(BLK)
    eid = eid_ref[...]
    tt = eid.shape[1]
    row = lax.broadcasted_iota(jnp.int32, (off.shape[0], tt), 0)
    table = jnp.concatenate([off] * (tt // LANES), axis=1)
    dest_ref[...] = _lookup(row, eid, table).astype(jnp.int32) + rank_ref[...]


def _destinations(cnt, eid, rank, tt=512):
    n = eid.shape[1]
    tt = min(tt, n)
    blk = pl.BlockSpec((TOP_K, tt), lambda i: (0, i))
    return pl.pallas_call(
        _dest_kernel,
        grid=(n // tt,),
        in_specs=[pl.BlockSpec(cnt.shape, lambda i: (0, 0)), blk, blk],
        out_specs=blk,
        out_shape=jax.ShapeDtypeStruct((TOP_K, n), jnp.int32),
        compiler_params=_params("arbitrary"),
        name="expert_destinations",
    )(cnt, eid, rank)


def _meta_kernel(cnt_ref, meta_ref, emeta_ref, *, nbp):
    cnt = cnt_ref[...]
    e = cnt.shape[0]
    nblk, end = _block_offsets(cnt)
    rep = lambda a, w: jnp.concatenate([a] * (w // LANES), axis=1)
    b = lax.broadcasted_iota(jnp.int32, (e, nbp), 1).astype(F32)
    blk_e = jnp.minimum(jnp.sum(jnp.where(rep(end, nbp) <= b, 1.0, 0.0), axis=0, keepdims=True), float(e - 1))
    row = lax.broadcasted_iota(jnp.int32, (e, nbp), 0).astype(F32)
    mine = row == blk_e
    left = rep(cnt + (end - nblk) * float(BLK), nbp) - b * float(BLK)
    nvalid = jnp.clip(jnp.sum(jnp.where(mine, left, 0.0), axis=0, keepdims=True), 0.0, float(BLK))
    nused = jnp.max(rep(end, nbp), axis=0, keepdims=True)
    meta_ref[...] = jnp.concatenate([blk_e, nvalid, nused, jnp.zeros((5, nbp), F32)], axis=0).astype(jnp.int32)
    eye = lax.broadcasted_iota(jnp.int32, (e, e), 0) == lax.broadcasted_iota(jnp.int32, (e, e), 1)
    to_row = lambda a: jnp.sum(jnp.where(eye, rep(a, e), 0.0), axis=0, keepdims=True)
    emeta_ref[...] = jnp.concatenate([to_row(end), to_row(cnt), jnp.zeros((6, e), F32)], axis=0).astype(jnp.int32)


def _block_meta(cnt, nb):
    e = cnt.shape[0]
    nbp = -(-nb // LANES) * LANES
    return pl.pallas_call(
        functools.partial(_meta_kernel, nbp=nbp),
        out_shape=[jax.ShapeDtypeStruct((8, nbp), jnp.int32), jax.ShapeDtypeStruct((8, e), jnp.int32)],
        compiler_params=pltpu.CompilerParams(vmem_limit_bytes=VMEM_LIMIT),
        name="expert_block_meta",
    )(cnt)


def _row_copy(src_ref, s, dst_ref, t, sem):
    return pltpu.make_async_copy(src_ref.at[pl.ds(s, 1), :], dst_ref.at[pl.ds(t, 1), :], sem)


def _dispatch_kernel(eend_ref, ecnt_ref, dest_ref, h_ref, xs_ref, zero_ref, zsem, sem, *, e, tt):
    @pl.when(pl.program_id(0) == 0)
    def _():
        zero_ref[...] = jnp.zeros_like(zero_ref)

        def tail(ex):
            start = pl.multiple_of((eend_ref[ex] - 1) * BLK, BLK)
            return pltpu.make_async_copy(zero_ref, xs_ref.at[pl.ds(start, BLK), :], zsem)

        def issue(ex, carry):
            @pl.when(ecnt_ref[ex] > 0)
            def _():
                tail(ex).start()
            return carry

        def drain(ex, carry):
            @pl.when(ecnt_ref[ex] > 0)
            def _():
                tail(ex).wait()
            return carry

        lax.fori_loop(0, e, issue, 0)
        lax.fori_loop(0, e, drain, 0)

    def issue_rows(t, carry):
        for kk in range(TOP_K):
            _row_copy(h_ref, t, xs_ref, dest_ref[kk, t], sem).start()
        return carry

    def drain_rows(t, carry):
        for kk in range(TOP_K):
            _row_copy(h_ref, t, xs_ref, dest_ref[kk, t], sem).wait()
        return carry

    lax.fori_loop(0, tt, issue_rows, 0)
    lax.fori_loop(0, tt, drain_rows, 0)


def _dispatch(eend, ecnt, dest, h2, nb, tt=256):
    n, d = h2.shape
    e = eend.shape[0]
    tt = min(tt, n)
    return pl.pallas_call(
        functools.partial(_dispatch_kernel, e=e, tt=tt),
        grid_spec=pltpu.PrefetchScalarGridSpec(
            num_scalar_prefetch=2,
            grid=(n // tt,),
            in_specs=[pl.BlockSpec((TOP_K, tt), lambda i, *_: (0, i), memory_space=pltpu.SMEM),
                      pl.BlockSpec((tt, d), lambda i, *_: (i, 0))],
            out_specs=pl.BlockSpec(memory_space=pl.ANY),
            scratch_shapes=[pltpu.VMEM((BLK, d), F32), pltpu.SemaphoreType.DMA(()),
                            pltpu.SemaphoreType.DMA(())],
        ),
        out_shape=jax.ShapeDtypeStruct((nb * BLK, d), F32),
        compiler_params=_params("arbitrary"),
        name="moe_dispatch",
    )(eend, ecnt, dest, h2)


def _experts_kernel(be_ref, nv_ref, nu_ref, x_ref, wg_ref, wu_ref, wd_ref, o_ref, wgb, wub, wdb):
    b = pl.program_id(0)
    used = b < nu_ref[0]

    @pl.when(used)
    def _():
        prev = be_ref[jnp.maximum(b - 1, 0)]

        @pl.when(jnp.logical_or(b == 0, be_ref[b] != prev))
        def _():
            wgb[...] = wg_ref[...].astype(BF16)
            wub[...] = wu_ref[...].astype(BF16)
            wdb[...] = wd_ref[...].astype(BF16)

        row = lax.broadcasted_iota(jnp.int32, (x_ref.shape[0], 1), 0)
        x = jnp.where(row < nv_ref[b], x_ref[...], 0.0).astype(BF16)
        act = _silu(_dot(x, wgb[...])) * _dot(x, wub[...])
        o_ref[...] = _dot(act.astype(BF16), wdb[...])

    @pl.when(jnp.logical_not(used))
    def _():
        o_ref[...] = jnp.zeros_like(o_ref)


def _experts(blk_e, nvalid, nused, xs, wg, wu, wd):
    p, d = xs.shape
    nb = p // BLK
    de = wg.shape[2]
    last = lambda b, be, nv, nu: jnp.minimum(b, nu[0] - 1)
    return pl.pallas_call(
        _experts_kernel,
        grid_spec=pltpu.PrefetchScalarGridSpec(
            num_scalar_prefetch=3,
            grid=(nb,),
            in_specs=[pl.BlockSpec((BLK, d), lambda b, be, nv, nu: (last(b, be, nv, nu), 0)),
                      pl.BlockSpec((None, d, de), lambda b, be, nv, nu: (be[last(b, be, nv, nu)], 0, 0)),
                      pl.BlockSpec((None, d, de), lambda b, be, nv, nu: (be[last(b, be, nv, nu)], 0, 0)),
                      pl.BlockSpec((None, de, d), lambda b, be, nv, nu: (be[last(b, be, nv, nu)], 0, 0))],
            out_specs=pl.BlockSpec((BLK, d), lambda b, be, nv, nu: (b, 0)),
            scratch_shapes=[pltpu.VMEM((d, de), BF16), pltpu.VMEM((d, de), BF16), pltpu.VMEM((de, d), BF16)],
        ),
        out_shape=jax.ShapeDtypeStruct((p, d), F32),
        compiler_params=_params("arbitrary"),
        name="moe_experts",
    )(blk_e, nvalid, nused, xs, wg, wu, wd)


def _combine_kernel(dest_ref, w_ref, x1_ref, h2_ref, mod_ref, ys_ref, sg_ref, su_ref, sd_ref, gf_ref,
                    sel_ref, o_ref, buf, sem, *, tt):
    def issue(t, carry):
        for kk in range(TOP_K):
            _row_copy(ys_ref, dest_ref[kk, t], buf.at[kk], t, sem).start()
        return carry

    def drain(t, carry):
        for kk in range(TOP_K):
            _row_copy(ys_ref, dest_ref[kk, t], buf.at[kk], t, sem).wait()
        return carry

    lax.fori_loop(0, tt, issue, 0)
    hb = h2_ref[...].astype(BF16)
    act = _silu(_dot(hb, sg_ref[...])) * _dot(hb, su_ref[...])
    ffn = _dot(act.astype(BF16), sd_ref[...])
    lax.fori_loop(0, tt, drain, 0)
    wh, wl = _split2(w_ref[...])
    for kk in range(TOP_K):
        sel = sel_ref[kk]
        ffn = ffn + buf[kk] * (_tn(wh, sel) + _tn(wl, sel))
    xo = x1_ref[...] + mod_ref[5:6, :] * ffn
    ms = jnp.mean(xo * xo, axis=-1, keepdims=True)
    o_ref[...] = xo * lax.rsqrt(ms + NORM_EPS) * gf_ref[...]


def _combine(dest, wsel, x1, h2, mod, ys, sh_wg, sh_wu, sh_wd, normf_g, tt=128):
    bsz, seq, d = x1.shape
    n = bsz * seq
    tt = min(tt, seq)
    per = seq // tt
    sel = jnp.asarray(np.broadcast_to(np.eye(TOP_K)[:, :, None], (TOP_K, TOP_K, d)), BF16)
    consts = [sh_wg.astype(BF16), sh_wu.astype(BF16), sh_wd.astype(BF16), normf_g.reshape(1, d), sel]
    const = lambda a: pl.BlockSpec(a.shape, lambda i: (0,) * a.ndim)
    rows = pl.BlockSpec((tt, d), lambda i: (i, 0))
    return pl.pallas_call(
        functools.partial(_combine_kernel, tt=tt),
        grid=(n // tt,),
        in_specs=[pl.BlockSpec((TOP_K, tt), lambda i: (0, i), memory_space=pltpu.SMEM),
                  pl.BlockSpec((TOP_K, tt), lambda i: (0, i)),
                  rows, rows,
                  pl.BlockSpec((None,) + mod.shape[1:], lambda i: (i // per, 0, 0)),
                  pl.BlockSpec(memory_space=pl.ANY)] + [const(a) for a in consts],
        out_specs=rows,
        out_shape=jax.ShapeDtypeStruct((n, d), F32),
        scratch_shapes=[pltpu.VMEM((TOP_K, tt, d), F32), pltpu.SemaphoreType.DMA(())],
        compiler_params=_params("arbitrary"),
        name="moe_combine",
    )(dest, wsel, x1.reshape(n, d), h2, mod, ys, *consts)


def _moe(x1, h2, mod, eid, wsel, exp_wg, exp_wu, exp_wd, sh_wg, sh_wu, sh_wd, normf_g):
    n = h2.shape[0]
    e = exp_wg.shape[0]
    nb = (n * TOP_K + e * (BLK - 1)) // BLK
    rank, cnt = _expert_ranks(eid, e)
    dest = _destinations(cnt, eid, rank)
    meta, emeta = _block_meta(cnt, nb)
    xs = _dispatch(emeta[0], emeta[1], dest, h2, nb)
    ys = _experts(meta[0, :nb], meta[1, :nb], meta[2, :1], xs, exp_wg, exp_wu, exp_wd)
    return _combine(dest, wsel, x1, h2, mod, ys, sh_wg, sh_wu, sh_wd, normf_g)


def kernel(x, c, norm1_g, norm2_g, normf_g, w_ada, b_ada, w_in, w_out, hy_conv_w, hy_conv_b, hy_pos_w1, hy_pos_b1, hy_pos_w2, hy_pos_b2, hy_pos_w3, hy_sin_freq, hy_skip, rw_mu, rw_w0, rw_w_up, rw_a0, rw_a_up, rw_g_up, rw_k_k, rw_k_a, rw_r_k, rw_ln_w, rw_ln_b, router_w, router_bias, exp_w_gate, exp_w_up, exp_w_down, sh_w_gate, sh_w_up, sh_w_down):
    bsz, seq, d = x.shape
    depth = w_ada.shape[0]
    assert depth == 1, "the final norm is fused into the last kernel of a single layer"
    for l in range(depth):
        mod = _modulation(c, w_ada[l], b_ada[l]).reshape(bsz, -1, d)
        uhy, r, k, v, kk, lw, a, g, bonus = _projection(
            x, mod, norm1_g[l], w_in[l], hy_conv_w[l], hy_conv_b[l], rw_mu[l], rw_w0[l], rw_w_up[l],
            rw_a0[l], rw_a_up[l], rw_g_up[l], rw_k_k[l], rw_k_a[l], rw_r_k[l])
        k2, ss = _hyena_filters(seq, hy_pos_w1[l], hy_pos_b1[l], hy_pos_w2[l], hy_pos_b2[l],
                                hy_pos_w3[l], hy_sin_freq[l])
        khat = _filter_spectrum(k2, ss, seq)
        z, z_col = uhy, 0
        for order in range(HYENA_ORDER):
            z = _long_conv_gate(z, z_col, uhy, (order + 1) * D_HYENA, khat, hy_skip[l], order)
            z_col = 0
        o_f, o_b = _wkv(r, k, v, kk, lw, a, rw_k_a[l])
        x1, h2, eid, wsel = _mix_out(x, mod, z, o_f, o_b, g, bonus, rw_ln_w[l], rw_ln_b[l], w_out[l],
                                     norm2_g[l], router_w[l], router_bias[l])
        x = _moe(x1, h2, mod, eid, wsel, exp_w_gate[l], exp_w_up[l], exp_w_down[l],
                 sh_w_gate[l], sh_w_up[l], sh_w_down[l], normf_g)
        x = x.reshape(bsz, seq, d)
    return x
```

```python
import functools
import math

import jax
import jax.numpy as jnp
import numpy as np
from jax import lax
from jax.experimental import pallas as pl
from jax.experimental.pallas import tpu as pltpu

F32 = jnp.float32
BF16 = jnp.bfloat16

LANES = 128
MXU_DIM = 256
VMEM_LIMIT = 56 * 1024 * 1024

D_HYENA = 512
D_RWKV = 512
HEAD = 64
N_HEADS = D_RWKV // HEAD
HYENA_ORDER = 2
FILTER_BANDS = 16
DECAY_TARGET = 1e-2
FAST_DECAY_PCT = 0.3
SLOW_DECAY_PCT = 1.5
FILTER_NORM_EPS = 1e-6
DECAY_LORA = 32
ICLR_LORA = 32
GATE_LORA = 96
GN_EPS = 64e-5
NORM_EPS = 1e-6
N_EXPERTS = 256
TOP_K = 8
N_GROUPS = 8
TOPK_GROUPS = 4
ROUTE_SCALE = 2.5
D_EXPERT = 256


def _params(*sem):
    return pltpu.CompilerParams(dimension_semantics=sem, vmem_limit_bytes=VMEM_LIMIT)


def _split2(a):
    hi = a.astype(BF16)
    lo = (a - hi.astype(F32)).astype(BF16)
    return hi, lo


def _dot(a, b):
    return jnp.dot(a, b, preferred_element_type=F32)


def _dot3(a, b):
    ah, al = _split2(a)
    bh, bl = _split2(b)
    return _dot(ah, bh) + (_dot(ah, bl) + _dot(al, bh))


def _dot_exact_rhs(a, b_bf16):
    ah, al = _split2(a)
    return _dot(ah, b_bf16) + _dot(al, b_bf16)


def _silu(x):
    return x * jax.nn.sigmoid(x)


U32 = jnp.uint32


def _pack_halves(x):
    w = x.shape[1] // 2
    rounded = x.astype(BF16).astype(F32)
    bits = lax.bitcast_convert_type(rounded, U32)
    return (bits[:, w:] & jnp.uint32(0xFFFF0000)) | (bits[:, :w] >> 16)


def _unpack_halves(p):
    lo = lax.bitcast_convert_type(p << 16, F32)
    hi = lax.bitcast_convert_type(p & jnp.uint32(0xFFFF0000), F32)
    return lo, hi


def _mod_kernel(c_ref, w_ref, b_ref, o_ref):
    o_ref[...] = _dot3(_silu(c_ref[...]), w_ref[...]) + b_ref[...]


def _modulation(c, w_ada, b_ada):
    bsz, d = c.shape
    n = w_ada.shape[1]
    blk = 1024
    return pl.pallas_call(
        _mod_kernel,
        grid=(n // blk,),
        in_specs=[
            pl.BlockSpec((bsz, d), lambda j: (0, 0)),
            pl.BlockSpec((d, blk), lambda j: (0, j)),
            pl.BlockSpec((1, blk), lambda j: (0, j)),
        ],
        out_specs=pl.BlockSpec((bsz, blk), lambda j: (0, j)),
        out_shape=jax.ShapeDtypeStruct((bsz, n), F32),
        compiler_params=_params("arbitrary"),
        name="adaln_mod",
    )(c, w_ada, b_ada.reshape(1, n))


def _filter_kernel(band_ref, w1_ref, b1_ref, w2_ref, b2_ref, w3_ref, freq_ref, delta_ref,
                   k_ref, ss_ref, *, seq, rows):
    half = pl.program_id(0)
    i = pl.program_id(1)
    r = lax.broadcasted_iota(jnp.int32, (rows, LANES), 0) + i * rows
    pos = jnp.where(half == 0, r, seq - r).astype(F32)
    tt = pos / float(max(seq - 1, 1))
    lane = lax.broadcasted_iota(jnp.int32, (rows, LANES), 1)
    ang = pos * band_ref[...]
    feats = jnp.where(lane == 0, tt,
                      jnp.where(lane <= FILTER_BANDS, jnp.cos(ang),
                                jnp.where(lane <= 2 * FILTER_BANDS, -jnp.sin(ang), 0.0)))
    freq = freq_ref[...]
    hdn = jnp.sin(freq * (_dot3(feats, w1_ref[...]) + b1_ref[...]))
    for j in range(w2_ref.shape[0]):
        hdn = jnp.sin(freq * (_dot3(hdn, w2_ref[j]) + b2_ref[j]))
    filt = _dot3(hdn, w3_ref[...])
    filt = filt * jnp.exp(-tt[:, :1] * delta_ref[...])
    valid = jnp.logical_or(half == 0, r[:, :1] > 0)
    filt = jnp.where(valid, filt, 0.0)
    k_ref[...] = filt

    @pl.when(jnp.logical_and(half == 0, i == 0))
    def _():
        ss_ref[...] = jnp.zeros_like(ss_ref)

    ss_ref[...] += jnp.broadcast_to(jnp.sum(filt * filt, axis=0, keepdims=True), ss_ref.shape)


def _hyena_filters(seq, pw1, pb1, pw2, pb2, pw3, freq):
    width = pw1.shape[1]
    ncol = HYENA_ORDER * D_HYENA
    rows = min(seq, 512)
    bands = np.zeros((1, LANES), np.float64)
    lin = np.linspace(1e-4, FILTER_BANDS - 1, FILTER_BANDS)
    bands[0, 1:1 + FILTER_BANDS] = lin
    bands[0, 1 + FILTER_BANDS:1 + 2 * FILTER_BANDS] = lin
    bands = jnp.asarray(bands * (2.0 * math.pi / seq), F32)
    deltas = np.abs(np.linspace(math.log(DECAY_TARGET) / SLOW_DECAY_PCT,
                                math.log(DECAY_TARGET) / FAST_DECAY_PCT, D_HYENA))
    deltas = jnp.asarray(np.tile(deltas, HYENA_ORDER)[None], F32)
    w1 = jnp.zeros((LANES, width), F32).at[:pw1.shape[0]].set(pw1)
    w3 = pw3.reshape(width, HYENA_ORDER, 2, D_HYENA).transpose(2, 0, 1, 3).reshape(2, width, ncol)
    nt = seq // rows
    full = lambda *shape: pl.BlockSpec(shape, lambda h, i: (0,) * len(shape))
    return pl.pallas_call(
        functools.partial(_filter_kernel, seq=seq, rows=rows),
        grid=(2, nt),
        in_specs=[
            full(1, LANES), full(LANES, width), full(1, width),
            full(pw2.shape[0], width, width), full(pw2.shape[0], 1, width),
            pl.BlockSpec((None, width, ncol), lambda h, i: (h, 0, 0)),
            full(1, width), full(1, ncol),
        ],
        out_specs=[
            pl.BlockSpec((rows, ncol), lambda h, i: (h * nt + i, 0)),
            pl.BlockSpec((8, ncol), lambda h, i: (0, 0)),
        ],
        out_shape=[jax.ShapeDtypeStruct((2 * seq, ncol), F32),
                   jax.ShapeDtypeStruct((8, ncol), F32)],
        compiler_params=_params("arbitrary", "arbitrary"),
        name="hyena_filters",
    )(bands, w1, pb1.reshape(1, width), pw2, pb2.reshape(pw2.shape[0], 1, width), w3,
      freq.reshape(1, width), deltas)


N1 = LANES
UNROLL = 8


def _dft_tables(seq):
    m = 2 * seq
    n2 = m // N1
    n2h = n2 // 2
    n1 = np.arange(N1)[:, None, None]
    f2 = np.arange(n2)[None, :, None]
    k2 = np.arange(n2)[None, None, :]
    th = 2.0 * np.pi * (n1 * f2 / m + (k2 * f2 % n2) / n2)
    fwd_a = np.concatenate([np.cos(th), -np.sin(th)], axis=1)
    tht = np.transpose(th, (0, 2, 1))
    inv_a = np.concatenate([np.cos(tht), -np.sin(tht)], axis=2)[:, :n2h] / m
    a = np.arange(N1)
    ph = 2.0 * np.pi * np.outer(a, a) / N1
    c, s = np.cos(ph), np.sin(ph)
    fwd_b = np.block([[c, s], [-s, c]])
    inv_b = np.block([[c, -s], [s, c]])
    cast = lambda t: jnp.asarray(t, BF16)
    return cast(fwd_a), cast(fwd_a[:, :, :n2h]), cast(inv_a), cast(fwd_b), cast(inv_b), n2, n2h


def _stage_a_fwd(x_ref, wa_ref, y_ref, n2, scale=None):
    def body(i, carry):
        trips = [i * UNROLL + j for j in range(UNROLL)]
        xs = [x_ref[pl.ds(n1, wa_ref.shape[2], stride=N1), :] for n1 in trips]
        if scale is not None:
            xs = [x * scale for x in xs]
        prods = [_dot(wa_ref[n1], x.astype(BF16)) for n1, x in zip(trips, xs)]
        for n1, a in zip(trips, prods):
            y_ref[pl.ds(n1, n2, stride=2 * N1), :] = a[:n2]
            y_ref[pl.ds(N1 + n1, n2, stride=2 * N1), :] = a[n2:]
        return carry
    lax.fori_loop(0, N1 // UNROLL, body, 0)


def _filter_fft_kernel(k_ref, ss_ref, wa_ref, fb_ref, o_ref, y_ref, *, n2):
    scale = lax.rsqrt(ss_ref[0:1, :] + FILTER_NORM_EPS)
    _stage_a_fwd(k_ref, wa_ref, y_ref, n2, scale=scale)

    unr = min(UNROLL, n2)

    def body(i, carry):
        trips = [i * unr + j for j in range(unr)]
        ys = [y_ref[pl.ds(pl.multiple_of(f2 * 2 * N1, 2 * N1), 2 * N1), :].astype(BF16) for f2 in trips]
        for f2, y in zip(trips, ys):
            o_ref[f2] = _dot(fb_ref[...], y)
        return carry
    lax.fori_loop(0, n2 // unr, body, 0)


def _filter_spectrum(k2, ss, seq):
    fwd_a, _, _, fwd_b, _, n2, _ = _dft_tables(seq)
    ncol = k2.shape[1]
    nblk = ncol // LANES
    return pl.pallas_call(
        functools.partial(_filter_fft_kernel, n2=n2),
        grid=(nblk,),
        in_specs=[
            pl.BlockSpec((2 * seq, LANES), lambda c: (0, c)),
            pl.BlockSpec((8, LANES), lambda c: (0, c)),
            pl.BlockSpec(fwd_a.shape, lambda c: (0, 0, 0)),
            pl.BlockSpec(fwd_b.shape, lambda c: (0, 0)),
        ],
        out_specs=pl.BlockSpec((None, n2, 2 * N1, LANES), lambda c: (c, 0, 0, 0)),
        out_shape=jax.ShapeDtypeStruct((nblk, n2, 2 * N1, LANES), F32),
        scratch_shapes=[pltpu.VMEM((n2 * 2 * N1, LANES), F32)],
        compiler_params=_params("arbitrary"),
        name="hyena_filter_fft",
    )(k2, ss, fwd_a, fwd_b)


def _conv_kernel(u_ref, g_ref, kh_ref, skip_ref, wa_ref, va_ref, fb_ref, ib_ref, o_ref, y_ref,
                 *, n2, n2h):
    _stage_a_fwd(u_ref, wa_ref, y_ref, n2)

    unr = min(UNROLL, n2)

    def mid(i, carry):
        trips = [i * unr + j for j in range(unr)]
        offs = [pl.multiple_of(f2 * 2 * N1, 2 * N1) for f2 in trips]
        zs = [_dot(fb_ref[...], y_ref[pl.ds(off, 2 * N1), :].astype(BF16)) for off in offs]
        ps = []
        for f2, z in zip(trips, zs):
            zr, zi = z[:N1], z[N1:]
            kh = kh_ref[f2]
            kr, ki = kh[:N1], kh[N1:]
            ps.append(jnp.concatenate([zr * kr - zi * ki, zr * ki + zi * kr], axis=0).astype(BF16))
        gs = [_dot(ib_ref[...], p) for p in ps]
        for off, g in zip(offs, gs):
            y_ref[pl.ds(off, 2 * N1), :] = g
        return carry
    lax.fori_loop(0, n2 // unr, mid, 0)

    skip = skip_ref[...]

    def last(i, carry):
        trips = [i * UNROLL + j for j in range(UNROLL)]
        gs = [jnp.concatenate([y_ref[pl.ds(n1, n2, stride=2 * N1), :],
                               y_ref[pl.ds(N1 + n1, n2, stride=2 * N1), :]], axis=0).astype(BF16)
              for n1 in trips]
        convs = [_dot(va_ref[n1], g) for n1, g in zip(trips, gs)]
        for n1, conv in zip(trips, convs):
            rows = pl.ds(n1, n2h, stride=N1)
            u = u_ref[rows, :]
            o_ref[rows, :] = g_ref[rows, :] * (conv + u * skip)
        return carry
    lax.fori_loop(0, N1 // UNROLL, last, 0)


def _long_conv_gate(u, u_col, gate, gate_col, khat, skip, order):
    bsz, seq, _ = u.shape
    ch = D_HYENA
    _, fwd_a, inv_a, fwd_b, inv_b, n2, n2h = _dft_tables(seq)
    nblk = ch // LANES
    const = lambda a: pl.BlockSpec(a.shape, lambda c, b: (0,) * a.ndim)
    at = lambda col: pl.BlockSpec((None, seq, LANES), lambda c, b: (b, 0, col // LANES + c))
    data = at(0)
    return pl.pallas_call(
        functools.partial(_conv_kernel, n2=n2, n2h=n2h),
        grid=(nblk, bsz),
        in_specs=[
            at(u_col), at(gate_col),
            pl.BlockSpec((None, n2, 2 * N1, LANES), lambda c, b: (order * nblk + c, 0, 0, 0)),
            pl.BlockSpec((1, LANES), lambda c, b: (0, c)),
            const(fwd_a), const(inv_a), const(fwd_b), const(inv_b),
        ],
        out_specs=data,
        out_shape=jax.ShapeDtypeStruct((bsz, seq, ch), F32),
        scratch_shapes=[pltpu.VMEM((n2 * 2 * N1, LANES), F32)],
        compiler_params=_params("arbitrary", "arbitrary"),
        name=f"hyena_conv{order}",
    )(u, gate, khat, skip[order].reshape(1, ch), fwd_a, inv_a, fwd_b, inv_b)


HALO = 8


def _shift_rows(p, k):
    return pltpu.roll(p, k % p.shape[0], axis=0)


def _proj_kernel(xp_ref, x_ref, xn_ref, mod_ref, g1_ref, why_ref, wrkv_ref, wlora_ref,
                 cw_ref, cb_ref, murkv_ref, mulora_ref, w0_ref, a0_ref, wwa_ref, gup_ref,
                 kk_ref, ka_ref, rk_ref, ones_ref,
                 uhy_ref, r_ref, k_ref, v_ref, kkn_ref, lw_ref, a_ref, g_ref, bonus_ref,
                 *, tt, nt):
    i = pl.program_id(1)
    xe = jnp.concatenate([xp_ref[...], x_ref[...], xn_ref[...]], axis=0)
    ms = jnp.mean(xe * xe, axis=-1, keepdims=True)
    h = xe * lax.rsqrt(ms + NORM_EPS) * g1_ref[...]
    h = h * (1.0 + mod_ref[1:2, :]) + mod_ref[0:1, :]
    row = lax.broadcasted_iota(jnp.int32, (tt + 2 * HALO, 1), 0)
    inside = jnp.logical_and(jnp.logical_or(row >= HALO, i > 0),
                             jnp.logical_or(row < tt + HALO, i < nt - 1))
    hb = jnp.where(inside, h, 0.0).astype(BF16)
    mid = slice(HALO, tt + HALO)

    p = _dot(hb, why_ref[...])
    u = (_shift_rows(p, 1) * cw_ref[0:1, :] + p * cw_ref[1:2, :]
         + _shift_rows(p, -1) * cw_ref[2:3, :] + cb_ref[...])
    uhy_ref[...] = u[mid]

    p = _dot(hb, wrkv_ref[...])
    p = p + murkv_ref[...] * (0.5 * (_shift_rows(p, 1) + _shift_rows(p, -1)) - p)
    p = p[mid]
    c = D_RWKV
    r, k, v = p[:, :c], p[:, c:2 * c], p[:, 2 * c:]
    r_ref[...] = r
    k_ref[...] = k
    v_ref[...] = v

    q = _dot(hb, wlora_ref[...])
    q = q + mulora_ref[...] * (0.5 * (_shift_rows(q, 1) + _shift_rows(q, -1)) - q)
    q = q[mid]
    wa = q[:, :LANES]
    lane = lax.broadcasted_iota(jnp.int32, wa.shape, 1)
    wa = jnp.where(lane < 2 * DECAY_LORA, jnp.tanh(wa), wa)
    up = _dot3(wa, wwa_ref[...])
    z = -(w0_ref[...] + up[:, :2 * c])
    softplus = jnp.maximum(z, 0.0) + jnp.log1p(jnp.exp(-jnp.abs(z)))
    lw_ref[...] = -jnp.exp(-softplus - 0.5)
    a = jax.nn.sigmoid(a0_ref[...] + up[:, 2 * c:])
    a_ref[...] = a
    g_ref[...] = _dot3(jax.nn.sigmoid(q[:, LANES:]), gup_ref[...])

    ones = ones_ref[...]
    kk = k * kk_ref[...]
    nrm = jnp.sqrt(_dot_exact_rhs(kk * kk, ones))
    kkn_ref[...] = kk / jnp.maximum(nrm, 1e-12)
    ka = ka_ref[...]
    ksum = k * (2.0 + (a[:, :c] + a[:, c:] - 2.0) * ka)
    bonus_ref[...] = _dot_exact_rhs(r * ksum * rk_ref[...], ones) * v


def _head_ones():
    hid = np.arange(D_RWKV) // HEAD
    return jnp.asarray(hid[:, None] == hid[None, :], BF16)


def _projection(x, mod, norm1_g, w_in, hy_conv_w, hy_conv_b, rw_mu, rw_w0, rw_w_up, rw_a0,
                rw_a_up, rw_g_up, rw_k_k, rw_k_a, rw_r_k, tt=256):
    bsz, seq, d = x.shape
    tt = min(tt, seq)
    nt = seq // tt
    c = D_RWKV
    hy = (HYENA_ORDER + 1) * D_HYENA
    nlora = 2 * LANES
    w_hy = w_in[:, :hy].astype(BF16)
    w_rkv = w_in[:, hy:hy + 3 * c].astype(BF16)
    w_lora = jnp.zeros((d, nlora), F32).at[:, :w_in.shape[1] - hy - 3 * c].set(w_in[:, hy + 3 * c:]).astype(BF16)
    mu_rkv = rw_mu[:3 * c].reshape(1, 3 * c)
    mu_lora = jnp.zeros((1, nlora), F32).at[0, :rw_mu.shape[0] - 3 * c].set(rw_mu[3 * c:])
    wwa = jnp.zeros((LANES, 4 * c), F32)
    for dd in range(2):
        wwa = wwa.at[dd * DECAY_LORA:(dd + 1) * DECAY_LORA, dd * c:(dd + 1) * c].set(rw_w_up[dd])
        wwa = wwa.at[2 * DECAY_LORA + dd * ICLR_LORA:2 * DECAY_LORA + (dd + 1) * ICLR_LORA,
                     2 * c + dd * c:2 * c + (dd + 1) * c].set(rw_a_up[dd])
    gup = jnp.zeros((LANES, c), F32).at[:GATE_LORA].set(rw_g_up)
    row = lambda a: a.reshape(1, -1)

    nb8 = seq // HALO
    tb = tt // HALO
    const = lambda a: pl.BlockSpec(a.shape, lambda b, i: (0,) * a.ndim)
    tile = lambda w: pl.BlockSpec((None, tt, w), lambda b, i: (b, i, 0))
    ins = [
        (x, pl.BlockSpec((None, HALO, d), lambda b, i: (b, jnp.maximum(i * tb - 1, 0), 0))),
        (x, pl.BlockSpec((None, tt, d), lambda b, i: (b, i, 0))),
        (x, pl.BlockSpec((None, HALO, d), lambda b, i: (b, jnp.minimum((i + 1) * tb, nb8 - 1), 0))),
        (mod, pl.BlockSpec((None,) + mod.shape[1:], lambda b, i: (b, 0, 0))),
    ]
    consts = [row(norm1_g), w_hy, w_rkv, w_lora, hy_conv_w, row(hy_conv_b), mu_rkv, mu_lora,
              row(rw_w0), row(rw_a0), wwa, gup, row(rw_k_k), row(rw_k_a), row(rw_r_k), _head_ones()]
    ins += [(a, const(a)) for a in consts]
    widths = [hy, c, c, c, c, 2 * c, 2 * c, c, c]
    return pl.pallas_call(
        functools.partial(_proj_kernel, tt=tt, nt=nt),
        grid=(bsz, nt),
        in_specs=[s for _, s in ins],
        out_specs=[tile(w) for w in widths],
        out_shape=[jax.ShapeDtypeStruct((bsz, seq, w), F32) for w in widths],
        compiler_params=_params("arbitrary", "arbitrary"),
        name="input_projection",
    )(*[a for a, _ in ins])


CHUNK = HEAD
GROUP = MXU_DIM // HEAD


def _nt(a, b):
    return lax.dot_general(a, b, (((1,), (1,)), ((), ())), preferred_element_type=F32)


def _tn(a, b):
    return lax.dot_general(a, b, (((0,), (0,)), ((), ())), preferred_element_type=F32)


def _wkv_direction(r, k, v, kk, lw, a, ka, s_ref, reverse):
    c = CHUNK
    ti = lax.broadcasted_iota(jnp.int32, (c, c), 0)
    si = lax.broadcasted_iota(jnp.int32, (c, c), 1)
    tri = (si >= ti) if reverse else (si <= ti)
    cum = _dot_exact_rhs_lhs(jnp.where(tri, 1.0, 0.0).astype(BF16), lw)
    tot = jnp.sum(lw, axis=0, keepdims=True)
    w_incl = jnp.exp(cum)
    w_prev = jnp.exp(cum - lw)
    w_inv = jnp.exp(-cum)
    w_end = jnp.exp(tot - cum)
    w_tot = jnp.exp(tot)
    kd = k * (1.0 + (a - 1.0) * ka)
    b = kk * a
    a_w = -kk * w_prev
    r_w = r * w_incl
    b_w = b * w_inv
    k_w = kd * w_inv
    b_e = b * w_end
    k_e = kd * w_end

    m = MXU_DIM
    ri = lax.broadcasted_iota(jnp.int32, (m, m), 0)
    ci = lax.broadcasted_iota(jnp.int32, (m, m), 1)
    head_mask = (ri // HEAD) == (ci // HEAD)
    tl = lax.broadcasted_iota(jnp.int32, (c, m), 0)
    sl = lax.broadcasted_iota(jnp.int32, (c, m), 1) % c
    strict = (sl > tl) if reverse else (sl < tl)
    incl = (sl >= tl) if reverse else (sl <= tl)
    eye = jnp.where(sl == tl, 1.0, 0.0)
    both = lambda top, bot: jnp.concatenate([top, bot], axis=0)

    def stack(xg):
        xb = xg.astype(BF16)
        return jnp.where(head_mask, jnp.concatenate([xb] * GROUP, axis=0), jnp.zeros((), BF16))

    streams = []
    for g in range(D_RWKV // m):
        sl_g = slice(g * m, (g + 1) * m)
        streams.append(dict(
            ar=both(a_w[:, sl_g], r_w[:, sl_g]).astype(BF16),
            b_st=stack(b_w[:, sl_g]), k_st=stack(k_w[:, sl_g]), v_st=stack(v[:, sl_g]),
            v=v[:, sl_g], bk=both(b_e[:, sl_g], k_e[:, sl_g]).astype(BF16),
            w_tot=w_tot[:, sl_g], s_ref=s_ref.at[g],
            strict=strict, incl=incl, eye=eye, head_mask=head_mask, stack=stack))
    return streams


def _wkv_streams_step(streams):
    c = CHUNK
    both = lambda top, bot: jnp.concatenate([top, bot], axis=0)
    for st in streams:
        st["s"] = st["s_ref"][...]
        st["xb"] = _nt(st["ar"], st["b_st"])
        st["xk"] = _nt(st["ar"], st["k_st"])
        st["xs"] = _nt(st["ar"], st["s"].astype(BF16))
    for st in streams:
        m_ak = jnp.where(st["strict"], st["xk"][:c], 0.0)
        st["rhs"] = st["xs"][:c] + _dot(m_ak.astype(BF16), st["v_st"])
        st["pw"] = jnp.where(st["strict"], st["xb"][:c], 0.0)
        st["t"] = st["eye"] + st["pw"]
        st["p_st"] = st["stack"](st["pw"])
    for _ in range(int(math.log2(c)) - 1):
        for st in streams:
            st["pw"] = _dot(st["pw"].astype(BF16), st["p_st"])
            st["p_st"] = st["stack"](st["pw"])
        for st in streams:
            st["t"] = st["t"] + _dot(st["t"].astype(BF16), st["p_st"])
    for st in streams:
        st["u"] = _dot(st["t"].astype(BF16), st["stack"](st["rhs"]))
    outs = []
    for st in streams:
        m_rb = jnp.where(st["incl"], st["xb"][c:], 0.0)
        m_rk = jnp.where(st["incl"], st["xk"][c:], 0.0)
        outs.append(st["xs"][c:] + _dot(m_rb.astype(BF16), st["stack"](st["u"]))
                    + _dot(m_rk.astype(BF16), st["v_st"]))
        uv = both(st["u"], st["v"]).astype(BF16)
        st["s_ref"][...] = st["s"] * st["w_tot"] + jnp.where(st["head_mask"], _tn(uv, st["bk"]), 0.0)
    return outs


def _dot_exact_rhs_lhs(tri_bf16, x):
    xh, xl = _split2(x)
    return _dot(tri_bf16, xh) + _dot(tri_bf16, xl)


def _wkv_kernel(rf, kf, vf, kkf, lwf, af, rb, kb, vb, kkb, lwb, ab, ka_ref, of_ref, ob_ref, s_ref):
    @pl.when(pl.program_id(1) == 0)
    def _():
        s_ref[...] = jnp.zeros_like(s_ref)

    ka = ka_ref[...]
    fwd = _wkv_direction(rf[...], kf[...], vf[...], kkf[...], lwf[...], af[...], ka, s_ref.at[0], False)
    bwd = _wkv_direction(rb[...], kb[...], vb[...], kkb[...], lwb[...], ab[...], ka, s_ref.at[1], True)
    outs = _wkv_streams_step(fwd + bwd)
    of_ref[...] = jnp.concatenate(outs[:len(fwd)], axis=1)
    ob_ref[...] = jnp.concatenate(outs[len(fwd):], axis=1)


def _wkv(r, k, v, kk, lw, a, rw_k_a):
    bsz, seq, c = r.shape
    nc = seq // CHUNK
    fwd = lambda lane_blk: pl.BlockSpec((None, CHUNK, c), lambda b, j: (b, j, lane_blk))
    bwd = lambda lane_blk: pl.BlockSpec((None, CHUNK, c), lambda b, j: (b, nc - 1 - j, lane_blk))
    return pl.pallas_call(
        _wkv_kernel,
        grid=(bsz, nc),
        in_specs=[fwd(0)] * 4 + [fwd(0), fwd(0)] + [bwd(0)] * 4 + [bwd(1), bwd(1)]
        + [pl.BlockSpec((1, c), lambda b, j: (0, 0))],
        out_specs=[fwd(0), bwd(0)],
        out_shape=[jax.ShapeDtypeStruct((bsz, seq, c), F32)] * 2,
        scratch_shapes=[pltpu.VMEM((2, c // MXU_DIM, MXU_DIM, MXU_DIM), F32)],
        compiler_params=_params("arbitrary", "arbitrary"),
        name="wkv7_chunked",
    )(r, k, v, kk, lw, a, r, k, v, kk, lw, a, rw_k_a.reshape(1, c))


NEG_INF = float("-inf")


def _first_max(vals, idx, size):
    m = jnp.max(vals, axis=0, keepdims=True)
    i = jnp.min(jnp.where(vals == m, idx, size), axis=0, keepdims=True)
    return m, i


def _route(scores, biased):
    e, tt = scores.shape
    per = e // N_GROUPS
    rowl = lax.broadcasted_iota(jnp.int32, (per, tt), 0)
    gs = []
    for g in range(N_GROUPS):
        blk = biased[g * per:(g + 1) * per]
        m1, i1 = _first_max(blk, rowl, per)
        m2 = jnp.max(jnp.where(rowl == i1, NEG_INF, blk), axis=0, keepdims=True)
        gs.append(m1 + m2)
    cur = jnp.concatenate(gs, axis=0)
    growl = lax.broadcasted_iota(jnp.int32, (N_GROUPS, tt), 0)
    gsel = jnp.zeros((N_GROUPS, tt), F32)
    for _ in range(TOPK_GROUPS):
        _, ig = _first_max(cur, growl, N_GROUPS)
        hit = growl == ig
        gsel = jnp.where(hit, 1.0, gsel)
        cur = jnp.where(hit, NEG_INF, cur)
    emask = jnp.concatenate([jnp.broadcast_to(gsel[g:g + 1], (per, tt)) for g in range(N_GROUPS)], axis=0)
    masked = jnp.where(emask > 0.5, biased, NEG_INF)
    row = lax.broadcasted_iota(jnp.int32, (e, tt), 0)
    ids, ws = [], []
    for _ in range(TOP_K):
        _, ie = _first_max(masked, row, e)
        hit = row == ie
        ids.append(ie)
        ws.append(jnp.sum(jnp.where(hit, scores, 0.0), axis=0, keepdims=True))
        masked = jnp.where(hit, NEG_INF, masked)
    w = jnp.concatenate(ws, axis=0)
    w = w / jnp.sum(w, axis=0, keepdims=True) * ROUTE_SCALE
    return jnp.concatenate(ids, axis=0), w


def _mixout_kernel(x_ref, mod_ref, yhy_ref, of_ref, ob_ref, g_ref, bonus_ref, lnw_ref, lnb_ref,
                   ones_ref, wout_ref, g2n_ref, rwt_ref, bias_ref,
                   x1_ref, h2_ref, eid_ref, wsel_ref):
    ones = ones_ref[...]
    s = of_ref[...] + ob_ref[...]
    mean = _dot_exact_rhs(s, ones) * (1.0 / HEAD)
    dlt = s - mean
    var = _dot_exact_rhs(dlt * dlt, ones) * (1.0 / HEAD)
    sn = dlt * lax.rsqrt(var + GN_EPS) * lnw_ref[...] + lnb_ref[...]
    yrw = (sn + bonus_ref[...]) * g_ref[...]
    ch = yhy_ref.shape[-1]
    mix = _dot(yhy_ref[...].astype(BF16), wout_ref[:ch, :]) + _dot(yrw.astype(BF16), wout_ref[ch:, :])
    x1 = x_ref[...] + mod_ref[2:3, :] * mix
    x1_ref[...] = x1
    ms = jnp.mean(x1 * x1, axis=-1, keepdims=True)
    h2 = x1 * lax.rsqrt(ms + NORM_EPS) * g2n_ref[...]
    h2 = h2 * (1.0 + mod_ref[4:5, :]) + mod_ref[3:4, :]
    h2_ref[...] = _pack_halves(h2)
    rh, rl = _split2(rwt_ref[...])
    hh, hl = _split2(h2)
    logits = _nt(rh, hh) + (_nt(rh, hl) + _nt(rl, hh))
    scores = jax.nn.sigmoid(logits)
    ids, w = _route(scores, scores + bias_ref[...])
    eid_ref[...] = ids
    wsel_ref[...] = w


def _mix_out(x, mod, yhy, o_f, o_b, g, bonus, ln_w, ln_b, w_out, norm2_g, router_w, router_bias, tt=256):
    bsz, seq, d = x.shape
    tt = min(tt, seq)
    nt = seq // tt
    n = bsz * seq
    c = D_RWKV
    e = router_w.shape[1]
    row = lambda a: a.reshape(1, -1)
    consts = [row(ln_w), row(ln_b), _head_ones(), w_out.astype(BF16), row(norm2_g), router_w.T,
              jnp.broadcast_to(router_bias.reshape(e, 1), (e, tt))]
    const = lambda a: pl.BlockSpec(a.shape, lambda b, i: (0,) * a.ndim)
    tile = lambda w: pl.BlockSpec((None, tt, w), lambda b, i: (b, i, 0))
    flat = lambda rows, dt: jax.ShapeDtypeStruct((rows, n), dt)
    return pl.pallas_call(
        _mixout_kernel,
        grid=(bsz, nt),
        in_specs=[tile(d), pl.BlockSpec((None,) + mod.shape[1:], lambda b, i: (b, 0, 0))]
        + [tile(c)] * 5 + [const(a) for a in consts],
        out_specs=[tile(d), pl.BlockSpec((tt, d // 2), lambda b, i: (b * nt + i, 0)),
                   pl.BlockSpec((TOP_K, tt), lambda b, i: (0, b * nt + i)),
                   pl.BlockSpec((TOP_K, tt), lambda b, i: (0, b * nt + i))],
        out_shape=[jax.ShapeDtypeStruct((bsz, seq, d), F32), jax.ShapeDtypeStruct((n, d // 2), U32),
                   flat(TOP_K, jnp.int32), flat(TOP_K, F32)],
        compiler_params=_params("arbitrary", "arbitrary"),
        name="mix_out_router",
    )(x, mod, yhy, o_f, o_b, g, bonus, *consts)


BLK = 256
BLK_SHIFT = 8


def _multi_hot(eid, e):
    row = lax.broadcasted_iota(jnp.int32, (e, eid.shape[1]), 0)
    mh = jnp.zeros((e, eid.shape[1]), F32)
    for kk in range(TOP_K):
        mh = mh + jnp.where(row == eid[kk:kk + 1, :], 1.0, 0.0)
    return row, mh


def _lookup(row, eid, table):
    return jnp.concatenate(
        [jnp.sum(jnp.where(row == eid[kk:kk + 1, :], table, 0.0), axis=0, keepdims=True)
         for kk in range(TOP_K)], axis=0)


def _rank_kernel(eid_ref, rank_ref, cnt_ref, *, e):
    @pl.when(pl.program_id(0) == 0)
    def _():
        cnt_ref[...] = jnp.zeros_like(cnt_ref)

    eid = eid_ref[...]
    tt = eid.shape[1]
    row, mh = _multi_hot(eid, e)
    mhb = mh.astype(BF16)
    si = lax.broadcasted_iota(jnp.int32, (tt, tt), 0)
    ti = lax.broadcasted_iota(jnp.int32, (tt, tt), 1)
    earlier = _dot(mhb, jnp.where(si < ti, 1.0, 0.0).astype(BF16))
    cnt = cnt_ref[...]
    full = earlier + jnp.concatenate([cnt] * (tt // LANES), axis=1)
    rank_ref[...] = _lookup(row, eid, full).astype(jnp.int32)
    cnt_ref[...] = cnt + _dot(mhb, jnp.ones((tt, LANES), BF16))


def _expert_ranks(eid, e, tt=512):
    n = eid.shape[1]
    tt = min(tt, n)
    return pl.pallas_call(
        functools.partial(_rank_kernel, e=e),
        grid=(n // tt,),
        in_specs=[pl.BlockSpec((TOP_K, tt), lambda i: (0, i))],
        out_specs=[pl.BlockSpec((TOP_K, tt), lambda i: (0, i)),
                   pl.BlockSpec((e, LANES), lambda i: (0, 0))],
        out_shape=[jax.ShapeDtypeStruct((TOP_K, n), jnp.int32), jax.ShapeDtypeStruct((e, LANES), F32)],
        compiler_params=_params("arbitrary"),
        name="expert_ranks",
    )(eid)


def _block_offsets(cnt):
    e = cnt.shape[0]
    nblk = ((cnt.astype(jnp.int32) + (BLK - 1)) >> BLK_SHIFT).astype(F32)
    ri = lax.broadcasted_iota(jnp.int32, (e, e), 0)
    ci = lax.broadcasted_iota(jnp.int32, (e, e), 1)
    tril = jnp.where(ci <= ri, 1.0, 0.0).astype(BF16)
    nh, nl = _split2(nblk)
    return nblk, _dot(tril, nh) + _dot(tril, nl)


def _dest_kernel(cnt_ref, eid_ref, rank_ref, dest_ref):
    nblk, end = _block_offsets(cnt_ref[...])
    off = (end - nblk) * float(BLK)
    eid = eid_ref[...]
    tt = eid.shape[1]
    row = lax.broadcasted_iota(jnp.int32, (off.shape[0], tt), 0)
    table = jnp.concatenate([off] * (tt // LANES), axis=1)
    dest_ref[...] = _lookup(row, eid, table).astype(jnp.int32) + rank_ref[...]


def _destinations(cnt, eid, rank, tt=512):
    n = eid.shape[1]
    tt = min(tt, n)
    blk = pl.BlockSpec((TOP_K, tt), lambda i: (0, i))
    return pl.pallas_call(
        _dest_kernel,
        grid=(n // tt,),
        in_specs=[pl.BlockSpec(cnt.shape, lambda i: (0, 0)), blk, blk],
        out_specs=blk,
        out_shape=jax.ShapeDtypeStruct((TOP_K, n), jnp.int32),
        compiler_params=_params("arbitrary"),
        name="expert_destinations",
    )(cnt, eid, rank)


def _meta_kernel(cnt_ref, meta_ref, emeta_ref, *, nbp):
    cnt = cnt_ref[...]
    e = cnt.shape[0]
    nblk, end = _block_offsets(cnt)
    rep = lambda a, w: jnp.concatenate([a] * (w // LANES), axis=1)
    b = lax.broadcasted_iota(jnp.int32, (e, nbp), 1).astype(F32)
    blk_e = jnp.minimum(jnp.sum(jnp.where(rep(end, nbp) <= b, 1.0, 0.0), axis=0, keepdims=True), float(e - 1))
    row = lax.broadcasted_iota(jnp.int32, (e, nbp), 0).astype(F32)
    mine = row == blk_e
    left = rep(cnt + (end - nblk) * float(BLK), nbp) - b * float(BLK)
    nvalid = jnp.clip(jnp.sum(jnp.where(mine, left, 0.0), axis=0, keepdims=True), 0.0, float(BLK))
    nused = jnp.max(rep(end, nbp), axis=0, keepdims=True)
    meta_ref[...] = jnp.concatenate([blk_e, nvalid, nused, jnp.zeros((5, nbp), F32)], axis=0).astype(jnp.int32)
    eye = lax.broadcasted_iota(jnp.int32, (e, e), 0) == lax.broadcasted_iota(jnp.int32, (e, e), 1)
    to_row = lambda a: jnp.sum(jnp.where(eye, rep(a, e), 0.0), axis=0, keepdims=True)
    emeta_ref[...] = jnp.concatenate([to_row(end), to_row(cnt), jnp.zeros((6, e), F32)], axis=0).astype(jnp.int32)


def _block_meta(cnt, nb):
    e = cnt.shape[0]
    nbp = -(-nb // LANES) * LANES
    return pl.pallas_call(
        functools.partial(_meta_kernel, nbp=nbp),
        out_shape=[jax.ShapeDtypeStruct((8, nbp), jnp.int32), jax.ShapeDtypeStruct((8, e), jnp.int32)],
        compiler_params=pltpu.CompilerParams(vmem_limit_bytes=VMEM_LIMIT),
        name="expert_block_meta",
    )(cnt)


def _row_copy(src_ref, s, dst_ref, t, sem):
    return pltpu.make_async_copy(src_ref.at[pl.ds(s, 1), :], dst_ref.at[pl.ds(t, 1), :], sem)


def _dispatch_kernel(eend_ref, ecnt_ref, dest_ref, h_ref, xs_ref, zero_ref, zsem, sem, *, e, tt):
    @pl.when(pl.program_id(0) == 0)
    def _():
        zero_ref[...] = jnp.zeros_like(zero_ref)

        def tail(ex):
            start = pl.multiple_of((eend_ref[ex] - 1) * BLK, BLK)
            return pltpu.make_async_copy(zero_ref, xs_ref.at[pl.ds(start, BLK), :], zsem)

        def issue(ex, carry):
            @pl.when(ecnt_ref[ex] > 0)
            def _():
                tail(ex).start()
            return carry

        def drain(ex, carry):
            @pl.when(ecnt_ref[ex] > 0)
            def _():
                tail(ex).wait()
            return carry

        lax.fori_loop(0, e, issue, 0)
        lax.fori_loop(0, e, drain, 0)

    def issue_rows(t, carry):
        for kk in range(TOP_K):
            _row_copy(h_ref, t, xs_ref, dest_ref[kk, t], sem).start(priority=kk % 2)
        return carry

    def drain_rows(t, carry):
        for kk in range(TOP_K):
            _row_copy(h_ref, t, xs_ref, dest_ref[kk, t], sem).wait()
        return carry

    lax.fori_loop(0, tt, issue_rows, 0)
    lax.fori_loop(0, tt, drain_rows, 0)


def _dispatch(eend, ecnt, dest, h2, nb, tt=256):
    n, d = h2.shape
    e = eend.shape[0]
    tt = min(tt, n)
    return pl.pallas_call(
        functools.partial(_dispatch_kernel, e=e, tt=tt),
        grid_spec=pltpu.PrefetchScalarGridSpec(
            num_scalar_prefetch=2,
            grid=(n // tt,),
            in_specs=[pl.BlockSpec((TOP_K, tt), lambda i, *_: (0, i), memory_space=pltpu.SMEM),
                      pl.BlockSpec((tt, d), lambda i, *_: (i, 0))],
            out_specs=pl.BlockSpec(memory_space=pl.ANY),
            scratch_shapes=[pltpu.VMEM((BLK, d), h2.dtype), pltpu.SemaphoreType.DMA(()),
                            pltpu.SemaphoreType.DMA(())],
        ),
        out_shape=jax.ShapeDtypeStruct((nb * BLK, d), h2.dtype),
        compiler_params=_params("arbitrary"),
        name="moe_dispatch",
    )(eend, ecnt, dest, h2)


def _experts_kernel(be_ref, nv_ref, nu_ref, x_ref, wg_ref, wu_ref, wd_ref, o_ref, wgb, wub, wdb):
    b = pl.program_id(0)
    used = b < nu_ref[0]

    @pl.when(used)
    def _():
        prev = be_ref[jnp.maximum(b - 1, 0)]

        @pl.when(jnp.logical_or(b == 0, be_ref[b] != prev))
        def _():
            wgb[...] = wg_ref[...].astype(BF16)
            wub[...] = wu_ref[...].astype(BF16)
            wdb[...] = wd_ref[...].astype(BF16)

        row = lax.broadcasted_iota(jnp.int32, (x_ref.shape[0], 1), 0)
        lo, hi = _unpack_halves(jnp.where(row < nv_ref[b], x_ref[...], jnp.uint32(0)))
        x = jnp.concatenate([lo.astype(BF16), hi.astype(BF16)], axis=1)
        act = _silu(_dot(x, wgb[...])) * _dot(x, wub[...])
        o_ref[...] = _pack_halves(_dot(act.astype(BF16), wdb[...]))

    @pl.when(jnp.logical_not(used))
    def _():
        o_ref[...] = jnp.zeros_like(o_ref)


def _experts(blk_e, nvalid, nused, xs, wg, wu, wd):
    p, dp = xs.shape
    nb = p // BLK
    d, de = wg.shape[1], wg.shape[2]
    last = lambda b, be, nv, nu: jnp.minimum(b, nu[0] - 1)
    return pl.pallas_call(
        _experts_kernel,
        grid_spec=pltpu.PrefetchScalarGridSpec(
            num_scalar_prefetch=3,
            grid=(nb,),
            in_specs=[pl.BlockSpec((BLK, dp), lambda b, be, nv, nu: (last(b, be, nv, nu), 0)),
                      pl.BlockSpec((None, d, de), lambda b, be, nv, nu: (be[last(b, be, nv, nu)], 0, 0)),
                      pl.BlockSpec((None, d, de), lambda b, be, nv, nu: (be[last(b, be, nv, nu)], 0, 0)),
                      pl.BlockSpec((None, de, d), lambda b, be, nv, nu: (be[last(b, be, nv, nu)], 0, 0))],
            out_specs=pl.BlockSpec((BLK, dp), lambda b, be, nv, nu: (b, 0)),
            scratch_shapes=[pltpu.VMEM((d, de), BF16), pltpu.VMEM((d, de), BF16), pltpu.VMEM((de, d), BF16)],
        ),
        out_shape=jax.ShapeDtypeStruct((p, dp), U32),
        compiler_params=_params("arbitrary"),
        name="moe_experts",
    )(blk_e, nvalid, nused, xs, wg, wu, wd)


def _combine_kernel(dest_ref, w_ref, x1_ref, h2_ref, mod_ref, ys_ref, sg_ref, su_ref, sd_ref, gf_ref,
                    sel_ref, o_ref, buf, sem, *, tt):
    def issue(t, carry):
        for kk in range(TOP_K):
            _row_copy(ys_ref, dest_ref[kk, t], buf.at[kk], t, sem).start(priority=kk % 2)
        return carry

    def drain(t, carry):
        for kk in range(TOP_K):
            _row_copy(ys_ref, dest_ref[kk, t], buf.at[kk], t, sem).wait()
        return carry

    lax.fori_loop(0, tt, issue, 0)
    lo, hi = _unpack_halves(h2_ref[...])
    hb = jnp.concatenate([lo.astype(BF16), hi.astype(BF16)], axis=1)
    act = _silu(_dot(hb, sg_ref[...])) * _dot(hb, su_ref[...])
    ffn = _dot(act.astype(BF16), sd_ref[...])
    lax.fori_loop(0, tt, drain, 0)
    wh, wl = _split2(w_ref[...])
    acc_lo = jnp.zeros(lo.shape, F32)
    acc_hi = jnp.zeros(lo.shape, F32)
    for kk in range(TOP_K):
        sel = sel_ref[kk]
        wk = _tn(wh, sel) + _tn(wl, sel)
        lo, hi = _unpack_halves(buf[kk])
        acc_lo = acc_lo + lo * wk
        acc_hi = acc_hi + hi * wk
    ffn = ffn + jnp.concatenate([acc_lo, acc_hi], axis=1)
    xo = x1_ref[...] + mod_ref[5:6, :] * ffn
    ms = jnp.mean(xo * xo, axis=-1, keepdims=True)
    o_ref[...] = xo * lax.rsqrt(ms + NORM_EPS) * gf_ref[...]


def _combine(dest, wsel, x1, h2, mod, ys, sh_wg, sh_wu, sh_wd, normf_g, tt=256):
    bsz, seq, d = x1.shape
    n = bsz * seq
    tt = min(tt, seq)
    per = seq // tt
    dp = h2.shape[1]
    sel = jnp.asarray(np.broadcast_to(np.eye(TOP_K)[:, :, None], (TOP_K, TOP_K, dp)), BF16)
    consts = [sh_wg.astype(BF16), sh_wu.astype(BF16), sh_wd.astype(BF16), normf_g.reshape(1, d), sel]
    const = lambda a: pl.BlockSpec(a.shape, lambda i: (0,) * a.ndim)
    rows = pl.BlockSpec((tt, d), lambda i: (i, 0))
    packed_rows = pl.BlockSpec((tt, dp), lambda i: (i, 0))
    return pl.pallas_call(
        functools.partial(_combine_kernel, tt=tt),
        grid=(n // tt,),
        in_specs=[pl.BlockSpec((TOP_K, tt), lambda i: (0, i), memory_space=pltpu.SMEM),
                  pl.BlockSpec((TOP_K, tt), lambda i: (0, i)),
                  rows, packed_rows,
                  pl.BlockSpec((None,) + mod.shape[1:], lambda i: (i // per, 0, 0)),
                  pl.BlockSpec(memory_space=pl.ANY)] + [const(a) for a in consts],
        out_specs=rows,
        out_shape=jax.ShapeDtypeStruct((n, d), F32),
        scratch_shapes=[pltpu.VMEM((TOP_K, tt, dp), U32), pltpu.SemaphoreType.DMA(())],
        compiler_params=_params("arbitrary"),
        name="moe_combine",
    )(dest, wsel, x1.reshape(n, d), h2, mod, ys, *consts)


def _moe(x1, h2, mod, eid, wsel, exp_wg, exp_wu, exp_wd, sh_wg, sh_wu, sh_wd, normf_g):
    n = h2.shape[0]
    e = exp_wg.shape[0]
    nb = (n * TOP_K + e * (BLK - 1)) // BLK
    rank, cnt = _expert_ranks(eid, e)
    dest = _destinations(cnt, eid, rank)
    meta, emeta = _block_meta(cnt, nb)
    xs = _dispatch(emeta[0], emeta[1], dest, h2, nb)
    ys = _experts(meta[0, :nb], meta[1, :nb], meta[2, :1], xs, exp_wg, exp_wu, exp_wd)
    return _combine(dest, wsel, x1, h2, mod, ys, sh_wg, sh_wu, sh_wd, normf_g)


def kernel(x, c, norm1_g, norm2_g, normf_g, w_ada, b_ada, w_in, w_out, hy_conv_w, hy_conv_b, hy_pos_w1, hy_pos_b1, hy_pos_w2, hy_pos_b2, hy_pos_w3, hy_sin_freq, hy_skip, rw_mu, rw_w0, rw_w_up, rw_a0, rw_a_up, rw_g_up, rw_k_k, rw_k_a, rw_r_k, rw_ln_w, rw_ln_b, router_w, router_bias, exp_w_gate, exp_w_up, exp_w_down, sh_w_gate, sh_w_up, sh_w_down):
    bsz, seq, d = x.shape
    depth = w_ada.shape[0]
    assert depth == 1, "the final norm is fused into the last kernel of a single layer"
    for l in range(depth):
        mod = _modulation(c, w_ada[l], b_ada[l]).reshape(bsz, -1, d)
        uhy, r, k, v, kk, lw, a, g, bonus = _projection(
            x, mod, norm1_g[l], w_in[l], hy_conv_w[l], hy_conv_b[l], rw_mu[l], rw_w0[l], rw_w_up[l],
            rw_a0[l], rw_a_up[l], rw_g_up[l], rw_k_k[l], rw_k_a[l], rw_r_k[l])
        k2, ss = _hyena_filters(seq, hy_pos_w1[l], hy_pos_b1[l], hy_pos_w2[l], hy_pos_b2[l],
                                hy_pos_w3[l], hy_sin_freq[l])
        khat = _filter_spectrum(k2, ss, seq)
        z, z_col = uhy, 0
        for order in range(HYENA_ORDER):
            z = _long_conv_gate(z, z_col, uhy, (order + 1) * D_HYENA, khat, hy_skip[l], order)
            z_col = 0
        o_f, o_b = _wkv(r, k, v, kk, lw, a, rw_k_a[l])
        x1, h2, eid, wsel = _mix_out(x, mod, z, o_f, o_b, g, bonus, rw_ln_w[l], rw_ln_b[l], w_out[l],
                                     norm2_g[l], router_w[l], router_bias[l])
        x = _moe(x1, h2, mod, eid, wsel, exp_w_gate[l], exp_w_up[l], exp_w_down[l],
                 sh_w_gate[l], sh_w_up[l], sh_w_down[l], normf_g)
        x = x.reshape(bsz, seq, d)
    return x
```

```python
import functools
import math

import jax
import jax.numpy as jnp
import numpy as np
from jax import lax
from jax.experimental import pallas as pl
from jax.experimental.pallas import tpu as pltpu
from jax.experimental.pallas import tpu_sc as plsc

F32 = jnp.float32
BF16 = jnp.bfloat16

LANES = 128
MXU_DIM = 256
VMEM_LIMIT = 56 * 1024 * 1024

D_HYENA = 512
D_RWKV = 512
HEAD = 64
N_HEADS = D_RWKV // HEAD
HYENA_ORDER = 2
FILTER_BANDS = 16
DECAY_TARGET = 1e-2
FAST_DECAY_PCT = 0.3
SLOW_DECAY_PCT = 1.5
FILTER_NORM_EPS = 1e-6
DECAY_LORA = 32
ICLR_LORA = 32
GATE_LORA = 96
GN_EPS = 64e-5
NORM_EPS = 1e-6
N_EXPERTS = 256
TOP_K = 8
N_GROUPS = 8
TOPK_GROUPS = 4
ROUTE_SCALE = 2.5
D_EXPERT = 256


def _params(*sem):
    return pltpu.CompilerParams(dimension_semantics=sem, vmem_limit_bytes=VMEM_LIMIT)


def _split2(a):
    hi = a.astype(BF16)
    lo = (a - hi.astype(F32)).astype(BF16)
    return hi, lo


def _dot(a, b):
    return jnp.dot(a, b, preferred_element_type=F32)


def _dot3(a, b):
    ah, al = _split2(a)
    bh, bl = _split2(b)
    return _dot(ah, bh) + (_dot(ah, bl) + _dot(al, bh))


def _dot_exact_rhs(a, b_bf16):
    ah, al = _split2(a)
    return _dot(ah, b_bf16) + _dot(al, b_bf16)


def _silu(x):
    return x * jax.nn.sigmoid(x)


U32 = jnp.uint32


def _pack_halves(x):
    w = x.shape[1] // 2
    rounded = x.astype(BF16).astype(F32)
    bits = lax.bitcast_convert_type(rounded, U32)
    return (bits[:, w:] & jnp.uint32(0xFFFF0000)) | (bits[:, :w] >> 16)


def _unpack_halves(p):
    lo = lax.bitcast_convert_type(p << 16, F32)
    hi = lax.bitcast_convert_type(p & jnp.uint32(0xFFFF0000), F32)
    return lo, hi


def _mod_kernel(c_ref, w_ref, b_ref, o_ref):
    o_ref[...] = _dot3(_silu(c_ref[...]), w_ref[...]) + b_ref[...]


def _modulation(c, w_ada, b_ada):
    bsz, d = c.shape
    n = w_ada.shape[1]
    blk = 1024
    return pl.pallas_call(
        _mod_kernel,
        grid=(n // blk,),
        in_specs=[
            pl.BlockSpec((bsz, d), lambda j: (0, 0)),
            pl.BlockSpec((d, blk), lambda j: (0, j)),
            pl.BlockSpec((1, blk), lambda j: (0, j)),
        ],
        out_specs=pl.BlockSpec((bsz, blk), lambda j: (0, j)),
        out_shape=jax.ShapeDtypeStruct((bsz, n), F32),
        compiler_params=_params("arbitrary"),
        name="adaln_mod",
    )(c, w_ada, b_ada.reshape(1, n))


def _filter_kernel(band_ref, w1_ref, b1_ref, w2_ref, b2_ref, w3_ref, freq_ref, delta_ref,
                   k_ref, ss_ref, *, seq, rows):
    half = pl.program_id(0)
    i = pl.program_id(1)
    r = lax.broadcasted_iota(jnp.int32, (rows, LANES), 0) + i * rows
    pos = jnp.where(half == 0, r, seq - r).astype(F32)
    tt = pos / float(max(seq - 1, 1))
    lane = lax.broadcasted_iota(jnp.int32, (rows, LANES), 1)
    ang = pos * band_ref[...]
    feats = jnp.where(lane == 0, tt,
                      jnp.where(lane <= FILTER_BANDS, jnp.cos(ang),
                                jnp.where(lane <= 2 * FILTER_BANDS, -jnp.sin(ang), 0.0)))
    freq = freq_ref[...]
    hdn = jnp.sin(freq * (_dot3(feats, w1_ref[...]) + b1_ref[...]))
    for j in range(w2_ref.shape[0]):
        hdn = jnp.sin(freq * (_dot3(hdn, w2_ref[j]) + b2_ref[j]))
    filt = _dot3(hdn, w3_ref[...])
    filt = filt * jnp.exp(-tt[:, :1] * delta_ref[...])
    valid = jnp.logical_or(half == 0, r[:, :1] > 0)
    filt = jnp.where(valid, filt, 0.0)
    k_ref[...] = filt

    @pl.when(jnp.logical_and(half == 0, i == 0))
    def _():
        ss_ref[...] = jnp.zeros_like(ss_ref)

    ss_ref[...] += jnp.broadcast_to(jnp.sum(filt * filt, axis=0, keepdims=True), ss_ref.shape)


def _hyena_filters(seq, pw1, pb1, pw2, pb2, pw3, freq):
    width = pw1.shape[1]
    ncol = HYENA_ORDER * D_HYENA
    rows = min(seq, 512)
    bands = np.zeros((1, LANES), np.float64)
    lin = np.linspace(1e-4, FILTER_BANDS - 1, FILTER_BANDS)
    bands[0, 1:1 + FILTER_BANDS] = lin
    bands[0, 1 + FILTER_BANDS:1 + 2 * FILTER_BANDS] = lin
    bands = jnp.asarray(bands * (2.0 * math.pi / seq), F32)
    deltas = np.abs(np.linspace(math.log(DECAY_TARGET) / SLOW_DECAY_PCT,
                                math.log(DECAY_TARGET) / FAST_DECAY_PCT, D_HYENA))
    deltas = jnp.asarray(np.tile(deltas, HYENA_ORDER)[None], F32)
    w1 = jnp.zeros((LANES, width), F32).at[:pw1.shape[0]].set(pw1)
    w3 = pw3.reshape(width, HYENA_ORDER, 2, D_HYENA).transpose(2, 0, 1, 3).reshape(2, width, ncol)
    nt = seq // rows
    full = lambda *shape: pl.BlockSpec(shape, lambda h, i: (0,) * len(shape))
    return pl.pallas_call(
        functools.partial(_filter_kernel, seq=seq, rows=rows),
        grid=(2, nt),
        in_specs=[
            full(1, LANES), full(LANES, width), full(1, width),
            full(pw2.shape[0], width, width), full(pw2.shape[0], 1, width),
            pl.BlockSpec((None, width, ncol), lambda h, i: (h, 0, 0)),
            full(1, width), full(1, ncol),
        ],
        out_specs=[
            pl.BlockSpec((rows, ncol), lambda h, i: (h * nt + i, 0)),
            pl.BlockSpec((8, ncol), lambda h, i: (0, 0)),
        ],
        out_shape=[jax.ShapeDtypeStruct((2 * seq, ncol), F32),
                   jax.ShapeDtypeStruct((8, ncol), F32)],
        compiler_params=_params("arbitrary", "arbitrary"),
        name="hyena_filters",
    )(bands, w1, pb1.reshape(1, width), pw2, pb2.reshape(pw2.shape[0], 1, width), w3,
      freq.reshape(1, width), deltas)


N1 = LANES
UNROLL = 8


def _dft_tables(seq):
    m = 2 * seq
    n2 = m // N1
    n2h = n2 // 2
    n1 = np.arange(N1)[:, None, None]
    f2 = np.arange(n2)[None, :, None]
    k2 = np.arange(n2)[None, None, :]
    th = 2.0 * np.pi * (n1 * f2 / m + (k2 * f2 % n2) / n2)
    fwd_a = np.concatenate([np.cos(th), -np.sin(th)], axis=1)
    tht = np.transpose(th, (0, 2, 1))
    inv_a = np.concatenate([np.cos(tht), -np.sin(tht)], axis=2)[:, :n2h] / m
    a = np.arange(N1)
    ph = 2.0 * np.pi * np.outer(a, a) / N1
    c, s = np.cos(ph), np.sin(ph)
    fwd_b = np.block([[c, s], [-s, c]])
    inv_b = np.block([[c, -s], [s, c]])
    cast = lambda t: jnp.asarray(t, BF16)
    return cast(fwd_a), cast(fwd_a[:, :, :n2h]), cast(inv_a), cast(fwd_b), cast(inv_b), n2, n2h


def _stage_a_fwd(x_ref, wa_ref, y_ref, n2, scale=None):
    def body(i, carry):
        trips = [i * UNROLL + j for j in range(UNROLL)]
        xs = [x_ref[pl.ds(n1, wa_ref.shape[2], stride=N1), :] for n1 in trips]
        if scale is not None:
            xs = [x * scale for x in xs]
        prods = [_dot(wa_ref[n1], x.astype(BF16)) for n1, x in zip(trips, xs)]
        for n1, a in zip(trips, prods):
            y_ref[pl.ds(n1, n2, stride=2 * N1), :] = a[:n2]
            y_ref[pl.ds(N1 + n1, n2, stride=2 * N1), :] = a[n2:]
        return carry
    lax.fori_loop(0, N1 // UNROLL, body, 0)


def _filter_fft_kernel(k_ref, ss_ref, wa_ref, fb_ref, o_ref, y_ref, *, n2):
    scale = lax.rsqrt(ss_ref[0:1, :] + FILTER_NORM_EPS)
    _stage_a_fwd(k_ref, wa_ref, y_ref, n2, scale=scale)

    unr = min(UNROLL, n2)

    def body(i, carry):
        trips = [i * unr + j for j in range(unr)]
        ys = [y_ref[pl.ds(pl.multiple_of(f2 * 2 * N1, 2 * N1), 2 * N1), :].astype(BF16) for f2 in trips]
        for f2, y in zip(trips, ys):
            o_ref[f2] = _dot(fb_ref[...], y)
        return carry
    lax.fori_loop(0, n2 // unr, body, 0)


def _filter_spectrum(k2, ss, seq):
    fwd_a, _, _, fwd_b, _, n2, _ = _dft_tables(seq)
    ncol = k2.shape[1]
    nblk = ncol // LANES
    return pl.pallas_call(
        functools.partial(_filter_fft_kernel, n2=n2),
        grid=(nblk,),
        in_specs=[
            pl.BlockSpec((2 * seq, LANES), lambda c: (0, c)),
            pl.BlockSpec((8, LANES), lambda c: (0, c)),
            pl.BlockSpec(fwd_a.shape, lambda c: (0, 0, 0)),
            pl.BlockSpec(fwd_b.shape, lambda c: (0, 0)),
        ],
        out_specs=pl.BlockSpec((None, n2, 2 * N1, LANES), lambda c: (c, 0, 0, 0)),
        out_shape=jax.ShapeDtypeStruct((nblk, n2, 2 * N1, LANES), F32),
        scratch_shapes=[pltpu.VMEM((n2 * 2 * N1, LANES), F32)],
        compiler_params=_params("arbitrary"),
        name="hyena_filter_fft",
    )(k2, ss, fwd_a, fwd_b)


def _conv_kernel(u_ref, g_ref, kh_ref, skip_ref, wa_ref, va_ref, fb_ref, ib_ref, o_ref, y_ref,
                 *, n2, n2h):
    _stage_a_fwd(u_ref, wa_ref, y_ref, n2)

    unr = min(UNROLL, n2)

    def mid(i, carry):
        trips = [i * unr + j for j in range(unr)]
        offs = [pl.multiple_of(f2 * 2 * N1, 2 * N1) for f2 in trips]
        zs = [_dot(fb_ref[...], y_ref[pl.ds(off, 2 * N1), :].astype(BF16)) for off in offs]
        ps = []
        for f2, z in zip(trips, zs):
            zr, zi = z[:N1], z[N1:]
            kh = kh_ref[f2]
            kr, ki = kh[:N1], kh[N1:]
            ps.append(jnp.concatenate([zr * kr - zi * ki, zr * ki + zi * kr], axis=0).astype(BF16))
        gs = [_dot(ib_ref[...], p) for p in ps]
        for off, g in zip(offs, gs):
            y_ref[pl.ds(off, 2 * N1), :] = g
        return carry
    lax.fori_loop(0, n2 // unr, mid, 0)

    skip = skip_ref[...]

    def last(i, carry):
        trips = [i * UNROLL + j for j in range(UNROLL)]
        gs = [jnp.concatenate([y_ref[pl.ds(n1, n2, stride=2 * N1), :],
                               y_ref[pl.ds(N1 + n1, n2, stride=2 * N1), :]], axis=0).astype(BF16)
              for n1 in trips]
        convs = [_dot(va_ref[n1], g) for n1, g in zip(trips, gs)]
        for n1, conv in zip(trips, convs):
            rows = pl.ds(n1, n2h, stride=N1)
            u = u_ref[rows, :]
            o_ref[rows, :] = g_ref[rows, :] * (conv + u * skip)
        return carry
    lax.fori_loop(0, N1 // UNROLL, last, 0)


def _long_conv_gate(u, u_col, gate, gate_col, khat, skip, order):
    bsz, seq, _ = u.shape
    ch = D_HYENA
    _, fwd_a, inv_a, fwd_b, inv_b, n2, n2h = _dft_tables(seq)
    nblk = ch // LANES
    const = lambda a: pl.BlockSpec(a.shape, lambda c, b: (0,) * a.ndim)
    at = lambda col: pl.BlockSpec((None, seq, LANES), lambda c, b: (b, 0, col // LANES + c))
    data = at(0)
    return pl.pallas_call(
        functools.partial(_conv_kernel, n2=n2, n2h=n2h),
        grid=(nblk, bsz),
        in_specs=[
            at(u_col), at(gate_col),
            pl.BlockSpec((None, n2, 2 * N1, LANES), lambda c, b: (order * nblk + c, 0, 0, 0)),
            pl.BlockSpec((1, LANES), lambda c, b: (0, c)),
            const(fwd_a), const(inv_a), const(fwd_b), const(inv_b),
        ],
        out_specs=data,
        out_shape=jax.ShapeDtypeStruct((bsz, seq, ch), F32),
        scratch_shapes=[pltpu.VMEM((n2 * 2 * N1, LANES), F32)],
        compiler_params=_params("arbitrary", "arbitrary"),
        name=f"hyena_conv{order}",
    )(u, gate, khat, skip[order].reshape(1, ch), fwd_a, inv_a, fwd_b, inv_b)


HALO = 8


def _shift_rows(p, k):
    return pltpu.roll(p, k % p.shape[0], axis=0)


def _proj_kernel(xp_ref, x_ref, xn_ref, mod_ref, g1_ref, why_ref, wrkv_ref, wlora_ref,
                 cw_ref, cb_ref, murkv_ref, mulora_ref, w0_ref, a0_ref, wwa_ref, gup_ref,
                 kk_ref, ka_ref, rk_ref, ones_ref,
                 uhy_ref, r_ref, k_ref, v_ref, kkn_ref, lw_ref, a_ref, g_ref, bonus_ref,
                 *, tt, nt):
    i = pl.program_id(1)
    xe = jnp.concatenate([xp_ref[...], x_ref[...], xn_ref[...]], axis=0)
    ms = jnp.mean(xe * xe, axis=-1, keepdims=True)
    h = xe * lax.rsqrt(ms + NORM_EPS) * g1_ref[...]
    h = h * (1.0 + mod_ref[1:2, :]) + mod_ref[0:1, :]
    row = lax.broadcasted_iota(jnp.int32, (tt + 2 * HALO, 1), 0)
    inside = jnp.logical_and(jnp.logical_or(row >= HALO, i > 0),
                             jnp.logical_or(row < tt + HALO, i < nt - 1))
    hb = jnp.where(inside, h, 0.0).astype(BF16)
    mid = slice(HALO, tt + HALO)

    p = _dot(hb, why_ref[...])
    u = (_shift_rows(p, 1) * cw_ref[0:1, :] + p * cw_ref[1:2, :]
         + _shift_rows(p, -1) * cw_ref[2:3, :] + cb_ref[...])
    uhy_ref[...] = u[mid]

    p = _dot(hb, wrkv_ref[...])
    p = p + murkv_ref[...] * (0.5 * (_shift_rows(p, 1) + _shift_rows(p, -1)) - p)
    p = p[mid]
    c = D_RWKV
    r, k, v = p[:, :c], p[:, c:2 * c], p[:, 2 * c:]
    r_ref[...] = r
    k_ref[...] = k
    v_ref[...] = v

    q = _dot(hb, wlora_ref[...])
    q = q + mulora_ref[...] * (0.5 * (_shift_rows(q, 1) + _shift_rows(q, -1)) - q)
    q = q[mid]
    wa = q[:, :LANES]
    lane = lax.broadcasted_iota(jnp.int32, wa.shape, 1)
    wa = jnp.where(lane < 2 * DECAY_LORA, jnp.tanh(wa), wa)
    up = _dot3(wa, wwa_ref[...])
    z = -(w0_ref[...] + up[:, :2 * c])
    softplus = jnp.maximum(z, 0.0) + jnp.log1p(jnp.exp(-jnp.abs(z)))
    lw_ref[...] = -jnp.exp(-softplus - 0.5)
    a = jax.nn.sigmoid(a0_ref[...] + up[:, 2 * c:])
    a_ref[...] = a
    g_ref[...] = _dot3(jax.nn.sigmoid(q[:, LANES:]), gup_ref[...])

    ones = ones_ref[...]
    kk = k * kk_ref[...]
    nrm = jnp.sqrt(_dot_exact_rhs(kk * kk, ones))
    kkn_ref[...] = kk / jnp.maximum(nrm, 1e-12)
    ka = ka_ref[...]
    ksum = k * (2.0 + (a[:, :c] + a[:, c:] - 2.0) * ka)
    bonus_ref[...] = _dot_exact_rhs(r * ksum * rk_ref[...], ones) * v


def _head_ones():
    hid = np.arange(D_RWKV) // HEAD
    return jnp.asarray(hid[:, None] == hid[None, :], BF16)


def _projection(x, mod, norm1_g, w_in, hy_conv_w, hy_conv_b, rw_mu, rw_w0, rw_w_up, rw_a0,
                rw_a_up, rw_g_up, rw_k_k, rw_k_a, rw_r_k, tt=256):
    bsz, seq, d = x.shape
    tt = min(tt, seq)
    nt = seq // tt
    c = D_RWKV
    hy = (HYENA_ORDER + 1) * D_HYENA
    nlora = 2 * LANES
    w_hy = w_in[:, :hy].astype(BF16)
    w_rkv = w_in[:, hy:hy + 3 * c].astype(BF16)
    w_lora = jnp.zeros((d, nlora), F32).at[:, :w_in.shape[1] - hy - 3 * c].set(w_in[:, hy + 3 * c:]).astype(BF16)
    mu_rkv = rw_mu[:3 * c].reshape(1, 3 * c)
    mu_lora = jnp.zeros((1, nlora), F32).at[0, :rw_mu.shape[0] - 3 * c].set(rw_mu[3 * c:])
    wwa = jnp.zeros((LANES, 4 * c), F32)
    for dd in range(2):
        wwa = wwa.at[dd * DECAY_LORA:(dd + 1) * DECAY_LORA, dd * c:(dd + 1) * c].set(rw_w_up[dd])
        wwa = wwa.at[2 * DECAY_LORA + dd * ICLR_LORA:2 * DECAY_LORA + (dd + 1) * ICLR_LORA,
                     2 * c + dd * c:2 * c + (dd + 1) * c].set(rw_a_up[dd])
    gup = jnp.zeros((LANES, c), F32).at[:GATE_LORA].set(rw_g_up)
    row = lambda a: a.reshape(1, -1)

    nb8 = seq // HALO
    tb = tt // HALO
    const = lambda a: pl.BlockSpec(a.shape, lambda b, i: (0,) * a.ndim)
    tile = lambda w: pl.BlockSpec((None, tt, w), lambda b, i: (b, i, 0))
    ins = [
        (x, pl.BlockSpec((None, HALO, d), lambda b, i: (b, jnp.maximum(i * tb - 1, 0), 0))),
        (x, pl.BlockSpec((None, tt, d), lambda b, i: (b, i, 0))),
        (x, pl.BlockSpec((None, HALO, d), lambda b, i: (b, jnp.minimum((i + 1) * tb, nb8 - 1), 0))),
        (mod, pl.BlockSpec((None,) + mod.shape[1:], lambda b, i: (b, 0, 0))),
    ]
    consts = [row(norm1_g), w_hy, w_rkv, w_lora, hy_conv_w, row(hy_conv_b), mu_rkv, mu_lora,
              row(rw_w0), row(rw_a0), wwa, gup, row(rw_k_k), row(rw_k_a), row(rw_r_k), _head_ones()]
    ins += [(a, const(a)) for a in consts]
    widths = [hy, c, c, c, c, 2 * c, 2 * c, c, c]
    return pl.pallas_call(
        functools.partial(_proj_kernel, tt=tt, nt=nt),
        grid=(bsz, nt),
        in_specs=[s for _, s in ins],
        out_specs=[tile(w) for w in widths],
        out_shape=[jax.ShapeDtypeStruct((bsz, seq, w), F32) for w in widths],
        compiler_params=_params("arbitrary", "arbitrary"),
        name="input_projection",
    )(*[a for a, _ in ins])


CHUNK = HEAD
GROUP = MXU_DIM // HEAD


def _nt(a, b):
    return lax.dot_general(a, b, (((1,), (1,)), ((), ())), preferred_element_type=F32)


def _tn(a, b):
    return lax.dot_general(a, b, (((0,), (0,)), ((), ())), preferred_element_type=F32)


def _wkv_direction(r, k, v, kk, lw, a, ka, s_ref, reverse):
    c = CHUNK
    ti = lax.broadcasted_iota(jnp.int32, (c, c), 0)
    si = lax.broadcasted_iota(jnp.int32, (c, c), 1)
    tri = (si >= ti) if reverse else (si <= ti)
    cum = _dot_exact_rhs_lhs(jnp.where(tri, 1.0, 0.0).astype(BF16), lw)
    tot = jnp.sum(lw, axis=0, keepdims=True)
    w_incl = jnp.exp(cum)
    w_prev = jnp.exp(cum - lw)
    w_inv = jnp.exp(-cum)
    w_end = jnp.exp(tot - cum)
    w_tot = jnp.exp(tot)
    kd = k * (1.0 + (a - 1.0) * ka)
    b = kk * a
    a_w = -kk * w_prev
    r_w = r * w_incl
    b_w = b * w_inv
    k_w = kd * w_inv
    b_e = b * w_end
    k_e = kd * w_end

    m = MXU_DIM
    ri = lax.broadcasted_iota(jnp.int32, (m, m), 0)
    ci = lax.broadcasted_iota(jnp.int32, (m, m), 1)
    head_mask = (ri // HEAD) == (ci // HEAD)
    tl = lax.broadcasted_iota(jnp.int32, (c, m), 0)
    sl = lax.broadcasted_iota(jnp.int32, (c, m), 1) % c
    strict = (sl > tl) if reverse else (sl < tl)
    incl = (sl >= tl) if reverse else (sl <= tl)
    eye = jnp.where(sl == tl, 1.0, 0.0)
    both = lambda top, bot: jnp.concatenate([top, bot], axis=0)

    def stack(xg):
        xb = xg.astype(BF16)
        return jnp.where(head_mask, jnp.concatenate([xb] * GROUP, axis=0), jnp.zeros((), BF16))

    streams = []
    for g in range(D_RWKV // m):
        sl_g = slice(g * m, (g + 1) * m)
        streams.append(dict(
            ar=both(a_w[:, sl_g], r_w[:, sl_g]).astype(BF16),
            b_st=stack(b_w[:, sl_g]), k_st=stack(k_w[:, sl_g]), v_st=stack(v[:, sl_g]),
            v=v[:, sl_g], bk=both(b_e[:, sl_g], k_e[:, sl_g]).astype(BF16),
            w_tot=w_tot[:, sl_g], s_ref=s_ref.at[g],
            strict=strict, incl=incl, eye=eye, head_mask=head_mask, stack=stack))
    return streams


def _wkv_streams_step(streams):
    c = CHUNK
    both = lambda top, bot: jnp.concatenate([top, bot], axis=0)
    for st in streams:
        st["s"] = st["s_ref"][...]
        st["xb"] = _nt(st["ar"], st["b_st"])
        st["xk"] = _nt(st["ar"], st["k_st"])
        st["xs"] = _nt(st["ar"], st["s"].astype(BF16))
    for st in streams:
        m_ak = jnp.where(st["strict"], st["xk"][:c], 0.0)
        st["rhs"] = st["xs"][:c] + _dot(m_ak.astype(BF16), st["v_st"])
        st["pw"] = jnp.where(st["strict"], st["xb"][:c], 0.0)
        st["t"] = st["eye"] + st["pw"]
        st["p_st"] = st["stack"](st["pw"])
    for _ in range(int(math.log2(c)) - 1):
        for st in streams:
            st["pw"] = _dot(st["pw"].astype(BF16), st["p_st"])
            st["p_st"] = st["stack"](st["pw"])
        for st in streams:
            st["t"] = st["t"] + _dot(st["t"].astype(BF16), st["p_st"])
    for st in streams:
        st["u"] = _dot(st["t"].astype(BF16), st["stack"](st["rhs"]))
    outs = []
    for st in streams:
        m_rb = jnp.where(st["incl"], st["xb"][c:], 0.0)
        m_rk = jnp.where(st["incl"], st["xk"][c:], 0.0)
        outs.append(st["xs"][c:] + _dot(m_rb.astype(BF16), st["stack"](st["u"]))
                    + _dot(m_rk.astype(BF16), st["v_st"]))
        uv = both(st["u"], st["v"]).astype(BF16)
        st["s_ref"][...] = st["s"] * st["w_tot"] + jnp.where(st["head_mask"], _tn(uv, st["bk"]), 0.0)
    return outs


def _dot_exact_rhs_lhs(tri_bf16, x):
    xh, xl = _split2(x)
    return _dot(tri_bf16, xh) + _dot(tri_bf16, xl)


def _wkv_kernel(rf, kf, vf, kkf, lwf, af, rb, kb, vb, kkb, lwb, ab, ka_ref, of_ref, ob_ref, s_ref):
    @pl.when(pl.program_id(1) == 0)
    def _():
        s_ref[...] = jnp.zeros_like(s_ref)

    ka = ka_ref[...]
    fwd = _wkv_direction(rf[...], kf[...], vf[...], kkf[...], lwf[...], af[...], ka, s_ref.at[0], False)
    bwd = _wkv_direction(rb[...], kb[...], vb[...], kkb[...], lwb[...], ab[...], ka, s_ref.at[1], True)
    outs = _wkv_streams_step(fwd + bwd)
    of_ref[...] = jnp.concatenate(outs[:len(fwd)], axis=1)
    ob_ref[...] = jnp.concatenate(outs[len(fwd):], axis=1)


def _wkv(r, k, v, kk, lw, a, rw_k_a):
    bsz, seq, c = r.shape
    nc = seq // CHUNK
    fwd = lambda lane_blk: pl.BlockSpec((None, CHUNK, c), lambda b, j: (b, j, lane_blk))
    bwd = lambda lane_blk: pl.BlockSpec((None, CHUNK, c), lambda b, j: (b, nc - 1 - j, lane_blk))
    return pl.pallas_call(
        _wkv_kernel,
        grid=(bsz, nc),
        in_specs=[fwd(0)] * 4 + [fwd(0), fwd(0)] + [bwd(0)] * 4 + [bwd(1), bwd(1)]
        + [pl.BlockSpec((1, c), lambda b, j: (0, 0))],
        out_specs=[fwd(0), bwd(0)],
        out_shape=[jax.ShapeDtypeStruct((bsz, seq, c), F32)] * 2,
        scratch_shapes=[pltpu.VMEM((2, c // MXU_DIM, MXU_DIM, MXU_DIM), F32)],
        compiler_params=_params("arbitrary", "arbitrary"),
        name="wkv7_chunked",
    )(r, k, v, kk, lw, a, r, k, v, kk, lw, a, rw_k_a.reshape(1, c))


NEG_INF = float("-inf")


def _first_max(vals, idx, size):
    m = jnp.max(vals, axis=0, keepdims=True)
    i = jnp.min(jnp.where(vals == m, idx, size), axis=0, keepdims=True)
    return m, i


def _route(scores, biased):
    e, tt = scores.shape
    per = e // N_GROUPS
    rowl = lax.broadcasted_iota(jnp.int32, (per, tt), 0)
    gs = []
    for g in range(N_GROUPS):
        blk = biased[g * per:(g + 1) * per]
        m1, i1 = _first_max(blk, rowl, per)
        m2 = jnp.max(jnp.where(rowl == i1, NEG_INF, blk), axis=0, keepdims=True)
        gs.append(m1 + m2)
    cur = jnp.concatenate(gs, axis=0)
    growl = lax.broadcasted_iota(jnp.int32, (N_GROUPS, tt), 0)
    gsel = jnp.zeros((N_GROUPS, tt), F32)
    for _ in range(TOPK_GROUPS):
        _, ig = _first_max(cur, growl, N_GROUPS)
        hit = growl == ig
        gsel = jnp.where(hit, 1.0, gsel)
        cur = jnp.where(hit, NEG_INF, cur)
    emask = jnp.concatenate([jnp.broadcast_to(gsel[g:g + 1], (per, tt)) for g in range(N_GROUPS)], axis=0)
    masked = jnp.where(emask > 0.5, biased, NEG_INF)
    row = lax.broadcasted_iota(jnp.int32, (e, tt), 0)
    ids, ws = [], []
    for _ in range(TOP_K):
        _, ie = _first_max(masked, row, e)
        hit = row == ie
        ids.append(ie)
        ws.append(jnp.sum(jnp.where(hit, scores, 0.0), axis=0, keepdims=True))
        masked = jnp.where(hit, NEG_INF, masked)
    w = jnp.concatenate(ws, axis=0)
    w = w / jnp.sum(w, axis=0, keepdims=True) * ROUTE_SCALE
    return jnp.concatenate(ids, axis=0), w


def _mixout_kernel(x_ref, mod_ref, yhy_ref, of_ref, ob_ref, g_ref, bonus_ref, lnw_ref, lnb_ref,
                   ones_ref, wout_ref, g2n_ref, rwt_ref, bias_ref,
                   x1_ref, h2_ref, eid_ref, wsel_ref):
    ones = ones_ref[...]
    s = of_ref[...] + ob_ref[...]
    mean = _dot_exact_rhs(s, ones) * (1.0 / HEAD)
    dlt = s - mean
    var = _dot_exact_rhs(dlt * dlt, ones) * (1.0 / HEAD)
    sn = dlt * lax.rsqrt(var + GN_EPS) * lnw_ref[...] + lnb_ref[...]
    yrw = (sn + bonus_ref[...]) * g_ref[...]
    ch = yhy_ref.shape[-1]
    mix = _dot(yhy_ref[...].astype(BF16), wout_ref[:ch, :]) + _dot(yrw.astype(BF16), wout_ref[ch:, :])
    x1 = x_ref[...] + mod_ref[2:3, :] * mix
    x1_ref[...] = x1
    ms = jnp.mean(x1 * x1, axis=-1, keepdims=True)
    h2 = x1 * lax.rsqrt(ms + NORM_EPS) * g2n_ref[...]
    h2 = h2 * (1.0 + mod_ref[4:5, :]) + mod_ref[3:4, :]
    h2_ref[...] = _pack_halves(h2)
    rh, rl = _split2(rwt_ref[...])
    hh, hl = _split2(h2)
    logits = _nt(rh, hh) + (_nt(rh, hl) + _nt(rl, hh))
    scores = jax.nn.sigmoid(logits)
    ids, w = _route(scores, scores + bias_ref[...])
    eid_ref[...] = ids
    wsel_ref[...] = w


def _mix_out(x, mod, yhy, o_f, o_b, g, bonus, ln_w, ln_b, w_out, norm2_g, router_w, router_bias, tt=256):
    bsz, seq, d = x.shape
    tt = min(tt, seq)
    nt = seq // tt
    n = bsz * seq
    c = D_RWKV
    e = router_w.shape[1]
    row = lambda a: a.reshape(1, -1)
    consts = [row(ln_w), row(ln_b), _head_ones(), w_out.astype(BF16), row(norm2_g), router_w.T,
              jnp.broadcast_to(router_bias.reshape(e, 1), (e, tt))]
    const = lambda a: pl.BlockSpec(a.shape, lambda b, i: (0,) * a.ndim)
    tile = lambda w: pl.BlockSpec((None, tt, w), lambda b, i: (b, i, 0))
    flat = lambda rows, dt: jax.ShapeDtypeStruct((rows, n), dt)
    return pl.pallas_call(
        _mixout_kernel,
        grid=(bsz, nt),
        in_specs=[tile(d), pl.BlockSpec((None,) + mod.shape[1:], lambda b, i: (b, 0, 0))]
        + [tile(c)] * 5 + [const(a) for a in consts],
        out_specs=[tile(d), pl.BlockSpec((tt, d // 2), lambda b, i: (b * nt + i, 0)),
                   pl.BlockSpec((TOP_K, tt), lambda b, i: (0, b * nt + i)),
                   pl.BlockSpec((TOP_K, tt), lambda b, i: (0, b * nt + i))],
        out_shape=[jax.ShapeDtypeStruct((bsz, seq, d), F32), jax.ShapeDtypeStruct((n, d // 2), U32),
                   flat(TOP_K, jnp.int32), flat(TOP_K, F32)],
        compiler_params=_params("arbitrary", "arbitrary"),
        name="mix_out_router",
    )(x, mod, yhy, o_f, o_b, g, bonus, *consts)


BLK = 256
BLK_SHIFT = 8


def _multi_hot(eid, e):
    row = lax.broadcasted_iota(jnp.int32, (e, eid.shape[1]), 0)
    mh = jnp.zeros((e, eid.shape[1]), F32)
    for kk in range(TOP_K):
        mh = mh + jnp.where(row == eid[kk:kk + 1, :], 1.0, 0.0)
    return row, mh


def _lookup(row, eid, table):
    return jnp.concatenate(
        [jnp.sum(jnp.where(row == eid[kk:kk + 1, :], table, 0.0), axis=0, keepdims=True)
         for kk in range(TOP_K)], axis=0)


def _rank_kernel(eid_ref, rank_ref, cnt_ref, *, e):
    @pl.when(pl.program_id(0) == 0)
    def _():
        cnt_ref[...] = jnp.zeros_like(cnt_ref)

    eid = eid_ref[...]
    tt = eid.shape[1]
    row, mh = _multi_hot(eid, e)
    mhb = mh.astype(BF16)
    si = lax.broadcasted_iota(jnp.int32, (tt, tt), 0)
    ti = lax.broadcasted_iota(jnp.int32, (tt, tt), 1)
    earlier = _dot(mhb, jnp.where(si < ti, 1.0, 0.0).astype(BF16))
    cnt = cnt_ref[...]
    full = earlier + jnp.concatenate([cnt] * (tt // LANES), axis=1)
    rank_ref[...] = _lookup(row, eid, full).astype(jnp.int32)
    cnt_ref[...] = cnt + _dot(mhb, jnp.ones((tt, LANES), BF16))


def _expert_ranks(eid, e, tt=512):
    n = eid.shape[1]
    tt = min(tt, n)
    return pl.pallas_call(
        functools.partial(_rank_kernel, e=e),
        grid=(n // tt,),
        in_specs=[pl.BlockSpec((TOP_K, tt), lambda i: (0, i))],
        out_specs=[pl.BlockSpec((TOP_K, tt), lambda i: (0, i)),
                   pl.BlockSpec((e, LANES), lambda i: (0, 0))],
        out_shape=[jax.ShapeDtypeStruct((TOP_K, n), jnp.int32), jax.ShapeDtypeStruct((e, LANES), F32)],
        compiler_params=_params("arbitrary"),
        name="expert_ranks",
    )(eid)


def _block_offsets(cnt):
    e = cnt.shape[0]
    nblk = ((cnt.astype(jnp.int32) + (BLK - 1)) >> BLK_SHIFT).astype(F32)
    ri = lax.broadcasted_iota(jnp.int32, (e, e), 0)
    ci = lax.broadcasted_iota(jnp.int32, (e, e), 1)
    tril = jnp.where(ci <= ri, 1.0, 0.0).astype(BF16)
    nh, nl = _split2(nblk)
    return nblk, _dot(tril, nh) + _dot(tril, nl)


def _dest_kernel(cnt_ref, eid_ref, rank_ref, dest_ref):
    nblk, end = _block_offsets(cnt_ref[...])
    off = (end - nblk) * float(BLK)
    eid = eid_ref[...]
    tt = eid.shape[1]
    row = lax.broadcasted_iota(jnp.int32, (off.shape[0], tt), 0)
    table = jnp.concatenate([off] * (tt // LANES), axis=1)
    dest_ref[...] = _lookup(row, eid, table).astype(jnp.int32) + rank_ref[...]


def _destinations(cnt, eid, rank, tt=512):
    n = eid.shape[1]
    tt = min(tt, n)
    blk = pl.BlockSpec((TOP_K, tt), lambda i: (0, i))
    return pl.pallas_call(
        _dest_kernel,
        grid=(n // tt,),
        in_specs=[pl.BlockSpec(cnt.shape, lambda i: (0, 0)), blk, blk],
        out_specs=blk,
        out_shape=jax.ShapeDtypeStruct((TOP_K, n), jnp.int32),
        compiler_params=_params("arbitrary"),
        name="expert_destinations",
    )(cnt, eid, rank)


def _meta_kernel(cnt_ref, meta_ref, emeta_ref, *, nbp):
    cnt = cnt_ref[...]
    e = cnt.shape[0]
    nblk, end = _block_offsets(cnt)
    rep = lambda a, w: jnp.concatenate([a] * (w // LANES), axis=1)
    b = lax.broadcasted_iota(jnp.int32, (e, nbp), 1).astype(F32)
    blk_e = jnp.minimum(jnp.sum(jnp.where(rep(end, nbp) <= b, 1.0, 0.0), axis=0, keepdims=True), float(e - 1))
    row = lax.broadcasted_iota(jnp.int32, (e, nbp), 0).astype(F32)
    mine = row == blk_e
    left = rep(cnt + (end - nblk) * float(BLK), nbp) - b * float(BLK)
    nvalid = jnp.clip(jnp.sum(jnp.where(mine, left, 0.0), axis=0, keepdims=True), 0.0, float(BLK))
    nused = jnp.max(rep(end, nbp), axis=0, keepdims=True)
    meta_ref[...] = jnp.concatenate([blk_e, nvalid, nused, jnp.zeros((5, nbp), F32)], axis=0).astype(jnp.int32)
    eye = lax.broadcasted_iota(jnp.int32, (e, e), 0) == lax.broadcasted_iota(jnp.int32, (e, e), 1)
    to_row = lambda a: jnp.sum(jnp.where(eye, rep(a, e), 0.0), axis=0, keepdims=True)
    emeta_ref[...] = jnp.concatenate([to_row(end), to_row(cnt), jnp.zeros((6, e), F32)], axis=0).astype(jnp.int32)


def _block_meta(cnt, nb):
    e = cnt.shape[0]
    nbp = -(-nb // LANES) * LANES
    return pl.pallas_call(
        functools.partial(_meta_kernel, nbp=nbp),
        out_shape=[jax.ShapeDtypeStruct((8, nbp), jnp.int32), jax.ShapeDtypeStruct((8, e), jnp.int32)],
        compiler_params=pltpu.CompilerParams(vmem_limit_bytes=VMEM_LIMIT),
        name="expert_block_meta",
    )(cnt)


def _row_copy(src_ref, s, dst_ref, t, sem):
    return pltpu.make_async_copy(src_ref.at[pl.ds(s, 1), :], dst_ref.at[pl.ds(t, 1), :], sem)


def _dispatch_kernel(eend_ref, ecnt_ref, dest_ref, h_ref, xs_ref, zero_ref, zsem, sem, *, e, tt):
    @pl.when(pl.program_id(0) == 0)
    def _():
        zero_ref[...] = jnp.zeros_like(zero_ref)

        def tail(ex):
            start = pl.multiple_of((eend_ref[ex] - 1) * BLK, BLK)
            return pltpu.make_async_copy(zero_ref, xs_ref.at[pl.ds(start, BLK), :], zsem)

        def issue(ex, carry):
            @pl.when(ecnt_ref[ex] > 0)
            def _():
                tail(ex).start()
            return carry

        def drain(ex, carry):
            @pl.when(ecnt_ref[ex] > 0)
            def _():
                tail(ex).wait()
            return carry

        lax.fori_loop(0, e, issue, 0)
        lax.fori_loop(0, e, drain, 0)

    def issue_rows(t, carry):
        for kk in range(TOP_K):
            _row_copy(h_ref, t, xs_ref, dest_ref[kk, t], sem).start(priority=kk % 2)
        return carry

    def drain_rows(t, carry):
        for kk in range(TOP_K):
            _row_copy(h_ref, t, xs_ref, dest_ref[kk, t], sem).wait()
        return carry

    lax.fori_loop(0, tt, issue_rows, 0)
    lax.fori_loop(0, tt, drain_rows, 0)


def _dispatch(eend, ecnt, dest, h2, nb, tt=256):
    n, d = h2.shape
    e = eend.shape[0]
    tt = min(tt, n)
    return pl.pallas_call(
        functools.partial(_dispatch_kernel, e=e, tt=tt),
        grid_spec=pltpu.PrefetchScalarGridSpec(
            num_scalar_prefetch=2,
            grid=(n // tt,),
            in_specs=[pl.BlockSpec((TOP_K, tt), lambda i, *_: (0, i), memory_space=pltpu.SMEM),
                      pl.BlockSpec((tt, d), lambda i, *_: (i, 0))],
            out_specs=pl.BlockSpec(memory_space=pl.ANY),
            scratch_shapes=[pltpu.VMEM((BLK, d), h2.dtype), pltpu.SemaphoreType.DMA(()),
                            pltpu.SemaphoreType.DMA(())],
        ),
        out_shape=jax.ShapeDtypeStruct((nb * BLK, d), h2.dtype),
        compiler_params=_params("arbitrary"),
        name="moe_dispatch",
    )(eend, ecnt, dest, h2)


def _experts_kernel(be_ref, nv_ref, nu_ref, x_ref, wg_ref, wu_ref, wd_ref, oa_ref, ob_ref, wgb, wub, wdb):
    b = pl.program_id(0)
    used = b < nu_ref[0]

    @pl.when(used)
    def _():
        prev = be_ref[jnp.maximum(b - 1, 0)]

        @pl.when(jnp.logical_or(b == 0, be_ref[b] != prev))
        def _():
            wgb[...] = wg_ref[...].astype(BF16)
            wub[...] = wu_ref[...].astype(BF16)
            wdb[...] = wd_ref[...].astype(BF16)

        row = lax.broadcasted_iota(jnp.int32, (x_ref.shape[0], 1), 0)
        lo, hi = _unpack_halves(jnp.where(row < nv_ref[b], x_ref[...], jnp.uint32(0)))
        x = jnp.concatenate([lo.astype(BF16), hi.astype(BF16)], axis=1)
        act = _silu(_dot(x, wgb[...])) * _dot(x, wub[...])
        packed = _pack_halves(_dot(act.astype(BF16), wdb[...]))
        half = packed.shape[1] // 2
        oa_ref[...] = packed[:, :half]
        ob_ref[...] = packed[:, half:]

    @pl.when(jnp.logical_not(used))
    def _():
        oa_ref[...] = jnp.zeros_like(oa_ref)
        ob_ref[...] = jnp.zeros_like(ob_ref)


def _experts(blk_e, nvalid, nused, xs, wg, wu, wd):
    p, dp = xs.shape
    nb = p // BLK
    d, de = wg.shape[1], wg.shape[2]
    last = lambda b, be, nv, nu: jnp.minimum(b, nu[0] - 1)
    return pl.pallas_call(
        _experts_kernel,
        grid_spec=pltpu.PrefetchScalarGridSpec(
            num_scalar_prefetch=3,
            grid=(nb,),
            in_specs=[pl.BlockSpec((BLK, dp), lambda b, be, nv, nu: (last(b, be, nv, nu), 0)),
                      pl.BlockSpec((None, d, de), lambda b, be, nv, nu: (be[last(b, be, nv, nu)], 0, 0)),
                      pl.BlockSpec((None, d, de), lambda b, be, nv, nu: (be[last(b, be, nv, nu)], 0, 0)),
                      pl.BlockSpec((None, de, d), lambda b, be, nv, nu: (be[last(b, be, nv, nu)], 0, 0))],
            out_specs=[pl.BlockSpec((BLK, dp // 2), lambda b, be, nv, nu: (b, 0))] * 2,
            scratch_shapes=[pltpu.VMEM((d, de), BF16), pltpu.VMEM((d, de), BF16), pltpu.VMEM((de, d), BF16)],
        ),
        out_shape=[jax.ShapeDtypeStruct((p, dp // 2), U32)] * 2,
        compiler_params=_params("arbitrary"),
        name="moe_experts",
    )(blk_e, nvalid, nused, xs, wg, wu, wd)


SC_WINDOW = 128


def _sc_gather_rows(src, idx):
    num = idx.shape[1]
    width = src.shape[1]
    mesh = plsc.VectorSubcoreMesh(core_axis_name="core", subcore_axis_name="subcore")

    @pl.kernel(out_type=jax.ShapeDtypeStruct((num, width), src.dtype), mesh=mesh)
    def gather(src_hbm, idx_hbm, out_hbm):
        def body(idx_vmem, out_vmem):
            pltpu.sync_copy(src_hbm.at[idx_vmem.at[0]], out_vmem)

        pltpu.emit_pipeline(
            body,
            grid=(num // SC_WINDOW,),
            in_specs=[pl.BlockSpec((1, SC_WINDOW), index_map=lambda i: (0, i))],
            out_specs=[pl.BlockSpec((SC_WINDOW, width), index_map=lambda i: (i, 0))],
            core_axis_name=("core", "subcore"),
            dimension_semantics=(pltpu.PARALLEL,),
        )(idx_hbm, out_hbm)

    return gather(src, idx)


def _combine_kernel(w_ref, x1_ref, h2_ref, mod_ref, ga_ref, gb_ref, sg_ref, su_ref, sd_ref, gf_ref,
                    sel_ref, o_ref):
    lo, hi = _unpack_halves(h2_ref[...])
    hb = jnp.concatenate([lo.astype(BF16), hi.astype(BF16)], axis=1)
    act = _silu(_dot(hb, sg_ref[...])) * _dot(hb, su_ref[...])
    ffn = _dot(act.astype(BF16), sd_ref[...])
    wh, wl = _split2(w_ref[...])
    acc = None
    for kk in range(TOP_K):
        sel = sel_ref[kk]
        wk = _tn(wh, sel) + _tn(wl, sel)
        a_lo, a_hi = _unpack_halves(ga_ref[kk])
        b_lo, b_hi = _unpack_halves(gb_ref[kk])
        parts = [a_lo * wk, b_lo * wk, a_hi * wk, b_hi * wk]
        acc = parts if acc is None else [p + q for p, q in zip(acc, parts)]
    ffn = ffn + jnp.concatenate(acc, axis=1)
    xo = x1_ref[...] + mod_ref[5:6, :] * ffn
    ms = jnp.mean(xo * xo, axis=-1, keepdims=True)
    o_ref[...] = xo * lax.rsqrt(ms + NORM_EPS) * gf_ref[...]


def _combine(wsel, x1, h2, mod, ga, gb, sh_wg, sh_wu, sh_wd, normf_g, tt=256):
    bsz, seq, d = x1.shape
    n = bsz * seq
    tt = min(tt, seq)
    per = seq // tt
    dp = h2.shape[1]
    dq = ga.shape[2]
    sel = jnp.asarray(np.broadcast_to(np.eye(TOP_K)[:, :, None], (TOP_K, TOP_K, dq)), BF16)
    consts = [sh_wg.astype(BF16), sh_wu.astype(BF16), sh_wd.astype(BF16), normf_g.reshape(1, d), sel]
    const = lambda a: pl.BlockSpec(a.shape, lambda i: (0,) * a.ndim)
    rows = pl.BlockSpec((tt, d), lambda i: (i, 0))
    packed_rows = pl.BlockSpec((tt, dp), lambda i: (i, 0))
    gathered = pl.BlockSpec((TOP_K, tt, dq), lambda i: (0, i, 0))
    return pl.pallas_call(
        _combine_kernel,
        grid=(n // tt,),
        in_specs=[pl.BlockSpec((TOP_K, tt), lambda i: (0, i)),
                  rows, packed_rows,
                  pl.BlockSpec((None,) + mod.shape[1:], lambda i: (i // per, 0, 0)),
                  gathered, gathered] + [const(a) for a in consts],
        out_specs=rows,
        out_shape=jax.ShapeDtypeStruct((n, d), F32),
        compiler_params=_params("arbitrary"),
        name="moe_combine",
    )(wsel, x1.reshape(n, d), h2, mod, ga, gb, *consts)


def _moe(x1, h2, mod, eid, wsel, exp_wg, exp_wu, exp_wd, sh_wg, sh_wu, sh_wd, normf_g):
    n = h2.shape[0]
    e = exp_wg.shape[0]
    nb = (n * TOP_K + e * (BLK - 1)) // BLK
    rank, cnt = _expert_ranks(eid, e)
    dest = _destinations(cnt, eid, rank)
    meta, emeta = _block_meta(cnt, nb)
    xs = _dispatch(emeta[0], emeta[1], dest, h2, nb)
    ys_a, ys_b = _experts(meta[0, :nb], meta[1, :nb], meta[2, :1], xs, exp_wg, exp_wu, exp_wd)
    idx = dest.reshape(1, TOP_K * n)
    ga = _sc_gather_rows(ys_a, idx).reshape(TOP_K, n, -1)
    gb = _sc_gather_rows(ys_b, idx).reshape(TOP_K, n, -1)
    return _combine(wsel, x1, h2, mod, ga, gb, sh_wg, sh_wu, sh_wd, normf_g)


def kernel(x, c, norm1_g, norm2_g, normf_g, w_ada, b_ada, w_in, w_out, hy_conv_w, hy_conv_b, hy_pos_w1, hy_pos_b1, hy_pos_w2, hy_pos_b2, hy_pos_w3, hy_sin_freq, hy_skip, rw_mu, rw_w0, rw_w_up, rw_a0, rw_a_up, rw_g_up, rw_k_k, rw_k_a, rw_r_k, rw_ln_w, rw_ln_b, router_w, router_bias, exp_w_gate, exp_w_up, exp_w_down, sh_w_gate, sh_w_up, sh_w_down):
    bsz, seq, d = x.shape
    depth = w_ada.shape[0]
    assert depth == 1, "the final norm is fused into the last kernel of a single layer"
    for l in range(depth):
        mod = _modulation(c, w_ada[l], b_ada[l]).reshape(bsz, -1, d)
        uhy, r, k, v, kk, lw, a, g, bonus = _projection(
            x, mod, norm1_g[l], w_in[l], hy_conv_w[l], hy_conv_b[l], rw_mu[l], rw_w0[l], rw_w_up[l],
            rw_a0[l], rw_a_up[l], rw_g_up[l], rw_k_k[l], rw_k_a[l], rw_r_k[l])
        k2, ss = _hyena_filters(seq, hy_pos_w1[l], hy_pos_b1[l], hy_pos_w2[l], hy_pos_b2[l],
                                hy_pos_w3[l], hy_sin_freq[l])
        khat = _filter_spectrum(k2, ss, seq)
        z, z_col = uhy, 0
        for order in range(HYENA_ORDER):
            z = _long_conv_gate(z, z_col, uhy, (order + 1) * D_HYENA, khat, hy_skip[l], order)
            z_col = 0
        o_f, o_b = _wkv(r, k, v, kk, lw, a, rw_k_a[l])
        x1, h2, eid, wsel = _mix_out(x, mod, z, o_f, o_b, g, bonus, rw_ln_w[l], rw_ln_b[l], w_out[l],
                                     norm2_g[l], router_w[l], router_bias[l])
        x = _moe(x1, h2, mod, eid, wsel, exp_w_gate[l], exp_w_up[l], exp_w_down[l],
                 sh_w_gate[l], sh_w_up[l], sh_w_down[l], normf_g)
        x = x.reshape(bsz, seq, d)
    return x
```

```python
import functools
import math

import jax
import jax.numpy as jnp
import numpy as np
from jax import lax
from jax.experimental import pallas as pl
from jax.experimental.pallas import tpu as pltpu
from jax.experimental.pallas import tpu_sc as plsc

F32 = jnp.float32
BF16 = jnp.bfloat16

LANES = 128
MXU_DIM = 256
VMEM_LIMIT = 56 * 1024 * 1024

D_HYENA = 512
D_RWKV = 512
HEAD = 64
N_HEADS = D_RWKV // HEAD
HYENA_ORDER = 2
FILTER_BANDS = 16
DECAY_TARGET = 1e-2
FAST_DECAY_PCT = 0.3
SLOW_DECAY_PCT = 1.5
FILTER_NORM_EPS = 1e-6
DECAY_LORA = 32
ICLR_LORA = 32
GATE_LORA = 96
GN_EPS = 64e-5
NORM_EPS = 1e-6
N_EXPERTS = 256
TOP_K = 8
N_GROUPS = 8
TOPK_GROUPS = 4
ROUTE_SCALE = 2.5
D_EXPERT = 256


def _params(*sem):
    return pltpu.CompilerParams(dimension_semantics=sem, vmem_limit_bytes=VMEM_LIMIT)


def _split2(a):
    hi = a.astype(BF16)
    lo = (a - hi.astype(F32)).astype(BF16)
    return hi, lo


def _dot(a, b):
    return jnp.dot(a, b, preferred_element_type=F32)


def _dot3(a, b):
    ah, al = _split2(a)
    bh, bl = _split2(b)
    return _dot(ah, bh) + (_dot(ah, bl) + _dot(al, bh))


def _dot_exact_rhs(a, b_bf16):
    ah, al = _split2(a)
    return _dot(ah, b_bf16) + _dot(al, b_bf16)


def _silu(x):
    return x * jax.nn.sigmoid(x)


U32 = jnp.uint32


def _pack_halves(x):
    w = x.shape[1] // 2
    rounded = x.astype(BF16).astype(F32)
    bits = lax.bitcast_convert_type(rounded, U32)
    return (bits[:, w:] & jnp.uint32(0xFFFF0000)) | (bits[:, :w] >> 16)


def _unpack_halves(p):
    lo = lax.bitcast_convert_type(p << 16, F32)
    hi = lax.bitcast_convert_type(p & jnp.uint32(0xFFFF0000), F32)
    return lo, hi


def _pack_rows(x):
    packed = _pack_halves(x)
    half = packed.shape[1] // 2
    return packed[:, :half], packed[:, half:]


def _unpack_rows(a, b):
    a_lo, a_hi = _unpack_halves(a)
    b_lo, b_hi = _unpack_halves(b)
    return jnp.concatenate([a_lo.astype(BF16), b_lo.astype(BF16), a_hi.astype(BF16), b_hi.astype(BF16)], axis=1)


def _mod_kernel(c_ref, w_ref, b_ref, o_ref):
    o_ref[...] = _dot3(_silu(c_ref[...]), w_ref[...]) + b_ref[...]


def _modulation(c, w_ada, b_ada):
    bsz, d = c.shape
    n = w_ada.shape[1]
    blk = 1024
    return pl.pallas_call(
        _mod_kernel,
        grid=(n // blk,),
        in_specs=[
            pl.BlockSpec((bsz, d), lambda j: (0, 0)),
            pl.BlockSpec((d, blk), lambda j: (0, j)),
            pl.BlockSpec((1, blk), lambda j: (0, j)),
        ],
        out_specs=pl.BlockSpec((bsz, blk), lambda j: (0, j)),
        out_shape=jax.ShapeDtypeStruct((bsz, n), F32),
        compiler_params=_params("arbitrary"),
        name="adaln_mod",
    )(c, w_ada, b_ada.reshape(1, n))


def _filter_kernel(band_ref, w1_ref, b1_ref, w2_ref, b2_ref, w3_ref, freq_ref, delta_ref,
                   k_ref, ss_ref, *, seq, rows):
    half = pl.program_id(0)
    i = pl.program_id(1)
    r = lax.broadcasted_iota(jnp.int32, (rows, LANES), 0) + i * rows
    pos = jnp.where(half == 0, r, seq - r).astype(F32)
    tt = pos / float(max(seq - 1, 1))
    lane = lax.broadcasted_iota(jnp.int32, (rows, LANES), 1)
    ang = pos * band_ref[...]
    feats = jnp.where(lane == 0, tt,
                      jnp.where(lane <= FILTER_BANDS, jnp.cos(ang),
                                jnp.where(lane <= 2 * FILTER_BANDS, -jnp.sin(ang), 0.0)))
    freq = freq_ref[...]
    hdn = jnp.sin(freq * (_dot3(feats, w1_ref[...]) + b1_ref[...]))
    for j in range(w2_ref.shape[0]):
        hdn = jnp.sin(freq * (_dot3(hdn, w2_ref[j]) + b2_ref[j]))
    filt = _dot3(hdn, w3_ref[...])
    filt = filt * jnp.exp(-tt[:, :1] * delta_ref[...])
    valid = jnp.logical_or(half == 0, r[:, :1] > 0)
    filt = jnp.where(valid, filt, 0.0)
    k_ref[...] = filt

    @pl.when(jnp.logical_and(half == 0, i == 0))
    def _():
        ss_ref[...] = jnp.zeros_like(ss_ref)

    ss_ref[...] += jnp.broadcast_to(jnp.sum(filt * filt, axis=0, keepdims=True), ss_ref.shape)


def _hyena_filters(seq, pw1, pb1, pw2, pb2, pw3, freq):
    width = pw1.shape[1]
    ncol = HYENA_ORDER * D_HYENA
    rows = min(seq, 512)
    bands = np.zeros((1, LANES), np.float64)
    lin = np.linspace(1e-4, FILTER_BANDS - 1, FILTER_BANDS)
    bands[0, 1:1 + FILTER_BANDS] = lin
    bands[0, 1 + FILTER_BANDS:1 + 2 * FILTER_BANDS] = lin
    bands = jnp.asarray(bands * (2.0 * math.pi / seq), F32)
    deltas = np.abs(np.linspace(math.log(DECAY_TARGET) / SLOW_DECAY_PCT,
                                math.log(DECAY_TARGET) / FAST_DECAY_PCT, D_HYENA))
    deltas = jnp.asarray(np.tile(deltas, HYENA_ORDER)[None], F32)
    w1 = jnp.zeros((LANES, width), F32).at[:pw1.shape[0]].set(pw1)
    w3 = pw3.reshape(width, HYENA_ORDER, 2, D_HYENA).transpose(2, 0, 1, 3).reshape(2, width, ncol)
    nt = seq // rows
    full = lambda *shape: pl.BlockSpec(shape, lambda h, i: (0,) * len(shape))
    return pl.pallas_call(
        functools.partial(_filter_kernel, seq=seq, rows=rows),
        grid=(2, nt),
        in_specs=[
            full(1, LANES), full(LANES, width), full(1, width),
            full(pw2.shape[0], width, width), full(pw2.shape[0], 1, width),
            pl.BlockSpec((None, width, ncol), lambda h, i: (h, 0, 0)),
            full(1, width), full(1, ncol),
        ],
        out_specs=[
            pl.BlockSpec((rows, ncol), lambda h, i: (h * nt + i, 0)),
            pl.BlockSpec((8, ncol), lambda h, i: (0, 0)),
        ],
        out_shape=[jax.ShapeDtypeStruct((2 * seq, ncol), F32),
                   jax.ShapeDtypeStruct((8, ncol), F32)],
        compiler_params=_params("arbitrary", "arbitrary"),
        name="hyena_filters",
    )(bands, w1, pb1.reshape(1, width), pw2, pb2.reshape(pw2.shape[0], 1, width), w3,
      freq.reshape(1, width), deltas)


N1 = LANES
UNROLL = 8


def _dft_tables(seq):
    m = 2 * seq
    n2 = m // N1
    n2h = n2 // 2
    n1 = np.arange(N1)[:, None, None]
    f2 = np.arange(n2)[None, :, None]
    k2 = np.arange(n2)[None, None, :]
    th = 2.0 * np.pi * (n1 * f2 / m + (k2 * f2 % n2) / n2)
    fwd_a = np.concatenate([np.cos(th), -np.sin(th)], axis=1)
    tht = np.transpose(th, (0, 2, 1))
    inv_a = np.concatenate([np.cos(tht), -np.sin(tht)], axis=2)[:, :n2h] / m
    a = np.arange(N1)
    ph = 2.0 * np.pi * np.outer(a, a) / N1
    c, s = np.cos(ph), np.sin(ph)
    fwd_b = np.block([[c, s], [-s, c]])
    inv_b = np.block([[c, -s], [s, c]])
    cast = lambda t: jnp.asarray(t, BF16)
    return cast(fwd_a), cast(fwd_a[:, :, :n2h]), cast(inv_a), cast(fwd_b), cast(inv_b), n2, n2h


def _stage_a_fwd(x_ref, wa_ref, y_ref, n2, scale=None):
    def body(i, carry):
        trips = [i * UNROLL + j for j in range(UNROLL)]
        xs = [x_ref[pl.ds(n1, wa_ref.shape[2], stride=N1), :] for n1 in trips]
        if scale is not None:
            xs = [x * scale for x in xs]
        prods = [_dot(wa_ref[n1], x.astype(BF16)) for n1, x in zip(trips, xs)]
        for n1, a in zip(trips, prods):
            y_ref[pl.ds(n1, n2, stride=2 * N1), :] = a[:n2]
            y_ref[pl.ds(N1 + n1, n2, stride=2 * N1), :] = a[n2:]
        return carry
    lax.fori_loop(0, N1 // UNROLL, body, 0)


def _filter_fft_kernel(k_ref, ss_ref, wa_ref, fb_ref, o_ref, y_ref, *, n2):
    scale = lax.rsqrt(ss_ref[0:1, :] + FILTER_NORM_EPS)
    _stage_a_fwd(k_ref, wa_ref, y_ref, n2, scale=scale)

    unr = min(UNROLL, n2)

    def body(i, carry):
        trips = [i * unr + j for j in range(unr)]
        ys = [y_ref[pl.ds(pl.multiple_of(f2 * 2 * N1, 2 * N1), 2 * N1), :].astype(BF16) for f2 in trips]
        for f2, y in zip(trips, ys):
            o_ref[f2] = _dot(fb_ref[...], y)
        return carry
    lax.fori_loop(0, n2 // unr, body, 0)


def _filter_spectrum(k2, ss, seq):
    fwd_a, _, _, fwd_b, _, n2, _ = _dft_tables(seq)
    ncol = k2.shape[1]
    nblk = ncol // LANES
    return pl.pallas_call(
        functools.partial(_filter_fft_kernel, n2=n2),
        grid=(nblk,),
        in_specs=[
            pl.BlockSpec((2 * seq, LANES), lambda c: (0, c)),
            pl.BlockSpec((8, LANES), lambda c: (0, c)),
            pl.BlockSpec(fwd_a.shape, lambda c: (0, 0, 0)),
            pl.BlockSpec(fwd_b.shape, lambda c: (0, 0)),
        ],
        out_specs=pl.BlockSpec((None, n2, 2 * N1, LANES), lambda c: (c, 0, 0, 0)),
        out_shape=jax.ShapeDtypeStruct((nblk, n2, 2 * N1, LANES), F32),
        scratch_shapes=[pltpu.VMEM((n2 * 2 * N1, LANES), F32)],
        compiler_params=_params("arbitrary"),
        name="hyena_filter_fft",
    )(k2, ss, fwd_a, fwd_b)


def _conv_kernel(u_ref, g_ref, kh_ref, skip_ref, wa_ref, va_ref, fb_ref, ib_ref, o_ref, y_ref,
                 *, n2, n2h):
    _stage_a_fwd(u_ref, wa_ref, y_ref, n2)

    unr = min(UNROLL, n2)

    def mid(i, carry):
        trips = [i * unr + j for j in range(unr)]
        offs = [pl.multiple_of(f2 * 2 * N1, 2 * N1) for f2 in trips]
        zs = [_dot(fb_ref[...], y_ref[pl.ds(off, 2 * N1), :].astype(BF16)) for off in offs]
        ps = []
        for f2, z in zip(trips, zs):
            zr, zi = z[:N1], z[N1:]
            kh = kh_ref[f2]
            kr, ki = kh[:N1], kh[N1:]
            ps.append(jnp.concatenate([zr * kr - zi * ki, zr * ki + zi * kr], axis=0).astype(BF16))
        gs = [_dot(ib_ref[...], p) for p in ps]
        for off, g in zip(offs, gs):
            y_ref[pl.ds(off, 2 * N1), :] = g
        return carry
    lax.fori_loop(0, n2 // unr, mid, 0)

    skip = skip_ref[...]

    def last(i, carry):
        trips = [i * UNROLL + j for j in range(UNROLL)]
        gs = [jnp.concatenate([y_ref[pl.ds(n1, n2, stride=2 * N1), :],
                               y_ref[pl.ds(N1 + n1, n2, stride=2 * N1), :]], axis=0).astype(BF16)
              for n1 in trips]
        convs = [_dot(va_ref[n1], g) for n1, g in zip(trips, gs)]
        for n1, conv in zip(trips, convs):
            rows = pl.ds(n1, n2h, stride=N1)
            u = u_ref[rows, :]
            o_ref[rows, :] = g_ref[rows, :] * (conv + u * skip)
        return carry
    lax.fori_loop(0, N1 // UNROLL, last, 0)


def _long_conv_gate(u, u_col, gate, gate_col, khat, skip, order):
    bsz, seq, _ = u.shape
    ch = D_HYENA
    _, fwd_a, inv_a, fwd_b, inv_b, n2, n2h = _dft_tables(seq)
    nblk = ch // LANES
    const = lambda a: pl.BlockSpec(a.shape, lambda c, b: (0,) * a.ndim)
    at = lambda col: pl.BlockSpec((None, seq, LANES), lambda c, b: (b, 0, col // LANES + c))
    data = at(0)
    return pl.pallas_call(
        functools.partial(_conv_kernel, n2=n2, n2h=n2h),
        grid=(nblk, bsz),
        in_specs=[
            at(u_col), at(gate_col),
            pl.BlockSpec((None, n2, 2 * N1, LANES), lambda c, b: (order * nblk + c, 0, 0, 0)),
            pl.BlockSpec((1, LANES), lambda c, b: (0, c)),
            const(fwd_a), const(inv_a), const(fwd_b), const(inv_b),
        ],
        out_specs=data,
        out_shape=jax.ShapeDtypeStruct((bsz, seq, ch), F32),
        scratch_shapes=[pltpu.VMEM((n2 * 2 * N1, LANES), F32)],
        compiler_params=_params("arbitrary", "arbitrary"),
        name=f"hyena_conv{order}",
    )(u, gate, khat, skip[order].reshape(1, ch), fwd_a, inv_a, fwd_b, inv_b)


HALO = 8


def _shift_rows(p, k):
    return pltpu.roll(p, k % p.shape[0], axis=0)


def _proj_kernel(xp_ref, x_ref, xn_ref, mod_ref, g1_ref, why_ref, wrkv_ref, wlora_ref,
                 cw_ref, cb_ref, murkv_ref, mulora_ref, w0_ref, a0_ref, wwa_ref, gup_ref,
                 kk_ref, ka_ref, rk_ref, ones_ref,
                 uhy_ref, r_ref, k_ref, v_ref, kkn_ref, lw_ref, a_ref, g_ref, bonus_ref,
                 *, tt, nt):
    i = pl.program_id(1)
    xe = jnp.concatenate([xp_ref[...], x_ref[...], xn_ref[...]], axis=0)
    ms = jnp.mean(xe * xe, axis=-1, keepdims=True)
    h = xe * lax.rsqrt(ms + NORM_EPS) * g1_ref[...]
    h = h * (1.0 + mod_ref[1:2, :]) + mod_ref[0:1, :]
    row = lax.broadcasted_iota(jnp.int32, (tt + 2 * HALO, 1), 0)
    inside = jnp.logical_and(jnp.logical_or(row >= HALO, i > 0),
                             jnp.logical_or(row < tt + HALO, i < nt - 1))
    hb = jnp.where(inside, h, 0.0).astype(BF16)
    mid = slice(HALO, tt + HALO)

    p = _dot(hb, why_ref[...])
    u = (_shift_rows(p, 1) * cw_ref[0:1, :] + p * cw_ref[1:2, :]
         + _shift_rows(p, -1) * cw_ref[2:3, :] + cb_ref[...])
    uhy_ref[...] = u[mid]

    p = _dot(hb, wrkv_ref[...])
    p = p + murkv_ref[...] * (0.5 * (_shift_rows(p, 1) + _shift_rows(p, -1)) - p)
    p = p[mid]
    c = D_RWKV
    r, k, v = p[:, :c], p[:, c:2 * c], p[:, 2 * c:]
    r_ref[...] = r
    k_ref[...] = k
    v_ref[...] = v

    q = _dot(hb, wlora_ref[...])
    q = q + mulora_ref[...] * (0.5 * (_shift_rows(q, 1) + _shift_rows(q, -1)) - q)
    q = q[mid]
    wa = q[:, :LANES]
    lane = lax.broadcasted_iota(jnp.int32, wa.shape, 1)
    wa = jnp.where(lane < 2 * DECAY_LORA, jnp.tanh(wa), wa)
    up = _dot3(wa, wwa_ref[...])
    z = -(w0_ref[...] + up[:, :2 * c])
    softplus = jnp.maximum(z, 0.0) + jnp.log1p(jnp.exp(-jnp.abs(z)))
    lw_ref[...] = -jnp.exp(-softplus - 0.5)
    a = jax.nn.sigmoid(a0_ref[...] + up[:, 2 * c:])
    a_ref[...] = a
    g_ref[...] = _dot3(jax.nn.sigmoid(q[:, LANES:]), gup_ref[...])

    ones = ones_ref[...]
    kk = k * kk_ref[...]
    nrm = jnp.sqrt(_dot_exact_rhs(kk * kk, ones))
    kkn_ref[...] = kk / jnp.maximum(nrm, 1e-12)
    ka = ka_ref[...]
    ksum = k * (2.0 + (a[:, :c] + a[:, c:] - 2.0) * ka)
    bonus_ref[...] = _dot_exact_rhs(r * ksum * rk_ref[...], ones) * v


def _head_ones():
    hid = np.arange(D_RWKV) // HEAD
    return jnp.asarray(hid[:, None] == hid[None, :], BF16)


def _projection(x, mod, norm1_g, w_in, hy_conv_w, hy_conv_b, rw_mu, rw_w0, rw_w_up, rw_a0,
                rw_a_up, rw_g_up, rw_k_k, rw_k_a, rw_r_k, tt=256):
    bsz, seq, d = x.shape
    tt = min(tt, seq)
    nt = seq // tt
    c = D_RWKV
    hy = (HYENA_ORDER + 1) * D_HYENA
    nlora = 2 * LANES
    w_hy = w_in[:, :hy].astype(BF16)
    w_rkv = w_in[:, hy:hy + 3 * c].astype(BF16)
    w_lora = jnp.zeros((d, nlora), F32).at[:, :w_in.shape[1] - hy - 3 * c].set(w_in[:, hy + 3 * c:]).astype(BF16)
    mu_rkv = rw_mu[:3 * c].reshape(1, 3 * c)
    mu_lora = jnp.zeros((1, nlora), F32).at[0, :rw_mu.shape[0] - 3 * c].set(rw_mu[3 * c:])
    wwa = jnp.zeros((LANES, 4 * c), F32)
    for dd in range(2):
        wwa = wwa.at[dd * DECAY_LORA:(dd + 1) * DECAY_LORA, dd * c:(dd + 1) * c].set(rw_w_up[dd])
        wwa = wwa.at[2 * DECAY_LORA + dd * ICLR_LORA:2 * DECAY_LORA + (dd + 1) * ICLR_LORA,
                     2 * c + dd * c:2 * c + (dd + 1) * c].set(rw_a_up[dd])
    gup = jnp.zeros((LANES, c), F32).at[:GATE_LORA].set(rw_g_up)
    row = lambda a: a.reshape(1, -1)

    nb8 = seq // HALO
    tb = tt // HALO
    const = lambda a: pl.BlockSpec(a.shape, lambda b, i: (0,) * a.ndim)
    tile = lambda w: pl.BlockSpec((None, tt, w), lambda b, i: (b, i, 0))
    ins = [
        (x, pl.BlockSpec((None, HALO, d), lambda b, i: (b, jnp.maximum(i * tb - 1, 0), 0))),
        (x, pl.BlockSpec((None, tt, d), lambda b, i: (b, i, 0))),
        (x, pl.BlockSpec((None, HALO, d), lambda b, i: (b, jnp.minimum((i + 1) * tb, nb8 - 1), 0))),
        (mod, pl.BlockSpec((None,) + mod.shape[1:], lambda b, i: (b, 0, 0))),
    ]
    consts = [row(norm1_g), w_hy, w_rkv, w_lora, hy_conv_w, row(hy_conv_b), mu_rkv, mu_lora,
              row(rw_w0), row(rw_a0), wwa, gup, row(rw_k_k), row(rw_k_a), row(rw_r_k), _head_ones()]
    ins += [(a, const(a)) for a in consts]
    widths = [hy, c, c, c, c, 2 * c, 2 * c, c, c]
    return pl.pallas_call(
        functools.partial(_proj_kernel, tt=tt, nt=nt),
        grid=(bsz, nt),
        in_specs=[s for _, s in ins],
        out_specs=[tile(w) for w in widths],
        out_shape=[jax.ShapeDtypeStruct((bsz, seq, w), F32) for w in widths],
        compiler_params=_params("arbitrary", "arbitrary"),
        name="input_projection",
    )(*[a for a, _ in ins])


CHUNK = HEAD
GROUP = MXU_DIM // HEAD


def _nt(a, b):
    return lax.dot_general(a, b, (((1,), (1,)), ((), ())), preferred_element_type=F32)


def _tn(a, b):
    return lax.dot_general(a, b, (((0,), (0,)), ((), ())), preferred_element_type=F32)


def _wkv_direction(r, k, v, kk, lw, a, ka, s_ref, reverse):
    c = CHUNK
    ti = lax.broadcasted_iota(jnp.int32, (c, c), 0)
    si = lax.broadcasted_iota(jnp.int32, (c, c), 1)
    tri = (si >= ti) if reverse else (si <= ti)
    cum = _dot_exact_rhs_lhs(jnp.where(tri, 1.0, 0.0).astype(BF16), lw)
    tot = jnp.sum(lw, axis=0, keepdims=True)
    w_incl = jnp.exp(cum)
    w_prev = jnp.exp(cum - lw)
    w_inv = jnp.exp(-cum)
    w_end = jnp.exp(tot - cum)
    w_tot = jnp.exp(tot)
    kd = k * (1.0 + (a - 1.0) * ka)
    b = kk * a
    a_w = -kk * w_prev
    r_w = r * w_incl
    b_w = b * w_inv
    k_w = kd * w_inv
    b_e = b * w_end
    k_e = kd * w_end

    m = MXU_DIM
    ri = lax.broadcasted_iota(jnp.int32, (m, m), 0)
    ci = lax.broadcasted_iota(jnp.int32, (m, m), 1)
    head_mask = (ri // HEAD) == (ci // HEAD)
    tl = lax.broadcasted_iota(jnp.int32, (c, m), 0)
    sl = lax.broadcasted_iota(jnp.int32, (c, m), 1) % c
    strict = (sl > tl) if reverse else (sl < tl)
    incl = (sl >= tl) if reverse else (sl <= tl)
    eye = jnp.where(sl == tl, 1.0, 0.0)
    both = lambda top, bot: jnp.concatenate([top, bot], axis=0)

    def stack(xg):
        xb = xg.astype(BF16)
        return jnp.where(head_mask, jnp.concatenate([xb] * GROUP, axis=0), jnp.zeros((), BF16))

    streams = []
    for g in range(D_RWKV // m):
        sl_g = slice(g * m, (g + 1) * m)
        streams.append(dict(
            ar=both(a_w[:, sl_g], r_w[:, sl_g]).astype(BF16),
            b_st=stack(b_w[:, sl_g]), k_st=stack(k_w[:, sl_g]), v_st=stack(v[:, sl_g]),
            v=v[:, sl_g], bk=both(b_e[:, sl_g], k_e[:, sl_g]).astype(BF16),
            w_tot=w_tot[:, sl_g], s_ref=s_ref.at[g],
            strict=strict, incl=incl, eye=eye, head_mask=head_mask, stack=stack))
    return streams


def _wkv_streams_step(streams):
    c = CHUNK
    both = lambda top, bot: jnp.concatenate([top, bot], axis=0)
    for st in streams:
        st["s"] = st["s_ref"][...]
        st["xb"] = _nt(st["ar"], st["b_st"])
        st["xk"] = _nt(st["ar"], st["k_st"])
        st["xs"] = _nt(st["ar"], st["s"].astype(BF16))
    for st in streams:
        m_ak = jnp.where(st["strict"], st["xk"][:c], 0.0)
        st["rhs"] = st["xs"][:c] + _dot(m_ak.astype(BF16), st["v_st"])
        st["pw"] = jnp.where(st["strict"], st["xb"][:c], 0.0)
        st["t"] = st["eye"] + st["pw"]
        st["p_st"] = st["stack"](st["pw"])
    for _ in range(int(math.log2(c)) - 1):
        for st in streams:
            st["pw"] = _dot(st["pw"].astype(BF16), st["p_st"])
            st["p_st"] = st["stack"](st["pw"])
        for st in streams:
            st["t"] = st["t"] + _dot(st["t"].astype(BF16), st["p_st"])
    for st in streams:
        st["u"] = _dot(st["t"].astype(BF16), st["stack"](st["rhs"]))
    outs = []
    for st in streams:
        m_rb = jnp.where(st["incl"], st["xb"][c:], 0.0)
        m_rk = jnp.where(st["incl"], st["xk"][c:], 0.0)
        outs.append(st["xs"][c:] + _dot(m_rb.astype(BF16), st["stack"](st["u"]))
                    + _dot(m_rk.astype(BF16), st["v_st"]))
        uv = both(st["u"], st["v"]).astype(BF16)
        st["s_ref"][...] = st["s"] * st["w_tot"] + jnp.where(st["head_mask"], _tn(uv, st["bk"]), 0.0)
    return outs


def _dot_exact_rhs_lhs(tri_bf16, x):
    xh, xl = _split2(x)
    return _dot(tri_bf16, xh) + _dot(tri_bf16, xl)


def _wkv_kernel(rf, kf, vf, kkf, lwf, af, rb, kb, vb, kkb, lwb, ab, ka_ref, of_ref, ob_ref, s_ref):
    @pl.when(pl.program_id(1) == 0)
    def _():
        s_ref[...] = jnp.zeros_like(s_ref)

    ka = ka_ref[...]
    fwd = _wkv_direction(rf[...], kf[...], vf[...], kkf[...], lwf[...], af[...], ka, s_ref.at[0], False)
    bwd = _wkv_direction(rb[...], kb[...], vb[...], kkb[...], lwb[...], ab[...], ka, s_ref.at[1], True)
    outs = _wkv_streams_step(fwd + bwd)
    of_ref[...] = jnp.concatenate(outs[:len(fwd)], axis=1)
    ob_ref[...] = jnp.concatenate(outs[len(fwd):], axis=1)


def _wkv(r, k, v, kk, lw, a, rw_k_a):
    bsz, seq, c = r.shape
    nc = seq // CHUNK
    fwd = lambda lane_blk: pl.BlockSpec((None, CHUNK, c), lambda b, j: (b, j, lane_blk))
    bwd = lambda lane_blk: pl.BlockSpec((None, CHUNK, c), lambda b, j: (b, nc - 1 - j, lane_blk))
    return pl.pallas_call(
        _wkv_kernel,
        grid=(bsz, nc),
        in_specs=[fwd(0)] * 4 + [fwd(0), fwd(0)] + [bwd(0)] * 4 + [bwd(1), bwd(1)]
        + [pl.BlockSpec((1, c), lambda b, j: (0, 0))],
        out_specs=[fwd(0), bwd(0)],
        out_shape=[jax.ShapeDtypeStruct((bsz, seq, c), F32)] * 2,
        scratch_shapes=[pltpu.VMEM((2, c // MXU_DIM, MXU_DIM, MXU_DIM), F32)],
        compiler_params=_params("arbitrary", "arbitrary"),
        name="wkv7_chunked",
    )(r, k, v, kk, lw, a, r, k, v, kk, lw, a, rw_k_a.reshape(1, c))


NEG_INF = float("-inf")


def _first_max(vals, idx, size):
    m = jnp.max(vals, axis=0, keepdims=True)
    i = jnp.min(jnp.where(vals == m, idx, size), axis=0, keepdims=True)
    return m, i


def _route(scores, biased):
    e, tt = scores.shape
    per = e // N_GROUPS
    rowl = lax.broadcasted_iota(jnp.int32, (per, tt), 0)
    gs = []
    for g in range(N_GROUPS):
        blk = biased[g * per:(g + 1) * per]
        m1, i1 = _first_max(blk, rowl, per)
        m2 = jnp.max(jnp.where(rowl == i1, NEG_INF, blk), axis=0, keepdims=True)
        gs.append(m1 + m2)
    cur = jnp.concatenate(gs, axis=0)
    growl = lax.broadcasted_iota(jnp.int32, (N_GROUPS, tt), 0)
    gsel = jnp.zeros((N_GROUPS, tt), F32)
    for _ in range(TOPK_GROUPS):
        _, ig = _first_max(cur, growl, N_GROUPS)
        hit = growl == ig
        gsel = jnp.where(hit, 1.0, gsel)
        cur = jnp.where(hit, NEG_INF, cur)
    emask = jnp.concatenate([jnp.broadcast_to(gsel[g:g + 1], (per, tt)) for g in range(N_GROUPS)], axis=0)
    masked = jnp.where(emask > 0.5, biased, NEG_INF)
    row = lax.broadcasted_iota(jnp.int32, (e, tt), 0)
    ids, ws = [], []
    for _ in range(TOP_K):
        _, ie = _first_max(masked, row, e)
        hit = row == ie
        ids.append(ie)
        ws.append(jnp.sum(jnp.where(hit, scores, 0.0), axis=0, keepdims=True))
        masked = jnp.where(hit, NEG_INF, masked)
    w = jnp.concatenate(ws, axis=0)
    w = w / jnp.sum(w, axis=0, keepdims=True) * ROUTE_SCALE
    return jnp.concatenate(ids, axis=0), w


def _mixout_kernel(x_ref, mod_ref, yhy_ref, of_ref, ob_ref, g_ref, bonus_ref, lnw_ref, lnb_ref,
                   ones_ref, wout_ref, g2n_ref, rwt_ref, bias_ref,
                   x1_ref, h2a_ref, h2b_ref, eid_ref, wsel_ref):
    ones = ones_ref[...]
    s = of_ref[...] + ob_ref[...]
    mean = _dot_exact_rhs(s, ones) * (1.0 / HEAD)
    dlt = s - mean
    var = _dot_exact_rhs(dlt * dlt, ones) * (1.0 / HEAD)
    sn = dlt * lax.rsqrt(var + GN_EPS) * lnw_ref[...] + lnb_ref[...]
    yrw = (sn + bonus_ref[...]) * g_ref[...]
    ch = yhy_ref.shape[-1]
    mix = _dot(yhy_ref[...].astype(BF16), wout_ref[:ch, :]) + _dot(yrw.astype(BF16), wout_ref[ch:, :])
    x1 = x_ref[...] + mod_ref[2:3, :] * mix
    x1_ref[...] = x1
    ms = jnp.mean(x1 * x1, axis=-1, keepdims=True)
    h2 = x1 * lax.rsqrt(ms + NORM_EPS) * g2n_ref[...]
    h2 = h2 * (1.0 + mod_ref[4:5, :]) + mod_ref[3:4, :]
    h2a_ref[...], h2b_ref[...] = _pack_rows(h2)
    rh, rl = _split2(rwt_ref[...])
    hh, hl = _split2(h2)
    logits = _nt(rh, hh) + (_nt(rh, hl) + _nt(rl, hh))
    scores = jax.nn.sigmoid(logits)
    ids, w = _route(scores, scores + bias_ref[...])
    eid_ref[...] = ids
    wsel_ref[...] = w


def _mix_out(x, mod, yhy, o_f, o_b, g, bonus, ln_w, ln_b, w_out, norm2_g, router_w, router_bias, tt=256):
    bsz, seq, d = x.shape
    tt = min(tt, seq)
    nt = seq // tt
    n = bsz * seq
    c = D_RWKV
    e = router_w.shape[1]
    row = lambda a: a.reshape(1, -1)
    consts = [row(ln_w), row(ln_b), _head_ones(), w_out.astype(BF16), row(norm2_g), router_w.T,
              jnp.broadcast_to(router_bias.reshape(e, 1), (e, tt))]
    const = lambda a: pl.BlockSpec(a.shape, lambda b, i: (0,) * a.ndim)
    tile = lambda w: pl.BlockSpec((None, tt, w), lambda b, i: (b, i, 0))
    flat = lambda rows, dt: jax.ShapeDtypeStruct((rows, n), dt)
    return pl.pallas_call(
        _mixout_kernel,
        grid=(bsz, nt),
        in_specs=[tile(d), pl.BlockSpec((None,) + mod.shape[1:], lambda b, i: (b, 0, 0))]
        + [tile(c)] * 5 + [const(a) for a in consts],
        out_specs=[tile(d), pl.BlockSpec((tt, d // 4), lambda b, i: (b * nt + i, 0)),
                   pl.BlockSpec((tt, d // 4), lambda b, i: (b * nt + i, 0)),
                   pl.BlockSpec((TOP_K, tt), lambda b, i: (0, b * nt + i)),
                   pl.BlockSpec((TOP_K, tt), lambda b, i: (0, b * nt + i))],
        out_shape=[jax.ShapeDtypeStruct((bsz, seq, d), F32), jax.ShapeDtypeStruct((n, d // 4), U32),
                   jax.ShapeDtypeStruct((n, d // 4), U32),
                   flat(TOP_K, jnp.int32), flat(TOP_K, F32)],
        compiler_params=_params("arbitrary", "arbitrary"),
        name="mix_out_router",
    )(x, mod, yhy, o_f, o_b, g, bonus, *consts)


BLK = 256
BLK_SHIFT = 8


def _multi_hot(eid, e):
    row = lax.broadcasted_iota(jnp.int32, (e, eid.shape[1]), 0)
    mh = jnp.zeros((e, eid.shape[1]), F32)
    for kk in range(TOP_K):
        mh = mh + jnp.where(row == eid[kk:kk + 1, :], 1.0, 0.0)
    return row, mh


def _lookup(row, eid, table):
    return jnp.concatenate(
        [jnp.sum(jnp.where(row == eid[kk:kk + 1, :], table, 0.0), axis=0, keepdims=True)
         for kk in range(TOP_K)], axis=0)


def _rank_kernel(eid_ref, rank_ref, cnt_ref, *, e):
    @pl.when(pl.program_id(0) == 0)
    def _():
        cnt_ref[...] = jnp.zeros_like(cnt_ref)

    eid = eid_ref[...]
    tt = eid.shape[1]
    row, mh = _multi_hot(eid, e)
    mhb = mh.astype(BF16)
    si = lax.broadcasted_iota(jnp.int32, (tt, tt), 0)
    ti = lax.broadcasted_iota(jnp.int32, (tt, tt), 1)
    earlier = _dot(mhb, jnp.where(si < ti, 1.0, 0.0).astype(BF16))
    cnt = cnt_ref[...]
    full = earlier + jnp.concatenate([cnt] * (tt // LANES), axis=1)
    rank_ref[...] = _lookup(row, eid, full).astype(jnp.int32)
    cnt_ref[...] = cnt + _dot(mhb, jnp.ones((tt, LANES), BF16))


def _expert_ranks(eid, e, tt=512):
    n = eid.shape[1]
    tt = min(tt, n)
    return pl.pallas_call(
        functools.partial(_rank_kernel, e=e),
        grid=(n // tt,),
        in_specs=[pl.BlockSpec((TOP_K, tt), lambda i: (0, i))],
        out_specs=[pl.BlockSpec((TOP_K, tt), lambda i: (0, i)),
                   pl.BlockSpec((e, LANES), lambda i: (0, 0))],
        out_shape=[jax.ShapeDtypeStruct((TOP_K, n), jnp.int32), jax.ShapeDtypeStruct((e, LANES), F32)],
        compiler_params=_params("arbitrary"),
        name="expert_ranks",
    )(eid)


def _block_offsets(cnt):
    e = cnt.shape[0]
    nblk = ((cnt.astype(jnp.int32) + (BLK - 1)) >> BLK_SHIFT).astype(F32)
    ri = lax.broadcasted_iota(jnp.int32, (e, e), 0)
    ci = lax.broadcasted_iota(jnp.int32, (e, e), 1)
    tril = jnp.where(ci <= ri, 1.0, 0.0).astype(BF16)
    nh, nl = _split2(nblk)
    return nblk, _dot(tril, nh) + _dot(tril, nl)


def _dest_kernel(cnt_ref, eid_ref, rank_ref, dest_ref):
    nblk, end = _block_offsets(cnt_ref[...])
    off = (end - nblk) * float(BLK)
    eid = eid_ref[...]
    tt = eid.shape[1]
    row = lax.broadcasted_iota(jnp.int32, (off.shape[0], tt), 0)
    table = jnp.concatenate([off] * (tt // LANES), axis=1)
    dest_ref[...] = _lookup(row, eid, table).astype(jnp.int32) + rank_ref[...]


def _destinations(cnt, eid, rank, tt=512):
    n = eid.shape[1]
    tt = min(tt, n)
    blk = pl.BlockSpec((TOP_K, tt), lambda i: (0, i))
    return pl.pallas_call(
        _dest_kernel,
        grid=(n // tt,),
        in_specs=[pl.BlockSpec(cnt.shape, lambda i: (0, 0)), blk, blk],
        out_specs=blk,
        out_shape=jax.ShapeDtypeStruct((TOP_K, n), jnp.int32),
        compiler_params=_params("arbitrary"),
        name="expert_destinations",
    )(cnt, eid, rank)


def _meta_kernel(cnt_ref, meta_ref, *, nbp):
    cnt = cnt_ref[...]
    e = cnt.shape[0]
    nblk, end = _block_offsets(cnt)
    rep = lambda a, w: jnp.concatenate([a] * (w // LANES), axis=1)
    b = lax.broadcasted_iota(jnp.int32, (e, nbp), 1).astype(F32)
    blk_e = jnp.minimum(jnp.sum(jnp.where(rep(end, nbp) <= b, 1.0, 0.0), axis=0, keepdims=True), float(e - 1))
    row = lax.broadcasted_iota(jnp.int32, (e, nbp), 0).astype(F32)
    mine = row == blk_e
    left = rep(cnt + (end - nblk) * float(BLK), nbp) - b * float(BLK)
    nvalid = jnp.clip(jnp.sum(jnp.where(mine, left, 0.0), axis=0, keepdims=True), 0.0, float(BLK))
    nused = jnp.max(rep(end, nbp), axis=0, keepdims=True)
    meta_ref[...] = jnp.concatenate([blk_e, nvalid, nused, jnp.zeros((5, nbp), F32)], axis=0).astype(jnp.int32)


def _block_meta(cnt, nb):
    nbp = -(-nb // LANES) * LANES
    return pl.pallas_call(
        functools.partial(_meta_kernel, nbp=nbp),
        out_shape=jax.ShapeDtypeStruct((8, nbp), jnp.int32),
        compiler_params=pltpu.CompilerParams(vmem_limit_bytes=VMEM_LIMIT),
        name="expert_block_meta",
    )(cnt)


SC_WINDOW = 128


def _sc_mesh():
    return plsc.VectorSubcoreMesh(core_axis_name="core", subcore_axis_name="subcore")


def _sc_scatter_rows(rows, idx, nrows):
    n, width = rows.shape

    @pl.kernel(out_type=jax.ShapeDtypeStruct((nrows, width), rows.dtype), mesh=_sc_mesh())
    def scatter(rows_hbm, idx_hbm, out_hbm):
        def body(rows_vmem, idx_vmem):
            pltpu.sync_copy(rows_vmem, out_hbm.at[idx_vmem.at[0]])

        pltpu.emit_pipeline(
            body,
            grid=(n // SC_WINDOW, idx.shape[0]),
            in_specs=[pl.BlockSpec((SC_WINDOW, width), index_map=lambda i, k: (i, 0)),
                      pl.BlockSpec((1, SC_WINDOW), index_map=lambda i, k: (k, i))],
            out_specs=[],
            core_axis_name=("core", "subcore"),
            dimension_semantics=(pltpu.PARALLEL, pltpu.ARBITRARY),
        )(rows_hbm, idx_hbm)

    return scatter(rows, idx)


def _sc_gather_rows(src, idx):
    num = idx.shape[1]
    width = src.shape[1]

    @pl.kernel(out_type=jax.ShapeDtypeStruct((num, width), src.dtype), mesh=_sc_mesh())
    def gather(src_hbm, idx_hbm, out_hbm):
        def body(idx_vmem, out_vmem):
            pltpu.sync_copy(src_hbm.at[idx_vmem.at[0]], out_vmem)

        pltpu.emit_pipeline(
            body,
            grid=(num // SC_WINDOW,),
            in_specs=[pl.BlockSpec((1, SC_WINDOW), index_map=lambda i: (0, i))],
            out_specs=[pl.BlockSpec((SC_WINDOW, width), index_map=lambda i: (i, 0))],
            core_axis_name=("core", "subcore"),
            dimension_semantics=(pltpu.PARALLEL,),
        )(idx_hbm, out_hbm)

    return gather(src, idx)


def _experts_kernel(be_ref, nv_ref, nu_ref, xa_ref, xb_ref, wg_ref, wu_ref, wd_ref, oa_ref, ob_ref,
                    wgb, wub, wdb):
    b = pl.program_id(0)
    used = b < nu_ref[0]

    @pl.when(used)
    def _():
        prev = be_ref[jnp.maximum(b - 1, 0)]

        @pl.when(jnp.logical_or(b == 0, be_ref[b] != prev))
        def _():
            wgb[...] = wg_ref[...].astype(BF16)
            wub[...] = wu_ref[...].astype(BF16)
            wdb[...] = wd_ref[...].astype(BF16)

        valid = lax.broadcasted_iota(jnp.int32, (xa_ref.shape[0], 1), 0) < nv_ref[b]
        zero = jnp.uint32(0)
        x = _unpack_rows(jnp.where(valid, xa_ref[...], zero), jnp.where(valid, xb_ref[...], zero))
        act = _silu(_dot(x, wgb[...])) * _dot(x, wub[...])
        oa_ref[...], ob_ref[...] = _pack_rows(_dot(act.astype(BF16), wdb[...]))

    @pl.when(jnp.logical_not(used))
    def _():
        oa_ref[...] = jnp.zeros_like(oa_ref)
        ob_ref[...] = jnp.zeros_like(ob_ref)


def _experts(blk_e, nvalid, nused, xs_a, xs_b, wg, wu, wd):
    p, dq = xs_a.shape
    nb = p // BLK
    d, de = wg.shape[1], wg.shape[2]
    last = lambda b, be, nv, nu: jnp.minimum(b, nu[0] - 1)
    rows_in = pl.BlockSpec((BLK, dq), lambda b, be, nv, nu: (last(b, be, nv, nu), 0))
    return pl.pallas_call(
        _experts_kernel,
        grid_spec=pltpu.PrefetchScalarGridSpec(
            num_scalar_prefetch=3,
            grid=(nb,),
            in_specs=[rows_in, rows_in,
                      pl.BlockSpec((None, d, de), lambda b, be, nv, nu: (be[last(b, be, nv, nu)], 0, 0)),
                      pl.BlockSpec((None, d, de), lambda b, be, nv, nu: (be[last(b, be, nv, nu)], 0, 0)),
                      pl.BlockSpec((None, de, d), lambda b, be, nv, nu: (be[last(b, be, nv, nu)], 0, 0))],
            out_specs=[pl.BlockSpec((BLK, dq), lambda b, be, nv, nu: (b, 0))] * 2,
            scratch_shapes=[pltpu.VMEM((d, de), BF16), pltpu.VMEM((d, de), BF16), pltpu.VMEM((de, d), BF16)],
        ),
        out_shape=[jax.ShapeDtypeStruct((p, dq), U32)] * 2,
        compiler_params=_params("arbitrary"),
        name="moe_experts",
    )(blk_e, nvalid, nused, xs_a, xs_b, wg, wu, wd)


def _combine_kernel(w_ref, x1_ref, ha_ref, hb_ref, mod_ref, ga_ref, gb_ref, sg_ref, su_ref, sd_ref, gf_ref,
                    sel_ref, o_ref):
    hb = _unpack_rows(ha_ref[...], hb_ref[...])
    act = _silu(_dot(hb, sg_ref[...])) * _dot(hb, su_ref[...])
    ffn = _dot(act.astype(BF16), sd_ref[...])
    wh, wl = _split2(w_ref[...])
    acc = None
    for kk in range(TOP_K):
        sel = sel_ref[kk]
        wk = _tn(wh, sel) + _tn(wl, sel)
        a_lo, a_hi = _unpack_halves(ga_ref[kk])
        b_lo, b_hi = _unpack_halves(gb_ref[kk])
        parts = [a_lo * wk, b_lo * wk, a_hi * wk, b_hi * wk]
        acc = parts if acc is None else [p + q for p, q in zip(acc, parts)]
    ffn = ffn + jnp.concatenate(acc, axis=1)
    xo = x1_ref[...] + mod_ref[5:6, :] * ffn
    ms = jnp.mean(xo * xo, axis=-1, keepdims=True)
    o_ref[...] = xo * lax.rsqrt(ms + NORM_EPS) * gf_ref[...]


def _combine(wsel, x1, h2a, h2b, mod, ga, gb, sh_wg, sh_wu, sh_wd, normf_g, tt=256):
    bsz, seq, d = x1.shape
    n = bsz * seq
    tt = min(tt, seq)
    per = seq // tt
    dp = h2a.shape[1]
    dq = ga.shape[2]
    sel = jnp.asarray(np.broadcast_to(np.eye(TOP_K)[:, :, None], (TOP_K, TOP_K, dq)), BF16)
    consts = [sh_wg.astype(BF16), sh_wu.astype(BF16), sh_wd.astype(BF16), normf_g.reshape(1, d), sel]
    const = lambda a: pl.BlockSpec(a.shape, lambda i: (0,) * a.ndim)
    rows = pl.BlockSpec((tt, d), lambda i: (i, 0))
    packed_rows = pl.BlockSpec((tt, dp), lambda i: (i, 0))
    gathered = pl.BlockSpec((TOP_K, tt, dq), lambda i: (0, i, 0))
    return pl.pallas_call(
        _combine_kernel,
        grid=(n // tt,),
        in_specs=[pl.BlockSpec((TOP_K, tt), lambda i: (0, i)),
                  rows, packed_rows, packed_rows,
                  pl.BlockSpec((None,) + mod.shape[1:], lambda i: (i // per, 0, 0)),
                  gathered, gathered] + [const(a) for a in consts],
        out_specs=rows,
        out_shape=jax.ShapeDtypeStruct((n, d), F32),
        compiler_params=_params("arbitrary"),
        name="moe_combine",
    )(wsel, x1.reshape(n, d), h2a, h2b, mod, ga, gb, *consts)


def _moe(x1, h2a, h2b, mod, eid, wsel, exp_wg, exp_wu, exp_wd, sh_wg, sh_wu, sh_wd, normf_g):
    n = h2a.shape[0]
    e = exp_wg.shape[0]
    nb = (n * TOP_K + e * (BLK - 1)) // BLK
    rank, cnt = _expert_ranks(eid, e)
    dest = _destinations(cnt, eid, rank)
    meta = _block_meta(cnt, nb)
    xs_a = _sc_scatter_rows(h2a, dest, nb * BLK)
    xs_b = _sc_scatter_rows(h2b, dest, nb * BLK)
    ys_a, ys_b = _experts(meta[0, :nb], meta[1, :nb], meta[2, :1], xs_a, xs_b, exp_wg, exp_wu, exp_wd)
    idx = dest.reshape(1, TOP_K * n)
    ga = _sc_gather_rows(ys_a, idx).reshape(TOP_K, n, -1)
    gb = _sc_gather_rows(ys_b, idx).reshape(TOP_K, n, -1)
    return _combine(wsel, x1, h2a, h2b, mod, ga, gb, sh_wg, sh_wu, sh_wd, normf_g)


def kernel(x, c, norm1_g, norm2_g, normf_g, w_ada, b_ada, w_in, w_out, hy_conv_w, hy_conv_b, hy_pos_w1, hy_pos_b1, hy_pos_w2, hy_pos_b2, hy_pos_w3, hy_sin_freq, hy_skip, rw_mu, rw_w0, rw_w_up, rw_a0, rw_a_up, rw_g_up, rw_k_k, rw_k_a, rw_r_k, rw_ln_w, rw_ln_b, router_w, router_bias, exp_w_gate, exp_w_up, exp_w_down, sh_w_gate, sh_w_up, sh_w_down):
    bsz, seq, d = x.shape
    depth = w_ada.shape[0]
    assert depth == 1, "the final norm is fused into the last kernel of a single layer"
    for l in range(depth):
        mod = _modulation(c, w_ada[l], b_ada[l]).reshape(bsz, -1, d)
        uhy, r, k, v, kk, lw, a, g, bonus = _projection(
            x, mod, norm1_g[l], w_in[l], hy_conv_w[l], hy_conv_b[l], rw_mu[l], rw_w0[l], rw_w_up[l],
            rw_a0[l], rw_a_up[l], rw_g_up[l], rw_k_k[l], rw_k_a[l], rw_r_k[l])
        k2, ss = _hyena_filters(seq, hy_pos_w1[l], hy_pos_b1[l], hy_pos_w2[l], hy_pos_b2[l],
                                hy_pos_w3[l], hy_sin_freq[l])
        khat = _filter_spectrum(k2, ss, seq)
        z, z_col = uhy, 0
        for order in range(HYENA_ORDER):
            z = _long_conv_gate(z, z_col, uhy, (order + 1) * D_HYENA, khat, hy_skip[l], order)
            z_col = 0
        o_f, o_b = _wkv(r, k, v, kk, lw, a, rw_k_a[l])
        x1, h2a, h2b, eid, wsel = _mix_out(x, mod, z, o_f, o_b, g, bonus, rw_ln_w[l], rw_ln_b[l], w_out[l],
                                           norm2_g[l], router_w[l], router_bias[l])
        x = _moe(x1, h2a, h2b, mod, eid, wsel, exp_w_gate[l], exp_w_up[l], exp_w_down[l],
                 sh_w_gate[l], sh_w_up[l], sh_w_down[l], normf_g)
        x = x.reshape(bsz, seq, d)
    return x
```

```python
import functools
import math

import jax
import jax.numpy as jnp
import numpy as np
from jax import lax
from jax.experimental import pallas as pl
from jax.experimental.pallas import tpu as pltpu
from jax.experimental.pallas import tpu_sc as plsc

F32 = jnp.float32
BF16 = jnp.bfloat16

LANES = 128
MXU_DIM = 256
VMEM_LIMIT = 56 * 1024 * 1024

D_HYENA = 512
D_RWKV = 512
HEAD = 64
N_HEADS = D_RWKV // HEAD
HYENA_ORDER = 2
FILTER_BANDS = 16
DECAY_TARGET = 1e-2
FAST_DECAY_PCT = 0.3
SLOW_DECAY_PCT = 1.5
FILTER_NORM_EPS = 1e-6
DECAY_LORA = 32
ICLR_LORA = 32
GATE_LORA = 96
GN_EPS = 64e-5
NORM_EPS = 1e-6
N_EXPERTS = 256
TOP_K = 8
N_GROUPS = 8
TOPK_GROUPS = 4
ROUTE_SCALE = 2.5
D_EXPERT = 256


def _params(*sem):
    return pltpu.CompilerParams(dimension_semantics=sem, vmem_limit_bytes=VMEM_LIMIT)


def _split2(a):
    hi = a.astype(BF16)
    lo = (a - hi.astype(F32)).astype(BF16)
    return hi, lo


def _dot(a, b):
    return jnp.dot(a, b, preferred_element_type=F32)


def _dot3(a, b):
    ah, al = _split2(a)
    bh, bl = _split2(b)
    return _dot(ah, bh) + (_dot(ah, bl) + _dot(al, bh))


def _dot_exact_rhs(a, b_bf16):
    ah, al = _split2(a)
    return _dot(ah, b_bf16) + _dot(al, b_bf16)


def _silu(x):
    return x * jax.nn.sigmoid(x)


U32 = jnp.uint32


def _pack_halves(x):
    w = x.shape[1] // 2
    rounded = x.astype(BF16).astype(F32)
    bits = lax.bitcast_convert_type(rounded, U32)
    return (bits[:, w:] & jnp.uint32(0xFFFF0000)) | (bits[:, :w] >> 16)


def _unpack_halves(p):
    lo = lax.bitcast_convert_type(p << 16, F32)
    hi = lax.bitcast_convert_type(p & jnp.uint32(0xFFFF0000), F32)
    return lo, hi


def _pack_rows(x):
    packed = _pack_halves(x)
    half = packed.shape[1] // 2
    return packed[:, :half], packed[:, half:]


def _unpack_rows(a, b):
    a_lo, a_hi = _unpack_halves(a)
    b_lo, b_hi = _unpack_halves(b)
    return jnp.concatenate([a_lo.astype(BF16), b_lo.astype(BF16), a_hi.astype(BF16), b_hi.astype(BF16)], axis=1)


def _mod_kernel(c_ref, w_ref, b_ref, o_ref):
    o_ref[...] = _dot3(_silu(c_ref[...]), w_ref[...]) + b_ref[...]


def _modulation(c, w_ada, b_ada):
    bsz, d = c.shape
    n = w_ada.shape[1]
    blk = 1024
    return pl.pallas_call(
        _mod_kernel,
        grid=(n // blk,),
        in_specs=[
            pl.BlockSpec((bsz, d), lambda j: (0, 0)),
            pl.BlockSpec((d, blk), lambda j: (0, j)),
            pl.BlockSpec((1, blk), lambda j: (0, j)),
        ],
        out_specs=pl.BlockSpec((bsz, blk), lambda j: (0, j)),
        out_shape=jax.ShapeDtypeStruct((bsz, n), F32),
        compiler_params=_params("arbitrary"),
        name="adaln_mod",
    )(c, w_ada, b_ada.reshape(1, n))


def _filter_kernel(band_ref, w1_ref, b1_ref, w2_ref, b2_ref, w3_ref, freq_ref, delta_ref,
                   k_ref, ss_ref, *, seq, rows):
    half = pl.program_id(0)
    i = pl.program_id(1)
    r = lax.broadcasted_iota(jnp.int32, (rows, LANES), 0) + i * rows
    pos = jnp.where(half == 0, r, seq - r).astype(F32)
    tt = pos / float(max(seq - 1, 1))
    lane = lax.broadcasted_iota(jnp.int32, (rows, LANES), 1)
    ang = pos * band_ref[...]
    feats = jnp.where(lane == 0, tt,
                      jnp.where(lane <= FILTER_BANDS, jnp.cos(ang),
                                jnp.where(lane <= 2 * FILTER_BANDS, -jnp.sin(ang), 0.0)))
    freq = freq_ref[...]
    hdn = jnp.sin(freq * (_dot3(feats, w1_ref[...]) + b1_ref[...]))
    for j in range(w2_ref.shape[0]):
        hdn = jnp.sin(freq * (_dot3(hdn, w2_ref[j]) + b2_ref[j]))
    filt = _dot3(hdn, w3_ref[...])
    filt = filt * jnp.exp(-tt[:, :1] * delta_ref[...])
    valid = jnp.logical_or(half == 0, r[:, :1] > 0)
    filt = jnp.where(valid, filt, 0.0)
    k_ref[...] = filt

    @pl.when(jnp.logical_and(half == 0, i == 0))
    def _():
        ss_ref[...] = jnp.zeros_like(ss_ref)

    ss_ref[...] += jnp.broadcast_to(jnp.sum(filt * filt, axis=0, keepdims=True), ss_ref.shape)


def _hyena_filters(seq, pw1, pb1, pw2, pb2, pw3, freq):
    width = pw1.shape[1]
    ncol = HYENA_ORDER * D_HYENA
    rows = min(seq, 512)
    bands = np.zeros((1, LANES), np.float64)
    lin = np.linspace(1e-4, FILTER_BANDS - 1, FILTER_BANDS)
    bands[0, 1:1 + FILTER_BANDS] = lin
    bands[0, 1 + FILTER_BANDS:1 + 2 * FILTER_BANDS] = lin
    bands = jnp.asarray(bands * (2.0 * math.pi / seq), F32)
    deltas = np.abs(np.linspace(math.log(DECAY_TARGET) / SLOW_DECAY_PCT,
                                math.log(DECAY_TARGET) / FAST_DECAY_PCT, D_HYENA))
    deltas = jnp.asarray(np.tile(deltas, HYENA_ORDER)[None], F32)
    w1 = jnp.zeros((LANES, width), F32).at[:pw1.shape[0]].set(pw1)
    w3 = pw3.reshape(width, HYENA_ORDER, 2, D_HYENA).transpose(2, 0, 1, 3).reshape(2, width, ncol)
    nt = seq // rows
    full = lambda *shape: pl.BlockSpec(shape, lambda h, i: (0,) * len(shape))
    return pl.pallas_call(
        functools.partial(_filter_kernel, seq=seq, rows=rows),
        grid=(2, nt),
        in_specs=[
            full(1, LANES), full(LANES, width), full(1, width),
            full(pw2.shape[0], width, width), full(pw2.shape[0], 1, width),
            pl.BlockSpec((None, width, ncol), lambda h, i: (h, 0, 0)),
            full(1, width), full(1, ncol),
        ],
        out_specs=[
            pl.BlockSpec((rows, ncol), lambda h, i: (h * nt + i, 0)),
            pl.BlockSpec((8, ncol), lambda h, i: (0, 0)),
        ],
        out_shape=[jax.ShapeDtypeStruct((2 * seq, ncol), F32),
                   jax.ShapeDtypeStruct((8, ncol), F32)],
        compiler_params=_params("arbitrary", "arbitrary"),
        name="hyena_filters",
    )(bands, w1, pb1.reshape(1, width), pw2, pb2.reshape(pw2.shape[0], 1, width), w3,
      freq.reshape(1, width), deltas)


N1 = LANES
UNROLL = 8


def _dft_tables(seq):
    m = 2 * seq
    n2 = m // N1
    n2h = n2 // 2
    n1 = np.arange(N1)[:, None, None]
    f2 = np.arange(n2)[None, :, None]
    k2 = np.arange(n2)[None, None, :]
    th = 2.0 * np.pi * (n1 * f2 / m + (k2 * f2 % n2) / n2)
    fwd_a = np.concatenate([np.cos(th), -np.sin(th)], axis=1)
    tht = np.transpose(th, (0, 2, 1))
    inv_a = np.concatenate([np.cos(tht), -np.sin(tht)], axis=2)[:, :n2h] / m
    a = np.arange(N1)
    ph = 2.0 * np.pi * np.outer(a, a) / N1
    c, s = np.cos(ph), np.sin(ph)
    fwd_b = np.block([[c, s], [-s, c]])
    inv_b = np.block([[c, -s], [s, c]])
    cast = lambda t: jnp.asarray(t, BF16)
    return cast(fwd_a), cast(fwd_a[:, :, :n2h]), cast(inv_a), cast(fwd_b), cast(inv_b), n2, n2h


def _stage_a_fwd(x_ref, wa_ref, y_ref, n2, scale=None):
    def body(i, carry):
        trips = [i * UNROLL + j for j in range(UNROLL)]
        xs = [x_ref[pl.ds(n1, wa_ref.shape[2], stride=N1), :] for n1 in trips]
        if scale is not None:
            xs = [x * scale for x in xs]
        prods = [_dot(wa_ref[n1], x.astype(BF16)) for n1, x in zip(trips, xs)]
        for n1, a in zip(trips, prods):
            y_ref[pl.ds(n1, n2, stride=2 * N1), :] = a[:n2]
            y_ref[pl.ds(N1 + n1, n2, stride=2 * N1), :] = a[n2:]
        return carry
    lax.fori_loop(0, N1 // UNROLL, body, 0)


def _filter_fft_kernel(k_ref, ss_ref, wa_ref, fb_ref, o_ref, y_ref, *, n2):
    scale = lax.rsqrt(ss_ref[0:1, :] + FILTER_NORM_EPS)
    _stage_a_fwd(k_ref, wa_ref, y_ref, n2, scale=scale)

    unr = min(UNROLL, n2)

    def body(i, carry):
        trips = [i * unr + j for j in range(unr)]
        ys = [y_ref[pl.ds(pl.multiple_of(f2 * 2 * N1, 2 * N1), 2 * N1), :].astype(BF16) for f2 in trips]
        for f2, y in zip(trips, ys):
            o_ref[f2] = _dot(fb_ref[...], y)
        return carry
    lax.fori_loop(0, n2 // unr, body, 0)


def _filter_spectrum(k2, ss, seq):
    fwd_a, _, _, fwd_b, _, n2, _ = _dft_tables(seq)
    ncol = k2.shape[1]
    nblk = ncol // LANES
    return pl.pallas_call(
        functools.partial(_filter_fft_kernel, n2=n2),
        grid=(nblk,),
        in_specs=[
            pl.BlockSpec((2 * seq, LANES), lambda c: (0, c)),
            pl.BlockSpec((8, LANES), lambda c: (0, c)),
            pl.BlockSpec(fwd_a.shape, lambda c: (0, 0, 0)),
            pl.BlockSpec(fwd_b.shape, lambda c: (0, 0)),
        ],
        out_specs=pl.BlockSpec((None, n2, 2 * N1, LANES), lambda c: (c, 0, 0, 0)),
        out_shape=jax.ShapeDtypeStruct((nblk, n2, 2 * N1, LANES), F32),
        scratch_shapes=[pltpu.VMEM((n2 * 2 * N1, LANES), F32)],
        compiler_params=_params("arbitrary"),
        name="hyena_filter_fft",
    )(k2, ss, fwd_a, fwd_b)


def _conv_kernel(u_ref, g_ref, kh_ref, skip_ref, wa_ref, va_ref, fb_ref, ib_ref, o_ref, y_ref,
                 *, n2, n2h):
    _stage_a_fwd(u_ref, wa_ref, y_ref, n2)

    unr = min(UNROLL, n2)

    def mid(i, carry):
        trips = [i * unr + j for j in range(unr)]
        offs = [pl.multiple_of(f2 * 2 * N1, 2 * N1) for f2 in trips]
        zs = [_dot(fb_ref[...], y_ref[pl.ds(off, 2 * N1), :].astype(BF16)) for off in offs]
        ps = []
        for f2, z in zip(trips, zs):
            zr, zi = z[:N1], z[N1:]
            kh = kh_ref[f2]
            kr, ki = kh[:N1], kh[N1:]
            ps.append(jnp.concatenate([zr * kr - zi * ki, zr * ki + zi * kr], axis=0).astype(BF16))
        gs = [_dot(ib_ref[...], p) for p in ps]
        for off, g in zip(offs, gs):
            y_ref[pl.ds(off, 2 * N1), :] = g
        return carry
    lax.fori_loop(0, n2 // unr, mid, 0)

    skip = skip_ref[...]

    def last(i, carry):
        trips = [i * UNROLL + j for j in range(UNROLL)]
        gs = [jnp.concatenate([y_ref[pl.ds(n1, n2, stride=2 * N1), :],
                               y_ref[pl.ds(N1 + n1, n2, stride=2 * N1), :]], axis=0).astype(BF16)
              for n1 in trips]
        convs = [_dot(va_ref[n1], g) for n1, g in zip(trips, gs)]
        for n1, conv in zip(trips, convs):
            rows = pl.ds(n1, n2h, stride=N1)
            u = u_ref[rows, :]
            o_ref[rows, :] = g_ref[rows, :] * (conv + u * skip)
        return carry
    lax.fori_loop(0, N1 // UNROLL, last, 0)


def _long_conv_gate(u, u_col, gate, gate_col, khat, skip, order):
    bsz, seq, _ = u.shape
    ch = D_HYENA
    _, fwd_a, inv_a, fwd_b, inv_b, n2, n2h = _dft_tables(seq)
    nblk = ch // LANES
    const = lambda a: pl.BlockSpec(a.shape, lambda c, b: (0,) * a.ndim)
    at = lambda col: pl.BlockSpec((None, seq, LANES), lambda c, b: (b, 0, col // LANES + c))
    data = at(0)
    return pl.pallas_call(
        functools.partial(_conv_kernel, n2=n2, n2h=n2h),
        grid=(nblk, bsz),
        in_specs=[
            at(u_col), at(gate_col),
            pl.BlockSpec((None, n2, 2 * N1, LANES), lambda c, b: (order * nblk + c, 0, 0, 0)),
            pl.BlockSpec((1, LANES), lambda c, b: (0, c)),
            const(fwd_a), const(inv_a), const(fwd_b), const(inv_b),
        ],
        out_specs=data,
        out_shape=jax.ShapeDtypeStruct((bsz, seq, ch), F32),
        scratch_shapes=[pltpu.VMEM((n2 * 2 * N1, LANES), F32)],
        compiler_params=_params("arbitrary", "arbitrary"),
        name=f"hyena_conv{order}",
    )(u, gate, khat, skip[order].reshape(1, ch), fwd_a, inv_a, fwd_b, inv_b)


HALO = 8


def _shift_rows(p, k):
    return pltpu.roll(p, k % p.shape[0], axis=0)


def _proj_kernel(xp_ref, x_ref, xn_ref, mod_ref, g1_ref, why_ref, wrkv_ref, wlora_ref,
                 cw_ref, cb_ref, murkv_ref, mulora_ref, w0_ref, a0_ref, wwa_ref, gup_ref,
                 kk_ref, ka_ref, rk_ref, ones_ref,
                 uhy_ref, rkvk_ref, lwa_ref, g_ref, bonus_ref,
                 *, tt, nt):
    i = pl.program_id(1)
    xe = jnp.concatenate([xp_ref[...], x_ref[...], xn_ref[...]], axis=0)
    ms = jnp.mean(xe * xe, axis=-1, keepdims=True)
    h = xe * lax.rsqrt(ms + NORM_EPS) * g1_ref[...]
    h = h * (1.0 + mod_ref[1:2, :]) + mod_ref[0:1, :]
    row = lax.broadcasted_iota(jnp.int32, (tt + 2 * HALO, 1), 0)
    inside = jnp.logical_and(jnp.logical_or(row >= HALO, i > 0),
                             jnp.logical_or(row < tt + HALO, i < nt - 1))
    hb = jnp.where(inside, h, 0.0).astype(BF16)
    mid = slice(HALO, tt + HALO)

    p = _dot(hb, why_ref[...])
    u = (_shift_rows(p, 1) * cw_ref[0:1, :] + p * cw_ref[1:2, :]
         + _shift_rows(p, -1) * cw_ref[2:3, :] + cb_ref[...])
    uhy_ref[...] = u[mid]

    p = _dot(hb, wrkv_ref[...])
    p = p + murkv_ref[...] * (0.5 * (_shift_rows(p, 1) + _shift_rows(p, -1)) - p)
    p = p[mid]
    c = D_RWKV
    r, k, v = p[:, :c], p[:, c:2 * c], p[:, 2 * c:]
    rkvk_ref[:, :3 * c] = p

    q = _dot(hb, wlora_ref[...])
    q = q + mulora_ref[...] * (0.5 * (_shift_rows(q, 1) + _shift_rows(q, -1)) - q)
    q = q[mid]
    wa = q[:, :LANES]
    lane = lax.broadcasted_iota(jnp.int32, wa.shape, 1)
    wa = jnp.where(lane < 2 * DECAY_LORA, jnp.tanh(wa), wa)
    up = _dot3(wa, wwa_ref[...])
    z = -(w0_ref[...] + up[:, :2 * c])
    softplus = jnp.maximum(z, 0.0) + jnp.log1p(jnp.exp(-jnp.abs(z)))
    lw = -jnp.exp(-softplus - 0.5)
    a = jax.nn.sigmoid(a0_ref[...] + up[:, 2 * c:])
    for dd in range(2):
        lwa_ref[:, 2 * dd * c:(2 * dd + 1) * c] = lw[:, dd * c:(dd + 1) * c]
        lwa_ref[:, (2 * dd + 1) * c:(2 * dd + 2) * c] = a[:, dd * c:(dd + 1) * c]
    g_ref[...] = _dot3(jax.nn.sigmoid(q[:, LANES:]), gup_ref[...])

    ones = ones_ref[...]
    kk = k * kk_ref[...]
    nrm = jnp.sqrt(_dot_exact_rhs(kk * kk, ones))
    rkvk_ref[:, 3 * c:] = kk / jnp.maximum(nrm, 1e-12)
    ka = ka_ref[...]
    ksum = k * (2.0 + (a[:, :c] + a[:, c:] - 2.0) * ka)
    bonus_ref[...] = _dot_exact_rhs(r * ksum * rk_ref[...], ones) * v


def _head_ones():
    hid = np.arange(D_RWKV) // HEAD
    return jnp.asarray(hid[:, None] == hid[None, :], BF16)


def _projection(x, mod, norm1_g, w_in, hy_conv_w, hy_conv_b, rw_mu, rw_w0, rw_w_up, rw_a0,
                rw_a_up, rw_g_up, rw_k_k, rw_k_a, rw_r_k, tt=256):
    bsz, seq, d = x.shape
    tt = min(tt, seq)
    nt = seq // tt
    c = D_RWKV
    hy = (HYENA_ORDER + 1) * D_HYENA
    nlora = 2 * LANES
    w_hy = w_in[:, :hy].astype(BF16)
    w_rkv = w_in[:, hy:hy + 3 * c].astype(BF16)
    w_lora = jnp.zeros((d, nlora), F32).at[:, :w_in.shape[1] - hy - 3 * c].set(w_in[:, hy + 3 * c:]).astype(BF16)
    mu_rkv = rw_mu[:3 * c].reshape(1, 3 * c)
    mu_lora = jnp.zeros((1, nlora), F32).at[0, :rw_mu.shape[0] - 3 * c].set(rw_mu[3 * c:])
    wwa = jnp.zeros((LANES, 4 * c), F32)
    for dd in range(2):
        wwa = wwa.at[dd * DECAY_LORA:(dd + 1) * DECAY_LORA, dd * c:(dd + 1) * c].set(rw_w_up[dd])
        wwa = wwa.at[2 * DECAY_LORA + dd * ICLR_LORA:2 * DECAY_LORA + (dd + 1) * ICLR_LORA,
                     2 * c + dd * c:2 * c + (dd + 1) * c].set(rw_a_up[dd])
    gup = jnp.zeros((LANES, c), F32).at[:GATE_LORA].set(rw_g_up)
    row = lambda a: a.reshape(1, -1)

    nb8 = seq // HALO
    tb = tt // HALO
    const = lambda a: pl.BlockSpec(a.shape, lambda b, i: (0,) * a.ndim)
    tile = lambda w: pl.BlockSpec((None, tt, w), lambda b, i: (b, i, 0))
    ins = [
        (x, pl.BlockSpec((None, HALO, d), lambda b, i: (b, jnp.maximum(i * tb - 1, 0), 0))),
        (x, pl.BlockSpec((None, tt, d), lambda b, i: (b, i, 0))),
        (x, pl.BlockSpec((None, HALO, d), lambda b, i: (b, jnp.minimum((i + 1) * tb, nb8 - 1), 0))),
        (mod, pl.BlockSpec((None,) + mod.shape[1:], lambda b, i: (b, 0, 0))),
    ]
    consts = [row(norm1_g), w_hy, w_rkv, w_lora, hy_conv_w, row(hy_conv_b), mu_rkv, mu_lora,
              row(rw_w0), row(rw_a0), wwa, gup, row(rw_k_k), row(rw_k_a), row(rw_r_k), _head_ones()]
    ins += [(a, const(a)) for a in consts]
    widths = [hy, 4 * c, 4 * c, c, c]
    return pl.pallas_call(
        functools.partial(_proj_kernel, tt=tt, nt=nt),
        grid=(bsz, nt),
        in_specs=[s for _, s in ins],
        out_specs=[tile(w) for w in widths],
        out_shape=[jax.ShapeDtypeStruct((bsz, seq, w), F32) for w in widths],
        compiler_params=_params("arbitrary", "arbitrary"),
        name="input_projection",
    )(*[a for a, _ in ins])


CHUNK = HEAD
GROUP = MXU_DIM // HEAD


def _nt(a, b):
    return lax.dot_general(a, b, (((1,), (1,)), ((), ())), preferred_element_type=F32)


def _tn(a, b):
    return lax.dot_general(a, b, (((0,), (0,)), ((), ())), preferred_element_type=F32)


def _wkv_direction(r, k, v, kk, lw, a, ka, s_ref, reverse):
    c = CHUNK
    ti = lax.broadcasted_iota(jnp.int32, (c, c), 0)
    si = lax.broadcasted_iota(jnp.int32, (c, c), 1)
    tri = (si >= ti) if reverse else (si <= ti)
    cum = _dot_exact_rhs_lhs(jnp.where(tri, 1.0, 0.0).astype(BF16), lw)
    tot = jnp.sum(lw, axis=0, keepdims=True)
    w_incl = jnp.exp(cum)
    w_prev = jnp.exp(cum - lw)
    w_inv = jnp.exp(-cum)
    w_end = jnp.exp(tot - cum)
    w_tot = jnp.exp(tot)
    kd = k * (1.0 + (a - 1.0) * ka)
    b = kk * a
    a_w = -kk * w_prev
    r_w = r * w_incl
    b_w = b * w_inv
    k_w = kd * w_inv
    b_e = b * w_end
    k_e = kd * w_end

    m = MXU_DIM
    ri = lax.broadcasted_iota(jnp.int32, (m, m), 0)
    ci = lax.broadcasted_iota(jnp.int32, (m, m), 1)
    head_mask = (ri // HEAD) == (ci // HEAD)
    tl = lax.broadcasted_iota(jnp.int32, (c, m), 0)
    sl = lax.broadcasted_iota(jnp.int32, (c, m), 1) % c
    strict = (sl > tl) if reverse else (sl < tl)
    incl = (sl >= tl) if reverse else (sl <= tl)
    eye = jnp.where(sl == tl, 1.0, 0.0)
    both = lambda top, bot: jnp.concatenate([top, bot], axis=0)

    def stack(xg):
        xb = xg.astype(BF16)
        return jnp.where(head_mask, jnp.concatenate([xb] * GROUP, axis=0), jnp.zeros((), BF16))

    streams = []
    for g in range(D_RWKV // m):
        sl_g = slice(g * m, (g + 1) * m)
        streams.append(dict(
            ar=both(a_w[:, sl_g], r_w[:, sl_g]).astype(BF16),
            b_st=stack(b_w[:, sl_g]), k_st=stack(k_w[:, sl_g]), v_st=stack(v[:, sl_g]),
            v=v[:, sl_g], bk=both(b_e[:, sl_g], k_e[:, sl_g]).astype(BF16),
            w_tot=w_tot[:, sl_g], s_ref=s_ref.at[g],
            strict=strict, incl=incl, eye=eye, head_mask=head_mask, stack=stack))
    return streams


def _wkv_streams_step(streams):
    c = CHUNK
    both = lambda top, bot: jnp.concatenate([top, bot], axis=0)
    for st in streams:
        st["s"] = st["s_ref"][...]
        st["xb"] = _nt(st["ar"], st["b_st"])
        st["xk"] = _nt(st["ar"], st["k_st"])
        st["xs"] = _nt(st["ar"], st["s"].astype(BF16))
    for st in streams:
        m_ak = jnp.where(st["strict"], st["xk"][:c], 0.0)
        st["rhs"] = st["xs"][:c] + _dot(m_ak.astype(BF16), st["v_st"])
        st["pw"] = jnp.where(st["strict"], st["xb"][:c], 0.0)
        st["t"] = st["eye"] + st["pw"]
        st["p_st"] = st["stack"](st["pw"])
    for _ in range(int(math.log2(c)) - 1):
        for st in streams:
            st["pw"] = _dot(st["pw"].astype(BF16), st["p_st"])
            st["p_st"] = st["stack"](st["pw"])
        for st in streams:
            st["t"] = st["t"] + _dot(st["t"].astype(BF16), st["p_st"])
    for st in streams:
        st["u"] = _dot(st["t"].astype(BF16), st["stack"](st["rhs"]))
    outs = []
    for st in streams:
        m_rb = jnp.where(st["incl"], st["xb"][c:], 0.0)
        m_rk = jnp.where(st["incl"], st["xk"][c:], 0.0)
        outs.append(st["xs"][c:] + _dot(m_rb.astype(BF16), st["stack"](st["u"]))
                    + _dot(m_rk.astype(BF16), st["v_st"]))
        uv = both(st["u"], st["v"]).astype(BF16)
        st["s_ref"][...] = st["s"] * st["w_tot"] + jnp.where(st["head_mask"], _tn(uv, st["bk"]), 0.0)
    return outs


def _dot_exact_rhs_lhs(tri_bf16, x):
    xh, xl = _split2(x)
    return _dot(tri_bf16, xh) + _dot(tri_bf16, xl)


def _wkv_kernel(rkvk_f, lwa_f, rkvk_b, lwa_b, ka_ref, of_ref, ob_ref, s_ref, *, nch):
    @pl.when(pl.program_id(1) == 0)
    def _():
        s_ref[...] = jnp.zeros_like(s_ref)

    ka = ka_ref[...]
    c = D_RWKV

    def operands(rkvk_ref, lwa_ref, rows):
        x = rkvk_ref[rows, :]
        la = lwa_ref[rows, :]
        return x[:, :c], x[:, c:2 * c], x[:, 2 * c:3 * c], x[:, 3 * c:], la[:, :c], la[:, c:]

    for ci in range(nch):
        rows_f = slice(ci * CHUNK, (ci + 1) * CHUNK)
        rows_b = slice((nch - 1 - ci) * CHUNK, (nch - ci) * CHUNK)
        fwd = _wkv_direction(*operands(rkvk_f, lwa_f, rows_f), ka, s_ref.at[0], False)
        bwd = _wkv_direction(*operands(rkvk_b, lwa_b, rows_b), ka, s_ref.at[1], True)
        outs = _wkv_streams_step(fwd + bwd)
        of_ref[rows_f, :] = jnp.concatenate(outs[:len(fwd)], axis=1)
        ob_ref[rows_b, :] = jnp.concatenate(outs[len(fwd):], axis=1)


WKV_CHUNKS_PER_STEP = 2


def _wkv(rkvk, lwa, rw_k_a):
    bsz, seq, _ = rkvk.shape
    c = D_RWKV
    nch = WKV_CHUNKS_PER_STEP if seq % (WKV_CHUNKS_PER_STEP * CHUNK) == 0 else 1
    rows = nch * CHUNK
    nb = seq // rows
    fwd = lambda w, lane_blk: pl.BlockSpec((None, rows, w), lambda b, j: (b, j, lane_blk))
    bwd = lambda w, lane_blk: pl.BlockSpec((None, rows, w), lambda b, j: (b, nb - 1 - j, lane_blk))
    return pl.pallas_call(
        functools.partial(_wkv_kernel, nch=nch),
        grid=(bsz, nb),
        in_specs=[fwd(4 * c, 0), fwd(2 * c, 0), bwd(4 * c, 0), bwd(2 * c, 1),
                  pl.BlockSpec((1, c), lambda b, j: (0, 0))],
        out_specs=[fwd(c, 0), bwd(c, 0)],
        out_shape=[jax.ShapeDtypeStruct((bsz, seq, c), F32)] * 2,
        scratch_shapes=[pltpu.VMEM((2, c // MXU_DIM, MXU_DIM, MXU_DIM), F32)],
        compiler_params=_params("arbitrary", "arbitrary"),
        name="wkv7_chunked",
    )(rkvk, lwa, rkvk, lwa, rw_k_a.reshape(1, c))


NEG_INF = float("-inf")


def _first_max(vals, idx, size):
    m = jnp.max(vals, axis=0, keepdims=True)
    i = jnp.min(jnp.where(vals == m, idx, size), axis=0, keepdims=True)
    return m, i


def _route(scores, biased):
    e, tt = scores.shape
    per = e // N_GROUPS
    rowl = lax.broadcasted_iota(jnp.int32, (per, tt), 0)
    gs = []
    for g in range(N_GROUPS):
        blk = biased[g * per:(g + 1) * per]
        m1, i1 = _first_max(blk, rowl, per)
        m2 = jnp.max(jnp.where(rowl == i1, NEG_INF, blk), axis=0, keepdims=True)
        gs.append(m1 + m2)
    cur = jnp.concatenate(gs, axis=0)
    growl = lax.broadcasted_iota(jnp.int32, (N_GROUPS, tt), 0)
    gsel = jnp.zeros((N_GROUPS, tt), F32)
    for _ in range(TOPK_GROUPS):
        _, ig = _first_max(cur, growl, N_GROUPS)
        hit = growl == ig
        gsel = jnp.where(hit, 1.0, gsel)
        cur = jnp.where(hit, NEG_INF, cur)
    emask = jnp.concatenate([jnp.broadcast_to(gsel[g:g + 1], (per, tt)) for g in range(N_GROUPS)], axis=0)
    masked = jnp.where(emask > 0.5, biased, NEG_INF)
    row = lax.broadcasted_iota(jnp.int32, (e, tt), 0)
    ids, ws = [], []
    for _ in range(TOP_K):
        _, ie = _first_max(masked, row, e)
        hit = row == ie
        ids.append(ie)
        ws.append(jnp.sum(jnp.where(hit, scores, 0.0), axis=0, keepdims=True))
        masked = jnp.where(hit, NEG_INF, masked)
    w = jnp.concatenate(ws, axis=0)
    w = w / jnp.sum(w, axis=0, keepdims=True) * ROUTE_SCALE
    return jnp.concatenate(ids, axis=0), w


def _mixout_kernel(x_ref, mod_ref, yhy_ref, of_ref, ob_ref, g_ref, bonus_ref, lnw_ref, lnb_ref,
                   ones_ref, wout_ref, g2n_ref, rwt_ref, bias_ref,
                   x1_ref, h2a_ref, h2b_ref, eid_ref, wsel_ref):
    ones = ones_ref[...]
    s = of_ref[...] + ob_ref[...]
    mean = _dot_exact_rhs(s, ones) * (1.0 / HEAD)
    dlt = s - mean
    var = _dot_exact_rhs(dlt * dlt, ones) * (1.0 / HEAD)
    sn = dlt * lax.rsqrt(var + GN_EPS) * lnw_ref[...] + lnb_ref[...]
    yrw = (sn + bonus_ref[...]) * g_ref[...]
    ch = yhy_ref.shape[-1]
    mix = _dot(yhy_ref[...].astype(BF16), wout_ref[:ch, :]) + _dot(yrw.astype(BF16), wout_ref[ch:, :])
    x1 = x_ref[...] + mod_ref[2:3, :] * mix
    x1_ref[...] = x1
    ms = jnp.mean(x1 * x1, axis=-1, keepdims=True)
    h2 = x1 * lax.rsqrt(ms + NORM_EPS) * g2n_ref[...]
    h2 = h2 * (1.0 + mod_ref[4:5, :]) + mod_ref[3:4, :]
    h2a_ref[...], h2b_ref[...] = _pack_rows(h2)
    rh, rl = _split2(rwt_ref[...])
    hh, hl = _split2(h2)
    logits = _nt(rh, hh) + (_nt(rh, hl) + _nt(rl, hh))
    scores = jax.nn.sigmoid(logits)
    ids, w = _route(scores, scores + bias_ref[...])
    eid_ref[...] = ids
    wsel_ref[...] = w


def _mix_out(x, mod, yhy, o_f, o_b, g, bonus, ln_w, ln_b, w_out, norm2_g, router_w, router_bias, tt=256):
    bsz, seq, d = x.shape
    tt = min(tt, seq)
    nt = seq // tt
    n = bsz * seq
    c = D_RWKV
    e = router_w.shape[1]
    row = lambda a: a.reshape(1, -1)
    consts = [row(ln_w), row(ln_b), _head_ones(), w_out.astype(BF16), row(norm2_g), router_w.T,
              jnp.broadcast_to(router_bias.reshape(e, 1), (e, tt))]
    const = lambda a: pl.BlockSpec(a.shape, lambda b, i: (0,) * a.ndim)
    tile = lambda w: pl.BlockSpec((None, tt, w), lambda b, i: (b, i, 0))
    flat = lambda rows, dt: jax.ShapeDtypeStruct((rows, n), dt)
    return pl.pallas_call(
        _mixout_kernel,
        grid=(bsz, nt),
        in_specs=[tile(d), pl.BlockSpec((None,) + mod.shape[1:], lambda b, i: (b, 0, 0))]
        + [tile(c)] * 5 + [const(a) for a in consts],
        out_specs=[tile(d), pl.BlockSpec((tt, d // 4), lambda b, i: (b * nt + i, 0)),
                   pl.BlockSpec((tt, d // 4), lambda b, i: (b * nt + i, 0)),
                   pl.BlockSpec((TOP_K, tt), lambda b, i: (0, b * nt + i)),
                   pl.BlockSpec((TOP_K, tt), lambda b, i: (0, b * nt + i))],
        out_shape=[jax.ShapeDtypeStruct((bsz, seq, d), F32), jax.ShapeDtypeStruct((n, d // 4), U32),
                   jax.ShapeDtypeStruct((n, d // 4), U32),
                   flat(TOP_K, jnp.int32), flat(TOP_K, F32)],
        compiler_params=_params("arbitrary", "arbitrary"),
        name="mix_out_router",
    )(x, mod, yhy, o_f, o_b, g, bonus, *consts)


BLK = 512
BLK_SHIFT = 9


def _multi_hot(eid, e):
    row = lax.broadcasted_iota(jnp.int32, (e, eid.shape[1]), 0)
    mh = jnp.zeros((e, eid.shape[1]), F32)
    for kk in range(TOP_K):
        mh = mh + jnp.where(row == eid[kk:kk + 1, :], 1.0, 0.0)
    return row, mh


def _lookup(row, eid, table):
    return jnp.concatenate(
        [jnp.sum(jnp.where(row == eid[kk:kk + 1, :], table, 0.0), axis=0, keepdims=True)
         for kk in range(TOP_K)], axis=0)


def _rank_kernel(eid_ref, rank_ref, cnt_ref, *, e):
    @pl.when(pl.program_id(0) == 0)
    def _():
        cnt_ref[...] = jnp.zeros_like(cnt_ref)

    eid = eid_ref[...]
    tt = eid.shape[1]
    row, mh = _multi_hot(eid, e)
    mhb = mh.astype(BF16)
    si = lax.broadcasted_iota(jnp.int32, (tt, tt), 0)
    ti = lax.broadcasted_iota(jnp.int32, (tt, tt), 1)
    earlier = _dot(mhb, jnp.where(si < ti, 1.0, 0.0).astype(BF16))
    cnt = cnt_ref[...]
    full = earlier + jnp.concatenate([cnt] * (tt // LANES), axis=1)
    rank_ref[...] = _lookup(row, eid, full).astype(jnp.int32)
    cnt_ref[...] = cnt + _dot(mhb, jnp.ones((tt, LANES), BF16))


def _expert_ranks(eid, e, tt=512):
    n = eid.shape[1]
    tt = min(tt, n)
    return pl.pallas_call(
        functools.partial(_rank_kernel, e=e),
        grid=(n // tt,),
        in_specs=[pl.BlockSpec((TOP_K, tt), lambda i: (0, i))],
        out_specs=[pl.BlockSpec((TOP_K, tt), lambda i: (0, i)),
                   pl.BlockSpec((e, LANES), lambda i: (0, 0))],
        out_shape=[jax.ShapeDtypeStruct((TOP_K, n), jnp.int32), jax.ShapeDtypeStruct((e, LANES), F32)],
        compiler_params=_params("arbitrary"),
        name="expert_ranks",
    )(eid)


def _block_offsets(cnt):
    e = cnt.shape[0]
    nblk = ((cnt.astype(jnp.int32) + (BLK - 1)) >> BLK_SHIFT).astype(F32)
    ri = lax.broadcasted_iota(jnp.int32, (e, e), 0)
    ci = lax.broadcasted_iota(jnp.int32, (e, e), 1)
    tril = jnp.where(ci <= ri, 1.0, 0.0).astype(BF16)
    nh, nl = _split2(nblk)
    return nblk, _dot(tril, nh) + _dot(tril, nl)


def _dest_kernel(cnt_ref, eid_ref, rank_ref, dest_ref):
    nblk, end = _block_offsets(cnt_ref[...])
    off = (end - nblk) * float(BLK)
    eid = eid_ref[...]
    tt = eid.shape[1]
    row = lax.broadcasted_iota(jnp.int32, (off.shape[0], tt), 0)
    table = jnp.concatenate([off] * (tt // LANES), axis=1)
    dest_ref[...] = _lookup(row, eid, table).astype(jnp.int32) + rank_ref[...]


def _destinations(cnt, eid, rank, tt=512):
    n = eid.shape[1]
    tt = min(tt, n)
    blk = pl.BlockSpec((TOP_K, tt), lambda i: (0, i))
    return pl.pallas_call(
        _dest_kernel,
        grid=(n // tt,),
        in_specs=[pl.BlockSpec(cnt.shape, lambda i: (0, 0)), blk, blk],
        out_specs=blk,
        out_shape=jax.ShapeDtypeStruct((TOP_K, n), jnp.int32),
        compiler_params=_params("arbitrary"),
        name="expert_destinations",
    )(cnt, eid, rank)


def _meta_kernel(cnt_ref, meta_ref, *, nbp):
    cnt = cnt_ref[...]
    e = cnt.shape[0]
    nblk, end = _block_offsets(cnt)
    rep = lambda a, w: jnp.concatenate([a] * (w // LANES), axis=1)
    b = lax.broadcasted_iota(jnp.int32, (e, nbp), 1).astype(F32)
    blk_e = jnp.minimum(jnp.sum(jnp.where(rep(end, nbp) <= b, 1.0, 0.0), axis=0, keepdims=True), float(e - 1))
    row = lax.broadcasted_iota(jnp.int32, (e, nbp), 0).astype(F32)
    mine = row == blk_e
    left = rep(cnt + (end - nblk) * float(BLK), nbp) - b * float(BLK)
    nvalid = jnp.clip(jnp.sum(jnp.where(mine, left, 0.0), axis=0, keepdims=True), 0.0, float(BLK))
    nused = jnp.max(rep(end, nbp), axis=0, keepdims=True)
    meta_ref[...] = jnp.concatenate([blk_e, nvalid, nused, jnp.zeros((5, nbp), F32)], axis=0).astype(jnp.int32)


def _block_meta(cnt, nb):
    nbp = -(-nb // LANES) * LANES
    return pl.pallas_call(
        functools.partial(_meta_kernel, nbp=nbp),
        out_shape=jax.ShapeDtypeStruct((8, nbp), jnp.int32),
        compiler_params=pltpu.CompilerParams(vmem_limit_bytes=VMEM_LIMIT),
        name="expert_block_meta",
    )(cnt)


SC_WINDOW = 128


def _sc_mesh():
    return plsc.VectorSubcoreMesh(core_axis_name="core", subcore_axis_name="subcore")


def _sc_scatter_rows(rows, idx, nrows):
    n, width = rows.shape

    @pl.kernel(out_type=jax.ShapeDtypeStruct((nrows, width), rows.dtype), mesh=_sc_mesh())
    def scatter(rows_hbm, idx_hbm, out_hbm):
        def body(rows_vmem, idx_vmem):
            pltpu.sync_copy(rows_vmem, out_hbm.at[idx_vmem.at[0]])

        pltpu.emit_pipeline(
            body,
            grid=(n // SC_WINDOW, idx.shape[0]),
            in_specs=[pl.BlockSpec((SC_WINDOW, width), index_map=lambda i, k: (i, 0)),
                      pl.BlockSpec((1, SC_WINDOW), index_map=lambda i, k: (k, i))],
            out_specs=[],
            core_axis_name=("core", "subcore"),
            dimension_semantics=(pltpu.PARALLEL, pltpu.ARBITRARY),
        )(rows_hbm, idx_hbm)

    return scatter(rows, idx)


def _sc_gather_rows(src, idx):
    num = idx.shape[1]
    width = src.shape[1]

    @pl.kernel(out_type=jax.ShapeDtypeStruct((num, width), src.dtype), mesh=_sc_mesh())
    def gather(src_hbm, idx_hbm, out_hbm):
        def body(idx_vmem, out_vmem):
            pltpu.sync_copy(src_hbm.at[idx_vmem.at[0]], out_vmem)

        pltpu.emit_pipeline(
            body,
            grid=(num // SC_WINDOW,),
            in_specs=[pl.BlockSpec((1, SC_WINDOW), index_map=lambda i: (0, i))],
            out_specs=[pl.BlockSpec((SC_WINDOW, width), index_map=lambda i: (i, 0))],
            core_axis_name=("core", "subcore"),
            dimension_semantics=(pltpu.PARALLEL,),
        )(idx_hbm, out_hbm)

    return gather(src, idx)


def _experts_kernel(be_ref, nv_ref, nu_ref, xa_ref, xb_ref, wg_ref, wu_ref, wd_ref, oa_ref, ob_ref,
                    wgb, wub, wdb):
    b = pl.program_id(0)
    used = b < nu_ref[0]

    @pl.when(used)
    def _():
        prev = be_ref[jnp.maximum(b - 1, 0)]

        @pl.when(jnp.logical_or(b == 0, be_ref[b] != prev))
        def _():
            wgb[...] = wg_ref[...].astype(BF16)
            wub[...] = wu_ref[...].astype(BF16)
            wdb[...] = wd_ref[...].astype(BF16)

        valid = lax.broadcasted_iota(jnp.int32, (xa_ref.shape[0], 1), 0) < nv_ref[b]
        zero = jnp.uint32(0)
        x = _unpack_rows(jnp.where(valid, xa_ref[...], zero), jnp.where(valid, xb_ref[...], zero))
        act = _silu(_dot(x, wgb[...])) * _dot(x, wub[...])
        oa_ref[...], ob_ref[...] = _pack_rows(_dot(act.astype(BF16), wdb[...]))

    @pl.when(jnp.logical_not(used))
    def _():
        oa_ref[...] = jnp.zeros_like(oa_ref)
        ob_ref[...] = jnp.zeros_like(ob_ref)


def _experts(blk_e, nvalid, nused, xs_a, xs_b, wg, wu, wd):
    p, dq = xs_a.shape
    nb = p // BLK
    d, de = wg.shape[1], wg.shape[2]
    last = lambda b, be, nv, nu: jnp.minimum(b, nu[0] - 1)
    rows_in = pl.BlockSpec((BLK, dq), lambda b, be, nv, nu: (last(b, be, nv, nu), 0))
    return pl.pallas_call(
        _experts_kernel,
        grid_spec=pltpu.PrefetchScalarGridSpec(
            num_scalar_prefetch=3,
            grid=(nb,),
            in_specs=[rows_in, rows_in,
                      pl.BlockSpec((None, d, de), lambda b, be, nv, nu: (be[last(b, be, nv, nu)], 0, 0)),
                      pl.BlockSpec((None, d, de), lambda b, be, nv, nu: (be[last(b, be, nv, nu)], 0, 0)),
                      pl.BlockSpec((None, de, d), lambda b, be, nv, nu: (be[last(b, be, nv, nu)], 0, 0))],
            out_specs=[pl.BlockSpec((BLK, dq), lambda b, be, nv, nu: (b, 0))] * 2,
            scratch_shapes=[pltpu.VMEM((d, de), BF16), pltpu.VMEM((d, de), BF16), pltpu.VMEM((de, d), BF16)],
        ),
        out_shape=[jax.ShapeDtypeStruct((p, dq), U32)] * 2,
        compiler_params=_params("arbitrary"),
        name="moe_experts",
    )(blk_e, nvalid, nused, xs_a, xs_b, wg, wu, wd)


def _combine_kernel(w_ref, x1_ref, ha_ref, hb_ref, mod_ref, ga_ref, gb_ref, sg_ref, su_ref, sd_ref, gf_ref,
                    sel_ref, o_ref):
    hb = _unpack_rows(ha_ref[...], hb_ref[...])
    act = _silu(_dot(hb, sg_ref[...])) * _dot(hb, su_ref[...])
    ffn = _dot(act.astype(BF16), sd_ref[...])
    wh, wl = _split2(w_ref[...])
    acc = None
    for kk in range(TOP_K):
        sel = sel_ref[kk]
        wk = _tn(wh, sel) + _tn(wl, sel)
        a_lo, a_hi = _unpack_halves(ga_ref[kk])
        b_lo, b_hi = _unpack_halves(gb_ref[kk])
        parts = [a_lo * wk, b_lo * wk, a_hi * wk, b_hi * wk]
        acc = parts if acc is None else [p + q for p, q in zip(acc, parts)]
    ffn = ffn + jnp.concatenate(acc, axis=1)
    xo = x1_ref[...] + mod_ref[5:6, :] * ffn
    ms = jnp.mean(xo * xo, axis=-1, keepdims=True)
    o_ref[...] = xo * lax.rsqrt(ms + NORM_EPS) * gf_ref[...]


def _combine(wsel, x1, h2a, h2b, mod, ga, gb, sh_wg, sh_wu, sh_wd, normf_g, tt=256):
    bsz, seq, d = x1.shape
    n = bsz * seq
    tt = min(tt, seq)
    per = seq // tt
    dp = h2a.shape[1]
    dq = ga.shape[2]
    sel = jnp.asarray(np.broadcast_to(np.eye(TOP_K)[:, :, None], (TOP_K, TOP_K, dq)), BF16)
    consts = [sh_wg.astype(BF16), sh_wu.astype(BF16), sh_wd.astype(BF16), normf_g.reshape(1, d), sel]
    const = lambda a: pl.BlockSpec(a.shape, lambda i: (0,) * a.ndim)
    rows = pl.BlockSpec((tt, d), lambda i: (i, 0))
    packed_rows = pl.BlockSpec((tt, dp), lambda i: (i, 0))
    gathered = pl.BlockSpec((TOP_K, tt, dq), lambda i: (0, i, 0))
    return pl.pallas_call(
        _combine_kernel,
        grid=(n // tt,),
        in_specs=[pl.BlockSpec((TOP_K, tt), lambda i: (0, i)),
                  rows, packed_rows, packed_rows,
                  pl.BlockSpec((None,) + mod.shape[1:], lambda i: (i // per, 0, 0)),
                  gathered, gathered] + [const(a) for a in consts],
        out_specs=rows,
        out_shape=jax.ShapeDtypeStruct((n, d), F32),
        compiler_params=_params("arbitrary"),
        name="moe_combine",
    )(wsel, x1.reshape(n, d), h2a, h2b, mod, ga, gb, *consts)


def _moe(x1, h2a, h2b, mod, eid, wsel, exp_wg, exp_wu, exp_wd, sh_wg, sh_wu, sh_wd, normf_g):
    n = h2a.shape[0]
    e = exp_wg.shape[0]
    nb = (n * TOP_K + e * (BLK - 1)) // BLK
    rank, cnt = _expert_ranks(eid, e)
    dest = _destinations(cnt, eid, rank)
    meta = _block_meta(cnt, nb)
    xs_a = _sc_scatter_rows(h2a, dest, nb * BLK)
    xs_b = _sc_scatter_rows(h2b, dest, nb * BLK)
    ys_a, ys_b = _experts(meta[0, :nb], meta[1, :nb], meta[2, :1], xs_a, xs_b, exp_wg, exp_wu, exp_wd)
    idx = dest.reshape(1, TOP_K * n)
    ga = _sc_gather_rows(ys_a, idx).reshape(TOP_K, n, -1)
    gb = _sc_gather_rows(ys_b, idx).reshape(TOP_K, n, -1)
    return _combine(wsel, x1, h2a, h2b, mod, ga, gb, sh_wg, sh_wu, sh_wd, normf_g)


def kernel(x, c, norm1_g, norm2_g, normf_g, w_ada, b_ada, w_in, w_out, hy_conv_w, hy_conv_b, hy_pos_w1, hy_pos_b1, hy_pos_w2, hy_pos_b2, hy_pos_w3, hy_sin_freq, hy_skip, rw_mu, rw_w0, rw_w_up, rw_a0, rw_a_up, rw_g_up, rw_k_k, rw_k_a, rw_r_k, rw_ln_w, rw_ln_b, router_w, router_bias, exp_w_gate, exp_w_up, exp_w_down, sh_w_gate, sh_w_up, sh_w_down):
    bsz, seq, d = x.shape
    depth = w_ada.shape[0]
    assert depth == 1, "the final norm is fused into the last kernel of a single layer"
    for l in range(depth):
        mod = _modulation(c, w_ada[l], b_ada[l]).reshape(bsz, -1, d)
        uhy, rkvk, lwa, g, bonus = _projection(
            x, mod, norm1_g[l], w_in[l], hy_conv_w[l], hy_conv_b[l], rw_mu[l], rw_w0[l], rw_w_up[l],
            rw_a0[l], rw_a_up[l], rw_g_up[l], rw_k_k[l], rw_k_a[l], rw_r_k[l])
        k2, ss = _hyena_filters(seq, hy_pos_w1[l], hy_pos_b1[l], hy_pos_w2[l], hy_pos_b2[l],
                                hy_pos_w3[l], hy_sin_freq[l])
        khat = _filter_spectrum(k2, ss, seq)
        z, z_col = uhy, 0
        for order in range(HYENA_ORDER):
            z = _long_conv_gate(z, z_col, uhy, (order + 1) * D_HYENA, khat, hy_skip[l], order)
            z_col = 0
        o_f, o_b = _wkv(rkvk, lwa, rw_k_a[l])
        x1, h2a, h2b, eid, wsel = _mix_out(x, mod, z, o_f, o_b, g, bonus, rw_ln_w[l], rw_ln_b[l], w_out[l],
                                           norm2_g[l], router_w[l], router_bias[l])
        x = _moe(x1, h2a, h2b, mod, eid, wsel, exp_w_gate[l], exp_w_up[l], exp_w_down[l],
                 sh_w_gate[l], sh_w_up[l], sh_w_down[l], normf_g)
        x = x.reshape(bsz, seq, d)
    return x
```

```python
import functools
import math

import jax
import jax.numpy as jnp
import numpy as np
from jax import lax
from jax.experimental import pallas as pl
from jax.experimental.pallas import tpu as pltpu
from jax.experimental.pallas import tpu_sc as plsc

F32 = jnp.float32
BF16 = jnp.bfloat16

LANES = 128
MXU_DIM = 256
VMEM_LIMIT = 56 * 1024 * 1024

D_HYENA = 512
D_RWKV = 512
HEAD = 64
N_HEADS = D_RWKV // HEAD
HYENA_ORDER = 2
FILTER_BANDS = 16
DECAY_TARGET = 1e-2
FAST_DECAY_PCT = 0.3
SLOW_DECAY_PCT = 1.5
FILTER_NORM_EPS = 1e-6
DECAY_LORA = 32
ICLR_LORA = 32
GATE_LORA = 96
GN_EPS = 64e-5
NORM_EPS = 1e-6
N_EXPERTS = 256
TOP_K = 8
N_GROUPS = 8
TOPK_GROUPS = 4
ROUTE_SCALE = 2.5
D_EXPERT = 256


def _params(*sem):
    return pltpu.CompilerParams(dimension_semantics=sem, vmem_limit_bytes=VMEM_LIMIT)


def _split2(a):
    hi = a.astype(BF16)
    lo = (a - hi.astype(F32)).astype(BF16)
    return hi, lo


def _dot(a, b):
    return jnp.dot(a, b, preferred_element_type=F32)


def _dot3(a, b):
    ah, al = _split2(a)
    bh, bl = _split2(b)
    return _dot(ah, bh) + (_dot(ah, bl) + _dot(al, bh))


def _dot_exact_rhs(a, b_bf16):
    ah, al = _split2(a)
    return _dot(ah, b_bf16) + _dot(al, b_bf16)


def _silu(x):
    return x * jax.nn.sigmoid(x)


U32 = jnp.uint32


def _pack_halves(x):
    w = x.shape[1] // 2
    rounded = x.astype(BF16).astype(F32)
    bits = lax.bitcast_convert_type(rounded, U32)
    return (bits[:, w:] & jnp.uint32(0xFFFF0000)) | (bits[:, :w] >> 16)


def _unpack_halves(p):
    lo = lax.bitcast_convert_type(p << 16, F32)
    hi = lax.bitcast_convert_type(p & jnp.uint32(0xFFFF0000), F32)
    return lo, hi


def _pack_rows(x):
    packed = _pack_halves(x)
    half = packed.shape[1] // 2
    return packed[:, :half], packed[:, half:]


def _unpack_rows(a, b):
    a_lo, a_hi = _unpack_halves(a)
    b_lo, b_hi = _unpack_halves(b)
    return jnp.concatenate([a_lo.astype(BF16), b_lo.astype(BF16), a_hi.astype(BF16), b_hi.astype(BF16)], axis=1)


def _mod_kernel(c_ref, w_ref, b_ref, o_ref):
    o_ref[...] = _dot3(_silu(c_ref[...]), w_ref[...]) + b_ref[...]


def _modulation(c, w_ada, b_ada):
    bsz, d = c.shape
    n = w_ada.shape[1]
    blk = 1024
    return pl.pallas_call(
        _mod_kernel,
        grid=(n // blk,),
        in_specs=[
            pl.BlockSpec((bsz, d), lambda j: (0, 0)),
            pl.BlockSpec((d, blk), lambda j: (0, j)),
            pl.BlockSpec((1, blk), lambda j: (0, j)),
        ],
        out_specs=pl.BlockSpec((bsz, blk), lambda j: (0, j)),
        out_shape=jax.ShapeDtypeStruct((bsz, n), F32),
        compiler_params=_params("arbitrary"),
        name="adaln_mod",
    )(c, w_ada, b_ada.reshape(1, n))


def _filter_kernel(band_ref, w1_ref, b1_ref, w2_ref, b2_ref, w3_ref, freq_ref, delta_ref,
                   k_ref, ss_ref, *, seq, rows):
    half = pl.program_id(0)
    i = pl.program_id(1)
    r = lax.broadcasted_iota(jnp.int32, (rows, LANES), 0) + i * rows
    pos = jnp.where(half == 0, r, seq - r).astype(F32)
    tt = pos / float(max(seq - 1, 1))
    lane = lax.broadcasted_iota(jnp.int32, (rows, LANES), 1)
    ang = pos * band_ref[...]
    feats = jnp.where(lane == 0, tt,
                      jnp.where(lane <= FILTER_BANDS, jnp.cos(ang),
                                jnp.where(lane <= 2 * FILTER_BANDS, -jnp.sin(ang), 0.0)))
    freq = freq_ref[...]
    hdn = jnp.sin(freq * (_dot3(feats, w1_ref[...]) + b1_ref[...]))
    for j in range(w2_ref.shape[0]):
        hdn = jnp.sin(freq * (_dot3(hdn, w2_ref[j]) + b2_ref[j]))
    filt = _dot3(hdn, w3_ref[...])
    filt = filt * jnp.exp(-tt[:, :1] * delta_ref[...])
    valid = jnp.logical_or(half == 0, r[:, :1] > 0)
    filt = jnp.where(valid, filt, 0.0)
    k_ref[...] = filt

    @pl.when(jnp.logical_and(half == 0, i == 0))
    def _():
        ss_ref[...] = jnp.zeros_like(ss_ref)

    ss_ref[...] += jnp.broadcast_to(jnp.sum(filt * filt, axis=0, keepdims=True), ss_ref.shape)


def _hyena_filters(seq, pw1, pb1, pw2, pb2, pw3, freq):
    width = pw1.shape[1]
    ncol = HYENA_ORDER * D_HYENA
    rows = min(seq, 512)
    bands = np.zeros((1, LANES), np.float64)
    lin = np.linspace(1e-4, FILTER_BANDS - 1, FILTER_BANDS)
    bands[0, 1:1 + FILTER_BANDS] = lin
    bands[0, 1 + FILTER_BANDS:1 + 2 * FILTER_BANDS] = lin
    bands = jnp.asarray(bands * (2.0 * math.pi / seq), F32)
    deltas = np.abs(np.linspace(math.log(DECAY_TARGET) / SLOW_DECAY_PCT,
                                math.log(DECAY_TARGET) / FAST_DECAY_PCT, D_HYENA))
    deltas = jnp.asarray(np.tile(deltas, HYENA_ORDER)[None], F32)
    w1 = jnp.zeros((LANES, width), F32).at[:pw1.shape[0]].set(pw1)
    w3 = pw3.reshape(width, HYENA_ORDER, 2, D_HYENA).transpose(2, 0, 1, 3).reshape(2, width, ncol)
    nt = seq // rows
    full = lambda *shape: pl.BlockSpec(shape, lambda h, i: (0,) * len(shape))
    return pl.pallas_call(
        functools.partial(_filter_kernel, seq=seq, rows=rows),
        grid=(2, nt),
        in_specs=[
            full(1, LANES), full(LANES, width), full(1, width),
            full(pw2.shape[0], width, width), full(pw2.shape[0], 1, width),
            pl.BlockSpec((None, width, ncol), lambda h, i: (h, 0, 0)),
            full(1, width), full(1, ncol),
        ],
        out_specs=[
            pl.BlockSpec((rows, ncol), lambda h, i: (h * nt + i, 0)),
            pl.BlockSpec((8, ncol), lambda h, i: (0, 0)),
        ],
        out_shape=[jax.ShapeDtypeStruct((2 * seq, ncol), F32),
                   jax.ShapeDtypeStruct((8, ncol), F32)],
        compiler_params=_params("arbitrary", "arbitrary"),
        name="hyena_filters",
    )(bands, w1, pb1.reshape(1, width), pw2, pb2.reshape(pw2.shape[0], 1, width), w3,
      freq.reshape(1, width), deltas)


N1 = LANES
UNROLL = 8


def _dft_tables(seq):
    tables = _dft_tables_np(seq)
    return tuple(jnp.asarray(t, BF16) for t in tables[:5]) + tables[5:]


def _dft_tables_np(seq):
    m = 2 * seq
    n2 = m // N1
    n2h = n2 // 2
    n1 = np.arange(N1)[:, None, None]
    f2 = np.arange(n2)[None, :, None]
    k2 = np.arange(n2)[None, None, :]
    th = 2.0 * np.pi * (n1 * f2 / m + (k2 * f2 % n2) / n2)
    fwd_a = np.concatenate([np.cos(th), -np.sin(th)], axis=1)
    tht = np.transpose(th, (0, 2, 1))
    inv_a = np.concatenate([np.cos(tht), -np.sin(tht)], axis=2)[:, :n2h] / m
    a = np.arange(N1)
    ph = 2.0 * np.pi * np.outer(a, a) / N1
    c, s = np.cos(ph), np.sin(ph)
    fwd_b = np.block([[c, s], [-s, c]])
    inv_b = np.block([[c, -s], [s, c]])
    return fwd_a, fwd_a[:, :, :n2h], inv_a, fwd_b, inv_b, n2, n2h


def _stage_a_fwd(x_ref, wa_ref, y_ref, n2, scale=None):
    def body(i, carry):
        trips = [i * UNROLL + j for j in range(UNROLL)]
        xs = [x_ref[pl.ds(n1, wa_ref.shape[2], stride=N1), :] for n1 in trips]
        if scale is not None:
            xs = [x * scale for x in xs]
        prods = [_dot(wa_ref[n1], x.astype(BF16)) for n1, x in zip(trips, xs)]
        for n1, a in zip(trips, prods):
            y_ref[pl.ds(n1, n2, stride=2 * N1), :] = a[:n2]
            y_ref[pl.ds(N1 + n1, n2, stride=2 * N1), :] = a[n2:]
        return carry
    lax.fori_loop(0, N1 // UNROLL, body, 0)


def _filter_fft_kernel(k_ref, ss_ref, wa_ref, fb_ref, o_ref, y_ref, *, n2):
    scale = lax.rsqrt(ss_ref[0:1, :] + FILTER_NORM_EPS)
    _stage_a_fwd(k_ref, wa_ref, y_ref, n2, scale=scale)

    unr = min(UNROLL, n2)

    def body(i, carry):
        trips = [i * unr + j for j in range(unr)]
        ys = [y_ref[pl.ds(pl.multiple_of(f2 * 2 * N1, 2 * N1), 2 * N1), :].astype(BF16) for f2 in trips]
        for f2, y in zip(trips, ys):
            o_ref[f2] = _dot(fb_ref[...], y)
        return carry
    lax.fori_loop(0, n2 // unr, body, 0)


def _filter_spectrum(k2, ss, seq):
    fwd_a, _, _, fwd_b, _, n2, _ = _dft_tables(seq)
    ncol = k2.shape[1]
    nblk = ncol // LANES
    return pl.pallas_call(
        functools.partial(_filter_fft_kernel, n2=n2),
        grid=(nblk,),
        in_specs=[
            pl.BlockSpec((2 * seq, LANES), lambda c: (0, c)),
            pl.BlockSpec((8, LANES), lambda c: (0, c)),
            pl.BlockSpec(fwd_a.shape, lambda c: (0, 0, 0)),
            pl.BlockSpec(fwd_b.shape, lambda c: (0, 0)),
        ],
        out_specs=pl.BlockSpec((None, n2, 2 * N1, LANES), lambda c: (c, 0, 0, 0)),
        out_shape=jax.ShapeDtypeStruct((nblk, n2, 2 * N1, LANES), F32),
        scratch_shapes=[pltpu.VMEM((n2 * 2 * N1, LANES), F32)],
        compiler_params=_params("arbitrary"),
        name="hyena_filter_fft",
    )(k2, ss, fwd_a, fwd_b)


TILE = 8
N1_GROUPS = N1 // TILE


def _tile_tables(seq):
    _, fwd_a, inv_a, _, _, n2, n2h = _dft_tables_np(seq)
    eye = np.eye(TILE)
    fa = fwd_a.reshape(N1_GROUPS, TILE, 2 * n2, n2h)
    wa = np.einsum("qjrn,jk->qrjnk", fa, eye).reshape(N1_GROUPS, 2 * n2 * TILE, n2h * TILE)
    ia = inv_a.reshape(N1_GROUPS, TILE, n2h, 2 * n2)
    vc = np.einsum("qjnr,jk->qnjrk", ia, eye).reshape(N1_GROUPS, n2h * TILE, 2 * n2 * TILE)
    return jnp.asarray(wa, BF16), jnp.asarray(vc, BF16)


def _conv_kernel(u_ref, g_ref, skip_ref, fb_ref, ib_ref, kh_hbm, wa_hbm, vc_hbm, o_ref,
                 y_ref, kh_ref, wa_ref, vc_ref, sem, *, n2, n2h, kh_first):
    c_id, b_id = pl.program_id(0), pl.program_id(1)

    @pl.when(jnp.logical_and(c_id == 0, b_id == 0))
    def _():
        for src, dst in ((wa_hbm, wa_ref), (vc_hbm, vc_ref)):
            cp = pltpu.make_async_copy(src, dst, sem)
            cp.start()
            cp.wait()

    @pl.when(b_id == 0)
    def _():
        cp = pltpu.make_async_copy(kh_hbm.at[kh_first + c_id], kh_ref, sem)
        cp.start()
        cp.wait()

    def y_tile(rf, base):
        ri, f2 = divmod(rf, n2)
        return pl.ds(f2 * 2 * N1 + ri * N1 + base, TILE)

    def stage_a(q, carry):
        base = pl.multiple_of(q * TILE, TILE)
        x = jnp.concatenate([u_ref[pl.ds(N1 * m + base, TILE), :] for m in range(n2h)], axis=0)
        r = _dot(wa_ref[q], x.astype(BF16))
        for rf in range(2 * n2):
            y_ref[y_tile(rf, base), :] = r[rf * TILE:(rf + 1) * TILE]
        return carry
    lax.fori_loop(0, N1_GROUPS, stage_a, 0, unroll=2)

    unr = min(UNROLL, n2)

    def mid(i, carry):
        trips = [i * unr + j for j in range(unr)]
        offs = [pl.multiple_of(f2 * 2 * N1, 2 * N1) for f2 in trips]
        zs = [_dot(fb_ref[...], y_ref[pl.ds(off, 2 * N1), :].astype(BF16)) for off in offs]
        ps = []
        for f2, z in zip(trips, zs):
            zr, zi = z[:N1], z[N1:]
            kh = kh_ref[f2]
            kr, ki = kh[:N1], kh[N1:]
            ps.append(jnp.concatenate([zr * kr - zi * ki, zr * ki + zi * kr], axis=0).astype(BF16))
        gs = [_dot(ib_ref[...], p) for p in ps]
        for off, g in zip(offs, gs):
            y_ref[pl.ds(off, 2 * N1), :] = g
        return carry
    lax.fori_loop(0, n2 // unr, mid, 0)

    skip = skip_ref[...]

    def stage_c(q, carry):
        base = pl.multiple_of(q * TILE, TILE)
        g = jnp.concatenate([y_ref[y_tile(rf, base), :] for rf in range(2 * n2)], axis=0)
        conv = _dot(vc_ref[q], g.astype(BF16))
        for m in range(n2h):
            rows = pl.ds(N1 * m + base, TILE)
            o_ref[rows, :] = g_ref[rows, :] * (conv[m * TILE:(m + 1) * TILE] + u_ref[rows, :] * skip)
        return carry
    lax.fori_loop(0, N1_GROUPS, stage_c, 0, unroll=2)


def _long_conv_gate(u, u_col, gate, gate_col, khat, skip, order):
    bsz, seq, _ = u.shape
    ch = D_HYENA
    _, _, _, fwd_b, inv_b, n2, n2h = _dft_tables(seq)
    wa, vc = _tile_tables(seq)
    nblk = ch // LANES
    const = lambda a: pl.BlockSpec(a.shape, lambda c, b: (0,) * a.ndim)
    at = lambda col: pl.BlockSpec((None, seq, LANES), lambda c, b: (b, 0, col // LANES + c))
    hbm = pl.BlockSpec(memory_space=pl.ANY)
    return pl.pallas_call(
        functools.partial(_conv_kernel, n2=n2, n2h=n2h, kh_first=order * nblk),
        grid=(nblk, bsz),
        in_specs=[
            at(u_col), at(gate_col),
            pl.BlockSpec((1, LANES), lambda c, b: (0, c)),
            const(fwd_b), const(inv_b), hbm, hbm, hbm,
        ],
        out_specs=at(0),
        out_shape=jax.ShapeDtypeStruct((bsz, seq, ch), F32),
        scratch_shapes=[pltpu.VMEM((n2 * 2 * N1, LANES), F32), pltpu.VMEM(khat.shape[1:], F32),
                        pltpu.VMEM(wa.shape, BF16), pltpu.VMEM(vc.shape, BF16), pltpu.SemaphoreType.DMA(())],
        compiler_params=_params("arbitrary", "arbitrary"),
        name=f"hyena_conv{order}",
    )(u, gate, skip[order].reshape(1, ch), fwd_b, inv_b, khat, wa, vc)


HALO = 8


def _shift_rows(p, k):
    return pltpu.roll(p, k % p.shape[0], axis=0)


def _proj_kernel(xp_ref, x_ref, xn_ref, mod_ref, g1_ref, why_ref, wrkv_ref, wlora_ref,
                 cw_ref, cb_ref, murkv_ref, mulora_ref, w0_ref, a0_ref, wwa_ref, gup_ref,
                 kk_ref, ka_ref, rk_ref, ones_ref,
                 uhy_ref, rkvk_ref, lwa_ref, g_ref, bonus_ref,
                 *, tt, nt):
    i = pl.program_id(1)
    xe = jnp.concatenate([xp_ref[...], x_ref[...], xn_ref[...]], axis=0)
    ms = jnp.mean(xe * xe, axis=-1, keepdims=True)
    h = xe * lax.rsqrt(ms + NORM_EPS) * g1_ref[...]
    h = h * (1.0 + mod_ref[1:2, :]) + mod_ref[0:1, :]
    row = lax.broadcasted_iota(jnp.int32, (tt + 2 * HALO, 1), 0)
    inside = jnp.logical_and(jnp.logical_or(row >= HALO, i > 0),
                             jnp.logical_or(row < tt + HALO, i < nt - 1))
    hb = jnp.where(inside, h, 0.0).astype(BF16)
    mid = slice(HALO, tt + HALO)

    p = _dot(hb, why_ref[...])
    u = (_shift_rows(p, 1) * cw_ref[0:1, :] + p * cw_ref[1:2, :]
         + _shift_rows(p, -1) * cw_ref[2:3, :] + cb_ref[...])
    uhy_ref[...] = u[mid]

    p = _dot(hb, wrkv_ref[...])
    p = p + murkv_ref[...] * (0.5 * (_shift_rows(p, 1) + _shift_rows(p, -1)) - p)
    p = p[mid]
    c = D_RWKV
    r, k, v = p[:, :c], p[:, c:2 * c], p[:, 2 * c:]
    rkvk_ref[:, :3 * c] = p

    q = _dot(hb, wlora_ref[...])
    q = q + mulora_ref[...] * (0.5 * (_shift_rows(q, 1) + _shift_rows(q, -1)) - q)
    q = q[mid]
    wa = q[:, :LANES]
    lane = lax.broadcasted_iota(jnp.int32, wa.shape, 1)
    wa = jnp.where(lane < 2 * DECAY_LORA, jnp.tanh(wa), wa)
    up = _dot3(wa, wwa_ref[...])
    z = -(w0_ref[...] + up[:, :2 * c])
    softplus = jnp.maximum(z, 0.0) + jnp.log1p(jnp.exp(-jnp.abs(z)))
    lw = -jnp.exp(-softplus - 0.5)
    a = jax.nn.sigmoid(a0_ref[...] + up[:, 2 * c:])
    for dd in range(2):
        lwa_ref[:, 2 * dd * c:(2 * dd + 1) * c] = lw[:, dd * c:(dd + 1) * c]
        lwa_ref[:, (2 * dd + 1) * c:(2 * dd + 2) * c] = a[:, dd * c:(dd + 1) * c]
    g_ref[...] = _dot3(jax.nn.sigmoid(q[:, LANES:]), gup_ref[...])

    ones = ones_ref[...]
    kk = k * kk_ref[...]
    nrm = jnp.sqrt(_dot_exact_rhs(kk * kk, ones))
    rkvk_ref[:, 3 * c:] = kk / jnp.maximum(nrm, 1e-12)
    ka = ka_ref[...]
    ksum = k * (2.0 + (a[:, :c] + a[:, c:] - 2.0) * ka)
    bonus_ref[...] = _dot_exact_rhs(r * ksum * rk_ref[...], ones) * v


def _head_ones():
    hid = np.arange(D_RWKV) // HEAD
    return jnp.asarray(hid[:, None] == hid[None, :], BF16)


def _projection(x, mod, norm1_g, w_in, hy_conv_w, hy_conv_b, rw_mu, rw_w0, rw_w_up, rw_a0,
                rw_a_up, rw_g_up, rw_k_k, rw_k_a, rw_r_k, tt=256):
    bsz, seq, d = x.shape
    tt = min(tt, seq)
    nt = seq // tt
    c = D_RWKV
    hy = (HYENA_ORDER + 1) * D_HYENA
    nlora = 2 * LANES
    w_hy = w_in[:, :hy].astype(BF16)
    w_rkv = w_in[:, hy:hy + 3 * c].astype(BF16)
    w_lora = jnp.zeros((d, nlora), F32).at[:, :w_in.shape[1] - hy - 3 * c].set(w_in[:, hy + 3 * c:]).astype(BF16)
    mu_rkv = rw_mu[:3 * c].reshape(1, 3 * c)
    mu_lora = jnp.zeros((1, nlora), F32).at[0, :rw_mu.shape[0] - 3 * c].set(rw_mu[3 * c:])
    wwa = jnp.zeros((LANES, 4 * c), F32)
    for dd in range(2):
        wwa = wwa.at[dd * DECAY_LORA:(dd + 1) * DECAY_LORA, dd * c:(dd + 1) * c].set(rw_w_up[dd])
        wwa = wwa.at[2 * DECAY_LORA + dd * ICLR_LORA:2 * DECAY_LORA + (dd + 1) * ICLR_LORA,
                     2 * c + dd * c:2 * c + (dd + 1) * c].set(rw_a_up[dd])
    gup = jnp.zeros((LANES, c), F32).at[:GATE_LORA].set(rw_g_up)
    row = lambda a: a.reshape(1, -1)

    nb8 = seq // HALO
    tb = tt // HALO
    const = lambda a: pl.BlockSpec(a.shape, lambda b, i: (0,) * a.ndim)
    tile = lambda w: pl.BlockSpec((None, tt, w), lambda b, i: (b, i, 0))
    ins = [
        (x, pl.BlockSpec((None, HALO, d), lambda b, i: (b, jnp.maximum(i * tb - 1, 0), 0))),
        (x, pl.BlockSpec((None, tt, d), lambda b, i: (b, i, 0))),
        (x, pl.BlockSpec((None, HALO, d), lambda b, i: (b, jnp.minimum((i + 1) * tb, nb8 - 1), 0))),
        (mod, pl.BlockSpec((None,) + mod.shape[1:], lambda b, i: (b, 0, 0))),
    ]
    consts = [row(norm1_g), w_hy, w_rkv, w_lora, hy_conv_w, row(hy_conv_b), mu_rkv, mu_lora,
              row(rw_w0), row(rw_a0), wwa, gup, row(rw_k_k), row(rw_k_a), row(rw_r_k), _head_ones()]
    ins += [(a, const(a)) for a in consts]
    widths = [hy, 4 * c, 4 * c, c, c]
    return pl.pallas_call(
        functools.partial(_proj_kernel, tt=tt, nt=nt),
        grid=(bsz, nt),
        in_specs=[s for _, s in ins],
        out_specs=[tile(w) for w in widths],
        out_shape=[jax.ShapeDtypeStruct((bsz, seq, w), F32) for w in widths],
        compiler_params=_params("arbitrary", "arbitrary"),
        name="input_projection",
    )(*[a for a, _ in ins])


CHUNK = HEAD
GROUP = MXU_DIM // HEAD


def _nt(a, b):
    return lax.dot_general(a, b, (((1,), (1,)), ((), ())), preferred_element_type=F32)


def _tn(a, b):
    return lax.dot_general(a, b, (((0,), (0,)), ((), ())), preferred_element_type=F32)


def _wkv_direction(r, k, v, kk, lw, a, ka, s_ref, reverse):
    c = CHUNK
    ti = lax.broadcasted_iota(jnp.int32, (c, c), 0)
    si = lax.broadcasted_iota(jnp.int32, (c, c), 1)
    tri = (si >= ti) if reverse else (si <= ti)
    cum = _dot_exact_rhs_lhs(jnp.where(tri, 1.0, 0.0).astype(BF16), lw)
    tot = jnp.sum(lw, axis=0, keepdims=True)
    w_incl = jnp.exp(cum)
    w_prev = jnp.exp(cum - lw)
    w_inv = jnp.exp(-cum)
    w_end = jnp.exp(tot - cum)
    w_tot = jnp.exp(tot)
    kd = k * (1.0 + (a - 1.0) * ka)
    b = kk * a
    a_w = -kk * w_prev
    r_w = r * w_incl
    b_w = b * w_inv
    k_w = kd * w_inv
    b_e = b * w_end
    k_e = kd * w_end

    m = MXU_DIM
    ri = lax.broadcasted_iota(jnp.int32, (m, m), 0)
    ci = lax.broadcasted_iota(jnp.int32, (m, m), 1)
    head_mask = (ri // HEAD) == (ci // HEAD)
    tl = lax.broadcasted_iota(jnp.int32, (c, m), 0)
    sl = lax.broadcasted_iota(jnp.int32, (c, m), 1) % c
    strict = (sl > tl) if reverse else (sl < tl)
    incl = (sl >= tl) if reverse else (sl <= tl)
    eye = jnp.where(sl == tl, 1.0, 0.0)
    both = lambda top, bot: jnp.concatenate([top, bot], axis=0)

    def stack(xg):
        xb = xg.astype(BF16)
        return jnp.where(head_mask, jnp.concatenate([xb] * GROUP, axis=0), jnp.zeros((), BF16))

    streams = []
    for g in range(D_RWKV // m):
        sl_g = slice(g * m, (g + 1) * m)
        streams.append(dict(
            ar=both(a_w[:, sl_g], r_w[:, sl_g]).astype(BF16),
            b_st=stack(b_w[:, sl_g]), k_st=stack(k_w[:, sl_g]), v_st=stack(v[:, sl_g]),
            v=v[:, sl_g], bk=both(b_e[:, sl_g], k_e[:, sl_g]).astype(BF16),
            w_tot=w_tot[:, sl_g], s_ref=s_ref.at[g],
            strict=strict, incl=incl, eye=eye, head_mask=head_mask, stack=stack))
    return streams


def _wkv_streams_step(streams):
    c = CHUNK
    both = lambda top, bot: jnp.concatenate([top, bot], axis=0)
    for st in streams:
        st["s"] = st["s_ref"][...]
        st["xb"] = _nt(st["ar"], st["b_st"])
        st["xk"] = _nt(st["ar"], st["k_st"])
        st["xs"] = _nt(st["ar"], st["s"].astype(BF16))
    for st in streams:
        m_ak = jnp.where(st["strict"], st["xk"][:c], 0.0)
        st["rhs"] = st["xs"][:c] + _dot(m_ak.astype(BF16), st["v_st"])
        st["pw"] = jnp.where(st["strict"], st["xb"][:c], 0.0)
        st["t"] = st["eye"] + st["pw"]
        st["p_st"] = st["stack"](st["pw"])
    for _ in range(int(math.log2(c)) - 1):
        for st in streams:
            st["pw"] = _dot(st["pw"].astype(BF16), st["p_st"])
            st["p_st"] = st["stack"](st["pw"])
        for st in streams:
            st["t"] = st["t"] + _dot(st["t"].astype(BF16), st["p_st"])
    for st in streams:
        st["u"] = _dot(st["t"].astype(BF16), st["stack"](st["rhs"]))
    outs = []
    for st in streams:
        m_rb = jnp.where(st["incl"], st["xb"][c:], 0.0)
        m_rk = jnp.where(st["incl"], st["xk"][c:], 0.0)
        outs.append(st["xs"][c:] + _dot(m_rb.astype(BF16), st["stack"](st["u"]))
                    + _dot(m_rk.astype(BF16), st["v_st"]))
        uv = both(st["u"], st["v"]).astype(BF16)
        st["s_ref"][...] = st["s"] * st["w_tot"] + jnp.where(st["head_mask"], _tn(uv, st["bk"]), 0.0)
    return outs


def _dot_exact_rhs_lhs(tri_bf16, x):
    xh, xl = _split2(x)
    return _dot(tri_bf16, xh) + _dot(tri_bf16, xl)


def _wkv_kernel(rkvk_f, lwa_f, rkvk_b, lwa_b, ka_ref, of_ref, ob_ref, s_ref, *, nch):
    @pl.when(pl.program_id(1) == 0)
    def _():
        s_ref[...] = jnp.zeros_like(s_ref)

    ka = ka_ref[...]
    c = D_RWKV

    def operands(rkvk_ref, lwa_ref, rows):
        x = rkvk_ref[rows, :]
        la = lwa_ref[rows, :]
        return x[:, :c], x[:, c:2 * c], x[:, 2 * c:3 * c], x[:, 3 * c:], la[:, :c], la[:, c:]

    for ci in range(nch):
        rows_f = slice(ci * CHUNK, (ci + 1) * CHUNK)
        rows_b = slice((nch - 1 - ci) * CHUNK, (nch - ci) * CHUNK)
        fwd = _wkv_direction(*operands(rkvk_f, lwa_f, rows_f), ka, s_ref.at[0], False)
        bwd = _wkv_direction(*operands(rkvk_b, lwa_b, rows_b), ka, s_ref.at[1], True)
        outs = _wkv_streams_step(fwd + bwd)
        of_ref[rows_f, :] = jnp.concatenate(outs[:len(fwd)], axis=1)
        ob_ref[rows_b, :] = jnp.concatenate(outs[len(fwd):], axis=1)


WKV_CHUNKS_PER_STEP = 2


def _wkv(rkvk, lwa, rw_k_a):
    bsz, seq, _ = rkvk.shape
    c = D_RWKV
    nch = WKV_CHUNKS_PER_STEP if seq % (WKV_CHUNKS_PER_STEP * CHUNK) == 0 else 1
    rows = nch * CHUNK
    nb = seq // rows
    fwd = lambda w, lane_blk: pl.BlockSpec((None, rows, w), lambda b, j: (b, j, lane_blk))
    bwd = lambda w, lane_blk: pl.BlockSpec((None, rows, w), lambda b, j: (b, nb - 1 - j, lane_blk))
    return pl.pallas_call(
        functools.partial(_wkv_kernel, nch=nch),
        grid=(bsz, nb),
        in_specs=[fwd(4 * c, 0), fwd(2 * c, 0), bwd(4 * c, 0), bwd(2 * c, 1),
                  pl.BlockSpec((1, c), lambda b, j: (0, 0))],
        out_specs=[fwd(c, 0), bwd(c, 0)],
        out_shape=[jax.ShapeDtypeStruct((bsz, seq, c), F32)] * 2,
        scratch_shapes=[pltpu.VMEM((2, c // MXU_DIM, MXU_DIM, MXU_DIM), F32)],
        compiler_params=_params("arbitrary", "arbitrary"),
        name="wkv7_chunked",
    )(rkvk, lwa, rkvk, lwa, rw_k_a.reshape(1, c))


NEG_INF = float("-inf")


def _first_max(vals, idx, size):
    m = jnp.max(vals, axis=0, keepdims=True)
    i = jnp.min(jnp.where(vals == m, idx, size), axis=0, keepdims=True)
    return m, i


def _route(scores, biased):
    e, tt = scores.shape
    per = e // N_GROUPS
    rowl = lax.broadcasted_iota(jnp.int32, (per, tt), 0)
    gs = []
    for g in range(N_GROUPS):
        blk = biased[g * per:(g + 1) * per]
        m1, i1 = _first_max(blk, rowl, per)
        m2 = jnp.max(jnp.where(rowl == i1, NEG_INF, blk), axis=0, keepdims=True)
        gs.append(m1 + m2)
    cur = jnp.concatenate(gs, axis=0)
    growl = lax.broadcasted_iota(jnp.int32, (N_GROUPS, tt), 0)
    gsel = jnp.zeros((N_GROUPS, tt), F32)
    for _ in range(TOPK_GROUPS):
        _, ig = _first_max(cur, growl, N_GROUPS)
        hit = growl == ig
        gsel = jnp.where(hit, 1.0, gsel)
        cur = jnp.where(hit, NEG_INF, cur)
    emask = jnp.concatenate([jnp.broadcast_to(gsel[g:g + 1], (per, tt)) for g in range(N_GROUPS)], axis=0)
    masked = jnp.where(emask > 0.5, biased, NEG_INF)
    row = lax.broadcasted_iota(jnp.int32, (e, tt), 0)
    ids, ws = [], []
    for _ in range(TOP_K):
        _, ie = _first_max(masked, row, e)
        hit = row == ie
        ids.append(ie)
        ws.append(jnp.sum(jnp.where(hit, scores, 0.0), axis=0, keepdims=True))
        masked = jnp.where(hit, NEG_INF, masked)
    w = jnp.concatenate(ws, axis=0)
    w = w / jnp.sum(w, axis=0, keepdims=True) * ROUTE_SCALE
    return jnp.concatenate(ids, axis=0), w


def _mixout_kernel(x_ref, mod_ref, yhy_ref, of_ref, ob_ref, g_ref, bonus_ref, lnw_ref, lnb_ref,
                   ones_ref, wout_ref, g2n_ref, rwt_ref, bias_ref,
                   x1_ref, h2a_ref, h2b_ref, eid_ref, wsel_ref):
    ones = ones_ref[...]
    s = of_ref[...] + ob_ref[...]
    mean = _dot_exact_rhs(s, ones) * (1.0 / HEAD)
    dlt = s - mean
    var = _dot_exact_rhs(dlt * dlt, ones) * (1.0 / HEAD)
    sn = dlt * lax.rsqrt(var + GN_EPS) * lnw_ref[...] + lnb_ref[...]
    yrw = (sn + bonus_ref[...]) * g_ref[...]
    ch = yhy_ref.shape[-1]
    mix = _dot(yhy_ref[...].astype(BF16), wout_ref[:ch, :]) + _dot(yrw.astype(BF16), wout_ref[ch:, :])
    x1 = x_ref[...] + mod_ref[2:3, :] * mix
    x1_ref[...] = x1
    ms = jnp.mean(x1 * x1, axis=-1, keepdims=True)
    h2 = x1 * lax.rsqrt(ms + NORM_EPS) * g2n_ref[...]
    h2 = h2 * (1.0 + mod_ref[4:5, :]) + mod_ref[3:4, :]
    h2a_ref[...], h2b_ref[...] = _pack_rows(h2)
    rh, rl = _split2(rwt_ref[...])
    hh, hl = _split2(h2)
    logits = _nt(rh, hh) + (_nt(rh, hl) + _nt(rl, hh))
    scores = jax.nn.sigmoid(logits)
    ids, w = _route(scores, scores + bias_ref[...])
    eid_ref[...] = ids
    wsel_ref[...] = w


def _mix_out(x, mod, yhy, o_f, o_b, g, bonus, ln_w, ln_b, w_out, norm2_g, router_w, router_bias, tt=256):
    bsz, seq, d = x.shape
    tt = min(tt, seq)
    nt = seq // tt
    n = bsz * seq
    c = D_RWKV
    e = router_w.shape[1]
    row = lambda a: a.reshape(1, -1)
    consts = [row(ln_w), row(ln_b), _head_ones(), w_out.astype(BF16), row(norm2_g), router_w.T,
              jnp.broadcast_to(router_bias.reshape(e, 1), (e, tt))]
    const = lambda a: pl.BlockSpec(a.shape, lambda b, i: (0,) * a.ndim)
    tile = lambda w: pl.BlockSpec((None, tt, w), lambda b, i: (b, i, 0))
    flat = lambda rows, dt: jax.ShapeDtypeStruct((rows, n), dt)
    return pl.pallas_call(
        _mixout_kernel,
        grid=(bsz, nt),
        in_specs=[tile(d), pl.BlockSpec((None,) + mod.shape[1:], lambda b, i: (b, 0, 0))]
        + [tile(c)] * 5 + [const(a) for a in consts],
        out_specs=[tile(d), pl.BlockSpec((tt, d // 4), lambda b, i: (b * nt + i, 0)),
                   pl.BlockSpec((tt, d // 4), lambda b, i: (b * nt + i, 0)),
                   pl.BlockSpec((TOP_K, tt), lambda b, i: (0, b * nt + i)),
                   pl.BlockSpec((TOP_K, tt), lambda b, i: (0, b * nt + i))],
        out_shape=[jax.ShapeDtypeStruct((bsz, seq, d), F32), jax.ShapeDtypeStruct((n, d // 4), U32),
                   jax.ShapeDtypeStruct((n, d // 4), U32),
                   flat(TOP_K, jnp.int32), flat(TOP_K, F32)],
        compiler_params=_params("arbitrary", "arbitrary"),
        name="mix_out_router",
    )(x, mod, yhy, o_f, o_b, g, bonus, *consts)


BLK = 512
BLK_SHIFT = 9


def _multi_hot(eid, e):
    row = lax.broadcasted_iota(jnp.int32, (e, eid.shape[1]), 0)
    mh = jnp.zeros((e, eid.shape[1]), F32)
    for kk in range(TOP_K):
        mh = mh + jnp.where(row == eid[kk:kk + 1, :], 1.0, 0.0)
    return row, mh


def _lookup(row, eid, table):
    return jnp.concatenate(
        [jnp.sum(jnp.where(row == eid[kk:kk + 1, :], table, 0.0), axis=0, keepdims=True)
         for kk in range(TOP_K)], axis=0)


def _rank_kernel(eid_ref, rank_ref, cnt_ref, *, e):
    @pl.when(pl.program_id(0) == 0)
    def _():
        cnt_ref[...] = jnp.zeros_like(cnt_ref)

    eid = eid_ref[...]
    tt = eid.shape[1]
    row, mh = _multi_hot(eid, e)
    mhb = mh.astype(BF16)
    si = lax.broadcasted_iota(jnp.int32, (tt, tt), 0)
    ti = lax.broadcasted_iota(jnp.int32, (tt, tt), 1)
    earlier = _dot(mhb, jnp.where(si < ti, 1.0, 0.0).astype(BF16))
    cnt = cnt_ref[...]
    full = earlier + jnp.concatenate([cnt] * (tt // LANES), axis=1)
    rank_ref[...] = _lookup(row, eid, full).astype(jnp.int32)
    cnt_ref[...] = cnt + _dot(mhb, jnp.ones((tt, LANES), BF16))


def _expert_ranks(eid, e, tt=512):
    n = eid.shape[1]
    tt = min(tt, n)
    return pl.pallas_call(
        functools.partial(_rank_kernel, e=e),
        grid=(n // tt,),
        in_specs=[pl.BlockSpec((TOP_K, tt), lambda i: (0, i))],
        out_specs=[pl.BlockSpec((TOP_K, tt), lambda i: (0, i)),
                   pl.BlockSpec((e, LANES), lambda i: (0, 0))],
        out_shape=[jax.ShapeDtypeStruct((TOP_K, n), jnp.int32), jax.ShapeDtypeStruct((e, LANES), F32)],
        compiler_params=_params("arbitrary"),
        name="expert_ranks",
    )(eid)


def _block_offsets(cnt):
    e = cnt.shape[0]
    nblk = ((cnt.astype(jnp.int32) + (BLK - 1)) >> BLK_SHIFT).astype(F32)
    ri = lax.broadcasted_iota(jnp.int32, (e, e), 0)
    ci = lax.broadcasted_iota(jnp.int32, (e, e), 1)
    tril = jnp.where(ci <= ri, 1.0, 0.0).astype(BF16)
    nh, nl = _split2(nblk)
    return nblk, _dot(tril, nh) + _dot(tril, nl)


def _dest_kernel(cnt_ref, eid_ref, rank_ref, dest_ref):
    nblk, end = _block_offsets(cnt_ref[...])
    off = (end - nblk) * float(BLK)
    eid = eid_ref[...]
    tt = eid.shape[1]
    row = lax.broadcasted_iota(jnp.int32, (off.shape[0], tt), 0)
    table = jnp.concatenate([off] * (tt // LANES), axis=1)
    dest_ref[...] = _lookup(row, eid, table).astype(jnp.int32) + rank_ref[...]


def _destinations(cnt, eid, rank, tt=512):
    n = eid.shape[1]
    tt = min(tt, n)
    blk = pl.BlockSpec((TOP_K, tt), lambda i: (0, i))
    return pl.pallas_call(
        _dest_kernel,
        grid=(n // tt,),
        in_specs=[pl.BlockSpec(cnt.shape, lambda i: (0, 0)), blk, blk],
        out_specs=blk,
        out_shape=jax.ShapeDtypeStruct((TOP_K, n), jnp.int32),
        compiler_params=_params("arbitrary"),
        name="expert_destinations",
    )(cnt, eid, rank)


def _meta_kernel(cnt_ref, meta_ref, *, nbp):
    cnt = cnt_ref[...]
    e = cnt.shape[0]
    nblk, end = _block_offsets(cnt)
    rep = lambda a, w: jnp.concatenate([a] * (w // LANES), axis=1)
    b = lax.broadcasted_iota(jnp.int32, (e, nbp), 1).astype(F32)
    blk_e = jnp.minimum(jnp.sum(jnp.where(rep(end, nbp) <= b, 1.0, 0.0), axis=0, keepdims=True), float(e - 1))
    row = lax.broadcasted_iota(jnp.int32, (e, nbp), 0).astype(F32)
    mine = row == blk_e
    left = rep(cnt + (end - nblk) * float(BLK), nbp) - b * float(BLK)
    nvalid = jnp.clip(jnp.sum(jnp.where(mine, left, 0.0), axis=0, keepdims=True), 0.0, float(BLK))
    nused = jnp.max(rep(end, nbp), axis=0, keepdims=True)
    meta_ref[...] = jnp.concatenate([blk_e, nvalid, nused, jnp.zeros((5, nbp), F32)], axis=0).astype(jnp.int32)


def _block_meta(cnt, nb):
    nbp = -(-nb // LANES) * LANES
    return pl.pallas_call(
        functools.partial(_meta_kernel, nbp=nbp),
        out_shape=jax.ShapeDtypeStruct((8, nbp), jnp.int32),
        compiler_params=pltpu.CompilerParams(vmem_limit_bytes=VMEM_LIMIT),
        name="expert_block_meta",
    )(cnt)


SC_WINDOW = 128


def _sc_mesh():
    return plsc.VectorSubcoreMesh(core_axis_name="core", subcore_axis_name="subcore")


def _sc_scatter_rows(rows, idx, nrows):
    n, width = rows.shape

    @pl.kernel(out_type=jax.ShapeDtypeStruct((nrows, width), rows.dtype), mesh=_sc_mesh())
    def scatter(rows_hbm, idx_hbm, out_hbm):
        def body(rows_vmem, idx_vmem):
            pltpu.sync_copy(rows_vmem, out_hbm.at[idx_vmem.at[0]])

        pltpu.emit_pipeline(
            body,
            grid=(n // SC_WINDOW, idx.shape[0]),
            in_specs=[pl.BlockSpec((SC_WINDOW, width), index_map=lambda i, k: (i, 0)),
                      pl.BlockSpec((1, SC_WINDOW), index_map=lambda i, k: (k, i))],
            out_specs=[],
            core_axis_name=("core", "subcore"),
            dimension_semantics=(pltpu.PARALLEL, pltpu.ARBITRARY),
        )(rows_hbm, idx_hbm)

    return scatter(rows, idx)


def _sc_gather_rows(src, idx):
    num = idx.shape[1]
    width = src.shape[1]

    @pl.kernel(out_type=jax.ShapeDtypeStruct((num, width), src.dtype), mesh=_sc_mesh())
    def gather(src_hbm, idx_hbm, out_hbm):
        def body(idx_vmem, out_vmem):
            pltpu.sync_copy(src_hbm.at[idx_vmem.at[0]], out_vmem)

        pltpu.emit_pipeline(
            body,
            grid=(num // SC_WINDOW,),
            in_specs=[pl.BlockSpec((1, SC_WINDOW), index_map=lambda i: (0, i))],
            out_specs=[pl.BlockSpec((SC_WINDOW, width), index_map=lambda i: (i, 0))],
            core_axis_name=("core", "subcore"),
            dimension_semantics=(pltpu.PARALLEL,),
        )(idx_hbm, out_hbm)

    return gather(src, idx)


def _experts_kernel(be_ref, nv_ref, nu_ref, xa_ref, xb_ref, wg_ref, wu_ref, wd_ref, oa_ref, ob_ref,
                    wgb, wub, wdb):
    b = pl.program_id(0)
    used = b < nu_ref[0]

    @pl.when(used)
    def _():
        prev = be_ref[jnp.maximum(b - 1, 0)]

        @pl.when(jnp.logical_or(b == 0, be_ref[b] != prev))
        def _():
            wgb[...] = wg_ref[...].astype(BF16)
            wub[...] = wu_ref[...].astype(BF16)
            wdb[...] = wd_ref[...].astype(BF16)

        valid = lax.broadcasted_iota(jnp.int32, (xa_ref.shape[0], 1), 0) < nv_ref[b]
        zero = jnp.uint32(0)
        x = _unpack_rows(jnp.where(valid, xa_ref[...], zero), jnp.where(valid, xb_ref[...], zero))
        act = _silu(_dot(x, wgb[...])) * _dot(x, wub[...])
        oa_ref[...], ob_ref[...] = _pack_rows(_dot(act.astype(BF16), wdb[...]))

    @pl.when(jnp.logical_not(used))
    def _():
        oa_ref[...] = jnp.zeros_like(oa_ref)
        ob_ref[...] = jnp.zeros_like(ob_ref)


def _experts(blk_e, nvalid, nused, xs_a, xs_b, wg, wu, wd):
    p, dq = xs_a.shape
    nb = p // BLK
    d, de = wg.shape[1], wg.shape[2]
    last = lambda b, be, nv, nu: jnp.minimum(b, nu[0] - 1)
    rows_in = pl.BlockSpec((BLK, dq), lambda b, be, nv, nu: (last(b, be, nv, nu), 0))
    return pl.pallas_call(
        _experts_kernel,
        grid_spec=pltpu.PrefetchScalarGridSpec(
            num_scalar_prefetch=3,
            grid=(nb,),
            in_specs=[rows_in, rows_in,
                      pl.BlockSpec((None, d, de), lambda b, be, nv, nu: (be[last(b, be, nv, nu)], 0, 0)),
                      pl.BlockSpec((None, d, de), lambda b, be, nv, nu: (be[last(b, be, nv, nu)], 0, 0)),
                      pl.BlockSpec((None, de, d), lambda b, be, nv, nu: (be[last(b, be, nv, nu)], 0, 0))],
            out_specs=[pl.BlockSpec((BLK, dq), lambda b, be, nv, nu: (b, 0))] * 2,
            scratch_shapes=[pltpu.VMEM((d, de), BF16), pltpu.VMEM((d, de), BF16), pltpu.VMEM((de, d), BF16)],
        ),
        out_shape=[jax.ShapeDtypeStruct((p, dq), U32)] * 2,
        compiler_params=_params("arbitrary"),
        name="moe_experts",
    )(blk_e, nvalid, nused, xs_a, xs_b, wg, wu, wd)


def _combine_kernel(w_ref, x1_ref, ha_ref, hb_ref, mod_ref, ga_ref, gb_ref, sg_ref, su_ref, sd_ref, gf_ref,
                    sel_ref, o_ref):
    hb = _unpack_rows(ha_ref[...], hb_ref[...])
    act = _silu(_dot(hb, sg_ref[...])) * _dot(hb, su_ref[...])
    ffn = _dot(act.astype(BF16), sd_ref[...])
    wh, wl = _split2(w_ref[...])
    acc = None
    for kk in range(TOP_K):
        sel = sel_ref[kk]
        wk = _tn(wh, sel) + _tn(wl, sel)
        a_lo, a_hi = _unpack_halves(ga_ref[kk])
        b_lo, b_hi = _unpack_halves(gb_ref[kk])
        parts = [a_lo * wk, b_lo * wk, a_hi * wk, b_hi * wk]
        acc = parts if acc is None else [p + q for p, q in zip(acc, parts)]
    ffn = ffn + jnp.concatenate(acc, axis=1)
    xo = x1_ref[...] + mod_ref[5:6, :] * ffn
    ms = jnp.mean(xo * xo, axis=-1, keepdims=True)
    o_ref[...] = xo * lax.rsqrt(ms + NORM_EPS) * gf_ref[...]


def _combine(wsel, x1, h2a, h2b, mod, ga, gb, sh_wg, sh_wu, sh_wd, normf_g, tt=256):
    bsz, seq, d = x1.shape
    n = bsz * seq
    tt = min(tt, seq)
    per = seq // tt
    dp = h2a.shape[1]
    dq = ga.shape[2]
    sel = jnp.asarray(np.broadcast_to(np.eye(TOP_K)[:, :, None], (TOP_K, TOP_K, dq)), BF16)
    consts = [sh_wg.astype(BF16), sh_wu.astype(BF16), sh_wd.astype(BF16), normf_g.reshape(1, d), sel]
    const = lambda a: pl.BlockSpec(a.shape, lambda i: (0,) * a.ndim)
    rows = pl.BlockSpec((tt, d), lambda i: (i, 0))
    packed_rows = pl.BlockSpec((tt, dp), lambda i: (i, 0))
    gathered = pl.BlockSpec((TOP_K, tt, dq), lambda i: (0, i, 0))
    return pl.pallas_call(
        _combine_kernel,
        grid=(n // tt,),
        in_specs=[pl.BlockSpec((TOP_K, tt), lambda i: (0, i)),
                  rows, packed_rows, packed_rows,
                  pl.BlockSpec((None,) + mod.shape[1:], lambda i: (i // per, 0, 0)),
                  gathered, gathered] + [const(a) for a in consts],
        out_specs=rows,
        out_shape=jax.ShapeDtypeStruct((n, d), F32),
        compiler_params=_params("arbitrary"),
        name="moe_combine",
    )(wsel, x1.reshape(n, d), h2a, h2b, mod, ga, gb, *consts)


def _moe(x1, h2a, h2b, mod, eid, wsel, exp_wg, exp_wu, exp_wd, sh_wg, sh_wu, sh_wd, normf_g):
    n = h2a.shape[0]
    e = exp_wg.shape[0]
    nb = (n * TOP_K + e * (BLK - 1)) // BLK
    rank, cnt = _expert_ranks(eid, e)
    dest = _destinations(cnt, eid, rank)
    meta = _block_meta(cnt, nb)
    xs_a = _sc_scatter_rows(h2a, dest, nb * BLK)
    xs_b = _sc_scatter_rows(h2b, dest, nb * BLK)
    ys_a, ys_b = _experts(meta[0, :nb], meta[1, :nb], meta[2, :1], xs_a, xs_b, exp_wg, exp_wu, exp_wd)
    idx = dest.reshape(1, TOP_K * n)
    ga = _sc_gather_rows(ys_a, idx).reshape(TOP_K, n, -1)
    gb = _sc_gather_rows(ys_b, idx).reshape(TOP_K, n, -1)
    return _combine(wsel, x1, h2a, h2b, mod, ga, gb, sh_wg, sh_wu, sh_wd, normf_g)


def kernel(x, c, norm1_g, norm2_g, normf_g, w_ada, b_ada, w_in, w_out, hy_conv_w, hy_conv_b, hy_pos_w1, hy_pos_b1, hy_pos_w2, hy_pos_b2, hy_pos_w3, hy_sin_freq, hy_skip, rw_mu, rw_w0, rw_w_up, rw_a0, rw_a_up, rw_g_up, rw_k_k, rw_k_a, rw_r_k, rw_ln_w, rw_ln_b, router_w, router_bias, exp_w_gate, exp_w_up, exp_w_down, sh_w_gate, sh_w_up, sh_w_down):
    bsz, seq, d = x.shape
    depth = w_ada.shape[0]
    assert depth == 1, "the final norm is fused into the last kernel of a single layer"
    for l in range(depth):
        mod = _modulation(c, w_ada[l], b_ada[l]).reshape(bsz, -1, d)
        uhy, rkvk, lwa, g, bonus = _projection(
            x, mod, norm1_g[l], w_in[l], hy_conv_w[l], hy_conv_b[l], rw_mu[l], rw_w0[l], rw_w_up[l],
            rw_a0[l], rw_a_up[l], rw_g_up[l], rw_k_k[l], rw_k_a[l], rw_r_k[l])
        k2, ss = _hyena_filters(seq, hy_pos_w1[l], hy_pos_b1[l], hy_pos_w2[l], hy_pos_b2[l],
                                hy_pos_w3[l], hy_sin_freq[l])
        khat = _filter_spectrum(k2, ss, seq)
        z, z_col = uhy, 0
        for order in range(HYENA_ORDER):
            z = _long_conv_gate(z, z_col, uhy, (order + 1) * D_HYENA, khat, hy_skip[l], order)
            z_col = 0
        o_f, o_b = _wkv(rkvk, lwa, rw_k_a[l])
        x1, h2a, h2b, eid, wsel = _mix_out(x, mod, z, o_f, o_b, g, bonus, rw_ln_w[l], rw_ln_b[l], w_out[l],
                                           norm2_g[l], router_w[l], router_bias[l])
        x = _moe(x1, h2a, h2b, mod, eid, wsel, exp_w_gate[l], exp_w_up[l], exp_w_down[l],
                 sh_w_gate[l], sh_w_up[l], sh_w_down[l], normf_g)
        x = x.reshape(bsz, seq, d)
    return x
```

```python
import functools
import math

import jax
import jax.numpy as jnp
import numpy as np
from jax import lax
from jax.experimental import pallas as pl
from jax.experimental.pallas import tpu as pltpu
from jax.experimental.pallas import tpu_sc as plsc

F32 = jnp.float32
BF16 = jnp.bfloat16

LANES = 128
MXU_DIM = 256
VMEM_LIMIT = 56 * 1024 * 1024

D_HYENA = 512
D_RWKV = 512
HEAD = 64
N_HEADS = D_RWKV // HEAD
HYENA_ORDER = 2
FILTER_BANDS = 16
DECAY_TARGET = 1e-2
FAST_DECAY_PCT = 0.3
SLOW_DECAY_PCT = 1.5
FILTER_NORM_EPS = 1e-6
DECAY_LORA = 32
ICLR_LORA = 32
GATE_LORA = 96
GN_EPS = 64e-5
NORM_EPS = 1e-6
N_EXPERTS = 256
TOP_K = 8
N_GROUPS = 8
TOPK_GROUPS = 4
ROUTE_SCALE = 2.5
D_EXPERT = 256


def _params(*sem):
    return pltpu.CompilerParams(dimension_semantics=sem, vmem_limit_bytes=VMEM_LIMIT)


def _split2(a):
    hi = a.astype(BF16)
    lo = (a - hi.astype(F32)).astype(BF16)
    return hi, lo


def _dot(a, b):
    return jnp.dot(a, b, preferred_element_type=F32)


def _dot3(a, b):
    ah, al = _split2(a)
    bh, bl = _split2(b)
    return _dot(ah, bh) + (_dot(ah, bl) + _dot(al, bh))


def _dot_exact_rhs(a, b_bf16):
    ah, al = _split2(a)
    return _dot(ah, b_bf16) + _dot(al, b_bf16)


def _silu(x):
    return x * jax.nn.sigmoid(x)


U32 = jnp.uint32


def _pack_halves(x):
    w = x.shape[1] // 2
    rounded = x.astype(BF16).astype(F32)
    bits = lax.bitcast_convert_type(rounded, U32)
    return (bits[:, w:] & jnp.uint32(0xFFFF0000)) | (bits[:, :w] >> 16)


def _unpack_halves(p):
    lo = lax.bitcast_convert_type(p << 16, F32)
    hi = lax.bitcast_convert_type(p & jnp.uint32(0xFFFF0000), F32)
    return lo, hi


def _pack_rows(x):
    packed = _pack_halves(x)
    half = packed.shape[1] // 2
    return packed[:, :half], packed[:, half:]


def _unpack_rows(a, b):
    a_lo, a_hi = _unpack_halves(a)
    b_lo, b_hi = _unpack_halves(b)
    return jnp.concatenate([a_lo.astype(BF16), b_lo.astype(BF16), a_hi.astype(BF16), b_hi.astype(BF16)], axis=1)


def _mod_kernel(c_ref, w_ref, b_ref, o_ref):
    o_ref[...] = _dot3(_silu(c_ref[...]), w_ref[...]) + b_ref[...]


def _modulation(c, w_ada, b_ada):
    bsz, d = c.shape
    n = w_ada.shape[1]
    blk = 1024
    return pl.pallas_call(
        _mod_kernel,
        grid=(n // blk,),
        in_specs=[
            pl.BlockSpec((bsz, d), lambda j: (0, 0)),
            pl.BlockSpec((d, blk), lambda j: (0, j)),
            pl.BlockSpec((1, blk), lambda j: (0, j)),
        ],
        out_specs=pl.BlockSpec((bsz, blk), lambda j: (0, j)),
        out_shape=jax.ShapeDtypeStruct((bsz, n), F32),
        compiler_params=_params("arbitrary"),
        name="adaln_mod",
    )(c, w_ada, b_ada.reshape(1, n))


def _filter_kernel(band_ref, w1_ref, b1_ref, w2_ref, b2_ref, w3_ref, freq_ref, delta_ref,
                   k_ref, ss_ref, *, seq, rows):
    half = pl.program_id(0)
    i = pl.program_id(1)
    r = lax.broadcasted_iota(jnp.int32, (rows, LANES), 0) + i * rows
    pos = jnp.where(half == 0, r, seq - r).astype(F32)
    tt = pos / float(max(seq - 1, 1))
    lane = lax.broadcasted_iota(jnp.int32, (rows, LANES), 1)
    ang = pos * band_ref[...]
    feats = jnp.where(lane == 0, tt,
                      jnp.where(lane <= FILTER_BANDS, jnp.cos(ang),
                                jnp.where(lane <= 2 * FILTER_BANDS, -jnp.sin(ang), 0.0)))
    freq = freq_ref[...]
    hdn = jnp.sin(freq * (_dot3(feats, w1_ref[...]) + b1_ref[...]))
    for j in range(w2_ref.shape[0]):
        hdn = jnp.sin(freq * (_dot3(hdn, w2_ref[j]) + b2_ref[j]))
    filt = _dot3(hdn, w3_ref[...])
    filt = filt * jnp.exp(-tt[:, :1] * delta_ref[...])
    valid = jnp.logical_or(half == 0, r[:, :1] > 0)
    filt = jnp.where(valid, filt, 0.0)
    k_ref[...] = filt

    @pl.when(jnp.logical_and(half == 0, i == 0))
    def _():
        ss_ref[...] = jnp.zeros_like(ss_ref)

    ss_ref[...] += jnp.broadcast_to(jnp.sum(filt * filt, axis=0, keepdims=True), ss_ref.shape)


def _hyena_filters(seq, pw1, pb1, pw2, pb2, pw3, freq):
    width = pw1.shape[1]
    ncol = HYENA_ORDER * D_HYENA
    rows = min(seq, 512)
    bands = np.zeros((1, LANES), np.float64)
    lin = np.linspace(1e-4, FILTER_BANDS - 1, FILTER_BANDS)
    bands[0, 1:1 + FILTER_BANDS] = lin
    bands[0, 1 + FILTER_BANDS:1 + 2 * FILTER_BANDS] = lin
    bands = jnp.asarray(bands * (2.0 * math.pi / seq), F32)
    deltas = np.abs(np.linspace(math.log(DECAY_TARGET) / SLOW_DECAY_PCT,
                                math.log(DECAY_TARGET) / FAST_DECAY_PCT, D_HYENA))
    deltas = jnp.asarray(np.tile(deltas, HYENA_ORDER)[None], F32)
    w1 = jnp.zeros((LANES, width), F32).at[:pw1.shape[0]].set(pw1)
    w3 = pw3.reshape(width, HYENA_ORDER, 2, D_HYENA).transpose(2, 0, 1, 3).reshape(2, width, ncol)
    nt = seq // rows
    full = lambda *shape: pl.BlockSpec(shape, lambda h, i: (0,) * len(shape))
    return pl.pallas_call(
        functools.partial(_filter_kernel, seq=seq, rows=rows),
        grid=(2, nt),
        in_specs=[
            full(1, LANES), full(LANES, width), full(1, width),
            full(pw2.shape[0], width, width), full(pw2.shape[0], 1, width),
            pl.BlockSpec((None, width, ncol), lambda h, i: (h, 0, 0)),
            full(1, width), full(1, ncol),
        ],
        out_specs=[
            pl.BlockSpec((rows, ncol), lambda h, i: (h * nt + i, 0)),
            pl.BlockSpec((8, ncol), lambda h, i: (0, 0)),
        ],
        out_shape=[jax.ShapeDtypeStruct((2 * seq, ncol), F32),
                   jax.ShapeDtypeStruct((8, ncol), F32)],
        compiler_params=_params("arbitrary", "arbitrary"),
        name="hyena_filters",
    )(bands, w1, pb1.reshape(1, width), pw2, pb2.reshape(pw2.shape[0], 1, width), w3,
      freq.reshape(1, width), deltas)


N1 = LANES
UNROLL = 8


def _dft_tables(seq):
    tables = _dft_tables_np(seq)
    return tuple(jnp.asarray(t, BF16) for t in tables[:5]) + tables[5:]


def _dft_tables_np(seq):
    m = 2 * seq
    n2 = m // N1
    n2h = n2 // 2
    n1 = np.arange(N1)[:, None, None]
    f2 = np.arange(n2)[None, :, None]
    k2 = np.arange(n2)[None, None, :]
    th = 2.0 * np.pi * (n1 * f2 / m + (k2 * f2 % n2) / n2)
    fwd_a = np.concatenate([np.cos(th), -np.sin(th)], axis=1)
    tht = np.transpose(th, (0, 2, 1))
    inv_a = np.concatenate([np.cos(tht), -np.sin(tht)], axis=2)[:, :n2h] / m
    a = np.arange(N1)
    ph = 2.0 * np.pi * np.outer(a, a) / N1
    c, s = np.cos(ph), np.sin(ph)
    fwd_b = np.block([[c, s], [-s, c]])
    inv_b = np.block([[c, -s], [s, c]])
    return fwd_a, fwd_a[:, :, :n2h], inv_a, fwd_b, inv_b, n2, n2h


def _stage_a_fwd(x_ref, wa_ref, y_ref, n2, scale=None):
    def body(i, carry):
        trips = [i * UNROLL + j for j in range(UNROLL)]
        xs = [x_ref[pl.ds(n1, wa_ref.shape[2], stride=N1), :] for n1 in trips]
        if scale is not None:
            xs = [x * scale for x in xs]
        prods = [_dot(wa_ref[n1], x.astype(BF16)) for n1, x in zip(trips, xs)]
        for n1, a in zip(trips, prods):
            y_ref[pl.ds(n1, n2, stride=2 * N1), :] = a[:n2]
            y_ref[pl.ds(N1 + n1, n2, stride=2 * N1), :] = a[n2:]
        return carry
    lax.fori_loop(0, N1 // UNROLL, body, 0)


def _filter_fft_kernel(k_ref, ss_ref, wa_ref, fb_ref, o_ref, y_ref, *, n2):
    scale = lax.rsqrt(ss_ref[0:1, :] + FILTER_NORM_EPS)
    _stage_a_fwd(k_ref, wa_ref, y_ref, n2, scale=scale)

    unr = min(UNROLL, n2)

    def body(i, carry):
        trips = [i * unr + j for j in range(unr)]
        ys = [y_ref[pl.ds(pl.multiple_of(f2 * 2 * N1, 2 * N1), 2 * N1), :].astype(BF16) for f2 in trips]
        for f2, y in zip(trips, ys):
            o_ref[f2] = _dot(fb_ref[...], y)
        return carry
    lax.fori_loop(0, n2 // unr, body, 0)


def _filter_spectrum(k2, ss, seq):
    fwd_a, _, _, fwd_b, _, n2, _ = _dft_tables(seq)
    ncol = k2.shape[1]
    nblk = ncol // LANES
    return pl.pallas_call(
        functools.partial(_filter_fft_kernel, n2=n2),
        grid=(nblk,),
        in_specs=[
            pl.BlockSpec((2 * seq, LANES), lambda c: (0, c)),
            pl.BlockSpec((8, LANES), lambda c: (0, c)),
            pl.BlockSpec(fwd_a.shape, lambda c: (0, 0, 0)),
            pl.BlockSpec(fwd_b.shape, lambda c: (0, 0)),
        ],
        out_specs=pl.BlockSpec((None, n2, 2 * N1, LANES), lambda c: (c, 0, 0, 0)),
        out_shape=jax.ShapeDtypeStruct((nblk, n2, 2 * N1, LANES), F32),
        scratch_shapes=[pltpu.VMEM((n2 * 2 * N1, LANES), F32)],
        compiler_params=_params("arbitrary"),
        name="hyena_filter_fft",
    )(k2, ss, fwd_a, fwd_b)


TILE = 8
N1_GROUPS = N1 // TILE


def _tile_tables(seq):
    _, fwd_a, inv_a, _, _, n2, n2h = _dft_tables_np(seq)
    eye = np.eye(TILE)
    fa = fwd_a.reshape(N1_GROUPS, TILE, 2 * n2, n2h)
    wa = np.einsum("qjrn,jk->qrjnk", fa, eye).reshape(N1_GROUPS, 2 * n2 * TILE, n2h * TILE)
    ia = inv_a.reshape(N1_GROUPS, TILE, n2h, 2 * n2)
    vc = np.einsum("qjnr,jk->qnjrk", ia, eye).reshape(N1_GROUPS, n2h * TILE, 2 * n2 * TILE)
    return jnp.asarray(wa, BF16), jnp.asarray(vc, BF16)


def _conv_kernel(u_ref, g_ref, skip_ref, fb_ref, ib_ref, kh_hbm, wa_hbm, vc_hbm, o_ref,
                 y_ref, kh_ref, wa_ref, vc_ref, sem, *, n2, n2h, kh_first):
    c_id, b_id = pl.program_id(0), pl.program_id(1)

    @pl.when(jnp.logical_and(c_id == 0, b_id == 0))
    def _():
        for src, dst in ((wa_hbm, wa_ref), (vc_hbm, vc_ref)):
            cp = pltpu.make_async_copy(src, dst, sem)
            cp.start()
            cp.wait()

    @pl.when(b_id == 0)
    def _():
        cp = pltpu.make_async_copy(kh_hbm.at[kh_first + c_id], kh_ref, sem)
        cp.start()
        cp.wait()

    def y_tile(rf, base):
        ri, f2 = divmod(rf, n2)
        return pl.ds(f2 * 2 * N1 + ri * N1 + base, TILE)

    def stage_a(q, carry):
        base = pl.multiple_of(q * TILE, TILE)
        x = jnp.concatenate([u_ref[pl.ds(N1 * m + base, TILE), :] for m in range(n2h)], axis=0)
        r = _dot(wa_ref[q], x.astype(BF16))
        for rf in range(2 * n2):
            y_ref[y_tile(rf, base), :] = r[rf * TILE:(rf + 1) * TILE]
        return carry
    lax.fori_loop(0, N1_GROUPS, stage_a, 0, unroll=2)

    unr = min(UNROLL, n2)

    def mid(i, carry):
        trips = [i * unr + j for j in range(unr)]
        offs = [pl.multiple_of(f2 * 2 * N1, 2 * N1) for f2 in trips]
        zs = [_dot(fb_ref[...], y_ref[pl.ds(off, 2 * N1), :].astype(BF16)) for off in offs]
        ps = []
        for f2, z in zip(trips, zs):
            zr, zi = z[:N1], z[N1:]
            kh = kh_ref[f2]
            kr, ki = kh[:N1], kh[N1:]
            ps.append(jnp.concatenate([zr * kr - zi * ki, zr * ki + zi * kr], axis=0).astype(BF16))
        gs = [_dot(ib_ref[...], p) for p in ps]
        for off, g in zip(offs, gs):
            y_ref[pl.ds(off, 2 * N1), :] = g
        return carry
    lax.fori_loop(0, n2 // unr, mid, 0)

    skip = skip_ref[...]

    def stage_c(q, carry):
        base = pl.multiple_of(q * TILE, TILE)
        g = jnp.concatenate([y_ref[y_tile(rf, base), :] for rf in range(2 * n2)], axis=0)
        conv = _dot(vc_ref[q], g.astype(BF16))
        for m in range(n2h):
            rows = pl.ds(N1 * m + base, TILE)
            o_ref[rows, :] = g_ref[rows, :] * (conv[m * TILE:(m + 1) * TILE] + u_ref[rows, :] * skip)
        return carry
    lax.fori_loop(0, N1_GROUPS, stage_c, 0, unroll=2)


def _long_conv_gate(u, u_col, gate, gate_col, khat, skip, order):
    bsz, seq, _ = u.shape
    ch = D_HYENA
    _, _, _, fwd_b, inv_b, n2, n2h = _dft_tables(seq)
    wa, vc = _tile_tables(seq)
    nblk = ch // LANES
    const = lambda a: pl.BlockSpec(a.shape, lambda c, b: (0,) * a.ndim)
    at = lambda col: pl.BlockSpec((None, seq, LANES), lambda c, b: (b, 0, col // LANES + c))
    hbm = pl.BlockSpec(memory_space=pl.ANY)
    return pl.pallas_call(
        functools.partial(_conv_kernel, n2=n2, n2h=n2h, kh_first=order * nblk),
        grid=(nblk, bsz),
        in_specs=[
            at(u_col), at(gate_col),
            pl.BlockSpec((1, LANES), lambda c, b: (0, c)),
            const(fwd_b), const(inv_b), hbm, hbm, hbm,
        ],
        out_specs=at(0),
        out_shape=jax.ShapeDtypeStruct((bsz, seq, ch), F32),
        scratch_shapes=[pltpu.VMEM((n2 * 2 * N1, LANES), F32), pltpu.VMEM(khat.shape[1:], F32),
                        pltpu.VMEM(wa.shape, BF16), pltpu.VMEM(vc.shape, BF16), pltpu.SemaphoreType.DMA(())],
        compiler_params=_params("arbitrary", "arbitrary"),
        name=f"hyena_conv{order}",
    )(u, gate, skip[order].reshape(1, ch), fwd_b, inv_b, khat, wa, vc)


HALO = 8


def _shift_rows(p, k):
    return pltpu.roll(p, k % p.shape[0], axis=0)


def _proj_kernel(xp_ref, x_ref, xn_ref, mod_ref, g1_ref, why_ref, wrkv_ref, wlora_ref,
                 cw_ref, cb_ref, murkv_ref, mulora_ref, w0_ref, a0_ref, wwa_ref, gup_ref,
                 kk_ref, ka_ref, rk_ref, ones_ref,
                 uhy_ref, rkvk_ref, lwa_ref, g_ref, bonus_ref,
                 *, tt, nt):
    i = pl.program_id(1)
    xe = jnp.concatenate([xp_ref[...], x_ref[...], xn_ref[...]], axis=0)
    ms = jnp.mean(xe * xe, axis=-1, keepdims=True)
    h = xe * lax.rsqrt(ms + NORM_EPS) * g1_ref[...]
    h = h * (1.0 + mod_ref[1:2, :]) + mod_ref[0:1, :]
    row = lax.broadcasted_iota(jnp.int32, (tt + 2 * HALO, 1), 0)
    inside = jnp.logical_and(jnp.logical_or(row >= HALO, i > 0),
                             jnp.logical_or(row < tt + HALO, i < nt - 1))
    hb = jnp.where(inside, h, 0.0).astype(BF16)
    mid = slice(HALO, tt + HALO)

    p = _dot(hb, why_ref[...])
    u = (_shift_rows(p, 1) * cw_ref[0:1, :] + p * cw_ref[1:2, :]
         + _shift_rows(p, -1) * cw_ref[2:3, :] + cb_ref[...])
    uhy_ref[...] = u[mid]

    p = _dot(hb, wrkv_ref[...])
    p = p + murkv_ref[...] * (0.5 * (_shift_rows(p, 1) + _shift_rows(p, -1)) - p)
    p = p[mid]
    c = D_RWKV
    r, k, v = p[:, :c], p[:, c:2 * c], p[:, 2 * c:]
    rkvk_ref[:, :3 * c] = p

    q = _dot(hb, wlora_ref[...])
    q = q + mulora_ref[...] * (0.5 * (_shift_rows(q, 1) + _shift_rows(q, -1)) - q)
    q = q[mid]
    wa = q[:, :LANES]
    lane = lax.broadcasted_iota(jnp.int32, wa.shape, 1)
    wa = jnp.where(lane < 2 * DECAY_LORA, jnp.tanh(wa), wa)
    up = _dot3(wa, wwa_ref[...])
    z = -(w0_ref[...] + up[:, :2 * c])
    softplus = jnp.maximum(z, 0.0) + jnp.log1p(jnp.exp(-jnp.abs(z)))
    lw = -jnp.exp(-softplus - 0.5)
    a = jax.nn.sigmoid(a0_ref[...] + up[:, 2 * c:])
    for dd in range(2):
        lwa_ref[:, 2 * dd * c:(2 * dd + 1) * c] = lw[:, dd * c:(dd + 1) * c]
        lwa_ref[:, (2 * dd + 1) * c:(2 * dd + 2) * c] = a[:, dd * c:(dd + 1) * c]
    g_ref[...] = _dot3(jax.nn.sigmoid(q[:, LANES:]), gup_ref[...])

    ones = ones_ref[...]
    kk = k * kk_ref[...]
    nrm = jnp.sqrt(_dot_exact_rhs(kk * kk, ones))
    rkvk_ref[:, 3 * c:] = kk / jnp.maximum(nrm, 1e-12)
    ka = ka_ref[...]
    ksum = k * (2.0 + (a[:, :c] + a[:, c:] - 2.0) * ka)
    bonus_ref[...] = _dot_exact_rhs(r * ksum * rk_ref[...], ones) * v


def _head_ones():
    hid = np.arange(D_RWKV) // HEAD
    return jnp.asarray(hid[:, None] == hid[None, :], BF16)


def _projection(x, mod, norm1_g, w_in, hy_conv_w, hy_conv_b, rw_mu, rw_w0, rw_w_up, rw_a0,
                rw_a_up, rw_g_up, rw_k_k, rw_k_a, rw_r_k, tt=256):
    bsz, seq, d = x.shape
    tt = min(tt, seq)
    nt = seq // tt
    c = D_RWKV
    hy = (HYENA_ORDER + 1) * D_HYENA
    nlora = 2 * LANES
    w_hy = w_in[:, :hy].astype(BF16)
    w_rkv = w_in[:, hy:hy + 3 * c].astype(BF16)
    w_lora = jnp.zeros((d, nlora), F32).at[:, :w_in.shape[1] - hy - 3 * c].set(w_in[:, hy + 3 * c:]).astype(BF16)
    mu_rkv = rw_mu[:3 * c].reshape(1, 3 * c)
    mu_lora = jnp.zeros((1, nlora), F32).at[0, :rw_mu.shape[0] - 3 * c].set(rw_mu[3 * c:])
    wwa = jnp.zeros((LANES, 4 * c), F32)
    for dd in range(2):
        wwa = wwa.at[dd * DECAY_LORA:(dd + 1) * DECAY_LORA, dd * c:(dd + 1) * c].set(rw_w_up[dd])
        wwa = wwa.at[2 * DECAY_LORA + dd * ICLR_LORA:2 * DECAY_LORA + (dd + 1) * ICLR_LORA,
                     2 * c + dd * c:2 * c + (dd + 1) * c].set(rw_a_up[dd])
    gup = jnp.zeros((LANES, c), F32).at[:GATE_LORA].set(rw_g_up)
    row = lambda a: a.reshape(1, -1)

    nb8 = seq // HALO
    tb = tt // HALO
    const = lambda a: pl.BlockSpec(a.shape, lambda b, i: (0,) * a.ndim)
    tile = lambda w: pl.BlockSpec((None, tt, w), lambda b, i: (b, i, 0))
    ins = [
        (x, pl.BlockSpec((None, HALO, d), lambda b, i: (b, jnp.maximum(i * tb - 1, 0), 0))),
        (x, pl.BlockSpec((None, tt, d), lambda b, i: (b, i, 0))),
        (x, pl.BlockSpec((None, HALO, d), lambda b, i: (b, jnp.minimum((i + 1) * tb, nb8 - 1), 0))),
        (mod, pl.BlockSpec((None,) + mod.shape[1:], lambda b, i: (b, 0, 0))),
    ]
    consts = [row(norm1_g), w_hy, w_rkv, w_lora, hy_conv_w, row(hy_conv_b), mu_rkv, mu_lora,
              row(rw_w0), row(rw_a0), wwa, gup, row(rw_k_k), row(rw_k_a), row(rw_r_k), _head_ones()]
    ins += [(a, const(a)) for a in consts]
    widths = [hy, 4 * c, 4 * c, c, c]
    return pl.pallas_call(
        functools.partial(_proj_kernel, tt=tt, nt=nt),
        grid=(bsz, nt),
        in_specs=[s for _, s in ins],
        out_specs=[tile(w) for w in widths],
        out_shape=[jax.ShapeDtypeStruct((bsz, seq, w), F32) for w in widths],
        compiler_params=_params("arbitrary", "arbitrary"),
        name="input_projection",
    )(*[a for a, _ in ins])


CHUNK = HEAD
GROUP = MXU_DIM // HEAD


def _nt(a, b):
    return lax.dot_general(a, b, (((1,), (1,)), ((), ())), preferred_element_type=F32)


def _tn(a, b):
    return lax.dot_general(a, b, (((0,), (0,)), ((), ())), preferred_element_type=F32)


def _wkv_direction(r, k, v, kk, lw, a, ka, s_ref, reverse):
    c = CHUNK
    ti = lax.broadcasted_iota(jnp.int32, (c, c), 0)
    si = lax.broadcasted_iota(jnp.int32, (c, c), 1)
    tri = (si >= ti) if reverse else (si <= ti)
    cum = _dot_exact_rhs_lhs(jnp.where(tri, 1.0, 0.0).astype(BF16), lw)
    tot = jnp.sum(lw, axis=0, keepdims=True)
    w_incl = jnp.exp(cum)
    w_prev = jnp.exp(cum - lw)
    w_inv = jnp.exp(-cum)
    w_end = jnp.exp(tot - cum)
    w_tot = jnp.exp(tot)
    kd = k * (1.0 + (a - 1.0) * ka)
    b = kk * a
    a_w = -kk * w_prev
    r_w = r * w_incl
    b_w = b * w_inv
    k_w = kd * w_inv
    b_e = b * w_end
    k_e = kd * w_end

    m = MXU_DIM
    ri = lax.broadcasted_iota(jnp.int32, (m, m), 0)
    ci = lax.broadcasted_iota(jnp.int32, (m, m), 1)
    head_mask = (ri // HEAD) == (ci // HEAD)
    tl = lax.broadcasted_iota(jnp.int32, (c, m), 0)
    sl = lax.broadcasted_iota(jnp.int32, (c, m), 1) % c
    strict = (sl > tl) if reverse else (sl < tl)
    incl = (sl >= tl) if reverse else (sl <= tl)
    eye = jnp.where(sl == tl, 1.0, 0.0)
    both = lambda top, bot: jnp.concatenate([top, bot], axis=0)

    def stack(xg):
        xb = xg.astype(BF16)
        return jnp.where(head_mask, jnp.concatenate([xb] * GROUP, axis=0), jnp.zeros((), BF16))

    streams = []
    for g in range(D_RWKV // m):
        sl_g = slice(g * m, (g + 1) * m)
        streams.append(dict(
            ar=both(a_w[:, sl_g], r_w[:, sl_g]).astype(BF16),
            b_st=stack(b_w[:, sl_g]), k_st=stack(k_w[:, sl_g]), v_st=stack(v[:, sl_g]),
            v=v[:, sl_g], bk=both(b_e[:, sl_g], k_e[:, sl_g]).astype(BF16),
            w_tot=w_tot[:, sl_g], s_ref=s_ref.at[g],
            strict=strict, incl=incl, eye=eye, head_mask=head_mask, stack=stack))
    return streams


def _wkv_streams_step(streams):
    c = CHUNK
    both = lambda top, bot: jnp.concatenate([top, bot], axis=0)
    for st in streams:
        st["s"] = st["s_ref"][...]
        st["xb"] = _nt(st["ar"], st["b_st"])
        st["xk"] = _nt(st["ar"], st["k_st"])
        st["xs"] = _nt(st["ar"], st["s"].astype(BF16))
    for st in streams:
        m_ak = jnp.where(st["strict"], st["xk"][:c], 0.0)
        st["rhs"] = st["xs"][:c] + _dot(m_ak.astype(BF16), st["v_st"])
        st["pw"] = jnp.where(st["strict"], st["xb"][:c], 0.0)
        st["t"] = st["eye"] + st["pw"]
        st["p_st"] = st["stack"](st["pw"])
    levels = int(math.log2(c)) - 1
    for st in streams:
        st["pw"] = _dot(st["pw"].astype(BF16), st["p_st"])
        st["p_st"] = st["stack"](st["pw"])
    for lvl in range(1, levels + 1):
        for st in streams:
            if lvl < levels:
                prod = _dot(both(st["pw"], st["t"]).astype(BF16), st["p_st"])
                st["pw"] = prod[:c]
                st["t"] = st["t"] + prod[c:]
                st["p_st"] = st["stack"](st["pw"])
            else:
                st["t"] = st["t"] + _dot(st["t"].astype(BF16), st["p_st"])
    for st in streams:
        st["u"] = _dot(st["t"].astype(BF16), st["stack"](st["rhs"]))
    outs = []
    for st in streams:
        m_rb = jnp.where(st["incl"], st["xb"][c:], 0.0)
        m_rk = jnp.where(st["incl"], st["xk"][c:], 0.0)
        outs.append(st["xs"][c:] + _dot(m_rb.astype(BF16), st["stack"](st["u"]))
                    + _dot(m_rk.astype(BF16), st["v_st"]))
        uv = both(st["u"], st["v"]).astype(BF16)
        st["s_ref"][...] = st["s"] * st["w_tot"] + jnp.where(st["head_mask"], _tn(uv, st["bk"]), 0.0)
    return outs


def _dot_exact_rhs_lhs(tri_bf16, x):
    xh, xl = _split2(x)
    return _dot(tri_bf16, xh) + _dot(tri_bf16, xl)


def _wkv_kernel(rkvk_f, lwa_f, rkvk_b, lwa_b, ka_ref, of_ref, ob_ref, s_ref, *, nch):
    @pl.when(pl.program_id(1) == 0)
    def _():
        s_ref[...] = jnp.zeros_like(s_ref)

    ka = ka_ref[...]
    c = D_RWKV

    def operands(rkvk_ref, lwa_ref, rows):
        x = rkvk_ref[rows, :]
        la = lwa_ref[rows, :]
        return x[:, :c], x[:, c:2 * c], x[:, 2 * c:3 * c], x[:, 3 * c:], la[:, :c], la[:, c:]

    for ci in range(nch):
        rows_f = slice(ci * CHUNK, (ci + 1) * CHUNK)
        rows_b = slice((nch - 1 - ci) * CHUNK, (nch - ci) * CHUNK)
        fwd = _wkv_direction(*operands(rkvk_f, lwa_f, rows_f), ka, s_ref.at[0], False)
        bwd = _wkv_direction(*operands(rkvk_b, lwa_b, rows_b), ka, s_ref.at[1], True)
        outs = _wkv_streams_step(fwd + bwd)
        of_ref[rows_f, :] = jnp.concatenate(outs[:len(fwd)], axis=1)
        ob_ref[rows_b, :] = jnp.concatenate(outs[len(fwd):], axis=1)


WKV_CHUNKS_PER_STEP = 2


def _wkv(rkvk, lwa, rw_k_a):
    bsz, seq, _ = rkvk.shape
    c = D_RWKV
    nch = WKV_CHUNKS_PER_STEP if seq % (WKV_CHUNKS_PER_STEP * CHUNK) == 0 else 1
    rows = nch * CHUNK
    nb = seq // rows
    fwd = lambda w, lane_blk: pl.BlockSpec((None, rows, w), lambda b, j: (b, j, lane_blk))
    bwd = lambda w, lane_blk: pl.BlockSpec((None, rows, w), lambda b, j: (b, nb - 1 - j, lane_blk))
    return pl.pallas_call(
        functools.partial(_wkv_kernel, nch=nch),
        grid=(bsz, nb),
        in_specs=[fwd(4 * c, 0), fwd(2 * c, 0), bwd(4 * c, 0), bwd(2 * c, 1),
                  pl.BlockSpec((1, c), lambda b, j: (0, 0))],
        out_specs=[fwd(c, 0), bwd(c, 0)],
        out_shape=[jax.ShapeDtypeStruct((bsz, seq, c), F32)] * 2,
        scratch_shapes=[pltpu.VMEM((2, c // MXU_DIM, MXU_DIM, MXU_DIM), F32)],
        compiler_params=_params("arbitrary", "arbitrary"),
        name="wkv7_chunked",
    )(rkvk, lwa, rkvk, lwa, rw_k_a.reshape(1, c))


NEG_INF = float("-inf")


def _first_max(vals, idx, size):
    m = jnp.max(vals, axis=0, keepdims=True)
    i = jnp.min(jnp.where(vals == m, idx, size), axis=0, keepdims=True)
    return m, i


def _route(scores, biased):
    e, tt = scores.shape
    per = e // N_GROUPS
    rowl = lax.broadcasted_iota(jnp.int32, (per, tt), 0)
    gs = []
    for g in range(N_GROUPS):
        blk = biased[g * per:(g + 1) * per]
        m1, i1 = _first_max(blk, rowl, per)
        m2 = jnp.max(jnp.where(rowl == i1, NEG_INF, blk), axis=0, keepdims=True)
        gs.append(m1 + m2)
    cur = jnp.concatenate(gs, axis=0)
    growl = lax.broadcasted_iota(jnp.int32, (N_GROUPS, tt), 0)
    gsel = jnp.zeros((N_GROUPS, tt), F32)
    for _ in range(TOPK_GROUPS):
        _, ig = _first_max(cur, growl, N_GROUPS)
        hit = growl == ig
        gsel = jnp.where(hit, 1.0, gsel)
        cur = jnp.where(hit, NEG_INF, cur)
    emask = jnp.concatenate([jnp.broadcast_to(gsel[g:g + 1], (per, tt)) for g in range(N_GROUPS)], axis=0)
    masked = jnp.where(emask > 0.5, biased, NEG_INF)
    row = lax.broadcasted_iota(jnp.int32, (e, tt), 0)
    ids, ws = [], []
    for _ in range(TOP_K):
        _, ie = _first_max(masked, row, e)
        hit = row == ie
        ids.append(ie)
        ws.append(jnp.sum(jnp.where(hit, scores, 0.0), axis=0, keepdims=True))
        masked = jnp.where(hit, NEG_INF, masked)
    w = jnp.concatenate(ws, axis=0)
    w = w / jnp.sum(w, axis=0, keepdims=True) * ROUTE_SCALE
    return jnp.concatenate(ids, axis=0), w


def _mixout_kernel(x_ref, mod_ref, yhy_ref, of_ref, ob_ref, g_ref, bonus_ref, lnw_ref, lnb_ref,
                   ones_ref, wout_ref, g2n_ref, rwt_ref, bias_ref,
                   x1_ref, h2a_ref, h2b_ref, eid_ref, wsel_ref):
    ones = ones_ref[...]
    s = of_ref[...] + ob_ref[...]
    mean = _dot_exact_rhs(s, ones) * (1.0 / HEAD)
    dlt = s - mean
    var = _dot_exact_rhs(dlt * dlt, ones) * (1.0 / HEAD)
    sn = dlt * lax.rsqrt(var + GN_EPS) * lnw_ref[...] + lnb_ref[...]
    yrw = (sn + bonus_ref[...]) * g_ref[...]
    ch = yhy_ref.shape[-1]
    mix = _dot(yhy_ref[...].astype(BF16), wout_ref[:ch, :]) + _dot(yrw.astype(BF16), wout_ref[ch:, :])
    x1 = x_ref[...] + mod_ref[2:3, :] * mix
    x1_ref[...] = x1
    ms = jnp.mean(x1 * x1, axis=-1, keepdims=True)
    h2 = x1 * lax.rsqrt(ms + NORM_EPS) * g2n_ref[...]
    h2 = h2 * (1.0 + mod_ref[4:5, :]) + mod_ref[3:4, :]
    h2a_ref[...], h2b_ref[...] = _pack_rows(h2)
    rh, rl = _split2(rwt_ref[...])
    hh, hl = _split2(h2)
    logits = _nt(rh, hh) + (_nt(rh, hl) + _nt(rl, hh))
    scores = jax.nn.sigmoid(logits)
    ids, w = _route(scores, scores + bias_ref[...])
    eid_ref[...] = ids
    wsel_ref[...] = w


def _mix_out(x, mod, yhy, o_f, o_b, g, bonus, ln_w, ln_b, w_out, norm2_g, router_w, router_bias, tt=256):
    bsz, seq, d = x.shape
    tt = min(tt, seq)
    nt = seq // tt
    n = bsz * seq
    c = D_RWKV
    e = router_w.shape[1]
    row = lambda a: a.reshape(1, -1)
    consts = [row(ln_w), row(ln_b), _head_ones(), w_out.astype(BF16), row(norm2_g), router_w.T,
              jnp.broadcast_to(router_bias.reshape(e, 1), (e, tt))]
    const = lambda a: pl.BlockSpec(a.shape, lambda b, i: (0,) * a.ndim)
    tile = lambda w: pl.BlockSpec((None, tt, w), lambda b, i: (b, i, 0))
    flat = lambda rows, dt: jax.ShapeDtypeStruct((rows, n), dt)
    return pl.pallas_call(
        _mixout_kernel,
        grid=(bsz, nt),
        in_specs=[tile(d), pl.BlockSpec((None,) + mod.shape[1:], lambda b, i: (b, 0, 0))]
        + [tile(c)] * 5 + [const(a) for a in consts],
        out_specs=[tile(d), pl.BlockSpec((tt, d // 4), lambda b, i: (b * nt + i, 0)),
                   pl.BlockSpec((tt, d // 4), lambda b, i: (b * nt + i, 0)),
                   pl.BlockSpec((TOP_K, tt), lambda b, i: (0, b * nt + i)),
                   pl.BlockSpec((TOP_K, tt), lambda b, i: (0, b * nt + i))],
        out_shape=[jax.ShapeDtypeStruct((bsz, seq, d), F32), jax.ShapeDtypeStruct((n, d // 4), U32),
                   jax.ShapeDtypeStruct((n, d // 4), U32),
                   flat(TOP_K, jnp.int32), flat(TOP_K, F32)],
        compiler_params=_params("arbitrary", "arbitrary"),
        name="mix_out_router",
    )(x, mod, yhy, o_f, o_b, g, bonus, *consts)


BLK = 512
BLK_SHIFT = 9


def _multi_hot(eid, e):
    row = lax.broadcasted_iota(jnp.int32, (e, eid.shape[1]), 0)
    mh = jnp.zeros((e, eid.shape[1]), F32)
    for kk in range(TOP_K):
        mh = mh + jnp.where(row == eid[kk:kk + 1, :], 1.0, 0.0)
    return row, mh


def _lookup(row, eid, table):
    return jnp.concatenate(
        [jnp.sum(jnp.where(row == eid[kk:kk + 1, :], table, 0.0), axis=0, keepdims=True)
         for kk in range(TOP_K)], axis=0)


def _rank_kernel(eid_ref, rank_ref, cnt_ref, *, e):
    @pl.when(pl.program_id(0) == 0)
    def _():
        cnt_ref[...] = jnp.zeros_like(cnt_ref)

    eid = eid_ref[...]
    tt = eid.shape[1]
    row, mh = _multi_hot(eid, e)
    mhb = mh.astype(BF16)
    si = lax.broadcasted_iota(jnp.int32, (tt, tt), 0)
    ti = lax.broadcasted_iota(jnp.int32, (tt, tt), 1)
    earlier = _dot(mhb, jnp.where(si < ti, 1.0, 0.0).astype(BF16))
    cnt = cnt_ref[...]
    full = earlier + jnp.concatenate([cnt] * (tt // LANES), axis=1)
    rank_ref[...] = _lookup(row, eid, full).astype(jnp.int32)
    cnt_ref[...] = cnt + _dot(mhb, jnp.ones((tt, LANES), BF16))


def _expert_ranks(eid, e, tt=512):
    n = eid.shape[1]
    tt = min(tt, n)
    return pl.pallas_call(
        functools.partial(_rank_kernel, e=e),
        grid=(n // tt,),
        in_specs=[pl.BlockSpec((TOP_K, tt), lambda i: (0, i))],
        out_specs=[pl.BlockSpec((TOP_K, tt), lambda i: (0, i)),
                   pl.BlockSpec((e, LANES), lambda i: (0, 0))],
        out_shape=[jax.ShapeDtypeStruct((TOP_K, n), jnp.int32), jax.ShapeDtypeStruct((e, LANES), F32)],
        compiler_params=_params("arbitrary"),
        name="expert_ranks",
    )(eid)


def _block_offsets(cnt):
    e = cnt.shape[0]
    nblk = ((cnt.astype(jnp.int32) + (BLK - 1)) >> BLK_SHIFT).astype(F32)
    ri = lax.broadcasted_iota(jnp.int32, (e, e), 0)
    ci = lax.broadcasted_iota(jnp.int32, (e, e), 1)
    tril = jnp.where(ci <= ri, 1.0, 0.0).astype(BF16)
    nh, nl = _split2(nblk)
    return nblk, _dot(tril, nh) + _dot(tril, nl)


def _dest_kernel(cnt_ref, eid_ref, rank_ref, dest_ref):
    nblk, end = _block_offsets(cnt_ref[...])
    off = (end - nblk) * float(BLK)
    eid = eid_ref[...]
    tt = eid.shape[1]
    row = lax.broadcasted_iota(jnp.int32, (off.shape[0], tt), 0)
    table = jnp.concatenate([off] * (tt // LANES), axis=1)
    dest_ref[...] = _lookup(row, eid, table).astype(jnp.int32) + rank_ref[...]


def _destinations(cnt, eid, rank, tt=512):
    n = eid.shape[1]
    tt = min(tt, n)
    blk = pl.BlockSpec((TOP_K, tt), lambda i: (0, i))
    return pl.pallas_call(
        _dest_kernel,
        grid=(n // tt,),
        in_specs=[pl.BlockSpec(cnt.shape, lambda i: (0, 0)), blk, blk],
        out_specs=blk,
        out_shape=jax.ShapeDtypeStruct((TOP_K, n), jnp.int32),
        compiler_params=_params("arbitrary"),
        name="expert_destinations",
    )(cnt, eid, rank)


def _meta_kernel(cnt_ref, meta_ref, *, nbp):
    cnt = cnt_ref[...]
    e = cnt.shape[0]
    nblk, end = _block_offsets(cnt)
    rep = lambda a, w: jnp.concatenate([a] * (w // LANES), axis=1)
    b = lax.broadcasted_iota(jnp.int32, (e, nbp), 1).astype(F32)
    blk_e = jnp.minimum(jnp.sum(jnp.where(rep(end, nbp) <= b, 1.0, 0.0), axis=0, keepdims=True), float(e - 1))
    row = lax.broadcasted_iota(jnp.int32, (e, nbp), 0).astype(F32)
    mine = row == blk_e
    left = rep(cnt + (end - nblk) * float(BLK), nbp) - b * float(BLK)
    nvalid = jnp.clip(jnp.sum(jnp.where(mine, left, 0.0), axis=0, keepdims=True), 0.0, float(BLK))
    nused = jnp.max(rep(end, nbp), axis=0, keepdims=True)
    later = jnp.logical_and(row > blk_e, rep(nblk, nbp) > 0.0)
    nxt = jnp.min(jnp.where(later, row, float(e)), axis=0, keepdims=True)
    nxt = jnp.where(nxt >= float(e), -1.0, nxt)
    meta_ref[...] = jnp.concatenate([blk_e, nvalid, nused, nxt, jnp.zeros((4, nbp), F32)],
                                    axis=0).astype(jnp.int32)


def _block_meta(cnt, nb):
    nbp = -(-nb // LANES) * LANES
    return pl.pallas_call(
        functools.partial(_meta_kernel, nbp=nbp),
        out_shape=jax.ShapeDtypeStruct((8, nbp), jnp.int32),
        compiler_params=pltpu.CompilerParams(vmem_limit_bytes=VMEM_LIMIT),
        name="expert_block_meta",
    )(cnt)


SC_WINDOW = 128


def _sc_mesh():
    return plsc.VectorSubcoreMesh(core_axis_name="core", subcore_axis_name="subcore")


def _sc_scatter_rows(rows, idx, nrows):
    n, width = rows.shape

    @pl.kernel(out_type=jax.ShapeDtypeStruct((nrows, width), rows.dtype), mesh=_sc_mesh())
    def scatter(rows_hbm, idx_hbm, out_hbm):
        def body(rows_vmem, idx_vmem):
            pltpu.sync_copy(rows_vmem, out_hbm.at[idx_vmem.at[0]])

        pltpu.emit_pipeline(
            body,
            grid=(n // SC_WINDOW, idx.shape[0]),
            in_specs=[pl.BlockSpec((SC_WINDOW, width), index_map=lambda i, k: (i, 0)),
                      pl.BlockSpec((1, SC_WINDOW), index_map=lambda i, k: (k, i))],
            out_specs=[],
            core_axis_name=("core", "subcore"),
            dimension_semantics=(pltpu.PARALLEL, pltpu.ARBITRARY),
        )(rows_hbm, idx_hbm)

    return scatter(rows, idx)


def _sc_gather_rows(src, idx):
    num = idx.shape[1]
    width = src.shape[1]

    @pl.kernel(out_type=jax.ShapeDtypeStruct((num, width), src.dtype), mesh=_sc_mesh())
    def gather(src_hbm, idx_hbm, out_hbm):
        def body(idx_vmem, out_vmem):
            pltpu.sync_copy(src_hbm.at[idx_vmem.at[0]], out_vmem)

        pltpu.emit_pipeline(
            body,
            grid=(num // SC_WINDOW,),
            in_specs=[pl.BlockSpec((1, SC_WINDOW), index_map=lambda i: (0, i))],
            out_specs=[pl.BlockSpec((SC_WINDOW, width), index_map=lambda i: (i, 0))],
            core_axis_name=("core", "subcore"),
            dimension_semantics=(pltpu.PARALLEL,),
        )(idx_hbm, out_hbm)

    return gather(src, idx)


def _experts_kernel(be_ref, nv_ref, nu_ref, nxt_ref, xa_ref, xb_ref, wg_hbm, wu_hbm, wd_hbm, oa_ref, ob_ref,
                    wgf, wuf, wdf, wgb, wub, wdb, sems, slot_ref):
    b = pl.program_id(0)
    used = b < nu_ref[0]

    def fetch(expert, slot):
        return [pltpu.make_async_copy(src.at[expert], dst.at[slot], sems.at[slot])
                for src, dst in ((wg_hbm, wgf), (wu_hbm, wuf), (wd_hbm, wdf))]

    @pl.when(b == 0)
    def _():
        slot_ref[0] = 0
        for cp in fetch(be_ref[0], 0):
            cp.start()

    @pl.when(used)
    def _():
        prev = be_ref[jnp.maximum(b - 1, 0)]

        @pl.when(jnp.logical_or(b == 0, be_ref[b] != prev))
        def _():
            slot = slot_ref[0]
            for cp in fetch(be_ref[b], slot):
                cp.wait()

            @pl.when(nxt_ref[b] >= 0)
            def _():
                for cp in fetch(nxt_ref[b], 1 - slot):
                    cp.start()

            wgb[...] = wgf[slot].astype(BF16)
            wub[...] = wuf[slot].astype(BF16)
            wdb[...] = wdf[slot].astype(BF16)
            slot_ref[0] = 1 - slot

        valid = lax.broadcasted_iota(jnp.int32, (xa_ref.shape[0], 1), 0) < nv_ref[b]
        zero = jnp.uint32(0)
        x = _unpack_rows(jnp.where(valid, xa_ref[...], zero), jnp.where(valid, xb_ref[...], zero))
        act = _silu(_dot(x, wgb[...])) * _dot(x, wub[...])
        oa_ref[...], ob_ref[...] = _pack_rows(_dot(act.astype(BF16), wdb[...]))

    @pl.when(jnp.logical_not(used))
    def _():
        oa_ref[...] = jnp.zeros_like(oa_ref)
        ob_ref[...] = jnp.zeros_like(ob_ref)


def _experts(blk_e, nvalid, nused, nxt_e, xs_a, xs_b, wg, wu, wd):
    p, dq = xs_a.shape
    nb = p // BLK
    d, de = wg.shape[1], wg.shape[2]
    rows_in = pl.BlockSpec((BLK, dq), lambda b, be, nv, nu, nx: (jnp.minimum(b, nu[0] - 1), 0))
    hbm = pl.BlockSpec(memory_space=pl.ANY)
    return pl.pallas_call(
        _experts_kernel,
        grid_spec=pltpu.PrefetchScalarGridSpec(
            num_scalar_prefetch=4,
            grid=(nb,),
            in_specs=[rows_in, rows_in, hbm, hbm, hbm],
            out_specs=[pl.BlockSpec((BLK, dq), lambda b, be, nv, nu, nx: (b, 0))] * 2,
            scratch_shapes=[pltpu.VMEM((2, d, de), F32), pltpu.VMEM((2, d, de), F32), pltpu.VMEM((2, de, d), F32),
                            pltpu.VMEM((d, de), BF16), pltpu.VMEM((d, de), BF16), pltpu.VMEM((de, d), BF16),
                            pltpu.SemaphoreType.DMA((2,)), pltpu.SMEM((1,), jnp.int32)],
        ),
        out_shape=[jax.ShapeDtypeStruct((p, dq), U32)] * 2,
        compiler_params=_params("arbitrary"),
        name="moe_experts",
    )(blk_e, nvalid, nused, nxt_e, xs_a, xs_b, wg, wu, wd)


def _combine_kernel(w_ref, x1_ref, ha_ref, hb_ref, mod_ref, ga_ref, gb_ref, sg_ref, su_ref, sd_ref, gf_ref,
                    sel_ref, o_ref):
    hb = _unpack_rows(ha_ref[...], hb_ref[...])
    act = _silu(_dot(hb, sg_ref[...])) * _dot(hb, su_ref[...])
    ffn = _dot(act.astype(BF16), sd_ref[...])
    wh, wl = _split2(w_ref[...])
    acc = None
    for kk in range(TOP_K):
        sel = sel_ref[kk]
        wk = _tn(wh, sel) + _tn(wl, sel)
        a_lo, a_hi = _unpack_halves(ga_ref[kk])
        b_lo, b_hi = _unpack_halves(gb_ref[kk])
        parts = [a_lo * wk, b_lo * wk, a_hi * wk, b_hi * wk]
        acc = parts if acc is None else [p + q for p, q in zip(acc, parts)]
    ffn = ffn + jnp.concatenate(acc, axis=1)
    xo = x1_ref[...] + mod_ref[5:6, :] * ffn
    ms = jnp.mean(xo * xo, axis=-1, keepdims=True)
    o_ref[...] = xo * lax.rsqrt(ms + NORM_EPS) * gf_ref[...]


def _combine(wsel, x1, h2a, h2b, mod, ga, gb, sh_wg, sh_wu, sh_wd, normf_g, tt=256):
    bsz, seq, d = x1.shape
    n = bsz * seq
    tt = min(tt, seq)
    per = seq // tt
    dp = h2a.shape[1]
    dq = ga.shape[2]
    sel = jnp.asarray(np.broadcast_to(np.eye(TOP_K)[:, :, None], (TOP_K, TOP_K, dq)), BF16)
    consts = [sh_wg.astype(BF16), sh_wu.astype(BF16), sh_wd.astype(BF16), normf_g.reshape(1, d), sel]
    const = lambda a: pl.BlockSpec(a.shape, lambda i: (0,) * a.ndim)
    rows = pl.BlockSpec((tt, d), lambda i: (i, 0))
    packed_rows = pl.BlockSpec((tt, dp), lambda i: (i, 0))
    gathered = pl.BlockSpec((TOP_K, tt, dq), lambda i: (0, i, 0))
    return pl.pallas_call(
        _combine_kernel,
        grid=(n // tt,),
        in_specs=[pl.BlockSpec((TOP_K, tt), lambda i: (0, i)),
                  rows, packed_rows, packed_rows,
                  pl.BlockSpec((None,) + mod.shape[1:], lambda i: (i // per, 0, 0)),
                  gathered, gathered] + [const(a) for a in consts],
        out_specs=rows,
        out_shape=jax.ShapeDtypeStruct((n, d), F32),
        compiler_params=_params("arbitrary"),
        name="moe_combine",
    )(wsel, x1.reshape(n, d), h2a, h2b, mod, ga, gb, *consts)


def _moe(x1, h2a, h2b, mod, eid, wsel, exp_wg, exp_wu, exp_wd, sh_wg, sh_wu, sh_wd, normf_g):
    n = h2a.shape[0]
    e = exp_wg.shape[0]
    nb = (n * TOP_K + e * (BLK - 1)) // BLK
    rank, cnt = _expert_ranks(eid, e)
    dest = _destinations(cnt, eid, rank)
    meta = _block_meta(cnt, nb)
    xs_a = _sc_scatter_rows(h2a, dest, nb * BLK)
    xs_b = _sc_scatter_rows(h2b, dest, nb * BLK)
    ys_a, ys_b = _experts(meta[0, :nb], meta[1, :nb], meta[2, :1], meta[3, :nb], xs_a, xs_b,
                          exp_wg, exp_wu, exp_wd)
    idx = dest.reshape(1, TOP_K * n)
    ga = _sc_gather_rows(ys_a, idx).reshape(TOP_K, n, -1)
    gb = _sc_gather_rows(ys_b, idx).reshape(TOP_K, n, -1)
    return _combine(wsel, x1, h2a, h2b, mod, ga, gb, sh_wg, sh_wu, sh_wd, normf_g)


def kernel(x, c, norm1_g, norm2_g, normf_g, w_ada, b_ada, w_in, w_out, hy_conv_w, hy_conv_b, hy_pos_w1, hy_pos_b1, hy_pos_w2, hy_pos_b2, hy_pos_w3, hy_sin_freq, hy_skip, rw_mu, rw_w0, rw_w_up, rw_a0, rw_a_up, rw_g_up, rw_k_k, rw_k_a, rw_r_k, rw_ln_w, rw_ln_b, router_w, router_bias, exp_w_gate, exp_w_up, exp_w_down, sh_w_gate, sh_w_up, sh_w_down):
    bsz, seq, d = x.shape
    depth = w_ada.shape[0]
    assert depth == 1, "the final norm is fused into the last kernel of a single layer"
    for l in range(depth):
        mod = _modulation(c, w_ada[l], b_ada[l]).reshape(bsz, -1, d)
        uhy, rkvk, lwa, g, bonus = _projection(
            x, mod, norm1_g[l], w_in[l], hy_conv_w[l], hy_conv_b[l], rw_mu[l], rw_w0[l], rw_w_up[l],
            rw_a0[l], rw_a_up[l], rw_g_up[l], rw_k_k[l], rw_k_a[l], rw_r_k[l])
        k2, ss = _hyena_filters(seq, hy_pos_w1[l], hy_pos_b1[l], hy_pos_w2[l], hy_pos_b2[l],
                                hy_pos_w3[l], hy_sin_freq[l])
        khat = _filter_spectrum(k2, ss, seq)
        z, z_col = uhy, 0
        for order in range(HYENA_ORDER):
            z = _long_conv_gate(z, z_col, uhy, (order + 1) * D_HYENA, khat, hy_skip[l], order)
            z_col = 0
        o_f, o_b = _wkv(rkvk, lwa, rw_k_a[l])
        x1, h2a, h2b, eid, wsel = _mix_out(x, mod, z, o_f, o_b, g, bonus, rw_ln_w[l], rw_ln_b[l], w_out[l],
                                           norm2_g[l], router_w[l], router_bias[l])
        x = _moe(x1, h2a, h2b, mod, eid, wsel, exp_w_gate[l], exp_w_up[l], exp_w_down[l],
                 sh_w_gate[l], sh_w_up[l], sh_w_down[l], normf_g)
        x = x.reshape(bsz, seq, d)
    return x
```

```python
import functools
import math

import jax
import jax.numpy as jnp
import numpy as np
from jax import lax
from jax.experimental import pallas as pl
from jax.experimental.pallas import tpu as pltpu
from jax.experimental.pallas import tpu_sc as plsc

F32 = jnp.float32
BF16 = jnp.bfloat16

LANES = 128
MXU_DIM = 256
VMEM_LIMIT = 56 * 1024 * 1024

D_HYENA = 512
D_RWKV = 512
HEAD = 64
N_HEADS = D_RWKV // HEAD
HYENA_ORDER = 2
FILTER_BANDS = 16
DECAY_TARGET = 1e-2
FAST_DECAY_PCT = 0.3
SLOW_DECAY_PCT = 1.5
FILTER_NORM_EPS = 1e-6
DECAY_LORA = 32
ICLR_LORA = 32
GATE_LORA = 96
GN_EPS = 64e-5
NORM_EPS = 1e-6
N_EXPERTS = 256
TOP_K = 8
N_GROUPS = 8
TOPK_GROUPS = 4
ROUTE_SCALE = 2.5
D_EXPERT = 256


def _params(*sem):
    return pltpu.CompilerParams(dimension_semantics=sem, vmem_limit_bytes=VMEM_LIMIT)


def _split2(a):
    hi = a.astype(BF16)
    lo = (a - hi.astype(F32)).astype(BF16)
    return hi, lo


def _dot(a, b):
    return jnp.dot(a, b, preferred_element_type=F32)


def _dot3(a, b):
    ah, al = _split2(a)
    bh, bl = _split2(b)
    return _dot(ah, bh) + (_dot(ah, bl) + _dot(al, bh))


def _dot_exact_rhs(a, b_bf16):
    ah, al = _split2(a)
    return _dot(ah, b_bf16) + _dot(al, b_bf16)


def _silu(x):
    return x * jax.nn.sigmoid(x)


U32 = jnp.int32


def _pack_halves(x):
    w = x.shape[1] // 2
    return pltpu.pack_elementwise([x[:, :w], x[:, w:]], packed_dtype=BF16)


def _unpack_halves(p):
    lo = pltpu.unpack_elementwise(p, index=0, packed_dtype=BF16, unpacked_dtype=F32)
    hi = pltpu.unpack_elementwise(p, index=1, packed_dtype=BF16, unpacked_dtype=F32)
    return lo, hi


def _pack_rows(x):
    packed = _pack_halves(x)
    half = packed.shape[1] // 2
    return packed[:, :half], packed[:, half:]


def _unpack_rows(a, b):
    a_lo, a_hi = _unpack_halves(a)
    b_lo, b_hi = _unpack_halves(b)
    return jnp.concatenate([a_lo.astype(BF16), b_lo.astype(BF16), a_hi.astype(BF16), b_hi.astype(BF16)], axis=1)


def _mod_kernel(c_ref, w_ref, b_ref, o_ref):
    o_ref[...] = _dot3(_silu(c_ref[...]), w_ref[...]) + b_ref[...]


def _modulation(c, w_ada, b_ada):
    bsz, d = c.shape
    n = w_ada.shape[1]
    blk = 1024
    return pl.pallas_call(
        _mod_kernel,
        grid=(n // blk,),
        in_specs=[
            pl.BlockSpec((bsz, d), lambda j: (0, 0)),
            pl.BlockSpec((d, blk), lambda j: (0, j)),
            pl.BlockSpec((1, blk), lambda j: (0, j)),
        ],
        out_specs=pl.BlockSpec((bsz, blk), lambda j: (0, j)),
        out_shape=jax.ShapeDtypeStruct((bsz, n), F32),
        compiler_params=_params("arbitrary"),
        name="adaln_mod",
    )(c, w_ada, b_ada.reshape(1, n))


def _filter_kernel(band_ref, w1_ref, b1_ref, w2_ref, b2_ref, w3_ref, freq_ref, delta_ref,
                   k_ref, ss_ref, *, seq, rows):
    half = pl.program_id(0)
    i = pl.program_id(1)
    r = lax.broadcasted_iota(jnp.int32, (rows, LANES), 0) + i * rows
    pos = jnp.where(half == 0, r, seq - r).astype(F32)
    tt = pos / float(max(seq - 1, 1))
    lane = lax.broadcasted_iota(jnp.int32, (rows, LANES), 1)
    ang = pos * band_ref[...]
    feats = jnp.where(lane == 0, tt,
                      jnp.where(lane <= FILTER_BANDS, jnp.cos(ang),
                                jnp.where(lane <= 2 * FILTER_BANDS, -jnp.sin(ang), 0.0)))
    freq = freq_ref[...]
    hdn = jnp.sin(freq * (_dot3(feats, w1_ref[...]) + b1_ref[...]))
    for j in range(w2_ref.shape[0]):
        hdn = jnp.sin(freq * (_dot3(hdn, w2_ref[j]) + b2_ref[j]))
    filt = _dot3(hdn, w3_ref[...])
    filt = filt * jnp.exp(-tt[:, :1] * delta_ref[...])
    valid = jnp.logical_or(half == 0, r[:, :1] > 0)
    filt = jnp.where(valid, filt, 0.0)
    k_ref[...] = filt

    @pl.when(jnp.logical_and(half == 0, i == 0))
    def _():
        ss_ref[...] = jnp.zeros_like(ss_ref)

    ss_ref[...] += jnp.broadcast_to(jnp.sum(filt * filt, axis=0, keepdims=True), ss_ref.shape)


def _hyena_filters(seq, pw1, pb1, pw2, pb2, pw3, freq):
    width = pw1.shape[1]
    ncol = HYENA_ORDER * D_HYENA
    rows = min(seq, 512)
    bands = np.zeros((1, LANES), np.float64)
    lin = np.linspace(1e-4, FILTER_BANDS - 1, FILTER_BANDS)
    bands[0, 1:1 + FILTER_BANDS] = lin
    bands[0, 1 + FILTER_BANDS:1 + 2 * FILTER_BANDS] = lin
    bands = jnp.asarray(bands * (2.0 * math.pi / seq), F32)
    deltas = np.abs(np.linspace(math.log(DECAY_TARGET) / SLOW_DECAY_PCT,
                                math.log(DECAY_TARGET) / FAST_DECAY_PCT, D_HYENA))
    deltas = jnp.asarray(np.tile(deltas, HYENA_ORDER)[None], F32)
    w1 = jnp.zeros((LANES, width), F32).at[:pw1.shape[0]].set(pw1)
    w3 = pw3.reshape(width, HYENA_ORDER, 2, D_HYENA).transpose(2, 0, 1, 3).reshape(2, width, ncol)
    nt = seq // rows
    full = lambda *shape: pl.BlockSpec(shape, lambda h, i: (0,) * len(shape))
    return pl.pallas_call(
        functools.partial(_filter_kernel, seq=seq, rows=rows),
        grid=(2, nt),
        in_specs=[
            full(1, LANES), full(LANES, width), full(1, width),
            full(pw2.shape[0], width, width), full(pw2.shape[0], 1, width),
            pl.BlockSpec((None, width, ncol), lambda h, i: (h, 0, 0)),
            full(1, width), full(1, ncol),
        ],
        out_specs=[
            pl.BlockSpec((rows, ncol), lambda h, i: (h * nt + i, 0)),
            pl.BlockSpec((8, ncol), lambda h, i: (0, 0)),
        ],
        out_shape=[jax.ShapeDtypeStruct((2 * seq, ncol), F32),
                   jax.ShapeDtypeStruct((8, ncol), F32)],
        compiler_params=_params("arbitrary", "arbitrary"),
        name="hyena_filters",
    )(bands, w1, pb1.reshape(1, width), pw2, pb2.reshape(pw2.shape[0], 1, width), w3,
      freq.reshape(1, width), deltas)


N1 = LANES
UNROLL = 8


def _dft_tables(seq):
    tables = _dft_tables_np(seq)
    return tuple(jnp.asarray(t, BF16) for t in tables[:5]) + tables[5:]


def _dft_tables_np(seq):
    m = 2 * seq
    n2 = m // N1
    n2h = n2 // 2
    n1 = np.arange(N1)[:, None, None]
    f2 = np.arange(n2)[None, :, None]
    k2 = np.arange(n2)[None, None, :]
    th = 2.0 * np.pi * (n1 * f2 / m + (k2 * f2 % n2) / n2)
    fwd_a = np.concatenate([np.cos(th), -np.sin(th)], axis=1)
    tht = np.transpose(th, (0, 2, 1))
    inv_a = np.concatenate([np.cos(tht), -np.sin(tht)], axis=2)[:, :n2h] / m
    a = np.arange(N1)
    ph = 2.0 * np.pi * np.outer(a, a) / N1
    c, s = np.cos(ph), np.sin(ph)
    fwd_b = np.block([[c, s], [-s, c]])
    inv_b = np.block([[c, -s], [s, c]])
    return fwd_a, fwd_a[:, :, :n2h], inv_a, fwd_b, inv_b, n2, n2h


def _stage_a_fwd(x_ref, wa_ref, y_ref, n2, scale=None):
    def body(i, carry):
        trips = [i * UNROLL + j for j in range(UNROLL)]
        xs = [x_ref[pl.ds(n1, wa_ref.shape[2], stride=N1), :] for n1 in trips]
        if scale is not None:
            xs = [x * scale for x in xs]
        prods = [_dot(wa_ref[n1], x.astype(BF16)) for n1, x in zip(trips, xs)]
        for n1, a in zip(trips, prods):
            y_ref[pl.ds(n1, n2, stride=2 * N1), :] = a[:n2]
            y_ref[pl.ds(N1 + n1, n2, stride=2 * N1), :] = a[n2:]
        return carry
    lax.fori_loop(0, N1 // UNROLL, body, 0)


def _filter_fft_kernel(k_ref, ss_ref, wa_ref, fb_ref, o_ref, y_ref, *, n2):
    scale = lax.rsqrt(ss_ref[0:1, :] + FILTER_NORM_EPS)
    _stage_a_fwd(k_ref, wa_ref, y_ref, n2, scale=scale)

    unr = min(UNROLL, n2)

    def body(i, carry):
        trips = [i * unr + j for j in range(unr)]
        ys = [y_ref[pl.ds(pl.multiple_of(f2 * 2 * N1, 2 * N1), 2 * N1), :].astype(BF16) for f2 in trips]
        for f2, y in zip(trips, ys):
            o_ref[f2] = _dot(fb_ref[...], y)
        return carry
    lax.fori_loop(0, n2 // unr, body, 0)


def _filter_spectrum(k2, ss, seq):
    fwd_a, _, _, fwd_b, _, n2, _ = _dft_tables(seq)
    ncol = k2.shape[1]
    nblk = ncol // LANES
    return pl.pallas_call(
        functools.partial(_filter_fft_kernel, n2=n2),
        grid=(nblk,),
        in_specs=[
            pl.BlockSpec((2 * seq, LANES), lambda c: (0, c)),
            pl.BlockSpec((8, LANES), lambda c: (0, c)),
            pl.BlockSpec(fwd_a.shape, lambda c: (0, 0, 0)),
            pl.BlockSpec(fwd_b.shape, lambda c: (0, 0)),
        ],
        out_specs=pl.BlockSpec((None, n2, 2 * N1, LANES), lambda c: (c, 0, 0, 0)),
        out_shape=jax.ShapeDtypeStruct((nblk, n2, 2 * N1, LANES), F32),
        scratch_shapes=[pltpu.VMEM((n2 * 2 * N1, LANES), F32)],
        compiler_params=_params("arbitrary"),
        name="hyena_filter_fft",
    )(k2, ss, fwd_a, fwd_b)


TILE = 8
N1_GROUPS = N1 // TILE


def _tile_tables(seq):
    _, fwd_a, inv_a, _, _, n2, n2h = _dft_tables_np(seq)
    eye = np.eye(TILE)
    fa = fwd_a.reshape(N1_GROUPS, TILE, 2 * n2, n2h)
    wa = np.einsum("qjrn,jk->qrjnk", fa, eye).reshape(N1_GROUPS, 2 * n2 * TILE, n2h * TILE)
    ia = inv_a.reshape(N1_GROUPS, TILE, n2h, 2 * n2)
    vc = np.einsum("qjnr,jk->qnjrk", ia, eye).reshape(N1_GROUPS, n2h * TILE, 2 * n2 * TILE)
    return jnp.asarray(wa, BF16), jnp.asarray(vc, BF16)


def _conv_kernel(u_ref, g_ref, skip_ref, fb_ref, ib_ref, kh_hbm, wa_hbm, vc_hbm, o_ref,
                 y_ref, kh_ref, wa_ref, vc_ref, sem, *, n2, n2h, kh_first):
    c_id, b_id = pl.program_id(0), pl.program_id(1)

    @pl.when(jnp.logical_and(c_id == 0, b_id == 0))
    def _():
        for src, dst in ((wa_hbm, wa_ref), (vc_hbm, vc_ref)):
            cp = pltpu.make_async_copy(src, dst, sem)
            cp.start()
            cp.wait()

    @pl.when(b_id == 0)
    def _():
        cp = pltpu.make_async_copy(kh_hbm.at[kh_first + c_id], kh_ref, sem)
        cp.start()
        cp.wait()

    def y_tile(rf, base):
        ri, f2 = divmod(rf, n2)
        return pl.ds(f2 * 2 * N1 + ri * N1 + base, TILE)

    def stage_a(q, carry):
        base = pl.multiple_of(q * TILE, TILE)
        x = jnp.concatenate([u_ref[pl.ds(N1 * m + base, TILE), :] for m in range(n2h)], axis=0)
        r = _dot(wa_ref[q], x.astype(BF16))
        for rf in range(2 * n2):
            y_ref[y_tile(rf, base), :] = r[rf * TILE:(rf + 1) * TILE]
        return carry
    lax.fori_loop(0, N1_GROUPS, stage_a, 0, unroll=2)

    unr = min(UNROLL, n2)

    def mid(i, carry):
        trips = [i * unr + j for j in range(unr)]
        offs = [pl.multiple_of(f2 * 2 * N1, 2 * N1) for f2 in trips]
        zs = [_dot(fb_ref[...], y_ref[pl.ds(off, 2 * N1), :].astype(BF16)) for off in offs]
        ps = []
        for f2, z in zip(trips, zs):
            zr, zi = z[:N1], z[N1:]
            kh = kh_ref[f2]
            kr, ki = kh[:N1], kh[N1:]
            ps.append(jnp.concatenate([zr * kr - zi * ki, zr * ki + zi * kr], axis=0).astype(BF16))
        gs = [_dot(ib_ref[...], p) for p in ps]
        for off, g in zip(offs, gs):
            y_ref[pl.ds(off, 2 * N1), :] = g
        return carry
    lax.fori_loop(0, n2 // unr, mid, 0)

    skip = skip_ref[...]

    def stage_c(q, carry):
        base = pl.multiple_of(q * TILE, TILE)
        g = jnp.concatenate([y_ref[y_tile(rf, base), :] for rf in range(2 * n2)], axis=0)
        conv = _dot(vc_ref[q], g.astype(BF16))
        for m in range(n2h):
            rows = pl.ds(N1 * m + base, TILE)
            o_ref[rows, :] = g_ref[rows, :] * (conv[m * TILE:(m + 1) * TILE] + u_ref[rows, :] * skip)
        return carry
    lax.fori_loop(0, N1_GROUPS, stage_c, 0, unroll=2)


def _long_conv_gate(u, u_col, gate, gate_col, khat, skip, order):
    bsz, seq, _ = u.shape
    ch = D_HYENA
    _, _, _, fwd_b, inv_b, n2, n2h = _dft_tables(seq)
    wa, vc = _tile_tables(seq)
    nblk = ch // LANES
    const = lambda a: pl.BlockSpec(a.shape, lambda c, b: (0,) * a.ndim)
    at = lambda col: pl.BlockSpec((None, seq, LANES), lambda c, b: (b, 0, col // LANES + c))
    hbm = pl.BlockSpec(memory_space=pl.ANY)
    return pl.pallas_call(
        functools.partial(_conv_kernel, n2=n2, n2h=n2h, kh_first=order * nblk),
        grid=(nblk, bsz),
        in_specs=[
            at(u_col), at(gate_col),
            pl.BlockSpec((1, LANES), lambda c, b: (0, c)),
            const(fwd_b), const(inv_b), hbm, hbm, hbm,
        ],
        out_specs=at(0),
        out_shape=jax.ShapeDtypeStruct((bsz, seq, ch), F32),
        scratch_shapes=[pltpu.VMEM((n2 * 2 * N1, LANES), F32), pltpu.VMEM(khat.shape[1:], F32),
                        pltpu.VMEM(wa.shape, BF16), pltpu.VMEM(vc.shape, BF16), pltpu.SemaphoreType.DMA(())],
        compiler_params=_params("arbitrary", "arbitrary"),
        name=f"hyena_conv{order}",
    )(u, gate, skip[order].reshape(1, ch), fwd_b, inv_b, khat, wa, vc)


HALO = 8


def _shift_rows(p, k):
    return pltpu.roll(p, k % p.shape[0], axis=0)


def _proj_kernel(xp_ref, x_ref, xn_ref, mod_ref, g1_ref, why_ref, wrkv_ref, wlora_ref,
                 cw_ref, cb_ref, murkv_ref, mulora_ref, w0_ref, a0_ref, wwah_ref, wwal_ref, gup_ref,
                 kk_ref, ka_ref, rk_ref, ones_ref,
                 uhy_ref, rkvk_ref, lwa_ref, g_ref, bonus_ref,
                 *, tt, nt):
    i = pl.program_id(1)
    xe = jnp.concatenate([xp_ref[...], x_ref[...], xn_ref[...]], axis=0)
    ms = jnp.mean(xe * xe, axis=-1, keepdims=True)
    h = xe * lax.rsqrt(ms + NORM_EPS) * g1_ref[...]
    h = h * (1.0 + mod_ref[1:2, :]) + mod_ref[0:1, :]
    row = lax.broadcasted_iota(jnp.int32, (tt + 2 * HALO, 1), 0)
    inside = jnp.logical_and(jnp.logical_or(row >= HALO, i > 0),
                             jnp.logical_or(row < tt + HALO, i < nt - 1))
    hb = jnp.where(inside, h, 0.0).astype(BF16)
    mid = slice(HALO, tt + HALO)

    p = _dot(hb, why_ref[...])
    u = (_shift_rows(p, 1) * cw_ref[0:1, :] + p * cw_ref[1:2, :]
         + _shift_rows(p, -1) * cw_ref[2:3, :] + cb_ref[...])
    uhy_ref[...] = u[mid]

    p = _dot(hb, wrkv_ref[...])
    p = p + murkv_ref[...] * (0.5 * (_shift_rows(p, 1) + _shift_rows(p, -1)) - p)
    p = p[mid]
    c = D_RWKV
    r, k, v = p[:, :c], p[:, c:2 * c], p[:, 2 * c:]
    rkvk_ref[:, :3 * c] = p

    q = _dot(hb, wlora_ref[...])
    q = q + mulora_ref[...] * (0.5 * (_shift_rows(q, 1) + _shift_rows(q, -1)) - q)
    q = q[mid]
    wa = q[:, :LANES]
    lane = lax.broadcasted_iota(jnp.int32, wa.shape, 1)
    wa = jnp.where(lane < 2 * DECAY_LORA, jnp.tanh(wa), wa)
    wah, wal = _split2(wa)
    up = _dot(wah, wwah_ref[...]) + (_dot(wah, wwal_ref[...]) + _dot(wal, wwah_ref[...]))
    z = -(w0_ref[...] + up[:, :2 * c])
    softplus = jnp.maximum(z, 0.0) + jnp.log1p(jnp.exp(-jnp.abs(z)))
    lw = -jnp.exp(-softplus - 0.5)
    a = jax.nn.sigmoid(a0_ref[...] + up[:, 2 * c:])
    for dd in range(2):
        lwa_ref[:, 2 * dd * c:(2 * dd + 1) * c] = lw[:, dd * c:(dd + 1) * c]
        lwa_ref[:, (2 * dd + 1) * c:(2 * dd + 2) * c] = a[:, dd * c:(dd + 1) * c]
    g_ref[...] = _dot3(jax.nn.sigmoid(q[:, LANES:]), gup_ref[...])

    ones = ones_ref[...]
    kk = k * kk_ref[...]
    nrm = jnp.sqrt(_dot_exact_rhs(kk * kk, ones))
    rkvk_ref[:, 3 * c:] = kk / jnp.maximum(nrm, 1e-12)
    ka = ka_ref[...]
    ksum = k * (2.0 + (a[:, :c] + a[:, c:] - 2.0) * ka)
    bonus_ref[...] = _dot_exact_rhs(r * ksum * rk_ref[...], ones) * v


def _head_ones():
    hid = np.arange(D_RWKV) // HEAD
    return jnp.asarray(hid[:, None] == hid[None, :], BF16)


def _projection(x, mod, norm1_g, w_in, hy_conv_w, hy_conv_b, rw_mu, rw_w0, rw_w_up, rw_a0,
                rw_a_up, rw_g_up, rw_k_k, rw_k_a, rw_r_k, tt=256):
    bsz, seq, d = x.shape
    tt = min(tt, seq)
    nt = seq // tt
    c = D_RWKV
    hy = (HYENA_ORDER + 1) * D_HYENA
    nlora = 2 * LANES
    w_hy = w_in[:, :hy].astype(BF16)
    w_rkv = w_in[:, hy:hy + 3 * c].astype(BF16)
    w_lora = jnp.zeros((d, nlora), F32).at[:, :w_in.shape[1] - hy - 3 * c].set(w_in[:, hy + 3 * c:]).astype(BF16)
    mu_rkv = rw_mu[:3 * c].reshape(1, 3 * c)
    mu_lora = jnp.zeros((1, nlora), F32).at[0, :rw_mu.shape[0] - 3 * c].set(rw_mu[3 * c:])
    wwa = jnp.zeros((LANES, 4 * c), F32)
    for dd in range(2):
        wwa = wwa.at[dd * DECAY_LORA:(dd + 1) * DECAY_LORA, dd * c:(dd + 1) * c].set(rw_w_up[dd])
        wwa = wwa.at[2 * DECAY_LORA + dd * ICLR_LORA:2 * DECAY_LORA + (dd + 1) * ICLR_LORA,
                     2 * c + dd * c:2 * c + (dd + 1) * c].set(rw_a_up[dd])
    gup = jnp.zeros((LANES, c), F32).at[:GATE_LORA].set(rw_g_up)
    row = lambda a: a.reshape(1, -1)

    nb8 = seq // HALO
    tb = tt // HALO
    const = lambda a: pl.BlockSpec(a.shape, lambda b, i: (0,) * a.ndim)
    tile = lambda w: pl.BlockSpec((None, tt, w), lambda b, i: (b, i, 0))
    ins = [
        (x, pl.BlockSpec((None, HALO, d), lambda b, i: (b, jnp.maximum(i * tb - 1, 0), 0))),
        (x, pl.BlockSpec((None, tt, d), lambda b, i: (b, i, 0))),
        (x, pl.BlockSpec((None, HALO, d), lambda b, i: (b, jnp.minimum((i + 1) * tb, nb8 - 1), 0))),
        (mod, pl.BlockSpec((None,) + mod.shape[1:], lambda b, i: (b, 0, 0))),
    ]
    consts = [row(norm1_g), w_hy, w_rkv, w_lora, hy_conv_w, row(hy_conv_b), mu_rkv, mu_lora,
              row(rw_w0), row(rw_a0), *_split2(wwa), gup, row(rw_k_k), row(rw_k_a), row(rw_r_k), _head_ones()]
    ins += [(a, const(a)) for a in consts]
    widths = [hy, 4 * c, 4 * c, c, c]
    return pl.pallas_call(
        functools.partial(_proj_kernel, tt=tt, nt=nt),
        grid=(bsz, nt),
        in_specs=[s for _, s in ins],
        out_specs=[tile(w) for w in widths],
        out_shape=[jax.ShapeDtypeStruct((bsz, seq, w), F32) for w in widths],
        compiler_params=_params("arbitrary", "arbitrary"),
        name="input_projection",
    )(*[a for a, _ in ins])


CHUNK = HEAD
GROUP = MXU_DIM // HEAD


def _nt(a, b):
    return lax.dot_general(a, b, (((1,), (1,)), ((), ())), preferred_element_type=F32)


def _tn(a, b):
    return lax.dot_general(a, b, (((0,), (0,)), ((), ())), preferred_element_type=F32)


def _wkv_direction(r, k, v, kk, lw, a, ka, s_ref, reverse):
    c = CHUNK
    ti = lax.broadcasted_iota(jnp.int32, (c, c), 0)
    si = lax.broadcasted_iota(jnp.int32, (c, c), 1)
    tri = (si >= ti) if reverse else (si <= ti)
    cum = _dot_exact_rhs_lhs(jnp.where(tri, 1.0, 0.0).astype(BF16), lw)
    tot = jnp.sum(lw, axis=0, keepdims=True)
    w_incl = jnp.exp(cum)
    w_prev = jnp.exp(cum - lw)
    w_inv = jnp.exp(-cum)
    w_end = jnp.exp(tot - cum)
    w_tot = jnp.exp(tot)
    kd = k * (1.0 + (a - 1.0) * ka)
    b = kk * a
    a_w = -kk * w_prev
    r_w = r * w_incl
    b_w = b * w_inv
    k_w = kd * w_inv
    b_e = b * w_end
    k_e = kd * w_end

    m = MXU_DIM
    ri = lax.broadcasted_iota(jnp.int32, (m, m), 0)
    ci = lax.broadcasted_iota(jnp.int32, (m, m), 1)
    head_mask = (ri // HEAD) == (ci // HEAD)
    tl = lax.broadcasted_iota(jnp.int32, (c, m), 0)
    sl = lax.broadcasted_iota(jnp.int32, (c, m), 1) % c
    strict = (sl > tl) if reverse else (sl < tl)
    incl = (sl >= tl) if reverse else (sl <= tl)
    eye = jnp.where(sl == tl, 1.0, 0.0)
    both = lambda top, bot: jnp.concatenate([top, bot], axis=0)

    def stack(xg):
        xb = xg.astype(BF16)
        return jnp.where(head_mask, jnp.concatenate([xb] * GROUP, axis=0), jnp.zeros((), BF16))

    streams = []
    for g in range(D_RWKV // m):
        sl_g = slice(g * m, (g + 1) * m)
        streams.append(dict(
            ar=both(a_w[:, sl_g], r_w[:, sl_g]).astype(BF16),
            b_st=stack(b_w[:, sl_g]), k_st=stack(k_w[:, sl_g]), v_st=stack(v[:, sl_g]),
            v=v[:, sl_g], bk=both(b_e[:, sl_g], k_e[:, sl_g]).astype(BF16),
            w_tot=w_tot[:, sl_g], s_ref=s_ref.at[g],
            strict=strict, incl=incl, eye=eye, head_mask=head_mask, stack=stack))
    return streams


def _wkv_streams_step(streams):
    c = CHUNK
    both = lambda top, bot: jnp.concatenate([top, bot], axis=0)
    for st in streams:
        st["s"] = st["s_ref"][...]
        st["xb"] = _nt(st["ar"], st["b_st"])
        st["xk"] = _nt(st["ar"], st["k_st"])
        st["xs"] = _nt(st["ar"], st["s"].astype(BF16))
    for st in streams:
        m_ak = jnp.where(st["strict"], st["xk"][:c], 0.0)
        st["rhs"] = st["xs"][:c] + _dot(m_ak.astype(BF16), st["v_st"])
        st["pw"] = jnp.where(st["strict"], st["xb"][:c], 0.0)
        st["t"] = st["eye"] + st["pw"]
        st["p_st"] = st["stack"](st["pw"])
    levels = int(math.log2(c)) - 1
    for st in streams:
        st["pw"] = _dot(st["pw"].astype(BF16), st["p_st"])
        st["p_st"] = st["stack"](st["pw"])
    for lvl in range(1, levels + 1):
        for st in streams:
            if lvl < levels:
                prod = _dot(both(st["pw"], st["t"]).astype(BF16), st["p_st"])
                st["pw"] = prod[:c]
                st["t"] = st["t"] + prod[c:]
                st["p_st"] = st["stack"](st["pw"])
            else:
                st["t"] = st["t"] + _dot(st["t"].astype(BF16), st["p_st"])
    for st in streams:
        st["u"] = _dot(st["t"].astype(BF16), st["stack"](st["rhs"]))
    outs = []
    for st in streams:
        m_rb = jnp.where(st["incl"], st["xb"][c:], 0.0)
        m_rk = jnp.where(st["incl"], st["xk"][c:], 0.0)
        outs.append(st["xs"][c:] + _dot(m_rb.astype(BF16), st["stack"](st["u"]))
                    + _dot(m_rk.astype(BF16), st["v_st"]))
        uv = both(st["u"], st["v"]).astype(BF16)
        st["s_ref"][...] = st["s"] * st["w_tot"] + jnp.where(st["head_mask"], _tn(uv, st["bk"]), 0.0)
    return outs


def _dot_exact_rhs_lhs(tri_bf16, x):
    xh, xl = _split2(x)
    return _dot(tri_bf16, xh) + _dot(tri_bf16, xl)


def _wkv_kernel(rkvk_f, lwa_f, rkvk_b, lwa_b, ka_ref, of_ref, ob_ref, s_ref, *, nch):
    @pl.when(pl.program_id(1) == 0)
    def _():
        s_ref[...] = jnp.zeros_like(s_ref)

    ka = ka_ref[...]
    c = D_RWKV

    def operands(rkvk_ref, lwa_ref, rows):
        x = rkvk_ref[rows, :]
        la = lwa_ref[rows, :]
        return x[:, :c], x[:, c:2 * c], x[:, 2 * c:3 * c], x[:, 3 * c:], la[:, :c], la[:, c:]

    for ci in range(nch):
        rows_f = slice(ci * CHUNK, (ci + 1) * CHUNK)
        rows_b = slice((nch - 1 - ci) * CHUNK, (nch - ci) * CHUNK)
        fwd = _wkv_direction(*operands(rkvk_f, lwa_f, rows_f), ka, s_ref.at[0], False)
        bwd = _wkv_direction(*operands(rkvk_b, lwa_b, rows_b), ka, s_ref.at[1], True)
        outs = _wkv_streams_step(fwd + bwd)
        of_ref[rows_f, :] = jnp.concatenate(outs[:len(fwd)], axis=1)
        ob_ref[rows_b, :] = jnp.concatenate(outs[len(fwd):], axis=1)


WKV_CHUNKS_PER_STEP = 2


def _wkv(rkvk, lwa, rw_k_a):
    bsz, seq, _ = rkvk.shape
    c = D_RWKV
    nch = WKV_CHUNKS_PER_STEP if seq % (WKV_CHUNKS_PER_STEP * CHUNK) == 0 else 1
    rows = nch * CHUNK
    nb = seq // rows
    fwd = lambda w, lane_blk: pl.BlockSpec((None, rows, w), lambda b, j: (b, j, lane_blk))
    bwd = lambda w, lane_blk: pl.BlockSpec((None, rows, w), lambda b, j: (b, nb - 1 - j, lane_blk))
    return pl.pallas_call(
        functools.partial(_wkv_kernel, nch=nch),
        grid=(bsz, nb),
        in_specs=[fwd(4 * c, 0), fwd(2 * c, 0), bwd(4 * c, 0), bwd(2 * c, 1),
                  pl.BlockSpec((1, c), lambda b, j: (0, 0))],
        out_specs=[fwd(c, 0), bwd(c, 0)],
        out_shape=[jax.ShapeDtypeStruct((bsz, seq, c), F32)] * 2,
        scratch_shapes=[pltpu.VMEM((2, c // MXU_DIM, MXU_DIM, MXU_DIM), F32)],
        compiler_params=_params("arbitrary", "arbitrary"),
        name="wkv7_chunked",
    )(rkvk, lwa, rkvk, lwa, rw_k_a.reshape(1, c))


NEG_INF = float("-inf")


def _first_max(vals, idx, size):
    m = jnp.max(vals, axis=0, keepdims=True)
    i = jnp.min(jnp.where(vals == m, idx, size), axis=0, keepdims=True)
    return m, i


def _route(scores, biased):
    e, tt = scores.shape
    per = e // N_GROUPS
    rowl = lax.broadcasted_iota(jnp.int32, (per, tt), 0)
    gs = []
    for g in range(N_GROUPS):
        blk = biased[g * per:(g + 1) * per]
        m1, i1 = _first_max(blk, rowl, per)
        m2 = jnp.max(jnp.where(rowl == i1, NEG_INF, blk), axis=0, keepdims=True)
        gs.append(m1 + m2)
    cur = jnp.concatenate(gs, axis=0)
    growl = lax.broadcasted_iota(jnp.int32, (N_GROUPS, tt), 0)
    gsel = jnp.zeros((N_GROUPS, tt), F32)
    for _ in range(TOPK_GROUPS):
        _, ig = _first_max(cur, growl, N_GROUPS)
        hit = growl == ig
        gsel = jnp.where(hit, 1.0, gsel)
        cur = jnp.where(hit, NEG_INF, cur)
    emask = jnp.concatenate([jnp.broadcast_to(gsel[g:g + 1], (per, tt)) for g in range(N_GROUPS)], axis=0)
    masked = jnp.where(emask > 0.5, biased, NEG_INF)
    row = lax.broadcasted_iota(jnp.int32, (e, tt), 0)
    ids, ws = [], []
    for _ in range(TOP_K):
        _, ie = _first_max(masked, row, e)
        hit = row == ie
        ids.append(ie)
        ws.append(jnp.sum(jnp.where(hit, scores, 0.0), axis=0, keepdims=True))
        masked = jnp.where(hit, NEG_INF, masked)
    w = jnp.concatenate(ws, axis=0)
    w = w / jnp.sum(w, axis=0, keepdims=True) * ROUTE_SCALE
    return jnp.concatenate(ids, axis=0), w


def _mixout_kernel(x_ref, mod_ref, yhy_ref, of_ref, ob_ref, g_ref, bonus_ref, lnw_ref, lnb_ref,
                   ones_ref, wout_ref, g2n_ref, rwth_ref, rwtl_ref, bias_ref,
                   x1_ref, h2a_ref, h2b_ref, eid_ref, wsel_ref):
    ones = ones_ref[...]
    s = of_ref[...] + ob_ref[...]
    mean = _dot_exact_rhs(s, ones) * (1.0 / HEAD)
    dlt = s - mean
    var = _dot_exact_rhs(dlt * dlt, ones) * (1.0 / HEAD)
    sn = dlt * lax.rsqrt(var + GN_EPS) * lnw_ref[...] + lnb_ref[...]
    yrw = (sn + bonus_ref[...]) * g_ref[...]
    ch = yhy_ref.shape[-1]
    mix = _dot(yhy_ref[...].astype(BF16), wout_ref[:ch, :]) + _dot(yrw.astype(BF16), wout_ref[ch:, :])
    x1 = x_ref[...] + mod_ref[2:3, :] * mix
    x1_ref[...] = x1
    ms = jnp.mean(x1 * x1, axis=-1, keepdims=True)
    h2 = x1 * lax.rsqrt(ms + NORM_EPS) * g2n_ref[...]
    h2 = h2 * (1.0 + mod_ref[4:5, :]) + mod_ref[3:4, :]
    h2a_ref[...], h2b_ref[...] = _pack_rows(h2)
    rh, rl = rwth_ref[...], rwtl_ref[...]
    hh, hl = _split2(h2)
    logits = _nt(rh, hh) + (_nt(rh, hl) + _nt(rl, hh))
    scores = jax.nn.sigmoid(logits)
    ids, w = _route(scores, scores + bias_ref[...])
    eid_ref[...] = ids
    wsel_ref[...] = w


def _mix_out(x, mod, yhy, o_f, o_b, g, bonus, ln_w, ln_b, w_out, norm2_g, router_w, router_bias, tt=256):
    bsz, seq, d = x.shape
    tt = min(tt, seq)
    nt = seq // tt
    n = bsz * seq
    c = D_RWKV
    e = router_w.shape[1]
    row = lambda a: a.reshape(1, -1)
    consts = [row(ln_w), row(ln_b), _head_ones(), w_out.astype(BF16), row(norm2_g), *_split2(router_w.T),
              jnp.broadcast_to(router_bias.reshape(e, 1), (e, tt))]
    const = lambda a: pl.BlockSpec(a.shape, lambda b, i: (0,) * a.ndim)
    tile = lambda w: pl.BlockSpec((None, tt, w), lambda b, i: (b, i, 0))
    flat = lambda rows, dt: jax.ShapeDtypeStruct((rows, n), dt)
    return pl.pallas_call(
        _mixout_kernel,
        grid=(bsz, nt),
        in_specs=[tile(d), pl.BlockSpec((None,) + mod.shape[1:], lambda b, i: (b, 0, 0))]
        + [tile(c)] * 5 + [const(a) for a in consts],
        out_specs=[tile(d), pl.BlockSpec((tt, d // 4), lambda b, i: (b * nt + i, 0)),
                   pl.BlockSpec((tt, d // 4), lambda b, i: (b * nt + i, 0)),
                   pl.BlockSpec((TOP_K, tt), lambda b, i: (0, b * nt + i)),
                   pl.BlockSpec((TOP_K, tt), lambda b, i: (0, b * nt + i))],
        out_shape=[jax.ShapeDtypeStruct((bsz, seq, d), F32), jax.ShapeDtypeStruct((n, d // 4), U32),
                   jax.ShapeDtypeStruct((n, d // 4), U32),
                   flat(TOP_K, jnp.int32), flat(TOP_K, F32)],
        compiler_params=_params("arbitrary", "arbitrary"),
        name="mix_out_router",
    )(x, mod, yhy, o_f, o_b, g, bonus, *consts)


BLK = 512
BLK_SHIFT = 9


def _multi_hot(eid, e):
    row = lax.broadcasted_iota(jnp.int32, (e, eid.shape[1]), 0)
    mh = jnp.zeros((e, eid.shape[1]), F32)
    for kk in range(TOP_K):
        mh = mh + jnp.where(row == eid[kk:kk + 1, :], 1.0, 0.0)
    return row, mh


def _lookup(row, eid, table):
    return jnp.concatenate(
        [jnp.sum(jnp.where(row == eid[kk:kk + 1, :], table, 0.0), axis=0, keepdims=True)
         for kk in range(TOP_K)], axis=0)


def _rank_kernel(eid_ref, rank_ref, cnt_ref, *, e):
    @pl.when(pl.program_id(0) == 0)
    def _():
        cnt_ref[...] = jnp.zeros_like(cnt_ref)

    eid = eid_ref[...]
    tt = eid.shape[1]
    row, mh = _multi_hot(eid, e)
    mhb = mh.astype(BF16)
    si = lax.broadcasted_iota(jnp.int32, (tt, tt), 0)
    ti = lax.broadcasted_iota(jnp.int32, (tt, tt), 1)
    earlier = _dot(mhb, jnp.where(si < ti, 1.0, 0.0).astype(BF16))
    cnt = cnt_ref[...]
    full = earlier + jnp.concatenate([cnt] * (tt // LANES), axis=1)
    rank_ref[...] = _lookup(row, eid, full).astype(jnp.int32)
    cnt_ref[...] = cnt + _dot(mhb, jnp.ones((tt, LANES), BF16))


def _expert_ranks(eid, e, tt=512):
    n = eid.shape[1]
    tt = min(tt, n)
    return pl.pallas_call(
        functools.partial(_rank_kernel, e=e),
        grid=(n // tt,),
        in_specs=[pl.BlockSpec((TOP_K, tt), lambda i: (0, i))],
        out_specs=[pl.BlockSpec((TOP_K, tt), lambda i: (0, i)),
                   pl.BlockSpec((e, LANES), lambda i: (0, 0))],
        out_shape=[jax.ShapeDtypeStruct((TOP_K, n), jnp.int32), jax.ShapeDtypeStruct((e, LANES), F32)],
        compiler_params=_params("arbitrary"),
        name="expert_ranks",
    )(eid)


def _block_offsets(cnt):
    e = cnt.shape[0]
    nblk = ((cnt.astype(jnp.int32) + (BLK - 1)) >> BLK_SHIFT).astype(F32)
    ri = lax.broadcasted_iota(jnp.int32, (e, e), 0)
    ci = lax.broadcasted_iota(jnp.int32, (e, e), 1)
    tril = jnp.where(ci <= ri, 1.0, 0.0).astype(BF16)
    nh, nl = _split2(nblk)
    return nblk, _dot(tril, nh) + _dot(tril, nl)


def _dest_kernel(cnt_ref, eid_ref, rank_ref, dest_ref):
    nblk, end = _block_offsets(cnt_ref[...])
    off = (end - nblk) * float(BLK)
    eid = eid_ref[...]
    tt = eid.shape[1]
    row = lax.broadcasted_iota(jnp.int32, (off.shape[0], tt), 0)
    table = jnp.concatenate([off] * (tt // LANES), axis=1)
    dest_ref[...] = _lookup(row, eid, table).astype(jnp.int32) + rank_ref[...]


def _destinations(cnt, eid, rank, tt=512):
    n = eid.shape[1]
    tt = min(tt, n)
    blk = pl.BlockSpec((TOP_K, tt), lambda i: (0, i))
    return pl.pallas_call(
        _dest_kernel,
        grid=(n // tt,),
        in_specs=[pl.BlockSpec(cnt.shape, lambda i: (0, 0)), blk, blk],
        out_specs=blk,
        out_shape=jax.ShapeDtypeStruct((TOP_K, n), jnp.int32),
        compiler_params=_params("arbitrary"),
        name="expert_destinations",
    )(cnt, eid, rank)


def _meta_kernel(cnt_ref, meta_ref, *, nbp):
    cnt = cnt_ref[...]
    e = cnt.shape[0]
    nblk, end = _block_offsets(cnt)
    rep = lambda a, w: jnp.concatenate([a] * (w // LANES), axis=1)
    b = lax.broadcasted_iota(jnp.int32, (e, nbp), 1).astype(F32)
    blk_e = jnp.minimum(jnp.sum(jnp.where(rep(end, nbp) <= b, 1.0, 0.0), axis=0, keepdims=True), float(e - 1))
    row = lax.broadcasted_iota(jnp.int32, (e, nbp), 0).astype(F32)
    mine = row == blk_e
    left = rep(cnt + (end - nblk) * float(BLK), nbp) - b * float(BLK)
    nvalid = jnp.clip(jnp.sum(jnp.where(mine, left, 0.0), axis=0, keepdims=True), 0.0, float(BLK))
    nused = jnp.max(rep(end, nbp), axis=0, keepdims=True)
    later = jnp.logical_and(row > blk_e, rep(nblk, nbp) > 0.0)
    nxt = jnp.min(jnp.where(later, row, float(e)), axis=0, keepdims=True)
    nxt = jnp.where(nxt >= float(e), -1.0, nxt)
    meta_ref[...] = jnp.concatenate([blk_e, nvalid, nused, nxt, jnp.zeros((4, nbp), F32)],
                                    axis=0).astype(jnp.int32)


def _block_meta(cnt, nb):
    nbp = -(-nb // LANES) * LANES
    return pl.pallas_call(
        functools.partial(_meta_kernel, nbp=nbp),
        out_shape=jax.ShapeDtypeStruct((8, nbp), jnp.int32),
        compiler_params=pltpu.CompilerParams(vmem_limit_bytes=VMEM_LIMIT),
        name="expert_block_meta",
    )(cnt)


SC_WINDOW = 128


def _sc_mesh():
    return plsc.VectorSubcoreMesh(core_axis_name="core", subcore_axis_name="subcore")


def _sc_scatter_rows(rows, idx, nrows):
    n, width = rows.shape

    @pl.kernel(out_type=jax.ShapeDtypeStruct((nrows, width), rows.dtype), mesh=_sc_mesh())
    def scatter(rows_hbm, idx_hbm, out_hbm):
        def body(rows_vmem, idx_vmem):
            pltpu.sync_copy(rows_vmem, out_hbm.at[idx_vmem.at[0]])

        pltpu.emit_pipeline(
            body,
            grid=(n // SC_WINDOW, idx.shape[0]),
            in_specs=[pl.BlockSpec((SC_WINDOW, width), index_map=lambda i, k: (i, 0)),
                      pl.BlockSpec((1, SC_WINDOW), index_map=lambda i, k: (k, i))],
            out_specs=[],
            core_axis_name=("core", "subcore"),
            dimension_semantics=(pltpu.PARALLEL, pltpu.ARBITRARY),
        )(rows_hbm, idx_hbm)

    return scatter(rows, idx)


def _sc_gather_rows(src, idx):
    num = idx.shape[1]
    width = src.shape[1]

    @pl.kernel(out_type=jax.ShapeDtypeStruct((num, width), src.dtype), mesh=_sc_mesh())
    def gather(src_hbm, idx_hbm, out_hbm):
        def body(idx_vmem, out_vmem):
            pltpu.sync_copy(src_hbm.at[idx_vmem.at[0]], out_vmem)

        pltpu.emit_pipeline(
            body,
            grid=(num // SC_WINDOW,),
            in_specs=[pl.BlockSpec((1, SC_WINDOW), index_map=lambda i: (0, i))],
            out_specs=[pl.BlockSpec((SC_WINDOW, width), index_map=lambda i: (i, 0))],
            core_axis_name=("core", "subcore"),
            dimension_semantics=(pltpu.PARALLEL,),
        )(idx_hbm, out_hbm)

    return gather(src, idx)


def _experts_kernel(be_ref, nv_ref, nu_ref, nxt_ref, xa_ref, xb_ref, wg_hbm, wu_hbm, wd_hbm, oa_ref, ob_ref,
                    wgf, wuf, wdf, wgb, wub, wdb, sems, slot_ref):
    b = pl.program_id(0)
    used = b < nu_ref[0]

    def fetch(expert, slot):
        return [pltpu.make_async_copy(src.at[expert], dst.at[slot], sems.at[slot])
                for src, dst in ((wg_hbm, wgf), (wu_hbm, wuf), (wd_hbm, wdf))]

    @pl.when(b == 0)
    def _():
        slot_ref[0] = 0
        for cp in fetch(be_ref[0], 0):
            cp.start()

    @pl.when(used)
    def _():
        prev = be_ref[jnp.maximum(b - 1, 0)]

        @pl.when(jnp.logical_or(b == 0, be_ref[b] != prev))
        def _():
            slot = slot_ref[0]
            for cp in fetch(be_ref[b], slot):
                cp.wait()

            @pl.when(nxt_ref[b] >= 0)
            def _():
                for cp in fetch(nxt_ref[b], 1 - slot):
                    cp.start()

            wgb[...] = wgf[slot].astype(BF16)
            wub[...] = wuf[slot].astype(BF16)
            wdb[...] = wdf[slot].astype(BF16)
            slot_ref[0] = 1 - slot

        valid = lax.broadcasted_iota(jnp.int32, (xa_ref.shape[0], 1), 0) < nv_ref[b]
        zero = jnp.zeros((), U32)
        x = _unpack_rows(jnp.where(valid, xa_ref[...], zero), jnp.where(valid, xb_ref[...], zero))
        act = _silu(_dot(x, wgb[...])) * _dot(x, wub[...])
        oa_ref[...], ob_ref[...] = _pack_rows(_dot(act.astype(BF16), wdb[...]))


def _experts(blk_e, nvalid, nused, nxt_e, xs_a, xs_b, wg, wu, wd):
    p, dq = xs_a.shape
    nb = p // BLK
    d, de = wg.shape[1], wg.shape[2]
    rows_in = pl.BlockSpec((BLK, dq), lambda b, be, nv, nu, nx: (jnp.minimum(b, nu[0] - 1), 0))
    hbm = pl.BlockSpec(memory_space=pl.ANY)
    return pl.pallas_call(
        _experts_kernel,
        grid_spec=pltpu.PrefetchScalarGridSpec(
            num_scalar_prefetch=4,
            grid=(nb,),
            in_specs=[rows_in, rows_in, hbm, hbm, hbm],
            out_specs=[rows_in, rows_in],
            scratch_shapes=[pltpu.VMEM((2, d, de), F32), pltpu.VMEM((2, d, de), F32), pltpu.VMEM((2, de, d), F32),
                            pltpu.VMEM((d, de), BF16), pltpu.VMEM((d, de), BF16), pltpu.VMEM((de, d), BF16),
                            pltpu.SemaphoreType.DMA((2,)), pltpu.SMEM((1,), jnp.int32)],
        ),
        out_shape=[jax.ShapeDtypeStruct((p, dq), U32)] * 2,
        compiler_params=_params("arbitrary"),
        name="moe_experts",
    )(blk_e, nvalid, nused, nxt_e, xs_a, xs_b, wg, wu, wd)


def _shared_kernel(ha_ref, hb_ref, sg_ref, su_ref, sd_ref, o_ref):
    hb = _unpack_rows(ha_ref[...], hb_ref[...])
    act = _silu(_dot(hb, sg_ref[...])) * _dot(hb, su_ref[...])
    o_ref[...] = _dot(act.astype(BF16), sd_ref[...]).astype(o_ref.dtype)


def _shared_expert(h2a, h2b, sh_wg, sh_wu, sh_wd, tt=512):
    n, dp = h2a.shape
    d = sh_wg.shape[0]
    tt = min(tt, n)
    consts = [sh_wg.astype(BF16), sh_wu.astype(BF16), sh_wd.astype(BF16)]
    packed_rows = pl.BlockSpec((tt, dp), lambda i: (i, 0))
    return pl.pallas_call(
        _shared_kernel,
        grid=(n // tt,),
        in_specs=[packed_rows, packed_rows] + [pl.BlockSpec(a.shape, lambda i: (0, 0)) for a in consts],
        out_specs=pl.BlockSpec((tt, d), lambda i: (i, 0)),
        out_shape=jax.ShapeDtypeStruct((n, d), BF16),
        compiler_params=_params("arbitrary"),
        name="shared_expert",
    )(h2a, h2b, *consts)


def _combine_kernel(w_ref, x1_ref, sh_ref, mod_ref, ga_ref, gb_ref, gf_ref, sel_ref, o_ref):
    ffn = sh_ref[...].astype(F32)
    wh, wl = _split2(w_ref[...])
    acc = None
    for kk in range(TOP_K):
        sel = sel_ref[kk]
        wk = _tn(wh, sel) + _tn(wl, sel)
        a_lo, a_hi = _unpack_halves(ga_ref[kk])
        b_lo, b_hi = _unpack_halves(gb_ref[kk])
        parts = [a_lo * wk, b_lo * wk, a_hi * wk, b_hi * wk]
        acc = parts if acc is None else [p + q for p, q in zip(acc, parts)]
    ffn = ffn + jnp.concatenate(acc, axis=1)
    xo = x1_ref[...] + mod_ref[5:6, :] * ffn
    ms = jnp.mean(xo * xo, axis=-1, keepdims=True)
    o_ref[...] = xo * lax.rsqrt(ms + NORM_EPS) * gf_ref[...]


def _combine(wsel, x1, shared, mod, ga, gb, normf_g, tt=256):
    bsz, seq, d = x1.shape
    n = bsz * seq
    tt = min(tt, seq)
    per = seq // tt
    dq = ga.shape[2]
    sel = jnp.asarray(np.broadcast_to(np.eye(TOP_K)[:, :, None], (TOP_K, TOP_K, dq)), BF16)
    consts = [normf_g.reshape(1, d), sel]
    const = lambda a: pl.BlockSpec(a.shape, lambda i: (0,) * a.ndim)
    rows = pl.BlockSpec((tt, d), lambda i: (i, 0))
    gathered = pl.BlockSpec((TOP_K, tt, dq), lambda i: (0, i, 0))
    return pl.pallas_call(
        _combine_kernel,
        grid=(n // tt,),
        in_specs=[pl.BlockSpec((TOP_K, tt), lambda i: (0, i)),
                  rows, rows,
                  pl.BlockSpec((None,) + mod.shape[1:], lambda i: (i // per, 0, 0)),
                  gathered, gathered] + [const(a) for a in consts],
        out_specs=rows,
        out_shape=jax.ShapeDtypeStruct((n, d), F32),
        compiler_params=_params("arbitrary"),
        name="moe_combine",
    )(wsel, x1.reshape(n, d), shared, mod, ga, gb, *consts)


def _moe(x1, h2a, h2b, mod, eid, wsel, exp_wg, exp_wu, exp_wd, sh_wg, sh_wu, sh_wd, normf_g):
    n = h2a.shape[0]
    e = exp_wg.shape[0]
    nb = (n * TOP_K + e * (BLK - 1)) // BLK
    rank, cnt = _expert_ranks(eid, e)
    dest = _destinations(cnt, eid, rank)
    meta = _block_meta(cnt, nb)
    xs_a = _sc_scatter_rows(h2a, dest, nb * BLK)
    xs_b = _sc_scatter_rows(h2b, dest, nb * BLK)
    shared = _shared_expert(h2a, h2b, sh_wg, sh_wu, sh_wd)
    ys_a, ys_b = _experts(meta[0, :nb], meta[1, :nb], meta[2, :1], meta[3, :nb], xs_a, xs_b,
                          exp_wg, exp_wu, exp_wd)
    idx = dest.reshape(1, TOP_K * n)
    ga = _sc_gather_rows(ys_a, idx).reshape(TOP_K, n, -1)
    gb = _sc_gather_rows(ys_b, idx).reshape(TOP_K, n, -1)
    return _combine(wsel, x1, shared, mod, ga, gb, normf_g)


def kernel(x, c, norm1_g, norm2_g, normf_g, w_ada, b_ada, w_in, w_out, hy_conv_w, hy_conv_b, hy_pos_w1, hy_pos_b1, hy_pos_w2, hy_pos_b2, hy_pos_w3, hy_sin_freq, hy_skip, rw_mu, rw_w0, rw_w_up, rw_a0, rw_a_up, rw_g_up, rw_k_k, rw_k_a, rw_r_k, rw_ln_w, rw_ln_b, router_w, router_bias, exp_w_gate, exp_w_up, exp_w_down, sh_w_gate, sh_w_up, sh_w_down):
    bsz, seq, d = x.shape
    depth = w_ada.shape[0]
    assert depth == 1, "the final norm is fused into the last kernel of a single layer"
    for l in range(depth):
        mod = _modulation(c, w_ada[l], b_ada[l]).reshape(bsz, -1, d)
        uhy, rkvk, lwa, g, bonus = _projection(
            x, mod, norm1_g[l], w_in[l], hy_conv_w[l], hy_conv_b[l], rw_mu[l], rw_w0[l], rw_w_up[l],
            rw_a0[l], rw_a_up[l], rw_g_up[l], rw_k_k[l], rw_k_a[l], rw_r_k[l])
        k2, ss = _hyena_filters(seq, hy_pos_w1[l], hy_pos_b1[l], hy_pos_w2[l], hy_pos_b2[l],
                                hy_pos_w3[l], hy_sin_freq[l])
        khat = _filter_spectrum(k2, ss, seq)
        z, z_col = uhy, 0
        for order in range(HYENA_ORDER):
            z = _long_conv_gate(z, z_col, uhy, (order + 1) * D_HYENA, khat, hy_skip[l], order)
            z_col = 0
        o_f, o_b = _wkv(rkvk, lwa, rw_k_a[l])
        x1, h2a, h2b, eid, wsel = _mix_out(x, mod, z, o_f, o_b, g, bonus, rw_ln_w[l], rw_ln_b[l], w_out[l],
                                           norm2_g[l], router_w[l], router_bias[l])
        x = _moe(x1, h2a, h2b, mod, eid, wsel, exp_w_gate[l], exp_w_up[l], exp_w_down[l],
                 sh_w_gate[l], sh_w_up[l], sh_w_down[l], normf_g)
        x = x.reshape(bsz, seq, d)
    return x
```

```python
import functools
import math

import jax
import jax.numpy as jnp
import numpy as np
from jax import lax
from jax.experimental import pallas as pl
from jax.experimental.pallas import tpu as pltpu
from jax.experimental.pallas import tpu_sc as plsc

F32 = jnp.float32
BF16 = jnp.bfloat16

LANES = 128
MXU_DIM = 256
VMEM_LIMIT = 56 * 1024 * 1024

D_HYENA = 512
D_RWKV = 512
HEAD = 64
N_HEADS = D_RWKV // HEAD
HYENA_ORDER = 2
FILTER_BANDS = 16
DECAY_TARGET = 1e-2
FAST_DECAY_PCT = 0.3
SLOW_DECAY_PCT = 1.5
FILTER_NORM_EPS = 1e-6
DECAY_LORA = 32
ICLR_LORA = 32
GATE_LORA = 96
GN_EPS = 64e-5
NORM_EPS = 1e-6
N_EXPERTS = 256
TOP_K = 8
N_GROUPS = 8
TOPK_GROUPS = 4
ROUTE_SCALE = 2.5
D_EXPERT = 256


def _params(*sem):
    return pltpu.CompilerParams(dimension_semantics=sem, vmem_limit_bytes=VMEM_LIMIT)


def _split2(a):
    hi = a.astype(BF16)
    lo = (a - hi.astype(F32)).astype(BF16)
    return hi, lo


def _dot(a, b):
    return jnp.dot(a, b, preferred_element_type=F32)


def _dot3(a, b):
    ah, al = _split2(a)
    bh, bl = _split2(b)
    return _dot(ah, bh) + (_dot(ah, bl) + _dot(al, bh))


def _dot_exact_rhs(a, b_bf16):
    ah, al = _split2(a)
    return _dot(ah, b_bf16) + _dot(al, b_bf16)


def _silu(x):
    return x * jax.nn.sigmoid(x)


U32 = jnp.int32


def _pack_halves(x):
    w = x.shape[1] // 2
    return pltpu.pack_elementwise([x[:, :w], x[:, w:]], packed_dtype=BF16)


def _unpack_halves(p):
    lo = pltpu.unpack_elementwise(p, index=0, packed_dtype=BF16, unpacked_dtype=F32)
    hi = pltpu.unpack_elementwise(p, index=1, packed_dtype=BF16, unpacked_dtype=F32)
    return lo, hi


def _pack_rows(x):
    packed = _pack_halves(x)
    half = packed.shape[1] // 2
    return packed[:, :half], packed[:, half:]


def _unpack_rows(a, b):
    a_lo, a_hi = _unpack_halves(a)
    b_lo, b_hi = _unpack_halves(b)
    return jnp.concatenate([a_lo.astype(BF16), b_lo.astype(BF16), a_hi.astype(BF16), b_hi.astype(BF16)], axis=1)


def _mod_kernel(c_ref, w_ref, b_ref, o_ref):
    o_ref[...] = _dot3(_silu(c_ref[...]), w_ref[...]) + b_ref[...]


def _modulation(c, w_ada, b_ada):
    bsz, d = c.shape
    n = w_ada.shape[1]
    blk = 1024
    return pl.pallas_call(
        _mod_kernel,
        grid=(n // blk,),
        in_specs=[
            pl.BlockSpec((bsz, d), lambda j: (0, 0)),
            pl.BlockSpec((d, blk), lambda j: (0, j)),
            pl.BlockSpec((1, blk), lambda j: (0, j)),
        ],
        out_specs=pl.BlockSpec((bsz, blk), lambda j: (0, j)),
        out_shape=jax.ShapeDtypeStruct((bsz, n), F32),
        compiler_params=_params("arbitrary"),
        name="adaln_mod",
    )(c, w_ada, b_ada.reshape(1, n))


def _filter_kernel(band_ref, w1_ref, b1_ref, w2_ref, b2_ref, w3_ref, freq_ref, delta_ref,
                   k_ref, ss_ref, *, seq, rows):
    half = pl.program_id(0)
    i = pl.program_id(1)
    r = lax.broadcasted_iota(jnp.int32, (rows, LANES), 0) + i * rows
    pos = jnp.where(half == 0, r, seq - r).astype(F32)
    tt = pos / float(max(seq - 1, 1))
    lane = lax.broadcasted_iota(jnp.int32, (rows, LANES), 1)
    ang = pos * band_ref[...]
    feats = jnp.where(lane == 0, tt,
                      jnp.where(lane <= FILTER_BANDS, jnp.cos(ang),
                                jnp.where(lane <= 2 * FILTER_BANDS, -jnp.sin(ang), 0.0)))
    freq = freq_ref[...]
    hdn = jnp.sin(freq * (_dot3(feats, w1_ref[...]) + b1_ref[...]))
    for j in range(w2_ref.shape[0]):
        hdn = jnp.sin(freq * (_dot3(hdn, w2_ref[j]) + b2_ref[j]))
    filt = _dot3(hdn, w3_ref[...])
    filt = filt * jnp.exp(-tt[:, :1] * delta_ref[...])
    valid = jnp.logical_or(half == 0, r[:, :1] > 0)
    filt = jnp.where(valid, filt, 0.0)
    k_ref[...] = filt

    @pl.when(jnp.logical_and(half == 0, i == 0))
    def _():
        ss_ref[...] = jnp.zeros_like(ss_ref)

    ss_ref[...] += jnp.broadcast_to(jnp.sum(filt * filt, axis=0, keepdims=True), ss_ref.shape)


def _hyena_filters(seq, pw1, pb1, pw2, pb2, pw3, freq):
    width = pw1.shape[1]
    ncol = HYENA_ORDER * D_HYENA
    rows = min(seq, 512)
    bands = np.zeros((1, LANES), np.float64)
    lin = np.linspace(1e-4, FILTER_BANDS - 1, FILTER_BANDS)
    bands[0, 1:1 + FILTER_BANDS] = lin
    bands[0, 1 + FILTER_BANDS:1 + 2 * FILTER_BANDS] = lin
    bands = jnp.asarray(bands * (2.0 * math.pi / seq), F32)
    deltas = np.abs(np.linspace(math.log(DECAY_TARGET) / SLOW_DECAY_PCT,
                                math.log(DECAY_TARGET) / FAST_DECAY_PCT, D_HYENA))
    deltas = jnp.asarray(np.tile(deltas, HYENA_ORDER)[None], F32)
    w1 = jnp.zeros((LANES, width), F32).at[:pw1.shape[0]].set(pw1)
    w3 = pw3.reshape(width, HYENA_ORDER, 2, D_HYENA).transpose(2, 0, 1, 3).reshape(2, width, ncol)
    nt = seq // rows
    full = lambda *shape: pl.BlockSpec(shape, lambda h, i: (0,) * len(shape))
    return pl.pallas_call(
        functools.partial(_filter_kernel, seq=seq, rows=rows),
        grid=(2, nt),
        in_specs=[
            full(1, LANES), full(LANES, width), full(1, width),
            full(pw2.shape[0], width, width), full(pw2.shape[0], 1, width),
            pl.BlockSpec((None, width, ncol), lambda h, i: (h, 0, 0)),
            full(1, width), full(1, ncol),
        ],
        out_specs=[
            pl.BlockSpec((rows, ncol), lambda h, i: (h * nt + i, 0)),
            pl.BlockSpec((8, ncol), lambda h, i: (0, 0)),
        ],
        out_shape=[jax.ShapeDtypeStruct((2 * seq, ncol), F32),
                   jax.ShapeDtypeStruct((8, ncol), F32)],
        compiler_params=_params("arbitrary", "arbitrary"),
        name="hyena_filters",
    )(bands, w1, pb1.reshape(1, width), pw2, pb2.reshape(pw2.shape[0], 1, width), w3,
      freq.reshape(1, width), deltas)


N1 = LANES
UNROLL = 8


def _dft_tables(seq):
    tables = _dft_tables_np(seq)
    return tuple(jnp.asarray(t, BF16) for t in tables[:5]) + tables[5:]


def _dft_tables_np(seq):
    m = 2 * seq
    n2 = m // N1
    n2h = n2 // 2
    n1 = np.arange(N1)[:, None, None]
    f2 = np.arange(n2)[None, :, None]
    k2 = np.arange(n2)[None, None, :]
    th = 2.0 * np.pi * (n1 * f2 / m + (k2 * f2 % n2) / n2)
    fwd_a = np.concatenate([np.cos(th), -np.sin(th)], axis=1)
    tht = np.transpose(th, (0, 2, 1))
    inv_a = np.concatenate([np.cos(tht), -np.sin(tht)], axis=2)[:, :n2h] / m
    a = np.arange(N1)
    ph = 2.0 * np.pi * np.outer(a, a) / N1
    c, s = np.cos(ph), np.sin(ph)
    fwd_b = np.block([[c, s], [-s, c]])
    inv_b = np.block([[c, -s], [s, c]])
    return fwd_a, fwd_a[:, :, :n2h], inv_a, fwd_b, inv_b, n2, n2h


def _stage_a_fwd(x_ref, wa_ref, y_ref, n2, scale=None):
    def body(i, carry):
        trips = [i * UNROLL + j for j in range(UNROLL)]
        xs = [x_ref[pl.ds(n1, wa_ref.shape[2], stride=N1), :] for n1 in trips]
        if scale is not None:
            xs = [x * scale for x in xs]
        prods = [_dot(wa_ref[n1], x.astype(BF16)) for n1, x in zip(trips, xs)]
        for n1, a in zip(trips, prods):
            y_ref[pl.ds(n1, n2, stride=2 * N1), :] = a[:n2]
            y_ref[pl.ds(N1 + n1, n2, stride=2 * N1), :] = a[n2:]
        return carry
    lax.fori_loop(0, N1 // UNROLL, body, 0)


def _filter_fft_kernel(k_ref, ss_ref, wa_ref, fb_ref, o_ref, y_ref, *, n2):
    scale = lax.rsqrt(ss_ref[0:1, :] + FILTER_NORM_EPS)
    _stage_a_fwd(k_ref, wa_ref, y_ref, n2, scale=scale)

    unr = min(UNROLL, n2)

    def body(i, carry):
        trips = [i * unr + j for j in range(unr)]
        ys = [y_ref[pl.ds(pl.multiple_of(f2 * 2 * N1, 2 * N1), 2 * N1), :].astype(BF16) for f2 in trips]
        for f2, y in zip(trips, ys):
            o_ref[f2] = _dot(fb_ref[...], y)
        return carry
    lax.fori_loop(0, n2 // unr, body, 0)


def _filter_spectrum(k2, ss, seq):
    fwd_a, _, _, fwd_b, _, n2, _ = _dft_tables(seq)
    ncol = k2.shape[1]
    nblk = ncol // LANES
    return pl.pallas_call(
        functools.partial(_filter_fft_kernel, n2=n2),
        grid=(nblk,),
        in_specs=[
            pl.BlockSpec((2 * seq, LANES), lambda c: (0, c)),
            pl.BlockSpec((8, LANES), lambda c: (0, c)),
            pl.BlockSpec(fwd_a.shape, lambda c: (0, 0, 0)),
            pl.BlockSpec(fwd_b.shape, lambda c: (0, 0)),
        ],
        out_specs=pl.BlockSpec((None, n2, 2 * N1, LANES), lambda c: (c, 0, 0, 0)),
        out_shape=jax.ShapeDtypeStruct((nblk, n2, 2 * N1, LANES), F32),
        scratch_shapes=[pltpu.VMEM((n2 * 2 * N1, LANES), F32)],
        compiler_params=_params("arbitrary"),
        name="hyena_filter_fft",
    )(k2, ss, fwd_a, fwd_b)


TILE = 8
N1_GROUPS = N1 // TILE


def _tile_tables(seq):
    _, fwd_a, inv_a, _, _, n2, n2h = _dft_tables_np(seq)
    eye = np.eye(TILE)
    fa = fwd_a.reshape(N1_GROUPS, TILE, 2 * n2, n2h)
    wa = np.einsum("qjrn,jk->qrjnk", fa, eye).reshape(N1_GROUPS, 2 * n2 * TILE, n2h * TILE)
    ia = inv_a.reshape(N1_GROUPS, TILE, n2h, 2 * n2)
    vc = np.einsum("qjnr,jk->qnjrk", ia, eye).reshape(N1_GROUPS, n2h * TILE, 2 * n2 * TILE)
    return jnp.asarray(wa, BF16), jnp.asarray(vc, BF16)


def _conv_kernel(u_ref, g_ref, skip_ref, fb_ref, ib_ref, kh_hbm, wa_hbm, vc_hbm, o_ref,
                 y_ref, kh_ref, wa_ref, vc_ref, sem, *, n2, n2h, kh_first):
    c_id, b_id = pl.program_id(0), pl.program_id(1)

    @pl.when(jnp.logical_and(c_id == 0, b_id == 0))
    def _():
        for src, dst in ((wa_hbm, wa_ref), (vc_hbm, vc_ref)):
            cp = pltpu.make_async_copy(src, dst, sem)
            cp.start()
            cp.wait()

    @pl.when(b_id == 0)
    def _():
        cp = pltpu.make_async_copy(kh_hbm.at[kh_first + c_id], kh_ref, sem)
        cp.start()
        cp.wait()

    def y_tile(rf, base):
        ri, f2 = divmod(rf, n2)
        return pl.ds(f2 * 2 * N1 + ri * N1 + base, TILE)

    def stage_a(q, carry):
        base = pl.multiple_of(q * TILE, TILE)
        x = jnp.concatenate([u_ref[pl.ds(N1 * m + base, TILE), :] for m in range(n2h)], axis=0)
        r = _dot(wa_ref[q], x.astype(BF16))
        for rf in range(2 * n2):
            y_ref[y_tile(rf, base), :] = r[rf * TILE:(rf + 1) * TILE]
        return carry
    lax.fori_loop(0, N1_GROUPS, stage_a, 0, unroll=2)

    unr = min(UNROLL, n2)

    def mid(i, carry):
        trips = [i * unr + j for j in range(unr)]
        offs = [pl.multiple_of(f2 * 2 * N1, 2 * N1) for f2 in trips]
        zs = [_dot(fb_ref[...], y_ref[pl.ds(off, 2 * N1), :].astype(BF16)) for off in offs]
        ps = []
        for f2, z in zip(trips, zs):
            zr, zi = z[:N1], z[N1:]
            kh = kh_ref[f2]
            kr, ki = kh[:N1], kh[N1:]
            ps.append(jnp.concatenate([zr * kr - zi * ki, zr * ki + zi * kr], axis=0).astype(BF16))
        gs = [_dot(ib_ref[...], p) for p in ps]
        for off, g in zip(offs, gs):
            y_ref[pl.ds(off, 2 * N1), :] = g
        return carry
    lax.fori_loop(0, n2 // unr, mid, 0)

    skip = skip_ref[...]

    def stage_c(q, carry):
        base = pl.multiple_of(q * TILE, TILE)
        g = jnp.concatenate([y_ref[y_tile(rf, base), :] for rf in range(2 * n2)], axis=0)
        conv = _dot(vc_ref[q], g.astype(BF16))
        for m in range(n2h):
            rows = pl.ds(N1 * m + base, TILE)
            o_ref[rows, :] = g_ref[rows, :] * (conv[m * TILE:(m + 1) * TILE] + u_ref[rows, :] * skip)
        return carry
    lax.fori_loop(0, N1_GROUPS, stage_c, 0, unroll=2)


def _long_conv_gate(u, u_col, gate, gate_col, khat, skip, order):
    bsz, seq, _ = u.shape
    ch = D_HYENA
    _, _, _, fwd_b, inv_b, n2, n2h = _dft_tables(seq)
    wa, vc = _tile_tables(seq)
    nblk = ch // LANES
    const = lambda a: pl.BlockSpec(a.shape, lambda c, b: (0,) * a.ndim)
    at = lambda col: pl.BlockSpec((None, seq, LANES), lambda c, b: (b, 0, col // LANES + c))
    hbm = pl.BlockSpec(memory_space=pl.ANY)
    return pl.pallas_call(
        functools.partial(_conv_kernel, n2=n2, n2h=n2h, kh_first=order * nblk),
        grid=(nblk, bsz),
        in_specs=[
            at(u_col), at(gate_col),
            pl.BlockSpec((1, LANES), lambda c, b: (0, c)),
            const(fwd_b), const(inv_b), hbm, hbm, hbm,
        ],
        out_specs=at(0),
        out_shape=jax.ShapeDtypeStruct((bsz, seq, ch), F32),
        scratch_shapes=[pltpu.VMEM((n2 * 2 * N1, LANES), F32), pltpu.VMEM(khat.shape[1:], F32),
                        pltpu.VMEM(wa.shape, BF16), pltpu.VMEM(vc.shape, BF16), pltpu.SemaphoreType.DMA(())],
        compiler_params=_params("arbitrary", "arbitrary"),
        name=f"hyena_conv{order}",
    )(u, gate, skip[order].reshape(1, ch), fwd_b, inv_b, khat, wa, vc)


HALO = 8


def _shift_rows(p, k):
    return pltpu.roll(p, k % p.shape[0], axis=0)


def _proj_kernel(xp_ref, x_ref, xn_ref, mod_ref, g1_ref, why_ref, wrkv_ref, wlora_ref,
                 cw_ref, cb_ref, murkv_ref, mulora_ref, w0_ref, a0_ref, wwah_ref, wwal_ref, gup_ref,
                 kk_ref, ka_ref, rk_ref, ones_ref,
                 uhy_ref, rkvk_ref, lwa_ref, g_ref, bonus_ref,
                 *, tt, nt):
    i = pl.program_id(1)
    xe = jnp.concatenate([xp_ref[...], x_ref[...], xn_ref[...]], axis=0)
    ms = jnp.mean(xe * xe, axis=-1, keepdims=True)
    h = xe * lax.rsqrt(ms + NORM_EPS) * g1_ref[...]
    h = h * (1.0 + mod_ref[1:2, :]) + mod_ref[0:1, :]
    row = lax.broadcasted_iota(jnp.int32, (tt + 2 * HALO, 1), 0)
    inside = jnp.logical_and(jnp.logical_or(row >= HALO, i > 0),
                             jnp.logical_or(row < tt + HALO, i < nt - 1))
    hb = jnp.where(inside, h, 0.0).astype(BF16)
    mid = slice(HALO, tt + HALO)

    p = _dot(hb, why_ref[...])
    u = (_shift_rows(p, 1) * cw_ref[0:1, :] + p * cw_ref[1:2, :]
         + _shift_rows(p, -1) * cw_ref[2:3, :] + cb_ref[...])
    uhy_ref[...] = u[mid]

    p = _dot(hb, wrkv_ref[...])
    p = p + murkv_ref[...] * (0.5 * (_shift_rows(p, 1) + _shift_rows(p, -1)) - p)
    p = p[mid]
    c = D_RWKV
    r, k, v = p[:, :c], p[:, c:2 * c], p[:, 2 * c:]
    rkvk_ref[:, :3 * c] = p

    q = _dot(hb, wlora_ref[...])
    q = q + mulora_ref[...] * (0.5 * (_shift_rows(q, 1) + _shift_rows(q, -1)) - q)
    q = q[mid]
    wa = q[:, :LANES]
    lane = lax.broadcasted_iota(jnp.int32, wa.shape, 1)
    wa = jnp.where(lane < 2 * DECAY_LORA, jnp.tanh(wa), wa)
    wah, wal = _split2(wa)
    up = _dot(wah, wwah_ref[...]) + (_dot(wah, wwal_ref[...]) + _dot(wal, wwah_ref[...]))
    lw = -math.exp(-0.5) * jax.nn.sigmoid(w0_ref[...] + up[:, :2 * c])
    a = jax.nn.sigmoid(a0_ref[...] + up[:, 2 * c:])
    for dd in range(2):
        lwa_ref[:, 2 * dd * c:(2 * dd + 1) * c] = lw[:, dd * c:(dd + 1) * c]
        lwa_ref[:, (2 * dd + 1) * c:(2 * dd + 2) * c] = a[:, dd * c:(dd + 1) * c]
    g_ref[...] = _dot3(jax.nn.sigmoid(q[:, LANES:]), gup_ref[...])

    ones = ones_ref[...]
    kk = k * kk_ref[...]
    nrm = jnp.sqrt(_dot_exact_rhs(kk * kk, ones))
    rkvk_ref[:, 3 * c:] = kk / jnp.maximum(nrm, 1e-12)
    ka = ka_ref[...]
    ksum = k * (2.0 + (a[:, :c] + a[:, c:] - 2.0) * ka)
    bonus_ref[...] = _dot_exact_rhs(r * ksum * rk_ref[...], ones) * v


def _head_ones():
    hid = np.arange(D_RWKV) // HEAD
    return jnp.asarray(hid[:, None] == hid[None, :], BF16)


def _projection(x, mod, norm1_g, w_in, hy_conv_w, hy_conv_b, rw_mu, rw_w0, rw_w_up, rw_a0,
                rw_a_up, rw_g_up, rw_k_k, rw_k_a, rw_r_k, tt=512):
    bsz, seq, d = x.shape
    tt = min(tt, seq)
    nt = seq // tt
    c = D_RWKV
    hy = (HYENA_ORDER + 1) * D_HYENA
    nlora = 2 * LANES
    w_hy = w_in[:, :hy].astype(BF16)
    w_rkv = w_in[:, hy:hy + 3 * c].astype(BF16)
    w_lora = jnp.zeros((d, nlora), F32).at[:, :w_in.shape[1] - hy - 3 * c].set(w_in[:, hy + 3 * c:]).astype(BF16)
    mu_rkv = rw_mu[:3 * c].reshape(1, 3 * c)
    mu_lora = jnp.zeros((1, nlora), F32).at[0, :rw_mu.shape[0] - 3 * c].set(rw_mu[3 * c:])
    wwa = jnp.zeros((LANES, 4 * c), F32)
    for dd in range(2):
        wwa = wwa.at[dd * DECAY_LORA:(dd + 1) * DECAY_LORA, dd * c:(dd + 1) * c].set(rw_w_up[dd])
        wwa = wwa.at[2 * DECAY_LORA + dd * ICLR_LORA:2 * DECAY_LORA + (dd + 1) * ICLR_LORA,
                     2 * c + dd * c:2 * c + (dd + 1) * c].set(rw_a_up[dd])
    gup = jnp.zeros((LANES, c), F32).at[:GATE_LORA].set(rw_g_up)
    row = lambda a: a.reshape(1, -1)

    nb8 = seq // HALO
    tb = tt // HALO
    const = lambda a: pl.BlockSpec(a.shape, lambda b, i: (0,) * a.ndim, pipeline_mode=pl.Buffered(1))
    tile = lambda w: pl.BlockSpec((None, tt, w), lambda b, i: (b, i, 0))
    ins = [
        (x, pl.BlockSpec((None, HALO, d), lambda b, i: (b, jnp.maximum(i * tb - 1, 0), 0))),
        (x, pl.BlockSpec((None, tt, d), lambda b, i: (b, i, 0))),
        (x, pl.BlockSpec((None, HALO, d), lambda b, i: (b, jnp.minimum((i + 1) * tb, nb8 - 1), 0))),
        (mod, pl.BlockSpec((None,) + mod.shape[1:], lambda b, i: (b, 0, 0))),
    ]
    consts = [row(norm1_g), w_hy, w_rkv, w_lora, hy_conv_w, row(hy_conv_b), mu_rkv, mu_lora,
              row(rw_w0), row(rw_a0), *_split2(wwa), gup, row(rw_k_k), row(rw_k_a), row(rw_r_k), _head_ones()]
    ins += [(a, const(a)) for a in consts]
    widths = [hy, 4 * c, 4 * c, c, c]
    return pl.pallas_call(
        functools.partial(_proj_kernel, tt=tt, nt=nt),
        grid=(bsz, nt),
        in_specs=[s for _, s in ins],
        out_specs=[tile(w) for w in widths],
        out_shape=[jax.ShapeDtypeStruct((bsz, seq, w), F32) for w in widths],
        compiler_params=_params("arbitrary", "arbitrary"),
        name="input_projection",
    )(*[a for a, _ in ins])


CHUNK = HEAD
GROUP = MXU_DIM // HEAD


def _nt(a, b):
    return lax.dot_general(a, b, (((1,), (1,)), ((), ())), preferred_element_type=F32)


def _tn(a, b):
    return lax.dot_general(a, b, (((0,), (0,)), ((), ())), preferred_element_type=F32)


def _wkv_direction(r, k, v, kk, lw, a, ka, s_ref, reverse):
    c = CHUNK
    ti = lax.broadcasted_iota(jnp.int32, (c, c), 0)
    si = lax.broadcasted_iota(jnp.int32, (c, c), 1)
    tri = (si >= ti) if reverse else (si <= ti)
    cum = _dot_exact_rhs_lhs(jnp.where(tri, 1.0, 0.0).astype(BF16), lw)
    tot = jnp.sum(lw, axis=0, keepdims=True)
    w_incl = jnp.exp(cum)
    w_prev = jnp.exp(cum - lw)
    w_inv = jnp.exp(-cum)
    w_end = jnp.exp(tot - cum)
    w_tot = jnp.exp(tot)
    kd = k * (1.0 + (a - 1.0) * ka)
    b = kk * a
    a_w = -kk * w_prev
    r_w = r * w_incl
    b_w = b * w_inv
    k_w = kd * w_inv
    b_e = b * w_end
    k_e = kd * w_end

    m = MXU_DIM
    ri = lax.broadcasted_iota(jnp.int32, (m, m), 0)
    ci = lax.broadcasted_iota(jnp.int32, (m, m), 1)
    head_mask = (ri // HEAD) == (ci // HEAD)
    tl = lax.broadcasted_iota(jnp.int32, (c, m), 0)
    sl = lax.broadcasted_iota(jnp.int32, (c, m), 1) % c
    strict = (sl > tl) if reverse else (sl < tl)
    incl = (sl >= tl) if reverse else (sl <= tl)
    eye = jnp.where(sl == tl, 1.0, 0.0)
    both = lambda top, bot: jnp.concatenate([top, bot], axis=0)

    def stack(xg):
        xb = xg.astype(BF16)
        return jnp.where(head_mask, jnp.concatenate([xb] * GROUP, axis=0), jnp.zeros((), BF16))

    streams = []
    for g in range(D_RWKV // m):
        sl_g = slice(g * m, (g + 1) * m)
        streams.append(dict(
            ar=both(a_w[:, sl_g], r_w[:, sl_g]).astype(BF16),
            b_st=stack(b_w[:, sl_g]), k_st=stack(k_w[:, sl_g]), v_st=stack(v[:, sl_g]),
            v=v[:, sl_g], bk=both(b_e[:, sl_g], k_e[:, sl_g]).astype(BF16),
            w_tot=w_tot[:, sl_g], s_ref=s_ref.at[g],
            strict=strict, incl=incl, eye=eye, head_mask=head_mask, stack=stack))
    return streams


def _wkv_streams_step(streams):
    c = CHUNK
    both = lambda top, bot: jnp.concatenate([top, bot], axis=0)
    for st in streams:
        st["s"] = st["s_ref"][...]
        st["xb"] = _nt(st["ar"], st["b_st"])
        st["xk"] = _nt(st["ar"], st["k_st"])
        st["xs"] = _nt(st["ar"], st["s"].astype(BF16))
    for st in streams:
        m_k = both(jnp.where(st["strict"], st["xk"][:c], 0.0), jnp.where(st["incl"], st["xk"][c:], 0.0))
        st["kv"] = _dot(m_k.astype(BF16), st["v_st"])
        st["rhs"] = st["xs"][:c] + st["kv"][:c]
        st["pw"] = jnp.where(st["strict"], st["xb"][:c], 0.0)
        st["t"] = st["eye"] + st["pw"]
        st["p_st"] = st["stack"](st["pw"])
    levels = int(math.log2(c)) - 1
    for st in streams:
        st["pw"] = _dot(st["pw"].astype(BF16), st["p_st"])
        st["p_st"] = st["stack"](st["pw"])
    for lvl in range(1, levels + 1):
        for st in streams:
            if lvl < levels:
                prod = _dot(both(st["pw"], st["t"]).astype(BF16), st["p_st"])
                st["pw"] = prod[:c]
                st["t"] = st["t"] + prod[c:]
                st["p_st"] = st["stack"](st["pw"])
            else:
                st["t"] = st["t"] + _dot(st["t"].astype(BF16), st["p_st"])
    for st in streams:
        st["u"] = _dot(st["t"].astype(BF16), st["stack"](st["rhs"]))
    outs = []
    for st in streams:
        m_rb = jnp.where(st["incl"], st["xb"][c:], 0.0)
        outs.append(st["xs"][c:] + _dot(m_rb.astype(BF16), st["stack"](st["u"])) + st["kv"][c:])
        uv = both(st["u"], st["v"]).astype(BF16)
        st["s_ref"][...] = st["s"] * st["w_tot"] + jnp.where(st["head_mask"], _tn(uv, st["bk"]), 0.0)
    return outs


def _dot_exact_rhs_lhs(tri_bf16, x):
    xh, xl = _split2(x)
    return _dot(tri_bf16, xh) + _dot(tri_bf16, xl)


def _wkv_kernel(rkvk_f, lwa_f, rkvk_b, lwa_b, ka_ref, of_ref, ob_ref, s_ref, *, nch):
    @pl.when(pl.program_id(1) == 0)
    def _():
        s_ref[...] = jnp.zeros_like(s_ref)

    ka = ka_ref[...]
    c = D_RWKV

    def operands(rkvk_ref, lwa_ref, rows):
        x = rkvk_ref[rows, :]
        la = lwa_ref[rows, :]
        return x[:, :c], x[:, c:2 * c], x[:, 2 * c:3 * c], x[:, 3 * c:], la[:, :c], la[:, c:]

    for ci in range(nch):
        rows_f = slice(ci * CHUNK, (ci + 1) * CHUNK)
        rows_b = slice((nch - 1 - ci) * CHUNK, (nch - ci) * CHUNK)
        fwd = _wkv_direction(*operands(rkvk_f, lwa_f, rows_f), ka, s_ref.at[0], False)
        bwd = _wkv_direction(*operands(rkvk_b, lwa_b, rows_b), ka, s_ref.at[1], True)
        outs = _wkv_streams_step(fwd + bwd)
        of_ref[rows_f, :] = jnp.concatenate(outs[:len(fwd)], axis=1)
        ob_ref[rows_b, :] = jnp.concatenate(outs[len(fwd):], axis=1)


WKV_CHUNKS_PER_STEP = 2


def _wkv(rkvk, lwa, rw_k_a):
    bsz, seq, _ = rkvk.shape
    c = D_RWKV
    nch = WKV_CHUNKS_PER_STEP if seq % (WKV_CHUNKS_PER_STEP * CHUNK) == 0 else 1
    rows = nch * CHUNK
    nb = seq // rows
    fwd = lambda w, lane_blk: pl.BlockSpec((None, rows, w), lambda b, j: (b, j, lane_blk))
    bwd = lambda w, lane_blk: pl.BlockSpec((None, rows, w), lambda b, j: (b, nb - 1 - j, lane_blk))
    return pl.pallas_call(
        functools.partial(_wkv_kernel, nch=nch),
        grid=(bsz, nb),
        in_specs=[fwd(4 * c, 0), fwd(2 * c, 0), bwd(4 * c, 0), bwd(2 * c, 1),
                  pl.BlockSpec((1, c), lambda b, j: (0, 0))],
        out_specs=[fwd(c, 0), bwd(c, 0)],
        out_shape=[jax.ShapeDtypeStruct((bsz, seq, c), F32)] * 2,
        scratch_shapes=[pltpu.VMEM((2, c // MXU_DIM, MXU_DIM, MXU_DIM), F32)],
        compiler_params=_params("arbitrary", "arbitrary"),
        name="wkv7_chunked",
    )(rkvk, lwa, rkvk, lwa, rw_k_a.reshape(1, c))


NEG_INF = float("-inf")


def _first_max(vals, idx, size):
    m = jnp.max(vals, axis=0, keepdims=True)
    i = jnp.min(jnp.where(vals == m, idx, size), axis=0, keepdims=True)
    return m, i


def _route(scores, biased):
    e, tt = scores.shape
    per = e // N_GROUPS
    rowl = lax.broadcasted_iota(jnp.int32, (per, tt), 0)
    gs = []
    for g in range(N_GROUPS):
        blk = biased[g * per:(g + 1) * per]
        m1, i1 = _first_max(blk, rowl, per)
        m2 = jnp.max(jnp.where(rowl == i1, NEG_INF, blk), axis=0, keepdims=True)
        gs.append(m1 + m2)
    cur = jnp.concatenate(gs, axis=0)
    growl = lax.broadcasted_iota(jnp.int32, (N_GROUPS, tt), 0)
    gsel = jnp.zeros((N_GROUPS, tt), F32)
    for _ in range(TOPK_GROUPS):
        _, ig = _first_max(cur, growl, N_GROUPS)
        hit = growl == ig
        gsel = jnp.where(hit, 1.0, gsel)
        cur = jnp.where(hit, NEG_INF, cur)
    emask = jnp.concatenate([jnp.broadcast_to(gsel[g:g + 1], (per, tt)) for g in range(N_GROUPS)], axis=0)
    masked = jnp.where(emask > 0.5, biased, NEG_INF)
    row = lax.broadcasted_iota(jnp.int32, (e, tt), 0)
    ids, ws = [], []
    for _ in range(TOP_K):
        _, ie = _first_max(masked, row, e)
        hit = row == ie
        ids.append(ie)
        ws.append(jnp.sum(jnp.where(hit, scores, 0.0), axis=0, keepdims=True))
        masked = jnp.where(hit, NEG_INF, masked)
    w = jnp.concatenate(ws, axis=0)
    w = w / jnp.sum(w, axis=0, keepdims=True) * ROUTE_SCALE
    return jnp.concatenate(ids, axis=0), w


def _mixout_kernel(x_ref, mod_ref, yhy_ref, of_ref, ob_ref, g_ref, bonus_ref, lnw_ref, lnb_ref,
                   ones_ref, wout_ref, g2n_ref, rwth_ref, rwtl_ref, bias_ref,
                   x1_ref, h2a_ref, h2b_ref, eid_ref, wsel_ref):
    ones = ones_ref[...]
    s = of_ref[...] + ob_ref[...]
    mean = _dot_exact_rhs(s, ones) * (1.0 / HEAD)
    dlt = s - mean
    var = _dot_exact_rhs(dlt * dlt, ones) * (1.0 / HEAD)
    sn = dlt * lax.rsqrt(var + GN_EPS) * lnw_ref[...] + lnb_ref[...]
    yrw = (sn + bonus_ref[...]) * g_ref[...]
    ch = yhy_ref.shape[-1]
    mix = _dot(yhy_ref[...].astype(BF16), wout_ref[:ch, :]) + _dot(yrw.astype(BF16), wout_ref[ch:, :])
    x1 = x_ref[...] + mod_ref[2:3, :] * mix
    x1_ref[...] = x1
    ms = jnp.mean(x1 * x1, axis=-1, keepdims=True)
    h2 = x1 * lax.rsqrt(ms + NORM_EPS) * g2n_ref[...]
    h2 = h2 * (1.0 + mod_ref[4:5, :]) + mod_ref[3:4, :]
    h2a_ref[...], h2b_ref[...] = _pack_rows(h2)
    rh, rl = rwth_ref[...], rwtl_ref[...]
    hh, hl = _split2(h2)
    logits = _nt(rh, hh) + (_nt(rh, hl) + _nt(rl, hh))
    scores = jax.nn.sigmoid(logits)
    ids, w = _route(scores, scores + bias_ref[...])
    eid_ref[...] = ids
    wsel_ref[...] = w


def _mix_out(x, mod, yhy, o_f, o_b, g, bonus, ln_w, ln_b, w_out, norm2_g, router_w, router_bias, tt=512):
    bsz, seq, d = x.shape
    tt = min(tt, seq)
    nt = seq // tt
    n = bsz * seq
    c = D_RWKV
    e = router_w.shape[1]
    row = lambda a: a.reshape(1, -1)
    consts = [row(ln_w), row(ln_b), _head_ones(), w_out.astype(BF16), row(norm2_g), *_split2(router_w.T),
              jnp.broadcast_to(router_bias.reshape(e, 1), (e, tt))]
    const = lambda a: pl.BlockSpec(a.shape, lambda b, i: (0,) * a.ndim, pipeline_mode=pl.Buffered(1))
    tile = lambda w: pl.BlockSpec((None, tt, w), lambda b, i: (b, i, 0))
    flat = lambda rows, dt: jax.ShapeDtypeStruct((rows, n), dt)
    return pl.pallas_call(
        _mixout_kernel,
        grid=(bsz, nt),
        in_specs=[tile(d), pl.BlockSpec((None,) + mod.shape[1:], lambda b, i: (b, 0, 0))]
        + [tile(c)] * 5 + [const(a) for a in consts],
        out_specs=[tile(d), pl.BlockSpec((tt, d // 4), lambda b, i: (b * nt + i, 0)),
                   pl.BlockSpec((tt, d // 4), lambda b, i: (b * nt + i, 0)),
                   pl.BlockSpec((TOP_K, tt), lambda b, i: (0, b * nt + i)),
                   pl.BlockSpec((TOP_K, tt), lambda b, i: (0, b * nt + i))],
        out_shape=[jax.ShapeDtypeStruct((bsz, seq, d), F32), jax.ShapeDtypeStruct((n, d // 4), U32),
                   jax.ShapeDtypeStruct((n, d // 4), U32),
                   flat(TOP_K, jnp.int32), flat(TOP_K, F32)],
        compiler_params=_params("arbitrary", "arbitrary"),
        name="mix_out_router",
    )(x, mod, yhy, o_f, o_b, g, bonus, *consts)


BLK = 512
BLK_SHIFT = 9


def _multi_hot(eid, e):
    row = lax.broadcasted_iota(jnp.int32, (e, eid.shape[1]), 0)
    mh = jnp.zeros((e, eid.shape[1]), F32)
    for kk in range(TOP_K):
        mh = mh + jnp.where(row == eid[kk:kk + 1, :], 1.0, 0.0)
    return row, mh


def _lookup(row, eid, table):
    return jnp.concatenate(
        [jnp.sum(jnp.where(row == eid[kk:kk + 1, :], table, 0.0), axis=0, keepdims=True)
         for kk in range(TOP_K)], axis=0)


def _rank_kernel(eid_ref, rank_ref, cnt_ref, *, e):
    @pl.when(pl.program_id(0) == 0)
    def _():
        cnt_ref[...] = jnp.zeros_like(cnt_ref)

    eid = eid_ref[...]
    tt = eid.shape[1]
    row, mh = _multi_hot(eid, e)
    mhb = mh.astype(BF16)
    si = lax.broadcasted_iota(jnp.int32, (tt, tt), 0)
    ti = lax.broadcasted_iota(jnp.int32, (tt, tt), 1)
    earlier = _dot(mhb, jnp.where(si < ti, 1.0, 0.0).astype(BF16))
    cnt = cnt_ref[...]
    full = earlier + jnp.concatenate([cnt] * (tt // LANES), axis=1)
    rank_ref[...] = _lookup(row, eid, full).astype(jnp.int32)
    cnt_ref[...] = cnt + _dot(mhb, jnp.ones((tt, LANES), BF16))


def _expert_ranks(eid, e, tt=512):
    n = eid.shape[1]
    tt = min(tt, n)
    return pl.pallas_call(
        functools.partial(_rank_kernel, e=e),
        grid=(n // tt,),
        in_specs=[pl.BlockSpec((TOP_K, tt), lambda i: (0, i))],
        out_specs=[pl.BlockSpec((TOP_K, tt), lambda i: (0, i)),
                   pl.BlockSpec((e, LANES), lambda i: (0, 0))],
        out_shape=[jax.ShapeDtypeStruct((TOP_K, n), jnp.int32), jax.ShapeDtypeStruct((e, LANES), F32)],
        compiler_params=_params("arbitrary"),
        name="expert_ranks",
    )(eid)


def _block_offsets(cnt):
    e = cnt.shape[0]
    nblk = ((cnt.astype(jnp.int32) + (BLK - 1)) >> BLK_SHIFT).astype(F32)
    ri = lax.broadcasted_iota(jnp.int32, (e, e), 0)
    ci = lax.broadcasted_iota(jnp.int32, (e, e), 1)
    tril = jnp.where(ci <= ri, 1.0, 0.0).astype(BF16)
    nh, nl = _split2(nblk)
    return nblk, _dot(tril, nh) + _dot(tril, nl)


def _dest_kernel(cnt_ref, eid_ref, rank_ref, dest_ref):
    nblk, end = _block_offsets(cnt_ref[...])
    off = (end - nblk) * float(BLK)
    eid = eid_ref[...]
    tt = eid.shape[1]
    row = lax.broadcasted_iota(jnp.int32, (off.shape[0], tt), 0)
    table = jnp.concatenate([off] * (tt // LANES), axis=1)
    dest_ref[...] = _lookup(row, eid, table).astype(jnp.int32) + rank_ref[...]


def _destinations(cnt, eid, rank, tt=512):
    n = eid.shape[1]
    tt = min(tt, n)
    blk = pl.BlockSpec((TOP_K, tt), lambda i: (0, i))
    return pl.pallas_call(
        _dest_kernel,
        grid=(n // tt,),
        in_specs=[pl.BlockSpec(cnt.shape, lambda i: (0, 0)), blk, blk],
        out_specs=blk,
        out_shape=jax.ShapeDtypeStruct((TOP_K, n), jnp.int32),
        compiler_params=_params("arbitrary"),
        name="expert_destinations",
    )(cnt, eid, rank)


def _meta_kernel(cnt_ref, meta_ref, *, nbp):
    cnt = cnt_ref[...]
    e = cnt.shape[0]
    nblk, end = _block_offsets(cnt)
    rep = lambda a, w: jnp.concatenate([a] * (w // LANES), axis=1)
    b = lax.broadcasted_iota(jnp.int32, (e, nbp), 1).astype(F32)
    blk_e = jnp.minimum(jnp.sum(jnp.where(rep(end, nbp) <= b, 1.0, 0.0), axis=0, keepdims=True), float(e - 1))
    row = lax.broadcasted_iota(jnp.int32, (e, nbp), 0).astype(F32)
    mine = row == blk_e
    left = rep(cnt + (end - nblk) * float(BLK), nbp) - b * float(BLK)
    nvalid = jnp.clip(jnp.sum(jnp.where(mine, left, 0.0), axis=0, keepdims=True), 0.0, float(BLK))
    nused = jnp.max(rep(end, nbp), axis=0, keepdims=True)
    later = jnp.logical_and(row > blk_e, rep(nblk, nbp) > 0.0)
    nxt = jnp.min(jnp.where(later, row, float(e)), axis=0, keepdims=True)
    nxt = jnp.where(nxt >= float(e), -1.0, nxt)
    meta_ref[...] = jnp.concatenate([blk_e, nvalid, nused, nxt, jnp.zeros((4, nbp), F32)],
                                    axis=0).astype(jnp.int32)


def _block_meta(cnt, nb):
    nbp = -(-nb // LANES) * LANES
    return pl.pallas_call(
        functools.partial(_meta_kernel, nbp=nbp),
        out_shape=jax.ShapeDtypeStruct((8, nbp), jnp.int32),
        compiler_params=pltpu.CompilerParams(vmem_limit_bytes=VMEM_LIMIT),
        name="expert_block_meta",
    )(cnt)


SC_WINDOW = 128


def _sc_mesh():
    return plsc.VectorSubcoreMesh(core_axis_name="core", subcore_axis_name="subcore")


def _sc_scatter_rows(rows, idx, nrows):
    n, width = rows.shape

    @pl.kernel(out_type=jax.ShapeDtypeStruct((nrows, width), rows.dtype), mesh=_sc_mesh())
    def scatter(rows_hbm, idx_hbm, out_hbm):
        def body(rows_vmem, idx_vmem):
            pltpu.sync_copy(rows_vmem, out_hbm.at[idx_vmem.at[0]])

        pltpu.emit_pipeline(
            body,
            grid=(n // SC_WINDOW, idx.shape[0]),
            in_specs=[pl.BlockSpec((SC_WINDOW, width), index_map=lambda i, k: (i, 0)),
                      pl.BlockSpec((1, SC_WINDOW), index_map=lambda i, k: (k, i))],
            out_specs=[],
            core_axis_name=("core", "subcore"),
            dimension_semantics=(pltpu.PARALLEL, pltpu.ARBITRARY),
        )(rows_hbm, idx_hbm)

    return scatter(rows, idx)


def _sc_gather_rows(src, idx):
    num = idx.shape[1]
    width = src.shape[1]

    @pl.kernel(out_type=jax.ShapeDtypeStruct((num, width), src.dtype), mesh=_sc_mesh())
    def gather(src_hbm, idx_hbm, out_hbm):
        def body(idx_vmem, out_vmem):
            pltpu.sync_copy(src_hbm.at[idx_vmem.at[0]], out_vmem)

        pltpu.emit_pipeline(
            body,
            grid=(num // SC_WINDOW,),
            in_specs=[pl.BlockSpec((1, SC_WINDOW), index_map=lambda i: (0, i))],
            out_specs=[pl.BlockSpec((SC_WINDOW, width), index_map=lambda i: (i, 0))],
            core_axis_name=("core", "subcore"),
            dimension_semantics=(pltpu.PARALLEL,),
        )(idx_hbm, out_hbm)

    return gather(src, idx)


def _experts_kernel(be_ref, nv_ref, nu_ref, nxt_ref, xa_ref, xb_ref, wg_hbm, wu_hbm, wd_hbm, oa_ref, ob_ref,
                    wgf, wuf, wdf, wgb, wub, wdb, sems, slot_ref):
    b = pl.program_id(0)
    used = b < nu_ref[0]

    def fetch(expert, slot):
        return [pltpu.make_async_copy(src.at[expert], dst.at[slot], sems.at[slot])
                for src, dst in ((wg_hbm, wgf), (wu_hbm, wuf), (wd_hbm, wdf))]

    @pl.when(b == 0)
    def _():
        slot_ref[0] = 0
        for cp in fetch(be_ref[0], 0):
            cp.start()

    @pl.when(used)
    def _():
        prev = be_ref[jnp.maximum(b - 1, 0)]

        @pl.when(jnp.logical_or(b == 0, be_ref[b] != prev))
        def _():
            slot = slot_ref[0]
            for cp in fetch(be_ref[b], slot):
                cp.wait()

            @pl.when(nxt_ref[b] >= 0)
            def _():
                for cp in fetch(nxt_ref[b], 1 - slot):
                    cp.start()

            wgb[...] = wgf[slot].astype(BF16)
            wub[...] = wuf[slot].astype(BF16)
            wdb[...] = wdf[slot].astype(BF16)
            slot_ref[0] = 1 - slot

        valid = lax.broadcasted_iota(jnp.int32, (xa_ref.shape[0], 1), 0) < nv_ref[b]
        zero = jnp.zeros((), U32)
        x = _unpack_rows(jnp.where(valid, xa_ref[...], zero), jnp.where(valid, xb_ref[...], zero))
        act = _silu(_dot(x, wgb[...])) * _dot(x, wub[...])
        oa_ref[...], ob_ref[...] = _pack_rows(_dot(act.astype(BF16), wdb[...]))


def _experts(blk_e, nvalid, nused, nxt_e, xs_a, xs_b, wg, wu, wd):
    p, dq = xs_a.shape
    nb = p // BLK
    d, de = wg.shape[1], wg.shape[2]
    rows_in = pl.BlockSpec((BLK, dq), lambda b, be, nv, nu, nx: (jnp.minimum(b, nu[0] - 1), 0))
    hbm = pl.BlockSpec(memory_space=pl.ANY)
    return pl.pallas_call(
        _experts_kernel,
        grid_spec=pltpu.PrefetchScalarGridSpec(
            num_scalar_prefetch=4,
            grid=(nb,),
            in_specs=[rows_in, rows_in, hbm, hbm, hbm],
            out_specs=[rows_in, rows_in],
            scratch_shapes=[pltpu.VMEM((2, d, de), F32), pltpu.VMEM((2, d, de), F32), pltpu.VMEM((2, de, d), F32),
                            pltpu.VMEM((d, de), BF16), pltpu.VMEM((d, de), BF16), pltpu.VMEM((de, d), BF16),
                            pltpu.SemaphoreType.DMA((2,)), pltpu.SMEM((1,), jnp.int32)],
        ),
        out_shape=[jax.ShapeDtypeStruct((p, dq), U32)] * 2,
        compiler_params=_params("arbitrary"),
        name="moe_experts",
    )(blk_e, nvalid, nused, nxt_e, xs_a, xs_b, wg, wu, wd)


def _shared_kernel(ha_ref, hb_ref, sg_ref, su_ref, sd_ref, o_ref):
    hb = _unpack_rows(ha_ref[...], hb_ref[...])
    act = _silu(_dot(hb, sg_ref[...])) * _dot(hb, su_ref[...])
    o_ref[...] = _dot(act.astype(BF16), sd_ref[...]).astype(o_ref.dtype)


def _shared_expert(h2a, h2b, sh_wg, sh_wu, sh_wd, tt=512):
    n, dp = h2a.shape
    d = sh_wg.shape[0]
    tt = min(tt, n)
    consts = [sh_wg.astype(BF16), sh_wu.astype(BF16), sh_wd.astype(BF16)]
    packed_rows = pl.BlockSpec((tt, dp), lambda i: (i, 0))
    return pl.pallas_call(
        _shared_kernel,
        grid=(n // tt,),
        in_specs=[packed_rows, packed_rows] + [pl.BlockSpec(a.shape, lambda i: (0, 0)) for a in consts],
        out_specs=pl.BlockSpec((tt, d), lambda i: (i, 0)),
        out_shape=jax.ShapeDtypeStruct((n, d), BF16),
        compiler_params=_params("arbitrary"),
        name="shared_expert",
    )(h2a, h2b, *consts)


def _combine_kernel(w_ref, x1_ref, sh_ref, mod_ref, ga_ref, gb_ref, gf_ref, sel_ref, o_ref):
    ffn = sh_ref[...].astype(F32)
    wh, wl = _split2(w_ref[...])
    acc = None
    for kk in range(TOP_K):
        sel = sel_ref[kk]
        wk = _tn(wh, sel) + _tn(wl, sel)
        a_lo, a_hi = _unpack_halves(ga_ref[kk])
        b_lo, b_hi = _unpack_halves(gb_ref[kk])
        parts = [a_lo * wk, b_lo * wk, a_hi * wk, b_hi * wk]
        acc = parts if acc is None else [p + q for p, q in zip(acc, parts)]
    ffn = ffn + jnp.concatenate(acc, axis=1)
    xo = x1_ref[...] + mod_ref[5:6, :] * ffn
    ms = jnp.mean(xo * xo, axis=-1, keepdims=True)
    o_ref[...] = xo * lax.rsqrt(ms + NORM_EPS) * gf_ref[...]


def _combine(wsel, x1, shared, mod, ga, gb, normf_g, tt=256):
    bsz, seq, d = x1.shape
    n = bsz * seq
    tt = min(tt, seq)
    per = seq // tt
    dq = ga.shape[2]
    sel = jnp.asarray(np.broadcast_to(np.eye(TOP_K)[:, :, None], (TOP_K, TOP_K, dq)), BF16)
    consts = [normf_g.reshape(1, d), sel]
    const = lambda a: pl.BlockSpec(a.shape, lambda i: (0,) * a.ndim)
    rows = pl.BlockSpec((tt, d), lambda i: (i, 0))
    gathered = pl.BlockSpec((TOP_K, tt, dq), lambda i: (0, i, 0))
    return pl.pallas_call(
        _combine_kernel,
        grid=(n // tt,),
        in_specs=[pl.BlockSpec((TOP_K, tt), lambda i: (0, i)),
                  rows, rows,
                  pl.BlockSpec((None,) + mod.shape[1:], lambda i: (i // per, 0, 0)),
                  gathered, gathered] + [const(a) for a in consts],
        out_specs=rows,
        out_shape=jax.ShapeDtypeStruct((n, d), F32),
        compiler_params=_params("arbitrary"),
        name="moe_combine",
    )(wsel, x1.reshape(n, d), shared, mod, ga, gb, *consts)


def _moe(x1, h2a, h2b, mod, eid, wsel, exp_wg, exp_wu, exp_wd, sh_wg, sh_wu, sh_wd, normf_g):
    n = h2a.shape[0]
    e = exp_wg.shape[0]
    nb = (n * TOP_K + e * (BLK - 1)) // BLK
    rank, cnt = _expert_ranks(eid, e)
    dest = _destinations(cnt, eid, rank)
    meta = _block_meta(cnt, nb)
    xs_a = _sc_scatter_rows(h2a, dest, nb * BLK)
    xs_b = _sc_scatter_rows(h2b, dest, nb * BLK)
    shared = _shared_expert(h2a, h2b, sh_wg, sh_wu, sh_wd)
    ys_a, ys_b = _experts(meta[0, :nb], meta[1, :nb], meta[2, :1], meta[3, :nb], xs_a, xs_b,
                          exp_wg, exp_wu, exp_wd)
    idx = dest.reshape(1, TOP_K * n)
    ga = _sc_gather_rows(ys_a, idx).reshape(TOP_K, n, -1)
    gb = _sc_gather_rows(ys_b, idx).reshape(TOP_K, n, -1)
    return _combine(wsel, x1, shared, mod, ga, gb, normf_g)


def kernel(x, c, norm1_g, norm2_g, normf_g, w_ada, b_ada, w_in, w_out, hy_conv_w, hy_conv_b, hy_pos_w1, hy_pos_b1, hy_pos_w2, hy_pos_b2, hy_pos_w3, hy_sin_freq, hy_skip, rw_mu, rw_w0, rw_w_up, rw_a0, rw_a_up, rw_g_up, rw_k_k, rw_k_a, rw_r_k, rw_ln_w, rw_ln_b, router_w, router_bias, exp_w_gate, exp_w_up, exp_w_down, sh_w_gate, sh_w_up, sh_w_down):
    bsz, seq, d = x.shape
    depth = w_ada.shape[0]
    assert depth == 1, "the final norm is fused into the last kernel of a single layer"
    for l in range(depth):
        mod = _modulation(c, w_ada[l], b_ada[l]).reshape(bsz, -1, d)
        uhy, rkvk, lwa, g, bonus = _projection(
            x, mod, norm1_g[l], w_in[l], hy_conv_w[l], hy_conv_b[l], rw_mu[l], rw_w0[l], rw_w_up[l],
            rw_a0[l], rw_a_up[l], rw_g_up[l], rw_k_k[l], rw_k_a[l], rw_r_k[l])
        k2, ss = _hyena_filters(seq, hy_pos_w1[l], hy_pos_b1[l], hy_pos_w2[l], hy_pos_b2[l],
                                hy_pos_w3[l], hy_sin_freq[l])
        khat = _filter_spectrum(k2, ss, seq)
        z, z_col = uhy, 0
        for order in range(HYENA_ORDER):
            z = _long_conv_gate(z, z_col, uhy, (order + 1) * D_HYENA, khat, hy_skip[l], order)
            z_col = 0
        o_f, o_b = _wkv(rkvk, lwa, rw_k_a[l])
        x1, h2a, h2b, eid, wsel = _mix_out(x, mod, z, o_f, o_b, g, bonus, rw_ln_w[l], rw_ln_b[l], w_out[l],
                                           norm2_g[l], router_w[l], router_bias[l])
        x = _moe(x1, h2a, h2b, mod, eid, wsel, exp_w_gate[l], exp_w_up[l], exp_w_down[l],
                 sh_w_gate[l], sh_w_up[l], sh_w_down[l], normf_g)
        x = x.reshape(bsz, seq, d)
    return x
```

```python
import functools
import math

import jax
import jax.numpy as jnp
import numpy as np
from jax import lax
from jax.experimental import pallas as pl
from jax.experimental.pallas import tpu as pltpu
from jax.experimental.pallas import tpu_sc as plsc

F32 = jnp.float32
BF16 = jnp.bfloat16

LANES = 128
MXU_DIM = 256
VMEM_LIMIT = 56 * 1024 * 1024

D_HYENA = 512
D_RWKV = 512
HEAD = 64
N_HEADS = D_RWKV // HEAD
HYENA_ORDER = 2
FILTER_BANDS = 16
DECAY_TARGET = 1e-2
FAST_DECAY_PCT = 0.3
SLOW_DECAY_PCT = 1.5
FILTER_NORM_EPS = 1e-6
DECAY_LORA = 32
ICLR_LORA = 32
GATE_LORA = 96
GN_EPS = 64e-5
NORM_EPS = 1e-6
N_EXPERTS = 256
TOP_K = 8
N_GROUPS = 8
TOPK_GROUPS = 4
ROUTE_SCALE = 2.5
D_EXPERT = 256


def _params(*sem):
    return pltpu.CompilerParams(dimension_semantics=sem, vmem_limit_bytes=VMEM_LIMIT)


def _split2(a):
    hi = a.astype(BF16)
    lo = (a - hi.astype(F32)).astype(BF16)
    return hi, lo


def _dot(a, b):
    return jnp.dot(a, b, preferred_element_type=F32)


def _dot3(a, b):
    ah, al = _split2(a)
    bh, bl = _split2(b)
    return _dot(ah, bh) + (_dot(ah, bl) + _dot(al, bh))


def _dot_exact_rhs(a, b_bf16):
    ah, al = _split2(a)
    return _dot(ah, b_bf16) + _dot(al, b_bf16)


def _silu(x):
    return x * jax.nn.sigmoid(x)


U32 = jnp.int32


def _pack_halves(x):
    w = x.shape[1] // 2
    return pltpu.pack_elementwise([x[:, :w], x[:, w:]], packed_dtype=BF16)


def _unpack_halves(p):
    lo = pltpu.unpack_elementwise(p, index=0, packed_dtype=BF16, unpacked_dtype=F32)
    hi = pltpu.unpack_elementwise(p, index=1, packed_dtype=BF16, unpacked_dtype=F32)
    return lo, hi


def _pack_rows(x):
    packed = _pack_halves(x)
    half = packed.shape[1] // 2
    return packed[:, :half], packed[:, half:]


def _unpack_rows(a, b):
    a_lo, a_hi = _unpack_halves(a)
    b_lo, b_hi = _unpack_halves(b)
    return jnp.concatenate([a_lo.astype(BF16), b_lo.astype(BF16), a_hi.astype(BF16), b_hi.astype(BF16)], axis=1)


def _mod_kernel(c_ref, w_ref, b_ref, o_ref):
    o_ref[...] = _dot3(_silu(c_ref[...]), w_ref[...]) + b_ref[...]


def _modulation(c, w_ada, b_ada):
    bsz, d = c.shape
    n = w_ada.shape[1]
    blk = 1024
    return pl.pallas_call(
        _mod_kernel,
        grid=(n // blk,),
        in_specs=[
            pl.BlockSpec((bsz, d), lambda j: (0, 0)),
            pl.BlockSpec((d, blk), lambda j: (0, j)),
            pl.BlockSpec((1, blk), lambda j: (0, j)),
        ],
        out_specs=pl.BlockSpec((bsz, blk), lambda j: (0, j)),
        out_shape=jax.ShapeDtypeStruct((bsz, n), F32),
        compiler_params=_params("arbitrary"),
        name="adaln_mod",
    )(c, w_ada, b_ada.reshape(1, n))


def _filter_kernel(band_ref, w1_ref, b1_ref, w2_ref, b2_ref, w3_ref, freq_ref, delta_ref,
                   k_ref, ss_ref, *, seq, rows):
    half = pl.program_id(0)
    i = pl.program_id(1)
    r = lax.broadcasted_iota(jnp.int32, (rows, LANES), 0) + i * rows
    pos = jnp.where(half == 0, r, seq - r).astype(F32)
    tt = pos / float(max(seq - 1, 1))
    lane = lax.broadcasted_iota(jnp.int32, (rows, LANES), 1)
    ang = pos * band_ref[...]
    feats = jnp.where(lane == 0, tt,
                      jnp.where(lane <= FILTER_BANDS, jnp.cos(ang),
                                jnp.where(lane <= 2 * FILTER_BANDS, -jnp.sin(ang), 0.0)))
    freq = freq_ref[...]
    hdn = jnp.sin(freq * (_dot3(feats, w1_ref[...]) + b1_ref[...]))
    for j in range(w2_ref.shape[0]):
        hdn = jnp.sin(freq * (_dot3(hdn, w2_ref[j]) + b2_ref[j]))
    filt = _dot3(hdn, w3_ref[...])
    filt = filt * jnp.exp(-tt[:, :1] * delta_ref[...])
    valid = jnp.logical_or(half == 0, r[:, :1] > 0)
    filt = jnp.where(valid, filt, 0.0)
    k_ref[...] = filt

    @pl.when(jnp.logical_and(half == 0, i == 0))
    def _():
        ss_ref[...] = jnp.zeros_like(ss_ref)

    ss_ref[...] += jnp.broadcast_to(jnp.sum(filt * filt, axis=0, keepdims=True), ss_ref.shape)


def _hyena_filters(seq, pw1, pb1, pw2, pb2, pw3, freq):
    width = pw1.shape[1]
    ncol = HYENA_ORDER * D_HYENA
    rows = min(seq, 512)
    bands = np.zeros((1, LANES), np.float64)
    lin = np.linspace(1e-4, FILTER_BANDS - 1, FILTER_BANDS)
    bands[0, 1:1 + FILTER_BANDS] = lin
    bands[0, 1 + FILTER_BANDS:1 + 2 * FILTER_BANDS] = lin
    bands = jnp.asarray(bands * (2.0 * math.pi / seq), F32)
    deltas = np.abs(np.linspace(math.log(DECAY_TARGET) / SLOW_DECAY_PCT,
                                math.log(DECAY_TARGET) / FAST_DECAY_PCT, D_HYENA))
    deltas = jnp.asarray(np.tile(deltas, HYENA_ORDER)[None], F32)
    w1 = jnp.zeros((LANES, width), F32).at[:pw1.shape[0]].set(pw1)
    w3 = pw3.reshape(width, HYENA_ORDER, 2, D_HYENA).transpose(2, 0, 1, 3).reshape(2, width, ncol)
    nt = seq // rows
    full = lambda *shape: pl.BlockSpec(shape, lambda h, i: (0,) * len(shape))
    return pl.pallas_call(
        functools.partial(_filter_kernel, seq=seq, rows=rows),
        grid=(2, nt),
        in_specs=[
            full(1, LANES), full(LANES, width), full(1, width),
            full(pw2.shape[0], width, width), full(pw2.shape[0], 1, width),
            pl.BlockSpec((None, width, ncol), lambda h, i: (h, 0, 0)),
            full(1, width), full(1, ncol),
        ],
        out_specs=[
            pl.BlockSpec((rows, ncol), lambda h, i: (h * nt + i, 0)),
            pl.BlockSpec((8, ncol), lambda h, i: (0, 0)),
        ],
        out_shape=[jax.ShapeDtypeStruct((2 * seq, ncol), F32),
                   jax.ShapeDtypeStruct((8, ncol), F32)],
        compiler_params=_params("arbitrary", "arbitrary"),
        name="hyena_filters",
    )(bands, w1, pb1.reshape(1, width), pw2, pb2.reshape(pw2.shape[0], 1, width), w3,
      freq.reshape(1, width), deltas)


N1 = LANES
UNROLL = 8


def _dft_tables(seq):
    tables = _dft_tables_np(seq)
    return tuple(jnp.asarray(t, BF16) for t in tables[:5]) + tables[5:]


def _dft_tables_np(seq):
    m = 2 * seq
    n2 = m // N1
    n2h = n2 // 2
    n1 = np.arange(N1)[:, None, None]
    f2 = np.arange(n2)[None, :, None]
    k2 = np.arange(n2)[None, None, :]
    th = 2.0 * np.pi * (n1 * f2 / m + (k2 * f2 % n2) / n2)
    fwd_a = np.concatenate([np.cos(th), -np.sin(th)], axis=1)
    tht = np.transpose(th, (0, 2, 1))
    inv_a = np.concatenate([np.cos(tht), -np.sin(tht)], axis=2)[:, :n2h] / m
    a = np.arange(N1)
    ph = 2.0 * np.pi * np.outer(a, a) / N1
    c, s = np.cos(ph), np.sin(ph)
    fwd_b = np.block([[c, s], [-s, c]])
    inv_b = np.block([[c, -s], [s, c]])
    return fwd_a, fwd_a[:, :, :n2h], inv_a, fwd_b, inv_b, n2, n2h


def _stage_a_fwd(x_ref, wa_ref, y_ref, n2, scale=None):
    def body(i, carry):
        trips = [i * UNROLL + j for j in range(UNROLL)]
        xs = [x_ref[pl.ds(n1, wa_ref.shape[2], stride=N1), :] for n1 in trips]
        if scale is not None:
            xs = [x * scale for x in xs]
        prods = [_dot(wa_ref[n1], x.astype(BF16)) for n1, x in zip(trips, xs)]
        for n1, a in zip(trips, prods):
            y_ref[pl.ds(n1, n2, stride=2 * N1), :] = a[:n2]
            y_ref[pl.ds(N1 + n1, n2, stride=2 * N1), :] = a[n2:]
        return carry
    lax.fori_loop(0, N1 // UNROLL, body, 0)


def _filter_fft_kernel(k_ref, ss_ref, wa_ref, fb_ref, o_ref, y_ref, *, n2):
    scale = lax.rsqrt(ss_ref[0:1, :] + FILTER_NORM_EPS)
    _stage_a_fwd(k_ref, wa_ref, y_ref, n2, scale=scale)

    unr = min(UNROLL, n2)

    def body(i, carry):
        trips = [i * unr + j for j in range(unr)]
        ys = [y_ref[pl.ds(pl.multiple_of(f2 * 2 * N1, 2 * N1), 2 * N1), :].astype(BF16) for f2 in trips]
        for f2, y in zip(trips, ys):
            o_ref[f2] = _dot(fb_ref[...], y)
        return carry
    lax.fori_loop(0, n2 // unr, body, 0)


def _filter_spectrum(k2, ss, seq):
    fwd_a, _, _, fwd_b, _, n2, _ = _dft_tables(seq)
    ncol = k2.shape[1]
    nblk = ncol // LANES
    return pl.pallas_call(
        functools.partial(_filter_fft_kernel, n2=n2),
        grid=(nblk,),
        in_specs=[
            pl.BlockSpec((2 * seq, LANES), lambda c: (0, c)),
            pl.BlockSpec((8, LANES), lambda c: (0, c)),
            pl.BlockSpec(fwd_a.shape, lambda c: (0, 0, 0)),
            pl.BlockSpec(fwd_b.shape, lambda c: (0, 0)),
        ],
        out_specs=pl.BlockSpec((None, n2, 2 * N1, LANES), lambda c: (c, 0, 0, 0)),
        out_shape=jax.ShapeDtypeStruct((nblk, n2, 2 * N1, LANES), F32),
        scratch_shapes=[pltpu.VMEM((n2 * 2 * N1, LANES), F32)],
        compiler_params=_params("arbitrary"),
        name="hyena_filter_fft",
    )(k2, ss, fwd_a, fwd_b)


TILE = 8
N1_GROUPS = N1 // TILE


def _tile_tables(seq):
    _, fwd_a, inv_a, _, _, n2, n2h = _dft_tables_np(seq)
    eye = np.eye(TILE)
    fa = fwd_a.reshape(N1_GROUPS, TILE, 2 * n2, n2h)
    wa = np.einsum("qjrn,jk->qrjnk", fa, eye).reshape(N1_GROUPS, 2 * n2 * TILE, n2h * TILE)
    ia = inv_a.reshape(N1_GROUPS, TILE, n2h, 2 * n2)
    vc = np.einsum("qjnr,jk->qnjrk", ia, eye).reshape(N1_GROUPS, n2h * TILE, 2 * n2 * TILE)
    return jnp.asarray(wa, BF16), jnp.asarray(vc, BF16)


def _conv_kernel(u_ref, g_ref, skip_ref, fb_ref, ib_ref, kh_hbm, wa_hbm, vc_hbm, o_ref,
                 y_ref, kh_ref, wa_ref, vc_ref, sem, *, n2, n2h, kh_first):
    c_id, b_id = pl.program_id(0), pl.program_id(1)

    @pl.when(jnp.logical_and(c_id == 0, b_id == 0))
    def _():
        for src, dst in ((wa_hbm, wa_ref), (vc_hbm, vc_ref)):
            cp = pltpu.make_async_copy(src, dst, sem)
            cp.start()
            cp.wait()

    @pl.when(b_id == 0)
    def _():
        cp = pltpu.make_async_copy(kh_hbm.at[kh_first + c_id], kh_ref, sem)
        cp.start()
        cp.wait()

    def y_tile(rf, base):
        ri, f2 = divmod(rf, n2)
        return pl.ds(f2 * 2 * N1 + ri * N1 + base, TILE)

    def stage_a(q, carry):
        base = pl.multiple_of(q * TILE, TILE)
        x = jnp.concatenate([u_ref[pl.ds(N1 * m + base, TILE), :] for m in range(n2h)], axis=0)
        r = _dot(wa_ref[q], x.astype(BF16))
        for rf in range(2 * n2):
            y_ref[y_tile(rf, base), :] = r[rf * TILE:(rf + 1) * TILE]
        return carry
    lax.fori_loop(0, N1_GROUPS, stage_a, 0, unroll=2)

    unr = min(UNROLL, n2)

    def mid(i, carry):
        trips = [i * unr + j for j in range(unr)]
        offs = [pl.multiple_of(f2 * 2 * N1, 2 * N1) for f2 in trips]
        zs = [_dot(fb_ref[...], y_ref[pl.ds(off, 2 * N1), :].astype(BF16)) for off in offs]
        ps = []
        for f2, z in zip(trips, zs):
            zr, zi = z[:N1], z[N1:]
            kh = kh_ref[f2]
            kr, ki = kh[:N1], kh[N1:]
            ps.append(jnp.concatenate([zr * kr - zi * ki, zr * ki + zi * kr], axis=0).astype(BF16))
        gs = [_dot(ib_ref[...], p) for p in ps]
        for off, g in zip(offs, gs):
            y_ref[pl.ds(off, 2 * N1), :] = g
        return carry
    lax.fori_loop(0, n2 // unr, mid, 0)

    skip = skip_ref[...]

    def stage_c(q, carry):
        base = pl.multiple_of(q * TILE, TILE)
        g = jnp.concatenate([y_ref[y_tile(rf, base), :] for rf in range(2 * n2)], axis=0)
        conv = _dot(vc_ref[q], g.astype(BF16))
        for m in range(n2h):
            rows = pl.ds(N1 * m + base, TILE)
            o_ref[rows, :] = g_ref[rows, :] * (conv[m * TILE:(m + 1) * TILE] + u_ref[rows, :] * skip)
        return carry
    lax.fori_loop(0, N1_GROUPS, stage_c, 0, unroll=2)


def _long_conv_gate(u, u_col, gate, gate_col, khat, skip, order):
    bsz, seq, _ = u.shape
    ch = D_HYENA
    _, _, _, fwd_b, inv_b, n2, n2h = _dft_tables(seq)
    wa, vc = _tile_tables(seq)
    nblk = ch // LANES
    const = lambda a: pl.BlockSpec(a.shape, lambda c, b: (0,) * a.ndim)
    at = lambda col: pl.BlockSpec((None, seq, LANES), lambda c, b: (b, 0, col // LANES + c))
    hbm = pl.BlockSpec(memory_space=pl.ANY)
    return pl.pallas_call(
        functools.partial(_conv_kernel, n2=n2, n2h=n2h, kh_first=order * nblk),
        grid=(nblk, bsz),
        in_specs=[
            at(u_col), at(gate_col),
            pl.BlockSpec((1, LANES), lambda c, b: (0, c)),
            const(fwd_b), const(inv_b), hbm, hbm, hbm,
        ],
        out_specs=at(0),
        out_shape=jax.ShapeDtypeStruct((bsz, seq, ch), F32),
        scratch_shapes=[pltpu.VMEM((n2 * 2 * N1, LANES), F32), pltpu.VMEM(khat.shape[1:], F32),
                        pltpu.VMEM(wa.shape, BF16), pltpu.VMEM(vc.shape, BF16), pltpu.SemaphoreType.DMA(())],
        compiler_params=_params("arbitrary", "arbitrary"),
        name=f"hyena_conv{order}",
    )(u, gate, skip[order].reshape(1, ch), fwd_b, inv_b, khat, wa, vc)


HALO = 8


def _shift_rows(p, k):
    return pltpu.roll(p, k % p.shape[0], axis=0)


def _proj_kernel(xp_ref, x_ref, xn_ref, mod_ref, g1_ref, why_ref, wrkv_ref, wlora_ref,
                 cw_ref, cb_ref, murkv_ref, mulora_ref, w0_ref, a0_ref, wwah_ref, wwal_ref, gup_ref,
                 kk_ref, ka_ref, rk_ref, ones_ref,
                 uhy_ref, rkvk_ref, lwa_ref, g_ref, bonus_ref,
                 *, tt, nt):
    i = pl.program_id(1)
    xe = jnp.concatenate([xp_ref[...], x_ref[...], xn_ref[...]], axis=0)
    ms = jnp.mean(xe * xe, axis=-1, keepdims=True)
    h = xe * lax.rsqrt(ms + NORM_EPS) * g1_ref[...]
    h = h * (1.0 + mod_ref[1:2, :]) + mod_ref[0:1, :]
    row = lax.broadcasted_iota(jnp.int32, (tt + 2 * HALO, 1), 0)
    inside = jnp.logical_and(jnp.logical_or(row >= HALO, i > 0),
                             jnp.logical_or(row < tt + HALO, i < nt - 1))
    hb = jnp.where(inside, h, 0.0).astype(BF16)
    mid = slice(HALO, tt + HALO)

    p = _dot(hb, why_ref[...])
    u = (_shift_rows(p, 1) * cw_ref[0:1, :] + p * cw_ref[1:2, :]
         + _shift_rows(p, -1) * cw_ref[2:3, :] + cb_ref[...])
    uhy_ref[...] = u[mid]

    p = _dot(hb, wrkv_ref[...])
    p = p + murkv_ref[...] * (0.5 * (_shift_rows(p, 1) + _shift_rows(p, -1)) - p)
    p = p[mid]
    c = D_RWKV
    r, k, v = p[:, :c], p[:, c:2 * c], p[:, 2 * c:]
    rkvk_ref[:, :3 * c] = p

    q = _dot(hb, wlora_ref[...])
    q = q + mulora_ref[...] * (0.5 * (_shift_rows(q, 1) + _shift_rows(q, -1)) - q)
    q = q[mid]
    wa = q[:, :LANES]
    lane = lax.broadcasted_iota(jnp.int32, wa.shape, 1)
    wa = jnp.where(lane < 2 * DECAY_LORA, jnp.tanh(wa), wa)
    wah, wal = _split2(wa)
    up = _dot(wah, wwah_ref[...]) + (_dot(wah, wwal_ref[...]) + _dot(wal, wwah_ref[...]))
    lw = -math.exp(-0.5) * jax.nn.sigmoid(w0_ref[...] + up[:, :2 * c])
    a = jax.nn.sigmoid(a0_ref[...] + up[:, 2 * c:])
    for dd in range(2):
        lwa_ref[:, 2 * dd * c:(2 * dd + 1) * c] = lw[:, dd * c:(dd + 1) * c]
        lwa_ref[:, (2 * dd + 1) * c:(2 * dd + 2) * c] = a[:, dd * c:(dd + 1) * c]
    g_ref[...] = _dot3(jax.nn.sigmoid(q[:, LANES:]), gup_ref[...])

    ones = ones_ref[...]
    kk = k * kk_ref[...]
    nrm = jnp.sqrt(_dot_exact_rhs(kk * kk, ones))
    rkvk_ref[:, 3 * c:] = kk / jnp.maximum(nrm, 1e-12)
    ka = ka_ref[...]
    ksum = k * (2.0 + (a[:, :c] + a[:, c:] - 2.0) * ka)
    bonus_ref[...] = _dot_exact_rhs(r * ksum * rk_ref[...], ones) * v


def _head_ones():
    hid = np.arange(D_RWKV) // HEAD
    return jnp.asarray(hid[:, None] == hid[None, :], BF16)


def _projection(x, mod, norm1_g, w_in, hy_conv_w, hy_conv_b, rw_mu, rw_w0, rw_w_up, rw_a0,
                rw_a_up, rw_g_up, rw_k_k, rw_k_a, rw_r_k, tt=512):
    bsz, seq, d = x.shape
    tt = min(tt, seq)
    nt = seq // tt
    c = D_RWKV
    hy = (HYENA_ORDER + 1) * D_HYENA
    nlora = 2 * LANES
    w_hy = w_in[:, :hy].astype(BF16)
    w_rkv = w_in[:, hy:hy + 3 * c].astype(BF16)
    w_lora = jnp.zeros((d, nlora), F32).at[:, :w_in.shape[1] - hy - 3 * c].set(w_in[:, hy + 3 * c:]).astype(BF16)
    mu_rkv = rw_mu[:3 * c].reshape(1, 3 * c)
    mu_lora = jnp.zeros((1, nlora), F32).at[0, :rw_mu.shape[0] - 3 * c].set(rw_mu[3 * c:])
    wwa = jnp.zeros((LANES, 4 * c), F32)
    for dd in range(2):
        wwa = wwa.at[dd * DECAY_LORA:(dd + 1) * DECAY_LORA, dd * c:(dd + 1) * c].set(rw_w_up[dd])
        wwa = wwa.at[2 * DECAY_LORA + dd * ICLR_LORA:2 * DECAY_LORA + (dd + 1) * ICLR_LORA,
                     2 * c + dd * c:2 * c + (dd + 1) * c].set(rw_a_up[dd])
    gup = jnp.zeros((LANES, c), F32).at[:GATE_LORA].set(rw_g_up)
    row = lambda a: a.reshape(1, -1)

    nb8 = seq // HALO
    tb = tt // HALO
    const = lambda a: pl.BlockSpec(a.shape, lambda b, i: (0,) * a.ndim, pipeline_mode=pl.Buffered(1))
    tile = lambda w: pl.BlockSpec((None, tt, w), lambda b, i: (b, i, 0))
    ins = [
        (x, pl.BlockSpec((None, HALO, d), lambda b, i: (b, jnp.maximum(i * tb - 1, 0), 0))),
        (x, pl.BlockSpec((None, tt, d), lambda b, i: (b, i, 0))),
        (x, pl.BlockSpec((None, HALO, d), lambda b, i: (b, jnp.minimum((i + 1) * tb, nb8 - 1), 0))),
        (mod, pl.BlockSpec((None,) + mod.shape[1:], lambda b, i: (b, 0, 0))),
    ]
    consts = [row(norm1_g), w_hy, w_rkv, w_lora, hy_conv_w, row(hy_conv_b), mu_rkv, mu_lora,
              row(rw_w0), row(rw_a0), *_split2(wwa), gup, row(rw_k_k), row(rw_k_a), row(rw_r_k), _head_ones()]
    ins += [(a, const(a)) for a in consts]
    widths = [hy, 4 * c, 4 * c, c, c]
    return pl.pallas_call(
        functools.partial(_proj_kernel, tt=tt, nt=nt),
        grid=(bsz, nt),
        in_specs=[s for _, s in ins],
        out_specs=[tile(w) for w in widths],
        out_shape=[jax.ShapeDtypeStruct((bsz, seq, w), F32) for w in widths],
        compiler_params=_params("arbitrary", "arbitrary"),
        name="input_projection",
    )(*[a for a, _ in ins])


CHUNK = HEAD
GROUP = MXU_DIM // HEAD


def _nt(a, b):
    return lax.dot_general(a, b, (((1,), (1,)), ((), ())), preferred_element_type=F32)


def _tn(a, b):
    return lax.dot_general(a, b, (((0,), (0,)), ((), ())), preferred_element_type=F32)


def _wkv_direction(r, k, v, kk, lw, a, ka, s_ref, reverse):
    c = CHUNK
    ti = lax.broadcasted_iota(jnp.int32, (c, c), 0)
    si = lax.broadcasted_iota(jnp.int32, (c, c), 1)
    tri = (si >= ti) if reverse else (si <= ti)
    cum = _dot_exact_rhs_lhs(jnp.where(tri, 1.0, 0.0).astype(BF16), lw)
    tot = jnp.sum(lw, axis=0, keepdims=True)
    w_incl = jnp.exp(cum)
    w_prev = jnp.exp(cum - lw)
    w_inv = jnp.exp(-cum)
    w_end = jnp.exp(tot - cum)
    w_tot = jnp.exp(tot)
    kd = k * (1.0 + (a - 1.0) * ka)
    b = kk * a
    a_w = -kk * w_prev
    r_w = r * w_incl
    b_w = b * w_inv
    k_w = kd * w_inv
    b_e = b * w_end
    k_e = kd * w_end

    m = MXU_DIM
    ri = lax.broadcasted_iota(jnp.int32, (m, m), 0)
    ci = lax.broadcasted_iota(jnp.int32, (m, m), 1)
    head_mask = (ri // HEAD) == (ci // HEAD)
    tl = lax.broadcasted_iota(jnp.int32, (c, m), 0)
    sl = lax.broadcasted_iota(jnp.int32, (c, m), 1) % c
    strict = (sl > tl) if reverse else (sl < tl)
    incl = (sl >= tl) if reverse else (sl <= tl)
    eye = jnp.where(sl == tl, 1.0, 0.0)
    both = lambda top, bot: jnp.concatenate([top, bot], axis=0)

    def stack(xg):
        xb = xg.astype(BF16)
        return jnp.where(head_mask, jnp.concatenate([xb] * GROUP, axis=0), jnp.zeros((), BF16))

    streams = []
    for g in range(D_RWKV // m):
        sl_g = slice(g * m, (g + 1) * m)
        streams.append(dict(
            ar=both(a_w[:, sl_g], r_w[:, sl_g]).astype(BF16),
            b_st=stack(b_w[:, sl_g]), k_st=stack(k_w[:, sl_g]), v_st=stack(v[:, sl_g]),
            v=v[:, sl_g], bk=both(b_e[:, sl_g], k_e[:, sl_g]).astype(BF16),
            w_tot=w_tot[:, sl_g], s_ref=s_ref.at[g],
            strict=strict, incl=incl, eye=eye, head_mask=head_mask, stack=stack))
    return streams


def _wkv_streams_step(streams):
    c = CHUNK
    both = lambda top, bot: jnp.concatenate([top, bot], axis=0)
    for st in streams:
        st["s"] = st["s_ref"][...]
        st["xb"] = _nt(st["ar"], st["b_st"])
        st["xk"] = _nt(st["ar"], st["k_st"])
        st["xs"] = _nt(st["ar"], st["s"].astype(BF16))
    for st in streams:
        m_k = both(jnp.where(st["strict"], st["xk"][:c], 0.0), jnp.where(st["incl"], st["xk"][c:], 0.0))
        st["kv"] = _dot(m_k.astype(BF16), st["v_st"])
        st["rhs"] = st["xs"][:c] + st["kv"][:c]
        st["pw"] = jnp.where(st["strict"], st["xb"][:c], 0.0)
        st["t"] = st["eye"] + st["pw"]
        st["p_st"] = st["stack"](st["pw"])
    levels = int(math.log2(c)) - 1
    for st in streams:
        st["pw"] = _dot(st["pw"].astype(BF16), st["p_st"])
        st["p_st"] = st["stack"](st["pw"])
    for lvl in range(1, levels + 1):
        for st in streams:
            if lvl < levels:
                prod = _dot(both(st["pw"], st["t"]).astype(BF16), st["p_st"])
                st["pw"] = prod[:c]
                st["t"] = st["t"] + prod[c:]
                st["p_st"] = st["stack"](st["pw"])
            else:
                st["t"] = st["t"] + _dot(st["t"].astype(BF16), st["p_st"])
    for st in streams:
        st["u"] = _dot(st["t"].astype(BF16), st["stack"](st["rhs"]))
    outs = []
    for st in streams:
        m_rb = jnp.where(st["incl"], st["xb"][c:], 0.0)
        outs.append(st["xs"][c:] + _dot(m_rb.astype(BF16), st["stack"](st["u"])) + st["kv"][c:])
        uv = both(st["u"], st["v"]).astype(BF16)
        st["s_ref"][...] = st["s"] * st["w_tot"] + jnp.where(st["head_mask"], _tn(uv, st["bk"]), 0.0)
    return outs


def _dot_exact_rhs_lhs(tri_bf16, x):
    xh, xl = _split2(x)
    return _dot(tri_bf16, xh) + _dot(tri_bf16, xl)


def _wkv_kernel(rkvk_f, lwa_f, rkvk_b, lwa_b, ka_ref, of_ref, ob_ref, s_ref, *, nch):
    @pl.when(pl.program_id(1) == 0)
    def _():
        s_ref[...] = jnp.zeros_like(s_ref)

    ka = ka_ref[...]
    c = D_RWKV

    def operands(rkvk_ref, lwa_ref, rows):
        x = rkvk_ref[rows, :]
        la = lwa_ref[rows, :]
        return x[:, :c], x[:, c:2 * c], x[:, 2 * c:3 * c], x[:, 3 * c:], la[:, :c], la[:, c:]

    for ci in range(nch):
        rows_f = slice(ci * CHUNK, (ci + 1) * CHUNK)
        rows_b = slice((nch - 1 - ci) * CHUNK, (nch - ci) * CHUNK)
        fwd = _wkv_direction(*operands(rkvk_f, lwa_f, rows_f), ka, s_ref.at[0], False)
        bwd = _wkv_direction(*operands(rkvk_b, lwa_b, rows_b), ka, s_ref.at[1], True)
        outs = _wkv_streams_step(fwd + bwd)
        of_ref[rows_f, :] = jnp.concatenate(outs[:len(fwd)], axis=1)
        ob_ref[rows_b, :] = jnp.concatenate(outs[len(fwd):], axis=1)


WKV_CHUNKS_PER_STEP = 4


def _wkv(rkvk, lwa, rw_k_a):
    bsz, seq, _ = rkvk.shape
    c = D_RWKV
    nch = WKV_CHUNKS_PER_STEP if seq % (WKV_CHUNKS_PER_STEP * CHUNK) == 0 else 1
    rows = nch * CHUNK
    nb = seq // rows
    fwd = lambda w, lane_blk: pl.BlockSpec((None, rows, w), lambda b, j: (b, j, lane_blk))
    bwd = lambda w, lane_blk: pl.BlockSpec((None, rows, w), lambda b, j: (b, nb - 1 - j, lane_blk))
    return pl.pallas_call(
        functools.partial(_wkv_kernel, nch=nch),
        grid=(bsz, nb),
        in_specs=[fwd(4 * c, 0), fwd(2 * c, 0), bwd(4 * c, 0), bwd(2 * c, 1),
                  pl.BlockSpec((1, c), lambda b, j: (0, 0))],
        out_specs=[fwd(c, 0), bwd(c, 0)],
        out_shape=[jax.ShapeDtypeStruct((bsz, seq, c), F32)] * 2,
        scratch_shapes=[pltpu.VMEM((2, c // MXU_DIM, MXU_DIM, MXU_DIM), F32)],
        compiler_params=_params("arbitrary", "arbitrary"),
        name="wkv7_chunked",
    )(rkvk, lwa, rkvk, lwa, rw_k_a.reshape(1, c))


NEG_INF = float("-inf")


def _first_max(vals, idx, size):
    m = jnp.max(vals, axis=0, keepdims=True)
    i = jnp.min(jnp.where(vals == m, idx, size), axis=0, keepdims=True)
    return m, i


def _route(scores, biased):
    e, tt = scores.shape
    per = e // N_GROUPS
    rowl = lax.broadcasted_iota(jnp.int32, (per, tt), 0)
    gs = []
    for g in range(N_GROUPS):
        blk = biased[g * per:(g + 1) * per]
        m1, i1 = _first_max(blk, rowl, per)
        m2 = jnp.max(jnp.where(rowl == i1, NEG_INF, blk), axis=0, keepdims=True)
        gs.append(m1 + m2)
    cur = jnp.concatenate(gs, axis=0)
    growl = lax.broadcasted_iota(jnp.int32, (N_GROUPS, tt), 0)
    gsel = jnp.zeros((N_GROUPS, tt), F32)
    for _ in range(TOPK_GROUPS):
        _, ig = _first_max(cur, growl, N_GROUPS)
        hit = growl == ig
        gsel = jnp.where(hit, 1.0, gsel)
        cur = jnp.where(hit, NEG_INF, cur)
    emask = jnp.concatenate([jnp.broadcast_to(gsel[g:g + 1], (per, tt)) for g in range(N_GROUPS)], axis=0)
    masked = jnp.where(emask > 0.5, biased, NEG_INF)
    row = lax.broadcasted_iota(jnp.int32, (e, tt), 0)
    ids, ws = [], []
    for _ in range(TOP_K):
        _, ie = _first_max(masked, row, e)
        hit = row == ie
        ids.append(ie)
        ws.append(jnp.sum(jnp.where(hit, scores, 0.0), axis=0, keepdims=True))
        masked = jnp.where(hit, NEG_INF, masked)
    w = jnp.concatenate(ws, axis=0)
    w = w / jnp.sum(w, axis=0, keepdims=True) * ROUTE_SCALE
    return jnp.concatenate(ids, axis=0), w


def _mixout_kernel(x_ref, mod_ref, yhy_ref, of_ref, ob_ref, g_ref, bonus_ref, lnw_ref, lnb_ref,
                   ones_ref, wout_ref, g2n_ref, rwth_ref, rwtl_ref, bias_ref,
                   x1_ref, h2a_ref, h2b_ref, eid_ref, wsel_ref):
    ones = ones_ref[...]
    s = of_ref[...] + ob_ref[...]
    mean = _dot_exact_rhs(s, ones) * (1.0 / HEAD)
    dlt = s - mean
    var = _dot_exact_rhs(dlt * dlt, ones) * (1.0 / HEAD)
    sn = dlt * lax.rsqrt(var + GN_EPS) * lnw_ref[...] + lnb_ref[...]
    yrw = (sn + bonus_ref[...]) * g_ref[...]
    ch = yhy_ref.shape[-1]
    mix = _dot(yhy_ref[...].astype(BF16), wout_ref[:ch, :]) + _dot(yrw.astype(BF16), wout_ref[ch:, :])
    x1 = x_ref[...] + mod_ref[2:3, :] * mix
    x1_ref[...] = x1
    ms = jnp.mean(x1 * x1, axis=-1, keepdims=True)
    h2 = x1 * lax.rsqrt(ms + NORM_EPS) * g2n_ref[...]
    h2 = h2 * (1.0 + mod_ref[4:5, :]) + mod_ref[3:4, :]
    h2a_ref[...], h2b_ref[...] = _pack_rows(h2)
    rh, rl = rwth_ref[...], rwtl_ref[...]
    hh, hl = _split2(h2)
    logits = _nt(rh, hh) + (_nt(rh, hl) + _nt(rl, hh))
    scores = jax.nn.sigmoid(logits)
    ids, w = _route(scores, scores + bias_ref[...])
    eid_ref[...] = ids
    wsel_ref[...] = w


def _mix_out(x, mod, yhy, o_f, o_b, g, bonus, ln_w, ln_b, w_out, norm2_g, router_w, router_bias, tt=512):
    bsz, seq, d = x.shape
    tt = min(tt, seq)
    nt = seq // tt
    n = bsz * seq
    c = D_RWKV
    e = router_w.shape[1]
    row = lambda a: a.reshape(1, -1)
    consts = [row(ln_w), row(ln_b), _head_ones(), w_out.astype(BF16), row(norm2_g), *_split2(router_w.T),
              jnp.broadcast_to(router_bias.reshape(e, 1), (e, tt))]
    const = lambda a: pl.BlockSpec(a.shape, lambda b, i: (0,) * a.ndim, pipeline_mode=pl.Buffered(1))
    tile = lambda w: pl.BlockSpec((None, tt, w), lambda b, i: (b, i, 0))
    flat = lambda rows, dt: jax.ShapeDtypeStruct((rows, n), dt)
    return pl.pallas_call(
        _mixout_kernel,
        grid=(bsz, nt),
        in_specs=[tile(d), pl.BlockSpec((None,) + mod.shape[1:], lambda b, i: (b, 0, 0))]
        + [tile(c)] * 5 + [const(a) for a in consts],
        out_specs=[tile(d), pl.BlockSpec((tt, d // 4), lambda b, i: (b * nt + i, 0)),
                   pl.BlockSpec((tt, d // 4), lambda b, i: (b * nt + i, 0)),
                   pl.BlockSpec((TOP_K, tt), lambda b, i: (0, b * nt + i)),
                   pl.BlockSpec((TOP_K, tt), lambda b, i: (0, b * nt + i))],
        out_shape=[jax.ShapeDtypeStruct((bsz, seq, d), F32), jax.ShapeDtypeStruct((n, d // 4), U32),
                   jax.ShapeDtypeStruct((n, d // 4), U32),
                   flat(TOP_K, jnp.int32), flat(TOP_K, F32)],
        compiler_params=_params("arbitrary", "arbitrary"),
        name="mix_out_router",
    )(x, mod, yhy, o_f, o_b, g, bonus, *consts)


BLK = 512
BLK_SHIFT = 9


def _multi_hot(eid, e):
    row = lax.broadcasted_iota(jnp.int32, (e, eid.shape[1]), 0)
    mh = jnp.zeros((e, eid.shape[1]), F32)
    for kk in range(TOP_K):
        mh = mh + jnp.where(row == eid[kk:kk + 1, :], 1.0, 0.0)
    return row, mh


def _lookup(row, eid, table):
    return jnp.concatenate(
        [jnp.sum(jnp.where(row == eid[kk:kk + 1, :], table, 0.0), axis=0, keepdims=True)
         for kk in range(TOP_K)], axis=0)


def _rank_kernel(eid_ref, rank_ref, cnt_ref, *, e):
    @pl.when(pl.program_id(0) == 0)
    def _():
        cnt_ref[...] = jnp.zeros_like(cnt_ref)

    eid = eid_ref[...]
    tt = eid.shape[1]
    row, mh = _multi_hot(eid, e)
    mhb = mh.astype(BF16)
    si = lax.broadcasted_iota(jnp.int32, (tt, tt), 0)
    ti = lax.broadcasted_iota(jnp.int32, (tt, tt), 1)
    earlier = _dot(mhb, jnp.where(si < ti, 1.0, 0.0).astype(BF16))
    cnt = cnt_ref[...]
    full = earlier + jnp.concatenate([cnt] * (tt // LANES), axis=1)
    rank_ref[...] = _lookup(row, eid, full).astype(jnp.int32)
    cnt_ref[...] = cnt + _dot(mhb, jnp.ones((tt, LANES), BF16))


def _expert_ranks(eid, e, tt=512):
    n = eid.shape[1]
    tt = min(tt, n)
    return pl.pallas_call(
        functools.partial(_rank_kernel, e=e),
        grid=(n // tt,),
        in_specs=[pl.BlockSpec((TOP_K, tt), lambda i: (0, i))],
        out_specs=[pl.BlockSpec((TOP_K, tt), lambda i: (0, i)),
                   pl.BlockSpec((e, LANES), lambda i: (0, 0))],
        out_shape=[jax.ShapeDtypeStruct((TOP_K, n), jnp.int32), jax.ShapeDtypeStruct((e, LANES), F32)],
        compiler_params=_params("arbitrary"),
        name="expert_ranks",
    )(eid)


def _block_offsets(cnt):
    e = cnt.shape[0]
    nblk = ((cnt.astype(jnp.int32) + (BLK - 1)) >> BLK_SHIFT).astype(F32)
    ri = lax.broadcasted_iota(jnp.int32, (e, e), 0)
    ci = lax.broadcasted_iota(jnp.int32, (e, e), 1)
    tril = jnp.where(ci <= ri, 1.0, 0.0).astype(BF16)
    nh, nl = _split2(nblk)
    return nblk, _dot(tril, nh) + _dot(tril, nl)


def _dest_kernel(cnt_ref, eid_ref, rank_ref, dest_ref):
    nblk, end = _block_offsets(cnt_ref[...])
    off = (end - nblk) * float(BLK)
    eid = eid_ref[...]
    tt = eid.shape[1]
    row = lax.broadcasted_iota(jnp.int32, (off.shape[0], tt), 0)
    table = jnp.concatenate([off] * (tt // LANES), axis=1)
    dest_ref[...] = _lookup(row, eid, table).astype(jnp.int32) + rank_ref[...]


def _destinations(cnt, eid, rank, tt=512):
    n = eid.shape[1]
    tt = min(tt, n)
    blk = pl.BlockSpec((TOP_K, tt), lambda i: (0, i))
    return pl.pallas_call(
        _dest_kernel,
        grid=(n // tt,),
        in_specs=[pl.BlockSpec(cnt.shape, lambda i: (0, 0)), blk, blk],
        out_specs=blk,
        out_shape=jax.ShapeDtypeStruct((TOP_K, n), jnp.int32),
        compiler_params=_params("arbitrary"),
        name="expert_destinations",
    )(cnt, eid, rank)


def _meta_kernel(cnt_ref, meta_ref, *, nbp):
    cnt = cnt_ref[...]
    e = cnt.shape[0]
    nblk, end = _block_offsets(cnt)
    rep = lambda a, w: jnp.concatenate([a] * (w // LANES), axis=1)
    b = lax.broadcasted_iota(jnp.int32, (e, nbp), 1).astype(F32)
    blk_e = jnp.minimum(jnp.sum(jnp.where(rep(end, nbp) <= b, 1.0, 0.0), axis=0, keepdims=True), float(e - 1))
    row = lax.broadcasted_iota(jnp.int32, (e, nbp), 0).astype(F32)
    mine = row == blk_e
    left = rep(cnt + (end - nblk) * float(BLK), nbp) - b * float(BLK)
    nvalid = jnp.clip(jnp.sum(jnp.where(mine, left, 0.0), axis=0, keepdims=True), 0.0, float(BLK))
    nused = jnp.max(rep(end, nbp), axis=0, keepdims=True)
    later = jnp.logical_and(row > blk_e, rep(nblk, nbp) > 0.0)
    nxt = jnp.min(jnp.where(later, row, float(e)), axis=0, keepdims=True)
    nxt = jnp.where(nxt >= float(e), -1.0, nxt)
    meta_ref[...] = jnp.concatenate([blk_e, nvalid, nused, nxt, jnp.zeros((4, nbp), F32)],
                                    axis=0).astype(jnp.int32)


def _block_meta(cnt, nb):
    nbp = -(-nb // LANES) * LANES
    return pl.pallas_call(
        functools.partial(_meta_kernel, nbp=nbp),
        out_shape=jax.ShapeDtypeStruct((8, nbp), jnp.int32),
        compiler_params=pltpu.CompilerParams(vmem_limit_bytes=VMEM_LIMIT),
        name="expert_block_meta",
    )(cnt)


SC_WINDOW = 128


def _sc_mesh():
    return plsc.VectorSubcoreMesh(core_axis_name="core", subcore_axis_name="subcore")


def _sc_scatter_rows(rows, idx, nrows):
    n, width = rows.shape

    @pl.kernel(out_type=jax.ShapeDtypeStruct((nrows, width), rows.dtype), mesh=_sc_mesh())
    def scatter(rows_hbm, idx_hbm, out_hbm):
        def body(rows_vmem, idx_vmem):
            pltpu.sync_copy(rows_vmem, out_hbm.at[idx_vmem.at[0]])

        pltpu.emit_pipeline(
            body,
            grid=(n // SC_WINDOW, idx.shape[0]),
            in_specs=[pl.BlockSpec((SC_WINDOW, width), index_map=lambda i, k: (i, 0)),
                      pl.BlockSpec((1, SC_WINDOW), index_map=lambda i, k: (k, i))],
            out_specs=[],
            core_axis_name=("core", "subcore"),
            dimension_semantics=(pltpu.PARALLEL, pltpu.ARBITRARY),
        )(rows_hbm, idx_hbm)

    return scatter(rows, idx)


def _sc_gather_rows(src, idx):
    num = idx.shape[1]
    width = src.shape[1]

    @pl.kernel(out_type=jax.ShapeDtypeStruct((num, width), src.dtype), mesh=_sc_mesh())
    def gather(src_hbm, idx_hbm, out_hbm):
        def body(idx_vmem, out_vmem):
            pltpu.sync_copy(src_hbm.at[idx_vmem.at[0]], out_vmem)

        pltpu.emit_pipeline(
            body,
            grid=(num // SC_WINDOW,),
            in_specs=[pl.BlockSpec((1, SC_WINDOW), index_map=lambda i: (0, i))],
            out_specs=[pl.BlockSpec((SC_WINDOW, width), index_map=lambda i: (i, 0))],
            core_axis_name=("core", "subcore"),
            dimension_semantics=(pltpu.PARALLEL,),
        )(idx_hbm, out_hbm)

    return gather(src, idx)


def _experts_kernel(be_ref, nv_ref, nu_ref, nxt_ref, xa_ref, xb_ref, wg_hbm, wu_hbm, wd_hbm, oa_ref, ob_ref,
                    wgf, wuf, wdf, wgb, wub, wdb, sems, slot_ref):
    b = pl.program_id(0)
    used = b < nu_ref[0]

    def fetch(expert, slot):
        return [pltpu.make_async_copy(src.at[expert], dst.at[slot], sems.at[slot])
                for src, dst in ((wg_hbm, wgf), (wu_hbm, wuf), (wd_hbm, wdf))]

    @pl.when(b == 0)
    def _():
        slot_ref[0] = 0
        for cp in fetch(be_ref[0], 0):
            cp.start()

    @pl.when(used)
    def _():
        prev = be_ref[jnp.maximum(b - 1, 0)]

        @pl.when(jnp.logical_or(b == 0, be_ref[b] != prev))
        def _():
            slot = slot_ref[0]
            for cp in fetch(be_ref[b], slot):
                cp.wait()

            @pl.when(nxt_ref[b] >= 0)
            def _():
                for cp in fetch(nxt_ref[b], 1 - slot):
                    cp.start()

            wgb[...] = wgf[slot].astype(BF16)
            wub[...] = wuf[slot].astype(BF16)
            wdb[...] = wdf[slot].astype(BF16)
            slot_ref[0] = 1 - slot

        valid = lax.broadcasted_iota(jnp.int32, (xa_ref.shape[0], 1), 0) < nv_ref[b]
        zero = jnp.zeros((), U32)
        x = _unpack_rows(jnp.where(valid, xa_ref[...], zero), jnp.where(valid, xb_ref[...], zero))
        act = _silu(_dot(x, wgb[...])) * _dot(x, wub[...])
        oa_ref[...], ob_ref[...] = _pack_rows(_dot(act.astype(BF16), wdb[...]))


def _experts(blk_e, nvalid, nused, nxt_e, xs_a, xs_b, wg, wu, wd):
    p, dq = xs_a.shape
    nb = p // BLK
    d, de = wg.shape[1], wg.shape[2]
    rows_in = pl.BlockSpec((BLK, dq), lambda b, be, nv, nu, nx: (jnp.minimum(b, nu[0] - 1), 0))
    hbm = pl.BlockSpec(memory_space=pl.ANY)
    return pl.pallas_call(
        _experts_kernel,
        grid_spec=pltpu.PrefetchScalarGridSpec(
            num_scalar_prefetch=4,
            grid=(nb,),
            in_specs=[rows_in, rows_in, hbm, hbm, hbm],
            out_specs=[rows_in, rows_in],
            scratch_shapes=[pltpu.VMEM((2, d, de), F32), pltpu.VMEM((2, d, de), F32), pltpu.VMEM((2, de, d), F32),
                            pltpu.VMEM((d, de), BF16), pltpu.VMEM((d, de), BF16), pltpu.VMEM((de, d), BF16),
                            pltpu.SemaphoreType.DMA((2,)), pltpu.SMEM((1,), jnp.int32)],
        ),
        out_shape=[jax.ShapeDtypeStruct((p, dq), U32)] * 2,
        compiler_params=_params("arbitrary"),
        name="moe_experts",
    )(blk_e, nvalid, nused, nxt_e, xs_a, xs_b, wg, wu, wd)


def _shared_kernel(ha_ref, hb_ref, sg_ref, su_ref, sd_ref, o_ref):
    hb = _unpack_rows(ha_ref[...], hb_ref[...])
    act = _silu(_dot(hb, sg_ref[...])) * _dot(hb, su_ref[...])
    o_ref[...] = _dot(act.astype(BF16), sd_ref[...]).astype(o_ref.dtype)


def _shared_expert(h2a, h2b, sh_wg, sh_wu, sh_wd, tt=512):
    n, dp = h2a.shape
    d = sh_wg.shape[0]
    tt = min(tt, n)
    consts = [sh_wg.astype(BF16), sh_wu.astype(BF16), sh_wd.astype(BF16)]
    packed_rows = pl.BlockSpec((tt, dp), lambda i: (i, 0))
    return pl.pallas_call(
        _shared_kernel,
        grid=(n // tt,),
        in_specs=[packed_rows, packed_rows] + [pl.BlockSpec(a.shape, lambda i: (0, 0)) for a in consts],
        out_specs=pl.BlockSpec((tt, d), lambda i: (i, 0)),
        out_shape=jax.ShapeDtypeStruct((n, d), BF16),
        compiler_params=_params("arbitrary"),
        name="shared_expert",
    )(h2a, h2b, *consts)


def _combine_kernel(w_ref, x1_ref, sh_ref, mod_ref, ga_ref, gb_ref, gf_ref, sel_ref, o_ref):
    ffn = sh_ref[...].astype(F32)
    wh, wl = _split2(w_ref[...])
    acc = None
    for kk in range(TOP_K):
        sel = sel_ref[kk]
        wk = _tn(wh, sel) + _tn(wl, sel)
        a_lo, a_hi = _unpack_halves(ga_ref[kk])
        b_lo, b_hi = _unpack_halves(gb_ref[kk])
        parts = [a_lo * wk, b_lo * wk, a_hi * wk, b_hi * wk]
        acc = parts if acc is None else [p + q for p, q in zip(acc, parts)]
    ffn = ffn + jnp.concatenate(acc, axis=1)
    xo = x1_ref[...] + mod_ref[5:6, :] * ffn
    ms = jnp.mean(xo * xo, axis=-1, keepdims=True)
    o_ref[...] = xo * lax.rsqrt(ms + NORM_EPS) * gf_ref[...]


def _combine(wsel, x1, shared, mod, ga, gb, normf_g, tok0, seq, tt=256):
    n, d = x1.shape
    part = ga.shape[1]
    tt = min(tt, seq, part)
    per = seq // tt
    off = tok0 // tt
    dq = ga.shape[2]
    sel = jnp.asarray(np.broadcast_to(np.eye(TOP_K)[:, :, None], (TOP_K, TOP_K, dq)), BF16)
    consts = [normf_g.reshape(1, d), sel]
    const = lambda a: pl.BlockSpec(a.shape, lambda i: (0,) * a.ndim)
    rows = pl.BlockSpec((tt, d), lambda i: (off + i, 0))
    gathered = pl.BlockSpec((TOP_K, tt, dq), lambda i: (0, i, 0))
    return pl.pallas_call(
        _combine_kernel,
        grid=(part // tt,),
        in_specs=[pl.BlockSpec((TOP_K, tt), lambda i: (0, off + i)),
                  rows, rows,
                  pl.BlockSpec((None,) + mod.shape[1:], lambda i: ((off + i) // per, 0, 0)),
                  gathered, gathered] + [const(a) for a in consts],
        out_specs=rows,
        out_shape=jax.ShapeDtypeStruct((n, d), F32),
        input_output_aliases={1: 0},
        compiler_params=_params("arbitrary"),
        name="moe_combine",
    )(wsel, x1, shared, mod, ga, gb, *consts)


COMBINE_PARTS = 4


def _moe(x1, h2a, h2b, mod, eid, wsel, exp_wg, exp_wu, exp_wd, sh_wg, sh_wu, sh_wd, normf_g):
    n = h2a.shape[0]
    e = exp_wg.shape[0]
    nb = (n * TOP_K + e * (BLK - 1)) // BLK
    rank, cnt = _expert_ranks(eid, e)
    dest = _destinations(cnt, eid, rank)
    meta = _block_meta(cnt, nb)
    xs_a = _sc_scatter_rows(h2a, dest, nb * BLK)
    xs_b = _sc_scatter_rows(h2b, dest, nb * BLK)
    shared = _shared_expert(h2a, h2b, sh_wg, sh_wu, sh_wd)
    ys_a, ys_b = _experts(meta[0, :nb], meta[1, :nb], meta[2, :1], meta[3, :nb], xs_a, xs_b,
                          exp_wg, exp_wu, exp_wd)
    bsz, seq, d = x1.shape
    out = x1.reshape(n, d)
    part = n // COMBINE_PARTS
    for j in range(COMBINE_PARTS):
        idx = dest[:, j * part:(j + 1) * part].reshape(1, TOP_K * part)
        ga = _sc_gather_rows(ys_a, idx).reshape(TOP_K, part, -1)
        gb = _sc_gather_rows(ys_b, idx).reshape(TOP_K, part, -1)
        out = _combine(wsel, out, shared, mod, ga, gb, normf_g, j * part, seq)
    return out


def kernel(x, c, norm1_g, norm2_g, normf_g, w_ada, b_ada, w_in, w_out, hy_conv_w, hy_conv_b, hy_pos_w1, hy_pos_b1, hy_pos_w2, hy_pos_b2, hy_pos_w3, hy_sin_freq, hy_skip, rw_mu, rw_w0, rw_w_up, rw_a0, rw_a_up, rw_g_up, rw_k_k, rw_k_a, rw_r_k, rw_ln_w, rw_ln_b, router_w, router_bias, exp_w_gate, exp_w_up, exp_w_down, sh_w_gate, sh_w_up, sh_w_down):
    bsz, seq, d = x.shape
    depth = w_ada.shape[0]
    assert depth == 1, "the final norm is fused into the last kernel of a single layer"
    for l in range(depth):
        mod = _modulation(c, w_ada[l], b_ada[l]).reshape(bsz, -1, d)
        uhy, rkvk, lwa, g, bonus = _projection(
            x, mod, norm1_g[l], w_in[l], hy_conv_w[l], hy_conv_b[l], rw_mu[l], rw_w0[l], rw_w_up[l],
            rw_a0[l], rw_a_up[l], rw_g_up[l], rw_k_k[l], rw_k_a[l], rw_r_k[l])
        k2, ss = _hyena_filters(seq, hy_pos_w1[l], hy_pos_b1[l], hy_pos_w2[l], hy_pos_b2[l],
                                hy_pos_w3[l], hy_sin_freq[l])
        khat = _filter_spectrum(k2, ss, seq)
        z, z_col = uhy, 0
        for order in range(HYENA_ORDER):
            z = _long_conv_gate(z, z_col, uhy, (order + 1) * D_HYENA, khat, hy_skip[l], order)
            z_col = 0
        o_f, o_b = _wkv(rkvk, lwa, rw_k_a[l])
        x1, h2a, h2b, eid, wsel = _mix_out(x, mod, z, o_f, o_b, g, bonus, rw_ln_w[l], rw_ln_b[l], w_out[l],
                                           norm2_g[l], router_w[l], router_bias[l])
        x = _moe(x1, h2a, h2b, mod, eid, wsel, exp_w_gate[l], exp_w_up[l], exp_w_down[l],
                 sh_w_gate[l], sh_w_up[l], sh_w_down[l], normf_g)
        x = x.reshape(bsz, seq, d)
    return x
```

```python
import functools
import math

import jax
import jax.numpy as jnp
import numpy as np
from jax import lax
from jax.experimental import pallas as pl
from jax.experimental.pallas import tpu as pltpu
from jax.experimental.pallas import tpu_sc as plsc

F32 = jnp.float32
BF16 = jnp.bfloat16

LANES = 128
MXU_DIM = 256
VMEM_LIMIT = 56 * 1024 * 1024

D_HYENA = 512
D_RWKV = 512
HEAD = 64
N_HEADS = D_RWKV // HEAD
HYENA_ORDER = 2
FILTER_BANDS = 16
DECAY_TARGET = 1e-2
FAST_DECAY_PCT = 0.3
SLOW_DECAY_PCT = 1.5
FILTER_NORM_EPS = 1e-6
DECAY_LORA = 32
ICLR_LORA = 32
GATE_LORA = 96
GN_EPS = 64e-5
NORM_EPS = 1e-6
N_EXPERTS = 256
TOP_K = 8
N_GROUPS = 8
TOPK_GROUPS = 4
ROUTE_SCALE = 2.5
D_EXPERT = 256


def _params(*sem):
    return pltpu.CompilerParams(dimension_semantics=sem, vmem_limit_bytes=VMEM_LIMIT)


def _split2(a):
    hi = a.astype(BF16)
    lo = (a - hi.astype(F32)).astype(BF16)
    return hi, lo


def _dot(a, b):
    return jnp.dot(a, b, preferred_element_type=F32)


def _dot3(a, b):
    ah, al = _split2(a)
    bh, bl = _split2(b)
    return _dot(ah, bh) + (_dot(ah, bl) + _dot(al, bh))


def _dot_exact_rhs(a, b_bf16):
    ah, al = _split2(a)
    return _dot(ah, b_bf16) + _dot(al, b_bf16)


def _silu(x):
    return x * jax.nn.sigmoid(x)


U32 = jnp.int32


def _pack_halves(x):
    w = x.shape[1] // 2
    return pltpu.pack_elementwise([x[:, :w], x[:, w:]], packed_dtype=BF16)


def _unpack_halves(p):
    lo = pltpu.unpack_elementwise(p, index=0, packed_dtype=BF16, unpacked_dtype=F32)
    hi = pltpu.unpack_elementwise(p, index=1, packed_dtype=BF16, unpacked_dtype=F32)
    return lo, hi


def _pack_rows(x):
    packed = _pack_halves(x)
    half = packed.shape[1] // 2
    return packed[:, :half], packed[:, half:]


def _unpack_rows(a, b):
    a_lo, a_hi = _unpack_halves(a)
    b_lo, b_hi = _unpack_halves(b)
    return jnp.concatenate([a_lo.astype(BF16), b_lo.astype(BF16), a_hi.astype(BF16), b_hi.astype(BF16)], axis=1)


def _mod_kernel(c_ref, w_ref, b_ref, o_ref):
    o_ref[...] = _dot3(_silu(c_ref[...]), w_ref[...]) + b_ref[...]


def _modulation(c, w_ada, b_ada):
    bsz, d = c.shape
    n = w_ada.shape[1]
    blk = 1024
    return pl.pallas_call(
        _mod_kernel,
        grid=(n // blk,),
        in_specs=[
            pl.BlockSpec((bsz, d), lambda j: (0, 0)),
            pl.BlockSpec((d, blk), lambda j: (0, j)),
            pl.BlockSpec((1, blk), lambda j: (0, j)),
        ],
        out_specs=pl.BlockSpec((bsz, blk), lambda j: (0, j)),
        out_shape=jax.ShapeDtypeStruct((bsz, n), F32),
        compiler_params=_params("arbitrary"),
        name="adaln_mod",
    )(c, w_ada, b_ada.reshape(1, n))


def _filter_kernel(band_ref, w1_ref, b1_ref, w2_ref, b2_ref, w3_ref, freq_ref, delta_ref,
                   k_ref, ss_ref, *, seq, rows):
    half = pl.program_id(0)
    i = pl.program_id(1)
    r = lax.broadcasted_iota(jnp.int32, (rows, LANES), 0) + i * rows
    pos = jnp.where(half == 0, r, seq - r).astype(F32)
    tt = pos / float(max(seq - 1, 1))
    lane = lax.broadcasted_iota(jnp.int32, (rows, LANES), 1)
    ang = pos * band_ref[...]
    feats = jnp.where(lane == 0, tt,
                      jnp.where(lane <= FILTER_BANDS, jnp.cos(ang),
                                jnp.where(lane <= 2 * FILTER_BANDS, -jnp.sin(ang), 0.0)))
    freq = freq_ref[...]
    hdn = jnp.sin(freq * (_dot3(feats, w1_ref[...]) + b1_ref[...]))
    for j in range(w2_ref.shape[0]):
        hdn = jnp.sin(freq * (_dot3(hdn, w2_ref[j]) + b2_ref[j]))
    filt = _dot3(hdn, w3_ref[...])
    filt = filt * jnp.exp(-tt[:, :1] * delta_ref[...])
    valid = jnp.logical_or(half == 0, r[:, :1] > 0)
    filt = jnp.where(valid, filt, 0.0)
    k_ref[...] = filt

    @pl.when(jnp.logical_and(half == 0, i == 0))
    def _():
        ss_ref[...] = jnp.zeros_like(ss_ref)

    ss_ref[...] += jnp.broadcast_to(jnp.sum(filt * filt, axis=0, keepdims=True), ss_ref.shape)


def _hyena_filters(seq, pw1, pb1, pw2, pb2, pw3, freq):
    width = pw1.shape[1]
    ncol = HYENA_ORDER * D_HYENA
    rows = min(seq, 512)
    bands = np.zeros((1, LANES), np.float64)
    lin = np.linspace(1e-4, FILTER_BANDS - 1, FILTER_BANDS)
    bands[0, 1:1 + FILTER_BANDS] = lin
    bands[0, 1 + FILTER_BANDS:1 + 2 * FILTER_BANDS] = lin
    bands = jnp.asarray(bands * (2.0 * math.pi / seq), F32)
    deltas = np.abs(np.linspace(math.log(DECAY_TARGET) / SLOW_DECAY_PCT,
                                math.log(DECAY_TARGET) / FAST_DECAY_PCT, D_HYENA))
    deltas = jnp.asarray(np.tile(deltas, HYENA_ORDER)[None], F32)
    w1 = jnp.zeros((LANES, width), F32).at[:pw1.shape[0]].set(pw1)
    w3 = pw3.reshape(width, HYENA_ORDER, 2, D_HYENA).transpose(2, 0, 1, 3).reshape(2, width, ncol)
    nt = seq // rows
    full = lambda *shape: pl.BlockSpec(shape, lambda h, i: (0,) * len(shape))
    return pl.pallas_call(
        functools.partial(_filter_kernel, seq=seq, rows=rows),
        grid=(2, nt),
        in_specs=[
            full(1, LANES), full(LANES, width), full(1, width),
            full(pw2.shape[0], width, width), full(pw2.shape[0], 1, width),
            pl.BlockSpec((None, width, ncol), lambda h, i: (h, 0, 0)),
            full(1, width), full(1, ncol),
        ],
        out_specs=[
            pl.BlockSpec((rows, ncol), lambda h, i: (h * nt + i, 0)),
            pl.BlockSpec((8, ncol), lambda h, i: (0, 0)),
        ],
        out_shape=[jax.ShapeDtypeStruct((2 * seq, ncol), F32),
                   jax.ShapeDtypeStruct((8, ncol), F32)],
        compiler_params=_params("arbitrary", "arbitrary"),
        name="hyena_filters",
    )(bands, w1, pb1.reshape(1, width), pw2, pb2.reshape(pw2.shape[0], 1, width), w3,
      freq.reshape(1, width), deltas)


N1 = LANES
UNROLL = 8


def _dft_tables(seq):
    tables = _dft_tables_np(seq)
    return tuple(jnp.asarray(t, BF16) for t in tables[:5]) + tables[5:]


def _dft_tables_np(seq):
    m = 2 * seq
    n2 = m // N1
    n2h = n2 // 2
    n1 = np.arange(N1)[:, None, None]
    f2 = np.arange(n2)[None, :, None]
    k2 = np.arange(n2)[None, None, :]
    th = 2.0 * np.pi * (n1 * f2 / m + (k2 * f2 % n2) / n2)
    fwd_a = np.concatenate([np.cos(th), -np.sin(th)], axis=1)
    tht = np.transpose(th, (0, 2, 1))
    inv_a = np.concatenate([np.cos(tht), -np.sin(tht)], axis=2)[:, :n2h] / m
    a = np.arange(N1)
    ph = 2.0 * np.pi * np.outer(a, a) / N1
    c, s = np.cos(ph), np.sin(ph)
    fwd_b = np.block([[c, s], [-s, c]])
    inv_b = np.block([[c, -s], [s, c]])
    return fwd_a, fwd_a[:, :, :n2h], inv_a, fwd_b, inv_b, n2, n2h


def _stage_a_fwd(x_ref, wa_ref, y_ref, n2, scale=None):
    def body(i, carry):
        trips = [i * UNROLL + j for j in range(UNROLL)]
        xs = [x_ref[pl.ds(n1, wa_ref.shape[2], stride=N1), :] for n1 in trips]
        if scale is not None:
            xs = [x * scale for x in xs]
        prods = [_dot(wa_ref[n1], x.astype(BF16)) for n1, x in zip(trips, xs)]
        for n1, a in zip(trips, prods):
            y_ref[pl.ds(n1, n2, stride=2 * N1), :] = a[:n2]
            y_ref[pl.ds(N1 + n1, n2, stride=2 * N1), :] = a[n2:]
        return carry
    lax.fori_loop(0, N1 // UNROLL, body, 0)


def _filter_fft_kernel(k_ref, ss_ref, wa_ref, fb_ref, o_ref, y_ref, *, n2):
    scale = lax.rsqrt(ss_ref[0:1, :] + FILTER_NORM_EPS)
    _stage_a_fwd(k_ref, wa_ref, y_ref, n2, scale=scale)

    unr = min(UNROLL, n2)

    def body(i, carry):
        trips = [i * unr + j for j in range(unr)]
        ys = [y_ref[pl.ds(pl.multiple_of(f2 * 2 * N1, 2 * N1), 2 * N1), :].astype(BF16) for f2 in trips]
        for f2, y in zip(trips, ys):
            o_ref[f2] = _dot(fb_ref[...], y)
        return carry
    lax.fori_loop(0, n2 // unr, body, 0)


def _filter_spectrum(k2, ss, seq):
    fwd_a, _, _, fwd_b, _, n2, _ = _dft_tables(seq)
    ncol = k2.shape[1]
    nblk = ncol // LANES
    return pl.pallas_call(
        functools.partial(_filter_fft_kernel, n2=n2),
        grid=(nblk,),
        in_specs=[
            pl.BlockSpec((2 * seq, LANES), lambda c: (0, c)),
            pl.BlockSpec((8, LANES), lambda c: (0, c)),
            pl.BlockSpec(fwd_a.shape, lambda c: (0, 0, 0)),
            pl.BlockSpec(fwd_b.shape, lambda c: (0, 0)),
        ],
        out_specs=pl.BlockSpec((None, n2, 2 * N1, LANES), lambda c: (c, 0, 0, 0)),
        out_shape=jax.ShapeDtypeStruct((nblk, n2, 2 * N1, LANES), F32),
        scratch_shapes=[pltpu.VMEM((n2 * 2 * N1, LANES), F32)],
        compiler_params=_params("arbitrary"),
        name="hyena_filter_fft",
    )(k2, ss, fwd_a, fwd_b)


TILE = 8
N1_GROUPS = N1 // TILE


def _tile_tables(seq):
    _, fwd_a, inv_a, _, _, n2, n2h = _dft_tables_np(seq)
    eye = np.eye(TILE)
    fa = fwd_a.reshape(N1_GROUPS, TILE, 2 * n2, n2h)
    wa = np.einsum("qjrn,jk->qrjnk", fa, eye).reshape(N1_GROUPS, 2 * n2 * TILE, n2h * TILE)
    ia = inv_a.reshape(N1_GROUPS, TILE, n2h, 2 * n2)
    vc = np.einsum("qjnr,jk->qnjrk", ia, eye).reshape(N1_GROUPS, n2h * TILE, 2 * n2 * TILE)
    return jnp.asarray(wa, BF16), jnp.asarray(vc, BF16)


def _conv_kernel(u_ref, g_ref, skip_ref, fb_ref, ib_ref, kh_hbm, wa_hbm, vc_hbm, o_ref,
                 y_ref, kh_ref, wa_ref, vc_ref, sem, *, n2, n2h, kh_first):
    c_id, b_id = pl.program_id(0), pl.program_id(1)

    @pl.when(jnp.logical_and(c_id == 0, b_id == 0))
    def _():
        for src, dst in ((wa_hbm, wa_ref), (vc_hbm, vc_ref)):
            cp = pltpu.make_async_copy(src, dst, sem)
            cp.start()
            cp.wait()

    @pl.when(b_id == 0)
    def _():
        cp = pltpu.make_async_copy(kh_hbm.at[kh_first + c_id], kh_ref, sem)
        cp.start()
        cp.wait()

    def y_tile(rf, base):
        ri, f2 = divmod(rf, n2)
        return pl.ds(f2 * 2 * N1 + ri * N1 + base, TILE)

    def stage_a(q, carry):
        base = pl.multiple_of(q * TILE, TILE)
        x = jnp.concatenate([u_ref[pl.ds(N1 * m + base, TILE), :] for m in range(n2h)], axis=0)
        r = _dot(wa_ref[q], x.astype(BF16))
        for rf in range(2 * n2):
            y_ref[y_tile(rf, base), :] = r[rf * TILE:(rf + 1) * TILE]
        return carry
    lax.fori_loop(0, N1_GROUPS, stage_a, 0, unroll=2)

    unr = min(UNROLL, n2)

    def mid(i, carry):
        trips = [i * unr + j for j in range(unr)]
        offs = [pl.multiple_of(f2 * 2 * N1, 2 * N1) for f2 in trips]
        zs = [_dot(fb_ref[...], y_ref[pl.ds(off, 2 * N1), :].astype(BF16)) for off in offs]
        ps = []
        for f2, z in zip(trips, zs):
            zr, zi = z[:N1], z[N1:]
            kh = kh_ref[f2]
            kr, ki = kh[:N1], kh[N1:]
            ps.append(jnp.concatenate([zr * kr - zi * ki, zr * ki + zi * kr], axis=0).astype(BF16))
        gs = [_dot(ib_ref[...], p) for p in ps]
        for off, g in zip(offs, gs):
            y_ref[pl.ds(off, 2 * N1), :] = g
        return carry
    lax.fori_loop(0, n2 // unr, mid, 0)

    skip = skip_ref[...]

    def stage_c(q, carry):
        base = pl.multiple_of(q * TILE, TILE)
        g = jnp.concatenate([y_ref[y_tile(rf, base), :] for rf in range(2 * n2)], axis=0)
        conv = _dot(vc_ref[q], g.astype(BF16))
        for m in range(n2h):
            rows = pl.ds(N1 * m + base, TILE)
            o_ref[rows, :] = g_ref[rows, :] * (conv[m * TILE:(m + 1) * TILE] + u_ref[rows, :] * skip)
        return carry
    lax.fori_loop(0, N1_GROUPS, stage_c, 0, unroll=2)


def _long_conv_gate(u, u_col, gate, gate_col, khat, skip, order):
    bsz, seq, _ = u.shape
    ch = D_HYENA
    _, _, _, fwd_b, inv_b, n2, n2h = _dft_tables(seq)
    wa, vc = _tile_tables(seq)
    nblk = ch // LANES
    const = lambda a: pl.BlockSpec(a.shape, lambda c, b: (0,) * a.ndim)
    at = lambda col: pl.BlockSpec((None, seq, LANES), lambda c, b: (b, 0, col // LANES + c))
    hbm = pl.BlockSpec(memory_space=pl.ANY)
    return pl.pallas_call(
        functools.partial(_conv_kernel, n2=n2, n2h=n2h, kh_first=order * nblk),
        grid=(nblk, bsz),
        in_specs=[
            at(u_col), at(gate_col),
            pl.BlockSpec((1, LANES), lambda c, b: (0, c)),
            const(fwd_b), const(inv_b), hbm, hbm, hbm,
        ],
        out_specs=at(0),
        out_shape=jax.ShapeDtypeStruct((bsz, seq, ch), F32),
        scratch_shapes=[pltpu.VMEM((n2 * 2 * N1, LANES), F32), pltpu.VMEM(khat.shape[1:], F32),
                        pltpu.VMEM(wa.shape, BF16), pltpu.VMEM(vc.shape, BF16), pltpu.SemaphoreType.DMA(())],
        compiler_params=_params("arbitrary", "arbitrary"),
        name=f"hyena_conv{order}",
    )(u, gate, skip[order].reshape(1, ch), fwd_b, inv_b, khat, wa, vc)


PAIR = MXU_DIM // LANES


def _conv_pair_kernel(u_ref, g_ref, skip_ref, fb_ref, ib_ref, kh_hbm, wa_hbm, vc_hbm, o_ref,
                      y_ref, kh_ref, wa_buf, vc_buf, kh_sem, tbl_sems, *, n2, n2h, kh_first):
    c_id, b_id = pl.program_id(0), pl.program_id(1)

    @pl.when(b_id == 0)
    def _():
        cp = pltpu.make_async_copy(kh_hbm.at[kh_first + c_id], kh_ref, kh_sem)
        cp.start()
        cp.wait()

    def table_copy(tbl_hbm, buf, q):
        slot = q % 2
        return pltpu.make_async_copy(tbl_hbm.at[q], buf.at[slot], tbl_sems.at[slot])

    def streamed(tbl_hbm, buf, body):
        table_copy(tbl_hbm, buf, 0).start()

        def step(q, carry):
            table_copy(tbl_hbm, buf, q).wait()

            @pl.when(q + 1 < N1_GROUPS)
            def _():
                table_copy(tbl_hbm, buf, q + 1).start()

            body(q, buf[q % 2])
            return carry
        lax.fori_loop(0, N1_GROUPS, step, 0)

    lanes = lambda parts: jnp.concatenate(parts, axis=1)

    def y_tile(rf, base):
        ri, f2 = divmod(rf, n2)
        return pl.ds(f2 * 2 * N1 + ri * N1 + base, TILE)

    def stage_a(q, matrix):
        base = pl.multiple_of(q * TILE, TILE)
        x = jnp.concatenate([lanes([u_ref[s, pl.ds(N1 * m + base, TILE), :] for s in range(PAIR)])
                             for m in range(n2h)], axis=0)
        r = _dot(matrix, x.astype(BF16))
        for rf in range(2 * n2):
            y_ref[y_tile(rf, base), :] = r[rf * TILE:(rf + 1) * TILE]
    streamed(wa_hbm, wa_buf, stage_a)

    unr = min(UNROLL // 2, n2)

    def mid(i, carry):
        trips = [i * unr + j for j in range(unr)]
        offs = [pl.multiple_of(f2 * 2 * N1, 2 * N1) for f2 in trips]
        zs = [_dot(fb_ref[...], y_ref[pl.ds(off, 2 * N1), :].astype(BF16)) for off in offs]
        ps = []
        for f2, z in zip(trips, zs):
            zr, zi = z[:N1], z[N1:]
            kh = kh_ref[f2]
            kr, ki = lanes([kh[:N1]] * PAIR), lanes([kh[N1:]] * PAIR)
            ps.append(jnp.concatenate([zr * kr - zi * ki, zr * ki + zi * kr], axis=0).astype(BF16))
        gs = [_dot(ib_ref[...], p) for p in ps]
        for off, g in zip(offs, gs):
            y_ref[pl.ds(off, 2 * N1), :] = g
        return carry
    lax.fori_loop(0, n2 // unr, mid, 0)

    skip = skip_ref[...]

    def stage_c(q, matrix):
        base = pl.multiple_of(q * TILE, TILE)
        g = jnp.concatenate([y_ref[y_tile(rf, base), :] for rf in range(2 * n2)], axis=0)
        conv = _dot(matrix, g.astype(BF16))
        for m in range(n2h):
            rows = pl.ds(N1 * m + base, TILE)
            for s in range(PAIR):
                cv = conv[m * TILE:(m + 1) * TILE, s * LANES:(s + 1) * LANES]
                o_ref[s, rows, :] = g_ref[s, rows, :] * (cv + u_ref[s, rows, :] * skip)
    streamed(vc_hbm, vc_buf, stage_c)


def _long_conv_gate_pairs(u, u_col, gate, gate_col, khat, skip, order):
    bsz, seq, _ = u.shape
    ch = D_HYENA
    _, _, _, fwd_b, inv_b, n2, n2h = _dft_tables(seq)
    wa, vc = _tile_tables(seq)
    nblk = ch // LANES
    const = lambda a: pl.BlockSpec(a.shape, lambda c, b: (0,) * a.ndim, pipeline_mode=pl.Buffered(1))
    at = lambda col: pl.BlockSpec((PAIR, seq, LANES), lambda c, b: (b, 0, col // LANES + c))
    hbm = pl.BlockSpec(memory_space=pl.ANY)
    return pl.pallas_call(
        functools.partial(_conv_pair_kernel, n2=n2, n2h=n2h, kh_first=order * nblk),
        grid=(nblk, bsz // PAIR),
        in_specs=[
            at(u_col), at(gate_col),
            pl.BlockSpec((1, LANES), lambda c, b: (0, c), pipeline_mode=pl.Buffered(1)),
            const(fwd_b), const(inv_b), hbm, hbm, hbm,
        ],
        out_specs=at(0),
        out_shape=jax.ShapeDtypeStruct((bsz, seq, ch), F32),
        scratch_shapes=[pltpu.VMEM((n2 * 2 * N1, PAIR * LANES), F32), pltpu.VMEM(khat.shape[1:], F32),
                        pltpu.VMEM((2,) + wa.shape[1:], BF16), pltpu.VMEM((2,) + vc.shape[1:], BF16),
                        pltpu.SemaphoreType.DMA(()), pltpu.SemaphoreType.DMA((2,))],
        compiler_params=_params("arbitrary", "arbitrary"),
        name=f"hyena_conv{order}",
    )(u, gate, skip[order].reshape(1, ch), fwd_b, inv_b, khat, wa, vc)


HALO = 8


def _shift_rows(p, k):
    return pltpu.roll(p, k % p.shape[0], axis=0)


def _proj_kernel(xp_ref, x_ref, xn_ref, mod_ref, g1_ref, why_ref, wrkv_ref, wlora_ref,
                 cw_ref, cb_ref, murkv_ref, mulora_ref, w0_ref, a0_ref, wwah_ref, wwal_ref, gup_ref,
                 kk_ref, ka_ref, rk_ref, ones_ref,
                 uhy_ref, rkvk_ref, lwa_ref, g_ref, bonus_ref,
                 *, tt, nt):
    i = pl.program_id(1)
    xe = jnp.concatenate([xp_ref[...], x_ref[...], xn_ref[...]], axis=0)
    ms = jnp.mean(xe * xe, axis=-1, keepdims=True)
    h = xe * lax.rsqrt(ms + NORM_EPS) * g1_ref[...]
    h = h * (1.0 + mod_ref[1:2, :]) + mod_ref[0:1, :]
    row = lax.broadcasted_iota(jnp.int32, (tt + 2 * HALO, 1), 0)
    inside = jnp.logical_and(jnp.logical_or(row >= HALO, i > 0),
                             jnp.logical_or(row < tt + HALO, i < nt - 1))
    hb = jnp.where(inside, h, 0.0).astype(BF16)
    mid = slice(HALO, tt + HALO)

    p = _dot(hb, why_ref[...])
    u = (_shift_rows(p, 1) * cw_ref[0:1, :] + p * cw_ref[1:2, :]
         + _shift_rows(p, -1) * cw_ref[2:3, :] + cb_ref[...])
    uhy_ref[...] = u[mid]

    p = _dot(hb, wrkv_ref[...])
    p = p + murkv_ref[...] * (0.5 * (_shift_rows(p, 1) + _shift_rows(p, -1)) - p)
    p = p[mid]
    c = D_RWKV
    r, k, v = p[:, :c], p[:, c:2 * c], p[:, 2 * c:]
    rkvk_ref[:, :3 * c] = p

    q = _dot(hb, wlora_ref[...])
    q = q + mulora_ref[...] * (0.5 * (_shift_rows(q, 1) + _shift_rows(q, -1)) - q)
    q = q[mid]
    wa = q[:, :LANES]
    lane = lax.broadcasted_iota(jnp.int32, wa.shape, 1)
    wa = jnp.where(lane < 2 * DECAY_LORA, jnp.tanh(wa), wa)
    wah, wal = _split2(wa)
    up = _dot(wah, wwah_ref[...]) + (_dot(wah, wwal_ref[...]) + _dot(wal, wwah_ref[...]))
    lw = -math.exp(-0.5) * jax.nn.sigmoid(w0_ref[...] + up[:, :2 * c])
    a = jax.nn.sigmoid(a0_ref[...] + up[:, 2 * c:])
    for dd in range(2):
        lwa_ref[:, 2 * dd * c:(2 * dd + 1) * c] = lw[:, dd * c:(dd + 1) * c]
        lwa_ref[:, (2 * dd + 1) * c:(2 * dd + 2) * c] = a[:, dd * c:(dd + 1) * c]
    g_ref[...] = _dot3(jax.nn.sigmoid(q[:, LANES:]), gup_ref[...])

    ones = ones_ref[...]
    kk = k * kk_ref[...]
    nrm = jnp.sqrt(_dot_exact_rhs(kk * kk, ones))
    rkvk_ref[:, 3 * c:] = kk / jnp.maximum(nrm, 1e-12)
    ka = ka_ref[...]
    ksum = k * (2.0 + (a[:, :c] + a[:, c:] - 2.0) * ka)
    bonus_ref[...] = _dot_exact_rhs(r * ksum * rk_ref[...], ones) * v


def _head_ones():
    hid = np.arange(D_RWKV) // HEAD
    return jnp.asarray(hid[:, None] == hid[None, :], BF16)


def _projection(x, mod, norm1_g, w_in, hy_conv_w, hy_conv_b, rw_mu, rw_w0, rw_w_up, rw_a0,
                rw_a_up, rw_g_up, rw_k_k, rw_k_a, rw_r_k, tt=512):
    bsz, seq, d = x.shape
    tt = min(tt, seq)
    nt = seq // tt
    c = D_RWKV
    hy = (HYENA_ORDER + 1) * D_HYENA
    nlora = 2 * LANES
    w_hy = w_in[:, :hy].astype(BF16)
    w_rkv = w_in[:, hy:hy + 3 * c].astype(BF16)
    w_lora = jnp.zeros((d, nlora), F32).at[:, :w_in.shape[1] - hy - 3 * c].set(w_in[:, hy + 3 * c:]).astype(BF16)
    mu_rkv = rw_mu[:3 * c].reshape(1, 3 * c)
    mu_lora = jnp.zeros((1, nlora), F32).at[0, :rw_mu.shape[0] - 3 * c].set(rw_mu[3 * c:])
    wwa = jnp.zeros((LANES, 4 * c), F32)
    for dd in range(2):
        wwa = wwa.at[dd * DECAY_LORA:(dd + 1) * DECAY_LORA, dd * c:(dd + 1) * c].set(rw_w_up[dd])
        wwa = wwa.at[2 * DECAY_LORA + dd * ICLR_LORA:2 * DECAY_LORA + (dd + 1) * ICLR_LORA,
                     2 * c + dd * c:2 * c + (dd + 1) * c].set(rw_a_up[dd])
    gup = jnp.zeros((LANES, c), F32).at[:GATE_LORA].set(rw_g_up)
    row = lambda a: a.reshape(1, -1)

    nb8 = seq // HALO
    tb = tt // HALO
    const = lambda a: pl.BlockSpec(a.shape, lambda b, i: (0,) * a.ndim, pipeline_mode=pl.Buffered(1))
    tile = lambda w: pl.BlockSpec((None, tt, w), lambda b, i: (b, i, 0))
    ins = [
        (x, pl.BlockSpec((None, HALO, d), lambda b, i: (b, jnp.maximum(i * tb - 1, 0), 0))),
        (x, pl.BlockSpec((None, tt, d), lambda b, i: (b, i, 0))),
        (x, pl.BlockSpec((None, HALO, d), lambda b, i: (b, jnp.minimum((i + 1) * tb, nb8 - 1), 0))),
        (mod, pl.BlockSpec((None,) + mod.shape[1:], lambda b, i: (b, 0, 0))),
    ]
    consts = [row(norm1_g), w_hy, w_rkv, w_lora, hy_conv_w, row(hy_conv_b), mu_rkv, mu_lora,
              row(rw_w0), row(rw_a0), *_split2(wwa), gup, row(rw_k_k), row(rw_k_a), row(rw_r_k), _head_ones()]
    ins += [(a, const(a)) for a in consts]
    widths = [hy, 4 * c, 4 * c, c, c]
    return pl.pallas_call(
        functools.partial(_proj_kernel, tt=tt, nt=nt),
        grid=(bsz, nt),
        in_specs=[s for _, s in ins],
        out_specs=[tile(w) for w in widths],
        out_shape=[jax.ShapeDtypeStruct((bsz, seq, w), F32) for w in widths],
        compiler_params=_params("arbitrary", "arbitrary"),
        name="input_projection",
    )(*[a for a, _ in ins])


CHUNK = HEAD
GROUP = MXU_DIM // HEAD


def _nt(a, b):
    return lax.dot_general(a, b, (((1,), (1,)), ((), ())), preferred_element_type=F32)


def _tn(a, b):
    return lax.dot_general(a, b, (((0,), (0,)), ((), ())), preferred_element_type=F32)


def _wkv_direction(r, k, v, kk, lw, a, ka, s_ref, reverse):
    c = CHUNK
    ti = lax.broadcasted_iota(jnp.int32, (c, c), 0)
    si = lax.broadcasted_iota(jnp.int32, (c, c), 1)
    tri = (si >= ti) if reverse else (si <= ti)
    cum = _dot_exact_rhs_lhs(jnp.where(tri, 1.0, 0.0).astype(BF16), lw)
    tot = jnp.sum(lw, axis=0, keepdims=True)
    w_incl = jnp.exp(cum)
    w_prev = jnp.exp(cum - lw)
    w_inv = jnp.exp(-cum)
    w_end = jnp.exp(tot - cum)
    w_tot = jnp.exp(tot)
    kd = k * (1.0 + (a - 1.0) * ka)
    b = kk * a
    a_w = -kk * w_prev
    r_w = r * w_incl
    b_w = b * w_inv
    k_w = kd * w_inv
    b_e = b * w_end
    k_e = kd * w_end

    m = MXU_DIM
    ri = lax.broadcasted_iota(jnp.int32, (m, m), 0)
    ci = lax.broadcasted_iota(jnp.int32, (m, m), 1)
    head_mask = (ri // HEAD) == (ci // HEAD)
    tl = lax.broadcasted_iota(jnp.int32, (c, m), 0)
    sl = lax.broadcasted_iota(jnp.int32, (c, m), 1) % c
    strict = (sl > tl) if reverse else (sl < tl)
    incl = (sl >= tl) if reverse else (sl <= tl)
    eye = jnp.where(sl == tl, 1.0, 0.0)
    both = lambda top, bot: jnp.concatenate([top, bot], axis=0)

    def stack(xg):
        xb = xg.astype(BF16)
        return jnp.where(head_mask, jnp.concatenate([xb] * GROUP, axis=0), jnp.zeros((), BF16))

    streams = []
    for g in range(D_RWKV // m):
        sl_g = slice(g * m, (g + 1) * m)
        streams.append(dict(
            ar=both(a_w[:, sl_g], r_w[:, sl_g]).astype(BF16),
            b_st=stack(b_w[:, sl_g]), k_st=stack(k_w[:, sl_g]), v_st=stack(v[:, sl_g]),
            v=v[:, sl_g], bk=both(b_e[:, sl_g], k_e[:, sl_g]).astype(BF16),
            w_tot=w_tot[:, sl_g], s_ref=s_ref.at[g],
            strict=strict, incl=incl, eye=eye, head_mask=head_mask, stack=stack))
    return streams


def _wkv_streams_step(streams):
    c = CHUNK
    both = lambda top, bot: jnp.concatenate([top, bot], axis=0)
    for st in streams:
        st["s"] = st["s_ref"][...]
        st["xb"] = _nt(st["ar"], st["b_st"])
        st["xk"] = _nt(st["ar"], st["k_st"])
        st["xs"] = _nt(st["ar"], st["s"].astype(BF16))
    for st in streams:
        m_k = both(jnp.where(st["strict"], st["xk"][:c], 0.0), jnp.where(st["incl"], st["xk"][c:], 0.0))
        st["kv"] = _dot(m_k.astype(BF16), st["v_st"])
        st["rhs"] = st["xs"][:c] + st["kv"][:c]
        st["pw"] = jnp.where(st["strict"], st["xb"][:c], 0.0)
        st["t"] = st["eye"] + st["pw"]
        st["p_st"] = st["stack"](st["pw"])
    levels = int(math.log2(c)) - 1
    for st in streams:
        st["pw"] = _dot(st["pw"].astype(BF16), st["p_st"])
        st["p_st"] = st["stack"](st["pw"])
    for lvl in range(1, levels + 1):
        for st in streams:
            if lvl < levels:
                prod = _dot(both(st["pw"], st["t"]).astype(BF16), st["p_st"])
                st["pw"] = prod[:c]
                st["t"] = st["t"] + prod[c:]
                st["p_st"] = st["stack"](st["pw"])
            else:
                st["t"] = st["t"] + _dot(st["t"].astype(BF16), st["p_st"])
    for st in streams:
        st["u"] = _dot(st["t"].astype(BF16), st["stack"](st["rhs"]))
    outs = []
    for st in streams:
        m_rb = jnp.where(st["incl"], st["xb"][c:], 0.0)
        outs.append(st["xs"][c:] + _dot(m_rb.astype(BF16), st["stack"](st["u"])) + st["kv"][c:])
        uv = both(st["u"], st["v"]).astype(BF16)
        st["s_ref"][...] = st["s"] * st["w_tot"] + jnp.where(st["head_mask"], _tn(uv, st["bk"]), 0.0)
    return outs


def _dot_exact_rhs_lhs(tri_bf16, x):
    xh, xl = _split2(x)
    return _dot(tri_bf16, xh) + _dot(tri_bf16, xl)


def _wkv_kernel(rkvk_f, lwa_f, rkvk_b, lwa_b, ka_ref, of_ref, ob_ref, s_ref, *, nch):
    @pl.when(pl.program_id(1) == 0)
    def _():
        s_ref[...] = jnp.zeros_like(s_ref)

    ka = ka_ref[...]
    c = D_RWKV

    def operands(rkvk_ref, lwa_ref, rows):
        x = rkvk_ref[rows, :]
        la = lwa_ref[rows, :]
        return x[:, :c], x[:, c:2 * c], x[:, 2 * c:3 * c], x[:, 3 * c:], la[:, :c], la[:, c:]

    for ci in range(nch):
        rows_f = slice(ci * CHUNK, (ci + 1) * CHUNK)
        rows_b = slice((nch - 1 - ci) * CHUNK, (nch - ci) * CHUNK)
        fwd = _wkv_direction(*operands(rkvk_f, lwa_f, rows_f), ka, s_ref.at[0], False)
        bwd = _wkv_direction(*operands(rkvk_b, lwa_b, rows_b), ka, s_ref.at[1], True)
        outs = _wkv_streams_step(fwd + bwd)
        of_ref[rows_f, :] = jnp.concatenate(outs[:len(fwd)], axis=1)
        ob_ref[rows_b, :] = jnp.concatenate(outs[len(fwd):], axis=1)


WKV_CHUNKS_PER_STEP = 4


def _wkv(rkvk, lwa, rw_k_a):
    bsz, seq, _ = rkvk.shape
    c = D_RWKV
    nch = WKV_CHUNKS_PER_STEP if seq % (WKV_CHUNKS_PER_STEP * CHUNK) == 0 else 1
    rows = nch * CHUNK
    nb = seq // rows
    fwd = lambda w, lane_blk: pl.BlockSpec((None, rows, w), lambda b, j: (b, j, lane_blk))
    bwd = lambda w, lane_blk: pl.BlockSpec((None, rows, w), lambda b, j: (b, nb - 1 - j, lane_blk))
    return pl.pallas_call(
        functools.partial(_wkv_kernel, nch=nch),
        grid=(bsz, nb),
        in_specs=[fwd(4 * c, 0), fwd(2 * c, 0), bwd(4 * c, 0), bwd(2 * c, 1),
                  pl.BlockSpec((1, c), lambda b, j: (0, 0))],
        out_specs=[fwd(c, 0), bwd(c, 0)],
        out_shape=[jax.ShapeDtypeStruct((bsz, seq, c), F32)] * 2,
        scratch_shapes=[pltpu.VMEM((2, c // MXU_DIM, MXU_DIM, MXU_DIM), F32)],
        compiler_params=_params("arbitrary", "arbitrary"),
        name="wkv7_chunked",
    )(rkvk, lwa, rkvk, lwa, rw_k_a.reshape(1, c))


NEG_INF = float("-inf")


def _first_max(vals, idx, size):
    m = jnp.max(vals, axis=0, keepdims=True)
    i = jnp.min(jnp.where(vals == m, idx, size), axis=0, keepdims=True)
    return m, i


def _route(scores, biased):
    e, tt = scores.shape
    per = e // N_GROUPS
    rowl = lax.broadcasted_iota(jnp.int32, (per, tt), 0)
    gs = []
    for g in range(N_GROUPS):
        blk = biased[g * per:(g + 1) * per]
        m1, i1 = _first_max(blk, rowl, per)
        m2 = jnp.max(jnp.where(rowl == i1, NEG_INF, blk), axis=0, keepdims=True)
        gs.append(m1 + m2)
    cur = jnp.concatenate(gs, axis=0)
    growl = lax.broadcasted_iota(jnp.int32, (N_GROUPS, tt), 0)
    gsel = jnp.zeros((N_GROUPS, tt), F32)
    for _ in range(TOPK_GROUPS):
        _, ig = _first_max(cur, growl, N_GROUPS)
        hit = growl == ig
        gsel = jnp.where(hit, 1.0, gsel)
        cur = jnp.where(hit, NEG_INF, cur)
    emask = jnp.concatenate([jnp.broadcast_to(gsel[g:g + 1], (per, tt)) for g in range(N_GROUPS)], axis=0)
    masked = jnp.where(emask > 0.5, biased, NEG_INF)
    row = lax.broadcasted_iota(jnp.int32, (e, tt), 0)
    ids, ws = [], []
    for _ in range(TOP_K):
        _, ie = _first_max(masked, row, e)
        hit = row == ie
        ids.append(ie)
        ws.append(jnp.sum(jnp.where(hit, scores, 0.0), axis=0, keepdims=True))
        masked = jnp.where(hit, NEG_INF, masked)
    w = jnp.concatenate(ws, axis=0)
    w = w / jnp.sum(w, axis=0, keepdims=True) * ROUTE_SCALE
    return jnp.concatenate(ids, axis=0), w


def _mixout_kernel(x_ref, mod_ref, yhy_ref, of_ref, ob_ref, g_ref, bonus_ref, lnw_ref, lnb_ref,
                   ones_ref, wout_ref, g2n_ref, rwth_ref, rwtl_ref, bias_ref,
                   x1_ref, h2a_ref, h2b_ref, eid_ref, wsel_ref):
    ones = ones_ref[...]
    s = of_ref[...] + ob_ref[...]
    mean = _dot_exact_rhs(s, ones) * (1.0 / HEAD)
    dlt = s - mean
    var = _dot_exact_rhs(dlt * dlt, ones) * (1.0 / HEAD)
    sn = dlt * lax.rsqrt(var + GN_EPS) * lnw_ref[...] + lnb_ref[...]
    yrw = (sn + bonus_ref[...]) * g_ref[...]
    ch = yhy_ref.shape[-1]
    mix = _dot(yhy_ref[...].astype(BF16), wout_ref[:ch, :]) + _dot(yrw.astype(BF16), wout_ref[ch:, :])
    x1 = x_ref[...] + mod_ref[2:3, :] * mix
    x1_ref[...] = x1
    ms = jnp.mean(x1 * x1, axis=-1, keepdims=True)
    h2 = x1 * lax.rsqrt(ms + NORM_EPS) * g2n_ref[...]
    h2 = h2 * (1.0 + mod_ref[4:5, :]) + mod_ref[3:4, :]
    h2a_ref[...], h2b_ref[...] = _pack_rows(h2)
    rh, rl = rwth_ref[...], rwtl_ref[...]
    hh, hl = _split2(h2)
    logits = _nt(rh, hh) + (_nt(rh, hl) + _nt(rl, hh))
    scores = jax.nn.sigmoid(logits)
    ids, w = _route(scores, scores + bias_ref[...])
    eid_ref[...] = ids
    wsel_ref[...] = w


def _mix_out(x, mod, yhy, o_f, o_b, g, bonus, ln_w, ln_b, w_out, norm2_g, router_w, router_bias, tt=512):
    bsz, seq, d = x.shape
    tt = min(tt, seq)
    nt = seq // tt
    n = bsz * seq
    c = D_RWKV
    e = router_w.shape[1]
    row = lambda a: a.reshape(1, -1)
    consts = [row(ln_w), row(ln_b), _head_ones(), w_out.astype(BF16), row(norm2_g), *_split2(router_w.T),
              jnp.broadcast_to(router_bias.reshape(e, 1), (e, tt))]
    const = lambda a: pl.BlockSpec(a.shape, lambda b, i: (0,) * a.ndim, pipeline_mode=pl.Buffered(1))
    tile = lambda w: pl.BlockSpec((None, tt, w), lambda b, i: (b, i, 0))
    flat = lambda rows, dt: jax.ShapeDtypeStruct((rows, n), dt)
    return pl.pallas_call(
        _mixout_kernel,
        grid=(bsz, nt),
        in_specs=[tile(d), pl.BlockSpec((None,) + mod.shape[1:], lambda b, i: (b, 0, 0))]
        + [tile(c)] * 5 + [const(a) for a in consts],
        out_specs=[tile(d), pl.BlockSpec((tt, d // 4), lambda b, i: (b * nt + i, 0)),
                   pl.BlockSpec((tt, d // 4), lambda b, i: (b * nt + i, 0)),
                   pl.BlockSpec((TOP_K, tt), lambda b, i: (0, b * nt + i)),
                   pl.BlockSpec((TOP_K, tt), lambda b, i: (0, b * nt + i))],
        out_shape=[jax.ShapeDtypeStruct((bsz, seq, d), F32), jax.ShapeDtypeStruct((n, d // 4), U32),
                   jax.ShapeDtypeStruct((n, d // 4), U32),
                   flat(TOP_K, jnp.int32), flat(TOP_K, F32)],
        compiler_params=_params("arbitrary", "arbitrary"),
        name="mix_out_router",
    )(x, mod, yhy, o_f, o_b, g, bonus, *consts)


BLK = 512
BLK_SHIFT = 9


def _multi_hot(eid, e):
    row = lax.broadcasted_iota(jnp.int32, (e, eid.shape[1]), 0)
    mh = jnp.zeros((e, eid.shape[1]), F32)
    for kk in range(TOP_K):
        mh = mh + jnp.where(row == eid[kk:kk + 1, :], 1.0, 0.0)
    return row, mh


def _lookup(row, eid, table):
    return jnp.concatenate(
        [jnp.sum(jnp.where(row == eid[kk:kk + 1, :], table, 0.0), axis=0, keepdims=True)
         for kk in range(TOP_K)], axis=0)


def _rank_kernel(eid_ref, rank_ref, cnt_ref, *, e):
    @pl.when(pl.program_id(0) == 0)
    def _():
        cnt_ref[...] = jnp.zeros_like(cnt_ref)

    eid = eid_ref[...]
    tt = eid.shape[1]
    row, mh = _multi_hot(eid, e)
    mhb = mh.astype(BF16)
    si = lax.broadcasted_iota(jnp.int32, (tt, tt), 0)
    ti = lax.broadcasted_iota(jnp.int32, (tt, tt), 1)
    earlier = _dot(mhb, jnp.where(si < ti, 1.0, 0.0).astype(BF16))
    cnt = cnt_ref[...]
    full = earlier + jnp.concatenate([cnt] * (tt // LANES), axis=1)
    rank_ref[...] = _lookup(row, eid, full).astype(jnp.int32)
    cnt_ref[...] = cnt + _dot(mhb, jnp.ones((tt, LANES), BF16))


def _expert_ranks(eid, e, tt=512):
    n = eid.shape[1]
    tt = min(tt, n)
    return pl.pallas_call(
        functools.partial(_rank_kernel, e=e),
        grid=(n // tt,),
        in_specs=[pl.BlockSpec((TOP_K, tt), lambda i: (0, i))],
        out_specs=[pl.BlockSpec((TOP_K, tt), lambda i: (0, i)),
                   pl.BlockSpec((e, LANES), lambda i: (0, 0))],
        out_shape=[jax.ShapeDtypeStruct((TOP_K, n), jnp.int32), jax.ShapeDtypeStruct((e, LANES), F32)],
        compiler_params=_params("arbitrary"),
        name="expert_ranks",
    )(eid)


def _block_offsets(cnt):
    e = cnt.shape[0]
    nblk = ((cnt.astype(jnp.int32) + (BLK - 1)) >> BLK_SHIFT).astype(F32)
    ri = lax.broadcasted_iota(jnp.int32, (e, e), 0)
    ci = lax.broadcasted_iota(jnp.int32, (e, e), 1)
    tril = jnp.where(ci <= ri, 1.0, 0.0).astype(BF16)
    nh, nl = _split2(nblk)
    return nblk, _dot(tril, nh) + _dot(tril, nl)


def _dest_kernel(cnt_ref, eid_ref, rank_ref, dest_ref):
    nblk, end = _block_offsets(cnt_ref[...])
    off = (end - nblk) * float(BLK)
    eid = eid_ref[...]
    tt = eid.shape[1]
    row = lax.broadcasted_iota(jnp.int32, (off.shape[0], tt), 0)
    table = jnp.concatenate([off] * (tt // LANES), axis=1)
    dest_ref[...] = _lookup(row, eid, table).astype(jnp.int32) + rank_ref[...]


def _destinations(cnt, eid, rank, tt=512):
    n = eid.shape[1]
    tt = min(tt, n)
    blk = pl.BlockSpec((TOP_K, tt), lambda i: (0, i))
    return pl.pallas_call(
        _dest_kernel,
        grid=(n // tt,),
        in_specs=[pl.BlockSpec(cnt.shape, lambda i: (0, 0)), blk, blk],
        out_specs=blk,
        out_shape=jax.ShapeDtypeStruct((TOP_K, n), jnp.int32),
        compiler_params=_params("arbitrary"),
        name="expert_destinations",
    )(cnt, eid, rank)


def _meta_kernel(cnt_ref, meta_ref, *, nbp):
    cnt = cnt_ref[...]
    e = cnt.shape[0]
    nblk, end = _block_offsets(cnt)
    rep = lambda a, w: jnp.concatenate([a] * (w // LANES), axis=1)
    b = lax.broadcasted_iota(jnp.int32, (e, nbp), 1).astype(F32)
    blk_e = jnp.minimum(jnp.sum(jnp.where(rep(end, nbp) <= b, 1.0, 0.0), axis=0, keepdims=True), float(e - 1))
    row = lax.broadcasted_iota(jnp.int32, (e, nbp), 0).astype(F32)
    mine = row == blk_e
    left = rep(cnt + (end - nblk) * float(BLK), nbp) - b * float(BLK)
    nvalid = jnp.clip(jnp.sum(jnp.where(mine, left, 0.0), axis=0, keepdims=True), 0.0, float(BLK))
    nused = jnp.max(rep(end, nbp), axis=0, keepdims=True)
    later = jnp.logical_and(row > blk_e, rep(nblk, nbp) > 0.0)
    nxt = jnp.min(jnp.where(later, row, float(e)), axis=0, keepdims=True)
    nxt = jnp.where(nxt >= float(e), -1.0, nxt)
    meta_ref[...] = jnp.concatenate([blk_e, nvalid, nused, nxt, jnp.zeros((4, nbp), F32)],
                                    axis=0).astype(jnp.int32)


def _block_meta(cnt, nb):
    nbp = -(-nb // LANES) * LANES
    return pl.pallas_call(
        functools.partial(_meta_kernel, nbp=nbp),
        out_shape=jax.ShapeDtypeStruct((8, nbp), jnp.int32),
        compiler_params=pltpu.CompilerParams(vmem_limit_bytes=VMEM_LIMIT),
        name="expert_block_meta",
    )(cnt)


SC_WINDOW = 128


def _sc_mesh():
    return plsc.VectorSubcoreMesh(core_axis_name="core", subcore_axis_name="subcore")


def _sc_scatter_rows(rows, idx, nrows):
    n, width = rows.shape

    @pl.kernel(out_type=jax.ShapeDtypeStruct((nrows, width), rows.dtype), mesh=_sc_mesh())
    def scatter(rows_hbm, idx_hbm, out_hbm):
        def body(rows_vmem, idx_vmem):
            pltpu.sync_copy(rows_vmem, out_hbm.at[idx_vmem.at[0]])

        pltpu.emit_pipeline(
            body,
            grid=(n // SC_WINDOW, idx.shape[0]),
            in_specs=[pl.BlockSpec((SC_WINDOW, width), index_map=lambda i, k: (i, 0)),
                      pl.BlockSpec((1, SC_WINDOW), index_map=lambda i, k: (k, i))],
            out_specs=[],
            core_axis_name=("core", "subcore"),
            dimension_semantics=(pltpu.PARALLEL, pltpu.ARBITRARY),
        )(rows_hbm, idx_hbm)

    return scatter(rows, idx)


def _sc_gather_rows(src, idx):
    num = idx.shape[1]
    width = src.shape[1]

    @pl.kernel(out_type=jax.ShapeDtypeStruct((num, width), src.dtype), mesh=_sc_mesh())
    def gather(src_hbm, idx_hbm, out_hbm):
        def body(idx_vmem, out_vmem):
            pltpu.sync_copy(src_hbm.at[idx_vmem.at[0]], out_vmem)

        pltpu.emit_pipeline(
            body,
            grid=(num // SC_WINDOW,),
            in_specs=[pl.BlockSpec((1, SC_WINDOW), index_map=lambda i: (0, i))],
            out_specs=[pl.BlockSpec((SC_WINDOW, width), index_map=lambda i: (i, 0))],
            core_axis_name=("core", "subcore"),
            dimension_semantics=(pltpu.PARALLEL,),
        )(idx_hbm, out_hbm)

    return gather(src, idx)


def _experts_kernel(be_ref, nv_ref, nu_ref, nxt_ref, xa_ref, xb_ref, wg_hbm, wu_hbm, wd_hbm, oa_ref, ob_ref,
                    wgf, wuf, wdf, wgb, wub, wdb, sems, slot_ref):
    b = pl.program_id(0)
    used = b < nu_ref[0]

    def fetch(expert, slot):
        return [pltpu.make_async_copy(src.at[expert], dst.at[slot], sems.at[slot])
                for src, dst in ((wg_hbm, wgf), (wu_hbm, wuf), (wd_hbm, wdf))]

    @pl.when(b == 0)
    def _():
        slot_ref[0] = 0
        for cp in fetch(be_ref[0], 0):
            cp.start()

    @pl.when(used)
    def _():
        prev = be_ref[jnp.maximum(b - 1, 0)]

        @pl.when(jnp.logical_or(b == 0, be_ref[b] != prev))
        def _():
            slot = slot_ref[0]
            for cp in fetch(be_ref[b], slot):
                cp.wait()

            @pl.when(nxt_ref[b] >= 0)
            def _():
                for cp in fetch(nxt_ref[b], 1 - slot):
                    cp.start()

            wgb[...] = wgf[slot].astype(BF16)
            wub[...] = wuf[slot].astype(BF16)
            wdb[...] = wdf[slot].astype(BF16)
            slot_ref[0] = 1 - slot

        valid = lax.broadcasted_iota(jnp.int32, (xa_ref.shape[0], 1), 0) < nv_ref[b]
        zero = jnp.zeros((), U32)
        x = _unpack_rows(jnp.where(valid, xa_ref[...], zero), jnp.where(valid, xb_ref[...], zero))
        act = _silu(_dot(x, wgb[...])) * _dot(x, wub[...])
        oa_ref[...], ob_ref[...] = _pack_rows(_dot(act.astype(BF16), wdb[...]))


def _experts(blk_e, nvalid, nused, nxt_e, xs_a, xs_b, wg, wu, wd):
    p, dq = xs_a.shape
    nb = p // BLK
    d, de = wg.shape[1], wg.shape[2]
    rows_in = pl.BlockSpec((BLK, dq), lambda b, be, nv, nu, nx: (jnp.minimum(b, nu[0] - 1), 0))
    hbm = pl.BlockSpec(memory_space=pl.ANY)
    return pl.pallas_call(
        _experts_kernel,
        grid_spec=pltpu.PrefetchScalarGridSpec(
            num_scalar_prefetch=4,
            grid=(nb,),
            in_specs=[rows_in, rows_in, hbm, hbm, hbm],
            out_specs=[rows_in, rows_in],
            scratch_shapes=[pltpu.VMEM((2, d, de), F32), pltpu.VMEM((2, d, de), F32), pltpu.VMEM((2, de, d), F32),
                            pltpu.VMEM((d, de), BF16), pltpu.VMEM((d, de), BF16), pltpu.VMEM((de, d), BF16),
                            pltpu.SemaphoreType.DMA((2,)), pltpu.SMEM((1,), jnp.int32)],
        ),
        out_shape=[jax.ShapeDtypeStruct((p, dq), U32)] * 2,
        compiler_params=_params("arbitrary"),
        name="moe_experts",
    )(blk_e, nvalid, nused, nxt_e, xs_a, xs_b, wg, wu, wd)


def _shared_kernel(ha_ref, hb_ref, sg_ref, su_ref, sd_ref, o_ref):
    hb = _unpack_rows(ha_ref[...], hb_ref[...])
    act = _silu(_dot(hb, sg_ref[...])) * _dot(hb, su_ref[...])
    o_ref[...] = _dot(act.astype(BF16), sd_ref[...]).astype(o_ref.dtype)


def _shared_expert(h2a, h2b, sh_wg, sh_wu, sh_wd, tt=512):
    n, dp = h2a.shape
    d = sh_wg.shape[0]
    tt = min(tt, n)
    consts = [sh_wg.astype(BF16), sh_wu.astype(BF16), sh_wd.astype(BF16)]
    packed_rows = pl.BlockSpec((tt, dp), lambda i: (i, 0))
    return pl.pallas_call(
        _shared_kernel,
        grid=(n // tt,),
        in_specs=[packed_rows, packed_rows] + [pl.BlockSpec(a.shape, lambda i: (0, 0)) for a in consts],
        out_specs=pl.BlockSpec((tt, d), lambda i: (i, 0)),
        out_shape=jax.ShapeDtypeStruct((n, d), BF16),
        compiler_params=_params("arbitrary"),
        name="shared_expert",
    )(h2a, h2b, *consts)


def _combine_kernel(w_ref, x1_ref, sh_ref, mod_ref, ga_ref, gb_ref, gf_ref, sel_ref, o_ref):
    ffn = sh_ref[...].astype(F32)
    wh, wl = _split2(w_ref[...])
    acc = None
    for kk in range(TOP_K):
        sel = sel_ref[kk]
        wk = _tn(wh, sel) + _tn(wl, sel)
        a_lo, a_hi = _unpack_halves(ga_ref[kk])
        b_lo, b_hi = _unpack_halves(gb_ref[kk])
        parts = [a_lo * wk, b_lo * wk, a_hi * wk, b_hi * wk]
        acc = parts if acc is None else [p + q for p, q in zip(acc, parts)]
    ffn = ffn + jnp.concatenate(acc, axis=1)
    xo = x1_ref[...] + mod_ref[5:6, :] * ffn
    ms = jnp.mean(xo * xo, axis=-1, keepdims=True)
    o_ref[...] = xo * lax.rsqrt(ms + NORM_EPS) * gf_ref[...]


def _combine(wsel, x1, shared, mod, ga, gb, normf_g, tok0, seq, tt=256):
    n, d = x1.shape
    part = ga.shape[1]
    tt = min(tt, seq, part)
    per = seq // tt
    off = tok0 // tt
    dq = ga.shape[2]
    sel = jnp.asarray(np.broadcast_to(np.eye(TOP_K)[:, :, None], (TOP_K, TOP_K, dq)), BF16)
    consts = [normf_g.reshape(1, d), sel]
    const = lambda a: pl.BlockSpec(a.shape, lambda i: (0,) * a.ndim)
    rows = pl.BlockSpec((tt, d), lambda i: (off + i, 0))
    gathered = pl.BlockSpec((TOP_K, tt, dq), lambda i: (0, i, 0))
    return pl.pallas_call(
        _combine_kernel,
        grid=(part // tt,),
        in_specs=[pl.BlockSpec((TOP_K, tt), lambda i: (0, off + i)),
                  rows, rows,
                  pl.BlockSpec((None,) + mod.shape[1:], lambda i: ((off + i) // per, 0, 0)),
                  gathered, gathered] + [const(a) for a in consts],
        out_specs=rows,
        out_shape=jax.ShapeDtypeStruct((n, d), F32),
        input_output_aliases={1: 0},
        compiler_params=_params("arbitrary"),
        name="moe_combine",
    )(wsel, x1, shared, mod, ga, gb, *consts)


COMBINE_PARTS = 4


def _moe(x1, h2a, h2b, mod, eid, wsel, exp_wg, exp_wu, exp_wd, sh_wg, sh_wu, sh_wd, normf_g):
    n = h2a.shape[0]
    e = exp_wg.shape[0]
    nb = (n * TOP_K + e * (BLK - 1)) // BLK
    rank, cnt = _expert_ranks(eid, e)
    dest = _destinations(cnt, eid, rank)
    meta = _block_meta(cnt, nb)
    xs_a = _sc_scatter_rows(h2a, dest, nb * BLK)
    xs_b = _sc_scatter_rows(h2b, dest, nb * BLK)
    shared = _shared_expert(h2a, h2b, sh_wg, sh_wu, sh_wd)
    ys_a, ys_b = _experts(meta[0, :nb], meta[1, :nb], meta[2, :1], meta[3, :nb], xs_a, xs_b,
                          exp_wg, exp_wu, exp_wd)
    bsz, seq, d = x1.shape
    out = x1.reshape(n, d)
    part = n // COMBINE_PARTS
    for j in range(COMBINE_PARTS):
        idx = dest[:, j * part:(j + 1) * part].reshape(1, TOP_K * part)
        ga = _sc_gather_rows(ys_a, idx).reshape(TOP_K, part, -1)
        gb = _sc_gather_rows(ys_b, idx).reshape(TOP_K, part, -1)
        out = _combine(wsel, out, shared, mod, ga, gb, normf_g, j * part, seq)
    return out


def kernel(x, c, norm1_g, norm2_g, normf_g, w_ada, b_ada, w_in, w_out, hy_conv_w, hy_conv_b, hy_pos_w1, hy_pos_b1, hy_pos_w2, hy_pos_b2, hy_pos_w3, hy_sin_freq, hy_skip, rw_mu, rw_w0, rw_w_up, rw_a0, rw_a_up, rw_g_up, rw_k_k, rw_k_a, rw_r_k, rw_ln_w, rw_ln_b, router_w, router_bias, exp_w_gate, exp_w_up, exp_w_down, sh_w_gate, sh_w_up, sh_w_down):
    bsz, seq, d = x.shape
    depth = w_ada.shape[0]
    assert depth == 1, "the final norm is fused into the last kernel of a single layer"
    for l in range(depth):
        mod = _modulation(c, w_ada[l], b_ada[l]).reshape(bsz, -1, d)
        uhy, rkvk, lwa, g, bonus = _projection(
            x, mod, norm1_g[l], w_in[l], hy_conv_w[l], hy_conv_b[l], rw_mu[l], rw_w0[l], rw_w_up[l],
            rw_a0[l], rw_a_up[l], rw_g_up[l], rw_k_k[l], rw_k_a[l], rw_r_k[l])
        k2, ss = _hyena_filters(seq, hy_pos_w1[l], hy_pos_b1[l], hy_pos_w2[l], hy_pos_b2[l],
                                hy_pos_w3[l], hy_sin_freq[l])
        khat = _filter_spectrum(k2, ss, seq)
        z, z_col = uhy, 0
        for order in range(HYENA_ORDER):
            z = _long_conv_gate_pairs(z, z_col, uhy, (order + 1) * D_HYENA, khat, hy_skip[l], order)
            z_col = 0
        o_f, o_b = _wkv(rkvk, lwa, rw_k_a[l])
        x1, h2a, h2b, eid, wsel = _mix_out(x, mod, z, o_f, o_b, g, bonus, rw_ln_w[l], rw_ln_b[l], w_out[l],
                                           norm2_g[l], router_w[l], router_bias[l])
        x = _moe(x1, h2a, h2b, mod, eid, wsel, exp_w_gate[l], exp_w_up[l], exp_w_down[l],
                 sh_w_gate[l], sh_w_up[l], sh_w_down[l], normf_g)
        x = x.reshape(bsz, seq, d)
    return x
```

```python
import functools
import math

import jax
import jax.numpy as jnp
import numpy as np
from jax import lax
from jax.experimental import pallas as pl
from jax.experimental.pallas import tpu as pltpu
from jax.experimental.pallas import tpu_sc as plsc

F32 = jnp.float32
BF16 = jnp.bfloat16

LANES = 128
MXU_DIM = 256
VMEM_LIMIT = 56 * 1024 * 1024

D_HYENA = 512
D_RWKV = 512
HEAD = 64
N_HEADS = D_RWKV // HEAD
HYENA_ORDER = 2
FILTER_BANDS = 16
DECAY_TARGET = 1e-2
FAST_DECAY_PCT = 0.3
SLOW_DECAY_PCT = 1.5
FILTER_NORM_EPS = 1e-6
DECAY_LORA = 32
ICLR_LORA = 32
GATE_LORA = 96
GN_EPS = 64e-5
NORM_EPS = 1e-6
N_EXPERTS = 256
TOP_K = 8
N_GROUPS = 8
TOPK_GROUPS = 4
ROUTE_SCALE = 2.5
D_EXPERT = 256


def _params(*sem):
    return pltpu.CompilerParams(dimension_semantics=sem, vmem_limit_bytes=VMEM_LIMIT)


def _split2(a):
    hi = a.astype(BF16)
    lo = (a - hi.astype(F32)).astype(BF16)
    return hi, lo


def _dot(a, b):
    return jnp.dot(a, b, preferred_element_type=F32)


def _dot3(a, b):
    ah, al = _split2(a)
    bh, bl = _split2(b)
    return _dot(ah, bh) + (_dot(ah, bl) + _dot(al, bh))


def _dot_exact_rhs(a, b_bf16):
    ah, al = _split2(a)
    return _dot(ah, b_bf16) + _dot(al, b_bf16)


def _silu(x):
    return x * jax.nn.sigmoid(x)


U32 = jnp.int32


def _pack_halves(x):
    w = x.shape[1] // 2
    return pltpu.pack_elementwise([x[:, :w], x[:, w:]], packed_dtype=BF16)


def _unpack_halves(p):
    lo = pltpu.unpack_elementwise(p, index=0, packed_dtype=BF16, unpacked_dtype=F32)
    hi = pltpu.unpack_elementwise(p, index=1, packed_dtype=BF16, unpacked_dtype=F32)
    return lo, hi


def _pack_rows(x):
    packed = _pack_halves(x)
    half = packed.shape[1] // 2
    return packed[:, :half], packed[:, half:]


def _unpack_rows(a, b):
    a_lo, a_hi = _unpack_halves(a)
    b_lo, b_hi = _unpack_halves(b)
    return jnp.concatenate([a_lo.astype(BF16), b_lo.astype(BF16), a_hi.astype(BF16), b_hi.astype(BF16)], axis=1)


def _mod_kernel(c_ref, w_ref, b_ref, o_ref):
    o_ref[...] = _dot3(_silu(c_ref[...]), w_ref[...]) + b_ref[...]


def _modulation(c, w_ada, b_ada):
    bsz, d = c.shape
    n = w_ada.shape[1]
    blk = 1024
    return pl.pallas_call(
        _mod_kernel,
        grid=(n // blk,),
        in_specs=[
            pl.BlockSpec((bsz, d), lambda j: (0, 0)),
            pl.BlockSpec((d, blk), lambda j: (0, j)),
            pl.BlockSpec((1, blk), lambda j: (0, j)),
        ],
        out_specs=pl.BlockSpec((bsz, blk), lambda j: (0, j)),
        out_shape=jax.ShapeDtypeStruct((bsz, n), F32),
        compiler_params=_params("arbitrary"),
        name="adaln_mod",
    )(c, w_ada, b_ada.reshape(1, n))


def _filter_kernel(band_ref, w1_ref, b1_ref, w2_ref, b2_ref, w3_ref, freq_ref, delta_ref,
                   k_ref, ss_ref, *, seq, rows):
    half = pl.program_id(0)
    i = pl.program_id(1)
    r = lax.broadcasted_iota(jnp.int32, (rows, LANES), 0) + i * rows
    pos = jnp.where(half == 0, r, seq - r).astype(F32)
    tt = pos / float(max(seq - 1, 1))
    lane = lax.broadcasted_iota(jnp.int32, (rows, LANES), 1)
    feats = jnp.where(lane == 0, tt, jnp.sin(pos * band_ref[0:1, :] + band_ref[1:2, :]))
    freq = freq_ref[...]
    hdn = jnp.sin(freq * (_dot3(feats, w1_ref[...]) + b1_ref[...]))
    for j in range(w2_ref.shape[0]):
        hdn = jnp.sin(freq * (_dot3(hdn, w2_ref[j]) + b2_ref[j]))
    filt = _dot3(hdn, w3_ref[...])
    filt = filt * jnp.exp(-tt[:, :1] * delta_ref[...])
    valid = jnp.logical_or(half == 0, r[:, :1] > 0)
    filt = jnp.where(valid, filt, 0.0)
    k_ref[...] = filt

    @pl.when(jnp.logical_and(half == 0, i == 0))
    def _():
        ss_ref[...] = jnp.zeros_like(ss_ref)

    ss_ref[...] += jnp.broadcast_to(jnp.sum(filt * filt, axis=0, keepdims=True), ss_ref.shape)


def _hyena_filters(seq, pw1, pb1, pw2, pb2, pw3, freq):
    width = pw1.shape[1]
    ncol = HYENA_ORDER * D_HYENA
    rows = min(seq, 512)
    bands = np.zeros((2, LANES), np.float64)
    lin = np.linspace(1e-4, FILTER_BANDS - 1, FILTER_BANDS) * (2.0 * math.pi / seq)
    bands[0, 1:1 + FILTER_BANDS] = lin
    bands[1, 1:1 + FILTER_BANDS] = 0.5 * math.pi
    bands[0, 1 + FILTER_BANDS:1 + 2 * FILTER_BANDS] = -lin
    bands = jnp.asarray(bands, F32)
    deltas = np.abs(np.linspace(math.log(DECAY_TARGET) / SLOW_DECAY_PCT,
                                math.log(DECAY_TARGET) / FAST_DECAY_PCT, D_HYENA))
    deltas = jnp.asarray(np.tile(deltas, HYENA_ORDER)[None], F32)
    w1 = jnp.zeros((LANES, width), F32).at[:pw1.shape[0]].set(pw1)
    w3 = pw3.reshape(width, HYENA_ORDER, 2, D_HYENA).transpose(2, 0, 1, 3).reshape(2, width, ncol)
    nt = seq // rows
    full = lambda *shape: pl.BlockSpec(shape, lambda h, i: (0,) * len(shape))
    return pl.pallas_call(
        functools.partial(_filter_kernel, seq=seq, rows=rows),
        grid=(2, nt),
        in_specs=[
            full(2, LANES), full(LANES, width), full(1, width),
            full(pw2.shape[0], width, width), full(pw2.shape[0], 1, width),
            pl.BlockSpec((None, width, ncol), lambda h, i: (h, 0, 0)),
            full(1, width), full(1, ncol),
        ],
        out_specs=[
            pl.BlockSpec((rows, ncol), lambda h, i: (h * nt + i, 0)),
            pl.BlockSpec((8, ncol), lambda h, i: (0, 0)),
        ],
        out_shape=[jax.ShapeDtypeStruct((2 * seq, ncol), F32),
                   jax.ShapeDtypeStruct((8, ncol), F32)],
        compiler_params=_params("arbitrary", "arbitrary"),
        name="hyena_filters",
    )(bands, w1, pb1.reshape(1, width), pw2, pb2.reshape(pw2.shape[0], 1, width), w3,
      freq.reshape(1, width), deltas)


N1 = LANES
UNROLL = 8


def _dft_tables(seq):
    tables = _dft_tables_np(seq)
    return tuple(jnp.asarray(t, BF16) for t in tables[:5]) + tables[5:]


def _dft_tables_np(seq):
    m = 2 * seq
    n2 = m // N1
    n2h = n2 // 2
    n1 = np.arange(N1)[:, None, None]
    f2 = np.arange(n2)[None, :, None]
    k2 = np.arange(n2)[None, None, :]
    th = 2.0 * np.pi * (n1 * f2 / m + (k2 * f2 % n2) / n2)
    fwd_a = np.concatenate([np.cos(th), -np.sin(th)], axis=1)
    tht = np.transpose(th, (0, 2, 1))
    inv_a = np.concatenate([np.cos(tht), -np.sin(tht)], axis=2)[:, :n2h] / m
    a = np.arange(N1)
    ph = 2.0 * np.pi * np.outer(a, a) / N1
    c, s = np.cos(ph), np.sin(ph)
    fwd_b = np.block([[c, s], [-s, c]])
    inv_b = np.block([[c, -s], [s, c]])
    return fwd_a, fwd_a[:, :, :n2h], inv_a, fwd_b, inv_b, n2, n2h


def _stage_a_fwd(x_ref, wa_ref, y_ref, n2, scale=None):
    def body(i, carry):
        trips = [i * UNROLL + j for j in range(UNROLL)]
        xs = [x_ref[pl.ds(n1, wa_ref.shape[2], stride=N1), :] for n1 in trips]
        if scale is not None:
            xs = [x * scale for x in xs]
        prods = [_dot(wa_ref[n1], x.astype(BF16)) for n1, x in zip(trips, xs)]
        for n1, a in zip(trips, prods):
            y_ref[pl.ds(n1, n2, stride=2 * N1), :] = a[:n2]
            y_ref[pl.ds(N1 + n1, n2, stride=2 * N1), :] = a[n2:]
        return carry
    lax.fori_loop(0, N1 // UNROLL, body, 0)


def _filter_fft_kernel(k_ref, ss_ref, wa_ref, fb_ref, o_ref, y_ref, *, n2):
    scale = lax.rsqrt(ss_ref[0:1, :] + FILTER_NORM_EPS)
    _stage_a_fwd(k_ref, wa_ref, y_ref, n2, scale=scale)

    unr = min(UNROLL, n2)

    def body(i, carry):
        trips = [i * unr + j for j in range(unr)]
        ys = [y_ref[pl.ds(pl.multiple_of(f2 * 2 * N1, 2 * N1), 2 * N1), :].astype(BF16) for f2 in trips]
        for f2, y in zip(trips, ys):
            o_ref[f2] = _dot(fb_ref[...], y)
        return carry
    lax.fori_loop(0, n2 // unr, body, 0)


def _filter_spectrum(k2, ss, seq):
    fwd_a, _, _, fwd_b, _, n2, _ = _dft_tables(seq)
    ncol = k2.shape[1]
    nblk = ncol // LANES
    return pl.pallas_call(
        functools.partial(_filter_fft_kernel, n2=n2),
        grid=(nblk,),
        in_specs=[
            pl.BlockSpec((2 * seq, LANES), lambda c: (0, c)),
            pl.BlockSpec((8, LANES), lambda c: (0, c)),
            pl.BlockSpec(fwd_a.shape, lambda c: (0, 0, 0)),
            pl.BlockSpec(fwd_b.shape, lambda c: (0, 0)),
        ],
        out_specs=pl.BlockSpec((None, n2, 2 * N1, LANES), lambda c: (c, 0, 0, 0)),
        out_shape=jax.ShapeDtypeStruct((nblk, n2, 2 * N1, LANES), F32),
        scratch_shapes=[pltpu.VMEM((n2 * 2 * N1, LANES), F32)],
        compiler_params=_params("arbitrary"),
        name="hyena_filter_fft",
    )(k2, ss, fwd_a, fwd_b)


TILE = 8
N1_GROUPS = N1 // TILE


def _tile_tables(seq):
    _, fwd_a, inv_a, _, _, n2, n2h = _dft_tables_np(seq)
    eye = np.eye(TILE)
    fa = fwd_a.reshape(N1_GROUPS, TILE, 2 * n2, n2h)
    wa = np.einsum("qjrn,jk->qrjnk", fa, eye).reshape(N1_GROUPS, 2 * n2 * TILE, n2h * TILE)
    ia = inv_a.reshape(N1_GROUPS, TILE, n2h, 2 * n2)
    vc = np.einsum("qjnr,jk->qnjrk", ia, eye).reshape(N1_GROUPS, n2h * TILE, 2 * n2 * TILE)
    return jnp.asarray(wa, BF16), jnp.asarray(vc, BF16)


def _conv_kernel(u_ref, g_ref, skip_ref, fb_ref, ib_ref, kh_hbm, wa_hbm, vc_hbm, o_ref,
                 y_ref, kh_ref, wa_ref, vc_ref, sem, *, n2, n2h, kh_first):
    c_id, b_id = pl.program_id(0), pl.program_id(1)

    @pl.when(jnp.logical_and(c_id == 0, b_id == 0))
    def _():
        for src, dst in ((wa_hbm, wa_ref), (vc_hbm, vc_ref)):
            cp = pltpu.make_async_copy(src, dst, sem)
            cp.start()
            cp.wait()

    @pl.when(b_id == 0)
    def _():
        cp = pltpu.make_async_copy(kh_hbm.at[kh_first + c_id], kh_ref, sem)
        cp.start()
        cp.wait()

    def y_tile(rf, base):
        ri, f2 = divmod(rf, n2)
        return pl.ds(f2 * 2 * N1 + ri * N1 + base, TILE)

    def stage_a(q, carry):
        base = pl.multiple_of(q * TILE, TILE)
        x = jnp.concatenate([u_ref[pl.ds(N1 * m + base, TILE), :] for m in range(n2h)], axis=0)
        r = _dot(wa_ref[q], x.astype(BF16))
        for rf in range(2 * n2):
            y_ref[y_tile(rf, base), :] = r[rf * TILE:(rf + 1) * TILE]
        return carry
    lax.fori_loop(0, N1_GROUPS, stage_a, 0, unroll=2)

    unr = min(UNROLL, n2)

    def mid(i, carry):
        trips = [i * unr + j for j in range(unr)]
        offs = [pl.multiple_of(f2 * 2 * N1, 2 * N1) for f2 in trips]
        zs = [_dot(fb_ref[...], y_ref[pl.ds(off, 2 * N1), :].astype(BF16)) for off in offs]
        ps = []
        for f2, z in zip(trips, zs):
            zr, zi = z[:N1], z[N1:]
            kh = kh_ref[f2]
            kr, ki = kh[:N1], kh[N1:]
            ps.append(jnp.concatenate([zr * kr - zi * ki, zr * ki + zi * kr], axis=0).astype(BF16))
        gs = [_dot(ib_ref[...], p) for p in ps]
        for off, g in zip(offs, gs):
            y_ref[pl.ds(off, 2 * N1), :] = g
        return carry
    lax.fori_loop(0, n2 // unr, mid, 0)

    skip = skip_ref[...]

    def stage_c(q, carry):
        base = pl.multiple_of(q * TILE, TILE)
        g = jnp.concatenate([y_ref[y_tile(rf, base), :] for rf in range(2 * n2)], axis=0)
        conv = _dot(vc_ref[q], g.astype(BF16))
        for m in range(n2h):
            rows = pl.ds(N1 * m + base, TILE)
            o_ref[rows, :] = g_ref[rows, :] * (conv[m * TILE:(m + 1) * TILE] + u_ref[rows, :] * skip)
        return carry
    lax.fori_loop(0, N1_GROUPS, stage_c, 0, unroll=2)


def _long_conv_gate(u, u_col, gate, gate_col, khat, skip, order):
    bsz, seq, _ = u.shape
    ch = D_HYENA
    _, _, _, fwd_b, inv_b, n2, n2h = _dft_tables(seq)
    wa, vc = _tile_tables(seq)
    nblk = ch // LANES
    const = lambda a: pl.BlockSpec(a.shape, lambda c, b: (0,) * a.ndim)
    at = lambda col: pl.BlockSpec((None, seq, LANES), lambda c, b: (b, 0, col // LANES + c))
    hbm = pl.BlockSpec(memory_space=pl.ANY)
    return pl.pallas_call(
        functools.partial(_conv_kernel, n2=n2, n2h=n2h, kh_first=order * nblk),
        grid=(nblk, bsz),
        in_specs=[
            at(u_col), at(gate_col),
            pl.BlockSpec((1, LANES), lambda c, b: (0, c)),
            const(fwd_b), const(inv_b), hbm, hbm, hbm,
        ],
        out_specs=at(0),
        out_shape=jax.ShapeDtypeStruct((bsz, seq, ch), F32),
        scratch_shapes=[pltpu.VMEM((n2 * 2 * N1, LANES), F32), pltpu.VMEM(khat.shape[1:], F32),
                        pltpu.VMEM(wa.shape, BF16), pltpu.VMEM(vc.shape, BF16), pltpu.SemaphoreType.DMA(())],
        compiler_params=_params("arbitrary", "arbitrary"),
        name=f"hyena_conv{order}",
    )(u, gate, skip[order].reshape(1, ch), fwd_b, inv_b, khat, wa, vc)


HALO = 8


def _shift_rows(p, k):
    return pltpu.roll(p, k % p.shape[0], axis=0)


def _proj_kernel(xp_ref, x_ref, xn_ref, mod_ref, g1_ref, why_ref, wrkv_ref, wlora_ref,
                 cw_ref, cb_ref, murkv_ref, mulora_ref, w0_ref, a0_ref, wwah_ref, wwal_ref, gup_ref,
                 kk_ref, ka_ref, rk_ref, ones_ref,
                 uhy_ref, rkvk_ref, lwa_ref, g_ref, bonus_ref,
                 *, tt, nt):
    i = pl.program_id(1)
    xe = jnp.concatenate([xp_ref[...], x_ref[...], xn_ref[...]], axis=0)
    ms = jnp.mean(xe * xe, axis=-1, keepdims=True)
    h = xe * lax.rsqrt(ms + NORM_EPS) * g1_ref[...]
    h = h * (1.0 + mod_ref[1:2, :]) + mod_ref[0:1, :]
    row = lax.broadcasted_iota(jnp.int32, (tt + 2 * HALO, 1), 0)
    inside = jnp.logical_and(jnp.logical_or(row >= HALO, i > 0),
                             jnp.logical_or(row < tt + HALO, i < nt - 1))
    hb = jnp.where(inside, h, 0.0).astype(BF16)
    mid = slice(HALO, tt + HALO)

    p = _dot(hb, why_ref[...])
    u = (_shift_rows(p, 1) * cw_ref[0:1, :] + p * cw_ref[1:2, :]
         + _shift_rows(p, -1) * cw_ref[2:3, :] + cb_ref[...])
    uhy_ref[...] = u[mid]

    p = _dot(hb, wrkv_ref[...])
    p = p + murkv_ref[...] * (0.5 * (_shift_rows(p, 1) + _shift_rows(p, -1)) - p)
    p = p[mid]
    c = D_RWKV
    r, k, v = p[:, :c], p[:, c:2 * c], p[:, 2 * c:]
    rkvk_ref[:, :3 * c] = p

    q = _dot(hb, wlora_ref[...])
    q = q + mulora_ref[...] * (0.5 * (_shift_rows(q, 1) + _shift_rows(q, -1)) - q)
    q = q[mid]
    wa = q[:, :LANES]
    lane = lax.broadcasted_iota(jnp.int32, wa.shape, 1)
    wa = jnp.where(lane < 2 * DECAY_LORA, jnp.tanh(wa), wa)
    wah, wal = _split2(wa)
    up = _dot(wah, wwah_ref[...]) + (_dot(wah, wwal_ref[...]) + _dot(wal, wwah_ref[...]))
    lw = -math.exp(-0.5) * jax.nn.sigmoid(w0_ref[...] + up[:, :2 * c])
    a = jax.nn.sigmoid(a0_ref[...] + up[:, 2 * c:])
    for dd in range(2):
        lwa_ref[:, 2 * dd * c:(2 * dd + 1) * c] = lw[:, dd * c:(dd + 1) * c]
        lwa_ref[:, (2 * dd + 1) * c:(2 * dd + 2) * c] = a[:, dd * c:(dd + 1) * c]
    g_ref[...] = _dot3(jax.nn.sigmoid(q[:, LANES:]), gup_ref[...])

    ones = ones_ref[...]
    kk = k * kk_ref[...]
    nrm = jnp.sqrt(_dot_exact_rhs(kk * kk, ones))
    rkvk_ref[:, 3 * c:] = kk / jnp.maximum(nrm, 1e-12)
    ka = ka_ref[...]
    ksum = k * (2.0 + (a[:, :c] + a[:, c:] - 2.0) * ka)
    bonus_ref[...] = _dot_exact_rhs(r * ksum * rk_ref[...], ones) * v


def _head_ones():
    hid = np.arange(D_RWKV) // HEAD
    return jnp.asarray(hid[:, None] == hid[None, :], BF16)


def _projection(x, mod, norm1_g, w_in, hy_conv_w, hy_conv_b, rw_mu, rw_w0, rw_w_up, rw_a0,
                rw_a_up, rw_g_up, rw_k_k, rw_k_a, rw_r_k, tt=512):
    bsz, seq, d = x.shape
    tt = min(tt, seq)
    nt = seq // tt
    c = D_RWKV
    hy = (HYENA_ORDER + 1) * D_HYENA
    nlora = 2 * LANES
    w_hy = w_in[:, :hy].astype(BF16)
    w_rkv = w_in[:, hy:hy + 3 * c].astype(BF16)
    w_lora = jnp.zeros((d, nlora), F32).at[:, :w_in.shape[1] - hy - 3 * c].set(w_in[:, hy + 3 * c:]).astype(BF16)
    mu_rkv = rw_mu[:3 * c].reshape(1, 3 * c)
    mu_lora = jnp.zeros((1, nlora), F32).at[0, :rw_mu.shape[0] - 3 * c].set(rw_mu[3 * c:])
    wwa = jnp.zeros((LANES, 4 * c), F32)
    for dd in range(2):
        wwa = wwa.at[dd * DECAY_LORA:(dd + 1) * DECAY_LORA, dd * c:(dd + 1) * c].set(rw_w_up[dd])
        wwa = wwa.at[2 * DECAY_LORA + dd * ICLR_LORA:2 * DECAY_LORA + (dd + 1) * ICLR_LORA,
                     2 * c + dd * c:2 * c + (dd + 1) * c].set(rw_a_up[dd])
    gup = jnp.zeros((LANES, c), F32).at[:GATE_LORA].set(rw_g_up)
    row = lambda a: a.reshape(1, -1)

    nb8 = seq // HALO
    tb = tt // HALO
    const = lambda a: pl.BlockSpec(a.shape, lambda b, i: (0,) * a.ndim, pipeline_mode=pl.Buffered(1))
    tile = lambda w: pl.BlockSpec((None, tt, w), lambda b, i: (b, i, 0))
    ins = [
        (x, pl.BlockSpec((None, HALO, d), lambda b, i: (b, jnp.maximum(i * tb - 1, 0), 0))),
        (x, pl.BlockSpec((None, tt, d), lambda b, i: (b, i, 0))),
        (x, pl.BlockSpec((None, HALO, d), lambda b, i: (b, jnp.minimum((i + 1) * tb, nb8 - 1), 0))),
        (mod, pl.BlockSpec((None,) + mod.shape[1:], lambda b, i: (b, 0, 0))),
    ]
    consts = [row(norm1_g), w_hy, w_rkv, w_lora, hy_conv_w, row(hy_conv_b), mu_rkv, mu_lora,
              row(rw_w0), row(rw_a0), *_split2(wwa), gup, row(rw_k_k), row(rw_k_a), row(rw_r_k), _head_ones()]
    ins += [(a, const(a)) for a in consts]
    widths = [hy, 4 * c, 4 * c, c, c]
    return pl.pallas_call(
        functools.partial(_proj_kernel, tt=tt, nt=nt),
        grid=(bsz, nt),
        in_specs=[s for _, s in ins],
        out_specs=[tile(w) for w in widths],
        out_shape=[jax.ShapeDtypeStruct((bsz, seq, w), F32) for w in widths],
        compiler_params=_params("arbitrary", "arbitrary"),
        name="input_projection",
    )(*[a for a, _ in ins])


CHUNK = HEAD
GROUP = MXU_DIM // HEAD


def _nt(a, b):
    return lax.dot_general(a, b, (((1,), (1,)), ((), ())), preferred_element_type=F32)


def _tn(a, b):
    return lax.dot_general(a, b, (((0,), (0,)), ((), ())), preferred_element_type=F32)


def _wkv_direction(r, k, v, kk, lw, a, ka, s_ref, reverse):
    c = CHUNK
    ti = lax.broadcasted_iota(jnp.int32, (c, c), 0)
    si = lax.broadcasted_iota(jnp.int32, (c, c), 1)
    tri = (si >= ti) if reverse else (si <= ti)
    cum = _dot_exact_rhs_lhs(jnp.where(tri, 1.0, 0.0).astype(BF16), lw)
    tot = jnp.sum(lw, axis=0, keepdims=True)
    w_incl = jnp.exp(cum)
    w_prev = jnp.exp(cum - lw)
    w_inv = jnp.exp(-cum)
    w_end = jnp.exp(tot - cum)
    w_tot = jnp.exp(tot)
    kd = k * (1.0 + (a - 1.0) * ka)
    b = kk * a
    a_w = -kk * w_prev
    r_w = r * w_incl
    b_w = b * w_inv
    k_w = kd * w_inv
    b_e = b * w_end
    k_e = kd * w_end

    m = MXU_DIM
    ri = lax.broadcasted_iota(jnp.int32, (m, m), 0)
    ci = lax.broadcasted_iota(jnp.int32, (m, m), 1)
    head_mask = (ri // HEAD) == (ci // HEAD)
    tl = lax.broadcasted_iota(jnp.int32, (c, m), 0)
    sl = lax.broadcasted_iota(jnp.int32, (c, m), 1) % c
    strict = (sl > tl) if reverse else (sl < tl)
    incl = (sl >= tl) if reverse else (sl <= tl)
    eye = jnp.where(sl == tl, 1.0, 0.0)
    both = lambda top, bot: jnp.concatenate([top, bot], axis=0)

    def stack(xg):
        xb = xg.astype(BF16)
        return jnp.where(head_mask, jnp.concatenate([xb] * GROUP, axis=0), jnp.zeros((), BF16))

    streams = []
    for g in range(D_RWKV // m):
        sl_g = slice(g * m, (g + 1) * m)
        streams.append(dict(
            ar=both(a_w[:, sl_g], r_w[:, sl_g]).astype(BF16),
            b_st=stack(b_w[:, sl_g]), k_st=stack(k_w[:, sl_g]), v_st=stack(v[:, sl_g]),
            v=v[:, sl_g], bk=both(b_e[:, sl_g], k_e[:, sl_g]).astype(BF16),
            w_tot=w_tot[:, sl_g], s_ref=s_ref.at[g],
            strict=strict, incl=incl, eye=eye, head_mask=head_mask, stack=stack))
    return streams


def _wkv_streams_step(streams):
    c = CHUNK
    both = lambda top, bot: jnp.concatenate([top, bot], axis=0)
    for st in streams:
        st["s"] = st["s_ref"][...]
        st["xb"] = _nt(st["ar"], st["b_st"])
        st["xk"] = _nt(st["ar"], st["k_st"])
        st["xs"] = _nt(st["ar"], st["s"].astype(BF16))
    for st in streams:
        m_k = both(jnp.where(st["strict"], st["xk"][:c], 0.0), jnp.where(st["incl"], st["xk"][c:], 0.0))
        st["kv"] = _dot(m_k.astype(BF16), st["v_st"])
        st["rhs"] = st["xs"][:c] + st["kv"][:c]
        st["pw"] = jnp.where(st["strict"], st["xb"][:c], 0.0)
        st["t"] = st["eye"] + st["pw"]
        st["p_st"] = st["stack"](st["pw"])
    levels = int(math.log2(c)) - 1
    for st in streams:
        st["pw"] = _dot(st["pw"].astype(BF16), st["p_st"])
        st["p_st"] = st["stack"](st["pw"])
    for lvl in range(1, levels + 1):
        for st in streams:
            if lvl < levels:
                prod = _dot(both(st["pw"], st["t"]).astype(BF16), st["p_st"])
                st["pw"] = prod[:c]
                st["t"] = st["t"] + prod[c:]
                st["p_st"] = st["stack"](st["pw"])
            else:
                st["t"] = st["t"] + _dot(st["t"].astype(BF16), st["p_st"])
    for st in streams:
        st["u"] = _dot(st["t"].astype(BF16), st["stack"](st["rhs"]))
    outs = []
    for st in streams:
        m_rb = jnp.where(st["incl"], st["xb"][c:], 0.0)
        outs.append(st["xs"][c:] + _dot(m_rb.astype(BF16), st["stack"](st["u"])) + st["kv"][c:])
        uv = both(st["u"], st["v"]).astype(BF16)
        st["s_ref"][...] = st["s"] * st["w_tot"] + jnp.where(st["head_mask"], _tn(uv, st["bk"]), 0.0)
    return outs


def _dot_exact_rhs_lhs(tri_bf16, x):
    xh, xl = _split2(x)
    return _dot(tri_bf16, xh) + _dot(tri_bf16, xl)


def _wkv_kernel(rkvk_f, lwa_f, rkvk_b, lwa_b, ka_ref, of_ref, ob_ref, s_ref, *, nch):
    @pl.when(pl.program_id(1) == 0)
    def _():
        s_ref[...] = jnp.zeros_like(s_ref)

    ka = ka_ref[...]
    c = D_RWKV

    def operands(rkvk_ref, lwa_ref, rows):
        x = rkvk_ref[rows, :]
        la = lwa_ref[rows, :]
        return x[:, :c], x[:, c:2 * c], x[:, 2 * c:3 * c], x[:, 3 * c:], la[:, :c], la[:, c:]

    for ci in range(nch):
        rows_f = slice(ci * CHUNK, (ci + 1) * CHUNK)
        rows_b = slice((nch - 1 - ci) * CHUNK, (nch - ci) * CHUNK)
        fwd = _wkv_direction(*operands(rkvk_f, lwa_f, rows_f), ka, s_ref.at[0], False)
        bwd = _wkv_direction(*operands(rkvk_b, lwa_b, rows_b), ka, s_ref.at[1], True)
        outs = _wkv_streams_step(fwd + bwd)
        of_ref[rows_f, :] = jnp.concatenate(outs[:len(fwd)], axis=1)
        ob_ref[rows_b, :] = jnp.concatenate(outs[len(fwd):], axis=1)


WKV_CHUNKS_PER_STEP = 4


def _wkv(rkvk, lwa, rw_k_a):
    bsz, seq, _ = rkvk.shape
    c = D_RWKV
    nch = WKV_CHUNKS_PER_STEP if seq % (WKV_CHUNKS_PER_STEP * CHUNK) == 0 else 1
    rows = nch * CHUNK
    nb = seq // rows
    fwd = lambda w, lane_blk: pl.BlockSpec((None, rows, w), lambda b, j: (b, j, lane_blk))
    bwd = lambda w, lane_blk: pl.BlockSpec((None, rows, w), lambda b, j: (b, nb - 1 - j, lane_blk))
    return pl.pallas_call(
        functools.partial(_wkv_kernel, nch=nch),
        grid=(bsz, nb),
        in_specs=[fwd(4 * c, 0), fwd(2 * c, 0), bwd(4 * c, 0), bwd(2 * c, 1),
                  pl.BlockSpec((1, c), lambda b, j: (0, 0))],
        out_specs=[fwd(c, 0), bwd(c, 0)],
        out_shape=[jax.ShapeDtypeStruct((bsz, seq, c), F32)] * 2,
        scratch_shapes=[pltpu.VMEM((2, c // MXU_DIM, MXU_DIM, MXU_DIM), F32)],
        compiler_params=_params("arbitrary", "arbitrary"),
        name="wkv7_chunked",
    )(rkvk, lwa, rkvk, lwa, rw_k_a.reshape(1, c))


NEG_INF = float("-inf")


def _first_max(vals, idx, size):
    m = jnp.max(vals, axis=0, keepdims=True)
    i = jnp.min(jnp.where(vals == m, idx, size), axis=0, keepdims=True)
    return m, i


def _route(scores, biased):
    e, tt = scores.shape
    per = e // N_GROUPS
    rowl = lax.broadcasted_iota(jnp.int32, (per, tt), 0)
    gs = []
    for g in range(N_GROUPS):
        blk = biased[g * per:(g + 1) * per]
        m1, i1 = _first_max(blk, rowl, per)
        m2 = jnp.max(jnp.where(rowl == i1, NEG_INF, blk), axis=0, keepdims=True)
        gs.append(m1 + m2)
    cur = jnp.concatenate(gs, axis=0)
    growl = lax.broadcasted_iota(jnp.int32, (N_GROUPS, tt), 0)
    gsel = jnp.zeros((N_GROUPS, tt), F32)
    for _ in range(TOPK_GROUPS):
        _, ig = _first_max(cur, growl, N_GROUPS)
        hit = growl == ig
        gsel = jnp.where(hit, 1.0, gsel)
        cur = jnp.where(hit, NEG_INF, cur)
    emask = jnp.concatenate([jnp.broadcast_to(gsel[g:g + 1], (per, tt)) for g in range(N_GROUPS)], axis=0)
    masked = jnp.where(emask > 0.5, biased, NEG_INF)
    row = lax.broadcasted_iota(jnp.int32, (e, tt), 0)
    ids, ws = [], []
    for _ in range(TOP_K):
        _, ie = _first_max(masked, row, e)
        hit = row == ie
        ids.append(ie)
        ws.append(jnp.sum(jnp.where(hit, scores, 0.0), axis=0, keepdims=True))
        masked = jnp.where(hit, NEG_INF, masked)
    w = jnp.concatenate(ws, axis=0)
    w = w / jnp.sum(w, axis=0, keepdims=True) * ROUTE_SCALE
    return jnp.concatenate(ids, axis=0), w


def _mixout_kernel(x_ref, mod_ref, yhy_ref, of_ref, ob_ref, g_ref, bonus_ref, lnw_ref, lnb_ref,
                   ones_ref, wout_ref, g2n_ref, rwth_ref, rwtl_ref, bias_ref,
                   x1_ref, h2a_ref, h2b_ref, eid_ref, wsel_ref):
    ones = ones_ref[...]
    s = of_ref[...] + ob_ref[...]
    mean = _dot_exact_rhs(s, ones) * (1.0 / HEAD)
    dlt = s - mean
    var = _dot_exact_rhs(dlt * dlt, ones) * (1.0 / HEAD)
    sn = dlt * lax.rsqrt(var + GN_EPS) * lnw_ref[...] + lnb_ref[...]
    yrw = (sn + bonus_ref[...]) * g_ref[...]
    ch = yhy_ref.shape[-1]
    mix = _dot(yhy_ref[...].astype(BF16), wout_ref[:ch, :]) + _dot(yrw.astype(BF16), wout_ref[ch:, :])
    x1 = x_ref[...] + mod_ref[2:3, :] * mix
    x1_ref[...] = x1
    ms = jnp.mean(x1 * x1, axis=-1, keepdims=True)
    h2 = x1 * lax.rsqrt(ms + NORM_EPS) * g2n_ref[...]
    h2 = h2 * (1.0 + mod_ref[4:5, :]) + mod_ref[3:4, :]
    h2a_ref[...], h2b_ref[...] = _pack_rows(h2)
    rh, rl = rwth_ref[...], rwtl_ref[...]
    hh, hl = _split2(h2)
    logits = _nt(rh, hh) + (_nt(rh, hl) + _nt(rl, hh))
    scores = jax.nn.sigmoid(logits)
    ids, w = _route(scores, scores + bias_ref[...])
    eid_ref[...] = ids
    wsel_ref[...] = w


def _mix_out(x, mod, yhy, o_f, o_b, g, bonus, ln_w, ln_b, w_out, norm2_g, router_w, router_bias, tt=1024):
    bsz, seq, d = x.shape
    tt = min(tt, seq)
    nt = seq // tt
    n = bsz * seq
    c = D_RWKV
    e = router_w.shape[1]
    row = lambda a: a.reshape(1, -1)
    consts = [row(ln_w), row(ln_b), _head_ones(), w_out.astype(BF16), row(norm2_g), *_split2(router_w.T),
              jnp.broadcast_to(router_bias.reshape(e, 1), (e, tt))]
    const = lambda a: pl.BlockSpec(a.shape, lambda b, i: (0,) * a.ndim, pipeline_mode=pl.Buffered(1))
    tile = lambda w: pl.BlockSpec((None, tt, w), lambda b, i: (b, i, 0))
    flat = lambda rows, dt: jax.ShapeDtypeStruct((rows, n), dt)
    return pl.pallas_call(
        _mixout_kernel,
        grid=(bsz, nt),
        in_specs=[tile(d), pl.BlockSpec((None,) + mod.shape[1:], lambda b, i: (b, 0, 0))]
        + [tile(c)] * 5 + [const(a) for a in consts],
        out_specs=[tile(d), pl.BlockSpec((tt, d // 4), lambda b, i: (b * nt + i, 0)),
                   pl.BlockSpec((tt, d // 4), lambda b, i: (b * nt + i, 0)),
                   pl.BlockSpec((TOP_K, tt), lambda b, i: (0, b * nt + i)),
                   pl.BlockSpec((TOP_K, tt), lambda b, i: (0, b * nt + i))],
        out_shape=[jax.ShapeDtypeStruct((bsz, seq, d), F32), jax.ShapeDtypeStruct((n, d // 4), U32),
                   jax.ShapeDtypeStruct((n, d // 4), U32),
                   flat(TOP_K, jnp.int32), flat(TOP_K, F32)],
        compiler_params=_params("arbitrary", "arbitrary"),
        name="mix_out_router",
    )(x, mod, yhy, o_f, o_b, g, bonus, *consts)


BLK = 512
BLK_SHIFT = 9


def _multi_hot(eid, e):
    row = lax.broadcasted_iota(jnp.int32, (e, eid.shape[1]), 0)
    mh = jnp.zeros((e, eid.shape[1]), F32)
    for kk in range(TOP_K):
        mh = mh + jnp.where(row == eid[kk:kk + 1, :], 1.0, 0.0)
    return row, mh


def _lookup(row, eid, table):
    return jnp.concatenate(
        [jnp.sum(jnp.where(row == eid[kk:kk + 1, :], table, 0.0), axis=0, keepdims=True)
         for kk in range(TOP_K)], axis=0)


def _rank_kernel(eid_ref, rank_ref, cnt_ref, *, e):
    @pl.when(pl.program_id(0) == 0)
    def _():
        cnt_ref[...] = jnp.zeros_like(cnt_ref)

    eid = eid_ref[...]
    tt = eid.shape[1]
    row, mh = _multi_hot(eid, e)
    mhb = mh.astype(BF16)
    si = lax.broadcasted_iota(jnp.int32, (tt, tt), 0)
    ti = lax.broadcasted_iota(jnp.int32, (tt, tt), 1)
    earlier = _dot(mhb, jnp.where(si < ti, 1.0, 0.0).astype(BF16))
    cnt = cnt_ref[...]
    full = earlier + jnp.concatenate([cnt] * (tt // LANES), axis=1)
    rank_ref[...] = _lookup(row, eid, full).astype(jnp.int32)
    cnt_ref[...] = cnt + _dot(mhb, jnp.ones((tt, LANES), BF16))


def _expert_ranks(eid, e, tt=512):
    n = eid.shape[1]
    tt = min(tt, n)
    return pl.pallas_call(
        functools.partial(_rank_kernel, e=e),
        grid=(n // tt,),
        in_specs=[pl.BlockSpec((TOP_K, tt), lambda i: (0, i))],
        out_specs=[pl.BlockSpec((TOP_K, tt), lambda i: (0, i)),
                   pl.BlockSpec((e, LANES), lambda i: (0, 0))],
        out_shape=[jax.ShapeDtypeStruct((TOP_K, n), jnp.int32), jax.ShapeDtypeStruct((e, LANES), F32)],
        compiler_params=_params("arbitrary"),
        name="expert_ranks",
    )(eid)


def _block_offsets(cnt):
    e = cnt.shape[0]
    nblk = ((cnt.astype(jnp.int32) + (BLK - 1)) >> BLK_SHIFT).astype(F32)
    ri = lax.broadcasted_iota(jnp.int32, (e, e), 0)
    ci = lax.broadcasted_iota(jnp.int32, (e, e), 1)
    tril = jnp.where(ci <= ri, 1.0, 0.0).astype(BF16)
    nh, nl = _split2(nblk)
    return nblk, _dot(tril, nh) + _dot(tril, nl)


def _dest_kernel(cnt_ref, eid_ref, rank_ref, dest_ref):
    nblk, end = _block_offsets(cnt_ref[...])
    off = (end - nblk) * float(BLK)
    eid = eid_ref[...]
    tt = eid.shape[1]
    row = lax.broadcasted_iota(jnp.int32, (off.shape[0], tt), 0)
    table = jnp.concatenate([off] * (tt // LANES), axis=1)
    dest_ref[...] = _lookup(row, eid, table).astype(jnp.int32) + rank_ref[...]


def _destinations(cnt, eid, rank, tt=512):
    n = eid.shape[1]
    tt = min(tt, n)
    blk = pl.BlockSpec((TOP_K, tt), lambda i: (0, i))
    return pl.pallas_call(
        _dest_kernel,
        grid=(n // tt,),
        in_specs=[pl.BlockSpec(cnt.shape, lambda i: (0, 0)), blk, blk],
        out_specs=blk,
        out_shape=jax.ShapeDtypeStruct((TOP_K, n), jnp.int32),
        compiler_params=_params("arbitrary"),
        name="expert_destinations",
    )(cnt, eid, rank)


def _meta_kernel(cnt_ref, meta_ref, *, nbp):
    cnt = cnt_ref[...]
    e = cnt.shape[0]
    nblk, end = _block_offsets(cnt)
    rep = lambda a, w: jnp.concatenate([a] * (w // LANES), axis=1)
    b = lax.broadcasted_iota(jnp.int32, (e, nbp), 1).astype(F32)
    blk_e = jnp.minimum(jnp.sum(jnp.where(rep(end, nbp) <= b, 1.0, 0.0), axis=0, keepdims=True), float(e - 1))
    row = lax.broadcasted_iota(jnp.int32, (e, nbp), 0).astype(F32)
    mine = row == blk_e
    left = rep(cnt + (end - nblk) * float(BLK), nbp) - b * float(BLK)
    nvalid = jnp.clip(jnp.sum(jnp.where(mine, left, 0.0), axis=0, keepdims=True), 0.0, float(BLK))
    nused = jnp.max(rep(end, nbp), axis=0, keepdims=True)
    later = jnp.logical_and(row > blk_e, rep(nblk, nbp) > 0.0)
    nxt = jnp.min(jnp.where(later, row, float(e)), axis=0, keepdims=True)
    nxt = jnp.where(nxt >= float(e), -1.0, nxt)
    meta_ref[...] = jnp.concatenate([blk_e, nvalid, nused, nxt, jnp.zeros((4, nbp), F32)],
                                    axis=0).astype(jnp.int32)


def _block_meta(cnt, nb):
    nbp = -(-nb // LANES) * LANES
    return pl.pallas_call(
        functools.partial(_meta_kernel, nbp=nbp),
        out_shape=jax.ShapeDtypeStruct((8, nbp), jnp.int32),
        compiler_params=pltpu.CompilerParams(vmem_limit_bytes=VMEM_LIMIT),
        name="expert_block_meta",
    )(cnt)


SC_WINDOW = 128


def _sc_mesh():
    return plsc.VectorSubcoreMesh(core_axis_name="core", subcore_axis_name="subcore")


def _sc_scatter_rows(rows, idx, nrows):
    n, width = rows.shape

    @pl.kernel(out_type=jax.ShapeDtypeStruct((nrows, width), rows.dtype), mesh=_sc_mesh())
    def scatter(rows_hbm, idx_hbm, out_hbm):
        def body(rows_vmem, idx_vmem):
            pltpu.sync_copy(rows_vmem, out_hbm.at[idx_vmem.at[0]])

        pltpu.emit_pipeline(
            body,
            grid=(n // SC_WINDOW, idx.shape[0]),
            in_specs=[pl.BlockSpec((SC_WINDOW, width), index_map=lambda i, k: (i, 0)),
                      pl.BlockSpec((1, SC_WINDOW), index_map=lambda i, k: (k, i))],
            out_specs=[],
            core_axis_name=("core", "subcore"),
            dimension_semantics=(pltpu.PARALLEL, pltpu.ARBITRARY),
        )(rows_hbm, idx_hbm)

    return scatter(rows, idx)


def _sc_gather_rows(src, idx):
    num = idx.shape[1]
    width = src.shape[1]

    @pl.kernel(out_type=jax.ShapeDtypeStruct((num, width), src.dtype), mesh=_sc_mesh())
    def gather(src_hbm, idx_hbm, out_hbm):
        def body(idx_vmem, out_vmem):
            pltpu.sync_copy(src_hbm.at[idx_vmem.at[0]], out_vmem)

        pltpu.emit_pipeline(
            body,
            grid=(num // SC_WINDOW,),
            in_specs=[pl.BlockSpec((1, SC_WINDOW), index_map=lambda i: (0, i))],
            out_specs=[pl.BlockSpec((SC_WINDOW, width), index_map=lambda i: (i, 0))],
            core_axis_name=("core", "subcore"),
            dimension_semantics=(pltpu.PARALLEL,),
        )(idx_hbm, out_hbm)

    return gather(src, idx)


def _experts_kernel(be_ref, nv_ref, nxt_ref, xa_ref, xb_ref, wg_hbm, wu_hbm, wd_hbm, oa_ref, ob_ref,
                    wgf, wuf, wdf, wgb, wub, wdb, sems, slot_ref):
    b = pl.program_id(0)

    def fetch(expert, slot):
        return [pltpu.make_async_copy(src.at[expert], dst.at[slot], sems.at[slot])
                for src, dst in ((wg_hbm, wgf), (wu_hbm, wuf), (wd_hbm, wdf))]

    @pl.when(b == 0)
    def _():
        slot_ref[0] = 0
        for cp in fetch(be_ref[0], 0):
            cp.start()

    prev = be_ref[jnp.maximum(b - 1, 0)]

    @pl.when(jnp.logical_or(b == 0, be_ref[b] != prev))
    def _():
        slot = slot_ref[0]
        for cp in fetch(be_ref[b], slot):
            cp.wait()

        @pl.when(nxt_ref[b] >= 0)
        def _():
            for cp in fetch(nxt_ref[b], 1 - slot):
                cp.start()

        wgb[...] = wgf[slot].astype(BF16)
        wub[...] = wuf[slot].astype(BF16)
        wdb[...] = wdf[slot].astype(BF16)
        slot_ref[0] = 1 - slot

    valid = lax.broadcasted_iota(jnp.int32, (xa_ref.shape[0], 1), 0) < nv_ref[b]
    zero = jnp.zeros((), U32)
    x = _unpack_rows(jnp.where(valid, xa_ref[...], zero), jnp.where(valid, xb_ref[...], zero))
    act = _silu(_dot(x, wgb[...])) * _dot(x, wub[...])
    oa_ref[...], ob_ref[...] = _pack_rows(_dot(act.astype(BF16), wdb[...]))


def _experts(blk_e, nvalid, nused, nxt_e, xs_a, xs_b, wg, wu, wd):
    p, dq = xs_a.shape
    d, de = wg.shape[1], wg.shape[2]
    rows_in = pl.BlockSpec((BLK, dq), lambda b, be, nv, nx: (b, 0))
    hbm = pl.BlockSpec(memory_space=pl.ANY)
    return pl.pallas_call(
        _experts_kernel,
        grid_spec=pltpu.PrefetchScalarGridSpec(
            num_scalar_prefetch=3,
            grid=(nused,),
            in_specs=[rows_in, rows_in, hbm, hbm, hbm],
            out_specs=[rows_in, rows_in],
            scratch_shapes=[pltpu.VMEM((2, d, de), F32), pltpu.VMEM((2, d, de), F32), pltpu.VMEM((2, de, d), F32),
                            pltpu.VMEM((d, de), BF16), pltpu.VMEM((d, de), BF16), pltpu.VMEM((de, d), BF16),
                            pltpu.SemaphoreType.DMA((2,)), pltpu.SMEM((1,), jnp.int32)],
        ),
        out_shape=[jax.ShapeDtypeStruct((p, dq), U32)] * 2,
        compiler_params=_params("arbitrary"),
        name="moe_experts",
    )(blk_e, nvalid, nxt_e, xs_a, xs_b, wg, wu, wd)


def _shared_kernel(ha_ref, hb_ref, sg_ref, su_ref, sd_ref, o_ref):
    hb = _unpack_rows(ha_ref[...], hb_ref[...])
    act = _silu(_dot(hb, sg_ref[...])) * _dot(hb, su_ref[...])
    o_ref[...] = _dot(act.astype(BF16), sd_ref[...]).astype(o_ref.dtype)


def _shared_expert(h2a, h2b, sh_wg, sh_wu, sh_wd, tt=512):
    n, dp = h2a.shape
    d = sh_wg.shape[0]
    tt = min(tt, n)
    consts = [sh_wg.astype(BF16), sh_wu.astype(BF16), sh_wd.astype(BF16)]
    packed_rows = pl.BlockSpec((tt, dp), lambda i: (i, 0))
    return pl.pallas_call(
        _shared_kernel,
        grid=(n // tt,),
        in_specs=[packed_rows, packed_rows] + [pl.BlockSpec(a.shape, lambda i: (0, 0)) for a in consts],
        out_specs=pl.BlockSpec((tt, d), lambda i: (i, 0)),
        out_shape=jax.ShapeDtypeStruct((n, d), BF16),
        compiler_params=_params("arbitrary"),
        name="shared_expert",
    )(h2a, h2b, *consts)


def _combine_kernel(w_ref, x1_ref, sh_ref, mod_ref, ga_ref, gb_ref, gf_ref, sel_ref, o_ref):
    ffn = sh_ref[...].astype(F32)
    wh, wl = _split2(w_ref[...])
    acc = None
    for kk in range(TOP_K):
        sel = sel_ref[kk]
        wk = _tn(wh, sel) + _tn(wl, sel)
        a_lo, a_hi = _unpack_halves(ga_ref[kk])
        b_lo, b_hi = _unpack_halves(gb_ref[kk])
        parts = [a_lo * wk, b_lo * wk, a_hi * wk, b_hi * wk]
        acc = parts if acc is None else [p + q for p, q in zip(acc, parts)]
    ffn = ffn + jnp.concatenate(acc, axis=1)
    xo = x1_ref[...] + mod_ref[5:6, :] * ffn
    ms = jnp.mean(xo * xo, axis=-1, keepdims=True)
    o_ref[...] = xo * lax.rsqrt(ms + NORM_EPS) * gf_ref[...]


def _combine(wsel, x1, shared, mod, ga, gb, normf_g, tok0, seq, tt=256):
    n, d = x1.shape
    part = ga.shape[1]
    tt = min(tt, seq, part)
    per = seq // tt
    off = tok0 // tt
    dq = ga.shape[2]
    sel = jnp.asarray(np.broadcast_to(np.eye(TOP_K)[:, :, None], (TOP_K, TOP_K, dq)), BF16)
    consts = [normf_g.reshape(1, d), sel]
    const = lambda a: pl.BlockSpec(a.shape, lambda i: (0,) * a.ndim)
    rows = pl.BlockSpec((tt, d), lambda i: (off + i, 0))
    gathered = pl.BlockSpec((TOP_K, tt, dq), lambda i: (0, i, 0))
    return pl.pallas_call(
        _combine_kernel,
        grid=(part // tt,),
        in_specs=[pl.BlockSpec((TOP_K, tt), lambda i: (0, off + i)),
                  rows, rows,
                  pl.BlockSpec((None,) + mod.shape[1:], lambda i: ((off + i) // per, 0, 0)),
                  gathered, gathered] + [const(a) for a in consts],
        out_specs=rows,
        out_shape=jax.ShapeDtypeStruct((n, d), F32),
        input_output_aliases={1: 0},
        compiler_params=_params("arbitrary"),
        name="moe_combine",
    )(wsel, x1, shared, mod, ga, gb, *consts)


COMBINE_PARTS = 4


def _moe(x1, h2a, h2b, mod, eid, wsel, exp_wg, exp_wu, exp_wd, sh_wg, sh_wu, sh_wd, normf_g):
    n = h2a.shape[0]
    e = exp_wg.shape[0]
    nb = (n * TOP_K + e * (BLK - 1)) // BLK
    rank, cnt = _expert_ranks(eid, e)
    dest = _destinations(cnt, eid, rank)
    meta = _block_meta(cnt, nb)
    xs_a = _sc_scatter_rows(h2a, dest, nb * BLK)
    xs_b = _sc_scatter_rows(h2b, dest, nb * BLK)
    shared = _shared_expert(h2a, h2b, sh_wg, sh_wu, sh_wd)
    ys_a, ys_b = _experts(meta[0, :nb], meta[1, :nb], meta[2, 0], meta[3, :nb], xs_a, xs_b,
                          exp_wg, exp_wu, exp_wd)
    bsz, seq, d = x1.shape
    out = x1.reshape(n, d)
    part = n // COMBINE_PARTS
    for j in range(COMBINE_PARTS):
        idx = dest[:, j * part:(j + 1) * part].reshape(1, TOP_K * part)
        ga = _sc_gather_rows(ys_a, idx).reshape(TOP_K, part, -1)
        gb = _sc_gather_rows(ys_b, idx).reshape(TOP_K, part, -1)
        out = _combine(wsel, out, shared, mod, ga, gb, normf_g, j * part, seq)
    return out


def kernel(x, c, norm1_g, norm2_g, normf_g, w_ada, b_ada, w_in, w_out, hy_conv_w, hy_conv_b, hy_pos_w1, hy_pos_b1, hy_pos_w2, hy_pos_b2, hy_pos_w3, hy_sin_freq, hy_skip, rw_mu, rw_w0, rw_w_up, rw_a0, rw_a_up, rw_g_up, rw_k_k, rw_k_a, rw_r_k, rw_ln_w, rw_ln_b, router_w, router_bias, exp_w_gate, exp_w_up, exp_w_down, sh_w_gate, sh_w_up, sh_w_down):
    bsz, seq, d = x.shape
    depth = w_ada.shape[0]
    assert depth == 1, "the final norm is fused into the last kernel of a single layer"
    for l in range(depth):
        mod = _modulation(c, w_ada[l], b_ada[l]).reshape(bsz, -1, d)
        uhy, rkvk, lwa, g, bonus = _projection(
            x, mod, norm1_g[l], w_in[l], hy_conv_w[l], hy_conv_b[l], rw_mu[l], rw_w0[l], rw_w_up[l],
            rw_a0[l], rw_a_up[l], rw_g_up[l], rw_k_k[l], rw_k_a[l], rw_r_k[l])
        k2, ss = _hyena_filters(seq, hy_pos_w1[l], hy_pos_b1[l], hy_pos_w2[l], hy_pos_b2[l],
                                hy_pos_w3[l], hy_sin_freq[l])
        khat = _filter_spectrum(k2, ss, seq)
        z, z_col = uhy, 0
        for order in range(HYENA_ORDER):
            z = _long_conv_gate(z, z_col, uhy, (order + 1) * D_HYENA, khat, hy_skip[l], order)
            z_col = 0
        o_f, o_b = _wkv(rkvk, lwa, rw_k_a[l])
        x1, h2a, h2b, eid, wsel = _mix_out(x, mod, z, o_f, o_b, g, bonus, rw_ln_w[l], rw_ln_b[l], w_out[l],
                                           norm2_g[l], router_w[l], router_bias[l])
        x = _moe(x1, h2a, h2b, mod, eid, wsel, exp_w_gate[l], exp_w_up[l], exp_w_down[l],
                 sh_w_gate[l], sh_w_up[l], sh_w_down[l], normf_g)
        x = x.reshape(bsz, seq, d)
    return x
```

```python
import functools
import math

import jax
import jax.numpy as jnp
import numpy as np
from jax import lax
from jax.experimental import pallas as pl
from jax.experimental.pallas import tpu as pltpu
from jax.experimental.pallas import tpu_sc as plsc

F32 = jnp.float32
BF16 = jnp.bfloat16

LANES = 128
MXU_DIM = 256
VMEM_LIMIT = 56 * 1024 * 1024

D_HYENA = 512
D_RWKV = 512
HEAD = 64
N_HEADS = D_RWKV // HEAD
HYENA_ORDER = 2
FILTER_BANDS = 16
DECAY_TARGET = 1e-2
FAST_DECAY_PCT = 0.3
SLOW_DECAY_PCT = 1.5
FILTER_NORM_EPS = 1e-6
DECAY_LORA = 32
ICLR_LORA = 32
GATE_LORA = 96
GN_EPS = 64e-5
NORM_EPS = 1e-6
N_EXPERTS = 256
TOP_K = 8
N_GROUPS = 8
TOPK_GROUPS = 4
ROUTE_SCALE = 2.5
D_EXPERT = 256


def _params(*sem):
    return pltpu.CompilerParams(dimension_semantics=sem, vmem_limit_bytes=VMEM_LIMIT)


def _split2(a):
    hi = a.astype(BF16)
    lo = (a - hi.astype(F32)).astype(BF16)
    return hi, lo


def _dot(a, b):
    return jnp.dot(a, b, preferred_element_type=F32)


def _dot3(a, b):
    ah, al = _split2(a)
    bh, bl = _split2(b)
    return _dot(ah, bh) + (_dot(ah, bl) + _dot(al, bh))


def _dot_exact_rhs(a, b_bf16):
    ah, al = _split2(a)
    return _dot(ah, b_bf16) + _dot(al, b_bf16)


def _silu(x):
    return x * jax.nn.sigmoid(x)


U32 = jnp.int32


def _pack_halves(x):
    w = x.shape[1] // 2
    return pltpu.pack_elementwise([x[:, :w], x[:, w:]], packed_dtype=BF16)


def _unpack_halves(p):
    lo = pltpu.unpack_elementwise(p, index=0, packed_dtype=BF16, unpacked_dtype=F32)
    hi = pltpu.unpack_elementwise(p, index=1, packed_dtype=BF16, unpacked_dtype=F32)
    return lo, hi


def _pack_rows(x):
    packed = _pack_halves(x)
    half = packed.shape[1] // 2
    return packed[:, :half], packed[:, half:]


def _unpack_rows(a, b):
    a_lo, a_hi = _unpack_halves(a)
    b_lo, b_hi = _unpack_halves(b)
    return jnp.concatenate([a_lo.astype(BF16), b_lo.astype(BF16), a_hi.astype(BF16), b_hi.astype(BF16)], axis=1)


def _mod_kernel(c_ref, w_ref, b_ref, o_ref):
    o_ref[...] = _dot3(_silu(c_ref[...]), w_ref[...]) + b_ref[...]


def _modulation(c, w_ada, b_ada):
    bsz, d = c.shape
    n = w_ada.shape[1]
    blk = 1024
    return pl.pallas_call(
        _mod_kernel,
        grid=(n // blk,),
        in_specs=[
            pl.BlockSpec((bsz, d), lambda j: (0, 0)),
            pl.BlockSpec((d, blk), lambda j: (0, j)),
            pl.BlockSpec((1, blk), lambda j: (0, j)),
        ],
        out_specs=pl.BlockSpec((bsz, blk), lambda j: (0, j)),
        out_shape=jax.ShapeDtypeStruct((bsz, n), F32),
        compiler_params=_params("arbitrary"),
        name="adaln_mod",
    )(c, w_ada, b_ada.reshape(1, n))


def _filter_kernel(band_ref, w1_ref, b1_ref, w2_ref, b2_ref, w3_ref, freq_ref, delta_ref,
                   k_ref, ss_ref, *, seq, rows):
    half = pl.program_id(0)
    i = pl.program_id(1)
    r = lax.broadcasted_iota(jnp.int32, (rows, LANES), 0) + i * rows
    pos = jnp.where(half == 0, r, seq - r).astype(F32)
    tt = pos / float(max(seq - 1, 1))
    lane = lax.broadcasted_iota(jnp.int32, (rows, LANES), 1)
    feats = jnp.where(lane == 0, tt, jnp.sin(pos * band_ref[0:1, :] + band_ref[1:2, :]))
    freq = freq_ref[...]
    hdn = jnp.sin(freq * (_dot3(feats, w1_ref[...]) + b1_ref[...]))
    for j in range(w2_ref.shape[0]):
        hdn = jnp.sin(freq * (_dot3(hdn, w2_ref[j]) + b2_ref[j]))
    filt = _dot3(hdn, w3_ref[...])
    filt = filt * jnp.exp(-tt[:, :1] * delta_ref[...])
    valid = jnp.logical_or(half == 0, r[:, :1] > 0)
    filt = jnp.where(valid, filt, 0.0)
    k_ref[...] = filt

    @pl.when(jnp.logical_and(half == 0, i == 0))
    def _():
        ss_ref[...] = jnp.zeros_like(ss_ref)

    ss_ref[...] += jnp.broadcast_to(jnp.sum(filt * filt, axis=0, keepdims=True), ss_ref.shape)


def _hyena_filters(seq, pw1, pb1, pw2, pb2, pw3, freq):
    width = pw1.shape[1]
    ncol = HYENA_ORDER * D_HYENA
    rows = min(seq, 512)
    bands = np.zeros((2, LANES), np.float64)
    lin = np.linspace(1e-4, FILTER_BANDS - 1, FILTER_BANDS) * (2.0 * math.pi / seq)
    bands[0, 1:1 + FILTER_BANDS] = lin
    bands[1, 1:1 + FILTER_BANDS] = 0.5 * math.pi
    bands[0, 1 + FILTER_BANDS:1 + 2 * FILTER_BANDS] = -lin
    bands = jnp.asarray(bands, F32)
    deltas = np.abs(np.linspace(math.log(DECAY_TARGET) / SLOW_DECAY_PCT,
                                math.log(DECAY_TARGET) / FAST_DECAY_PCT, D_HYENA))
    deltas = jnp.asarray(np.tile(deltas, HYENA_ORDER)[None], F32)
    w1 = jnp.zeros((LANES, width), F32).at[:pw1.shape[0]].set(pw1)
    w3 = pw3.reshape(width, HYENA_ORDER, 2, D_HYENA).transpose(2, 0, 1, 3).reshape(2, width, ncol)
    nt = seq // rows
    full = lambda *shape: pl.BlockSpec(shape, lambda h, i: (0,) * len(shape))
    return pl.pallas_call(
        functools.partial(_filter_kernel, seq=seq, rows=rows),
        grid=(2, nt),
        in_specs=[
            full(2, LANES), full(LANES, width), full(1, width),
            full(pw2.shape[0], width, width), full(pw2.shape[0], 1, width),
            pl.BlockSpec((None, width, ncol), lambda h, i: (h, 0, 0)),
            full(1, width), full(1, ncol),
        ],
        out_specs=[
            pl.BlockSpec((rows, ncol), lambda h, i: (h * nt + i, 0)),
            pl.BlockSpec((8, ncol), lambda h, i: (0, 0)),
        ],
        out_shape=[jax.ShapeDtypeStruct((2 * seq, ncol), F32),
                   jax.ShapeDtypeStruct((8, ncol), F32)],
        compiler_params=_params("arbitrary", "arbitrary"),
        name="hyena_filters",
    )(bands, w1, pb1.reshape(1, width), pw2, pb2.reshape(pw2.shape[0], 1, width), w3,
      freq.reshape(1, width), deltas)


N1 = LANES
UNROLL = 8


def _dft_tables(seq):
    tables = _dft_tables_np(seq)
    return tuple(jnp.asarray(t, BF16) for t in tables[:5]) + tables[5:]


def _dft_tables_np(seq):
    m = 2 * seq
    n2 = m // N1
    n2h = n2 // 2
    n1 = np.arange(N1)[:, None, None]
    f2 = np.arange(n2)[None, :, None]
    k2 = np.arange(n2)[None, None, :]
    th = 2.0 * np.pi * (n1 * f2 / m + (k2 * f2 % n2) / n2)
    fwd_a = np.concatenate([np.cos(th), -np.sin(th)], axis=1)
    tht = np.transpose(th, (0, 2, 1))
    inv_a = np.concatenate([np.cos(tht), -np.sin(tht)], axis=2)[:, :n2h] / m
    a = np.arange(N1)
    ph = 2.0 * np.pi * np.outer(a, a) / N1
    c, s = np.cos(ph), np.sin(ph)
    fwd_b = np.block([[c, s], [-s, c]])
    inv_b = np.block([[c, -s], [s, c]])
    return fwd_a, fwd_a[:, :, :n2h], inv_a, fwd_b, inv_b, n2, n2h


def _stage_a_fwd(x_ref, wa_ref, y_ref, n2, scale=None):
    def body(i, carry):
        trips = [i * UNROLL + j for j in range(UNROLL)]
        xs = [x_ref[pl.ds(n1, wa_ref.shape[2], stride=N1), :] for n1 in trips]
        if scale is not None:
            xs = [x * scale for x in xs]
        prods = [_dot(wa_ref[n1], x.astype(BF16)) for n1, x in zip(trips, xs)]
        for n1, a in zip(trips, prods):
            y_ref[pl.ds(n1, n2, stride=2 * N1), :] = a[:n2]
            y_ref[pl.ds(N1 + n1, n2, stride=2 * N1), :] = a[n2:]
        return carry
    lax.fori_loop(0, N1 // UNROLL, body, 0)


def _filter_fft_kernel(k_ref, ss_ref, wa_ref, fb_ref, o_ref, y_ref, *, n2):
    scale = lax.rsqrt(ss_ref[0:1, :] + FILTER_NORM_EPS)
    _stage_a_fwd(k_ref, wa_ref, y_ref, n2, scale=scale)

    unr = min(UNROLL, n2)

    def body(i, carry):
        trips = [i * unr + j for j in range(unr)]
        ys = [y_ref[pl.ds(pl.multiple_of(f2 * 2 * N1, 2 * N1), 2 * N1), :].astype(BF16) for f2 in trips]
        for j in range(0, unr, 2):
            z = _dot(fb_ref[...], jnp.concatenate(ys[j:j + 2], axis=1))
            o_ref[trips[j]] = z[:, :LANES]
            o_ref[trips[j + 1]] = z[:, LANES:]
        return carry
    lax.fori_loop(0, n2 // unr, body, 0)


def _filter_spectrum(k2, ss, seq):
    fwd_a, _, _, fwd_b, _, n2, _ = _dft_tables(seq)
    ncol = k2.shape[1]
    nblk = ncol // LANES
    return pl.pallas_call(
        functools.partial(_filter_fft_kernel, n2=n2),
        grid=(nblk,),
        in_specs=[
            pl.BlockSpec((2 * seq, LANES), lambda c: (0, c)),
            pl.BlockSpec((8, LANES), lambda c: (0, c)),
            pl.BlockSpec(fwd_a.shape, lambda c: (0, 0, 0)),
            pl.BlockSpec(fwd_b.shape, lambda c: (0, 0)),
        ],
        out_specs=pl.BlockSpec((None, n2, 2 * N1, LANES), lambda c: (c, 0, 0, 0)),
        out_shape=jax.ShapeDtypeStruct((nblk, n2, 2 * N1, LANES), F32),
        scratch_shapes=[pltpu.VMEM((n2 * 2 * N1, LANES), F32)],
        compiler_params=_params("arbitrary"),
        name="hyena_filter_fft",
    )(k2, ss, fwd_a, fwd_b)


TILE = 8
N1_GROUPS = N1 // TILE


def _tile_tables(seq):
    _, fwd_a, inv_a, _, _, n2, n2h = _dft_tables_np(seq)
    eye = np.eye(TILE)
    fa = fwd_a.reshape(N1_GROUPS, TILE, 2 * n2, n2h)
    wa = np.einsum("qjrn,jk->qrjnk", fa, eye).reshape(N1_GROUPS, 2 * n2 * TILE, n2h * TILE)
    ia = inv_a.reshape(N1_GROUPS, TILE, n2h, 2 * n2)
    vc = np.einsum("qjnr,jk->qnjrk", ia, eye).reshape(N1_GROUPS, n2h * TILE, 2 * n2 * TILE)
    return jnp.asarray(wa, BF16), jnp.asarray(vc, BF16)


def _conv_kernel(u_ref, g_ref, skip_ref, fb_ref, ib_ref, kh_hbm, wa_hbm, vc_hbm, o_ref,
                 y_ref, kh_ref, wa_ref, vc_ref, sem, *, n2, n2h, kh_first):
    c_id, b_id = pl.program_id(0), pl.program_id(1)

    @pl.when(jnp.logical_and(c_id == 0, b_id == 0))
    def _():
        for src, dst in ((wa_hbm, wa_ref), (vc_hbm, vc_ref)):
            cp = pltpu.make_async_copy(src, dst, sem)
            cp.start()
            cp.wait()

    @pl.when(b_id == 0)
    def _():
        cp = pltpu.make_async_copy(kh_hbm.at[kh_first + c_id], kh_ref, sem)
        cp.start()
        cp.wait()

    def y_tile(rf, base):
        ri, f2 = divmod(rf, n2)
        return pl.ds(f2 * 2 * N1 + ri * N1 + base, TILE)

    def stage_a(q, carry):
        base = pl.multiple_of(q * TILE, TILE)
        x = jnp.concatenate([u_ref[pl.ds(N1 * m + base, TILE), :] for m in range(n2h)], axis=0)
        r = _dot(wa_ref[q], x.astype(BF16))
        for rf in range(2 * n2):
            y_ref[y_tile(rf, base), :] = r[rf * TILE:(rf + 1) * TILE]
        return carry
    lax.fori_loop(0, N1_GROUPS, stage_a, 0, unroll=2)

    unr = min(UNROLL, n2)

    def mid(i, carry):
        trips = [i * unr + j for j in range(unr)]
        offs = [pl.multiple_of(f2 * 2 * N1, 2 * N1) for f2 in trips]
        wide = lambda blocks: [jnp.concatenate(blocks[j:j + 2], axis=1) for j in range(0, len(blocks), 2)]
        ys = wide([y_ref[pl.ds(off, 2 * N1), :].astype(BF16) for off in offs])
        khs = wide([kh_ref[f2] for f2 in trips])
        zs = [_dot(fb_ref[...], y) for y in ys]
        ps = []
        for z, kh in zip(zs, khs):
            zr, zi = z[:N1], z[N1:]
            kr, ki = kh[:N1], kh[N1:]
            ps.append(jnp.concatenate([zr * kr - zi * ki, zr * ki + zi * kr], axis=0).astype(BF16))
        gs = [_dot(ib_ref[...], p) for p in ps]
        for j, g in enumerate(gs):
            y_ref[pl.ds(offs[2 * j], 2 * N1), :] = g[:, :LANES]
            y_ref[pl.ds(offs[2 * j + 1], 2 * N1), :] = g[:, LANES:]
        return carry
    lax.fori_loop(0, n2 // unr, mid, 0)

    skip = skip_ref[...]

    def stage_c(q, carry):
        base = pl.multiple_of(q * TILE, TILE)
        g = jnp.concatenate([y_ref[y_tile(rf, base), :] for rf in range(2 * n2)], axis=0)
        conv = _dot(vc_ref[q], g.astype(BF16))
        for m in range(n2h):
            rows = pl.ds(N1 * m + base, TILE)
            o_ref[rows, :] = g_ref[rows, :] * (conv[m * TILE:(m + 1) * TILE] + u_ref[rows, :] * skip)
        return carry
    lax.fori_loop(0, N1_GROUPS, stage_c, 0, unroll=2)


def _long_conv_gate(u, u_col, gate, gate_col, khat, skip, order):
    bsz, seq, _ = u.shape
    ch = D_HYENA
    _, _, _, fwd_b, inv_b, n2, n2h = _dft_tables(seq)
    wa, vc = _tile_tables(seq)
    nblk = ch // LANES
    const = lambda a: pl.BlockSpec(a.shape, lambda c, b: (0,) * a.ndim)
    at = lambda col: pl.BlockSpec((None, seq, LANES), lambda c, b: (b, 0, col // LANES + c))
    hbm = pl.BlockSpec(memory_space=pl.ANY)
    return pl.pallas_call(
        functools.partial(_conv_kernel, n2=n2, n2h=n2h, kh_first=order * nblk),
        grid=(nblk, bsz),
        in_specs=[
            at(u_col), at(gate_col),
            pl.BlockSpec((1, LANES), lambda c, b: (0, c)),
            const(fwd_b), const(inv_b), hbm, hbm, hbm,
        ],
        out_specs=at(0),
        out_shape=jax.ShapeDtypeStruct((bsz, seq, ch), F32),
        scratch_shapes=[pltpu.VMEM((n2 * 2 * N1, LANES), F32), pltpu.VMEM(khat.shape[1:], F32),
                        pltpu.VMEM(wa.shape, BF16), pltpu.VMEM(vc.shape, BF16), pltpu.SemaphoreType.DMA(())],
        compiler_params=_params("arbitrary", "arbitrary"),
        name=f"hyena_conv{order}",
    )(u, gate, skip[order].reshape(1, ch), fwd_b, inv_b, khat, wa, vc)


HALO = 8


def _shift_rows(p, k):
    return pltpu.roll(p, k % p.shape[0], axis=0)


def _proj_kernel(xp_ref, x_ref, xn_ref, mod_ref, g1_ref, why_ref, wrkv_ref, wlora_ref,
                 cw_ref, cb_ref, murkv_ref, mulora_ref, w0_ref, a0_ref, wwah_ref, wwal_ref, gup_ref,
                 kk_ref, ka_ref, rk_ref, ones_ref,
                 uhy_ref, rkvk_ref, lwa_ref, g_ref, bonus_ref,
                 *, tt, nt):
    i = pl.program_id(1)
    xe = jnp.concatenate([xp_ref[...], x_ref[...], xn_ref[...]], axis=0)
    ms = jnp.mean(xe * xe, axis=-1, keepdims=True)
    h = xe * lax.rsqrt(ms + NORM_EPS) * g1_ref[...]
    h = h * (1.0 + mod_ref[1:2, :]) + mod_ref[0:1, :]
    row = lax.broadcasted_iota(jnp.int32, (tt + 2 * HALO, 1), 0)
    inside = jnp.logical_and(jnp.logical_or(row >= HALO, i > 0),
                             jnp.logical_or(row < tt + HALO, i < nt - 1))
    hb = jnp.where(inside, h, 0.0).astype(BF16)
    mid = slice(HALO, tt + HALO)

    p = _dot(hb, why_ref[...])
    u = (_shift_rows(p, 1) * cw_ref[0:1, :] + p * cw_ref[1:2, :]
         + _shift_rows(p, -1) * cw_ref[2:3, :] + cb_ref[...])
    uhy_ref[...] = u[mid]

    p = _dot(hb, wrkv_ref[...])
    p = p + murkv_ref[...] * (0.5 * (_shift_rows(p, 1) + _shift_rows(p, -1)) - p)
    p = p[mid]
    c = D_RWKV
    r, k, v = p[:, :c], p[:, c:2 * c], p[:, 2 * c:]
    rkvk_ref[:, :3 * c] = p

    q = _dot(hb, wlora_ref[...])
    q = q + mulora_ref[...] * (0.5 * (_shift_rows(q, 1) + _shift_rows(q, -1)) - q)
    q = q[mid]
    wa = q[:, :LANES]
    lane = lax.broadcasted_iota(jnp.int32, wa.shape, 1)
    wa = jnp.where(lane < 2 * DECAY_LORA, jnp.tanh(wa), wa)
    wah, wal = _split2(wa)
    up = _dot(wah, wwah_ref[...]) + (_dot(wah, wwal_ref[...]) + _dot(wal, wwah_ref[...]))
    lw = -math.exp(-0.5) * jax.nn.sigmoid(w0_ref[...] + up[:, :2 * c])
    a = jax.nn.sigmoid(a0_ref[...] + up[:, 2 * c:])
    for dd in range(2):
        lwa_ref[:, 2 * dd * c:(2 * dd + 1) * c] = lw[:, dd * c:(dd + 1) * c]
        lwa_ref[:, (2 * dd + 1) * c:(2 * dd + 2) * c] = a[:, dd * c:(dd + 1) * c]
    g_ref[...] = _dot3(jax.nn.sigmoid(q[:, LANES:]), gup_ref[...])

    ones = ones_ref[...]
    kk = k * kk_ref[...]
    nrm = jnp.sqrt(_dot_exact_rhs(kk * kk, ones))
    rkvk_ref[:, 3 * c:] = kk / jnp.maximum(nrm, 1e-12)
    ka = ka_ref[...]
    ksum = k * (2.0 + (a[:, :c] + a[:, c:] - 2.0) * ka)
    bonus_ref[...] = _dot_exact_rhs(r * ksum * rk_ref[...], ones) * v


def _head_ones():
    hid = np.arange(D_RWKV) // HEAD
    return jnp.asarray(hid[:, None] == hid[None, :], BF16)


def _projection(x, mod, norm1_g, w_in, hy_conv_w, hy_conv_b, rw_mu, rw_w0, rw_w_up, rw_a0,
                rw_a_up, rw_g_up, rw_k_k, rw_k_a, rw_r_k, tt=512):
    bsz, seq, d = x.shape
    tt = min(tt, seq)
    nt = seq // tt
    c = D_RWKV
    hy = (HYENA_ORDER + 1) * D_HYENA
    nlora = 2 * LANES
    w_hy = w_in[:, :hy].astype(BF16)
    w_rkv = w_in[:, hy:hy + 3 * c].astype(BF16)
    w_lora = jnp.zeros((d, nlora), F32).at[:, :w_in.shape[1] - hy - 3 * c].set(w_in[:, hy + 3 * c:]).astype(BF16)
    mu_rkv = rw_mu[:3 * c].reshape(1, 3 * c)
    mu_lora = jnp.zeros((1, nlora), F32).at[0, :rw_mu.shape[0] - 3 * c].set(rw_mu[3 * c:])
    wwa = jnp.zeros((LANES, 4 * c), F32)
    for dd in range(2):
        wwa = wwa.at[dd * DECAY_LORA:(dd + 1) * DECAY_LORA, dd * c:(dd + 1) * c].set(rw_w_up[dd])
        wwa = wwa.at[2 * DECAY_LORA + dd * ICLR_LORA:2 * DECAY_LORA + (dd + 1) * ICLR_LORA,
                     2 * c + dd * c:2 * c + (dd + 1) * c].set(rw_a_up[dd])
    gup = jnp.zeros((LANES, c), F32).at[:GATE_LORA].set(rw_g_up)
    row = lambda a: a.reshape(1, -1)

    nb8 = seq // HALO
    tb = tt // HALO
    const = lambda a: pl.BlockSpec(a.shape, lambda b, i: (0,) * a.ndim, pipeline_mode=pl.Buffered(1))
    tile = lambda w: pl.BlockSpec((None, tt, w), lambda b, i: (b, i, 0))
    ins = [
        (x, pl.BlockSpec((None, HALO, d), lambda b, i: (b, jnp.maximum(i * tb - 1, 0), 0))),
        (x, pl.BlockSpec((None, tt, d), lambda b, i: (b, i, 0))),
        (x, pl.BlockSpec((None, HALO, d), lambda b, i: (b, jnp.minimum((i + 1) * tb, nb8 - 1), 0))),
        (mod, pl.BlockSpec((None,) + mod.shape[1:], lambda b, i: (b, 0, 0))),
    ]
    consts = [row(norm1_g), w_hy, w_rkv, w_lora, hy_conv_w, row(hy_conv_b), mu_rkv, mu_lora,
              row(rw_w0), row(rw_a0), *_split2(wwa), gup, row(rw_k_k), row(rw_k_a), row(rw_r_k), _head_ones()]
    ins += [(a, const(a)) for a in consts]
    widths = [hy, 4 * c, 4 * c, c, c]
    return pl.pallas_call(
        functools.partial(_proj_kernel, tt=tt, nt=nt),
        grid=(bsz, nt),
        in_specs=[s for _, s in ins],
        out_specs=[tile(w) for w in widths],
        out_shape=[jax.ShapeDtypeStruct((bsz, seq, w), F32) for w in widths],
        compiler_params=_params("arbitrary", "arbitrary"),
        name="input_projection",
    )(*[a for a, _ in ins])


CHUNK = HEAD
GROUP = MXU_DIM // HEAD


def _nt(a, b):
    return lax.dot_general(a, b, (((1,), (1,)), ((), ())), preferred_element_type=F32)


def _tn(a, b):
    return lax.dot_general(a, b, (((0,), (0,)), ((), ())), preferred_element_type=F32)


def _wkv_direction(r, k, v, kk, lw, a, ka, s_ref, reverse):
    c = CHUNK
    ti = lax.broadcasted_iota(jnp.int32, (c, c), 0)
    si = lax.broadcasted_iota(jnp.int32, (c, c), 1)
    tri = (si >= ti) if reverse else (si <= ti)
    cum = _dot_exact_rhs_lhs(jnp.where(tri, 1.0, 0.0).astype(BF16), lw)
    tot = jnp.sum(lw, axis=0, keepdims=True)
    w_incl = jnp.exp(cum)
    w_prev = jnp.exp(cum - lw)
    w_inv = jnp.exp(-cum)
    w_end = jnp.exp(tot - cum)
    w_tot = jnp.exp(tot)
    kd = k * (1.0 + (a - 1.0) * ka)
    b = kk * a
    a_w = -kk * w_prev
    r_w = r * w_incl
    b_w = b * w_inv
    k_w = kd * w_inv
    b_e = b * w_end
    k_e = kd * w_end

    m = MXU_DIM
    ri = lax.broadcasted_iota(jnp.int32, (m, m), 0)
    ci = lax.broadcasted_iota(jnp.int32, (m, m), 1)
    head_mask = (ri // HEAD) == (ci // HEAD)
    tl = lax.broadcasted_iota(jnp.int32, (c, m), 0)
    sl = lax.broadcasted_iota(jnp.int32, (c, m), 1) % c
    strict = (sl > tl) if reverse else (sl < tl)
    incl = (sl >= tl) if reverse else (sl <= tl)
    eye = jnp.where(sl == tl, 1.0, 0.0)
    both = lambda top, bot: jnp.concatenate([top, bot], axis=0)

    def stack(xg):
        xb = xg.astype(BF16)
        return jnp.where(head_mask, jnp.concatenate([xb] * GROUP, axis=0), jnp.zeros((), BF16))

    streams = []
    for g in range(D_RWKV // m):
        sl_g = slice(g * m, (g + 1) * m)
        streams.append(dict(
            ar=both(a_w[:, sl_g], r_w[:, sl_g]).astype(BF16),
            b_st=stack(b_w[:, sl_g]), k_st=stack(k_w[:, sl_g]), v_st=stack(v[:, sl_g]),
            v=v[:, sl_g], bk=both(b_e[:, sl_g], k_e[:, sl_g]).astype(BF16),
            w_tot=w_tot[:, sl_g], s_ref=s_ref.at[g],
            strict=strict, incl=incl, eye=eye, head_mask=head_mask, stack=stack))
    return streams


def _wkv_streams_step(streams):
    c = CHUNK
    both = lambda top, bot: jnp.concatenate([top, bot], axis=0)
    for st in streams:
        st["s"] = st["s_ref"][...]
        st["xb"] = _nt(st["ar"], st["b_st"])
        st["xk"] = _nt(st["ar"], st["k_st"])
        st["xs"] = _nt(st["ar"], st["s"].astype(BF16))
    for st in streams:
        m_k = both(jnp.where(st["strict"], st["xk"][:c], 0.0), jnp.where(st["incl"], st["xk"][c:], 0.0))
        st["kv"] = _dot(m_k.astype(BF16), st["v_st"])
        st["rhs"] = st["xs"][:c] + st["kv"][:c]
        st["pw"] = jnp.where(st["strict"], st["xb"][:c], 0.0)
        st["t"] = st["eye"] + st["pw"]
        st["p_st"] = st["stack"](st["pw"])
    levels = int(math.log2(c)) - 1
    for st in streams:
        st["pw"] = _dot(st["pw"].astype(BF16), st["p_st"])
        st["p_st"] = st["stack"](st["pw"])
    for lvl in range(1, levels + 1):
        for st in streams:
            if lvl < levels:
                prod = _dot(both(st["pw"], st["t"]).astype(BF16), st["p_st"])
                st["pw"] = prod[:c]
                st["t"] = st["t"] + prod[c:]
                st["p_st"] = st["stack"](st["pw"])
            else:
                st["t"] = st["t"] + _dot(st["t"].astype(BF16), st["p_st"])
    for st in streams:
        st["u"] = _dot(st["t"].astype(BF16), st["stack"](st["rhs"]))
    outs = []
    for st in streams:
        m_rb = jnp.where(st["incl"], st["xb"][c:], 0.0)
        outs.append(st["xs"][c:] + _dot(m_rb.astype(BF16), st["stack"](st["u"])) + st["kv"][c:])
        uv = both(st["u"], st["v"]).astype(BF16)
        st["s_ref"][...] = st["s"] * st["w_tot"] + jnp.where(st["head_mask"], _tn(uv, st["bk"]), 0.0)
    return outs


def _dot_exact_rhs_lhs(tri_bf16, x):
    xh, xl = _split2(x)
    return _dot(tri_bf16, xh) + _dot(tri_bf16, xl)


def _wkv_kernel(rkvk_f, lwa_f, rkvk_b, lwa_b, ka_ref, of_ref, ob_ref, s_ref, *, nch):
    @pl.when(pl.program_id(1) == 0)
    def _():
        s_ref[...] = jnp.zeros_like(s_ref)

    ka = ka_ref[...]
    c = D_RWKV

    def operands(rkvk_ref, lwa_ref, rows):
        x = rkvk_ref[rows, :]
        la = lwa_ref[rows, :]
        return x[:, :c], x[:, c:2 * c], x[:, 2 * c:3 * c], x[:, 3 * c:], la[:, :c], la[:, c:]

    for ci in range(nch):
        rows_f = slice(ci * CHUNK, (ci + 1) * CHUNK)
        rows_b = slice((nch - 1 - ci) * CHUNK, (nch - ci) * CHUNK)
        fwd = _wkv_direction(*operands(rkvk_f, lwa_f, rows_f), ka, s_ref.at[0], False)
        bwd = _wkv_direction(*operands(rkvk_b, lwa_b, rows_b), ka, s_ref.at[1], True)
        outs = _wkv_streams_step(fwd + bwd)
        of_ref[rows_f, :] = jnp.concatenate(outs[:len(fwd)], axis=1)
        ob_ref[rows_b, :] = jnp.concatenate(outs[len(fwd):], axis=1)


WKV_CHUNKS_PER_STEP = 4


def _wkv(rkvk, lwa, rw_k_a):
    bsz, seq, _ = rkvk.shape
    c = D_RWKV
    nch = WKV_CHUNKS_PER_STEP if seq % (WKV_CHUNKS_PER_STEP * CHUNK) == 0 else 1
    rows = nch * CHUNK
    nb = seq // rows
    fwd = lambda w, lane_blk: pl.BlockSpec((None, rows, w), lambda b, j: (b, j, lane_blk))
    bwd = lambda w, lane_blk: pl.BlockSpec((None, rows, w), lambda b, j: (b, nb - 1 - j, lane_blk))
    return pl.pallas_call(
        functools.partial(_wkv_kernel, nch=nch),
        grid=(bsz, nb),
        in_specs=[fwd(4 * c, 0), fwd(2 * c, 0), bwd(4 * c, 0), bwd(2 * c, 1),
                  pl.BlockSpec((1, c), lambda b, j: (0, 0))],
        out_specs=[fwd(c, 0), bwd(c, 0)],
        out_shape=[jax.ShapeDtypeStruct((bsz, seq, c), F32)] * 2,
        scratch_shapes=[pltpu.VMEM((2, c // MXU_DIM, MXU_DIM, MXU_DIM), F32)],
        compiler_params=_params("arbitrary", "arbitrary"),
        name="wkv7_chunked",
    )(rkvk, lwa, rkvk, lwa, rw_k_a.reshape(1, c))


NEG_INF = float("-inf")


def _first_max(vals, idx, size):
    m = jnp.max(vals, axis=0, keepdims=True)
    i = jnp.min(jnp.where(vals == m, idx, size), axis=0, keepdims=True)
    return m, i


def _route(scores, biased):
    e, tt = scores.shape
    per = e // N_GROUPS
    rowl = lax.broadcasted_iota(jnp.int32, (per, tt), 0)
    gs = []
    for g in range(N_GROUPS):
        blk = biased[g * per:(g + 1) * per]
        m1, i1 = _first_max(blk, rowl, per)
        m2 = jnp.max(jnp.where(rowl == i1, NEG_INF, blk), axis=0, keepdims=True)
        gs.append(m1 + m2)
    cur = jnp.concatenate(gs, axis=0)
    growl = lax.broadcasted_iota(jnp.int32, (N_GROUPS, tt), 0)
    gsel = jnp.zeros((N_GROUPS, tt), F32)
    for _ in range(TOPK_GROUPS):
        _, ig = _first_max(cur, growl, N_GROUPS)
        hit = growl == ig
        gsel = jnp.where(hit, 1.0, gsel)
        cur = jnp.where(hit, NEG_INF, cur)
    emask = jnp.concatenate([jnp.broadcast_to(gsel[g:g + 1], (per, tt)) for g in range(N_GROUPS)], axis=0)
    masked = jnp.where(emask > 0.5, biased, NEG_INF)
    row = lax.broadcasted_iota(jnp.int32, (e, tt), 0)
    ids, ws = [], []
    for _ in range(TOP_K):
        _, ie = _first_max(masked, row, e)
        hit = row == ie
        ids.append(ie)
        ws.append(jnp.sum(jnp.where(hit, scores, 0.0), axis=0, keepdims=True))
        masked = jnp.where(hit, NEG_INF, masked)
    w = jnp.concatenate(ws, axis=0)
    w = w / jnp.sum(w, axis=0, keepdims=True) * ROUTE_SCALE
    return jnp.concatenate(ids, axis=0), w


def _mixout_kernel(x_ref, mod_ref, yhy_ref, of_ref, ob_ref, g_ref, bonus_ref, lnw_ref, lnb_ref,
                   ones_ref, wout_ref, g2n_ref, rwth_ref, rwtl_ref, bias_ref,
                   x1_ref, h2a_ref, h2b_ref, eid_ref, wsel_ref):
    ones = ones_ref[...]
    s = of_ref[...] + ob_ref[...]
    mean = _dot_exact_rhs(s, ones) * (1.0 / HEAD)
    dlt = s - mean
    var = _dot_exact_rhs(dlt * dlt, ones) * (1.0 / HEAD)
    sn = dlt * lax.rsqrt(var + GN_EPS) * lnw_ref[...] + lnb_ref[...]
    yrw = (sn + bonus_ref[...]) * g_ref[...]
    ch = yhy_ref.shape[-1]
    mix = _dot(yhy_ref[...].astype(BF16), wout_ref[:ch, :]) + _dot(yrw.astype(BF16), wout_ref[ch:, :])
    x1 = x_ref[...] + mod_ref[2:3, :] * mix
    x1_ref[...] = x1
    ms = jnp.mean(x1 * x1, axis=-1, keepdims=True)
    h2 = x1 * lax.rsqrt(ms + NORM_EPS) * g2n_ref[...]
    h2 = h2 * (1.0 + mod_ref[4:5, :]) + mod_ref[3:4, :]
    h2a_ref[...], h2b_ref[...] = _pack_rows(h2)
    rh, rl = rwth_ref[...], rwtl_ref[...]
    hh, hl = _split2(h2)
    logits = _nt(rh, hh) + (_nt(rh, hl) + _nt(rl, hh))
    scores = jax.nn.sigmoid(logits)
    ids, w = _route(scores, scores + bias_ref[...])
    eid_ref[...] = ids
    wsel_ref[...] = w


def _mix_out(x, mod, yhy, o_f, o_b, g, bonus, ln_w, ln_b, w_out, norm2_g, router_w, router_bias, tt=1024):
    bsz, seq, d = x.shape
    tt = min(tt, seq)
    nt = seq // tt
    n = bsz * seq
    c = D_RWKV
    e = router_w.shape[1]
    row = lambda a: a.reshape(1, -1)
    consts = [row(ln_w), row(ln_b), _head_ones(), w_out.astype(BF16), row(norm2_g), *_split2(router_w.T),
              jnp.broadcast_to(router_bias.reshape(e, 1), (e, tt))]
    const = lambda a: pl.BlockSpec(a.shape, lambda b, i: (0,) * a.ndim, pipeline_mode=pl.Buffered(1))
    tile = lambda w: pl.BlockSpec((None, tt, w), lambda b, i: (b, i, 0))
    flat = lambda rows, dt: jax.ShapeDtypeStruct((rows, n), dt)
    return pl.pallas_call(
        _mixout_kernel,
        grid=(bsz, nt),
        in_specs=[tile(d), pl.BlockSpec((None,) + mod.shape[1:], lambda b, i: (b, 0, 0))]
        + [tile(c)] * 5 + [const(a) for a in consts],
        out_specs=[tile(d), pl.BlockSpec((tt, d // 4), lambda b, i: (b * nt + i, 0)),
                   pl.BlockSpec((tt, d // 4), lambda b, i: (b * nt + i, 0)),
                   pl.BlockSpec((TOP_K, tt), lambda b, i: (0, b * nt + i)),
                   pl.BlockSpec((TOP_K, tt), lambda b, i: (0, b * nt + i))],
        out_shape=[jax.ShapeDtypeStruct((bsz, seq, d), F32), jax.ShapeDtypeStruct((n, d // 4), U32),
                   jax.ShapeDtypeStruct((n, d // 4), U32),
                   flat(TOP_K, jnp.int32), flat(TOP_K, F32)],
        compiler_params=_params("arbitrary", "arbitrary"),
        name="mix_out_router",
    )(x, mod, yhy, o_f, o_b, g, bonus, *consts)


BLK = 512
BLK_SHIFT = 9


def _multi_hot(eid, e):
    row = lax.broadcasted_iota(jnp.int32, (e, eid.shape[1]), 0)
    mh = jnp.zeros((e, eid.shape[1]), F32)
    for kk in range(TOP_K):
        mh = mh + jnp.where(row == eid[kk:kk + 1, :], 1.0, 0.0)
    return row, mh


def _lookup(row, eid, table):
    return jnp.concatenate(
        [jnp.sum(jnp.where(row == eid[kk:kk + 1, :], table, 0.0), axis=0, keepdims=True)
         for kk in range(TOP_K)], axis=0)


def _rank_kernel(eid_ref, rank_ref, cnt_ref, *, e):
    @pl.when(pl.program_id(0) == 0)
    def _():
        cnt_ref[...] = jnp.zeros_like(cnt_ref)

    eid = eid_ref[...]
    tt = eid.shape[1]
    row, mh = _multi_hot(eid, e)
    mhb = mh.astype(BF16)
    si = lax.broadcasted_iota(jnp.int32, (tt, tt), 0)
    ti = lax.broadcasted_iota(jnp.int32, (tt, tt), 1)
    earlier = _dot(mhb, jnp.where(si < ti, 1.0, 0.0).astype(BF16))
    cnt = cnt_ref[...]
    full = earlier + jnp.concatenate([cnt] * (tt // LANES), axis=1)
    rank_ref[...] = _lookup(row, eid, full).astype(jnp.int32)
    cnt_ref[...] = cnt + _dot(mhb, jnp.ones((tt, LANES), BF16))


def _expert_ranks(eid, e, tt=512):
    n = eid.shape[1]
    tt = min(tt, n)
    return pl.pallas_call(
        functools.partial(_rank_kernel, e=e),
        grid=(n // tt,),
        in_specs=[pl.BlockSpec((TOP_K, tt), lambda i: (0, i))],
        out_specs=[pl.BlockSpec((TOP_K, tt), lambda i: (0, i)),
                   pl.BlockSpec((e, LANES), lambda i: (0, 0))],
        out_shape=[jax.ShapeDtypeStruct((TOP_K, n), jnp.int32), jax.ShapeDtypeStruct((e, LANES), F32)],
        compiler_params=_params("arbitrary"),
        name="expert_ranks",
    )(eid)


def _block_offsets(cnt):
    e = cnt.shape[0]
    nblk = ((cnt.astype(jnp.int32) + (BLK - 1)) >> BLK_SHIFT).astype(F32)
    ri = lax.broadcasted_iota(jnp.int32, (e, e), 0)
    ci = lax.broadcasted_iota(jnp.int32, (e, e), 1)
    tril = jnp.where(ci <= ri, 1.0, 0.0).astype(BF16)
    nh, nl = _split2(nblk)
    return nblk, _dot(tril, nh) + _dot(tril, nl)


def _dest_kernel(cnt_ref, eid_ref, rank_ref, dest_ref):
    nblk, end = _block_offsets(cnt_ref[...])
    off = (end - nblk) * float(BLK)
    eid = eid_ref[...]
    tt = eid.shape[1]
    row = lax.broadcasted_iota(jnp.int32, (off.shape[0], tt), 0)
    table = jnp.concatenate([off] * (tt // LANES), axis=1)
    dest_ref[...] = _lookup(row, eid, table).astype(jnp.int32) + rank_ref[...]


def _destinations(cnt, eid, rank, tt=512):
    n = eid.shape[1]
    tt = min(tt, n)
    blk = pl.BlockSpec((TOP_K, tt), lambda i: (0, i))
    return pl.pallas_call(
        _dest_kernel,
        grid=(n // tt,),
        in_specs=[pl.BlockSpec(cnt.shape, lambda i: (0, 0)), blk, blk],
        out_specs=blk,
        out_shape=jax.ShapeDtypeStruct((TOP_K, n), jnp.int32),
        compiler_params=_params("arbitrary"),
        name="expert_destinations",
    )(cnt, eid, rank)


def _meta_kernel(cnt_ref, meta_ref, *, nbp):
    cnt = cnt_ref[...]
    e = cnt.shape[0]
    nblk, end = _block_offsets(cnt)
    rep = lambda a, w: jnp.concatenate([a] * (w // LANES), axis=1)
    b = lax.broadcasted_iota(jnp.int32, (e, nbp), 1).astype(F32)
    blk_e = jnp.minimum(jnp.sum(jnp.where(rep(end, nbp) <= b, 1.0, 0.0), axis=0, keepdims=True), float(e - 1))
    row = lax.broadcasted_iota(jnp.int32, (e, nbp), 0).astype(F32)
    mine = row == blk_e
    left = rep(cnt + (end - nblk) * float(BLK), nbp) - b * float(BLK)
    nvalid = jnp.clip(jnp.sum(jnp.where(mine, left, 0.0), axis=0, keepdims=True), 0.0, float(BLK))
    nused = jnp.max(rep(end, nbp), axis=0, keepdims=True)
    later = jnp.logical_and(row > blk_e, rep(nblk, nbp) > 0.0)
    nxt = jnp.min(jnp.where(later, row, float(e)), axis=0, keepdims=True)
    nxt = jnp.where(nxt >= float(e), -1.0, nxt)
    meta_ref[...] = jnp.concatenate([blk_e, nvalid, nused, nxt, jnp.zeros((4, nbp), F32)],
                                    axis=0).astype(jnp.int32)


def _block_meta(cnt, nb):
    nbp = -(-nb // LANES) * LANES
    return pl.pallas_call(
        functools.partial(_meta_kernel, nbp=nbp),
        out_shape=jax.ShapeDtypeStruct((8, nbp), jnp.int32),
        compiler_params=pltpu.CompilerParams(vmem_limit_bytes=VMEM_LIMIT),
        name="expert_block_meta",
    )(cnt)


SC_WINDOW = 128


def _sc_mesh():
    return plsc.VectorSubcoreMesh(core_axis_name="core", subcore_axis_name="subcore")


def _sc_scatter_rows(rows, idx, nrows):
    n, width = rows.shape

    @pl.kernel(out_type=jax.ShapeDtypeStruct((nrows, width), rows.dtype), mesh=_sc_mesh())
    def scatter(rows_hbm, idx_hbm, out_hbm):
        def body(rows_vmem, idx_vmem):
            pltpu.sync_copy(rows_vmem, out_hbm.at[idx_vmem.at[0]])

        pltpu.emit_pipeline(
            body,
            grid=(n // SC_WINDOW, idx.shape[0]),
            in_specs=[pl.BlockSpec((SC_WINDOW, width), index_map=lambda i, k: (i, 0)),
                      pl.BlockSpec((1, SC_WINDOW), index_map=lambda i, k: (k, i))],
            out_specs=[],
            core_axis_name=("core", "subcore"),
            dimension_semantics=(pltpu.PARALLEL, pltpu.ARBITRARY),
        )(rows_hbm, idx_hbm)

    return scatter(rows, idx)


def _sc_gather_rows(src, idx):
    num = idx.shape[1]
    width = src.shape[1]

    @pl.kernel(out_type=jax.ShapeDtypeStruct((num, width), src.dtype), mesh=_sc_mesh())
    def gather(src_hbm, idx_hbm, out_hbm):
        def body(idx_vmem, out_vmem):
            pltpu.sync_copy(src_hbm.at[idx_vmem.at[0]], out_vmem)

        pltpu.emit_pipeline(
            body,
            grid=(num // SC_WINDOW,),
            in_specs=[pl.BlockSpec((1, SC_WINDOW), index_map=lambda i: (0, i))],
            out_specs=[pl.BlockSpec((SC_WINDOW, width), index_map=lambda i: (i, 0))],
            core_axis_name=("core", "subcore"),
            dimension_semantics=(pltpu.PARALLEL,),
        )(idx_hbm, out_hbm)

    return gather(src, idx)


def _experts_kernel(be_ref, nv_ref, nxt_ref, xa_ref, xb_ref, wg_hbm, wu_hbm, wd_hbm, oa_ref, ob_ref,
                    wgf, wuf, wdf, wgb, wub, wdb, sems, slot_ref):
    b = pl.program_id(0)

    def fetch(expert, slot):
        return [pltpu.make_async_copy(src.at[expert], dst.at[slot], sems.at[slot])
                for src, dst in ((wg_hbm, wgf), (wu_hbm, wuf), (wd_hbm, wdf))]

    @pl.when(b == 0)
    def _():
        slot_ref[0] = 0
        for cp in fetch(be_ref[0], 0):
            cp.start()

    prev = be_ref[jnp.maximum(b - 1, 0)]

    @pl.when(jnp.logical_or(b == 0, be_ref[b] != prev))
    def _():
        slot = slot_ref[0]
        for cp in fetch(be_ref[b], slot):
            cp.wait()

        @pl.when(nxt_ref[b] >= 0)
        def _():
            for cp in fetch(nxt_ref[b], 1 - slot):
                cp.start()

        wgb[...] = wgf[slot].astype(BF16)
        wub[...] = wuf[slot].astype(BF16)
        wdb[...] = wdf[slot].astype(BF16)
        slot_ref[0] = 1 - slot

    valid = lax.broadcasted_iota(jnp.int32, (xa_ref.shape[0], 1), 0) < nv_ref[b]
    zero = jnp.zeros((), U32)
    x = _unpack_rows(jnp.where(valid, xa_ref[...], zero), jnp.where(valid, xb_ref[...], zero))
    act = _silu(_dot(x, wgb[...])) * _dot(x, wub[...])
    oa_ref[...], ob_ref[...] = _pack_rows(_dot(act.astype(BF16), wdb[...]))


def _experts(blk_e, nvalid, nused, nxt_e, xs_a, xs_b, wg, wu, wd):
    p, dq = xs_a.shape
    d, de = wg.shape[1], wg.shape[2]
    rows_in = pl.BlockSpec((BLK, dq), lambda b, be, nv, nx: (b, 0))
    hbm = pl.BlockSpec(memory_space=pl.ANY)
    return pl.pallas_call(
        _experts_kernel,
        grid_spec=pltpu.PrefetchScalarGridSpec(
            num_scalar_prefetch=3,
            grid=(nused,),
            in_specs=[rows_in, rows_in, hbm, hbm, hbm],
            out_specs=[rows_in, rows_in],
            scratch_shapes=[pltpu.VMEM((2, d, de), F32), pltpu.VMEM((2, d, de), F32), pltpu.VMEM((2, de, d), F32),
                            pltpu.VMEM((d, de), BF16), pltpu.VMEM((d, de), BF16), pltpu.VMEM((de, d), BF16),
                            pltpu.SemaphoreType.DMA((2,)), pltpu.SMEM((1,), jnp.int32)],
        ),
        out_shape=[jax.ShapeDtypeStruct((p, dq), U32)] * 2,
        compiler_params=_params("arbitrary"),
        name="moe_experts",
    )(blk_e, nvalid, nxt_e, xs_a, xs_b, wg, wu, wd)


def _shared_kernel(ha_ref, hb_ref, sg_ref, su_ref, sd_ref, o_ref):
    hb = _unpack_rows(ha_ref[...], hb_ref[...])
    act = _silu(_dot(hb, sg_ref[...])) * _dot(hb, su_ref[...])
    o_ref[...] = _dot(act.astype(BF16), sd_ref[...]).astype(o_ref.dtype)


def _shared_expert(h2a, h2b, sh_wg, sh_wu, sh_wd, tt=512):
    n, dp = h2a.shape
    d = sh_wg.shape[0]
    tt = min(tt, n)
    consts = [sh_wg.astype(BF16), sh_wu.astype(BF16), sh_wd.astype(BF16)]
    packed_rows = pl.BlockSpec((tt, dp), lambda i: (i, 0))
    return pl.pallas_call(
        _shared_kernel,
        grid=(n // tt,),
        in_specs=[packed_rows, packed_rows] + [pl.BlockSpec(a.shape, lambda i: (0, 0)) for a in consts],
        out_specs=pl.BlockSpec((tt, d), lambda i: (i, 0)),
        out_shape=jax.ShapeDtypeStruct((n, d), BF16),
        compiler_params=_params("arbitrary"),
        name="shared_expert",
    )(h2a, h2b, *consts)


def _combine_kernel(w_ref, x1_ref, sh_ref, mod_ref, ga_ref, gb_ref, gf_ref, sel_ref, o_ref):
    ffn = sh_ref[...].astype(F32)
    wh, wl = _split2(w_ref[...])
    acc = None
    for kk in range(TOP_K):
        sel = sel_ref[kk]
        wk = _tn(wh, sel) + _tn(wl, sel)
        a_lo, a_hi = _unpack_halves(ga_ref[kk])
        b_lo, b_hi = _unpack_halves(gb_ref[kk])
        parts = [a_lo * wk, b_lo * wk, a_hi * wk, b_hi * wk]
        acc = parts if acc is None else [p + q for p, q in zip(acc, parts)]
    ffn = ffn + jnp.concatenate(acc, axis=1)
    xo = x1_ref[...] + mod_ref[5:6, :] * ffn
    ms = jnp.mean(xo * xo, axis=-1, keepdims=True)
    o_ref[...] = xo * lax.rsqrt(ms + NORM_EPS) * gf_ref[...]


def _combine(wsel, x1, shared, mod, ga, gb, normf_g, tok0, seq, tt=256):
    n, d = x1.shape
    part = ga.shape[1]
    tt = min(tt, seq, part)
    per = seq // tt
    off = tok0 // tt
    dq = ga.shape[2]
    sel = jnp.asarray(np.broadcast_to(np.eye(TOP_K)[:, :, None], (TOP_K, TOP_K, dq)), BF16)
    consts = [normf_g.reshape(1, d), sel]
    const = lambda a: pl.BlockSpec(a.shape, lambda i: (0,) * a.ndim)
    rows = pl.BlockSpec((tt, d), lambda i: (off + i, 0))
    gathered = pl.BlockSpec((TOP_K, tt, dq), lambda i: (0, i, 0))
    return pl.pallas_call(
        _combine_kernel,
        grid=(part // tt,),
        in_specs=[pl.BlockSpec((TOP_K, tt), lambda i: (0, off + i)),
                  rows, rows,
                  pl.BlockSpec((None,) + mod.shape[1:], lambda i: ((off + i) // per, 0, 0)),
                  gathered, gathered] + [const(a) for a in consts],
        out_specs=rows,
        out_shape=jax.ShapeDtypeStruct((n, d), F32),
        input_output_aliases={1: 0},
        compiler_params=_params("arbitrary"),
        name="moe_combine",
    )(wsel, x1, shared, mod, ga, gb, *consts)


COMBINE_PARTS = 4


def _moe(x1, h2a, h2b, mod, eid, wsel, exp_wg, exp_wu, exp_wd, sh_wg, sh_wu, sh_wd, normf_g):
    n = h2a.shape[0]
    e = exp_wg.shape[0]
    nb = (n * TOP_K + e * (BLK - 1)) // BLK
    rank, cnt = _expert_ranks(eid, e)
    dest = _destinations(cnt, eid, rank)
    meta = _block_meta(cnt, nb)
    xs_a = _sc_scatter_rows(h2a, dest, nb * BLK)
    xs_b = _sc_scatter_rows(h2b, dest, nb * BLK)
    shared = _shared_expert(h2a, h2b, sh_wg, sh_wu, sh_wd)
    ys_a, ys_b = _experts(meta[0, :nb], meta[1, :nb], meta[2, 0], meta[3, :nb], xs_a, xs_b,
                          exp_wg, exp_wu, exp_wd)
    bsz, seq, d = x1.shape
    out = x1.reshape(n, d)
    part = n // COMBINE_PARTS
    for j in range(COMBINE_PARTS):
        idx = dest[:, j * part:(j + 1) * part].reshape(1, TOP_K * part)
        ga = _sc_gather_rows(ys_a, idx).reshape(TOP_K, part, -1)
        gb = _sc_gather_rows(ys_b, idx).reshape(TOP_K, part, -1)
        out = _combine(wsel, out, shared, mod, ga, gb, normf_g, j * part, seq)
    return out


def kernel(x, c, norm1_g, norm2_g, normf_g, w_ada, b_ada, w_in, w_out, hy_conv_w, hy_conv_b, hy_pos_w1, hy_pos_b1, hy_pos_w2, hy_pos_b2, hy_pos_w3, hy_sin_freq, hy_skip, rw_mu, rw_w0, rw_w_up, rw_a0, rw_a_up, rw_g_up, rw_k_k, rw_k_a, rw_r_k, rw_ln_w, rw_ln_b, router_w, router_bias, exp_w_gate, exp_w_up, exp_w_down, sh_w_gate, sh_w_up, sh_w_down):
    bsz, seq, d = x.shape
    depth = w_ada.shape[0]
    assert depth == 1, "the final norm is fused into the last kernel of a single layer"
    for l in range(depth):
        mod = _modulation(c, w_ada[l], b_ada[l]).reshape(bsz, -1, d)
        uhy, rkvk, lwa, g, bonus = _projection(
            x, mod, norm1_g[l], w_in[l], hy_conv_w[l], hy_conv_b[l], rw_mu[l], rw_w0[l], rw_w_up[l],
            rw_a0[l], rw_a_up[l], rw_g_up[l], rw_k_k[l], rw_k_a[l], rw_r_k[l])
        k2, ss = _hyena_filters(seq, hy_pos_w1[l], hy_pos_b1[l], hy_pos_w2[l], hy_pos_b2[l],
                                hy_pos_w3[l], hy_sin_freq[l])
        khat = _filter_spectrum(k2, ss, seq)
        z, z_col = uhy, 0
        for order in range(HYENA_ORDER):
            z = _long_conv_gate(z, z_col, uhy, (order + 1) * D_HYENA, khat, hy_skip[l], order)
            z_col = 0
        o_f, o_b = _wkv(rkvk, lwa, rw_k_a[l])
        x1, h2a, h2b, eid, wsel = _mix_out(x, mod, z, o_f, o_b, g, bonus, rw_ln_w[l], rw_ln_b[l], w_out[l],
                                           norm2_g[l], router_w[l], router_bias[l])
        x = _moe(x1, h2a, h2b, mod, eid, wsel, exp_w_gate[l], exp_w_up[l], exp_w_down[l],
                 sh_w_gate[l], sh_w_up[l], sh_w_down[l], normf_g)
        x = x.reshape(bsz, seq, d)
    return x
```

```python
import functools
import math

import jax
import jax.numpy as jnp
import numpy as np
from jax import lax
from jax.experimental import pallas as pl
from jax.experimental.pallas import tpu as pltpu
from jax.experimental.pallas import tpu_sc as plsc

F32 = jnp.float32
BF16 = jnp.bfloat16

LANES = 128
MXU_DIM = 256
VMEM_LIMIT = 56 * 1024 * 1024

D_HYENA = 512
D_RWKV = 512
HEAD = 64
N_HEADS = D_RWKV // HEAD
HYENA_ORDER = 2
FILTER_BANDS = 16
DECAY_TARGET = 1e-2
FAST_DECAY_PCT = 0.3
SLOW_DECAY_PCT = 1.5
FILTER_NORM_EPS = 1e-6
DECAY_LORA = 32
ICLR_LORA = 32
GATE_LORA = 96
GN_EPS = 64e-5
NORM_EPS = 1e-6
N_EXPERTS = 256
TOP_K = 8
N_GROUPS = 8
TOPK_GROUPS = 4
ROUTE_SCALE = 2.5
D_EXPERT = 256


def _params(*sem):
    return pltpu.CompilerParams(dimension_semantics=sem, vmem_limit_bytes=VMEM_LIMIT)


def _split2(a):
    hi = a.astype(BF16)
    lo = (a - hi.astype(F32)).astype(BF16)
    return hi, lo


def _dot(a, b):
    return jnp.dot(a, b, preferred_element_type=F32)


def _dot3(a, b):
    ah, al = _split2(a)
    bh, bl = _split2(b)
    return _dot(ah, bh) + (_dot(ah, bl) + _dot(al, bh))


def _dot_exact_rhs(a, b_bf16):
    ah, al = _split2(a)
    return _dot(ah, b_bf16) + _dot(al, b_bf16)


def _silu(x):
    return x * jax.nn.sigmoid(x)


U32 = jnp.int32


def _pack_halves(x):
    w = x.shape[1] // 2
    return pltpu.pack_elementwise([x[:, :w], x[:, w:]], packed_dtype=BF16)


def _unpack_halves(p):
    lo = pltpu.unpack_elementwise(p, index=0, packed_dtype=BF16, unpacked_dtype=F32)
    hi = pltpu.unpack_elementwise(p, index=1, packed_dtype=BF16, unpacked_dtype=F32)
    return lo, hi


def _pack_rows(x):
    packed = _pack_halves(x)
    half = packed.shape[1] // 2
    return packed[:, :half], packed[:, half:]


def _unpack_rows(a, b):
    a_lo, a_hi = _unpack_halves(a)
    b_lo, b_hi = _unpack_halves(b)
    return jnp.concatenate([a_lo.astype(BF16), b_lo.astype(BF16), a_hi.astype(BF16), b_hi.astype(BF16)], axis=1)


def _mod_kernel(c_ref, w_ref, b_ref, o_ref):
    o_ref[...] = _dot3(_silu(c_ref[...]), w_ref[...]) + b_ref[...]


def _modulation(c, w_ada, b_ada):
    bsz, d = c.shape
    n = w_ada.shape[1]
    blk = 1024
    return pl.pallas_call(
        _mod_kernel,
        grid=(n // blk,),
        in_specs=[
            pl.BlockSpec((bsz, d), lambda j: (0, 0)),
            pl.BlockSpec((d, blk), lambda j: (0, j)),
            pl.BlockSpec((1, blk), lambda j: (0, j)),
        ],
        out_specs=pl.BlockSpec((bsz, blk), lambda j: (0, j)),
        out_shape=jax.ShapeDtypeStruct((bsz, n), F32),
        compiler_params=_params("arbitrary"),
        name="adaln_mod",
    )(c, w_ada, b_ada.reshape(1, n))


def _filter_kernel(band_ref, w1_ref, b1_ref, w2_ref, b2_ref, w3_ref, freq_ref, delta_ref,
                   k_ref, ss_ref, *, seq, rows):
    half = pl.program_id(0)
    i = pl.program_id(1)
    r = lax.broadcasted_iota(jnp.int32, (rows, LANES), 0) + i * rows
    pos = jnp.where(half == 0, r, seq - r).astype(F32)
    tt = pos / float(max(seq - 1, 1))
    lane = lax.broadcasted_iota(jnp.int32, (rows, LANES), 1)
    feats = jnp.where(lane == 0, tt, jnp.sin(pos * band_ref[0:1, :] + band_ref[1:2, :]))
    freq = freq_ref[...]
    hdn = jnp.sin(freq * (_dot3(feats, w1_ref[...]) + b1_ref[...]))
    for j in range(w2_ref.shape[0]):
        hdn = jnp.sin(freq * (_dot3(hdn, w2_ref[j]) + b2_ref[j]))
    filt = _dot3(hdn, w3_ref[...])
    filt = filt * jnp.exp(-tt[:, :1] * delta_ref[...])
    valid = jnp.logical_or(half == 0, r[:, :1] > 0)
    filt = jnp.where(valid, filt, 0.0)
    k_ref[...] = filt

    @pl.when(jnp.logical_and(half == 0, i == 0))
    def _():
        ss_ref[...] = jnp.zeros_like(ss_ref)

    ss_ref[...] += jnp.broadcast_to(jnp.sum(filt * filt, axis=0, keepdims=True), ss_ref.shape)


def _hyena_filters(seq, pw1, pb1, pw2, pb2, pw3, freq):
    width = pw1.shape[1]
    ncol = HYENA_ORDER * D_HYENA
    rows = min(seq, 512)
    bands = np.zeros((2, LANES), np.float64)
    lin = np.linspace(1e-4, FILTER_BANDS - 1, FILTER_BANDS) * (2.0 * math.pi / seq)
    bands[0, 1:1 + FILTER_BANDS] = lin
    bands[1, 1:1 + FILTER_BANDS] = 0.5 * math.pi
    bands[0, 1 + FILTER_BANDS:1 + 2 * FILTER_BANDS] = -lin
    bands = jnp.asarray(bands, F32)
    deltas = np.abs(np.linspace(math.log(DECAY_TARGET) / SLOW_DECAY_PCT,
                                math.log(DECAY_TARGET) / FAST_DECAY_PCT, D_HYENA))
    deltas = jnp.asarray(np.tile(deltas, HYENA_ORDER)[None], F32)
    w1 = jnp.zeros((LANES, width), F32).at[:pw1.shape[0]].set(pw1)
    w3 = pw3.reshape(width, HYENA_ORDER, 2, D_HYENA).transpose(2, 0, 1, 3).reshape(2, width, ncol)
    nt = seq // rows
    full = lambda *shape: pl.BlockSpec(shape, lambda h, i: (0,) * len(shape))
    return pl.pallas_call(
        functools.partial(_filter_kernel, seq=seq, rows=rows),
        grid=(2, nt),
        in_specs=[
            full(2, LANES), full(LANES, width), full(1, width),
            full(pw2.shape[0], width, width), full(pw2.shape[0], 1, width),
            pl.BlockSpec((None, width, ncol), lambda h, i: (h, 0, 0)),
            full(1, width), full(1, ncol),
        ],
        out_specs=[
            pl.BlockSpec((rows, ncol), lambda h, i: (h * nt + i, 0)),
            pl.BlockSpec((8, ncol), lambda h, i: (0, 0)),
        ],
        out_shape=[jax.ShapeDtypeStruct((2 * seq, ncol), F32),
                   jax.ShapeDtypeStruct((8, ncol), F32)],
        compiler_params=_params("arbitrary", "arbitrary"),
        name="hyena_filters",
    )(bands, w1, pb1.reshape(1, width), pw2, pb2.reshape(pw2.shape[0], 1, width), w3,
      freq.reshape(1, width), deltas)


N1 = LANES
UNROLL = 8


def _dft_tables(seq):
    tables = _dft_tables_np(seq)
    return tuple(jnp.asarray(t, BF16) for t in tables[:5]) + tables[5:]


def _dft_tables_np(seq):
    m = 2 * seq
    n2 = m // N1
    n2h = n2 // 2
    n1 = np.arange(N1)[:, None, None]
    f2 = np.arange(n2)[None, :, None]
    k2 = np.arange(n2)[None, None, :]
    th = 2.0 * np.pi * (n1 * f2 / m + (k2 * f2 % n2) / n2)
    fwd_a = np.concatenate([np.cos(th), -np.sin(th)], axis=1)
    tht = np.transpose(th, (0, 2, 1))
    inv_a = np.concatenate([np.cos(tht), -np.sin(tht)], axis=2)[:, :n2h] / m
    a = np.arange(N1)
    ph = 2.0 * np.pi * np.outer(a, a) / N1
    c, s = np.cos(ph), np.sin(ph)
    fwd_b = np.block([[c, s], [-s, c]])
    inv_b = np.block([[c, -s], [s, c]])
    return fwd_a, fwd_a[:, :, :n2h], inv_a, fwd_b, inv_b, n2, n2h


def _stage_a_fwd(x_ref, wa_ref, y_ref, n2, scale=None):
    def body(i, carry):
        trips = [i * UNROLL + j for j in range(UNROLL)]
        xs = [x_ref[pl.ds(n1, wa_ref.shape[2], stride=N1), :] for n1 in trips]
        if scale is not None:
            xs = [x * scale for x in xs]
        prods = [_dot(wa_ref[n1], x.astype(BF16)) for n1, x in zip(trips, xs)]
        for n1, a in zip(trips, prods):
            y_ref[pl.ds(n1, n2, stride=2 * N1), :] = a[:n2]
            y_ref[pl.ds(N1 + n1, n2, stride=2 * N1), :] = a[n2:]
        return carry
    lax.fori_loop(0, N1 // UNROLL, body, 0)


def _filter_fft_kernel(k_ref, ss_ref, wa_ref, fb_ref, o_ref, y_ref, *, n2):
    scale = lax.rsqrt(ss_ref[0:1, :] + FILTER_NORM_EPS)
    _stage_a_fwd(k_ref, wa_ref, y_ref, n2, scale=scale)

    unr = min(UNROLL, n2)

    def body(i, carry):
        trips = [i * unr + j for j in range(unr)]
        ys = [y_ref[pl.ds(pl.multiple_of(f2 * 2 * N1, 2 * N1), 2 * N1), :].astype(BF16) for f2 in trips]
        for j in range(0, unr, 2):
            z = _dot(fb_ref[...], jnp.concatenate(ys[j:j + 2], axis=1))
            o_ref[trips[j]] = z[:, :LANES]
            o_ref[trips[j + 1]] = z[:, LANES:]
        return carry
    lax.fori_loop(0, n2 // unr, body, 0)


def _filter_spectrum(k2, ss, seq):
    fwd_a, _, _, fwd_b, _, n2, _ = _dft_tables(seq)
    ncol = k2.shape[1]
    nblk = ncol // LANES
    return pl.pallas_call(
        functools.partial(_filter_fft_kernel, n2=n2),
        grid=(nblk,),
        in_specs=[
            pl.BlockSpec((2 * seq, LANES), lambda c: (0, c)),
            pl.BlockSpec((8, LANES), lambda c: (0, c)),
            pl.BlockSpec(fwd_a.shape, lambda c: (0, 0, 0)),
            pl.BlockSpec(fwd_b.shape, lambda c: (0, 0)),
        ],
        out_specs=pl.BlockSpec((None, n2, 2 * N1, LANES), lambda c: (c, 0, 0, 0)),
        out_shape=jax.ShapeDtypeStruct((nblk, n2, 2 * N1, LANES), F32),
        scratch_shapes=[pltpu.VMEM((n2 * 2 * N1, LANES), F32)],
        compiler_params=_params("arbitrary"),
        name="hyena_filter_fft",
    )(k2, ss, fwd_a, fwd_b)


TILE = 8
N1_GROUPS = N1 // TILE


def _tile_tables(seq):
    _, fwd_a, inv_a, _, _, n2, n2h = _dft_tables_np(seq)
    eye = np.eye(TILE)
    fa = fwd_a.reshape(N1_GROUPS, TILE, 2 * n2, n2h)
    wa = np.einsum("qjrn,jk->qrjnk", fa, eye).reshape(N1_GROUPS, 2 * n2 * TILE, n2h * TILE)
    ia = inv_a.reshape(N1_GROUPS, TILE, n2h, 2 * n2)
    vc = np.einsum("qjnr,jk->qnjrk", ia, eye).reshape(N1_GROUPS, n2h * TILE, 2 * n2 * TILE)
    return jnp.asarray(wa, BF16), jnp.asarray(vc, BF16)


def _conv_kernel(u_ref, g_ref, skip_ref, fb_ref, ib_ref, kh_hbm, wa_hbm, vc_hbm, o_ref,
                 y_ref, kh_ref, wa_ref, vc_ref, sem, *, n2, n2h, kh_first):
    c_id, b_id = pl.program_id(0), pl.program_id(1)

    @pl.when(jnp.logical_and(c_id == 0, b_id == 0))
    def _():
        for src, dst in ((wa_hbm, wa_ref), (vc_hbm, vc_ref)):
            cp = pltpu.make_async_copy(src, dst, sem)
            cp.start()
            cp.wait()

    @pl.when(b_id == 0)
    def _():
        cp = pltpu.make_async_copy(kh_hbm.at[kh_first + c_id], kh_ref, sem)
        cp.start()
        cp.wait()

    def y_tile(rf, base):
        ri, f2 = divmod(rf, n2)
        return pl.ds(f2 * 2 * N1 + ri * N1 + base, TILE)

    def stage_a(q, carry):
        base = pl.multiple_of(q * TILE, TILE)
        x = jnp.concatenate([u_ref[pl.ds(N1 * m + base, TILE), :] for m in range(n2h)], axis=0)
        r = _dot(wa_ref[q], x.astype(BF16))
        for rf in range(2 * n2):
            y_ref[y_tile(rf, base), :] = r[rf * TILE:(rf + 1) * TILE]
        return carry
    lax.fori_loop(0, N1_GROUPS, stage_a, 0, unroll=4)

    unr = min(UNROLL, n2)

    def mid(i, carry):
        trips = [i * unr + j for j in range(unr)]
        offs = [pl.multiple_of(f2 * 2 * N1, 2 * N1) for f2 in trips]
        wide = lambda blocks: [jnp.concatenate(blocks[j:j + 2], axis=1) for j in range(0, len(blocks), 2)]
        ys = wide([y_ref[pl.ds(off, 2 * N1), :].astype(BF16) for off in offs])
        khs = wide([kh_ref[f2] for f2 in trips])
        zs = [_dot(fb_ref[...], y) for y in ys]
        ps = []
        for z, kh in zip(zs, khs):
            zr, zi = z[:N1], z[N1:]
            kr, ki = kh[:N1], kh[N1:]
            ps.append(jnp.concatenate([zr * kr - zi * ki, zr * ki + zi * kr], axis=0).astype(BF16))
        gs = [_dot(ib_ref[...], p) for p in ps]
        for j, g in enumerate(gs):
            y_ref[pl.ds(offs[2 * j], 2 * N1), :] = g[:, :LANES]
            y_ref[pl.ds(offs[2 * j + 1], 2 * N1), :] = g[:, LANES:]
        return carry
    lax.fori_loop(0, n2 // unr, mid, 0)

    skip = skip_ref[...]

    def stage_c(q, carry):
        base = pl.multiple_of(q * TILE, TILE)
        g = jnp.concatenate([y_ref[y_tile(rf, base), :] for rf in range(2 * n2)], axis=0)
        conv = _dot(vc_ref[q], g.astype(BF16))
        for m in range(n2h):
            rows = pl.ds(N1 * m + base, TILE)
            o_ref[rows, :] = g_ref[rows, :] * (conv[m * TILE:(m + 1) * TILE] + u_ref[rows, :] * skip)
        return carry
    lax.fori_loop(0, N1_GROUPS, stage_c, 0, unroll=4)


def _long_conv_gate(u, u_col, gate, gate_col, khat, skip, order):
    bsz, seq, _ = u.shape
    ch = D_HYENA
    _, _, _, fwd_b, inv_b, n2, n2h = _dft_tables(seq)
    wa, vc = _tile_tables(seq)
    nblk = ch // LANES
    const = lambda a: pl.BlockSpec(a.shape, lambda c, b: (0,) * a.ndim)
    at = lambda col: pl.BlockSpec((None, seq, LANES), lambda c, b: (b, 0, col // LANES + c))
    hbm = pl.BlockSpec(memory_space=pl.ANY)
    return pl.pallas_call(
        functools.partial(_conv_kernel, n2=n2, n2h=n2h, kh_first=order * nblk),
        grid=(nblk, bsz),
        in_specs=[
            at(u_col), at(gate_col),
            pl.BlockSpec((1, LANES), lambda c, b: (0, c)),
            const(fwd_b), const(inv_b), hbm, hbm, hbm,
        ],
        out_specs=at(0),
        out_shape=jax.ShapeDtypeStruct((bsz, seq, ch), F32),
        scratch_shapes=[pltpu.VMEM((n2 * 2 * N1, LANES), F32), pltpu.VMEM(khat.shape[1:], F32),
                        pltpu.VMEM(wa.shape, BF16), pltpu.VMEM(vc.shape, BF16), pltpu.SemaphoreType.DMA(())],
        compiler_params=_params("arbitrary", "arbitrary"),
        name=f"hyena_conv{order}",
    )(u, gate, skip[order].reshape(1, ch), fwd_b, inv_b, khat, wa, vc)


HALO = 8


def _shift_rows(p, k):
    return pltpu.roll(p, k % p.shape[0], axis=0)


def _proj_kernel(xp_ref, x_ref, xn_ref, mod_ref, g1_ref, why_ref, wrkv_ref, wlora_ref,
                 cw_ref, cb_ref, murkv_ref, mulora_ref, w0_ref, a0_ref, wwah_ref, wwal_ref, gup_ref,
                 kk_ref, ka_ref, rk_ref, ones_ref,
                 uhy_ref, rkvk_ref, lwa_ref, g_ref, bonus_ref,
                 *, tt, nt):
    i = pl.program_id(1)
    xe = jnp.concatenate([xp_ref[...], x_ref[...], xn_ref[...]], axis=0)
    ms = jnp.mean(xe * xe, axis=-1, keepdims=True)
    h = xe * lax.rsqrt(ms + NORM_EPS) * g1_ref[...]
    h = h * (1.0 + mod_ref[1:2, :]) + mod_ref[0:1, :]
    row = lax.broadcasted_iota(jnp.int32, (tt + 2 * HALO, 1), 0)
    inside = jnp.logical_and(jnp.logical_or(row >= HALO, i > 0),
                             jnp.logical_or(row < tt + HALO, i < nt - 1))
    hb = jnp.where(inside, h, 0.0).astype(BF16)
    mid = slice(HALO, tt + HALO)

    p = _dot(hb, why_ref[...])
    u = (_shift_rows(p, 1) * cw_ref[0:1, :] + p * cw_ref[1:2, :]
         + _shift_rows(p, -1) * cw_ref[2:3, :] + cb_ref[...])
    uhy_ref[...] = u[mid]

    p = _dot(hb, wrkv_ref[...])
    p = p + murkv_ref[...] * (0.5 * (_shift_rows(p, 1) + _shift_rows(p, -1)) - p)
    p = p[mid]
    c = D_RWKV
    r, k, v = p[:, :c], p[:, c:2 * c], p[:, 2 * c:]
    rkvk_ref[:, :3 * c] = p

    q = _dot(hb, wlora_ref[...])
    q = q + mulora_ref[...] * (0.5 * (_shift_rows(q, 1) + _shift_rows(q, -1)) - q)
    q = q[mid]
    wa = q[:, :LANES]
    lane = lax.broadcasted_iota(jnp.int32, wa.shape, 1)
    wa = jnp.where(lane < 2 * DECAY_LORA, jnp.tanh(wa), wa)
    wah, wal = _split2(wa)
    up = _dot(wah, wwah_ref[...]) + (_dot(wah, wwal_ref[...]) + _dot(wal, wwah_ref[...]))
    lw = -math.exp(-0.5) * jax.nn.sigmoid(w0_ref[...] + up[:, :2 * c])
    a = jax.nn.sigmoid(a0_ref[...] + up[:, 2 * c:])
    for dd in range(2):
        lwa_ref[:, 2 * dd * c:(2 * dd + 1) * c] = lw[:, dd * c:(dd + 1) * c]
        lwa_ref[:, (2 * dd + 1) * c:(2 * dd + 2) * c] = a[:, dd * c:(dd + 1) * c]
    g_ref[...] = _dot3(jax.nn.sigmoid(q[:, LANES:]), gup_ref[...])

    ones = ones_ref[...]
    kk = k * kk_ref[...]
    nrm = jnp.sqrt(_dot_exact_rhs(kk * kk, ones))
    rkvk_ref[:, 3 * c:] = kk / jnp.maximum(nrm, 1e-12)
    ka = ka_ref[...]
    ksum = k * (2.0 + (a[:, :c] + a[:, c:] - 2.0) * ka)
    bonus_ref[...] = _dot_exact_rhs(r * ksum * rk_ref[...], ones) * v


def _head_ones():
    hid = np.arange(D_RWKV) // HEAD
    return jnp.asarray(hid[:, None] == hid[None, :], BF16)


def _projection(x, mod, norm1_g, w_in, hy_conv_w, hy_conv_b, rw_mu, rw_w0, rw_w_up, rw_a0,
                rw_a_up, rw_g_up, rw_k_k, rw_k_a, rw_r_k, tt=512):
    bsz, seq, d = x.shape
    tt = min(tt, seq)
    nt = seq // tt
    c = D_RWKV
    hy = (HYENA_ORDER + 1) * D_HYENA
    nlora = 2 * LANES
    w_hy = w_in[:, :hy].astype(BF16)
    w_rkv = w_in[:, hy:hy + 3 * c].astype(BF16)
    w_lora = jnp.zeros((d, nlora), F32).at[:, :w_in.shape[1] - hy - 3 * c].set(w_in[:, hy + 3 * c:]).astype(BF16)
    mu_rkv = rw_mu[:3 * c].reshape(1, 3 * c)
    mu_lora = jnp.zeros((1, nlora), F32).at[0, :rw_mu.shape[0] - 3 * c].set(rw_mu[3 * c:])
    wwa = jnp.zeros((LANES, 4 * c), F32)
    for dd in range(2):
        wwa = wwa.at[dd * DECAY_LORA:(dd + 1) * DECAY_LORA, dd * c:(dd + 1) * c].set(rw_w_up[dd])
        wwa = wwa.at[2 * DECAY_LORA + dd * ICLR_LORA:2 * DECAY_LORA + (dd + 1) * ICLR_LORA,
                     2 * c + dd * c:2 * c + (dd + 1) * c].set(rw_a_up[dd])
    gup = jnp.zeros((LANES, c), F32).at[:GATE_LORA].set(rw_g_up)
    row = lambda a: a.reshape(1, -1)

    nb8 = seq // HALO
    tb = tt // HALO
    const = lambda a: pl.BlockSpec(a.shape, lambda b, i: (0,) * a.ndim, pipeline_mode=pl.Buffered(1))
    tile = lambda w: pl.BlockSpec((None, tt, w), lambda b, i: (b, i, 0))
    ins = [
        (x, pl.BlockSpec((None, HALO, d), lambda b, i: (b, jnp.maximum(i * tb - 1, 0), 0))),
        (x, pl.BlockSpec((None, tt, d), lambda b, i: (b, i, 0))),
        (x, pl.BlockSpec((None, HALO, d), lambda b, i: (b, jnp.minimum((i + 1) * tb, nb8 - 1), 0))),
        (mod, pl.BlockSpec((None,) + mod.shape[1:], lambda b, i: (b, 0, 0))),
    ]
    consts = [row(norm1_g), w_hy, w_rkv, w_lora, hy_conv_w, row(hy_conv_b), mu_rkv, mu_lora,
              row(rw_w0), row(rw_a0), *_split2(wwa), gup, row(rw_k_k), row(rw_k_a), row(rw_r_k), _head_ones()]
    ins += [(a, const(a)) for a in consts]
    widths = [hy, 4 * c, 4 * c, c, c]
    return pl.pallas_call(
        functools.partial(_proj_kernel, tt=tt, nt=nt),
        grid=(bsz, nt),
        in_specs=[s for _, s in ins],
        out_specs=[tile(w) for w in widths],
        out_shape=[jax.ShapeDtypeStruct((bsz, seq, w), F32) for w in widths],
        compiler_params=_params("arbitrary", "arbitrary"),
        name="input_projection",
    )(*[a for a, _ in ins])


CHUNK = HEAD
GROUP = MXU_DIM // HEAD


def _nt(a, b):
    return lax.dot_general(a, b, (((1,), (1,)), ((), ())), preferred_element_type=F32)


def _tn(a, b):
    return lax.dot_general(a, b, (((0,), (0,)), ((), ())), preferred_element_type=F32)


def _wkv_direction(r, k, v, kk, lw, a, ka, s_ref, reverse):
    c = CHUNK
    ti = lax.broadcasted_iota(jnp.int32, (c, c), 0)
    si = lax.broadcasted_iota(jnp.int32, (c, c), 1)
    tri = (si >= ti) if reverse else (si <= ti)
    cum = _dot_exact_rhs_lhs(jnp.where(tri, 1.0, 0.0).astype(BF16), lw)
    tot = jnp.sum(lw, axis=0, keepdims=True)
    w_incl = jnp.exp(cum)
    w_prev = jnp.exp(cum - lw)
    w_inv = jnp.exp(-cum)
    w_end = jnp.exp(tot - cum)
    w_tot = jnp.exp(tot)
    kd = k * (1.0 + (a - 1.0) * ka)
    b = kk * a
    a_w = -kk * w_prev
    r_w = r * w_incl
    b_w = b * w_inv
    k_w = kd * w_inv
    b_e = b * w_end
    k_e = kd * w_end

    m = MXU_DIM
    ri = lax.broadcasted_iota(jnp.int32, (m, m), 0)
    ci = lax.broadcasted_iota(jnp.int32, (m, m), 1)
    head_mask = (ri // HEAD) == (ci // HEAD)
    tl = lax.broadcasted_iota(jnp.int32, (c, m), 0)
    sl = lax.broadcasted_iota(jnp.int32, (c, m), 1) % c
    strict = (sl > tl) if reverse else (sl < tl)
    incl = (sl >= tl) if reverse else (sl <= tl)
    eye = jnp.where(sl == tl, 1.0, 0.0)
    both = lambda top, bot: jnp.concatenate([top, bot], axis=0)

    def stack(xg):
        xb = xg.astype(BF16)
        return jnp.where(head_mask, jnp.concatenate([xb] * GROUP, axis=0), jnp.zeros((), BF16))

    streams = []
    for g in range(D_RWKV // m):
        sl_g = slice(g * m, (g + 1) * m)
        streams.append(dict(
            ar=both(a_w[:, sl_g], r_w[:, sl_g]).astype(BF16),
            b_st=stack(b_w[:, sl_g]), k_st=stack(k_w[:, sl_g]), v_st=stack(v[:, sl_g]),
            v=v[:, sl_g], bk=both(b_e[:, sl_g], k_e[:, sl_g]).astype(BF16),
            w_tot=w_tot[:, sl_g], s_ref=s_ref.at[g],
            strict=strict, incl=incl, eye=eye, head_mask=head_mask, stack=stack))
    return streams


def _wkv_streams_step(streams):
    c = CHUNK
    both = lambda top, bot: jnp.concatenate([top, bot], axis=0)
    for st in streams:
        st["s"] = st["s_ref"][...]
        st["xb"] = _nt(st["ar"], st["b_st"])
        st["xk"] = _nt(st["ar"], st["k_st"])
        st["xs"] = _nt(st["ar"], st["s"].astype(BF16))
    for st in streams:
        m_k = both(jnp.where(st["strict"], st["xk"][:c], 0.0), jnp.where(st["incl"], st["xk"][c:], 0.0))
        st["kv"] = _dot(m_k.astype(BF16), st["v_st"])
        st["rhs"] = st["xs"][:c] + st["kv"][:c]
        st["pw"] = jnp.where(st["strict"], st["xb"][:c], 0.0)
        st["t"] = st["eye"] + st["pw"]
        st["p_st"] = st["stack"](st["pw"])
    levels = int(math.log2(c)) - 1
    for st in streams:
        st["pw"] = _dot(st["pw"].astype(BF16), st["p_st"])
        st["p_st"] = st["stack"](st["pw"])
    for lvl in range(1, levels + 1):
        for st in streams:
            if lvl < levels:
                prod = _dot(both(st["pw"], st["t"]).astype(BF16), st["p_st"])
                st["pw"] = prod[:c]
                st["t"] = st["t"] + prod[c:]
                st["p_st"] = st["stack"](st["pw"])
            else:
                st["t"] = st["t"] + _dot(st["t"].astype(BF16), st["p_st"])
    for st in streams:
        st["u"] = _dot(st["t"].astype(BF16), st["stack"](st["rhs"]))
    outs = []
    for st in streams:
        m_rb = jnp.where(st["incl"], st["xb"][c:], 0.0)
        outs.append(st["xs"][c:] + _dot(m_rb.astype(BF16), st["stack"](st["u"])) + st["kv"][c:])
        uv = both(st["u"], st["v"]).astype(BF16)
        st["s_ref"][...] = st["s"] * st["w_tot"] + jnp.where(st["head_mask"], _tn(uv, st["bk"]), 0.0)
    return outs


def _dot_exact_rhs_lhs(tri_bf16, x):
    xh, xl = _split2(x)
    return _dot(tri_bf16, xh) + _dot(tri_bf16, xl)


def _wkv_kernel(rkvk_f, lwa_f, rkvk_b, lwa_b, ka_ref, of_ref, ob_ref, s_ref, *, nch):
    @pl.when(pl.program_id(1) == 0)
    def _():
        s_ref[...] = jnp.zeros_like(s_ref)

    ka = ka_ref[...]
    c = D_RWKV

    def operands(rkvk_ref, lwa_ref, rows):
        x = rkvk_ref[rows, :]
        la = lwa_ref[rows, :]
        return x[:, :c], x[:, c:2 * c], x[:, 2 * c:3 * c], x[:, 3 * c:], la[:, :c], la[:, c:]

    for ci in range(nch):
        rows_f = slice(ci * CHUNK, (ci + 1) * CHUNK)
        rows_b = slice((nch - 1 - ci) * CHUNK, (nch - ci) * CHUNK)
        fwd = _wkv_direction(*operands(rkvk_f, lwa_f, rows_f), ka, s_ref.at[0], False)
        bwd = _wkv_direction(*operands(rkvk_b, lwa_b, rows_b), ka, s_ref.at[1], True)
        outs = _wkv_streams_step(fwd + bwd)
        of_ref[rows_f, :] = jnp.concatenate(outs[:len(fwd)], axis=1)
        ob_ref[rows_b, :] = jnp.concatenate(outs[len(fwd):], axis=1)


WKV_CHUNKS_PER_STEP = 4


def _wkv(rkvk, lwa, rw_k_a):
    bsz, seq, _ = rkvk.shape
    c = D_RWKV
    nch = WKV_CHUNKS_PER_STEP if seq % (WKV_CHUNKS_PER_STEP * CHUNK) == 0 else 1
    rows = nch * CHUNK
    nb = seq // rows
    fwd = lambda w, lane_blk: pl.BlockSpec((None, rows, w), lambda b, j: (b, j, lane_blk))
    bwd = lambda w, lane_blk: pl.BlockSpec((None, rows, w), lambda b, j: (b, nb - 1 - j, lane_blk))
    return pl.pallas_call(
        functools.partial(_wkv_kernel, nch=nch),
        grid=(bsz, nb),
        in_specs=[fwd(4 * c, 0), fwd(2 * c, 0), bwd(4 * c, 0), bwd(2 * c, 1),
                  pl.BlockSpec((1, c), lambda b, j: (0, 0))],
        out_specs=[fwd(c, 0), bwd(c, 0)],
        out_shape=[jax.ShapeDtypeStruct((bsz, seq, c), F32)] * 2,
        scratch_shapes=[pltpu.VMEM((2, c // MXU_DIM, MXU_DIM, MXU_DIM), F32)],
        compiler_params=_params("arbitrary", "arbitrary"),
        name="wkv7_chunked",
    )(rkvk, lwa, rkvk, lwa, rw_k_a.reshape(1, c))


NEG_INF = float("-inf")


def _first_max(vals, idx, size):
    m = jnp.max(vals, axis=0, keepdims=True)
    i = jnp.min(jnp.where(vals == m, idx, size), axis=0, keepdims=True)
    return m, i


def _route(scores, biased):
    e, tt = scores.shape
    per = e // N_GROUPS
    rowl = lax.broadcasted_iota(jnp.int32, (per, tt), 0)
    gs = []
    for g in range(N_GROUPS):
        blk = biased[g * per:(g + 1) * per]
        m1, i1 = _first_max(blk, rowl, per)
        m2 = jnp.max(jnp.where(rowl == i1, NEG_INF, blk), axis=0, keepdims=True)
        gs.append(m1 + m2)
    cur = jnp.concatenate(gs, axis=0)
    growl = lax.broadcasted_iota(jnp.int32, (N_GROUPS, tt), 0)
    gsel = jnp.zeros((N_GROUPS, tt), F32)
    for _ in range(TOPK_GROUPS):
        _, ig = _first_max(cur, growl, N_GROUPS)
        hit = growl == ig
        gsel = jnp.where(hit, 1.0, gsel)
        cur = jnp.where(hit, NEG_INF, cur)
    emask = jnp.concatenate([jnp.broadcast_to(gsel[g:g + 1], (per, tt)) for g in range(N_GROUPS)], axis=0)
    masked = jnp.where(emask > 0.5, biased, NEG_INF)
    row = lax.broadcasted_iota(jnp.int32, (e, tt), 0)
    ids, ws = [], []
    for _ in range(TOP_K):
        _, ie = _first_max(masked, row, e)
        hit = row == ie
        ids.append(ie)
        ws.append(jnp.sum(jnp.where(hit, scores, 0.0), axis=0, keepdims=True))
        masked = jnp.where(hit, NEG_INF, masked)
    w = jnp.concatenate(ws, axis=0)
    w = w / jnp.sum(w, axis=0, keepdims=True) * ROUTE_SCALE
    return jnp.concatenate(ids, axis=0), w


def _mixout_kernel(x_ref, mod_ref, yhy_ref, of_ref, ob_ref, g_ref, bonus_ref, lnw_ref, lnb_ref,
                   ones_ref, wout_ref, g2n_ref, rwth_ref, rwtl_ref, bias_ref,
                   x1_ref, h2a_ref, h2b_ref, eid_ref, wsel_ref):
    ones = ones_ref[...]
    s = of_ref[...] + ob_ref[...]
    mean = _dot_exact_rhs(s, ones) * (1.0 / HEAD)
    dlt = s - mean
    var = _dot_exact_rhs(dlt * dlt, ones) * (1.0 / HEAD)
    sn = dlt * lax.rsqrt(var + GN_EPS) * lnw_ref[...] + lnb_ref[...]
    yrw = (sn + bonus_ref[...]) * g_ref[...]
    ch = yhy_ref.shape[-1]
    mix = _dot(yhy_ref[...].astype(BF16), wout_ref[:ch, :]) + _dot(yrw.astype(BF16), wout_ref[ch:, :])
    x1 = x_ref[...] + mod_ref[2:3, :] * mix
    x1_ref[...] = x1
    ms = jnp.mean(x1 * x1, axis=-1, keepdims=True)
    h2 = x1 * lax.rsqrt(ms + NORM_EPS) * g2n_ref[...]
    h2 = h2 * (1.0 + mod_ref[4:5, :]) + mod_ref[3:4, :]
    h2a_ref[...], h2b_ref[...] = _pack_rows(h2)
    rh, rl = rwth_ref[...], rwtl_ref[...]
    hh, hl = _split2(h2)
    logits = _nt(rh, hh) + (_nt(rh, hl) + _nt(rl, hh))
    scores = jax.nn.sigmoid(logits)
    ids, w = _route(scores, scores + bias_ref[...])
    eid_ref[...] = ids
    wsel_ref[...] = w


def _mix_out(x, mod, yhy, o_f, o_b, g, bonus, ln_w, ln_b, w_out, norm2_g, router_w, router_bias, tt=1024):
    bsz, seq, d = x.shape
    tt = min(tt, seq)
    nt = seq // tt
    n = bsz * seq
    c = D_RWKV
    e = router_w.shape[1]
    row = lambda a: a.reshape(1, -1)
    consts = [row(ln_w), row(ln_b), _head_ones(), w_out.astype(BF16), row(norm2_g), *_split2(router_w.T),
              jnp.broadcast_to(router_bias.reshape(e, 1), (e, tt))]
    const = lambda a: pl.BlockSpec(a.shape, lambda b, i: (0,) * a.ndim, pipeline_mode=pl.Buffered(1))
    tile = lambda w: pl.BlockSpec((None, tt, w), lambda b, i: (b, i, 0))
    flat = lambda rows, dt: jax.ShapeDtypeStruct((rows, n), dt)
    return pl.pallas_call(
        _mixout_kernel,
        grid=(bsz, nt),
        in_specs=[tile(d), pl.BlockSpec((None,) + mod.shape[1:], lambda b, i: (b, 0, 0))]
        + [tile(c)] * 5 + [const(a) for a in consts],
        out_specs=[tile(d), pl.BlockSpec((tt, d // 4), lambda b, i: (b * nt + i, 0)),
                   pl.BlockSpec((tt, d // 4), lambda b, i: (b * nt + i, 0)),
                   pl.BlockSpec((TOP_K, tt), lambda b, i: (0, b * nt + i)),
                   pl.BlockSpec((TOP_K, tt), lambda b, i: (0, b * nt + i))],
        out_shape=[jax.ShapeDtypeStruct((bsz, seq, d), F32), jax.ShapeDtypeStruct((n, d // 4), U32),
                   jax.ShapeDtypeStruct((n, d // 4), U32),
                   flat(TOP_K, jnp.int32), flat(TOP_K, F32)],
        compiler_params=_params("arbitrary", "arbitrary"),
        name="mix_out_router",
    )(x, mod, yhy, o_f, o_b, g, bonus, *consts)


BLK = 512
BLK_SHIFT = 9


def _multi_hot(eid, e):
    row = lax.broadcasted_iota(jnp.int32, (e, eid.shape[1]), 0)
    mh = jnp.zeros((e, eid.shape[1]), F32)
    for kk in range(TOP_K):
        mh = mh + jnp.where(row == eid[kk:kk + 1, :], 1.0, 0.0)
    return row, mh


def _lookup(row, eid, table):
    return jnp.concatenate(
        [jnp.sum(jnp.where(row == eid[kk:kk + 1, :], table, 0.0), axis=0, keepdims=True)
         for kk in range(TOP_K)], axis=0)


def _rank_kernel(eid_ref, rank_ref, cnt_ref, *, e):
    @pl.when(pl.program_id(0) == 0)
    def _():
        cnt_ref[...] = jnp.zeros_like(cnt_ref)

    eid = eid_ref[...]
    tt = eid.shape[1]
    row, mh = _multi_hot(eid, e)
    mhb = mh.astype(BF16)
    si = lax.broadcasted_iota(jnp.int32, (tt, tt), 0)
    ti = lax.broadcasted_iota(jnp.int32, (tt, tt), 1)
    earlier = _dot(mhb, jnp.where(si < ti, 1.0, 0.0).astype(BF16))
    cnt = cnt_ref[...]
    full = earlier + jnp.concatenate([cnt] * (tt // LANES), axis=1)
    rank_ref[...] = _lookup(row, eid, full).astype(jnp.int32)
    cnt_ref[...] = cnt + _dot(mhb, jnp.ones((tt, LANES), BF16))


def _expert_ranks(eid, e, tt=512):
    n = eid.shape[1]
    tt = min(tt, n)
    return pl.pallas_call(
        functools.partial(_rank_kernel, e=e),
        grid=(n // tt,),
        in_specs=[pl.BlockSpec((TOP_K, tt), lambda i: (0, i))],
        out_specs=[pl.BlockSpec((TOP_K, tt), lambda i: (0, i)),
                   pl.BlockSpec((e, LANES), lambda i: (0, 0))],
        out_shape=[jax.ShapeDtypeStruct((TOP_K, n), jnp.int32), jax.ShapeDtypeStruct((e, LANES), F32)],
        compiler_params=_params("arbitrary"),
        name="expert_ranks",
    )(eid)


def _block_offsets(cnt):
    e = cnt.shape[0]
    nblk = ((cnt.astype(jnp.int32) + (BLK - 1)) >> BLK_SHIFT).astype(F32)
    ri = lax.broadcasted_iota(jnp.int32, (e, e), 0)
    ci = lax.broadcasted_iota(jnp.int32, (e, e), 1)
    tril = jnp.where(ci <= ri, 1.0, 0.0).astype(BF16)
    nh, nl = _split2(nblk)
    return nblk, _dot(tril, nh) + _dot(tril, nl)


def _dest_kernel(cnt_ref, eid_ref, rank_ref, dest_ref):
    nblk, end = _block_offsets(cnt_ref[...])
    off = (end - nblk) * float(BLK)
    eid = eid_ref[...]
    tt = eid.shape[1]
    row = lax.broadcasted_iota(jnp.int32, (off.shape[0], tt), 0)
    table = jnp.concatenate([off] * (tt // LANES), axis=1)
    dest_ref[...] = _lookup(row, eid, table).astype(jnp.int32) + rank_ref[...]


def _destinations(cnt, eid, rank, tt=512):
    n = eid.shape[1]
    tt = min(tt, n)
    blk = pl.BlockSpec((TOP_K, tt), lambda i: (0, i))
    return pl.pallas_call(
        _dest_kernel,
        grid=(n // tt,),
        in_specs=[pl.BlockSpec(cnt.shape, lambda i: (0, 0)), blk, blk],
        out_specs=blk,
        out_shape=jax.ShapeDtypeStruct((TOP_K, n), jnp.int32),
        compiler_params=_params("arbitrary"),
        name="expert_destinations",
    )(cnt, eid, rank)


def _meta_kernel(cnt_ref, meta_ref, *, nbp):
    cnt = cnt_ref[...]
    e = cnt.shape[0]
    nblk, end = _block_offsets(cnt)
    rep = lambda a, w: jnp.concatenate([a] * (w // LANES), axis=1)
    b = lax.broadcasted_iota(jnp.int32, (e, nbp), 1).astype(F32)
    blk_e = jnp.minimum(jnp.sum(jnp.where(rep(end, nbp) <= b, 1.0, 0.0), axis=0, keepdims=True), float(e - 1))
    row = lax.broadcasted_iota(jnp.int32, (e, nbp), 0).astype(F32)
    mine = row == blk_e
    left = rep(cnt + (end - nblk) * float(BLK), nbp) - b * float(BLK)
    nvalid = jnp.clip(jnp.sum(jnp.where(mine, left, 0.0), axis=0, keepdims=True), 0.0, float(BLK))
    nused = jnp.max(rep(end, nbp), axis=0, keepdims=True)
    later = jnp.logical_and(row > blk_e, rep(nblk, nbp) > 0.0)
    nxt = jnp.min(jnp.where(later, row, float(e)), axis=0, keepdims=True)
    nxt = jnp.where(nxt >= float(e), -1.0, nxt)
    meta_ref[...] = jnp.concatenate([blk_e, nvalid, nused, nxt, jnp.zeros((4, nbp), F32)],
                                    axis=0).astype(jnp.int32)


def _block_meta(cnt, nb):
    nbp = -(-nb // LANES) * LANES
    return pl.pallas_call(
        functools.partial(_meta_kernel, nbp=nbp),
        out_shape=jax.ShapeDtypeStruct((8, nbp), jnp.int32),
        compiler_params=pltpu.CompilerParams(vmem_limit_bytes=VMEM_LIMIT),
        name="expert_block_meta",
    )(cnt)


SC_WINDOW = 128


def _sc_mesh():
    return plsc.VectorSubcoreMesh(core_axis_name="core", subcore_axis_name="subcore")


def _sc_scatter_rows(rows, idx, nrows):
    n, width = rows.shape

    @pl.kernel(out_type=jax.ShapeDtypeStruct((nrows, width), rows.dtype), mesh=_sc_mesh())
    def scatter(rows_hbm, idx_hbm, out_hbm):
        def body(rows_vmem, idx_vmem):
            pltpu.sync_copy(rows_vmem, out_hbm.at[idx_vmem.at[0]])

        pltpu.emit_pipeline(
            body,
            grid=(n // SC_WINDOW, idx.shape[0]),
            in_specs=[pl.BlockSpec((SC_WINDOW, width), index_map=lambda i, k: (i, 0)),
                      pl.BlockSpec((1, SC_WINDOW), index_map=lambda i, k: (k, i))],
            out_specs=[],
            core_axis_name=("core", "subcore"),
            dimension_semantics=(pltpu.PARALLEL, pltpu.ARBITRARY),
        )(rows_hbm, idx_hbm)

    return scatter(rows, idx)


def _sc_gather_rows(src, idx):
    num = idx.shape[1]
    width = src.shape[1]

    @pl.kernel(out_type=jax.ShapeDtypeStruct((num, width), src.dtype), mesh=_sc_mesh())
    def gather(src_hbm, idx_hbm, out_hbm):
        def body(idx_vmem, out_vmem):
            pltpu.sync_copy(src_hbm.at[idx_vmem.at[0]], out_vmem)

        pltpu.emit_pipeline(
            body,
            grid=(num // SC_WINDOW,),
            in_specs=[pl.BlockSpec((1, SC_WINDOW), index_map=lambda i: (0, i))],
            out_specs=[pl.BlockSpec((SC_WINDOW, width), index_map=lambda i: (i, 0))],
            core_axis_name=("core", "subcore"),
            dimension_semantics=(pltpu.PARALLEL,),
        )(idx_hbm, out_hbm)

    return gather(src, idx)


BLOCKS_PER_STEP = 2


def _experts_kernel(be_ref, nv_ref, nxt_ref, nu_ref, xa_ref, xb_ref, wg_hbm, wu_hbm, wd_hbm, oa_ref, ob_ref,
                    wgf, wuf, wdf, wgb, wub, wdb, sems, slot_ref):
    step = pl.program_id(0)

    def fetch(expert, slot):
        return [pltpu.make_async_copy(src.at[expert], dst.at[slot], sems.at[slot])
                for src, dst in ((wg_hbm, wgf), (wu_hbm, wuf), (wd_hbm, wdf))]

    @pl.when(step == 0)
    def _():
        slot_ref[0] = 0
        for cp in fetch(be_ref[0], 0):
            cp.start()

    def one_block(b, rows):
        prev = be_ref[jnp.maximum(b - 1, 0)]

        @pl.when(jnp.logical_or(b == 0, be_ref[b] != prev))
        def _():
            slot = slot_ref[0]
            for cp in fetch(be_ref[b], slot):
                cp.wait()

            @pl.when(nxt_ref[b] >= 0)
            def _():
                for cp in fetch(nxt_ref[b], 1 - slot):
                    cp.start()

            wgb[...] = wgf[slot].astype(BF16)
            wub[...] = wuf[slot].astype(BF16)
            wdb[...] = wdf[slot].astype(BF16)
            slot_ref[0] = 1 - slot

        valid = lax.broadcasted_iota(jnp.int32, (BLK, 1), 0) < nv_ref[b]
        zero = jnp.zeros((), U32)
        x = _unpack_rows(jnp.where(valid, xa_ref[rows, :], zero), jnp.where(valid, xb_ref[rows, :], zero))
        act = _silu(_dot(x, wgb[...])) * _dot(x, wub[...])
        oa_ref[rows, :], ob_ref[rows, :] = _pack_rows(_dot(act.astype(BF16), wdb[...]))

    for j in range(BLOCKS_PER_STEP):
        b = step * BLOCKS_PER_STEP + j
        rows = slice(j * BLK, (j + 1) * BLK)
        if j == 0:
            one_block(b, rows)
        else:
            pl.when(b < nu_ref[0])(functools.partial(one_block, b, rows))


def _experts(blk_e, nvalid, nused, nxt_e, xs_a, xs_b, wg, wu, wd):
    p, dq = xs_a.shape
    d, de = wg.shape[1], wg.shape[2]
    rows_in = pl.BlockSpec((BLOCKS_PER_STEP * BLK, dq), lambda b, be, nv, nx, nu: (b, 0))
    hbm = pl.BlockSpec(memory_space=pl.ANY)
    return pl.pallas_call(
        _experts_kernel,
        grid_spec=pltpu.PrefetchScalarGridSpec(
            num_scalar_prefetch=4,
            grid=((nused + BLOCKS_PER_STEP - 1) // BLOCKS_PER_STEP,),
            in_specs=[rows_in, rows_in, hbm, hbm, hbm],
            out_specs=[rows_in, rows_in],
            scratch_shapes=[pltpu.VMEM((2, d, de), F32), pltpu.VMEM((2, d, de), F32), pltpu.VMEM((2, de, d), F32),
                            pltpu.VMEM((d, de), BF16), pltpu.VMEM((d, de), BF16), pltpu.VMEM((de, d), BF16),
                            pltpu.SemaphoreType.DMA((2,)), pltpu.SMEM((1,), jnp.int32)],
        ),
        out_shape=[jax.ShapeDtypeStruct((p, dq), U32)] * 2,
        compiler_params=_params("arbitrary"),
        name="moe_experts",
    )(blk_e, nvalid, nxt_e, nused.reshape(1), xs_a, xs_b, wg, wu, wd)


def _shared_kernel(ha_ref, hb_ref, sg_ref, su_ref, sd_ref, o_ref):
    hb = _unpack_rows(ha_ref[...], hb_ref[...])
    act = _silu(_dot(hb, sg_ref[...])) * _dot(hb, su_ref[...])
    o_ref[...] = _dot(act.astype(BF16), sd_ref[...]).astype(o_ref.dtype)


def _shared_expert(h2a, h2b, sh_wg, sh_wu, sh_wd, tt=512):
    n, dp = h2a.shape
    d = sh_wg.shape[0]
    tt = min(tt, n)
    consts = [sh_wg.astype(BF16), sh_wu.astype(BF16), sh_wd.astype(BF16)]
    packed_rows = pl.BlockSpec((tt, dp), lambda i: (i, 0))
    return pl.pallas_call(
        _shared_kernel,
        grid=(n // tt,),
        in_specs=[packed_rows, packed_rows] + [pl.BlockSpec(a.shape, lambda i: (0, 0)) for a in consts],
        out_specs=pl.BlockSpec((tt, d), lambda i: (i, 0)),
        out_shape=jax.ShapeDtypeStruct((n, d), BF16),
        compiler_params=_params("arbitrary"),
        name="shared_expert",
    )(h2a, h2b, *consts)


def _combine_kernel(w_ref, x1_ref, sh_ref, mod_ref, ga_ref, gb_ref, gf_ref, sel_ref, o_ref):
    ffn = sh_ref[...].astype(F32)
    wh, wl = _split2(w_ref[...])
    acc = None
    for kk in range(TOP_K):
        sel = sel_ref[kk]
        wk = _tn(wh, sel) + _tn(wl, sel)
        a_lo, a_hi = _unpack_halves(ga_ref[kk])
        b_lo, b_hi = _unpack_halves(gb_ref[kk])
        parts = [a_lo * wk, b_lo * wk, a_hi * wk, b_hi * wk]
        acc = parts if acc is None else [p + q for p, q in zip(acc, parts)]
    ffn = ffn + jnp.concatenate(acc, axis=1)
    xo = x1_ref[...] + mod_ref[5:6, :] * ffn
    ms = jnp.mean(xo * xo, axis=-1, keepdims=True)
    o_ref[...] = xo * lax.rsqrt(ms + NORM_EPS) * gf_ref[...]


def _combine(wsel, x1, shared, mod, ga, gb, normf_g, tok0, seq, tt=256):
    n, d = x1.shape
    part = ga.shape[1]
    tt = min(tt, seq, part)
    per = seq // tt
    off = tok0 // tt
    dq = ga.shape[2]
    sel = jnp.asarray(np.broadcast_to(np.eye(TOP_K)[:, :, None], (TOP_K, TOP_K, dq)), BF16)
    consts = [normf_g.reshape(1, d), sel]
    const = lambda a: pl.BlockSpec(a.shape, lambda i: (0,) * a.ndim)
    rows = pl.BlockSpec((tt, d), lambda i: (off + i, 0))
    gathered = pl.BlockSpec((TOP_K, tt, dq), lambda i: (0, i, 0))
    return pl.pallas_call(
        _combine_kernel,
        grid=(part // tt,),
        in_specs=[pl.BlockSpec((TOP_K, tt), lambda i: (0, off + i)),
                  rows, rows,
                  pl.BlockSpec((None,) + mod.shape[1:], lambda i: ((off + i) // per, 0, 0)),
                  gathered, gathered] + [const(a) for a in consts],
        out_specs=rows,
        out_shape=jax.ShapeDtypeStruct((n, d), F32),
        input_output_aliases={1: 0},
        compiler_params=_params("arbitrary"),
        name="moe_combine",
    )(wsel, x1, shared, mod, ga, gb, *consts)


COMBINE_PARTS = 4


def _moe(x1, h2a, h2b, mod, eid, wsel, exp_wg, exp_wu, exp_wd, sh_wg, sh_wu, sh_wd, normf_g):
    n = h2a.shape[0]
    e = exp_wg.shape[0]
    nb = (n * TOP_K + e * (BLK - 1)) // BLK
    nb = -(-nb // BLOCKS_PER_STEP) * BLOCKS_PER_STEP
    rank, cnt = _expert_ranks(eid, e)
    dest = _destinations(cnt, eid, rank)
    meta = _block_meta(cnt, nb)
    xs_a = _sc_scatter_rows(h2a, dest, nb * BLK)
    xs_b = _sc_scatter_rows(h2b, dest, nb * BLK)
    shared = _shared_expert(h2a, h2b, sh_wg, sh_wu, sh_wd)
    ys_a, ys_b = _experts(meta[0, :nb], meta[1, :nb], meta[2, 0], meta[3, :nb], xs_a, xs_b,
                          exp_wg, exp_wu, exp_wd)
    bsz, seq, d = x1.shape
    out = x1.reshape(n, d)
    part = n // COMBINE_PARTS
    for j in range(COMBINE_PARTS):
        idx = dest[:, j * part:(j + 1) * part].reshape(1, TOP_K * part)
        ga = _sc_gather_rows(ys_a, idx).reshape(TOP_K, part, -1)
        gb = _sc_gather_rows(ys_b, idx).reshape(TOP_K, part, -1)
        out = _combine(wsel, out, shared, mod, ga, gb, normf_g, j * part, seq)
    return out


def kernel(x, c, norm1_g, norm2_g, normf_g, w_ada, b_ada, w_in, w_out, hy_conv_w, hy_conv_b, hy_pos_w1, hy_pos_b1, hy_pos_w2, hy_pos_b2, hy_pos_w3, hy_sin_freq, hy_skip, rw_mu, rw_w0, rw_w_up, rw_a0, rw_a_up, rw_g_up, rw_k_k, rw_k_a, rw_r_k, rw_ln_w, rw_ln_b, router_w, router_bias, exp_w_gate, exp_w_up, exp_w_down, sh_w_gate, sh_w_up, sh_w_down):
    bsz, seq, d = x.shape
    depth = w_ada.shape[0]
    assert depth == 1, "the final norm is fused into the last kernel of a single layer"
    for l in range(depth):
        mod = _modulation(c, w_ada[l], b_ada[l]).reshape(bsz, -1, d)
        uhy, rkvk, lwa, g, bonus = _projection(
            x, mod, norm1_g[l], w_in[l], hy_conv_w[l], hy_conv_b[l], rw_mu[l], rw_w0[l], rw_w_up[l],
            rw_a0[l], rw_a_up[l], rw_g_up[l], rw_k_k[l], rw_k_a[l], rw_r_k[l])
        k2, ss = _hyena_filters(seq, hy_pos_w1[l], hy_pos_b1[l], hy_pos_w2[l], hy_pos_b2[l],
                                hy_pos_w3[l], hy_sin_freq[l])
        khat = _filter_spectrum(k2, ss, seq)
        z, z_col = uhy, 0
        for order in range(HYENA_ORDER):
            z = _long_conv_gate(z, z_col, uhy, (order + 1) * D_HYENA, khat, hy_skip[l], order)
            z_col = 0
        o_f, o_b = _wkv(rkvk, lwa, rw_k_a[l])
        x1, h2a, h2b, eid, wsel = _mix_out(x, mod, z, o_f, o_b, g, bonus, rw_ln_w[l], rw_ln_b[l], w_out[l],
                                           norm2_g[l], router_w[l], router_bias[l])
        x = _moe(x1, h2a, h2b, mod, eid, wsel, exp_w_gate[l], exp_w_up[l], exp_w_down[l],
                 sh_w_gate[l], sh_w_up[l], sh_w_down[l], normf_g)
        x = x.reshape(bsz, seq, d)
    return x
```

```python
import functools
import math

import jax
import jax.numpy as jnp
import numpy as np
from jax import lax
from jax.experimental import pallas as pl
from jax.experimental.pallas import tpu as pltpu
from jax.experimental.pallas import tpu_sc as plsc

F32 = jnp.float32
BF16 = jnp.bfloat16

LANES = 128
MXU_DIM = 256
VMEM_LIMIT = 56 * 1024 * 1024

D_HYENA = 512
D_RWKV = 512
HEAD = 64
N_HEADS = D_RWKV // HEAD
HYENA_ORDER = 2
FILTER_BANDS = 16
DECAY_TARGET = 1e-2
FAST_DECAY_PCT = 0.3
SLOW_DECAY_PCT = 1.5
FILTER_NORM_EPS = 1e-6
DECAY_LORA = 32
ICLR_LORA = 32
GATE_LORA = 96
GN_EPS = 64e-5
NORM_EPS = 1e-6
N_EXPERTS = 256
TOP_K = 8
N_GROUPS = 8
TOPK_GROUPS = 4
ROUTE_SCALE = 2.5
D_EXPERT = 256


def _params(*sem):
    return pltpu.CompilerParams(dimension_semantics=sem, vmem_limit_bytes=VMEM_LIMIT)


def _split2(a):
    hi = a.astype(BF16)
    lo = (a - hi.astype(F32)).astype(BF16)
    return hi, lo


def _dot(a, b):
    return jnp.dot(a, b, preferred_element_type=F32)


def _dot3(a, b):
    ah, al = _split2(a)
    bh, bl = _split2(b)
    return _dot(ah, bh) + (_dot(ah, bl) + _dot(al, bh))


def _dot_exact_rhs(a, b_bf16):
    ah, al = _split2(a)
    return _dot(ah, b_bf16) + _dot(al, b_bf16)


def _silu(x):
    return x * jax.nn.sigmoid(x)


U32 = jnp.int32


def _pack_halves(x):
    w = x.shape[1] // 2
    return pltpu.pack_elementwise([x[:, :w], x[:, w:]], packed_dtype=BF16)


def _unpack_halves(p):
    lo = pltpu.unpack_elementwise(p, index=0, packed_dtype=BF16, unpacked_dtype=F32)
    hi = pltpu.unpack_elementwise(p, index=1, packed_dtype=BF16, unpacked_dtype=F32)
    return lo, hi


def _pack_rows(x):
    packed = _pack_halves(x)
    half = packed.shape[1] // 2
    return packed[:, :half], packed[:, half:]


def _unpack_rows(a, b):
    a_lo, a_hi = _unpack_halves(a)
    b_lo, b_hi = _unpack_halves(b)
    return jnp.concatenate([a_lo.astype(BF16), b_lo.astype(BF16), a_hi.astype(BF16), b_hi.astype(BF16)], axis=1)


def _mod_kernel(c_ref, w_ref, b_ref, o_ref):
    o_ref[...] = _dot3(_silu(c_ref[...]), w_ref[...]) + b_ref[...]


def _modulation(c, w_ada, b_ada):
    bsz, d = c.shape
    n = w_ada.shape[1]
    blk = 1024
    return pl.pallas_call(
        _mod_kernel,
        grid=(n // blk,),
        in_specs=[
            pl.BlockSpec((bsz, d), lambda j: (0, 0)),
            pl.BlockSpec((d, blk), lambda j: (0, j)),
            pl.BlockSpec((1, blk), lambda j: (0, j)),
        ],
        out_specs=pl.BlockSpec((bsz, blk), lambda j: (0, j)),
        out_shape=jax.ShapeDtypeStruct((bsz, n), F32),
        compiler_params=_params("arbitrary"),
        name="adaln_mod",
    )(c, w_ada, b_ada.reshape(1, n))


def _filter_kernel(band_ref, w1_ref, b1_ref, w2_ref, b2_ref, w3_ref, freq_ref, delta_ref,
                   k_ref, ss_ref, *, seq, rows):
    half = pl.program_id(0)
    i = pl.program_id(1)
    r = lax.broadcasted_iota(jnp.int32, (rows, LANES), 0) + i * rows
    pos = jnp.where(half == 0, r, seq - r).astype(F32)
    tt = pos / float(max(seq - 1, 1))
    lane = lax.broadcasted_iota(jnp.int32, (rows, LANES), 1)
    feats = jnp.where(lane == 0, tt, jnp.sin(pos * band_ref[0:1, :] + band_ref[1:2, :]))
    freq = freq_ref[...]
    hdn = jnp.sin(freq * (_dot3(feats, w1_ref[...]) + b1_ref[...]))
    for j in range(w2_ref.shape[0]):
        hdn = jnp.sin(freq * (_dot3(hdn, w2_ref[j]) + b2_ref[j]))
    filt = _dot3(hdn, w3_ref[...])
    filt = filt * jnp.exp(-tt[:, :1] * delta_ref[...])
    valid = jnp.logical_or(half == 0, r[:, :1] > 0)
    filt = jnp.where(valid, filt, 0.0)
    k_ref[...] = filt

    @pl.when(jnp.logical_and(half == 0, i == 0))
    def _():
        ss_ref[...] = jnp.zeros_like(ss_ref)

    ss_ref[...] += jnp.broadcast_to(jnp.sum(filt * filt, axis=0, keepdims=True), ss_ref.shape)


def _hyena_filters(seq, pw1, pb1, pw2, pb2, pw3, freq):
    width = pw1.shape[1]
    ncol = HYENA_ORDER * D_HYENA
    rows = min(seq, 512)
    bands = np.zeros((2, LANES), np.float64)
    lin = np.linspace(1e-4, FILTER_BANDS - 1, FILTER_BANDS) * (2.0 * math.pi / seq)
    bands[0, 1:1 + FILTER_BANDS] = lin
    bands[1, 1:1 + FILTER_BANDS] = 0.5 * math.pi
    bands[0, 1 + FILTER_BANDS:1 + 2 * FILTER_BANDS] = -lin
    bands = jnp.asarray(bands, F32)
    deltas = np.abs(np.linspace(math.log(DECAY_TARGET) / SLOW_DECAY_PCT,
                                math.log(DECAY_TARGET) / FAST_DECAY_PCT, D_HYENA))
    deltas = jnp.asarray(np.tile(deltas, HYENA_ORDER)[None], F32)
    w1 = jnp.zeros((LANES, width), F32).at[:pw1.shape[0]].set(pw1)
    w3 = pw3.reshape(width, HYENA_ORDER, 2, D_HYENA).transpose(2, 0, 1, 3).reshape(2, width, ncol)
    nt = seq // rows
    full = lambda *shape: pl.BlockSpec(shape, lambda h, i: (0,) * len(shape))
    return pl.pallas_call(
        functools.partial(_filter_kernel, seq=seq, rows=rows),
        grid=(2, nt),
        in_specs=[
            full(2, LANES), full(LANES, width), full(1, width),
            full(pw2.shape[0], width, width), full(pw2.shape[0], 1, width),
            pl.BlockSpec((None, width, ncol), lambda h, i: (h, 0, 0)),
            full(1, width), full(1, ncol),
        ],
        out_specs=[
            pl.BlockSpec((rows, ncol), lambda h, i: (h * nt + i, 0)),
            pl.BlockSpec((8, ncol), lambda h, i: (0, 0)),
        ],
        out_shape=[jax.ShapeDtypeStruct((2 * seq, ncol), F32),
                   jax.ShapeDtypeStruct((8, ncol), F32)],
        compiler_params=_params("arbitrary", "arbitrary"),
        name="hyena_filters",
    )(bands, w1, pb1.reshape(1, width), pw2, pb2.reshape(pw2.shape[0], 1, width), w3,
      freq.reshape(1, width), deltas)


N1 = LANES
UNROLL = 8


def _dft_tables(seq):
    tables = _dft_tables_np(seq)
    return tuple(jnp.asarray(t, BF16) for t in tables[:5]) + tables[5:]


def _dft_tables_np(seq):
    m = 2 * seq
    n2 = m // N1
    n2h = n2 // 2
    n1 = np.arange(N1)[:, None, None]
    f2 = np.arange(n2)[None, :, None]
    k2 = np.arange(n2)[None, None, :]
    th = 2.0 * np.pi * (n1 * f2 / m + (k2 * f2 % n2) / n2)
    fwd_a = np.concatenate([np.cos(th), -np.sin(th)], axis=1)
    tht = np.transpose(th, (0, 2, 1))
    inv_a = np.concatenate([np.cos(tht), -np.sin(tht)], axis=2)[:, :n2h] / m
    a = np.arange(N1)
    ph = 2.0 * np.pi * np.outer(a, a) / N1
    c, s = np.cos(ph), np.sin(ph)
    fwd_b = np.block([[c, s], [-s, c]])
    inv_b = np.block([[c, -s], [s, c]])
    return fwd_a, fwd_a[:, :, :n2h], inv_a, fwd_b, inv_b, n2, n2h


def _stage_a_fwd(x_ref, wa_ref, y_ref, n2, scale=None):
    def body(i, carry):
        trips = [i * UNROLL + j for j in range(UNROLL)]
        xs = [x_ref[pl.ds(n1, wa_ref.shape[2], stride=N1), :] for n1 in trips]
        if scale is not None:
            xs = [x * scale for x in xs]
        prods = [_dot(wa_ref[n1], x.astype(BF16)) for n1, x in zip(trips, xs)]
        for n1, a in zip(trips, prods):
            y_ref[pl.ds(n1, n2, stride=2 * N1), :] = a[:n2]
            y_ref[pl.ds(N1 + n1, n2, stride=2 * N1), :] = a[n2:]
        return carry
    lax.fori_loop(0, N1 // UNROLL, body, 0)


def _filter_fft_kernel(k_ref, ss_ref, wa_ref, fb_ref, o_ref, y_ref, *, n2):
    scale = lax.rsqrt(ss_ref[0:1, :] + FILTER_NORM_EPS)
    _stage_a_fwd(k_ref, wa_ref, y_ref, n2, scale=scale)

    unr = min(UNROLL, n2)

    def body(i, carry):
        trips = [i * unr + j for j in range(unr)]
        ys = [y_ref[pl.ds(pl.multiple_of(f2 * 2 * N1, 2 * N1), 2 * N1), :].astype(BF16) for f2 in trips]
        for j in range(0, unr, 2):
            z = _dot(fb_ref[...], jnp.concatenate(ys[j:j + 2], axis=1))
            o_ref[trips[j]] = z[:, :LANES]
            o_ref[trips[j + 1]] = z[:, LANES:]
        return carry
    lax.fori_loop(0, n2 // unr, body, 0)


def _filter_spectrum(k2, ss, seq):
    fwd_a, _, _, fwd_b, _, n2, _ = _dft_tables(seq)
    ncol = k2.shape[1]
    nblk = ncol // LANES
    return pl.pallas_call(
        functools.partial(_filter_fft_kernel, n2=n2),
        grid=(nblk,),
        in_specs=[
            pl.BlockSpec((2 * seq, LANES), lambda c: (0, c)),
            pl.BlockSpec((8, LANES), lambda c: (0, c)),
            pl.BlockSpec(fwd_a.shape, lambda c: (0, 0, 0)),
            pl.BlockSpec(fwd_b.shape, lambda c: (0, 0)),
        ],
        out_specs=pl.BlockSpec((None, n2, 2 * N1, LANES), lambda c: (c, 0, 0, 0)),
        out_shape=jax.ShapeDtypeStruct((nblk, n2, 2 * N1, LANES), F32),
        scratch_shapes=[pltpu.VMEM((n2 * 2 * N1, LANES), F32)],
        compiler_params=_params("arbitrary"),
        name="hyena_filter_fft",
    )(k2, ss, fwd_a, fwd_b)


TILE = 8
N1_GROUPS = N1 // TILE


def _tile_tables(seq):
    _, fwd_a, inv_a, _, _, n2, n2h = _dft_tables_np(seq)
    eye = np.eye(TILE)
    fa = fwd_a.reshape(N1_GROUPS, TILE, 2 * n2, n2h)
    wa = np.einsum("qjrn,jk->qrjnk", fa, eye).reshape(N1_GROUPS, 2 * n2 * TILE, n2h * TILE)
    ia = inv_a.reshape(N1_GROUPS, TILE, n2h, 2 * n2)
    vc = np.einsum("qjnr,jk->qnjrk", ia, eye).reshape(N1_GROUPS, n2h * TILE, 2 * n2 * TILE)
    return jnp.asarray(wa, BF16), jnp.asarray(vc, BF16)


def _conv_kernel(u_ref, g_ref, skip_ref, fb_ref, ib_ref, kh_hbm, wa_hbm, vc_hbm, o_ref,
                 y_ref, kh_ref, wa_ref, vc_ref, sem, *, n2, n2h, kh_first):
    c_id, b_id = pl.program_id(0), pl.program_id(1)

    @pl.when(jnp.logical_and(c_id == 0, b_id == 0))
    def _():
        for src, dst in ((wa_hbm, wa_ref), (vc_hbm, vc_ref)):
            cp = pltpu.make_async_copy(src, dst, sem)
            cp.start()
            cp.wait()

    @pl.when(b_id == 0)
    def _():
        cp = pltpu.make_async_copy(kh_hbm.at[kh_first + c_id], kh_ref, sem)
        cp.start()
        cp.wait()

    def y_tile(rf, base):
        ri, f2 = divmod(rf, n2)
        return pl.ds(f2 * 2 * N1 + ri * N1 + base, TILE)

    def stage_a(q, carry):
        base = pl.multiple_of(q * TILE, TILE)
        x = jnp.concatenate([u_ref[pl.ds(N1 * m + base, TILE), :] for m in range(n2h)], axis=0)
        r = _dot(wa_ref[q], x.astype(BF16))
        for rf in range(2 * n2):
            y_ref[y_tile(rf, base), :] = r[rf * TILE:(rf + 1) * TILE]
        return carry
    lax.fori_loop(0, N1_GROUPS, stage_a, 0, unroll=8)

    unr = min(2 * UNROLL, n2)

    def mid(i, carry):
        trips = [i * unr + j for j in range(unr)]
        offs = [pl.multiple_of(f2 * 2 * N1, 2 * N1) for f2 in trips]
        wide = lambda blocks: [jnp.concatenate(blocks[j:j + 2], axis=1) for j in range(0, len(blocks), 2)]
        ys = wide([y_ref[pl.ds(off, 2 * N1), :].astype(BF16) for off in offs])
        khs = wide([kh_ref[f2] for f2 in trips])
        zs = [_dot(fb_ref[...], y) for y in ys]
        ps = []
        for z, kh in zip(zs, khs):
            zr, zi = z[:N1], z[N1:]
            kr, ki = kh[:N1], kh[N1:]
            ps.append(jnp.concatenate([zr * kr - zi * ki, zr * ki + zi * kr], axis=0).astype(BF16))
        gs = [_dot(ib_ref[...], p) for p in ps]
        for j, g in enumerate(gs):
            y_ref[pl.ds(offs[2 * j], 2 * N1), :] = g[:, :LANES]
            y_ref[pl.ds(offs[2 * j + 1], 2 * N1), :] = g[:, LANES:]
        return carry
    lax.fori_loop(0, n2 // unr, mid, 0)

    skip = skip_ref[...]

    def stage_c(q, carry):
        base = pl.multiple_of(q * TILE, TILE)
        g = jnp.concatenate([y_ref[y_tile(rf, base), :] for rf in range(2 * n2)], axis=0)
        conv = _dot(vc_ref[q], g.astype(BF16))
        for m in range(n2h):
            rows = pl.ds(N1 * m + base, TILE)
            o_ref[rows, :] = g_ref[rows, :] * (conv[m * TILE:(m + 1) * TILE] + u_ref[rows, :] * skip)
        return carry
    lax.fori_loop(0, N1_GROUPS, stage_c, 0, unroll=8)


def _long_conv_gate(u, u_col, gate, gate_col, khat, skip, order):
    bsz, seq, _ = u.shape
    ch = D_HYENA
    _, _, _, fwd_b, inv_b, n2, n2h = _dft_tables(seq)
    wa, vc = _tile_tables(seq)
    nblk = ch // LANES
    const = lambda a: pl.BlockSpec(a.shape, lambda c, b: (0,) * a.ndim)
    at = lambda col: pl.BlockSpec((None, seq, LANES), lambda c, b: (b, 0, col // LANES + c))
    hbm = pl.BlockSpec(memory_space=pl.ANY)
    return pl.pallas_call(
        functools.partial(_conv_kernel, n2=n2, n2h=n2h, kh_first=order * nblk),
        grid=(nblk, bsz),
        in_specs=[
            at(u_col), at(gate_col),
            pl.BlockSpec((1, LANES), lambda c, b: (0, c)),
            const(fwd_b), const(inv_b), hbm, hbm, hbm,
        ],
        out_specs=at(0),
        out_shape=jax.ShapeDtypeStruct((bsz, seq, ch), F32),
        scratch_shapes=[pltpu.VMEM((n2 * 2 * N1, LANES), F32), pltpu.VMEM(khat.shape[1:], F32),
                        pltpu.VMEM(wa.shape, BF16), pltpu.VMEM(vc.shape, BF16), pltpu.SemaphoreType.DMA(())],
        compiler_params=_params("arbitrary", "arbitrary"),
        name=f"hyena_conv{order}",
    )(u, gate, skip[order].reshape(1, ch), fwd_b, inv_b, khat, wa, vc)


HALO = 8


def _shift_rows(p, k):
    return pltpu.roll(p, k % p.shape[0], axis=0)


def _proj_kernel(xp_ref, x_ref, xn_ref, mod_ref, g1_ref, why_ref, wrkv_ref, wlora_ref,
                 cw_ref, cb_ref, murkv_ref, mulora_ref, w0_ref, a0_ref, wwah_ref, wwal_ref, gup_ref,
                 kk_ref, ka_ref, rk_ref, ones_ref,
                 uhy_ref, rkvk_ref, lwa_ref, g_ref, bonus_ref,
                 *, tt, nt):
    i = pl.program_id(1)
    xe = jnp.concatenate([xp_ref[...], x_ref[...], xn_ref[...]], axis=0)
    ms = jnp.mean(xe * xe, axis=-1, keepdims=True)
    h = xe * lax.rsqrt(ms + NORM_EPS) * g1_ref[...]
    h = h * (1.0 + mod_ref[1:2, :]) + mod_ref[0:1, :]
    row = lax.broadcasted_iota(jnp.int32, (tt + 2 * HALO, 1), 0)
    inside = jnp.logical_and(jnp.logical_or(row >= HALO, i > 0),
                             jnp.logical_or(row < tt + HALO, i < nt - 1))
    hb = jnp.where(inside, h, 0.0).astype(BF16)
    mid = slice(HALO, tt + HALO)

    p = _dot(hb, why_ref[...])
    u = (_shift_rows(p, 1) * cw_ref[0:1, :] + p * cw_ref[1:2, :]
         + _shift_rows(p, -1) * cw_ref[2:3, :] + cb_ref[...])
    uhy_ref[...] = u[mid]

    p = _dot(hb, wrkv_ref[...])
    p = p + murkv_ref[...] * (0.5 * (_shift_rows(p, 1) + _shift_rows(p, -1)) - p)
    p = p[mid]
    c = D_RWKV
    r, k, v = p[:, :c], p[:, c:2 * c], p[:, 2 * c:]
    rkvk_ref[:, :3 * c] = p

    q = _dot(hb, wlora_ref[...])
    q = q + mulora_ref[...] * (0.5 * (_shift_rows(q, 1) + _shift_rows(q, -1)) - q)
    q = q[mid]
    wa = q[:, :LANES]
    lane = lax.broadcasted_iota(jnp.int32, wa.shape, 1)
    wa = jnp.where(lane < 2 * DECAY_LORA, jnp.tanh(wa), wa)
    wah, wal = _split2(wa)
    up = _dot(wah, wwah_ref[...]) + (_dot(wah, wwal_ref[...]) + _dot(wal, wwah_ref[...]))
    lw = -math.exp(-0.5) * jax.nn.sigmoid(w0_ref[...] + up[:, :2 * c])
    a = jax.nn.sigmoid(a0_ref[...] + up[:, 2 * c:])
    for dd in range(2):
        lwa_ref[:, 2 * dd * c:(2 * dd + 1) * c] = lw[:, dd * c:(dd + 1) * c]
        lwa_ref[:, (2 * dd + 1) * c:(2 * dd + 2) * c] = a[:, dd * c:(dd + 1) * c]
    g_ref[...] = _dot3(jax.nn.sigmoid(q[:, LANES:]), gup_ref[...])

    ones = ones_ref[...]
    kk = k * kk_ref[...]
    nrm = jnp.sqrt(_dot_exact_rhs(kk * kk, ones))
    rkvk_ref[:, 3 * c:] = kk / jnp.maximum(nrm, 1e-12)
    ka = ka_ref[...]
    ksum = k * (2.0 + (a[:, :c] + a[:, c:] - 2.0) * ka)
    bonus_ref[...] = _dot_exact_rhs(r * ksum * rk_ref[...], ones) * v


def _head_ones():
    hid = np.arange(D_RWKV) // HEAD
    return jnp.asarray(hid[:, None] == hid[None, :], BF16)


def _projection(x, mod, norm1_g, w_in, hy_conv_w, hy_conv_b, rw_mu, rw_w0, rw_w_up, rw_a0,
                rw_a_up, rw_g_up, rw_k_k, rw_k_a, rw_r_k, tt=512):
    bsz, seq, d = x.shape
    tt = min(tt, seq)
    nt = seq // tt
    c = D_RWKV
    hy = (HYENA_ORDER + 1) * D_HYENA
    nlora = 2 * LANES
    w_hy = w_in[:, :hy].astype(BF16)
    w_rkv = w_in[:, hy:hy + 3 * c].astype(BF16)
    w_lora = jnp.zeros((d, nlora), F32).at[:, :w_in.shape[1] - hy - 3 * c].set(w_in[:, hy + 3 * c:]).astype(BF16)
    mu_rkv = rw_mu[:3 * c].reshape(1, 3 * c)
    mu_lora = jnp.zeros((1, nlora), F32).at[0, :rw_mu.shape[0] - 3 * c].set(rw_mu[3 * c:])
    wwa = jnp.zeros((LANES, 4 * c), F32)
    for dd in range(2):
        wwa = wwa.at[dd * DECAY_LORA:(dd + 1) * DECAY_LORA, dd * c:(dd + 1) * c].set(rw_w_up[dd])
        wwa = wwa.at[2 * DECAY_LORA + dd * ICLR_LORA:2 * DECAY_LORA + (dd + 1) * ICLR_LORA,
                     2 * c + dd * c:2 * c + (dd + 1) * c].set(rw_a_up[dd])
    gup = jnp.zeros((LANES, c), F32).at[:GATE_LORA].set(rw_g_up)
    row = lambda a: a.reshape(1, -1)

    nb8 = seq // HALO
    tb = tt // HALO
    const = lambda a: pl.BlockSpec(a.shape, lambda b, i: (0,) * a.ndim, pipeline_mode=pl.Buffered(1))
    tile = lambda w: pl.BlockSpec((None, tt, w), lambda b, i: (b, i, 0))
    ins = [
        (x, pl.BlockSpec((None, HALO, d), lambda b, i: (b, jnp.maximum(i * tb - 1, 0), 0))),
        (x, pl.BlockSpec((None, tt, d), lambda b, i: (b, i, 0))),
        (x, pl.BlockSpec((None, HALO, d), lambda b, i: (b, jnp.minimum((i + 1) * tb, nb8 - 1), 0))),
        (mod, pl.BlockSpec((None,) + mod.shape[1:], lambda b, i: (b, 0, 0))),
    ]
    consts = [row(norm1_g), w_hy, w_rkv, w_lora, hy_conv_w, row(hy_conv_b), mu_rkv, mu_lora,
              row(rw_w0), row(rw_a0), *_split2(wwa), gup, row(rw_k_k), row(rw_k_a), row(rw_r_k), _head_ones()]
    ins += [(a, const(a)) for a in consts]
    widths = [hy, 4 * c, 4 * c, c, c]
    return pl.pallas_call(
        functools.partial(_proj_kernel, tt=tt, nt=nt),
        grid=(bsz, nt),
        in_specs=[s for _, s in ins],
        out_specs=[tile(w) for w in widths],
        out_shape=[jax.ShapeDtypeStruct((bsz, seq, w), F32) for w in widths],
        compiler_params=_params("arbitrary", "arbitrary"),
        name="input_projection",
    )(*[a for a, _ in ins])


CHUNK = HEAD
GROUP = MXU_DIM // HEAD


def _nt(a, b):
    return lax.dot_general(a, b, (((1,), (1,)), ((), ())), preferred_element_type=F32)


def _tn(a, b):
    return lax.dot_general(a, b, (((0,), (0,)), ((), ())), preferred_element_type=F32)


def _wkv_direction(r, k, v, kk, lw, a, ka, s_ref, reverse):
    c = CHUNK
    ti = lax.broadcasted_iota(jnp.int32, (c, c), 0)
    si = lax.broadcasted_iota(jnp.int32, (c, c), 1)
    tri = (si >= ti) if reverse else (si <= ti)
    cum = _dot_exact_rhs_lhs(jnp.where(tri, 1.0, 0.0).astype(BF16), lw)
    tot = jnp.sum(lw, axis=0, keepdims=True)
    w_incl = jnp.exp(cum)
    w_prev = jnp.exp(cum - lw)
    w_inv = jnp.exp(-cum)
    w_end = jnp.exp(tot - cum)
    w_tot = jnp.exp(tot)
    kd = k * (1.0 + (a - 1.0) * ka)
    b = kk * a
    a_w = -kk * w_prev
    r_w = r * w_incl
    b_w = b * w_inv
    k_w = kd * w_inv
    b_e = b * w_end
    k_e = kd * w_end

    m = MXU_DIM
    ri = lax.broadcasted_iota(jnp.int32, (m, m), 0)
    ci = lax.broadcasted_iota(jnp.int32, (m, m), 1)
    head_mask = (ri // HEAD) == (ci // HEAD)
    tl = lax.broadcasted_iota(jnp.int32, (c, m), 0)
    sl = lax.broadcasted_iota(jnp.int32, (c, m), 1) % c
    strict = (sl > tl) if reverse else (sl < tl)
    incl = (sl >= tl) if reverse else (sl <= tl)
    eye = jnp.where(sl == tl, 1.0, 0.0)
    both = lambda top, bot: jnp.concatenate([top, bot], axis=0)

    def stack(xg):
        xb = xg.astype(BF16)
        return jnp.where(head_mask, jnp.concatenate([xb] * GROUP, axis=0), jnp.zeros((), BF16))

    streams = []
    for g in range(D_RWKV // m):
        sl_g = slice(g * m, (g + 1) * m)
        streams.append(dict(
            ar=both(a_w[:, sl_g], r_w[:, sl_g]).astype(BF16),
            b_st=stack(b_w[:, sl_g]), k_st=stack(k_w[:, sl_g]), v_st=stack(v[:, sl_g]),
            v=v[:, sl_g], bk=both(b_e[:, sl_g], k_e[:, sl_g]).astype(BF16),
            w_tot=w_tot[:, sl_g], s_ref=s_ref.at[g],
            strict=strict, incl=incl, eye=eye, head_mask=head_mask, stack=stack))
    return streams


def _wkv_streams_step(streams):
    c = CHUNK
    both = lambda top, bot: jnp.concatenate([top, bot], axis=0)
    for st in streams:
        st["s"] = st["s_ref"][...]
        st["xb"] = _nt(st["ar"], st["b_st"])
        st["xk"] = _nt(st["ar"], st["k_st"])
        st["xs"] = _nt(st["ar"], st["s"].astype(BF16))
    for st in streams:
        m_k = both(jnp.where(st["strict"], st["xk"][:c], 0.0), jnp.where(st["incl"], st["xk"][c:], 0.0))
        st["kv"] = _dot(m_k.astype(BF16), st["v_st"])
        st["rhs"] = st["xs"][:c] + st["kv"][:c]
        st["pw"] = jnp.where(st["strict"], st["xb"][:c], 0.0)
        st["t"] = st["eye"] + st["pw"]
        st["p_st"] = st["stack"](st["pw"])
    levels = int(math.log2(c)) - 1
    for st in streams:
        st["pw"] = _dot(st["pw"].astype(BF16), st["p_st"])
        st["p_st"] = st["stack"](st["pw"])
    for lvl in range(1, levels + 1):
        for st in streams:
            if lvl < levels:
                prod = _dot(both(st["pw"], st["t"]).astype(BF16), st["p_st"])
                st["pw"] = prod[:c]
                st["t"] = st["t"] + prod[c:]
                st["p_st"] = st["stack"](st["pw"])
            else:
                st["t"] = st["t"] + _dot(st["t"].astype(BF16), st["p_st"])
    for st in streams:
        st["u"] = _dot(st["t"].astype(BF16), st["stack"](st["rhs"]))
    outs = []
    for st in streams:
        m_rb = jnp.where(st["incl"], st["xb"][c:], 0.0)
        outs.append(st["xs"][c:] + _dot(m_rb.astype(BF16), st["stack"](st["u"])) + st["kv"][c:])
        uv = both(st["u"], st["v"]).astype(BF16)
        st["s_ref"][...] = st["s"] * st["w_tot"] + jnp.where(st["head_mask"], _tn(uv, st["bk"]), 0.0)
    return outs


def _dot_exact_rhs_lhs(tri_bf16, x):
    xh, xl = _split2(x)
    return _dot(tri_bf16, xh) + _dot(tri_bf16, xl)


def _wkv_kernel(rkvk_f, lwa_f, rkvk_b, lwa_b, ka_ref, of_ref, ob_ref, s_ref, *, nch):
    @pl.when(pl.program_id(1) == 0)
    def _():
        s_ref[...] = jnp.zeros_like(s_ref)

    ka = ka_ref[...]
    c = D_RWKV

    def operands(rkvk_ref, lwa_ref, rows):
        x = rkvk_ref[rows, :]
        la = lwa_ref[rows, :]
        return x[:, :c], x[:, c:2 * c], x[:, 2 * c:3 * c], x[:, 3 * c:], la[:, :c], la[:, c:]

    for ci in range(nch):
        rows_f = slice(ci * CHUNK, (ci + 1) * CHUNK)
        rows_b = slice((nch - 1 - ci) * CHUNK, (nch - ci) * CHUNK)
        fwd = _wkv_direction(*operands(rkvk_f, lwa_f, rows_f), ka, s_ref.at[0], False)
        bwd = _wkv_direction(*operands(rkvk_b, lwa_b, rows_b), ka, s_ref.at[1], True)
        outs = _wkv_streams_step(fwd + bwd)
        of_ref[rows_f, :] = jnp.concatenate(outs[:len(fwd)], axis=1)
        ob_ref[rows_b, :] = jnp.concatenate(outs[len(fwd):], axis=1)


WKV_CHUNKS_PER_STEP = 8


def _wkv(rkvk, lwa, rw_k_a):
    bsz, seq, _ = rkvk.shape
    c = D_RWKV
    nch = WKV_CHUNKS_PER_STEP if seq % (WKV_CHUNKS_PER_STEP * CHUNK) == 0 else 1
    rows = nch * CHUNK
    nb = seq // rows
    fwd = lambda w, lane_blk: pl.BlockSpec((None, rows, w), lambda b, j: (b, j, lane_blk))
    bwd = lambda w, lane_blk: pl.BlockSpec((None, rows, w), lambda b, j: (b, nb - 1 - j, lane_blk))
    return pl.pallas_call(
        functools.partial(_wkv_kernel, nch=nch),
        grid=(bsz, nb),
        in_specs=[fwd(4 * c, 0), fwd(2 * c, 0), bwd(4 * c, 0), bwd(2 * c, 1),
                  pl.BlockSpec((1, c), lambda b, j: (0, 0))],
        out_specs=[fwd(c, 0), bwd(c, 0)],
        out_shape=[jax.ShapeDtypeStruct((bsz, seq, c), F32)] * 2,
        scratch_shapes=[pltpu.VMEM((2, c // MXU_DIM, MXU_DIM, MXU_DIM), F32)],
        compiler_params=_params("arbitrary", "arbitrary"),
        name="wkv7_chunked",
    )(rkvk, lwa, rkvk, lwa, rw_k_a.reshape(1, c))


NEG_INF = float("-inf")


def _first_max(vals, idx, size):
    m = jnp.max(vals, axis=0, keepdims=True)
    i = jnp.min(jnp.where(vals == m, idx, size), axis=0, keepdims=True)
    return m, i


def _route(scores, biased):
    e, tt = scores.shape
    per = e // N_GROUPS
    rowl = lax.broadcasted_iota(jnp.int32, (per, tt), 0)
    gs = []
    for g in range(N_GROUPS):
        blk = biased[g * per:(g + 1) * per]
        m1, i1 = _first_max(blk, rowl, per)
        m2 = jnp.max(jnp.where(rowl == i1, NEG_INF, blk), axis=0, keepdims=True)
        gs.append(m1 + m2)
    cur = jnp.concatenate(gs, axis=0)
    growl = lax.broadcasted_iota(jnp.int32, (N_GROUPS, tt), 0)
    gsel = jnp.zeros((N_GROUPS, tt), F32)
    for _ in range(TOPK_GROUPS):
        _, ig = _first_max(cur, growl, N_GROUPS)
        hit = growl == ig
        gsel = jnp.where(hit, 1.0, gsel)
        cur = jnp.where(hit, NEG_INF, cur)
    emask = jnp.concatenate([jnp.broadcast_to(gsel[g:g + 1], (per, tt)) for g in range(N_GROUPS)], axis=0)
    masked = jnp.where(emask > 0.5, biased, NEG_INF)
    row = lax.broadcasted_iota(jnp.int32, (e, tt), 0)
    ids, ws = [], []
    for _ in range(TOP_K):
        _, ie = _first_max(masked, row, e)
        hit = row == ie
        ids.append(ie)
        ws.append(jnp.sum(jnp.where(hit, scores, 0.0), axis=0, keepdims=True))
        masked = jnp.where(hit, NEG_INF, masked)
    w = jnp.concatenate(ws, axis=0)
    w = w / jnp.sum(w, axis=0, keepdims=True) * ROUTE_SCALE
    return jnp.concatenate(ids, axis=0), w


def _mixout_kernel(x_ref, mod_ref, yhy_ref, of_ref, ob_ref, g_ref, bonus_ref, lnw_ref, lnb_ref,
                   ones_ref, wout_ref, g2n_ref, rwth_ref, rwtl_ref, bias_ref,
                   x1_ref, h2a_ref, h2b_ref, eid_ref, wsel_ref):
    ones = ones_ref[...]
    s = of_ref[...] + ob_ref[...]
    mean = _dot_exact_rhs(s, ones) * (1.0 / HEAD)
    dlt = s - mean
    var = _dot_exact_rhs(dlt * dlt, ones) * (1.0 / HEAD)
    sn = dlt * lax.rsqrt(var + GN_EPS) * lnw_ref[...] + lnb_ref[...]
    yrw = (sn + bonus_ref[...]) * g_ref[...]
    ch = yhy_ref.shape[-1]
    mix = _dot(yhy_ref[...].astype(BF16), wout_ref[:ch, :]) + _dot(yrw.astype(BF16), wout_ref[ch:, :])
    x1 = x_ref[...] + mod_ref[2:3, :] * mix
    x1_ref[...] = x1
    ms = jnp.mean(x1 * x1, axis=-1, keepdims=True)
    h2 = x1 * lax.rsqrt(ms + NORM_EPS) * g2n_ref[...]
    h2 = h2 * (1.0 + mod_ref[4:5, :]) + mod_ref[3:4, :]
    h2a_ref[...], h2b_ref[...] = _pack_rows(h2)
    rh, rl = rwth_ref[...], rwtl_ref[...]
    hh, hl = _split2(h2)
    logits = _nt(rh, hh) + (_nt(rh, hl) + _nt(rl, hh))
    scores = jax.nn.sigmoid(logits)
    ids, w = _route(scores, scores + bias_ref[...])
    eid_ref[...] = ids
    wsel_ref[...] = w


def _mix_out(x, mod, yhy, o_f, o_b, g, bonus, ln_w, ln_b, w_out, norm2_g, router_w, router_bias, tt=1024):
    bsz, seq, d = x.shape
    tt = min(tt, seq)
    nt = seq // tt
    n = bsz * seq
    c = D_RWKV
    e = router_w.shape[1]
    row = lambda a: a.reshape(1, -1)
    consts = [row(ln_w), row(ln_b), _head_ones(), w_out.astype(BF16), row(norm2_g), *_split2(router_w.T),
              jnp.broadcast_to(router_bias.reshape(e, 1), (e, tt))]
    const = lambda a: pl.BlockSpec(a.shape, lambda b, i: (0,) * a.ndim, pipeline_mode=pl.Buffered(1))
    tile = lambda w: pl.BlockSpec((None, tt, w), lambda b, i: (b, i, 0))
    flat = lambda rows, dt: jax.ShapeDtypeStruct((rows, n), dt)
    return pl.pallas_call(
        _mixout_kernel,
        grid=(bsz, nt),
        in_specs=[tile(d), pl.BlockSpec((None,) + mod.shape[1:], lambda b, i: (b, 0, 0))]
        + [tile(c)] * 5 + [const(a) for a in consts],
        out_specs=[tile(d), pl.BlockSpec((tt, d // 4), lambda b, i: (b * nt + i, 0)),
                   pl.BlockSpec((tt, d // 4), lambda b, i: (b * nt + i, 0)),
                   pl.BlockSpec((TOP_K, tt), lambda b, i: (0, b * nt + i)),
                   pl.BlockSpec((TOP_K, tt), lambda b, i: (0, b * nt + i))],
        out_shape=[jax.ShapeDtypeStruct((bsz, seq, d), F32), jax.ShapeDtypeStruct((n, d // 4), U32),
                   jax.ShapeDtypeStruct((n, d // 4), U32),
                   flat(TOP_K, jnp.int32), flat(TOP_K, F32)],
        compiler_params=_params("arbitrary", "arbitrary"),
        name="mix_out_router",
    )(x, mod, yhy, o_f, o_b, g, bonus, *consts)


BLK = 512
BLK_SHIFT = 9


def _multi_hot(eid, e):
    row = lax.broadcasted_iota(jnp.int32, (e, eid.shape[1]), 0)
    mh = jnp.zeros((e, eid.shape[1]), F32)
    for kk in range(TOP_K):
        mh = mh + jnp.where(row == eid[kk:kk + 1, :], 1.0, 0.0)
    return row, mh


def _lookup(row, eid, table):
    return jnp.concatenate(
        [jnp.sum(jnp.where(row == eid[kk:kk + 1, :], table, 0.0), axis=0, keepdims=True)
         for kk in range(TOP_K)], axis=0)


def _rank_kernel(eid_ref, rank_ref, cnt_ref, *, e):
    @pl.when(pl.program_id(0) == 0)
    def _():
        cnt_ref[...] = jnp.zeros_like(cnt_ref)

    eid = eid_ref[...]
    tt = eid.shape[1]
    row, mh = _multi_hot(eid, e)
    mhb = mh.astype(BF16)
    si = lax.broadcasted_iota(jnp.int32, (tt, tt), 0)
    ti = lax.broadcasted_iota(jnp.int32, (tt, tt), 1)
    earlier = _dot(mhb, jnp.where(si < ti, 1.0, 0.0).astype(BF16))
    cnt = cnt_ref[...]
    full = earlier + jnp.concatenate([cnt] * (tt // LANES), axis=1)
    rank_ref[...] = _lookup(row, eid, full).astype(jnp.int32)
    cnt_ref[...] = cnt + _dot(mhb, jnp.ones((tt, LANES), BF16))


def _expert_ranks(eid, e, tt=512):
    n = eid.shape[1]
    tt = min(tt, n)
    return pl.pallas_call(
        functools.partial(_rank_kernel, e=e),
        grid=(n // tt,),
        in_specs=[pl.BlockSpec((TOP_K, tt), lambda i: (0, i))],
        out_specs=[pl.BlockSpec((TOP_K, tt), lambda i: (0, i)),
                   pl.BlockSpec((e, LANES), lambda i: (0, 0))],
        out_shape=[jax.ShapeDtypeStruct((TOP_K, n), jnp.int32), jax.ShapeDtypeStruct((e, LANES), F32)],
        compiler_params=_params("arbitrary"),
        name="expert_ranks",
    )(eid)


def _block_offsets(cnt):
    e = cnt.shape[0]
    nblk = ((cnt.astype(jnp.int32) + (BLK - 1)) >> BLK_SHIFT).astype(F32)
    ri = lax.broadcasted_iota(jnp.int32, (e, e), 0)
    ci = lax.broadcasted_iota(jnp.int32, (e, e), 1)
    tril = jnp.where(ci <= ri, 1.0, 0.0).astype(BF16)
    nh, nl = _split2(nblk)
    return nblk, _dot(tril, nh) + _dot(tril, nl)


def _dest_kernel(cnt_ref, eid_ref, rank_ref, dest_ref):
    nblk, end = _block_offsets(cnt_ref[...])
    off = (end - nblk) * float(BLK)
    eid = eid_ref[...]
    tt = eid.shape[1]
    row = lax.broadcasted_iota(jnp.int32, (off.shape[0], tt), 0)
    table = jnp.concatenate([off] * (tt // LANES), axis=1)
    dest_ref[...] = _lookup(row, eid, table).astype(jnp.int32) + rank_ref[...]


def _destinations(cnt, eid, rank, tt=512):
    n = eid.shape[1]
    tt = min(tt, n)
    blk = pl.BlockSpec((TOP_K, tt), lambda i: (0, i))
    return pl.pallas_call(
        _dest_kernel,
        grid=(n // tt,),
        in_specs=[pl.BlockSpec(cnt.shape, lambda i: (0, 0)), blk, blk],
        out_specs=blk,
        out_shape=jax.ShapeDtypeStruct((TOP_K, n), jnp.int32),
        compiler_params=_params("arbitrary"),
        name="expert_destinations",
    )(cnt, eid, rank)


def _meta_kernel(cnt_ref, meta_ref, *, nbp):
    cnt = cnt_ref[...]
    e = cnt.shape[0]
    nblk, end = _block_offsets(cnt)
    rep = lambda a, w: jnp.concatenate([a] * (w // LANES), axis=1)
    b = lax.broadcasted_iota(jnp.int32, (e, nbp), 1).astype(F32)
    blk_e = jnp.minimum(jnp.sum(jnp.where(rep(end, nbp) <= b, 1.0, 0.0), axis=0, keepdims=True), float(e - 1))
    row = lax.broadcasted_iota(jnp.int32, (e, nbp), 0).astype(F32)
    mine = row == blk_e
    left = rep(cnt + (end - nblk) * float(BLK), nbp) - b * float(BLK)
    nvalid = jnp.clip(jnp.sum(jnp.where(mine, left, 0.0), axis=0, keepdims=True), 0.0, float(BLK))
    nused = jnp.max(rep(end, nbp), axis=0, keepdims=True)
    later = jnp.logical_and(row > blk_e, rep(nblk, nbp) > 0.0)
    nxt = jnp.min(jnp.where(later, row, float(e)), axis=0, keepdims=True)
    nxt = jnp.where(nxt >= float(e), -1.0, nxt)
    meta_ref[...] = jnp.concatenate([blk_e, nvalid, nused, nxt, jnp.zeros((4, nbp), F32)],
                                    axis=0).astype(jnp.int32)


def _block_meta(cnt, nb):
    nbp = -(-nb // LANES) * LANES
    return pl.pallas_call(
        functools.partial(_meta_kernel, nbp=nbp),
        out_shape=jax.ShapeDtypeStruct((8, nbp), jnp.int32),
        compiler_params=pltpu.CompilerParams(vmem_limit_bytes=VMEM_LIMIT),
        name="expert_block_meta",
    )(cnt)


SC_WINDOW = 128


def _sc_mesh():
    return plsc.VectorSubcoreMesh(core_axis_name="core", subcore_axis_name="subcore")


def _sc_scatter_rows(rows, idx, nrows):
    n, width = rows.shape

    @pl.kernel(out_type=jax.ShapeDtypeStruct((nrows, width), rows.dtype), mesh=_sc_mesh())
    def scatter(rows_hbm, idx_hbm, out_hbm):
        def body(rows_vmem, idx_vmem):
            pltpu.sync_copy(rows_vmem, out_hbm.at[idx_vmem.at[0]])

        pltpu.emit_pipeline(
            body,
            grid=(n // SC_WINDOW, idx.shape[0]),
            in_specs=[pl.BlockSpec((SC_WINDOW, width), index_map=lambda i, k: (i, 0)),
                      pl.BlockSpec((1, SC_WINDOW), index_map=lambda i, k: (k, i))],
            out_specs=[],
            core_axis_name=("core", "subcore"),
            dimension_semantics=(pltpu.PARALLEL, pltpu.ARBITRARY),
        )(rows_hbm, idx_hbm)

    return scatter(rows, idx)


def _sc_gather_rows(src, idx):
    num = idx.shape[1]
    width = src.shape[1]

    @pl.kernel(out_type=jax.ShapeDtypeStruct((num, width), src.dtype), mesh=_sc_mesh())
    def gather(src_hbm, idx_hbm, out_hbm):
        def body(idx_vmem, out_vmem):
            pltpu.sync_copy(src_hbm.at[idx_vmem.at[0]], out_vmem)

        pltpu.emit_pipeline(
            body,
            grid=(num // SC_WINDOW,),
            in_specs=[pl.BlockSpec((1, SC_WINDOW), index_map=lambda i: (0, i))],
            out_specs=[pl.BlockSpec((SC_WINDOW, width), index_map=lambda i: (i, 0))],
            core_axis_name=("core", "subcore"),
            dimension_semantics=(pltpu.PARALLEL,),
        )(idx_hbm, out_hbm)

    return gather(src, idx)


BLOCKS_PER_STEP = 4


def _experts_kernel(be_ref, nv_ref, nxt_ref, nu_ref, xa_ref, xb_ref, wg_hbm, wu_hbm, wd_hbm, oa_ref, ob_ref,
                    wgf, wuf, wdf, wgb, wub, wdb, sems, slot_ref):
    step = pl.program_id(0)

    def fetch(expert, slot):
        return [pltpu.make_async_copy(src.at[expert], dst.at[slot], sems.at[slot])
                for src, dst in ((wg_hbm, wgf), (wu_hbm, wuf), (wd_hbm, wdf))]

    @pl.when(step == 0)
    def _():
        slot_ref[0] = 0
        for cp in fetch(be_ref[0], 0):
            cp.start()

    def one_block(b, rows):
        prev = be_ref[jnp.maximum(b - 1, 0)]

        @pl.when(jnp.logical_or(b == 0, be_ref[b] != prev))
        def _():
            slot = slot_ref[0]
            for cp in fetch(be_ref[b], slot):
                cp.wait()

            @pl.when(nxt_ref[b] >= 0)
            def _():
                for cp in fetch(nxt_ref[b], 1 - slot):
                    cp.start()

            wgb[...] = wgf[slot].astype(BF16)
            wub[...] = wuf[slot].astype(BF16)
            wdb[...] = wdf[slot].astype(BF16)
            slot_ref[0] = 1 - slot

        valid = lax.broadcasted_iota(jnp.int32, (BLK, 1), 0) < nv_ref[b]
        zero = jnp.zeros((), U32)
        x = _unpack_rows(jnp.where(valid, xa_ref[rows, :], zero), jnp.where(valid, xb_ref[rows, :], zero))
        act = _silu(_dot(x, wgb[...])) * _dot(x, wub[...])
        oa_ref[rows, :], ob_ref[rows, :] = _pack_rows(_dot(act.astype(BF16), wdb[...]))

    for j in range(BLOCKS_PER_STEP):
        b = step * BLOCKS_PER_STEP + j
        rows = slice(j * BLK, (j + 1) * BLK)
        if j == 0:
            one_block(b, rows)
        else:
            pl.when(b < nu_ref[0])(functools.partial(one_block, b, rows))


def _experts(blk_e, nvalid, nused, nxt_e, xs_a, xs_b, wg, wu, wd):
    p, dq = xs_a.shape
    d, de = wg.shape[1], wg.shape[2]
    rows_in = pl.BlockSpec((BLOCKS_PER_STEP * BLK, dq), lambda b, be, nv, nx, nu: (b, 0))
    hbm = pl.BlockSpec(memory_space=pl.ANY)
    return pl.pallas_call(
        _experts_kernel,
        grid_spec=pltpu.PrefetchScalarGridSpec(
            num_scalar_prefetch=4,
            grid=((nused + BLOCKS_PER_STEP - 1) // BLOCKS_PER_STEP,),
            in_specs=[rows_in, rows_in, hbm, hbm, hbm],
            out_specs=[rows_in, rows_in],
            scratch_shapes=[pltpu.VMEM((2, d, de), F32), pltpu.VMEM((2, d, de), F32), pltpu.VMEM((2, de, d), F32),
                            pltpu.VMEM((d, de), BF16), pltpu.VMEM((d, de), BF16), pltpu.VMEM((de, d), BF16),
                            pltpu.SemaphoreType.DMA((2,)), pltpu.SMEM((1,), jnp.int32)],
        ),
        out_shape=[jax.ShapeDtypeStruct((p, dq), U32)] * 2,
        compiler_params=_params("arbitrary"),
        name="moe_experts",
    )(blk_e, nvalid, nxt_e, nused.reshape(1), xs_a, xs_b, wg, wu, wd)


def _shared_kernel(ha_ref, hb_ref, sg_ref, su_ref, sd_ref, o_ref):
    hb = _unpack_rows(ha_ref[...], hb_ref[...])
    act = _silu(_dot(hb, sg_ref[...])) * _dot(hb, su_ref[...])
    o_ref[...] = _dot(act.astype(BF16), sd_ref[...]).astype(o_ref.dtype)


def _shared_expert(h2a, h2b, sh_wg, sh_wu, sh_wd, tt=512):
    n, dp = h2a.shape
    d = sh_wg.shape[0]
    tt = min(tt, n)
    consts = [sh_wg.astype(BF16), sh_wu.astype(BF16), sh_wd.astype(BF16)]
    packed_rows = pl.BlockSpec((tt, dp), lambda i: (i, 0))
    return pl.pallas_call(
        _shared_kernel,
        grid=(n // tt,),
        in_specs=[packed_rows, packed_rows] + [pl.BlockSpec(a.shape, lambda i: (0, 0)) for a in consts],
        out_specs=pl.BlockSpec((tt, d), lambda i: (i, 0)),
        out_shape=jax.ShapeDtypeStruct((n, d), BF16),
        compiler_params=_params("arbitrary"),
        name="shared_expert",
    )(h2a, h2b, *consts)


def _combine_kernel(w_ref, x1_ref, sh_ref, mod_ref, ga_ref, gb_ref, gf_ref, sel_ref, o_ref):
    ffn = sh_ref[...].astype(F32)
    wh, wl = _split2(w_ref[...])
    acc = None
    for kk in range(TOP_K):
        sel = sel_ref[kk]
        wk = _tn(wh, sel) + _tn(wl, sel)
        a_lo, a_hi = _unpack_halves(ga_ref[kk])
        b_lo, b_hi = _unpack_halves(gb_ref[kk])
        parts = [a_lo * wk, b_lo * wk, a_hi * wk, b_hi * wk]
        acc = parts if acc is None else [p + q for p, q in zip(acc, parts)]
    ffn = ffn + jnp.concatenate(acc, axis=1)
    xo = x1_ref[...] + mod_ref[5:6, :] * ffn
    ms = jnp.mean(xo * xo, axis=-1, keepdims=True)
    o_ref[...] = xo * lax.rsqrt(ms + NORM_EPS) * gf_ref[...]


def _combine(wsel, x1, shared, mod, ga, gb, normf_g, tok0, seq, tt=256):
    n, d = x1.shape
    part = ga.shape[1]
    tt = min(tt, seq, part)
    per = seq // tt
    off = tok0 // tt
    dq = ga.shape[2]
    sel = jnp.asarray(np.broadcast_to(np.eye(TOP_K)[:, :, None], (TOP_K, TOP_K, dq)), BF16)
    consts = [normf_g.reshape(1, d), sel]
    const = lambda a: pl.BlockSpec(a.shape, lambda i: (0,) * a.ndim)
    rows = pl.BlockSpec((tt, d), lambda i: (off + i, 0))
    gathered = pl.BlockSpec((TOP_K, tt, dq), lambda i: (0, i, 0))
    return pl.pallas_call(
        _combine_kernel,
        grid=(part // tt,),
        in_specs=[pl.BlockSpec((TOP_K, tt), lambda i: (0, off + i)),
                  rows, rows,
                  pl.BlockSpec((None,) + mod.shape[1:], lambda i: ((off + i) // per, 0, 0)),
                  gathered, gathered] + [const(a) for a in consts],
        out_specs=rows,
        out_shape=jax.ShapeDtypeStruct((n, d), F32),
        input_output_aliases={1: 0},
        compiler_params=_params("arbitrary"),
        name="moe_combine",
    )(wsel, x1, shared, mod, ga, gb, *consts)


COMBINE_PARTS = 4


def _moe(x1, h2a, h2b, mod, eid, wsel, exp_wg, exp_wu, exp_wd, sh_wg, sh_wu, sh_wd, normf_g):
    n = h2a.shape[0]
    e = exp_wg.shape[0]
    nb = (n * TOP_K + e * (BLK - 1)) // BLK
    nb = -(-nb // BLOCKS_PER_STEP) * BLOCKS_PER_STEP
    rank, cnt = _expert_ranks(eid, e)
    dest = _destinations(cnt, eid, rank)
    meta = _block_meta(cnt, nb)
    xs_a = _sc_scatter_rows(h2a, dest, nb * BLK)
    xs_b = _sc_scatter_rows(h2b, dest, nb * BLK)
    shared = _shared_expert(h2a, h2b, sh_wg, sh_wu, sh_wd)
    ys_a, ys_b = _experts(meta[0, :nb], meta[1, :nb], meta[2, 0], meta[3, :nb], xs_a, xs_b,
                          exp_wg, exp_wu, exp_wd)
    bsz, seq, d = x1.shape
    out = x1.reshape(n, d)
    part = n // COMBINE_PARTS
    for j in range(COMBINE_PARTS):
        idx = dest[:, j * part:(j + 1) * part].reshape(1, TOP_K * part)
        ga = _sc_gather_rows(ys_a, idx).reshape(TOP_K, part, -1)
        gb = _sc_gather_rows(ys_b, idx).reshape(TOP_K, part, -1)
        out = _combine(wsel, out, shared, mod, ga, gb, normf_g, j * part, seq)
    return out


def kernel(x, c, norm1_g, norm2_g, normf_g, w_ada, b_ada, w_in, w_out, hy_conv_w, hy_conv_b, hy_pos_w1, hy_pos_b1, hy_pos_w2, hy_pos_b2, hy_pos_w3, hy_sin_freq, hy_skip, rw_mu, rw_w0, rw_w_up, rw_a0, rw_a_up, rw_g_up, rw_k_k, rw_k_a, rw_r_k, rw_ln_w, rw_ln_b, router_w, router_bias, exp_w_gate, exp_w_up, exp_w_down, sh_w_gate, sh_w_up, sh_w_down):
    bsz, seq, d = x.shape
    depth = w_ada.shape[0]
    assert depth == 1, "the final norm is fused into the last kernel of a single layer"
    for l in range(depth):
        mod = _modulation(c, w_ada[l], b_ada[l]).reshape(bsz, -1, d)
        uhy, rkvk, lwa, g, bonus = _projection(
            x, mod, norm1_g[l], w_in[l], hy_conv_w[l], hy_conv_b[l], rw_mu[l], rw_w0[l], rw_w_up[l],
            rw_a0[l], rw_a_up[l], rw_g_up[l], rw_k_k[l], rw_k_a[l], rw_r_k[l])
        k2, ss = _hyena_filters(seq, hy_pos_w1[l], hy_pos_b1[l], hy_pos_w2[l], hy_pos_b2[l],
                                hy_pos_w3[l], hy_sin_freq[l])
        khat = _filter_spectrum(k2, ss, seq)
        z, z_col = uhy, 0
        for order in range(HYENA_ORDER):
            z = _long_conv_gate(z, z_col, uhy, (order + 1) * D_HYENA, khat, hy_skip[l], order)
            z_col = 0
        o_f, o_b = _wkv(rkvk, lwa, rw_k_a[l])
        x1, h2a, h2b, eid, wsel = _mix_out(x, mod, z, o_f, o_b, g, bonus, rw_ln_w[l], rw_ln_b[l], w_out[l],
                                           norm2_g[l], router_w[l], router_bias[l])
        x = _moe(x1, h2a, h2b, mod, eid, wsel, exp_w_gate[l], exp_w_up[l], exp_w_down[l],
                 sh_w_gate[l], sh_w_up[l], sh_w_down[l], normf_g)
        x = x.reshape(bsz, seq, d)
    return x
```

```python
import functools
import math

import jax
import jax.numpy as jnp
import numpy as np
from jax import lax
from jax.experimental import pallas as pl
from jax.experimental.pallas import tpu as pltpu
from jax.experimental.pallas import tpu_sc as plsc

F32 = jnp.float32
BF16 = jnp.bfloat16

LANES = 128
MXU_DIM = 256
VMEM_LIMIT = 56 * 1024 * 1024

D_HYENA = 512
D_RWKV = 512
HEAD = 64
HYENA_ORDER = 2
FILTER_BANDS = 16
DECAY_TARGET = 1e-2
FAST_DECAY_PCT = 0.3
SLOW_DECAY_PCT = 1.5
FILTER_NORM_EPS = 1e-6
DECAY_LORA = 32
ICLR_LORA = 32
GATE_LORA = 96
GN_EPS = 64e-5
NORM_EPS = 1e-6
TOP_K = 8
N_GROUPS = 8
TOPK_GROUPS = 4
ROUTE_SCALE = 2.5


def _params(*sem):
    return pltpu.CompilerParams(dimension_semantics=sem, vmem_limit_bytes=VMEM_LIMIT)


def _split2(a):
    hi = a.astype(BF16)
    lo = (a - hi.astype(F32)).astype(BF16)
    return hi, lo


def _dot(a, b):
    return jnp.dot(a, b, preferred_element_type=F32)


def _dot3(a, b):
    ah, al = _split2(a)
    bh, bl = _split2(b)
    return _dot(ah, bh) + (_dot(ah, bl) + _dot(al, bh))


def _dot_exact_rhs(a, b_bf16):
    ah, al = _split2(a)
    return _dot(ah, b_bf16) + _dot(al, b_bf16)


def _silu(x):
    return x * jax.nn.sigmoid(x)


U32 = jnp.int32


def _pack_halves(x):
    w = x.shape[1] // 2
    return pltpu.pack_elementwise([x[:, :w], x[:, w:]], packed_dtype=BF16)


def _unpack_halves(p):
    lo = pltpu.unpack_elementwise(p, index=0, packed_dtype=BF16, unpacked_dtype=F32)
    hi = pltpu.unpack_elementwise(p, index=1, packed_dtype=BF16, unpacked_dtype=F32)
    return lo, hi


def _pack_rows(x):
    packed = _pack_halves(x)
    half = packed.shape[1] // 2
    return packed[:, :half], packed[:, half:]


def _unpack_rows(a, b):
    a_lo, a_hi = _unpack_halves(a)
    b_lo, b_hi = _unpack_halves(b)
    return jnp.concatenate([a_lo.astype(BF16), b_lo.astype(BF16), a_hi.astype(BF16), b_hi.astype(BF16)], axis=1)


def _mod_kernel(c_ref, w_ref, b_ref, o_ref):
    o_ref[...] = _dot3(_silu(c_ref[...]), w_ref[...]) + b_ref[...]


def _modulation(c, w_ada, b_ada):
    bsz, d = c.shape
    n = w_ada.shape[1]
    blk = 1024
    return pl.pallas_call(
        _mod_kernel,
        grid=(n // blk,),
        in_specs=[
            pl.BlockSpec((bsz, d), lambda j: (0, 0)),
            pl.BlockSpec((d, blk), lambda j: (0, j)),
            pl.BlockSpec((1, blk), lambda j: (0, j)),
        ],
        out_specs=pl.BlockSpec((bsz, blk), lambda j: (0, j)),
        out_shape=jax.ShapeDtypeStruct((bsz, n), F32),
        compiler_params=_params("arbitrary"),
        name="adaln_mod",
    )(c, w_ada, b_ada.reshape(1, n))


def _filter_kernel(band_ref, w1_ref, b1_ref, w2_ref, b2_ref, w3_ref, freq_ref, delta_ref,
                   k_ref, ss_ref, *, seq, rows):
    half = pl.program_id(0)
    i = pl.program_id(1)
    r = lax.broadcasted_iota(jnp.int32, (rows, LANES), 0) + i * rows
    pos = jnp.where(half == 0, r, seq - r).astype(F32)
    tt = pos / float(max(seq - 1, 1))
    lane = lax.broadcasted_iota(jnp.int32, (rows, LANES), 1)
    feats = jnp.where(lane == 0, tt, jnp.sin(pos * band_ref[0:1, :] + band_ref[1:2, :]))
    freq = freq_ref[...]
    hdn = jnp.sin(freq * (_dot3(feats, w1_ref[...]) + b1_ref[...]))
    for j in range(w2_ref.shape[0]):
        hdn = jnp.sin(freq * (_dot3(hdn, w2_ref[j]) + b2_ref[j]))
    filt = _dot3(hdn, w3_ref[...])
    filt = filt * jnp.exp(-tt[:, :1] * delta_ref[...])
    valid = jnp.logical_or(half == 0, r[:, :1] > 0)
    filt = jnp.where(valid, filt, 0.0)
    k_ref[...] = filt

    @pl.when(jnp.logical_and(half == 0, i == 0))
    def _():
        ss_ref[...] = jnp.zeros_like(ss_ref)

    ss_ref[...] += jnp.broadcast_to(jnp.sum(filt * filt, axis=0, keepdims=True), ss_ref.shape)


def _hyena_filters(seq, pw1, pb1, pw2, pb2, pw3, freq):
    width = pw1.shape[1]
    ncol = HYENA_ORDER * D_HYENA
    rows = min(seq, 512)
    bands = np.zeros((2, LANES), np.float64)
    lin = np.linspace(1e-4, FILTER_BANDS - 1, FILTER_BANDS) * (2.0 * math.pi / seq)
    bands[0, 1:1 + FILTER_BANDS] = lin
    bands[1, 1:1 + FILTER_BANDS] = 0.5 * math.pi
    bands[0, 1 + FILTER_BANDS:1 + 2 * FILTER_BANDS] = -lin
    bands = jnp.asarray(bands, F32)
    deltas = np.abs(np.linspace(math.log(DECAY_TARGET) / SLOW_DECAY_PCT,
                                math.log(DECAY_TARGET) / FAST_DECAY_PCT, D_HYENA))
    deltas = jnp.asarray(np.tile(deltas, HYENA_ORDER)[None], F32)
    w1 = jnp.zeros((LANES, width), F32).at[:pw1.shape[0]].set(pw1)
    w3 = pw3.reshape(width, HYENA_ORDER, 2, D_HYENA).transpose(2, 0, 1, 3).reshape(2, width, ncol)
    nt = seq // rows
    full = lambda *shape: pl.BlockSpec(shape, lambda h, i: (0,) * len(shape))
    return pl.pallas_call(
        functools.partial(_filter_kernel, seq=seq, rows=rows),
        grid=(2, nt),
        in_specs=[
            full(2, LANES), full(LANES, width), full(1, width),
            full(pw2.shape[0], width, width), full(pw2.shape[0], 1, width),
            pl.BlockSpec((None, width, ncol), lambda h, i: (h, 0, 0)),
            full(1, width), full(1, ncol),
        ],
        out_specs=[
            pl.BlockSpec((rows, ncol), lambda h, i: (h * nt + i, 0)),
            pl.BlockSpec((8, ncol), lambda h, i: (0, 0)),
        ],
        out_shape=[jax.ShapeDtypeStruct((2 * seq, ncol), F32),
                   jax.ShapeDtypeStruct((8, ncol), F32)],
        compiler_params=_params("arbitrary", "arbitrary"),
        name="hyena_filters",
    )(bands, w1, pb1.reshape(1, width), pw2, pb2.reshape(pw2.shape[0], 1, width), w3,
      freq.reshape(1, width), deltas)


N1 = LANES
UNROLL = 8


def _dft_tables(seq):
    tables = _dft_tables_np(seq)
    return tuple(jnp.asarray(t, BF16) for t in tables[:5]) + tables[5:]


def _dft_tables_np(seq):
    m = 2 * seq
    n2 = m // N1
    n2h = n2 // 2
    n1 = np.arange(N1)[:, None, None]
    f2 = np.arange(n2)[None, :, None]
    k2 = np.arange(n2)[None, None, :]
    th = 2.0 * np.pi * (n1 * f2 / m + (k2 * f2 % n2) / n2)
    fwd_a = np.concatenate([np.cos(th), -np.sin(th)], axis=1)
    tht = np.transpose(th, (0, 2, 1))
    inv_a = np.concatenate([np.cos(tht), -np.sin(tht)], axis=2)[:, :n2h] / m
    a = np.arange(N1)
    ph = 2.0 * np.pi * np.outer(a, a) / N1
    c, s = np.cos(ph), np.sin(ph)
    fwd_b = np.block([[c, s], [-s, c]])
    inv_b = np.block([[c, -s], [s, c]])
    return fwd_a, fwd_a[:, :, :n2h], inv_a, fwd_b, inv_b, n2, n2h


def _stage_a_fwd(x_ref, wa_ref, y_ref, n2, scale=None):
    def body(i, carry):
        trips = [i * UNROLL + j for j in range(UNROLL)]
        xs = [x_ref[pl.ds(n1, wa_ref.shape[2], stride=N1), :] for n1 in trips]
        if scale is not None:
            xs = [x * scale for x in xs]
        prods = [_dot(wa_ref[n1], x.astype(BF16)) for n1, x in zip(trips, xs)]
        for n1, a in zip(trips, prods):
            y_ref[pl.ds(n1, n2, stride=2 * N1), :] = a[:n2]
            y_ref[pl.ds(N1 + n1, n2, stride=2 * N1), :] = a[n2:]
        return carry
    lax.fori_loop(0, N1 // UNROLL, body, 0)


def _filter_fft_kernel(k_ref, ss_ref, wa_ref, fb_ref, o_ref, y_ref, *, n2):
    scale = lax.rsqrt(ss_ref[0:1, :] + FILTER_NORM_EPS)
    _stage_a_fwd(k_ref, wa_ref, y_ref, n2, scale=scale)

    unr = min(UNROLL, n2)

    def body(i, carry):
        trips = [i * unr + j for j in range(unr)]
        ys = [y_ref[pl.ds(pl.multiple_of(f2 * 2 * N1, 2 * N1), 2 * N1), :].astype(BF16) for f2 in trips]
        for j in range(0, unr, 2):
            z = _dot(fb_ref[...], jnp.concatenate(ys[j:j + 2], axis=1))
            o_ref[trips[j]] = z[:, :LANES]
            o_ref[trips[j + 1]] = z[:, LANES:]
        return carry
    lax.fori_loop(0, n2 // unr, body, 0)


def _filter_spectrum(k2, ss, seq):
    fwd_a, _, _, fwd_b, _, n2, _ = _dft_tables(seq)
    ncol = k2.shape[1]
    nblk = ncol // LANES
    return pl.pallas_call(
        functools.partial(_filter_fft_kernel, n2=n2),
        grid=(nblk,),
        in_specs=[
            pl.BlockSpec((2 * seq, LANES), lambda c: (0, c)),
            pl.BlockSpec((8, LANES), lambda c: (0, c)),
            pl.BlockSpec(fwd_a.shape, lambda c: (0, 0, 0)),
            pl.BlockSpec(fwd_b.shape, lambda c: (0, 0)),
        ],
        out_specs=pl.BlockSpec((None, n2, 2 * N1, LANES), lambda c: (c, 0, 0, 0)),
        out_shape=jax.ShapeDtypeStruct((nblk, n2, 2 * N1, LANES), F32),
        scratch_shapes=[pltpu.VMEM((n2 * 2 * N1, LANES), F32)],
        compiler_params=_params("arbitrary"),
        name="hyena_filter_fft",
    )(k2, ss, fwd_a, fwd_b)


TILE = 8
N1_GROUPS = N1 // TILE


def _tile_tables(seq):
    _, fwd_a, inv_a, _, _, n2, n2h = _dft_tables_np(seq)
    eye = np.eye(TILE)
    fa = fwd_a.reshape(N1_GROUPS, TILE, 2 * n2, n2h)
    wa = np.einsum("qjrn,jk->qrjnk", fa, eye).reshape(N1_GROUPS, 2 * n2 * TILE, n2h * TILE)
    ia = inv_a.reshape(N1_GROUPS, TILE, n2h, 2 * n2)
    vc = np.einsum("qjnr,jk->qnjrk", ia, eye).reshape(N1_GROUPS, n2h * TILE, 2 * n2 * TILE)
    return jnp.asarray(wa, BF16), jnp.asarray(vc, BF16)


def _conv_kernel(u_ref, g_ref, skip_ref, fb_ref, ib_ref, kh_hbm, wa_hbm, vc_hbm, o_ref,
                 y_ref, kh_ref, wa_ref, vc_ref, sem, *, n2, n2h, kh_first):
    c_id, b_id = pl.program_id(0), pl.program_id(1)

    @pl.when(jnp.logical_and(c_id == 0, b_id == 0))
    def _():
        for src, dst in ((wa_hbm, wa_ref), (vc_hbm, vc_ref)):
            cp = pltpu.make_async_copy(src, dst, sem)
            cp.start()
            cp.wait()

    @pl.when(b_id == 0)
    def _():
        cp = pltpu.make_async_copy(kh_hbm.at[kh_first + c_id], kh_ref, sem)
        cp.start()
        cp.wait()

    def y_tile(rf, base):
        ri, f2 = divmod(rf, n2)
        return pl.ds(f2 * 2 * N1 + ri * N1 + base, TILE)

    def stage_a(q, carry):
        base = pl.multiple_of(q * TILE, TILE)
        x = jnp.concatenate([u_ref[pl.ds(N1 * m + base, TILE), :] for m in range(n2h)], axis=0)
        r = _dot(wa_ref[q], x.astype(BF16))
        for rf in range(2 * n2):
            y_ref[y_tile(rf, base), :] = r[rf * TILE:(rf + 1) * TILE]
        return carry
    lax.fori_loop(0, N1_GROUPS, stage_a, 0, unroll=8)

    unr = min(2 * UNROLL, n2)

    def mid(i, carry):
        trips = [i * unr + j for j in range(unr)]
        offs = [pl.multiple_of(f2 * 2 * N1, 2 * N1) for f2 in trips]
        wide = lambda blocks: [jnp.concatenate(blocks[j:j + 2], axis=1) for j in range(0, len(blocks), 2)]
        ys = wide([y_ref[pl.ds(off, 2 * N1), :].astype(BF16) for off in offs])
        khs = wide([kh_ref[f2] for f2 in trips])
        zs = [_dot(fb_ref[...], y) for y in ys]
        ps = []
        for z, kh in zip(zs, khs):
            zr, zi = z[:N1], z[N1:]
            kr, ki = kh[:N1], kh[N1:]
            ps.append(jnp.concatenate([zr * kr - zi * ki, zr * ki + zi * kr], axis=0).astype(BF16))
        gs = [_dot(ib_ref[...], p) for p in ps]
        for j, g in enumerate(gs):
            y_ref[pl.ds(offs[2 * j], 2 * N1), :] = g[:, :LANES]
            y_ref[pl.ds(offs[2 * j + 1], 2 * N1), :] = g[:, LANES:]
        return carry
    lax.fori_loop(0, n2 // unr, mid, 0)

    skip = skip_ref[...]

    def stage_c(q, carry):
        base = pl.multiple_of(q * TILE, TILE)
        g = jnp.concatenate([y_ref[y_tile(rf, base), :] for rf in range(2 * n2)], axis=0)
        conv = _dot(vc_ref[q], g.astype(BF16))
        for m in range(n2h):
            rows = pl.ds(N1 * m + base, TILE)
            o_ref[rows, :] = g_ref[rows, :] * (conv[m * TILE:(m + 1) * TILE] + u_ref[rows, :] * skip)
        return carry
    lax.fori_loop(0, N1_GROUPS, stage_c, 0, unroll=8)


def _long_conv_gate(u, u_col, gate, gate_col, khat, skip, order):
    bsz, seq, _ = u.shape
    ch = D_HYENA
    _, _, _, fwd_b, inv_b, n2, n2h = _dft_tables(seq)
    wa, vc = _tile_tables(seq)
    nblk = ch // LANES
    const = lambda a: pl.BlockSpec(a.shape, lambda c, b: (0,) * a.ndim)
    at = lambda col: pl.BlockSpec((None, seq, LANES), lambda c, b: (b, 0, col // LANES + c))
    hbm = pl.BlockSpec(memory_space=pl.ANY)
    return pl.pallas_call(
        functools.partial(_conv_kernel, n2=n2, n2h=n2h, kh_first=order * nblk),
        grid=(nblk, bsz),
        in_specs=[
            at(u_col), at(gate_col),
            pl.BlockSpec((1, LANES), lambda c, b: (0, c)),
            const(fwd_b), const(inv_b), hbm, hbm, hbm,
        ],
        out_specs=at(0),
        out_shape=jax.ShapeDtypeStruct((bsz, seq, ch), F32),
        scratch_shapes=[pltpu.VMEM((n2 * 2 * N1, LANES), F32), pltpu.VMEM(khat.shape[1:], F32),
                        pltpu.VMEM(wa.shape, BF16), pltpu.VMEM(vc.shape, BF16), pltpu.SemaphoreType.DMA(())],
        compiler_params=_params("arbitrary", "arbitrary"),
        name=f"hyena_conv{order}",
    )(u, gate, skip[order].reshape(1, ch), fwd_b, inv_b, khat, wa, vc)


HALO = 8


def _shift_rows(p, k):
    return pltpu.roll(p, k % p.shape[0], axis=0)


def _proj_kernel(xp_ref, x_ref, xn_ref, mod_ref, g1_ref, why_ref, wrkv_ref, wlora_ref,
                 cw_ref, cb_ref, murkv_ref, mulora_ref, w0_ref, a0_ref, wwah_ref, wwal_ref, gup_ref,
                 kk_ref, ka_ref, rk_ref, ones_ref,
                 uhy_ref, rkvk_ref, lwa_ref, g_ref, bonus_ref,
                 *, tt, nt):
    i = pl.program_id(1)
    xe = jnp.concatenate([xp_ref[...], x_ref[...], xn_ref[...]], axis=0)
    ms = jnp.mean(xe * xe, axis=-1, keepdims=True)
    h = xe * lax.rsqrt(ms + NORM_EPS) * g1_ref[...]
    h = h * (1.0 + mod_ref[1:2, :]) + mod_ref[0:1, :]
    row = lax.broadcasted_iota(jnp.int32, (tt + 2 * HALO, 1), 0)
    inside = jnp.logical_and(jnp.logical_or(row >= HALO, i > 0),
                             jnp.logical_or(row < tt + HALO, i < nt - 1))
    hb = jnp.where(inside, h, 0.0).astype(BF16)
    mid = slice(HALO, tt + HALO)

    p = _dot(hb, why_ref[...])
    u = (_shift_rows(p, 1) * cw_ref[0:1, :] + p * cw_ref[1:2, :]
         + _shift_rows(p, -1) * cw_ref[2:3, :] + cb_ref[...])
    uhy_ref[...] = u[mid]

    p = _dot(hb, wrkv_ref[...])
    p = p + murkv_ref[...] * (0.5 * (_shift_rows(p, 1) + _shift_rows(p, -1)) - p)
    p = p[mid]
    c = D_RWKV
    r, k, v = p[:, :c], p[:, c:2 * c], p[:, 2 * c:]
    rkvk_ref[:, :3 * c] = p

    q = _dot(hb, wlora_ref[...])
    q = q + mulora_ref[...] * (0.5 * (_shift_rows(q, 1) + _shift_rows(q, -1)) - q)
    q = q[mid]
    wa = q[:, :LANES]
    lane = lax.broadcasted_iota(jnp.int32, wa.shape, 1)
    wa = jnp.where(lane < 2 * DECAY_LORA, jnp.tanh(wa), wa)
    wah, wal = _split2(wa)
    up = _dot(wah, wwah_ref[...]) + (_dot(wah, wwal_ref[...]) + _dot(wal, wwah_ref[...]))
    lw = -math.exp(-0.5) * jax.nn.sigmoid(w0_ref[...] + up[:, :2 * c])
    a = jax.nn.sigmoid(a0_ref[...] + up[:, 2 * c:])
    for dd in range(2):
        lwa_ref[:, 2 * dd * c:(2 * dd + 1) * c] = lw[:, dd * c:(dd + 1) * c]
        lwa_ref[:, (2 * dd + 1) * c:(2 * dd + 2) * c] = a[:, dd * c:(dd + 1) * c]
    g_ref[...] = _dot3(jax.nn.sigmoid(q[:, LANES:]), gup_ref[...])

    ones = ones_ref[...]
    kk = k * kk_ref[...]
    nrm = jnp.sqrt(_dot_exact_rhs(kk * kk, ones))
    rkvk_ref[:, 3 * c:] = kk / jnp.maximum(nrm, 1e-12)
    ka = ka_ref[...]
    ksum = k * (2.0 + (a[:, :c] + a[:, c:] - 2.0) * ka)
    bonus_ref[...] = _dot_exact_rhs(r * ksum * rk_ref[...], ones) * v


def _head_ones():
    hid = np.arange(D_RWKV) // HEAD
    return jnp.asarray(hid[:, None] == hid[None, :], BF16)


def _projection(x, mod, norm1_g, w_in, hy_conv_w, hy_conv_b, rw_mu, rw_w0, rw_w_up, rw_a0,
                rw_a_up, rw_g_up, rw_k_k, rw_k_a, rw_r_k, tt=512):
    bsz, seq, d = x.shape
    tt = min(tt, seq)
    nt = seq // tt
    c = D_RWKV
    hy = (HYENA_ORDER + 1) * D_HYENA
    nlora = 2 * LANES
    w_hy = w_in[:, :hy].astype(BF16)
    w_rkv = w_in[:, hy:hy + 3 * c].astype(BF16)
    w_lora = jnp.zeros((d, nlora), F32).at[:, :w_in.shape[1] - hy - 3 * c].set(w_in[:, hy + 3 * c:]).astype(BF16)
    mu_rkv = rw_mu[:3 * c].reshape(1, 3 * c)
    mu_lora = jnp.zeros((1, nlora), F32).at[0, :rw_mu.shape[0] - 3 * c].set(rw_mu[3 * c:])
    wwa = jnp.zeros((LANES, 4 * c), F32)
    for dd in range(2):
        wwa = wwa.at[dd * DECAY_LORA:(dd + 1) * DECAY_LORA, dd * c:(dd + 1) * c].set(rw_w_up[dd])
        wwa = wwa.at[2 * DECAY_LORA + dd * ICLR_LORA:2 * DECAY_LORA + (dd + 1) * ICLR_LORA,
                     2 * c + dd * c:2 * c + (dd + 1) * c].set(rw_a_up[dd])
    gup = jnp.zeros((LANES, c), F32).at[:GATE_LORA].set(rw_g_up)
    row = lambda a: a.reshape(1, -1)

    nb8 = seq // HALO
    tb = tt // HALO
    const = lambda a: pl.BlockSpec(a.shape, lambda b, i: (0,) * a.ndim, pipeline_mode=pl.Buffered(1))
    tile = lambda w: pl.BlockSpec((None, tt, w), lambda b, i: (b, i, 0))
    ins = [
        (x, pl.BlockSpec((None, HALO, d), lambda b, i: (b, jnp.maximum(i * tb - 1, 0), 0))),
        (x, pl.BlockSpec((None, tt, d), lambda b, i: (b, i, 0))),
        (x, pl.BlockSpec((None, HALO, d), lambda b, i: (b, jnp.minimum((i + 1) * tb, nb8 - 1), 0))),
        (mod, pl.BlockSpec((None,) + mod.shape[1:], lambda b, i: (b, 0, 0))),
    ]
    consts = [row(norm1_g), w_hy, w_rkv, w_lora, hy_conv_w, row(hy_conv_b), mu_rkv, mu_lora,
              row(rw_w0), row(rw_a0), *_split2(wwa), gup, row(rw_k_k), row(rw_k_a), row(rw_r_k), _head_ones()]
    ins += [(a, const(a)) for a in consts]
    widths = [hy, 4 * c, 4 * c, c, c]
    return pl.pallas_call(
        functools.partial(_proj_kernel, tt=tt, nt=nt),
        grid=(bsz, nt),
        in_specs=[s for _, s in ins],
        out_specs=[tile(w) for w in widths],
        out_shape=[jax.ShapeDtypeStruct((bsz, seq, w), F32) for w in widths],
        compiler_params=_params("arbitrary", "arbitrary"),
        name="input_projection",
    )(*[a for a, _ in ins])


CHUNK = HEAD
GROUP = MXU_DIM // HEAD


def _nt(a, b):
    return lax.dot_general(a, b, (((1,), (1,)), ((), ())), preferred_element_type=F32)


def _tn(a, b):
    return lax.dot_general(a, b, (((0,), (0,)), ((), ())), preferred_element_type=F32)


def _wkv_direction(r, k, v, kk, lw, a, ka, s_ref, reverse):
    c = CHUNK
    ti = lax.broadcasted_iota(jnp.int32, (c, c), 0)
    si = lax.broadcasted_iota(jnp.int32, (c, c), 1)
    tri = (si >= ti) if reverse else (si <= ti)
    cum = _dot_exact_lhs(jnp.where(tri, 1.0, 0.0).astype(BF16), lw)
    tot = jnp.sum(lw, axis=0, keepdims=True)
    w_incl = jnp.exp(cum)
    w_prev = jnp.exp(cum - lw)
    w_inv = jnp.exp(-cum)
    w_end = jnp.exp(tot - cum)
    w_tot = jnp.exp(tot)
    kd = k * (1.0 + (a - 1.0) * ka)
    b = kk * a
    a_w = -kk * w_prev
    r_w = r * w_incl
    b_w = b * w_inv
    k_w = kd * w_inv
    b_e = b * w_end
    k_e = kd * w_end

    m = MXU_DIM
    ri = lax.broadcasted_iota(jnp.int32, (m, m), 0)
    ci = lax.broadcasted_iota(jnp.int32, (m, m), 1)
    head_mask = (ri // HEAD) == (ci // HEAD)
    tl = lax.broadcasted_iota(jnp.int32, (c, m), 0)
    sl = lax.broadcasted_iota(jnp.int32, (c, m), 1) % c
    strict = (sl > tl) if reverse else (sl < tl)
    incl = (sl >= tl) if reverse else (sl <= tl)
    eye = jnp.where(sl == tl, 1.0, 0.0)
    both = lambda top, bot: jnp.concatenate([top, bot], axis=0)

    def stack(xg):
        xb = xg.astype(BF16)
        return jnp.where(head_mask, jnp.concatenate([xb] * GROUP, axis=0), jnp.zeros((), BF16))

    streams = []
    for g in range(D_RWKV // m):
        sl_g = slice(g * m, (g + 1) * m)
        streams.append(dict(
            ar=both(a_w[:, sl_g], r_w[:, sl_g]).astype(BF16),
            b_st=stack(b_w[:, sl_g]), k_st=stack(k_w[:, sl_g]), v_st=stack(v[:, sl_g]),
            v=v[:, sl_g], bk=both(b_e[:, sl_g], k_e[:, sl_g]).astype(BF16),
            w_tot=w_tot[:, sl_g], s_ref=s_ref.at[g],
            strict=strict, incl=incl, eye=eye, head_mask=head_mask, stack=stack))
    return streams


def _wkv_streams_step(streams):
    c = CHUNK
    both = lambda top, bot: jnp.concatenate([top, bot], axis=0)
    for st in streams:
        st["s"] = st["s_ref"][...]
        st["xb"] = _nt(st["ar"], st["b_st"])
        st["xk"] = _nt(st["ar"], st["k_st"])
        st["xs"] = _nt(st["ar"], st["s"].astype(BF16))
    for st in streams:
        m_k = both(jnp.where(st["strict"], st["xk"][:c], 0.0), jnp.where(st["incl"], st["xk"][c:], 0.0))
        st["kv"] = _dot(m_k.astype(BF16), st["v_st"])
        st["rhs"] = st["xs"][:c] + st["kv"][:c]
        st["pw"] = jnp.where(st["strict"], st["xb"][:c], 0.0)
        st["t"] = st["eye"] + st["pw"]
        st["p_st"] = st["stack"](st["pw"])
    levels = int(math.log2(c)) - 1
    for st in streams:
        st["pw"] = _dot(st["pw"].astype(BF16), st["p_st"])
        st["p_st"] = st["stack"](st["pw"])
    for lvl in range(1, levels + 1):
        for st in streams:
            if lvl < levels:
                prod = _dot(both(st["pw"], st["t"]).astype(BF16), st["p_st"])
                st["pw"] = prod[:c]
                st["t"] = st["t"] + prod[c:]
                st["p_st"] = st["stack"](st["pw"])
            else:
                st["t"] = st["t"] + _dot(st["t"].astype(BF16), st["p_st"])
    for st in streams:
        st["u"] = _dot(st["t"].astype(BF16), st["stack"](st["rhs"]))
    outs = []
    for st in streams:
        m_rb = jnp.where(st["incl"], st["xb"][c:], 0.0)
        outs.append(st["xs"][c:] + _dot(m_rb.astype(BF16), st["stack"](st["u"])) + st["kv"][c:])
        uv = both(st["u"], st["v"]).astype(BF16)
        st["s_ref"][...] = st["s"] * st["w_tot"] + jnp.where(st["head_mask"], _tn(uv, st["bk"]), 0.0)
    return outs


def _dot_exact_lhs(tri_bf16, x):
    xh, xl = _split2(x)
    return _dot(tri_bf16, xh) + _dot(tri_bf16, xl)


def _wkv_kernel(rkvk_f, lwa_f, rkvk_b, lwa_b, ka_ref, of_ref, ob_ref, s_ref, *, nch):
    @pl.when(pl.program_id(1) == 0)
    def _():
        s_ref[...] = jnp.zeros_like(s_ref)

    ka = ka_ref[...]
    c = D_RWKV

    def operands(rkvk_ref, lwa_ref, rows):
        x = rkvk_ref[rows, :]
        la = lwa_ref[rows, :]
        return x[:, :c], x[:, c:2 * c], x[:, 2 * c:3 * c], x[:, 3 * c:], la[:, :c], la[:, c:]

    for ci in range(nch):
        rows_f = slice(ci * CHUNK, (ci + 1) * CHUNK)
        rows_b = slice((nch - 1 - ci) * CHUNK, (nch - ci) * CHUNK)
        fwd = _wkv_direction(*operands(rkvk_f, lwa_f, rows_f), ka, s_ref.at[0], False)
        bwd = _wkv_direction(*operands(rkvk_b, lwa_b, rows_b), ka, s_ref.at[1], True)
        outs = _wkv_streams_step(fwd + bwd)
        of_ref[rows_f, :] = jnp.concatenate(outs[:len(fwd)], axis=1)
        ob_ref[rows_b, :] = jnp.concatenate(outs[len(fwd):], axis=1)


WKV_CHUNKS_PER_STEP = 8


def _wkv(rkvk, lwa, rw_k_a):
    bsz, seq, _ = rkvk.shape
    c = D_RWKV
    nch = WKV_CHUNKS_PER_STEP if seq % (WKV_CHUNKS_PER_STEP * CHUNK) == 0 else 1
    rows = nch * CHUNK
    nb = seq // rows
    fwd = lambda w, lane_blk: pl.BlockSpec((None, rows, w), lambda b, j: (b, j, lane_blk))
    bwd = lambda w, lane_blk: pl.BlockSpec((None, rows, w), lambda b, j: (b, nb - 1 - j, lane_blk))
    return pl.pallas_call(
        functools.partial(_wkv_kernel, nch=nch),
        grid=(bsz, nb),
        in_specs=[fwd(4 * c, 0), fwd(2 * c, 0), bwd(4 * c, 0), bwd(2 * c, 1),
                  pl.BlockSpec((1, c), lambda b, j: (0, 0))],
        out_specs=[fwd(c, 0), bwd(c, 0)],
        out_shape=[jax.ShapeDtypeStruct((bsz, seq, c), F32)] * 2,
        scratch_shapes=[pltpu.VMEM((2, c // MXU_DIM, MXU_DIM, MXU_DIM), F32)],
        compiler_params=_params("arbitrary", "arbitrary"),
        name="wkv7_chunked",
    )(rkvk, lwa, rkvk, lwa, rw_k_a.reshape(1, c))


NEG_INF = float("-inf")


def _first_max(vals, idx, size):
    m = jnp.max(vals, axis=0, keepdims=True)
    i = jnp.min(jnp.where(vals == m, idx, size), axis=0, keepdims=True)
    return m, i


def _route(scores, biased):
    e, tt = scores.shape
    per = e // N_GROUPS
    rowl = lax.broadcasted_iota(jnp.int32, (per, tt), 0)
    gs = []
    for g in range(N_GROUPS):
        blk = biased[g * per:(g + 1) * per]
        m1, i1 = _first_max(blk, rowl, per)
        m2 = jnp.max(jnp.where(rowl == i1, NEG_INF, blk), axis=0, keepdims=True)
        gs.append(m1 + m2)
    cur = jnp.concatenate(gs, axis=0)
    growl = lax.broadcasted_iota(jnp.int32, (N_GROUPS, tt), 0)
    gsel = jnp.zeros((N_GROUPS, tt), F32)
    for _ in range(TOPK_GROUPS):
        _, ig = _first_max(cur, growl, N_GROUPS)
        hit = growl == ig
        gsel = jnp.where(hit, 1.0, gsel)
        cur = jnp.where(hit, NEG_INF, cur)
    emask = jnp.concatenate([jnp.broadcast_to(gsel[g:g + 1], (per, tt)) for g in range(N_GROUPS)], axis=0)
    masked = jnp.where(emask > 0.5, biased, NEG_INF)
    row = lax.broadcasted_iota(jnp.int32, (e, tt), 0)
    ids, ws = [], []
    for _ in range(TOP_K):
        _, ie = _first_max(masked, row, e)
        hit = row == ie
        ids.append(ie)
        ws.append(jnp.sum(jnp.where(hit, scores, 0.0), axis=0, keepdims=True))
        masked = jnp.where(hit, NEG_INF, masked)
    w = jnp.concatenate(ws, axis=0)
    w = w / jnp.sum(w, axis=0, keepdims=True) * ROUTE_SCALE
    return jnp.concatenate(ids, axis=0), w


def _mixout_kernel(x_ref, mod_ref, yhy_ref, of_ref, ob_ref, g_ref, bonus_ref, lnw_ref, lnb_ref,
                   ones_ref, wout_ref, g2n_ref, rwth_ref, rwtl_ref, bias_ref,
                   x1_ref, h2a_ref, h2b_ref, eid_ref, wsel_ref):
    ones = ones_ref[...]
    s = of_ref[...] + ob_ref[...]
    mean = _dot_exact_rhs(s, ones) * (1.0 / HEAD)
    dlt = s - mean
    var = _dot_exact_rhs(dlt * dlt, ones) * (1.0 / HEAD)
    sn = dlt * lax.rsqrt(var + GN_EPS) * lnw_ref[...] + lnb_ref[...]
    yrw = (sn + bonus_ref[...]) * g_ref[...]
    ch = yhy_ref.shape[-1]
    mix = _dot(yhy_ref[...].astype(BF16), wout_ref[:ch, :]) + _dot(yrw.astype(BF16), wout_ref[ch:, :])
    x1 = x_ref[...] + mod_ref[2:3, :] * mix
    x1_ref[...] = x1
    ms = jnp.mean(x1 * x1, axis=-1, keepdims=True)
    h2 = x1 * lax.rsqrt(ms + NORM_EPS) * g2n_ref[...]
    h2 = h2 * (1.0 + mod_ref[4:5, :]) + mod_ref[3:4, :]
    h2a_ref[...], h2b_ref[...] = _pack_rows(h2)
    rh, rl = rwth_ref[...], rwtl_ref[...]
    hh, hl = _split2(h2)
    logits = _nt(rh, hh) + (_nt(rh, hl) + _nt(rl, hh))
    scores = jax.nn.sigmoid(logits)
    ids, w = _route(scores, scores + bias_ref[...])
    eid_ref[...] = ids
    wsel_ref[...] = w


def _mix_out(x, mod, yhy, o_f, o_b, g, bonus, ln_w, ln_b, w_out, norm2_g, router_w, router_bias, tt=1024):
    bsz, seq, d = x.shape
    tt = min(tt, seq)
    nt = seq // tt
    n = bsz * seq
    c = D_RWKV
    e = router_w.shape[1]
    row = lambda a: a.reshape(1, -1)
    consts = [row(ln_w), row(ln_b), _head_ones(), w_out.astype(BF16), row(norm2_g), *_split2(router_w.T),
              jnp.broadcast_to(router_bias.reshape(e, 1), (e, tt))]
    const = lambda a: pl.BlockSpec(a.shape, lambda b, i: (0,) * a.ndim, pipeline_mode=pl.Buffered(1))
    tile = lambda w: pl.BlockSpec((None, tt, w), lambda b, i: (b, i, 0))
    flat = lambda rows, dt: jax.ShapeDtypeStruct((rows, n), dt)
    return pl.pallas_call(
        _mixout_kernel,
        grid=(bsz, nt),
        in_specs=[tile(d), pl.BlockSpec((None,) + mod.shape[1:], lambda b, i: (b, 0, 0))]
        + [tile(c)] * 5 + [const(a) for a in consts],
        out_specs=[tile(d), pl.BlockSpec((tt, d // 4), lambda b, i: (b * nt + i, 0)),
                   pl.BlockSpec((tt, d // 4), lambda b, i: (b * nt + i, 0)),
                   pl.BlockSpec((TOP_K, tt), lambda b, i: (0, b * nt + i)),
                   pl.BlockSpec((TOP_K, tt), lambda b, i: (0, b * nt + i))],
        out_shape=[jax.ShapeDtypeStruct((bsz, seq, d), F32), jax.ShapeDtypeStruct((n, d // 4), U32),
                   jax.ShapeDtypeStruct((n, d // 4), U32),
                   flat(TOP_K, jnp.int32), flat(TOP_K, F32)],
        compiler_params=_params("arbitrary", "arbitrary"),
        name="mix_out_router",
    )(x, mod, yhy, o_f, o_b, g, bonus, *consts)


BLK = 512
BLK_SHIFT = 9


def _multi_hot(eid, e):
    row = lax.broadcasted_iota(jnp.int32, (e, eid.shape[1]), 0)
    mh = jnp.zeros((e, eid.shape[1]), F32)
    for kk in range(TOP_K):
        mh = mh + jnp.where(row == eid[kk:kk + 1, :], 1.0, 0.0)
    return row, mh


def _lookup(row, eid, table):
    return jnp.concatenate(
        [jnp.sum(jnp.where(row == eid[kk:kk + 1, :], table, 0.0), axis=0, keepdims=True)
         for kk in range(TOP_K)], axis=0)


def _rank_kernel(eid_ref, rank_ref, cnt_ref, *, e):
    @pl.when(pl.program_id(0) == 0)
    def _():
        cnt_ref[...] = jnp.zeros_like(cnt_ref)

    eid = eid_ref[...]
    tt = eid.shape[1]
    row, mh = _multi_hot(eid, e)
    mhb = mh.astype(BF16)
    si = lax.broadcasted_iota(jnp.int32, (tt, tt), 0)
    ti = lax.broadcasted_iota(jnp.int32, (tt, tt), 1)
    earlier = _dot(mhb, jnp.where(si < ti, 1.0, 0.0).astype(BF16))
    cnt = cnt_ref[...]
    full = earlier + jnp.concatenate([cnt] * (tt // LANES), axis=1)
    rank_ref[...] = _lookup(row, eid, full).astype(jnp.int32)
    cnt_ref[...] = cnt + _dot(mhb, jnp.ones((tt, LANES), BF16))


def _expert_ranks(eid, e, tt=512):
    n = eid.shape[1]
    tt = min(tt, n)
    return pl.pallas_call(
        functools.partial(_rank_kernel, e=e),
        grid=(n // tt,),
        in_specs=[pl.BlockSpec((TOP_K, tt), lambda i: (0, i))],
        out_specs=[pl.BlockSpec((TOP_K, tt), lambda i: (0, i)),
                   pl.BlockSpec((e, LANES), lambda i: (0, 0))],
        out_shape=[jax.ShapeDtypeStruct((TOP_K, n), jnp.int32), jax.ShapeDtypeStruct((e, LANES), F32)],
        compiler_params=_params("arbitrary"),
        name="expert_ranks",
    )(eid)


def _block_offsets(cnt):
    e = cnt.shape[0]
    nblk = ((cnt.astype(jnp.int32) + (BLK - 1)) >> BLK_SHIFT).astype(F32)
    ri = lax.broadcasted_iota(jnp.int32, (e, e), 0)
    ci = lax.broadcasted_iota(jnp.int32, (e, e), 1)
    tril = jnp.where(ci <= ri, 1.0, 0.0).astype(BF16)
    nh, nl = _split2(nblk)
    return nblk, _dot(tril, nh) + _dot(tril, nl)


def _dest_kernel(cnt_ref, eid_ref, rank_ref, dest_ref):
    nblk, end = _block_offsets(cnt_ref[...])
    off = (end - nblk) * float(BLK)
    eid = eid_ref[...]
    tt = eid.shape[1]
    row = lax.broadcasted_iota(jnp.int32, (off.shape[0], tt), 0)
    table = jnp.concatenate([off] * (tt // LANES), axis=1)
    dest_ref[...] = _lookup(row, eid, table).astype(jnp.int32) + rank_ref[...]


def _destinations(cnt, eid, rank, tt=512):
    n = eid.shape[1]
    tt = min(tt, n)
    blk = pl.BlockSpec((TOP_K, tt), lambda i: (0, i))
    return pl.pallas_call(
        _dest_kernel,
        grid=(n // tt,),
        in_specs=[pl.BlockSpec(cnt.shape, lambda i: (0, 0)), blk, blk],
        out_specs=blk,
        out_shape=jax.ShapeDtypeStruct((TOP_K, n), jnp.int32),
        compiler_params=_params("arbitrary"),
        name="expert_destinations",
    )(cnt, eid, rank)


def _meta_kernel(cnt_ref, meta_ref, *, nbp):
    cnt = cnt_ref[...]
    e = cnt.shape[0]
    nblk, end = _block_offsets(cnt)
    rep = lambda a, w: jnp.concatenate([a] * (w // LANES), axis=1)
    b = lax.broadcasted_iota(jnp.int32, (e, nbp), 1).astype(F32)
    blk_e = jnp.minimum(jnp.sum(jnp.where(rep(end, nbp) <= b, 1.0, 0.0), axis=0, keepdims=True), float(e - 1))
    row = lax.broadcasted_iota(jnp.int32, (e, nbp), 0).astype(F32)
    mine = row == blk_e
    left = rep(cnt + (end - nblk) * float(BLK), nbp) - b * float(BLK)
    nvalid = jnp.clip(jnp.sum(jnp.where(mine, left, 0.0), axis=0, keepdims=True), 0.0, float(BLK))
    nused = jnp.max(rep(end, nbp), axis=0, keepdims=True)
    later = jnp.logical_and(row > blk_e, rep(nblk, nbp) > 0.0)
    nxt = jnp.min(jnp.where(later, row, float(e)), axis=0, keepdims=True)
    nxt = jnp.where(nxt >= float(e), -1.0, nxt)
    meta_ref[...] = jnp.concatenate([blk_e, nvalid, nused, nxt, jnp.zeros((4, nbp), F32)],
                                    axis=0).astype(jnp.int32)


def _block_meta(cnt, nb):
    nbp = -(-nb // LANES) * LANES
    return pl.pallas_call(
        functools.partial(_meta_kernel, nbp=nbp),
        out_shape=jax.ShapeDtypeStruct((8, nbp), jnp.int32),
        compiler_params=pltpu.CompilerParams(vmem_limit_bytes=VMEM_LIMIT),
        name="expert_block_meta",
    )(cnt)


SC_WINDOW = 128


def _sc_mesh():
    return plsc.VectorSubcoreMesh(core_axis_name="core", subcore_axis_name="subcore")


def _sc_scatter_rows(rows, idx, nrows):
    n, width = rows.shape

    @pl.kernel(out_type=jax.ShapeDtypeStruct((nrows, width), rows.dtype), mesh=_sc_mesh())
    def scatter(rows_hbm, idx_hbm, out_hbm):
        def body(rows_vmem, idx_vmem):
            pltpu.sync_copy(rows_vmem, out_hbm.at[idx_vmem.at[0]])

        pltpu.emit_pipeline(
            body,
            grid=(n // SC_WINDOW, idx.shape[0]),
            in_specs=[pl.BlockSpec((SC_WINDOW, width), index_map=lambda i, k: (i, 0)),
                      pl.BlockSpec((1, SC_WINDOW), index_map=lambda i, k: (k, i))],
            out_specs=[],
            core_axis_name=("core", "subcore"),
            dimension_semantics=(pltpu.PARALLEL, pltpu.ARBITRARY),
        )(rows_hbm, idx_hbm)

    return scatter(rows, idx)


def _sc_gather_rows(src, idx):
    num = idx.shape[1]
    width = src.shape[1]

    @pl.kernel(out_type=jax.ShapeDtypeStruct((num, width), src.dtype), mesh=_sc_mesh())
    def gather(src_hbm, idx_hbm, out_hbm):
        def body(idx_vmem, out_vmem):
            pltpu.sync_copy(src_hbm.at[idx_vmem.at[0]], out_vmem)

        pltpu.emit_pipeline(
            body,
            grid=(num // SC_WINDOW,),
            in_specs=[pl.BlockSpec((1, SC_WINDOW), index_map=lambda i: (0, i))],
            out_specs=[pl.BlockSpec((SC_WINDOW, width), index_map=lambda i: (i, 0))],
            core_axis_name=("core", "subcore"),
            dimension_semantics=(pltpu.PARALLEL,),
        )(idx_hbm, out_hbm)

    return gather(src, idx)


BLOCKS_PER_STEP = 2


def _experts_kernel(be_ref, nv_ref, nxt_ref, nu_ref, xa_ref, xb_ref, wg_hbm, wu_hbm, wd_hbm, oa_ref, ob_ref,
                    wgf, wuf, wdf, wgb, wub, wdb, sems, slot_ref):
    step = pl.program_id(0)

    def fetch(expert, slot):
        return [pltpu.make_async_copy(src.at[expert], dst.at[slot], sems.at[slot])
                for src, dst in ((wg_hbm, wgf), (wu_hbm, wuf), (wd_hbm, wdf))]

    @pl.when(step == 0)
    def _():
        slot_ref[0] = 0
        for cp in fetch(be_ref[0], 0):
            cp.start()

    def one_block(b, rows):
        prev = be_ref[jnp.maximum(b - 1, 0)]

        @pl.when(jnp.logical_or(b == 0, be_ref[b] != prev))
        def _():
            slot = slot_ref[0]
            for cp in fetch(be_ref[b], slot):
                cp.wait()

            @pl.when(nxt_ref[b] >= 0)
            def _():
                for cp in fetch(nxt_ref[b], 1 - slot):
                    cp.start()

            wgb[...] = wgf[slot].astype(BF16)
            wub[...] = wuf[slot].astype(BF16)
            wdb[...] = wdf[slot].astype(BF16)
            slot_ref[0] = 1 - slot

        valid = lax.broadcasted_iota(jnp.int32, (BLK, 1), 0) < nv_ref[b]
        zero = jnp.zeros((), U32)
        x = _unpack_rows(jnp.where(valid, xa_ref[rows, :], zero), jnp.where(valid, xb_ref[rows, :], zero))
        act = _silu(_dot(x, wgb[...])) * _dot(x, wub[...])
        oa_ref[rows, :], ob_ref[rows, :] = _pack_rows(_dot(act.astype(BF16), wdb[...]))

    for j in range(BLOCKS_PER_STEP):
        b = step * BLOCKS_PER_STEP + j
        rows = slice(j * BLK, (j + 1) * BLK)
        if j == 0:
            one_block(b, rows)
        else:
            pl.when(b < nu_ref[0])(functools.partial(one_block, b, rows))


def _experts(blk_e, nvalid, nused, nxt_e, xs_a, xs_b, wg, wu, wd):
    p, dq = xs_a.shape
    d, de = wg.shape[1], wg.shape[2]
    rows_in = pl.BlockSpec((BLOCKS_PER_STEP * BLK, dq), lambda b, be, nv, nx, nu: (b, 0))
    hbm = pl.BlockSpec(memory_space=pl.ANY)
    return pl.pallas_call(
        _experts_kernel,
        grid_spec=pltpu.PrefetchScalarGridSpec(
            num_scalar_prefetch=4,
            grid=((nused + BLOCKS_PER_STEP - 1) // BLOCKS_PER_STEP,),
            in_specs=[rows_in, rows_in, hbm, hbm, hbm],
            out_specs=[rows_in, rows_in],
            scratch_shapes=[pltpu.VMEM((2, d, de), F32), pltpu.VMEM((2, d, de), F32), pltpu.VMEM((2, de, d), F32),
                            pltpu.VMEM((d, de), BF16), pltpu.VMEM((d, de), BF16), pltpu.VMEM((de, d), BF16),
                            pltpu.SemaphoreType.DMA((2,)), pltpu.SMEM((1,), jnp.int32)],
        ),
        out_shape=[jax.ShapeDtypeStruct((p, dq), U32)] * 2,
        compiler_params=_params("arbitrary"),
        name="moe_experts",
    )(blk_e, nvalid, nxt_e, nused.reshape(1), xs_a, xs_b, wg, wu, wd)


def _shared_kernel(ha_ref, hb_ref, sg_ref, su_ref, sd_ref, o_ref):
    hb = _unpack_rows(ha_ref[...], hb_ref[...])
    act = _silu(_dot(hb, sg_ref[...])) * _dot(hb, su_ref[...])
    o_ref[...] = _dot(act.astype(BF16), sd_ref[...]).astype(o_ref.dtype)


def _shared_expert(h2a, h2b, sh_wg, sh_wu, sh_wd, tt=512):
    n, dp = h2a.shape
    d = sh_wg.shape[0]
    tt = min(tt, n)
    consts = [sh_wg.astype(BF16), sh_wu.astype(BF16), sh_wd.astype(BF16)]
    packed_rows = pl.BlockSpec((tt, dp), lambda i: (i, 0))
    return pl.pallas_call(
        _shared_kernel,
        grid=(n // tt,),
        in_specs=[packed_rows, packed_rows] + [pl.BlockSpec(a.shape, lambda i: (0, 0)) for a in consts],
        out_specs=pl.BlockSpec((tt, d), lambda i: (i, 0)),
        out_shape=jax.ShapeDtypeStruct((n, d), BF16),
        compiler_params=_params("arbitrary"),
        name="shared_expert",
    )(h2a, h2b, *consts)


def _combine_kernel(w_ref, x1_ref, sh_ref, mod_ref, ga_ref, gb_ref, gf_ref, sel_ref, o_ref):
    ffn = sh_ref[...].astype(F32)
    wh, wl = _split2(w_ref[...])
    acc = None
    for kk in range(TOP_K):
        sel = sel_ref[kk]
        wk = _tn(wh, sel) + _tn(wl, sel)
        a_lo, a_hi = _unpack_halves(ga_ref[kk])
        b_lo, b_hi = _unpack_halves(gb_ref[kk])
        parts = [a_lo * wk, b_lo * wk, a_hi * wk, b_hi * wk]
        acc = parts if acc is None else [p + q for p, q in zip(acc, parts)]
    ffn = ffn + jnp.concatenate(acc, axis=1)
    xo = x1_ref[...] + mod_ref[5:6, :] * ffn
    ms = jnp.mean(xo * xo, axis=-1, keepdims=True)
    o_ref[...] = xo * lax.rsqrt(ms + NORM_EPS) * gf_ref[...]


def _combine(wsel, x1, shared, mod, ga, gb, normf_g, tok0, seq, tt=256):
    n, d = x1.shape
    part = ga.shape[1]
    tt = min(tt, seq, part)
    per = seq // tt
    off = tok0 // tt
    dq = ga.shape[2]
    sel = jnp.asarray(np.broadcast_to(np.eye(TOP_K)[:, :, None], (TOP_K, TOP_K, dq)), BF16)
    consts = [normf_g.reshape(1, d), sel]
    const = lambda a: pl.BlockSpec(a.shape, lambda i: (0,) * a.ndim)
    rows = pl.BlockSpec((tt, d), lambda i: (off + i, 0))
    gathered = pl.BlockSpec((TOP_K, tt, dq), lambda i: (0, i, 0))
    return pl.pallas_call(
        _combine_kernel,
        grid=(part // tt,),
        in_specs=[pl.BlockSpec((TOP_K, tt), lambda i: (0, off + i)),
                  rows, rows,
                  pl.BlockSpec((None,) + mod.shape[1:], lambda i: ((off + i) // per, 0, 0)),
                  gathered, gathered] + [const(a) for a in consts],
        out_specs=rows,
        out_shape=jax.ShapeDtypeStruct((n, d), F32),
        input_output_aliases={1: 0},
        compiler_params=_params("arbitrary"),
        name="moe_combine",
    )(wsel, x1, shared, mod, ga, gb, *consts)


COMBINE_PARTS = 4


def _moe(x1, h2a, h2b, mod, eid, wsel, exp_wg, exp_wu, exp_wd, sh_wg, sh_wu, sh_wd, normf_g):
    n = h2a.shape[0]
    e = exp_wg.shape[0]
    nb = (n * TOP_K + e * (BLK - 1)) // BLK
    nb = -(-nb // BLOCKS_PER_STEP) * BLOCKS_PER_STEP
    rank, cnt = _expert_ranks(eid, e)
    dest = _destinations(cnt, eid, rank)
    meta = _block_meta(cnt, nb)
    xs_a = _sc_scatter_rows(h2a, dest, nb * BLK)
    xs_b = _sc_scatter_rows(h2b, dest, nb * BLK)
    shared = _shared_expert(h2a, h2b, sh_wg, sh_wu, sh_wd)
    ys_a, ys_b = _experts(meta[0, :nb], meta[1, :nb], meta[2, 0], meta[3, :nb], xs_a, xs_b,
                          exp_wg, exp_wu, exp_wd)
    bsz, seq, d = x1.shape
    out = x1.reshape(n, d)
    part = n // COMBINE_PARTS
    for j in range(COMBINE_PARTS):
        idx = dest[:, j * part:(j + 1) * part].reshape(1, TOP_K * part)
        ga = _sc_gather_rows(ys_a, idx).reshape(TOP_K, part, -1)
        gb = _sc_gather_rows(ys_b, idx).reshape(TOP_K, part, -1)
        out = _combine(wsel, out, shared, mod, ga, gb, normf_g, j * part, seq)
    return out


def kernel(x, c, norm1_g, norm2_g, normf_g, w_ada, b_ada, w_in, w_out, hy_conv_w, hy_conv_b, hy_pos_w1, hy_pos_b1, hy_pos_w2, hy_pos_b2, hy_pos_w3, hy_sin_freq, hy_skip, rw_mu, rw_w0, rw_w_up, rw_a0, rw_a_up, rw_g_up, rw_k_k, rw_k_a, rw_r_k, rw_ln_w, rw_ln_b, router_w, router_bias, exp_w_gate, exp_w_up, exp_w_down, sh_w_gate, sh_w_up, sh_w_down):
    bsz, seq, d = x.shape
    depth = w_ada.shape[0]
    assert depth == 1, "the final norm is fused into the last kernel of a single layer"
    for l in range(depth):
        mod = _modulation(c, w_ada[l], b_ada[l]).reshape(bsz, -1, d)
        uhy, rkvk, lwa, g, bonus = _projection(
            x, mod, norm1_g[l], w_in[l], hy_conv_w[l], hy_conv_b[l], rw_mu[l], rw_w0[l], rw_w_up[l],
            rw_a0[l], rw_a_up[l], rw_g_up[l], rw_k_k[l], rw_k_a[l], rw_r_k[l])
        k2, ss = _hyena_filters(seq, hy_pos_w1[l], hy_pos_b1[l], hy_pos_w2[l], hy_pos_b2[l],
                                hy_pos_w3[l], hy_sin_freq[l])
        khat = _filter_spectrum(k2, ss, seq)
        z, z_col = uhy, 0
        for order in range(HYENA_ORDER):
            z = _long_conv_gate(z, z_col, uhy, (order + 1) * D_HYENA, khat, hy_skip[l], order)
            z_col = 0
        o_f, o_b = _wkv(rkvk, lwa, rw_k_a[l])
        x1, h2a, h2b, eid, wsel = _mix_out(x, mod, z, o_f, o_b, g, bonus, rw_ln_w[l], rw_ln_b[l], w_out[l],
                                           norm2_g[l], router_w[l], router_bias[l])
        x = _moe(x1, h2a, h2b, mod, eid, wsel, exp_w_gate[l], exp_w_up[l], exp_w_down[l],
                 sh_w_gate[l], sh_w_up[l], sh_w_down[l], normf_g)
        x = x.reshape(bsz, seq, d)
    return x
```

```python
import functools
import math

import jax
import jax.numpy as jnp
import numpy as np
from jax import lax
from jax.experimental import pallas as pl
from jax.experimental.pallas import tpu as pltpu
from jax.experimental.pallas import tpu_sc as plsc

F32 = jnp.float32
BF16 = jnp.bfloat16

LANES = 128
MXU_DIM = 256
VMEM_LIMIT = 56 * 1024 * 1024

D_HYENA = 512
D_RWKV = 512
HEAD = 64
HYENA_ORDER = 2
FILTER_BANDS = 16
DECAY_TARGET = 1e-2
FAST_DECAY_PCT = 0.3
SLOW_DECAY_PCT = 1.5
FILTER_NORM_EPS = 1e-6
DECAY_LORA = 32
ICLR_LORA = 32
GATE_LORA = 96
GN_EPS = 64e-5
NORM_EPS = 1e-6
TOP_K = 8
N_GROUPS = 8
TOPK_GROUPS = 4
ROUTE_SCALE = 2.5


def _params(*sem):
    return pltpu.CompilerParams(dimension_semantics=sem, vmem_limit_bytes=VMEM_LIMIT)


def _split2(a):
    hi = a.astype(BF16)
    lo = (a - hi.astype(F32)).astype(BF16)
    return hi, lo


def _dot(a, b):
    return jnp.dot(a, b, preferred_element_type=F32)


def _dot3(a, b):
    ah, al = _split2(a)
    bh, bl = _split2(b)
    return _dot(ah, bh) + (_dot(ah, bl) + _dot(al, bh))


def _dot_exact_rhs(a, b_bf16):
    ah, al = _split2(a)
    return _dot(ah, b_bf16) + _dot(al, b_bf16)


def _silu(x):
    return x * jax.nn.sigmoid(x)


U32 = jnp.int32


def _pack_halves(x):
    w = x.shape[1] // 2
    return pltpu.pack_elementwise([x[:, :w], x[:, w:]], packed_dtype=BF16)


def _unpack_halves(p):
    lo = pltpu.unpack_elementwise(p, index=0, packed_dtype=BF16, unpacked_dtype=F32)
    hi = pltpu.unpack_elementwise(p, index=1, packed_dtype=BF16, unpacked_dtype=F32)
    return lo, hi


def _pack_rows(x):
    packed = _pack_halves(x)
    half = packed.shape[1] // 2
    return packed[:, :half], packed[:, half:]


def _unpack_rows(a, b):
    a_lo, a_hi = _unpack_halves(a)
    b_lo, b_hi = _unpack_halves(b)
    return jnp.concatenate([a_lo.astype(BF16), b_lo.astype(BF16), a_hi.astype(BF16), b_hi.astype(BF16)], axis=1)


def _mod_kernel(c_ref, w_ref, b_ref, o_ref):
    o_ref[...] = _dot3(_silu(c_ref[...]), w_ref[...]) + b_ref[...]


def _modulation(c, w_ada, b_ada):
    bsz, d = c.shape
    n = w_ada.shape[1]
    blk = 1024
    return pl.pallas_call(
        _mod_kernel,
        grid=(n // blk,),
        in_specs=[
            pl.BlockSpec((bsz, d), lambda j: (0, 0)),
            pl.BlockSpec((d, blk), lambda j: (0, j)),
            pl.BlockSpec((1, blk), lambda j: (0, j)),
        ],
        out_specs=pl.BlockSpec((bsz, blk), lambda j: (0, j)),
        out_shape=jax.ShapeDtypeStruct((bsz, n), F32),
        compiler_params=_params("arbitrary"),
        name="adaln_mod",
    )(c, w_ada, b_ada.reshape(1, n))


def _filter_kernel(band_ref, w1_ref, b1_ref, w2_ref, b2_ref, w3_ref, freq_ref, delta_ref,
                   k_ref, ss_ref, *, seq, rows):
    half = pl.program_id(0)
    i = pl.program_id(1)
    r = lax.broadcasted_iota(jnp.int32, (rows, LANES), 0) + i * rows
    pos = jnp.where(half == 0, r, seq - r).astype(F32)
    tt = pos / float(max(seq - 1, 1))
    lane = lax.broadcasted_iota(jnp.int32, (rows, LANES), 1)
    feats = jnp.where(lane == 0, tt, jnp.sin(pos * band_ref[0:1, :] + band_ref[1:2, :]))
    freq = freq_ref[...]
    hdn = jnp.sin(freq * (_dot3(feats, w1_ref[...]) + b1_ref[...]))
    for j in range(w2_ref.shape[0]):
        hdn = jnp.sin(freq * (_dot3(hdn, w2_ref[j]) + b2_ref[j]))
    filt = _dot3(hdn, w3_ref[...])
    filt = filt * jnp.exp(-tt[:, :1] * delta_ref[...])
    valid = jnp.logical_or(half == 0, r[:, :1] > 0)
    filt = jnp.where(valid, filt, 0.0)
    k_ref[...] = filt

    @pl.when(jnp.logical_and(half == 0, i == 0))
    def _():
        ss_ref[...] = jnp.zeros_like(ss_ref)

    ss_ref[...] += jnp.broadcast_to(jnp.sum(filt * filt, axis=0, keepdims=True), ss_ref.shape)


def _hyena_filters(seq, pw1, pb1, pw2, pb2, pw3, freq):
    width = pw1.shape[1]
    ncol = HYENA_ORDER * D_HYENA
    rows = min(seq, 512)
    bands = np.zeros((2, LANES), np.float64)
    lin = np.linspace(1e-4, FILTER_BANDS - 1, FILTER_BANDS) * (2.0 * math.pi / seq)
    bands[0, 1:1 + FILTER_BANDS] = lin
    bands[1, 1:1 + FILTER_BANDS] = 0.5 * math.pi
    bands[0, 1 + FILTER_BANDS:1 + 2 * FILTER_BANDS] = -lin
    bands = jnp.asarray(bands, F32)
    deltas = np.abs(np.linspace(math.log(DECAY_TARGET) / SLOW_DECAY_PCT,
                                math.log(DECAY_TARGET) / FAST_DECAY_PCT, D_HYENA))
    deltas = jnp.asarray(np.tile(deltas, HYENA_ORDER)[None], F32)
    w1 = jnp.zeros((LANES, width), F32).at[:pw1.shape[0]].set(pw1)
    w3 = pw3.reshape(width, HYENA_ORDER, 2, D_HYENA).transpose(2, 0, 1, 3).reshape(2, width, ncol)
    nt = seq // rows
    full = lambda *shape: pl.BlockSpec(shape, lambda h, i: (0,) * len(shape))
    return pl.pallas_call(
        functools.partial(_filter_kernel, seq=seq, rows=rows),
        grid=(2, nt),
        in_specs=[
            full(2, LANES), full(LANES, width), full(1, width),
            full(pw2.shape[0], width, width), full(pw2.shape[0], 1, width),
            pl.BlockSpec((None, width, ncol), lambda h, i: (h, 0, 0)),
            full(1, width), full(1, ncol),
        ],
        out_specs=[
            pl.BlockSpec((rows, ncol), lambda h, i: (h * nt + i, 0)),
            pl.BlockSpec((8, ncol), lambda h, i: (0, 0)),
        ],
        out_shape=[jax.ShapeDtypeStruct((2 * seq, ncol), F32),
                   jax.ShapeDtypeStruct((8, ncol), F32)],
        compiler_params=_params("arbitrary", "arbitrary"),
        name="hyena_filters",
    )(bands, w1, pb1.reshape(1, width), pw2, pb2.reshape(pw2.shape[0], 1, width), w3,
      freq.reshape(1, width), deltas)


N1 = LANES
UNROLL = 8


def _dft_tables(seq):
    tables = _dft_tables_np(seq)
    return tuple(jnp.asarray(t, BF16) for t in tables[:5]) + tables[5:]


def _dft_tables_np(seq):
    m = 2 * seq
    n2 = m // N1
    n2h = n2 // 2
    n1 = np.arange(N1)[:, None, None]
    f2 = np.arange(n2)[None, :, None]
    k2 = np.arange(n2)[None, None, :]
    th = 2.0 * np.pi * (n1 * f2 / m + (k2 * f2 % n2) / n2)
    fwd_a = np.concatenate([np.cos(th), -np.sin(th)], axis=1)
    tht = np.transpose(th, (0, 2, 1))
    inv_a = np.concatenate([np.cos(tht), -np.sin(tht)], axis=2)[:, :n2h] / m
    a = np.arange(N1)
    ph = 2.0 * np.pi * np.outer(a, a) / N1
    c, s = np.cos(ph), np.sin(ph)
    fwd_b = np.block([[c, s], [-s, c]])
    inv_b = np.block([[c, -s], [s, c]])
    return fwd_a, fwd_a[:, :, :n2h], inv_a, fwd_b, inv_b, n2, n2h


def _stage_a_fwd(x_ref, wa_ref, y_ref, n2, scale=None):
    def body(i, carry):
        trips = [i * UNROLL + j for j in range(UNROLL)]
        xs = [x_ref[pl.ds(n1, wa_ref.shape[2], stride=N1), :] for n1 in trips]
        if scale is not None:
            xs = [x * scale for x in xs]
        prods = [_dot(wa_ref[n1], x.astype(BF16)) for n1, x in zip(trips, xs)]
        for n1, a in zip(trips, prods):
            y_ref[pl.ds(n1, n2, stride=2 * N1), :] = a[:n2]
            y_ref[pl.ds(N1 + n1, n2, stride=2 * N1), :] = a[n2:]
        return carry
    lax.fori_loop(0, N1 // UNROLL, body, 0)


def _filter_fft_kernel(k_ref, ss_ref, wa_ref, fb_ref, o_ref, y_ref, *, n2):
    scale = lax.rsqrt(ss_ref[0:1, :] + FILTER_NORM_EPS)
    _stage_a_fwd(k_ref, wa_ref, y_ref, n2, scale=scale)

    unr = min(UNROLL, n2)

    def body(i, carry):
        trips = [i * unr + j for j in range(unr)]
        ys = [y_ref[pl.ds(pl.multiple_of(f2 * 2 * N1, 2 * N1), 2 * N1), :].astype(BF16) for f2 in trips]
        for j in range(0, unr, 2):
            z = _dot(fb_ref[...], jnp.concatenate(ys[j:j + 2], axis=1))
            o_ref[trips[j]] = z[:, :LANES]
            o_ref[trips[j + 1]] = z[:, LANES:]
        return carry
    lax.fori_loop(0, n2 // unr, body, 0)


def _filter_spectrum(k2, ss, seq):
    fwd_a, _, _, fwd_b, _, n2, _ = _dft_tables(seq)
    ncol = k2.shape[1]
    nblk = ncol // LANES
    return pl.pallas_call(
        functools.partial(_filter_fft_kernel, n2=n2),
        grid=(nblk,),
        in_specs=[
            pl.BlockSpec((2 * seq, LANES), lambda c: (0, c)),
            pl.BlockSpec((8, LANES), lambda c: (0, c)),
            pl.BlockSpec(fwd_a.shape, lambda c: (0, 0, 0)),
            pl.BlockSpec(fwd_b.shape, lambda c: (0, 0)),
        ],
        out_specs=pl.BlockSpec((None, n2, 2 * N1, LANES), lambda c: (c, 0, 0, 0)),
        out_shape=jax.ShapeDtypeStruct((nblk, n2, 2 * N1, LANES), F32),
        scratch_shapes=[pltpu.VMEM((n2 * 2 * N1, LANES), F32)],
        compiler_params=_params("arbitrary"),
        name="hyena_filter_fft",
    )(k2, ss, fwd_a, fwd_b)


TILE = 8
N1_GROUPS = N1 // TILE


def _tile_tables(seq):
    _, fwd_a, inv_a, _, _, n2, n2h = _dft_tables_np(seq)
    eye = np.eye(TILE)
    fa = fwd_a.reshape(N1_GROUPS, TILE, 2 * n2, n2h)
    wa = np.einsum("qjrn,jk->qrjnk", fa, eye).reshape(N1_GROUPS, 2 * n2 * TILE, n2h * TILE)
    ia = inv_a.reshape(N1_GROUPS, TILE, n2h, 2 * n2)
    vc = np.einsum("qjnr,jk->qnjrk", ia, eye).reshape(N1_GROUPS, n2h * TILE, 2 * n2 * TILE)
    return jnp.asarray(wa, BF16), jnp.asarray(vc, BF16)


def _conv_kernel(u_ref, g_ref, skip_ref, fb_ref, ib_ref, kh_hbm, wa_hbm, vc_hbm, o_ref,
                 y_ref, kh_ref, wa_ref, vc_ref, sem, *, n2, n2h, kh_first):
    c_id, b_id = pl.program_id(0), pl.program_id(1)

    @pl.when(jnp.logical_and(c_id == 0, b_id == 0))
    def _():
        for src, dst in ((wa_hbm, wa_ref), (vc_hbm, vc_ref)):
            cp = pltpu.make_async_copy(src, dst, sem)
            cp.start()
            cp.wait()

    @pl.when(b_id == 0)
    def _():
        cp = pltpu.make_async_copy(kh_hbm.at[kh_first + c_id], kh_ref, sem)
        cp.start()
        cp.wait()

    def y_tile(rf, base):
        ri, f2 = divmod(rf, n2)
        return pl.ds(f2 * 2 * N1 + ri * N1 + base, TILE)

    def stage_a(q, carry):
        base = pl.multiple_of(q * TILE, TILE)
        x = jnp.concatenate([u_ref[pl.ds(N1 * m + base, TILE), :] for m in range(n2h)], axis=0)
        r = _dot(wa_ref[q], x.astype(BF16))
        for rf in range(2 * n2):
            y_ref[y_tile(rf, base), :] = r[rf * TILE:(rf + 1) * TILE]
        return carry
    lax.fori_loop(0, N1_GROUPS, stage_a, 0, unroll=8)

    unr = min(2 * UNROLL, n2)

    def mid(i, carry):
        trips = [i * unr + j for j in range(unr)]
        offs = [pl.multiple_of(f2 * 2 * N1, 2 * N1) for f2 in trips]
        wide = lambda blocks: [jnp.concatenate(blocks[j:j + 2], axis=1) for j in range(0, len(blocks), 2)]
        ys = wide([y_ref[pl.ds(off, 2 * N1), :].astype(BF16) for off in offs])
        khs = wide([kh_ref[f2] for f2 in trips])
        zs = [_dot(fb_ref[...], y) for y in ys]
        ps = []
        for z, kh in zip(zs, khs):
            zr, zi = z[:N1], z[N1:]
            kr, ki = kh[:N1], kh[N1:]
            ps.append(jnp.concatenate([zr * kr - zi * ki, zr * ki + zi * kr], axis=0).astype(BF16))
        gs = [_dot(ib_ref[...], p) for p in ps]
        for j, g in enumerate(gs):
            y_ref[pl.ds(offs[2 * j], 2 * N1), :] = g[:, :LANES]
            y_ref[pl.ds(offs[2 * j + 1], 2 * N1), :] = g[:, LANES:]
        return carry
    lax.fori_loop(0, n2 // unr, mid, 0)

    skip = skip_ref[...]

    def stage_c(q, carry):
        base = pl.multiple_of(q * TILE, TILE)
        g = jnp.concatenate([y_ref[y_tile(rf, base), :] for rf in range(2 * n2)], axis=0)
        conv = _dot(vc_ref[q], g.astype(BF16))
        for m in range(n2h):
            rows = pl.ds(N1 * m + base, TILE)
            o_ref[rows, :] = g_ref[rows, :] * (conv[m * TILE:(m + 1) * TILE] + u_ref[rows, :] * skip)
        return carry
    lax.fori_loop(0, N1_GROUPS, stage_c, 0, unroll=8)


def _long_conv_gate(u, u_col, gate, gate_col, khat, skip, order):
    bsz, seq, _ = u.shape
    ch = D_HYENA
    _, _, _, fwd_b, inv_b, n2, n2h = _dft_tables(seq)
    wa, vc = _tile_tables(seq)
    nblk = ch // LANES
    const = lambda a: pl.BlockSpec(a.shape, lambda c, b: (0,) * a.ndim)
    at = lambda col: pl.BlockSpec((None, seq, LANES), lambda c, b: (b, 0, col // LANES + c))
    hbm = pl.BlockSpec(memory_space=pl.ANY)
    return pl.pallas_call(
        functools.partial(_conv_kernel, n2=n2, n2h=n2h, kh_first=order * nblk),
        grid=(nblk, bsz),
        in_specs=[
            at(u_col), at(gate_col),
            pl.BlockSpec((1, LANES), lambda c, b: (0, c)),
            const(fwd_b), const(inv_b), hbm, hbm, hbm,
        ],
        out_specs=at(0),
        out_shape=jax.ShapeDtypeStruct((bsz, seq, ch), F32),
        scratch_shapes=[pltpu.VMEM((n2 * 2 * N1, LANES), F32), pltpu.VMEM(khat.shape[1:], F32),
                        pltpu.VMEM(wa.shape, BF16), pltpu.VMEM(vc.shape, BF16), pltpu.SemaphoreType.DMA(())],
        compiler_params=_params("arbitrary", "arbitrary"),
        name=f"hyena_conv{order}",
    )(u, gate, skip[order].reshape(1, ch), fwd_b, inv_b, khat, wa, vc)


HALO = 8


def _shift_rows(p, k):
    return pltpu.roll(p, k % p.shape[0], axis=0)


def _proj_kernel(xp_ref, x_ref, xn_ref, mod_ref, g1_ref, why_ref, wrkv_ref, wlora_ref,
                 cw_ref, cb_ref, murkv_ref, mulora_ref, w0_ref, a0_ref, wwah_ref, wwal_ref, gup_ref,
                 kk_ref, ka_ref, rk_ref, ones_ref,
                 uhy_ref, rkvk_ref, lwa_ref, g_ref, bonus_ref,
                 *, tt, nt):
    i = pl.program_id(1)
    xe = jnp.concatenate([xp_ref[...], x_ref[...], xn_ref[...]], axis=0)
    ms = jnp.mean(xe * xe, axis=-1, keepdims=True)
    h = xe * lax.rsqrt(ms + NORM_EPS) * g1_ref[...]
    h = h * (1.0 + mod_ref[1:2, :]) + mod_ref[0:1, :]
    row = lax.broadcasted_iota(jnp.int32, (tt + 2 * HALO, 1), 0)
    inside = jnp.logical_and(jnp.logical_or(row >= HALO, i > 0),
                             jnp.logical_or(row < tt + HALO, i < nt - 1))
    hb = jnp.where(inside, h, 0.0).astype(BF16)
    mid = slice(HALO, tt + HALO)

    p = _dot(hb, why_ref[...])
    u = (_shift_rows(p, 1) * cw_ref[0:1, :] + p * cw_ref[1:2, :]
         + _shift_rows(p, -1) * cw_ref[2:3, :] + cb_ref[...])
    uhy_ref[...] = u[mid]

    p = _dot(hb, wrkv_ref[...])
    p = p + murkv_ref[...] * (0.5 * (_shift_rows(p, 1) + _shift_rows(p, -1)) - p)
    p = p[mid]
    c = D_RWKV
    r, k, v = p[:, :c], p[:, c:2 * c], p[:, 2 * c:]
    rkvk_ref[:, :3 * c] = p

    q = _dot(hb, wlora_ref[...])
    q = q + mulora_ref[...] * (0.5 * (_shift_rows(q, 1) + _shift_rows(q, -1)) - q)
    q = q[mid]
    wa = q[:, :LANES]
    lane = lax.broadcasted_iota(jnp.int32, wa.shape, 1)
    wa = jnp.where(lane < 2 * DECAY_LORA, jnp.tanh(wa), wa)
    wah, wal = _split2(wa)
    up = _dot(wah, wwah_ref[...]) + (_dot(wah, wwal_ref[...]) + _dot(wal, wwah_ref[...]))
    lw = -math.exp(-0.5) * jax.nn.sigmoid(w0_ref[...] + up[:, :2 * c])
    a = jax.nn.sigmoid(a0_ref[...] + up[:, 2 * c:])
    for dd in range(2):
        lwa_ref[:, 2 * dd * c:(2 * dd + 1) * c] = lw[:, dd * c:(dd + 1) * c]
        lwa_ref[:, (2 * dd + 1) * c:(2 * dd + 2) * c] = a[:, dd * c:(dd + 1) * c]
    g_ref[...] = _dot3(jax.nn.sigmoid(q[:, LANES:]), gup_ref[...])

    ones = ones_ref[...]
    kk = k * kk_ref[...]
    nrm = jnp.sqrt(_dot_exact_rhs(kk * kk, ones))
    rkvk_ref[:, 3 * c:] = kk / jnp.maximum(nrm, 1e-12)
    ka = ka_ref[...]
    ksum = k * (2.0 + (a[:, :c] + a[:, c:] - 2.0) * ka)
    bonus_ref[...] = _dot_exact_rhs(r * ksum * rk_ref[...], ones) * v


def _head_ones():
    hid = np.arange(D_RWKV) // HEAD
    return jnp.asarray(hid[:, None] == hid[None, :], BF16)


def _projection(x, mod, norm1_g, w_in, hy_conv_w, hy_conv_b, rw_mu, rw_w0, rw_w_up, rw_a0,
                rw_a_up, rw_g_up, rw_k_k, rw_k_a, rw_r_k, tt=512):
    bsz, seq, d = x.shape
    tt = min(tt, seq)
    nt = seq // tt
    c = D_RWKV
    hy = (HYENA_ORDER + 1) * D_HYENA
    nlora = 2 * LANES
    w_hy = w_in[:, :hy].astype(BF16)
    w_rkv = w_in[:, hy:hy + 3 * c].astype(BF16)
    w_lora = jnp.zeros((d, nlora), F32).at[:, :w_in.shape[1] - hy - 3 * c].set(w_in[:, hy + 3 * c:]).astype(BF16)
    mu_rkv = rw_mu[:3 * c].reshape(1, 3 * c)
    mu_lora = jnp.zeros((1, nlora), F32).at[0, :rw_mu.shape[0] - 3 * c].set(rw_mu[3 * c:])
    wwa = jnp.zeros((LANES, 4 * c), F32)
    for dd in range(2):
        wwa = wwa.at[dd * DECAY_LORA:(dd + 1) * DECAY_LORA, dd * c:(dd + 1) * c].set(rw_w_up[dd])
        wwa = wwa.at[2 * DECAY_LORA + dd * ICLR_LORA:2 * DECAY_LORA + (dd + 1) * ICLR_LORA,
                     2 * c + dd * c:2 * c + (dd + 1) * c].set(rw_a_up[dd])
    gup = jnp.zeros((LANES, c), F32).at[:GATE_LORA].set(rw_g_up)
    row = lambda a: a.reshape(1, -1)

    nb8 = seq // HALO
    tb = tt // HALO
    const = lambda a: pl.BlockSpec(a.shape, lambda b, i: (0,) * a.ndim, pipeline_mode=pl.Buffered(1))
    tile = lambda w: pl.BlockSpec((None, tt, w), lambda b, i: (b, i, 0))
    ins = [
        (x, pl.BlockSpec((None, HALO, d), lambda b, i: (b, jnp.maximum(i * tb - 1, 0), 0))),
        (x, pl.BlockSpec((None, tt, d), lambda b, i: (b, i, 0))),
        (x, pl.BlockSpec((None, HALO, d), lambda b, i: (b, jnp.minimum((i + 1) * tb, nb8 - 1), 0))),
        (mod, pl.BlockSpec((None,) + mod.shape[1:], lambda b, i: (b, 0, 0))),
    ]
    consts = [row(norm1_g), w_hy, w_rkv, w_lora, hy_conv_w, row(hy_conv_b), mu_rkv, mu_lora,
              row(rw_w0), row(rw_a0), *_split2(wwa), gup, row(rw_k_k), row(rw_k_a), row(rw_r_k), _head_ones()]
    ins += [(a, const(a)) for a in consts]
    widths = [hy, 4 * c, 4 * c, c, c]
    return pl.pallas_call(
        functools.partial(_proj_kernel, tt=tt, nt=nt),
        grid=(bsz, nt),
        in_specs=[s for _, s in ins],
        out_specs=[tile(w) for w in widths],
        out_shape=[jax.ShapeDtypeStruct((bsz, seq, w), F32) for w in widths],
        compiler_params=_params("arbitrary", "arbitrary"),
        name="input_projection",
    )(*[a for a, _ in ins])


CHUNK = HEAD
GROUP = MXU_DIM // HEAD


def _nt(a, b):
    return lax.dot_general(a, b, (((1,), (1,)), ((), ())), preferred_element_type=F32)


def _tn(a, b):
    return lax.dot_general(a, b, (((0,), (0,)), ((), ())), preferred_element_type=F32)


def _wkv_direction(r, k, v, kk, lw, a, ka, s_ref, reverse):
    c = CHUNK
    ti = lax.broadcasted_iota(jnp.int32, (c, c), 0)
    si = lax.broadcasted_iota(jnp.int32, (c, c), 1)
    tri = (si >= ti) if reverse else (si <= ti)
    cum = _dot_exact_lhs(jnp.where(tri, 1.0, 0.0).astype(BF16), lw)
    tot = jnp.sum(lw, axis=0, keepdims=True)
    w_incl = jnp.exp(cum)
    w_prev = jnp.exp(cum - lw)
    w_inv = jnp.exp(-cum)
    w_end = jnp.exp(tot - cum)
    w_tot = jnp.exp(tot)
    kd = k * (1.0 + (a - 1.0) * ka)
    b = kk * a
    a_w = -kk * w_prev
    r_w = r * w_incl
    b_w = b * w_inv
    k_w = kd * w_inv
    b_e = b * w_end
    k_e = kd * w_end

    m = MXU_DIM
    ri = lax.broadcasted_iota(jnp.int32, (m, m), 0)
    ci = lax.broadcasted_iota(jnp.int32, (m, m), 1)
    head_mask = (ri // HEAD) == (ci // HEAD)
    tl = lax.broadcasted_iota(jnp.int32, (c, m), 0)
    sl = lax.broadcasted_iota(jnp.int32, (c, m), 1) % c
    strict = (sl > tl) if reverse else (sl < tl)
    incl = (sl >= tl) if reverse else (sl <= tl)
    eye = jnp.where(sl == tl, 1.0, 0.0)
    both = lambda top, bot: jnp.concatenate([top, bot], axis=0)

    def stack(xg):
        xb = xg.astype(BF16)
        return jnp.where(head_mask, jnp.concatenate([xb] * GROUP, axis=0), jnp.zeros((), BF16))

    streams = []
    for g in range(D_RWKV // m):
        sl_g = slice(g * m, (g + 1) * m)
        streams.append(dict(
            ar=both(a_w[:, sl_g], r_w[:, sl_g]).astype(BF16),
            b_st=stack(b_w[:, sl_g]), k_st=stack(k_w[:, sl_g]), v_st=stack(v[:, sl_g]),
            v=v[:, sl_g], bk=both(b_e[:, sl_g], k_e[:, sl_g]).astype(BF16),
            w_tot=w_tot[:, sl_g], s_ref=s_ref.at[g],
            strict=strict, incl=incl, eye=eye, head_mask=head_mask, stack=stack))
    return streams


def _wkv_streams_step(streams):
    c = CHUNK
    both = lambda top, bot: jnp.concatenate([top, bot], axis=0)
    for st in streams:
        st["s"] = st["s_ref"][...]
        st["xb"] = _nt(st["ar"], st["b_st"])
        st["xk"] = _nt(st["ar"], st["k_st"])
        st["xs"] = _nt(st["ar"], st["s"].astype(BF16))
    for st in streams:
        m_k = both(jnp.where(st["strict"], st["xk"][:c], 0.0), jnp.where(st["incl"], st["xk"][c:], 0.0))
        st["kv"] = _dot(m_k.astype(BF16), st["v_st"])
        st["rhs"] = st["xs"][:c] + st["kv"][:c]
        st["pw"] = jnp.where(st["strict"], st["xb"][:c], 0.0)
        st["t"] = st["eye"] + st["pw"]
        st["p_st"] = st["stack"](st["pw"])
    levels = int(math.log2(c)) - 1
    for st in streams:
        st["pw"] = _dot(st["pw"].astype(BF16), st["p_st"])
        st["p_st"] = st["stack"](st["pw"])
    for lvl in range(1, levels + 1):
        for st in streams:
            if lvl < levels:
                prod = _dot(both(st["pw"], st["t"]).astype(BF16), st["p_st"])
                st["pw"] = prod[:c]
                st["t"] = st["t"] + prod[c:]
                st["p_st"] = st["stack"](st["pw"])
            else:
                st["t"] = st["t"] + _dot(st["t"].astype(BF16), st["p_st"])
    for st in streams:
        st["u"] = _dot(st["t"].astype(BF16), st["stack"](st["rhs"]))
    outs = []
    for st in streams:
        m_rb = jnp.where(st["incl"], st["xb"][c:], 0.0)
        outs.append(st["xs"][c:] + _dot(m_rb.astype(BF16), st["stack"](st["u"])) + st["kv"][c:])
        uv = both(st["u"], st["v"]).astype(BF16)
        st["s_ref"][...] = st["s"] * st["w_tot"] + jnp.where(st["head_mask"], _tn(uv, st["bk"]), 0.0)
    return outs


def _dot_exact_lhs(tri_bf16, x):
    xh, xl = _split2(x)
    return _dot(tri_bf16, xh) + _dot(tri_bf16, xl)


def _wkv_kernel(rkvk_f, lwa_f, rkvk_b, lwa_b, ka_ref, of_ref, ob_ref, s_ref, *, nch):
    @pl.when(pl.program_id(1) == 0)
    def _():
        s_ref[...] = jnp.zeros_like(s_ref)

    ka = ka_ref[...]
    c = D_RWKV

    def operands(rkvk_ref, lwa_ref, rows):
        x = rkvk_ref[rows, :]
        la = lwa_ref[rows, :]
        return x[:, :c], x[:, c:2 * c], x[:, 2 * c:3 * c], x[:, 3 * c:], la[:, :c], la[:, c:]

    for ci in range(nch):
        rows_f = slice(ci * CHUNK, (ci + 1) * CHUNK)
        rows_b = slice((nch - 1 - ci) * CHUNK, (nch - ci) * CHUNK)
        fwd = _wkv_direction(*operands(rkvk_f, lwa_f, rows_f), ka, s_ref.at[0], False)
        bwd = _wkv_direction(*operands(rkvk_b, lwa_b, rows_b), ka, s_ref.at[1], True)
        outs = _wkv_streams_step(fwd + bwd)
        of_ref[rows_f, :] = jnp.concatenate(outs[:len(fwd)], axis=1)
        ob_ref[rows_b, :] = jnp.concatenate(outs[len(fwd):], axis=1)


WKV_CHUNKS_PER_STEP = 8


def _wkv(rkvk, lwa, rw_k_a):
    bsz, seq, _ = rkvk.shape
    c = D_RWKV
    nch = WKV_CHUNKS_PER_STEP if seq % (WKV_CHUNKS_PER_STEP * CHUNK) == 0 else 1
    rows = nch * CHUNK
    nb = seq // rows
    fwd = lambda w, lane_blk: pl.BlockSpec((None, rows, w), lambda b, j: (b, j, lane_blk))
    bwd = lambda w, lane_blk: pl.BlockSpec((None, rows, w), lambda b, j: (b, nb - 1 - j, lane_blk))
    return pl.pallas_call(
        functools.partial(_wkv_kernel, nch=nch),
        grid=(bsz, nb),
        in_specs=[fwd(4 * c, 0), fwd(2 * c, 0), bwd(4 * c, 0), bwd(2 * c, 1),
                  pl.BlockSpec((1, c), lambda b, j: (0, 0))],
        out_specs=[fwd(c, 0), bwd(c, 0)],
        out_shape=[jax.ShapeDtypeStruct((bsz, seq, c), F32)] * 2,
        scratch_shapes=[pltpu.VMEM((2, c // MXU_DIM, MXU_DIM, MXU_DIM), F32)],
        compiler_params=_params("arbitrary", "arbitrary"),
        name="wkv7_chunked",
    )(rkvk, lwa, rkvk, lwa, rw_k_a.reshape(1, c))


NEG_INF = float("-inf")


def _first_max(vals, idx, size):
    m = jnp.max(vals, axis=0, keepdims=True)
    i = jnp.min(jnp.where(vals == m, idx, size), axis=0, keepdims=True)
    return m, i


def _route(scores, biased):
    e, tt = scores.shape
    per = e // N_GROUPS
    rowl = lax.broadcasted_iota(jnp.int32, (per, tt), 0)
    gs = []
    for g in range(N_GROUPS):
        blk = biased[g * per:(g + 1) * per]
        m1, i1 = _first_max(blk, rowl, per)
        m2 = jnp.max(jnp.where(rowl == i1, NEG_INF, blk), axis=0, keepdims=True)
        gs.append(m1 + m2)
    cur = jnp.concatenate(gs, axis=0)
    growl = lax.broadcasted_iota(jnp.int32, (N_GROUPS, tt), 0)
    gsel = jnp.zeros((N_GROUPS, tt), F32)
    for _ in range(TOPK_GROUPS):
        _, ig = _first_max(cur, growl, N_GROUPS)
        hit = growl == ig
        gsel = jnp.where(hit, 1.0, gsel)
        cur = jnp.where(hit, NEG_INF, cur)
    emask = jnp.concatenate([jnp.broadcast_to(gsel[g:g + 1], (per, tt)) for g in range(N_GROUPS)], axis=0)
    masked = jnp.where(emask > 0.5, biased, NEG_INF)
    row = lax.broadcasted_iota(jnp.int32, (e, tt), 0)
    ids, ws = [], []
    for _ in range(TOP_K):
        _, ie = _first_max(masked, row, e)
        hit = row == ie
        ids.append(ie)
        ws.append(jnp.sum(jnp.where(hit, scores, 0.0), axis=0, keepdims=True))
        masked = jnp.where(hit, NEG_INF, masked)
    w = jnp.concatenate(ws, axis=0)
    w = w / jnp.sum(w, axis=0, keepdims=True) * ROUTE_SCALE
    return jnp.concatenate(ids, axis=0), w


def _mixout_kernel(x_ref, mod_ref, yhy_ref, of_ref, ob_ref, g_ref, bonus_ref, lnw_ref, lnb_ref,
                   ones_ref, wout_ref, g2n_ref, rwth_ref, rwtl_ref, bias_ref,
                   x1_ref, h2a_ref, h2b_ref, eid_ref, wsel_ref):
    ones = ones_ref[...]
    s = of_ref[...] + ob_ref[...]
    mean = _dot_exact_rhs(s, ones) * (1.0 / HEAD)
    dlt = s - mean
    var = _dot_exact_rhs(dlt * dlt, ones) * (1.0 / HEAD)
    sn = dlt * lax.rsqrt(var + GN_EPS) * lnw_ref[...] + lnb_ref[...]
    yrw = (sn + bonus_ref[...]) * g_ref[...]
    ch = yhy_ref.shape[-1]
    mix = _dot(yhy_ref[...].astype(BF16), wout_ref[:ch, :]) + _dot(yrw.astype(BF16), wout_ref[ch:, :])
    x1 = x_ref[...] + mod_ref[2:3, :] * mix
    x1_ref[...] = x1
    ms = jnp.mean(x1 * x1, axis=-1, keepdims=True)
    h2 = x1 * lax.rsqrt(ms + NORM_EPS) * g2n_ref[...]
    h2 = h2 * (1.0 + mod_ref[4:5, :]) + mod_ref[3:4, :]
    h2a_ref[...], h2b_ref[...] = _pack_rows(h2)
    rh, rl = rwth_ref[...], rwtl_ref[...]
    hh, hl = _split2(h2)
    logits = _nt(rh, hh) + (_nt(rh, hl) + _nt(rl, hh))
    scores = jax.nn.sigmoid(logits)
    ids, w = _route(scores, scores + bias_ref[...])
    eid_ref[...] = ids
    wsel_ref[...] = w


def _mix_out(x, mod, yhy, o_f, o_b, g, bonus, ln_w, ln_b, w_out, norm2_g, router_w, router_bias, tt=1024):
    bsz, seq, d = x.shape
    tt = min(tt, seq)
    nt = seq // tt
    n = bsz * seq
    c = D_RWKV
    e = router_w.shape[1]
    row = lambda a: a.reshape(1, -1)
    consts = [row(ln_w), row(ln_b), _head_ones(), w_out.astype(BF16), row(norm2_g), *_split2(router_w.T),
              jnp.broadcast_to(router_bias.reshape(e, 1), (e, tt))]
    const = lambda a: pl.BlockSpec(a.shape, lambda b, i: (0,) * a.ndim, pipeline_mode=pl.Buffered(1))
    tile = lambda w: pl.BlockSpec((None, tt, w), lambda b, i: (b, i, 0))
    flat = lambda rows, dt: jax.ShapeDtypeStruct((rows, n), dt)
    return pl.pallas_call(
        _mixout_kernel,
        grid=(bsz, nt),
        in_specs=[tile(d), pl.BlockSpec((None,) + mod.shape[1:], lambda b, i: (b, 0, 0))]
        + [tile(c)] * 5 + [const(a) for a in consts],
        out_specs=[tile(d), pl.BlockSpec((tt, d // 4), lambda b, i: (b * nt + i, 0)),
                   pl.BlockSpec((tt, d // 4), lambda b, i: (b * nt + i, 0)),
                   pl.BlockSpec((TOP_K, tt), lambda b, i: (0, b * nt + i)),
                   pl.BlockSpec((TOP_K, tt), lambda b, i: (0, b * nt + i))],
        out_shape=[jax.ShapeDtypeStruct((bsz, seq, d), F32), jax.ShapeDtypeStruct((n, d // 4), U32),
                   jax.ShapeDtypeStruct((n, d // 4), U32),
                   flat(TOP_K, jnp.int32), flat(TOP_K, F32)],
        compiler_params=_params("arbitrary", "arbitrary"),
        name="mix_out_router",
    )(x, mod, yhy, o_f, o_b, g, bonus, *consts)


BLK = 512
BLK_SHIFT = 9


def _multi_hot(eid, e):
    row = lax.broadcasted_iota(jnp.int32, (e, eid.shape[1]), 0)
    mh = jnp.zeros((e, eid.shape[1]), F32)
    for kk in range(TOP_K):
        mh = mh + jnp.where(row == eid[kk:kk + 1, :], 1.0, 0.0)
    return row, mh


def _lookup(row, eid, table):
    return jnp.concatenate(
        [jnp.sum(jnp.where(row == eid[kk:kk + 1, :], table, 0.0), axis=0, keepdims=True)
         for kk in range(TOP_K)], axis=0)


def _rank_kernel(eid_ref, rank_ref, cnt_ref, *, e):
    @pl.when(pl.program_id(0) == 0)
    def _():
        cnt_ref[...] = jnp.zeros_like(cnt_ref)

    eid = eid_ref[...]
    tt = eid.shape[1]
    row, mh = _multi_hot(eid, e)
    mhb = mh.astype(BF16)
    si = lax.broadcasted_iota(jnp.int32, (tt, tt), 0)
    ti = lax.broadcasted_iota(jnp.int32, (tt, tt), 1)
    earlier = _dot(mhb, jnp.where(si < ti, 1.0, 0.0).astype(BF16))
    cnt = cnt_ref[...]
    full = earlier + jnp.concatenate([cnt] * (tt // LANES), axis=1)
    rank_ref[...] = _lookup(row, eid, full).astype(jnp.int32)
    cnt_ref[...] = cnt + _dot(mhb, jnp.ones((tt, LANES), BF16))


def _expert_ranks(eid, e, tt=512):
    n = eid.shape[1]
    tt = min(tt, n)
    return pl.pallas_call(
        functools.partial(_rank_kernel, e=e),
        grid=(n // tt,),
        in_specs=[pl.BlockSpec((TOP_K, tt), lambda i: (0, i))],
        out_specs=[pl.BlockSpec((TOP_K, tt), lambda i: (0, i)),
                   pl.BlockSpec((e, LANES), lambda i: (0, 0))],
        out_shape=[jax.ShapeDtypeStruct((TOP_K, n), jnp.int32), jax.ShapeDtypeStruct((e, LANES), F32)],
        compiler_params=_params("arbitrary"),
        name="expert_ranks",
    )(eid)


def _block_offsets(cnt):
    e = cnt.shape[0]
    nblk = ((cnt.astype(jnp.int32) + (BLK - 1)) >> BLK_SHIFT).astype(F32)
    ri = lax.broadcasted_iota(jnp.int32, (e, e), 0)
    ci = lax.broadcasted_iota(jnp.int32, (e, e), 1)
    tril = jnp.where(ci <= ri, 1.0, 0.0).astype(BF16)
    nh, nl = _split2(nblk)
    return nblk, _dot(tril, nh) + _dot(tril, nl)


def _dest_kernel(cnt_ref, eid_ref, rank_ref, dest_ref):
    nblk, end = _block_offsets(cnt_ref[...])
    off = (end - nblk) * float(BLK)
    eid = eid_ref[...]
    tt = eid.shape[1]
    row = lax.broadcasted_iota(jnp.int32, (off.shape[0], tt), 0)
    table = jnp.concatenate([off] * (tt // LANES), axis=1)
    dest_ref[...] = _lookup(row, eid, table).astype(jnp.int32) + rank_ref[...]


def _destinations(cnt, eid, rank, tt=512):
    n = eid.shape[1]
    tt = min(tt, n)
    blk = pl.BlockSpec((TOP_K, tt), lambda i: (0, i))
    return pl.pallas_call(
        _dest_kernel,
        grid=(n // tt,),
        in_specs=[pl.BlockSpec(cnt.shape, lambda i: (0, 0)), blk, blk],
        out_specs=blk,
        out_shape=jax.ShapeDtypeStruct((TOP_K, n), jnp.int32),
        compiler_params=_params("arbitrary"),
        name="expert_destinations",
    )(cnt, eid, rank)


def _meta_kernel(cnt_ref, meta_ref, *, nbp):
    cnt = cnt_ref[...]
    e = cnt.shape[0]
    nblk, end = _block_offsets(cnt)
    rep = lambda a, w: jnp.concatenate([a] * (w // LANES), axis=1)
    b = lax.broadcasted_iota(jnp.int32, (e, nbp), 1).astype(F32)
    blk_e = jnp.minimum(jnp.sum(jnp.where(rep(end, nbp) <= b, 1.0, 0.0), axis=0, keepdims=True), float(e - 1))
    row = lax.broadcasted_iota(jnp.int32, (e, nbp), 0).astype(F32)
    mine = row == blk_e
    left = rep(cnt + (end - nblk) * float(BLK), nbp) - b * float(BLK)
    nvalid = jnp.clip(jnp.sum(jnp.where(mine, left, 0.0), axis=0, keepdims=True), 0.0, float(BLK))
    nused = jnp.max(rep(end, nbp), axis=0, keepdims=True)
    later = jnp.logical_and(row > blk_e, rep(nblk, nbp) > 0.0)
    nxt = jnp.min(jnp.where(later, row, float(e)), axis=0, keepdims=True)
    nxt = jnp.where(nxt >= float(e), -1.0, nxt)
    meta_ref[...] = jnp.concatenate([blk_e, nvalid, nused, nxt, jnp.zeros((4, nbp), F32)],
                                    axis=0).astype(jnp.int32)


def _block_meta(cnt, nb):
    nbp = -(-nb // LANES) * LANES
    return pl.pallas_call(
        functools.partial(_meta_kernel, nbp=nbp),
        out_shape=jax.ShapeDtypeStruct((8, nbp), jnp.int32),
        compiler_params=pltpu.CompilerParams(vmem_limit_bytes=VMEM_LIMIT),
        name="expert_block_meta",
    )(cnt)


SC_WINDOW = 128


def _sc_mesh():
    return plsc.VectorSubcoreMesh(core_axis_name="core", subcore_axis_name="subcore")


def _sc_scatter_rows(rows, idx, nrows):
    n, width = rows.shape

    @pl.kernel(out_type=jax.ShapeDtypeStruct((nrows, width), rows.dtype), mesh=_sc_mesh())
    def scatter(rows_hbm, idx_hbm, out_hbm):
        def body(rows_vmem, idx_vmem):
            pltpu.sync_copy(rows_vmem, out_hbm.at[idx_vmem.at[0]])

        pltpu.emit_pipeline(
            body,
            grid=(n // SC_WINDOW, idx.shape[0]),
            in_specs=[pl.BlockSpec((SC_WINDOW, width), index_map=lambda i, k: (i, 0)),
                      pl.BlockSpec((1, SC_WINDOW), index_map=lambda i, k: (k, i))],
            out_specs=[],
            core_axis_name=("core", "subcore"),
            dimension_semantics=(pltpu.PARALLEL, pltpu.ARBITRARY),
        )(rows_hbm, idx_hbm)

    return scatter(rows, idx)


def _sc_gather_rows(src, idx):
    num = idx.shape[1]
    width = src.shape[1]

    @pl.kernel(out_type=jax.ShapeDtypeStruct((num, width), src.dtype), mesh=_sc_mesh())
    def gather(src_hbm, idx_hbm, out_hbm):
        def body(idx_vmem, out_vmem):
            pltpu.sync_copy(src_hbm.at[idx_vmem.at[0]], out_vmem)

        pltpu.emit_pipeline(
            body,
            grid=(num // SC_WINDOW,),
            in_specs=[pl.BlockSpec((1, SC_WINDOW), index_map=lambda i: (0, i))],
            out_specs=[pl.BlockSpec((SC_WINDOW, width), index_map=lambda i: (i, 0))],
            core_axis_name=("core", "subcore"),
            dimension_semantics=(pltpu.PARALLEL,),
        )(idx_hbm, out_hbm)

    return gather(src, idx)


BLOCKS_PER_STEP = 2


def _experts_kernel(be_ref, nv_ref, nxt_ref, nu_ref, xa_ref, xb_ref, wg_hbm, wu_hbm, wd_hbm, after_hbm,
                    oa_ref, ob_ref, wgf, wuf, wdf, wgb, wub, wdb, sems, slot_ref):
    del after_hbm
    step = pl.program_id(0)

    def fetch(expert, slot):
        return [pltpu.make_async_copy(src.at[expert], dst.at[slot], sems.at[slot])
                for src, dst in ((wg_hbm, wgf), (wu_hbm, wuf), (wd_hbm, wdf))]

    @pl.when(step == 0)
    def _():
        slot_ref[0] = 0
        for cp in fetch(be_ref[0], 0):
            cp.start()

    def one_block(b, rows):
        prev = be_ref[jnp.maximum(b - 1, 0)]

        @pl.when(jnp.logical_or(b == 0, be_ref[b] != prev))
        def _():
            slot = slot_ref[0]
            for cp in fetch(be_ref[b], slot):
                cp.wait()

            @pl.when(nxt_ref[b] >= 0)
            def _():
                for cp in fetch(nxt_ref[b], 1 - slot):
                    cp.start()

            wgb[...] = wgf[slot].astype(BF16)
            wub[...] = wuf[slot].astype(BF16)
            wdb[...] = wdf[slot].astype(BF16)
            slot_ref[0] = 1 - slot

        valid = lax.broadcasted_iota(jnp.int32, (BLK, 1), 0) < nv_ref[b]
        zero = jnp.zeros((), U32)
        x = _unpack_rows(jnp.where(valid, xa_ref[rows, :], zero), jnp.where(valid, xb_ref[rows, :], zero))
        act = _silu(_dot(x, wgb[...])) * _dot(x, wub[...])
        oa_ref[rows, :], ob_ref[rows, :] = _pack_rows(_dot(act.astype(BF16), wdb[...]))

    for j in range(BLOCKS_PER_STEP):
        b = step * BLOCKS_PER_STEP + j
        rows = slice(j * BLK, (j + 1) * BLK)
        if j == 0:
            one_block(b, rows)
        else:
            pl.when(b < nu_ref[0])(functools.partial(one_block, b, rows))


def _experts(blk_e, nvalid, nused, nxt_e, xs_a, xs_b, wg, wu, wd, after):
    p, dq = xs_a.shape
    d, de = wg.shape[1], wg.shape[2]
    rows_in = pl.BlockSpec((BLOCKS_PER_STEP * BLK, dq), lambda b, be, nv, nx, nu: (b, 0))
    hbm = pl.BlockSpec(memory_space=pl.ANY)
    return pl.pallas_call(
        _experts_kernel,
        grid_spec=pltpu.PrefetchScalarGridSpec(
            num_scalar_prefetch=4,
            grid=((nused + BLOCKS_PER_STEP - 1) // BLOCKS_PER_STEP,),
            in_specs=[rows_in, rows_in, hbm, hbm, hbm, hbm],
            out_specs=[rows_in, rows_in],
            scratch_shapes=[pltpu.VMEM((2, d, de), F32), pltpu.VMEM((2, d, de), F32), pltpu.VMEM((2, de, d), F32),
                            pltpu.VMEM((d, de), BF16), pltpu.VMEM((d, de), BF16), pltpu.VMEM((de, d), BF16),
                            pltpu.SemaphoreType.DMA((2,)), pltpu.SMEM((1,), jnp.int32)],
        ),
        out_shape=[jax.ShapeDtypeStruct((p, dq), U32)] * 2,
        compiler_params=_params("arbitrary"),
        name="moe_experts",
    )(blk_e, nvalid, nxt_e, nused.reshape(1), xs_a, xs_b, wg, wu, wd, after)


def _shared_kernel(ha_ref, hb_ref, sg_ref, su_ref, sd_ref, o_ref):
    hb = _unpack_rows(ha_ref[...], hb_ref[...])
    act = _silu(_dot(hb, sg_ref[...])) * _dot(hb, su_ref[...])
    o_ref[...] = _dot(act.astype(BF16), sd_ref[...]).astype(o_ref.dtype)


def _shared_expert(h2a, h2b, sh_wg, sh_wu, sh_wd, tt=512):
    n, dp = h2a.shape
    d = sh_wg.shape[0]
    tt = min(tt, n)
    consts = [sh_wg.astype(BF16), sh_wu.astype(BF16), sh_wd.astype(BF16)]
    packed_rows = pl.BlockSpec((tt, dp), lambda i: (i, 0))
    return pl.pallas_call(
        _shared_kernel,
        grid=(n // tt,),
        in_specs=[packed_rows, packed_rows] + [pl.BlockSpec(a.shape, lambda i: (0, 0)) for a in consts],
        out_specs=pl.BlockSpec((tt, d), lambda i: (i, 0)),
        out_shape=jax.ShapeDtypeStruct((n, d), BF16),
        compiler_params=_params("arbitrary"),
        name="shared_expert",
    )(h2a, h2b, *consts)


def _combine_kernel(w_ref, x1_ref, sh_ref, mod_ref, ga_ref, gb_ref, gf_ref, sel_ref, o_ref):
    ffn = sh_ref[...].astype(F32)
    wh, wl = _split2(w_ref[...])
    acc = None
    for kk in range(TOP_K):
        sel = sel_ref[kk]
        wk = _tn(wh, sel) + _tn(wl, sel)
        a_lo, a_hi = _unpack_halves(ga_ref[kk])
        b_lo, b_hi = _unpack_halves(gb_ref[kk])
        parts = [a_lo * wk, b_lo * wk, a_hi * wk, b_hi * wk]
        acc = parts if acc is None else [p + q for p, q in zip(acc, parts)]
    ffn = ffn + jnp.concatenate(acc, axis=1)
    xo = x1_ref[...] + mod_ref[5:6, :] * ffn
    ms = jnp.mean(xo * xo, axis=-1, keepdims=True)
    o_ref[...] = xo * lax.rsqrt(ms + NORM_EPS) * gf_ref[...]


def _combine(wsel, x1, shared, mod, ga, gb, normf_g, tok0, seq, tt=256):
    n, d = x1.shape
    part = ga.shape[1]
    tt = min(tt, seq, part)
    per = seq // tt
    off = tok0 // tt
    dq = ga.shape[2]
    sel = jnp.asarray(np.broadcast_to(np.eye(TOP_K)[:, :, None], (TOP_K, TOP_K, dq)), BF16)
    consts = [normf_g.reshape(1, d), sel]
    const = lambda a: pl.BlockSpec(a.shape, lambda i: (0,) * a.ndim)
    rows = pl.BlockSpec((tt, d), lambda i: (off + i, 0))
    gathered = pl.BlockSpec((TOP_K, tt, dq), lambda i: (0, i, 0))
    return pl.pallas_call(
        _combine_kernel,
        grid=(part // tt,),
        in_specs=[pl.BlockSpec((TOP_K, tt), lambda i: (0, off + i)),
                  rows, rows,
                  pl.BlockSpec((None,) + mod.shape[1:], lambda i: ((off + i) // per, 0, 0)),
                  gathered, gathered] + [const(a) for a in consts],
        out_specs=rows,
        out_shape=jax.ShapeDtypeStruct((n, d), F32),
        input_output_aliases={1: 0},
        compiler_params=_params("arbitrary"),
        name="moe_combine",
    )(wsel, x1, shared, mod, ga, gb, *consts)


COMBINE_PARTS = 4


def _moe(x1, h2a, h2b, mod, eid, wsel, exp_wg, exp_wu, exp_wd, sh_wg, sh_wu, sh_wd, normf_g):
    n = h2a.shape[0]
    e = exp_wg.shape[0]
    nb = (n * TOP_K + e * (BLK - 1)) // BLK
    nb = -(-nb // BLOCKS_PER_STEP) * BLOCKS_PER_STEP
    rank, cnt = _expert_ranks(eid, e)
    dest = _destinations(cnt, eid, rank)
    meta = _block_meta(cnt, nb)
    xs_a = _sc_scatter_rows(h2a, dest, nb * BLK)
    xs_b = _sc_scatter_rows(h2b, dest, nb * BLK)
    shared = _shared_expert(h2a, h2b, sh_wg, sh_wu, sh_wd)
    ys_a, ys_b = _experts(meta[0, :nb], meta[1, :nb], meta[2, 0], meta[3, :nb], xs_a, xs_b,
                          exp_wg, exp_wu, exp_wd, shared)
    bsz, seq, d = x1.shape
    out = x1.reshape(n, d)
    part = n // COMBINE_PARTS
    for j in range(COMBINE_PARTS):
        idx = dest[:, j * part:(j + 1) * part].reshape(1, TOP_K * part)
        ga = _sc_gather_rows(ys_a, idx).reshape(TOP_K, part, -1)
        gb = _sc_gather_rows(ys_b, idx).reshape(TOP_K, part, -1)
        out = _combine(wsel, out, shared, mod, ga, gb, normf_g, j * part, seq)
    return out


def kernel(x, c, norm1_g, norm2_g, normf_g, w_ada, b_ada, w_in, w_out, hy_conv_w, hy_conv_b, hy_pos_w1, hy_pos_b1, hy_pos_w2, hy_pos_b2, hy_pos_w3, hy_sin_freq, hy_skip, rw_mu, rw_w0, rw_w_up, rw_a0, rw_a_up, rw_g_up, rw_k_k, rw_k_a, rw_r_k, rw_ln_w, rw_ln_b, router_w, router_bias, exp_w_gate, exp_w_up, exp_w_down, sh_w_gate, sh_w_up, sh_w_down):
    bsz, seq, d = x.shape
    depth = w_ada.shape[0]
    assert depth == 1, "the final norm is fused into the last kernel of a single layer"
    for l in range(depth):
        mod = _modulation(c, w_ada[l], b_ada[l]).reshape(bsz, -1, d)
        uhy, rkvk, lwa, g, bonus = _projection(
            x, mod, norm1_g[l], w_in[l], hy_conv_w[l], hy_conv_b[l], rw_mu[l], rw_w0[l], rw_w_up[l],
            rw_a0[l], rw_a_up[l], rw_g_up[l], rw_k_k[l], rw_k_a[l], rw_r_k[l])
        k2, ss = _hyena_filters(seq, hy_pos_w1[l], hy_pos_b1[l], hy_pos_w2[l], hy_pos_b2[l],
                                hy_pos_w3[l], hy_sin_freq[l])
        khat = _filter_spectrum(k2, ss, seq)
        z, z_col = uhy, 0
        for order in range(HYENA_ORDER):
            z = _long_conv_gate(z, z_col, uhy, (order + 1) * D_HYENA, khat, hy_skip[l], order)
            z_col = 0
        o_f, o_b = _wkv(rkvk, lwa, rw_k_a[l])
        x1, h2a, h2b, eid, wsel = _mix_out(x, mod, z, o_f, o_b, g, bonus, rw_ln_w[l], rw_ln_b[l], w_out[l],
                                           norm2_g[l], router_w[l], router_bias[l])
        x = _moe(x1, h2a, h2b, mod, eid, wsel, exp_w_gate[l], exp_w_up[l], exp_w_down[l],
                 sh_w_gate[l], sh_w_up[l], sh_w_down[l], normf_g)
        x = x.reshape(bsz, seq, d)
    return x
```

```python
import functools
import math

import jax
import jax.numpy as jnp
import numpy as np
from jax import lax
from jax.experimental import pallas as pl
from jax.experimental.pallas import tpu as pltpu
from jax.experimental.pallas import tpu_sc as plsc

F32 = jnp.float32
BF16 = jnp.bfloat16

LANES = 128
MXU_DIM = 256
VMEM_LIMIT = 56 * 1024 * 1024

D_HYENA = 512
D_RWKV = 512
HEAD = 64
HYENA_ORDER = 2
FILTER_BANDS = 16
DECAY_TARGET = 1e-2
FAST_DECAY_PCT = 0.3
SLOW_DECAY_PCT = 1.5
FILTER_NORM_EPS = 1e-6
DECAY_LORA = 32
ICLR_LORA = 32
GATE_LORA = 96
GN_EPS = 64e-5
NORM_EPS = 1e-6
TOP_K = 8
N_GROUPS = 8
TOPK_GROUPS = 4
ROUTE_SCALE = 2.5


def _params(*sem):
    return pltpu.CompilerParams(dimension_semantics=sem, vmem_limit_bytes=VMEM_LIMIT)


def _split2(a):
    hi = a.astype(BF16)
    lo = (a - hi.astype(F32)).astype(BF16)
    return hi, lo


def _dot(a, b):
    return jnp.dot(a, b, preferred_element_type=F32)


def _dot3(a, b):
    ah, al = _split2(a)
    bh, bl = _split2(b)
    return _dot(ah, bh) + (_dot(ah, bl) + _dot(al, bh))


def _dot_exact_rhs(a, b_bf16):
    ah, al = _split2(a)
    return _dot(ah, b_bf16) + _dot(al, b_bf16)


def _silu(x):
    return x * jax.nn.sigmoid(x)


U32 = jnp.int32


def _pack_halves(x):
    w = x.shape[1] // 2
    return pltpu.pack_elementwise([x[:, :w], x[:, w:]], packed_dtype=BF16)


def _unpack_halves(p):
    lo = pltpu.unpack_elementwise(p, index=0, packed_dtype=BF16, unpacked_dtype=F32)
    hi = pltpu.unpack_elementwise(p, index=1, packed_dtype=BF16, unpacked_dtype=F32)
    return lo, hi


def _pack_rows(x):
    packed = _pack_halves(x)
    half = packed.shape[1] // 2
    return packed[:, :half], packed[:, half:]


def _unpack_rows(a, b):
    a_lo, a_hi = _unpack_halves(a)
    b_lo, b_hi = _unpack_halves(b)
    return jnp.concatenate([a_lo.astype(BF16), b_lo.astype(BF16), a_hi.astype(BF16), b_hi.astype(BF16)], axis=1)


def _mod_kernel(c_ref, w_ref, b_ref, o_ref):
    o_ref[...] = _dot3(_silu(c_ref[...]), w_ref[...]) + b_ref[...]


def _modulation(c, w_ada, b_ada):
    bsz, d = c.shape
    n = w_ada.shape[1]
    blk = 1024
    return pl.pallas_call(
        _mod_kernel,
        grid=(n // blk,),
        in_specs=[
            pl.BlockSpec((bsz, d), lambda j: (0, 0)),
            pl.BlockSpec((d, blk), lambda j: (0, j)),
            pl.BlockSpec((1, blk), lambda j: (0, j)),
        ],
        out_specs=pl.BlockSpec((bsz, blk), lambda j: (0, j)),
        out_shape=jax.ShapeDtypeStruct((bsz, n), F32),
        compiler_params=_params("arbitrary"),
        name="adaln_mod",
    )(c, w_ada, b_ada.reshape(1, n))


def _filter_kernel(band_ref, w1_ref, b1_ref, w2_ref, b2_ref, w3_ref, freq_ref, delta_ref,
                   k_ref, ss_ref, *, seq, rows):
    half = pl.program_id(0)
    i = pl.program_id(1)
    r = lax.broadcasted_iota(jnp.int32, (rows, LANES), 0) + i * rows
    pos = jnp.where(half == 0, r, seq - r).astype(F32)
    tt = pos / float(max(seq - 1, 1))
    lane = lax.broadcasted_iota(jnp.int32, (rows, LANES), 1)
    feats = jnp.where(lane == 0, tt, jnp.sin(pos * band_ref[0:1, :] + band_ref[1:2, :]))
    freq = freq_ref[...]
    hdn = jnp.sin(freq * (_dot3(feats, w1_ref[...]) + b1_ref[...]))
    for j in range(w2_ref.shape[0]):
        hdn = jnp.sin(freq * (_dot3(hdn, w2_ref[j]) + b2_ref[j]))
    filt = _dot3(hdn, w3_ref[...])
    filt = filt * jnp.exp(-tt[:, :1] * delta_ref[...])
    valid = jnp.logical_or(half == 0, r[:, :1] > 0)
    filt = jnp.where(valid, filt, 0.0)
    k_ref[...] = filt

    @pl.when(jnp.logical_and(half == 0, i == 0))
    def _():
        ss_ref[...] = jnp.zeros_like(ss_ref)

    ss_ref[...] += jnp.broadcast_to(jnp.sum(filt * filt, axis=0, keepdims=True), ss_ref.shape)


def _hyena_filters(seq, pw1, pb1, pw2, pb2, pw3, freq):
    width = pw1.shape[1]
    ncol = HYENA_ORDER * D_HYENA
    rows = min(seq, 512)
    bands = np.zeros((2, LANES), np.float64)
    lin = np.linspace(1e-4, FILTER_BANDS - 1, FILTER_BANDS) * (2.0 * math.pi / seq)
    bands[0, 1:1 + FILTER_BANDS] = lin
    bands[1, 1:1 + FILTER_BANDS] = 0.5 * math.pi
    bands[0, 1 + FILTER_BANDS:1 + 2 * FILTER_BANDS] = -lin
    bands = jnp.asarray(bands, F32)
    deltas = np.abs(np.linspace(math.log(DECAY_TARGET) / SLOW_DECAY_PCT,
                                math.log(DECAY_TARGET) / FAST_DECAY_PCT, D_HYENA))
    deltas = jnp.asarray(np.tile(deltas, HYENA_ORDER)[None], F32)
    w1 = jnp.zeros((LANES, width), F32).at[:pw1.shape[0]].set(pw1)
    w3 = pw3.reshape(width, HYENA_ORDER, 2, D_HYENA).transpose(2, 0, 1, 3).reshape(2, width, ncol)
    nt = seq // rows
    full = lambda *shape: pl.BlockSpec(shape, lambda h, i: (0,) * len(shape))
    return pl.pallas_call(
        functools.partial(_filter_kernel, seq=seq, rows=rows),
        grid=(2, nt),
        in_specs=[
            full(2, LANES), full(LANES, width), full(1, width),
            full(pw2.shape[0], width, width), full(pw2.shape[0], 1, width),
            pl.BlockSpec((None, width, ncol), lambda h, i: (h, 0, 0)),
            full(1, width), full(1, ncol),
        ],
        out_specs=[
            pl.BlockSpec((rows, ncol), lambda h, i: (h * nt + i, 0)),
            pl.BlockSpec((8, ncol), lambda h, i: (0, 0)),
        ],
        out_shape=[jax.ShapeDtypeStruct((2 * seq, ncol), F32),
                   jax.ShapeDtypeStruct((8, ncol), F32)],
        compiler_params=_params("arbitrary", "arbitrary"),
        name="hyena_filters",
    )(bands, w1, pb1.reshape(1, width), pw2, pb2.reshape(pw2.shape[0], 1, width), w3,
      freq.reshape(1, width), deltas)


N1 = LANES
UNROLL = 8


def _dft_tables(seq):
    tables = _dft_tables_np(seq)
    return tuple(jnp.asarray(t, BF16) for t in tables[:5]) + tables[5:]


def _dft_tables_np(seq):
    m = 2 * seq
    n2 = m // N1
    n2h = n2 // 2
    n1 = np.arange(N1)[:, None, None]
    f2 = np.arange(n2)[None, :, None]
    k2 = np.arange(n2)[None, None, :]
    th = 2.0 * np.pi * (n1 * f2 / m + (k2 * f2 % n2) / n2)
    fwd_a = np.concatenate([np.cos(th), -np.sin(th)], axis=1)
    tht = np.transpose(th, (0, 2, 1))
    inv_a = np.concatenate([np.cos(tht), -np.sin(tht)], axis=2)[:, :n2h] / m
    a = np.arange(N1)
    ph = 2.0 * np.pi * np.outer(a, a) / N1
    c, s = np.cos(ph), np.sin(ph)
    fwd_b = np.block([[c, s], [-s, c]])
    inv_b = np.block([[c, -s], [s, c]])
    return fwd_a, fwd_a[:, :, :n2h], inv_a, fwd_b, inv_b, n2, n2h


def _stage_a_fwd(x_ref, wa_ref, y_ref, n2, scale=None):
    def body(i, carry):
        trips = [i * UNROLL + j for j in range(UNROLL)]
        xs = [x_ref[pl.ds(n1, wa_ref.shape[2], stride=N1), :] for n1 in trips]
        if scale is not None:
            xs = [x * scale for x in xs]
        prods = [_dot(wa_ref[n1], x.astype(BF16)) for n1, x in zip(trips, xs)]
        for n1, a in zip(trips, prods):
            y_ref[pl.ds(n1, n2, stride=2 * N1), :] = a[:n2]
            y_ref[pl.ds(N1 + n1, n2, stride=2 * N1), :] = a[n2:]
        return carry
    lax.fori_loop(0, N1 // UNROLL, body, 0)


def _filter_fft_kernel(k_ref, ss_ref, wa_ref, fb_ref, o_ref, y_ref, *, n2):
    scale = lax.rsqrt(ss_ref[0:1, :] + FILTER_NORM_EPS)
    _stage_a_fwd(k_ref, wa_ref, y_ref, n2, scale=scale)

    unr = min(UNROLL, n2)

    def body(i, carry):
        trips = [i * unr + j for j in range(unr)]
        ys = [y_ref[pl.ds(pl.multiple_of(f2 * 2 * N1, 2 * N1), 2 * N1), :].astype(BF16) for f2 in trips]
        for j in range(0, unr, 2):
            z = _dot(fb_ref[...], jnp.concatenate(ys[j:j + 2], axis=1))
            o_ref[trips[j]] = z[:, :LANES]
            o_ref[trips[j + 1]] = z[:, LANES:]
        return carry
    lax.fori_loop(0, n2 // unr, body, 0)


def _filter_spectrum(k2, ss, seq):
    fwd_a, _, _, fwd_b, _, n2, _ = _dft_tables(seq)
    ncol = k2.shape[1]
    nblk = ncol // LANES
    return pl.pallas_call(
        functools.partial(_filter_fft_kernel, n2=n2),
        grid=(nblk,),
        in_specs=[
            pl.BlockSpec((2 * seq, LANES), lambda c: (0, c)),
            pl.BlockSpec((8, LANES), lambda c: (0, c)),
            pl.BlockSpec(fwd_a.shape, lambda c: (0, 0, 0)),
            pl.BlockSpec(fwd_b.shape, lambda c: (0, 0)),
        ],
        out_specs=pl.BlockSpec((None, n2, 2 * N1, LANES), lambda c: (c, 0, 0, 0)),
        out_shape=jax.ShapeDtypeStruct((nblk, n2, 2 * N1, LANES), F32),
        scratch_shapes=[pltpu.VMEM((n2 * 2 * N1, LANES), F32)],
        compiler_params=_params("arbitrary"),
        name="hyena_filter_fft",
    )(k2, ss, fwd_a, fwd_b)


TILE = 8
N1_GROUPS = N1 // TILE


def _tile_tables(seq):
    _, fwd_a, inv_a, _, _, n2, n2h = _dft_tables_np(seq)
    eye = np.eye(TILE)
    fa = fwd_a.reshape(N1_GROUPS, TILE, 2 * n2, n2h)
    wa = np.einsum("qjrn,jk->qrjnk", fa, eye).reshape(N1_GROUPS, 2 * n2 * TILE, n2h * TILE)
    ia = inv_a.reshape(N1_GROUPS, TILE, n2h, 2 * n2)
    vc = np.einsum("qjnr,jk->qnjrk", ia, eye).reshape(N1_GROUPS, n2h * TILE, 2 * n2 * TILE)
    return jnp.asarray(wa, BF16), jnp.asarray(vc, BF16)


def _conv_kernel(u_ref, g_ref, skip_ref, fb_ref, ib_ref, kh_hbm, wa_hbm, vc_hbm, o_ref,
                 y_ref, kh_ref, wa_ref, vc_ref, sem, *, n2, n2h, kh_first):
    c_id, b_id = pl.program_id(0), pl.program_id(1)

    @pl.when(jnp.logical_and(c_id == 0, b_id == 0))
    def _():
        for src, dst in ((wa_hbm, wa_ref), (vc_hbm, vc_ref)):
            cp = pltpu.make_async_copy(src, dst, sem)
            cp.start()
            cp.wait()

    @pl.when(b_id == 0)
    def _():
        cp = pltpu.make_async_copy(kh_hbm.at[kh_first + c_id], kh_ref, sem)
        cp.start()
        cp.wait()

    def y_tile(rf, base):
        ri, f2 = divmod(rf, n2)
        return pl.ds(f2 * 2 * N1 + ri * N1 + base, TILE)

    def stage_a(q, carry):
        base = pl.multiple_of(q * TILE, TILE)
        x = jnp.concatenate([u_ref[pl.ds(N1 * m + base, TILE), :] for m in range(n2h)], axis=0)
        r = _dot(wa_ref[q], x.astype(BF16))
        for rf in range(2 * n2):
            y_ref[y_tile(rf, base), :] = r[rf * TILE:(rf + 1) * TILE]
        return carry
    lax.fori_loop(0, N1_GROUPS, stage_a, 0, unroll=8)

    unr = min(2 * UNROLL, n2)

    def mid(i, carry):
        trips = [i * unr + j for j in range(unr)]
        offs = [pl.multiple_of(f2 * 2 * N1, 2 * N1) for f2 in trips]
        wide = lambda blocks: [jnp.concatenate(blocks[j:j + 2], axis=1) for j in range(0, len(blocks), 2)]
        ys = wide([y_ref[pl.ds(off, 2 * N1), :].astype(BF16) for off in offs])
        khs = wide([kh_ref[f2] for f2 in trips])
        zs = [_dot(fb_ref[...], y) for y in ys]
        ps = []
        for z, kh in zip(zs, khs):
            zr, zi = z[:N1], z[N1:]
            kr, ki = kh[:N1], kh[N1:]
            ps.append(jnp.concatenate([zr * kr - zi * ki, zr * ki + zi * kr], axis=0).astype(BF16))
        gs = [_dot(ib_ref[...], p) for p in ps]
        for j, g in enumerate(gs):
            y_ref[pl.ds(offs[2 * j], 2 * N1), :] = g[:, :LANES]
            y_ref[pl.ds(offs[2 * j + 1], 2 * N1), :] = g[:, LANES:]
        return carry
    lax.fori_loop(0, n2 // unr, mid, 0)

    skip = skip_ref[...]

    def stage_c(q, carry):
        base = pl.multiple_of(q * TILE, TILE)
        g = jnp.concatenate([y_ref[y_tile(rf, base), :] for rf in range(2 * n2)], axis=0)
        conv = _dot(vc_ref[q], g.astype(BF16))
        for m in range(n2h):
            rows = pl.ds(N1 * m + base, TILE)
            o_ref[rows, :] = g_ref[rows, :] * (conv[m * TILE:(m + 1) * TILE] + u_ref[rows, :] * skip)
        return carry
    lax.fori_loop(0, N1_GROUPS, stage_c, 0, unroll=8)


def _long_conv_gate(u, u_col, gate, gate_col, khat, skip, order):
    bsz, seq, _ = u.shape
    ch = D_HYENA
    _, _, _, fwd_b, inv_b, n2, n2h = _dft_tables(seq)
    wa, vc = _tile_tables(seq)
    nblk = ch // LANES
    const = lambda a: pl.BlockSpec(a.shape, lambda c, b: (0,) * a.ndim)
    at = lambda col: pl.BlockSpec((None, seq, LANES), lambda c, b: (b, 0, col // LANES + c))
    hbm = pl.BlockSpec(memory_space=pl.ANY)
    return pl.pallas_call(
        functools.partial(_conv_kernel, n2=n2, n2h=n2h, kh_first=order * nblk),
        grid=(nblk, bsz),
        in_specs=[
            at(u_col), at(gate_col),
            pl.BlockSpec((1, LANES), lambda c, b: (0, c)),
            const(fwd_b), const(inv_b), hbm, hbm, hbm,
        ],
        out_specs=at(0),
        out_shape=jax.ShapeDtypeStruct((bsz, seq, ch), F32),
        scratch_shapes=[pltpu.VMEM((n2 * 2 * N1, LANES), F32), pltpu.VMEM(khat.shape[1:], F32),
                        pltpu.VMEM(wa.shape, BF16), pltpu.VMEM(vc.shape, BF16), pltpu.SemaphoreType.DMA(())],
        compiler_params=_params("arbitrary", "arbitrary"),
        name=f"hyena_conv{order}",
    )(u, gate, skip[order].reshape(1, ch), fwd_b, inv_b, khat, wa, vc)


HALO = 8


def _shift_rows(p, k):
    return pltpu.roll(p, k % p.shape[0], axis=0)


def _proj_kernel(xp_ref, x_ref, xn_ref, mod_ref, g1_ref, why_ref, wrkv_ref, wlora_ref,
                 cw_ref, cb_ref, murkv_ref, mulora_ref, w0_ref, a0_ref, wwah_ref, wwal_ref, gup_ref,
                 kk_ref, ka_ref, rk_ref, ones_ref,
                 uhy_ref, rkvk_ref, lwa_ref, g_ref, bonus_ref,
                 *, tt, nt):
    i = pl.program_id(1)
    xe = jnp.concatenate([xp_ref[...], x_ref[...], xn_ref[...]], axis=0)
    ms = jnp.mean(xe * xe, axis=-1, keepdims=True)
    h = xe * lax.rsqrt(ms + NORM_EPS) * g1_ref[...]
    h = h * (1.0 + mod_ref[1:2, :]) + mod_ref[0:1, :]
    row = lax.broadcasted_iota(jnp.int32, (tt + 2 * HALO, 1), 0)
    inside = jnp.logical_and(jnp.logical_or(row >= HALO, i > 0),
                             jnp.logical_or(row < tt + HALO, i < nt - 1))
    hb = jnp.where(inside, h, 0.0).astype(BF16)
    mid = slice(HALO, tt + HALO)

    p = _dot(hb, why_ref[...])
    u = (_shift_rows(p, 1) * cw_ref[0:1, :] + p * cw_ref[1:2, :]
         + _shift_rows(p, -1) * cw_ref[2:3, :] + cb_ref[...])
    uhy_ref[...] = u[mid]

    p = _dot(hb, wrkv_ref[...])
    p = p + murkv_ref[...] * (0.5 * (_shift_rows(p, 1) + _shift_rows(p, -1)) - p)
    p = p[mid]
    c = D_RWKV
    r, k, v = p[:, :c], p[:, c:2 * c], p[:, 2 * c:]
    rkvk_ref[:, :3 * c] = p

    q = _dot(hb, wlora_ref[...])
    q = q + mulora_ref[...] * (0.5 * (_shift_rows(q, 1) + _shift_rows(q, -1)) - q)
    q = q[mid]
    wa = q[:, :LANES]
    lane = lax.broadcasted_iota(jnp.int32, wa.shape, 1)
    wa = jnp.where(lane < 2 * DECAY_LORA, jnp.tanh(wa), wa)
    wah, wal = _split2(wa)
    up = _dot(wah, wwah_ref[...]) + (_dot(wah, wwal_ref[...]) + _dot(wal, wwah_ref[...]))
    lw = -math.exp(-0.5) * jax.nn.sigmoid(w0_ref[...] + up[:, :2 * c])
    a = jax.nn.sigmoid(a0_ref[...] + up[:, 2 * c:])
    for dd in range(2):
        lwa_ref[:, 2 * dd * c:(2 * dd + 1) * c] = lw[:, dd * c:(dd + 1) * c]
        lwa_ref[:, (2 * dd + 1) * c:(2 * dd + 2) * c] = a[:, dd * c:(dd + 1) * c]
    g_ref[...] = _dot3(jax.nn.sigmoid(q[:, LANES:]), gup_ref[...])

    ones = ones_ref[...]
    kk = k * kk_ref[...]
    nrm = jnp.sqrt(_dot_exact_rhs(kk * kk, ones))
    rkvk_ref[:, 3 * c:] = kk / jnp.maximum(nrm, 1e-12)
    ka = ka_ref[...]
    ksum = k * (2.0 + (a[:, :c] + a[:, c:] - 2.0) * ka)
    bonus_ref[...] = _dot_exact_rhs(r * ksum * rk_ref[...], ones) * v


def _head_ones():
    hid = np.arange(D_RWKV) // HEAD
    return jnp.asarray(hid[:, None] == hid[None, :], BF16)


def _projection(x, mod, norm1_g, w_in, hy_conv_w, hy_conv_b, rw_mu, rw_w0, rw_w_up, rw_a0,
                rw_a_up, rw_g_up, rw_k_k, rw_k_a, rw_r_k, tt=512):
    bsz, seq, d = x.shape
    tt = min(tt, seq)
    nt = seq // tt
    c = D_RWKV
    hy = (HYENA_ORDER + 1) * D_HYENA
    nlora = 2 * LANES
    w_hy = w_in[:, :hy].astype(BF16)
    w_rkv = w_in[:, hy:hy + 3 * c].astype(BF16)
    w_lora = jnp.zeros((d, nlora), F32).at[:, :w_in.shape[1] - hy - 3 * c].set(w_in[:, hy + 3 * c:]).astype(BF16)
    mu_rkv = rw_mu[:3 * c].reshape(1, 3 * c)
    mu_lora = jnp.zeros((1, nlora), F32).at[0, :rw_mu.shape[0] - 3 * c].set(rw_mu[3 * c:])
    wwa = jnp.zeros((LANES, 4 * c), F32)
    for dd in range(2):
        wwa = wwa.at[dd * DECAY_LORA:(dd + 1) * DECAY_LORA, dd * c:(dd + 1) * c].set(rw_w_up[dd])
        wwa = wwa.at[2 * DECAY_LORA + dd * ICLR_LORA:2 * DECAY_LORA + (dd + 1) * ICLR_LORA,
                     2 * c + dd * c:2 * c + (dd + 1) * c].set(rw_a_up[dd])
    gup = jnp.zeros((LANES, c), F32).at[:GATE_LORA].set(rw_g_up)
    row = lambda a: a.reshape(1, -1)

    nb8 = seq // HALO
    tb = tt // HALO
    const = lambda a: pl.BlockSpec(a.shape, lambda b, i: (0,) * a.ndim, pipeline_mode=pl.Buffered(1))
    tile = lambda w: pl.BlockSpec((None, tt, w), lambda b, i: (b, i, 0))
    ins = [
        (x, pl.BlockSpec((None, HALO, d), lambda b, i: (b, jnp.maximum(i * tb - 1, 0), 0))),
        (x, pl.BlockSpec((None, tt, d), lambda b, i: (b, i, 0))),
        (x, pl.BlockSpec((None, HALO, d), lambda b, i: (b, jnp.minimum((i + 1) * tb, nb8 - 1), 0))),
        (mod, pl.BlockSpec((None,) + mod.shape[1:], lambda b, i: (b, 0, 0))),
    ]
    consts = [row(norm1_g), w_hy, w_rkv, w_lora, hy_conv_w, row(hy_conv_b), mu_rkv, mu_lora,
              row(rw_w0), row(rw_a0), *_split2(wwa), gup, row(rw_k_k), row(rw_k_a), row(rw_r_k), _head_ones()]
    ins += [(a, const(a)) for a in consts]
    widths = [hy, 4 * c, 4 * c, c, c]
    return pl.pallas_call(
        functools.partial(_proj_kernel, tt=tt, nt=nt),
        grid=(bsz, nt),
        in_specs=[s for _, s in ins],
        out_specs=[tile(w) for w in widths],
        out_shape=[jax.ShapeDtypeStruct((bsz, seq, w), F32) for w in widths],
        compiler_params=_params("arbitrary", "arbitrary"),
        name="input_projection",
    )(*[a for a, _ in ins])


CHUNK = HEAD
GROUP = MXU_DIM // HEAD


def _nt(a, b):
    return lax.dot_general(a, b, (((1,), (1,)), ((), ())), preferred_element_type=F32)


def _tn(a, b):
    return lax.dot_general(a, b, (((0,), (0,)), ((), ())), preferred_element_type=F32)


def _wkv_direction(r, k, v, kk, lw, a, ka, s_ref, reverse):
    c = CHUNK
    ti = lax.broadcasted_iota(jnp.int32, (c, c), 0)
    si = lax.broadcasted_iota(jnp.int32, (c, c), 1)
    tri = (si >= ti) if reverse else (si <= ti)
    cum = _dot_exact_lhs(jnp.where(tri, 1.0, 0.0).astype(BF16), lw)
    tot = jnp.sum(lw, axis=0, keepdims=True)
    w_incl = jnp.exp(cum)
    w_prev = jnp.exp(cum - lw)
    w_inv = jnp.exp(-cum)
    w_end = jnp.exp(tot - cum)
    w_tot = jnp.exp(tot)
    kd = k * (1.0 + (a - 1.0) * ka)
    b = kk * a
    a_w = -kk * w_prev
    r_w = r * w_incl
    b_w = b * w_inv
    k_w = kd * w_inv
    b_e = b * w_end
    k_e = kd * w_end

    m = MXU_DIM
    ri = lax.broadcasted_iota(jnp.int32, (m, m), 0)
    ci = lax.broadcasted_iota(jnp.int32, (m, m), 1)
    head_mask = (ri // HEAD) == (ci // HEAD)
    tl = lax.broadcasted_iota(jnp.int32, (c, m), 0)
    sl = lax.broadcasted_iota(jnp.int32, (c, m), 1) % c
    strict = (sl > tl) if reverse else (sl < tl)
    incl = (sl >= tl) if reverse else (sl <= tl)
    eye = jnp.where(sl == tl, 1.0, 0.0)
    both = lambda top, bot: jnp.concatenate([top, bot], axis=0)

    def stack(xg):
        xb = xg.astype(BF16)
        return jnp.where(head_mask, jnp.concatenate([xb] * GROUP, axis=0), jnp.zeros((), BF16))

    streams = []
    for g in range(D_RWKV // m):
        sl_g = slice(g * m, (g + 1) * m)
        streams.append(dict(
            ar=both(a_w[:, sl_g], r_w[:, sl_g]).astype(BF16),
            b_st=stack(b_w[:, sl_g]), k_st=stack(k_w[:, sl_g]), v_st=stack(v[:, sl_g]),
            v=v[:, sl_g], bk=both(b_e[:, sl_g], k_e[:, sl_g]).astype(BF16),
            w_tot=w_tot[:, sl_g], s_ref=s_ref.at[g],
            strict=strict, incl=incl, eye=eye, head_mask=head_mask, stack=stack))
    return streams


def _wkv_streams_step(streams):
    c = CHUNK
    both = lambda top, bot: jnp.concatenate([top, bot], axis=0)
    for st in streams:
        st["s"] = st["s_ref"][...]
        st["xb"] = _nt(st["ar"], st["b_st"])
        st["xk"] = _nt(st["ar"], st["k_st"])
        st["xs"] = _nt(st["ar"], st["s"].astype(BF16))
    for st in streams:
        m_k = both(jnp.where(st["strict"], st["xk"][:c], 0.0), jnp.where(st["incl"], st["xk"][c:], 0.0))
        st["kv"] = _dot(m_k.astype(BF16), st["v_st"])
        st["rhs"] = st["xs"][:c] + st["kv"][:c]
        st["pw"] = jnp.where(st["strict"], st["xb"][:c], 0.0)
        st["t"] = st["eye"] + st["pw"]
        st["p_st"] = st["stack"](st["pw"])
    levels = int(math.log2(c)) - 1
    for st in streams:
        st["pw"] = _dot(st["pw"].astype(BF16), st["p_st"])
        st["p_st"] = st["stack"](st["pw"])
    for lvl in range(1, levels + 1):
        for st in streams:
            if lvl < levels:
                prod = _dot(both(st["pw"], st["t"]).astype(BF16), st["p_st"])
                st["pw"] = prod[:c]
                st["t"] = st["t"] + prod[c:]
                st["p_st"] = st["stack"](st["pw"])
            else:
                st["t"] = st["t"] + _dot(st["t"].astype(BF16), st["p_st"])
    for st in streams:
        st["u"] = _dot(st["t"].astype(BF16), st["stack"](st["rhs"]))
    outs = []
    for st in streams:
        m_rb = jnp.where(st["incl"], st["xb"][c:], 0.0)
        outs.append(st["xs"][c:] + _dot(m_rb.astype(BF16), st["stack"](st["u"])) + st["kv"][c:])
        uv = both(st["u"], st["v"]).astype(BF16)
        st["s_ref"][...] = st["s"] * st["w_tot"] + jnp.where(st["head_mask"], _tn(uv, st["bk"]), 0.0)
    return outs


def _dot_exact_lhs(tri_bf16, x):
    xh, xl = _split2(x)
    return _dot(tri_bf16, xh) + _dot(tri_bf16, xl)


def _wkv_kernel(rkvk_f, lwa_f, rkvk_b, lwa_b, ka_ref, of_ref, ob_ref, s_ref, *, nch):
    @pl.when(pl.program_id(1) == 0)
    def _():
        s_ref[...] = jnp.zeros_like(s_ref)

    ka = ka_ref[...]
    c = D_RWKV

    def operands(rkvk_ref, lwa_ref, rows):
        x = rkvk_ref[rows, :]
        la = lwa_ref[rows, :]
        return x[:, :c], x[:, c:2 * c], x[:, 2 * c:3 * c], x[:, 3 * c:], la[:, :c], la[:, c:]

    for ci in range(nch):
        rows_f = slice(ci * CHUNK, (ci + 1) * CHUNK)
        rows_b = slice((nch - 1 - ci) * CHUNK, (nch - ci) * CHUNK)
        fwd = _wkv_direction(*operands(rkvk_f, lwa_f, rows_f), ka, s_ref.at[0], False)
        bwd = _wkv_direction(*operands(rkvk_b, lwa_b, rows_b), ka, s_ref.at[1], True)
        outs = _wkv_streams_step(fwd + bwd)
        of_ref[rows_f, :] = jnp.concatenate(outs[:len(fwd)], axis=1)
        ob_ref[rows_b, :] = jnp.concatenate(outs[len(fwd):], axis=1)


WKV_CHUNKS_PER_STEP = 8


def _wkv(rkvk, lwa, rw_k_a):
    bsz, seq, _ = rkvk.shape
    c = D_RWKV
    nch = WKV_CHUNKS_PER_STEP if seq % (WKV_CHUNKS_PER_STEP * CHUNK) == 0 else 1
    rows = nch * CHUNK
    nb = seq // rows
    fwd = lambda w, lane_blk: pl.BlockSpec((None, rows, w), lambda b, j: (b, j, lane_blk))
    bwd = lambda w, lane_blk: pl.BlockSpec((None, rows, w), lambda b, j: (b, nb - 1 - j, lane_blk))
    return pl.pallas_call(
        functools.partial(_wkv_kernel, nch=nch),
        grid=(bsz, nb),
        in_specs=[fwd(4 * c, 0), fwd(2 * c, 0), bwd(4 * c, 0), bwd(2 * c, 1),
                  pl.BlockSpec((1, c), lambda b, j: (0, 0))],
        out_specs=[fwd(c, 0), bwd(c, 0)],
        out_shape=[jax.ShapeDtypeStruct((bsz, seq, c), F32)] * 2,
        scratch_shapes=[pltpu.VMEM((2, c // MXU_DIM, MXU_DIM, MXU_DIM), F32)],
        compiler_params=_params("arbitrary", "arbitrary"),
        name="wkv7_chunked",
    )(rkvk, lwa, rkvk, lwa, rw_k_a.reshape(1, c))


NEG_INF = float("-inf")


def _first_max(vals, idx, size):
    m = jnp.max(vals, axis=0, keepdims=True)
    i = jnp.min(jnp.where(vals == m, idx, size), axis=0, keepdims=True)
    return m, i


def _route(scores, biased):
    e, tt = scores.shape
    per = e // N_GROUPS
    rowl = lax.broadcasted_iota(jnp.int32, (per, tt), 0)
    gs = []
    for g in range(N_GROUPS):
        blk = biased[g * per:(g + 1) * per]
        m1, i1 = _first_max(blk, rowl, per)
        m2 = jnp.max(jnp.where(rowl == i1, NEG_INF, blk), axis=0, keepdims=True)
        gs.append(m1 + m2)
    cur = jnp.concatenate(gs, axis=0)
    growl = lax.broadcasted_iota(jnp.int32, (N_GROUPS, tt), 0)
    gsel = jnp.zeros((N_GROUPS, tt), F32)
    for _ in range(TOPK_GROUPS):
        _, ig = _first_max(cur, growl, N_GROUPS)
        hit = growl == ig
        gsel = jnp.where(hit, 1.0, gsel)
        cur = jnp.where(hit, NEG_INF, cur)
    emask = jnp.concatenate([jnp.broadcast_to(gsel[g:g + 1], (per, tt)) for g in range(N_GROUPS)], axis=0)
    masked = jnp.where(emask > 0.5, biased, NEG_INF)
    row = lax.broadcasted_iota(jnp.int32, (e, tt), 0)
    ids, ws = [], []
    for _ in range(TOP_K):
        _, ie = _first_max(masked, row, e)
        hit = row == ie
        ids.append(ie)
        ws.append(jnp.sum(jnp.where(hit, scores, 0.0), axis=0, keepdims=True))
        masked = jnp.where(hit, NEG_INF, masked)
    w = jnp.concatenate(ws, axis=0)
    w = w / jnp.sum(w, axis=0, keepdims=True) * ROUTE_SCALE
    return jnp.concatenate(ids, axis=0), w


def _mixout_kernel(x_ref, mod_ref, yhy_ref, of_ref, ob_ref, g_ref, bonus_ref, lnw_ref, lnb_ref,
                   ones_ref, wout_ref, g2n_ref, rwth_ref, rwtl_ref, bias_ref,
                   x1_ref, h2a_ref, h2b_ref, eid_ref, wsel_ref):
    ones = ones_ref[...]
    s = of_ref[...] + ob_ref[...]
    mean = _dot_exact_rhs(s, ones) * (1.0 / HEAD)
    dlt = s - mean
    var = _dot_exact_rhs(dlt * dlt, ones) * (1.0 / HEAD)
    sn = dlt * lax.rsqrt(var + GN_EPS) * lnw_ref[...] + lnb_ref[...]
    yrw = (sn + bonus_ref[...]) * g_ref[...]
    ch = yhy_ref.shape[-1]
    mix = _dot(yhy_ref[...].astype(BF16), wout_ref[:ch, :]) + _dot(yrw.astype(BF16), wout_ref[ch:, :])
    x1 = x_ref[...] + mod_ref[2:3, :] * mix
    x1_ref[...] = x1
    ms = jnp.mean(x1 * x1, axis=-1, keepdims=True)
    h2 = x1 * lax.rsqrt(ms + NORM_EPS) * g2n_ref[...]
    h2 = h2 * (1.0 + mod_ref[4:5, :]) + mod_ref[3:4, :]
    h2a_ref[...], h2b_ref[...] = _pack_rows(h2)
    rh, rl = rwth_ref[...], rwtl_ref[...]
    hh, hl = _split2(h2)
    logits = _nt(rh, hh) + (_nt(rh, hl) + _nt(rl, hh))
    scores = jax.nn.sigmoid(logits)
    ids, w = _route(scores, scores + bias_ref[...])
    eid_ref[...] = ids
    wsel_ref[...] = w


def _mix_out(x, mod, yhy, o_f, o_b, g, bonus, ln_w, ln_b, w_out, norm2_g, router_w, router_bias, tt=1024):
    bsz, seq, d = x.shape
    tt = min(tt, seq)
    nt = seq // tt
    n = bsz * seq
    c = D_RWKV
    e = router_w.shape[1]
    row = lambda a: a.reshape(1, -1)
    consts = [row(ln_w), row(ln_b), _head_ones(), w_out.astype(BF16), row(norm2_g), *_split2(router_w.T),
              jnp.broadcast_to(router_bias.reshape(e, 1), (e, tt))]
    const = lambda a: pl.BlockSpec(a.shape, lambda b, i: (0,) * a.ndim, pipeline_mode=pl.Buffered(1))
    tile = lambda w: pl.BlockSpec((None, tt, w), lambda b, i: (b, i, 0))
    flat = lambda rows, dt: jax.ShapeDtypeStruct((rows, n), dt)
    return pl.pallas_call(
        _mixout_kernel,
        grid=(bsz, nt),
        in_specs=[tile(d), pl.BlockSpec((None,) + mod.shape[1:], lambda b, i: (b, 0, 0))]
        + [tile(c)] * 5 + [const(a) for a in consts],
        out_specs=[tile(d), pl.BlockSpec((tt, d // 4), lambda b, i: (b * nt + i, 0)),
                   pl.BlockSpec((tt, d // 4), lambda b, i: (b * nt + i, 0)),
                   pl.BlockSpec((TOP_K, tt), lambda b, i: (0, b * nt + i)),
                   pl.BlockSpec((TOP_K, tt), lambda b, i: (0, b * nt + i))],
        out_shape=[jax.ShapeDtypeStruct((bsz, seq, d), F32), jax.ShapeDtypeStruct((n, d // 4), U32),
                   jax.ShapeDtypeStruct((n, d // 4), U32),
                   flat(TOP_K, jnp.int32), flat(TOP_K, F32)],
        compiler_params=_params("arbitrary", "arbitrary"),
        name="mix_out_router",
    )(x, mod, yhy, o_f, o_b, g, bonus, *consts)


BLK = 512
BLK_SHIFT = 9


def _multi_hot(eid, e):
    row = lax.broadcasted_iota(jnp.int32, (e, eid.shape[1]), 0)
    mh = jnp.zeros((e, eid.shape[1]), F32)
    for kk in range(TOP_K):
        mh = mh + jnp.where(row == eid[kk:kk + 1, :], 1.0, 0.0)
    return row, mh


def _lookup(row, eid, table):
    return jnp.concatenate(
        [jnp.sum(jnp.where(row == eid[kk:kk + 1, :], table, 0.0), axis=0, keepdims=True)
         for kk in range(TOP_K)], axis=0)


def _rank_kernel(eid_ref, rank_ref, cnt_ref, *, e):
    @pl.when(pl.program_id(0) == 0)
    def _():
        cnt_ref[...] = jnp.zeros_like(cnt_ref)

    eid = eid_ref[...]
    tt = eid.shape[1]
    row, mh = _multi_hot(eid, e)
    mhb = mh.astype(BF16)
    si = lax.broadcasted_iota(jnp.int32, (tt, tt), 0)
    ti = lax.broadcasted_iota(jnp.int32, (tt, tt), 1)
    earlier = _dot(mhb, jnp.where(si < ti, 1.0, 0.0).astype(BF16))
    cnt = cnt_ref[...]
    full = earlier + jnp.concatenate([cnt] * (tt // LANES), axis=1)
    rank_ref[...] = _lookup(row, eid, full).astype(jnp.int32)
    cnt_ref[...] = cnt + _dot(mhb, jnp.ones((tt, LANES), BF16))


def _expert_ranks(eid, e, tt=512):
    n = eid.shape[1]
    tt = min(tt, n)
    return pl.pallas_call(
        functools.partial(_rank_kernel, e=e),
        grid=(n // tt,),
        in_specs=[pl.BlockSpec((TOP_K, tt), lambda i: (0, i))],
        out_specs=[pl.BlockSpec((TOP_K, tt), lambda i: (0, i)),
                   pl.BlockSpec((e, LANES), lambda i: (0, 0))],
        out_shape=[jax.ShapeDtypeStruct((TOP_K, n), jnp.int32), jax.ShapeDtypeStruct((e, LANES), F32)],
        compiler_params=_params("arbitrary"),
        name="expert_ranks",
    )(eid)


def _block_offsets(cnt):
    e = cnt.shape[0]
    nblk = ((cnt.astype(jnp.int32) + (BLK - 1)) >> BLK_SHIFT).astype(F32)
    ri = lax.broadcasted_iota(jnp.int32, (e, e), 0)
    ci = lax.broadcasted_iota(jnp.int32, (e, e), 1)
    tril = jnp.where(ci <= ri, 1.0, 0.0).astype(BF16)
    nh, nl = _split2(nblk)
    return nblk, _dot(tril, nh) + _dot(tril, nl)


def _dest_kernel(cnt_ref, eid_ref, rank_ref, dest_ref):
    nblk, end = _block_offsets(cnt_ref[...])
    off = (end - nblk) * float(BLK)
    eid = eid_ref[...]
    tt = eid.shape[1]
    row = lax.broadcasted_iota(jnp.int32, (off.shape[0], tt), 0)
    table = jnp.concatenate([off] * (tt // LANES), axis=1)
    dest_ref[...] = _lookup(row, eid, table).astype(jnp.int32) + rank_ref[...]


def _destinations(cnt, eid, rank, tt=1024):
    n = eid.shape[1]
    tt = min(tt, n)
    blk = pl.BlockSpec((TOP_K, tt), lambda i: (0, i))
    return pl.pallas_call(
        _dest_kernel,
        grid=(n // tt,),
        in_specs=[pl.BlockSpec(cnt.shape, lambda i: (0, 0)), blk, blk],
        out_specs=blk,
        out_shape=jax.ShapeDtypeStruct((TOP_K, n), jnp.int32),
        compiler_params=_params("arbitrary"),
        name="expert_destinations",
    )(cnt, eid, rank)


def _meta_kernel(cnt_ref, meta_ref, *, nbp):
    cnt = cnt_ref[...]
    e = cnt.shape[0]
    nblk, end = _block_offsets(cnt)
    rep = lambda a, w: jnp.concatenate([a] * (w // LANES), axis=1)
    b = lax.broadcasted_iota(jnp.int32, (e, nbp), 1).astype(F32)
    blk_e = jnp.minimum(jnp.sum(jnp.where(rep(end, nbp) <= b, 1.0, 0.0), axis=0, keepdims=True), float(e - 1))
    row = lax.broadcasted_iota(jnp.int32, (e, nbp), 0).astype(F32)
    mine = row == blk_e
    left = rep(cnt + (end - nblk) * float(BLK), nbp) - b * float(BLK)
    nvalid = jnp.clip(jnp.sum(jnp.where(mine, left, 0.0), axis=0, keepdims=True), 0.0, float(BLK))
    nused = jnp.max(rep(end, nbp), axis=0, keepdims=True)
    later = jnp.logical_and(row > blk_e, rep(nblk, nbp) > 0.0)
    nxt = jnp.min(jnp.where(later, row, float(e)), axis=0, keepdims=True)
    nxt = jnp.where(nxt >= float(e), -1.0, nxt)
    meta_ref[...] = jnp.concatenate([blk_e, nvalid, nused, nxt, jnp.zeros((4, nbp), F32)],
                                    axis=0).astype(jnp.int32)


def _block_meta(cnt, nb):
    nbp = -(-nb // LANES) * LANES
    return pl.pallas_call(
        functools.partial(_meta_kernel, nbp=nbp),
        out_shape=jax.ShapeDtypeStruct((8, nbp), jnp.int32),
        compiler_params=pltpu.CompilerParams(vmem_limit_bytes=VMEM_LIMIT),
        name="expert_block_meta",
    )(cnt)


SC_WINDOW = 128


def _sc_mesh():
    return plsc.VectorSubcoreMesh(core_axis_name="core", subcore_axis_name="subcore")


def _sc_scatter_rows(rows, idx, nrows):
    n, width = rows.shape

    @pl.kernel(out_type=jax.ShapeDtypeStruct((nrows, width), rows.dtype), mesh=_sc_mesh())
    def scatter(rows_hbm, idx_hbm, out_hbm):
        def body(rows_vmem, idx_vmem):
            pltpu.sync_copy(rows_vmem, out_hbm.at[idx_vmem.at[0]])

        pltpu.emit_pipeline(
            body,
            grid=(n // SC_WINDOW, idx.shape[0]),
            in_specs=[pl.BlockSpec((SC_WINDOW, width), index_map=lambda i, k: (i, 0)),
                      pl.BlockSpec((1, SC_WINDOW), index_map=lambda i, k: (k, i))],
            out_specs=[],
            core_axis_name=("core", "subcore"),
            dimension_semantics=(pltpu.PARALLEL, pltpu.ARBITRARY),
        )(rows_hbm, idx_hbm)

    return scatter(rows, idx)


def _sc_gather_rows(src, idx):
    num = idx.shape[1]
    width = src.shape[1]

    @pl.kernel(out_type=jax.ShapeDtypeStruct((num, width), src.dtype), mesh=_sc_mesh())
    def gather(src_hbm, idx_hbm, out_hbm):
        def body(idx_vmem, out_vmem):
            pltpu.sync_copy(src_hbm.at[idx_vmem.at[0]], out_vmem)

        pltpu.emit_pipeline(
            body,
            grid=(num // SC_WINDOW,),
            in_specs=[pl.BlockSpec((1, SC_WINDOW), index_map=lambda i: (0, i))],
            out_specs=[pl.BlockSpec((SC_WINDOW, width), index_map=lambda i: (i, 0))],
            core_axis_name=("core", "subcore"),
            dimension_semantics=(pltpu.PARALLEL,),
        )(idx_hbm, out_hbm)

    return gather(src, idx)


BLOCKS_PER_STEP = 2


def _experts_kernel(be_ref, nv_ref, nxt_ref, nu_ref, xa_ref, xb_ref, wg_hbm, wu_hbm, wd_hbm, oa_ref, ob_ref,
                    wgf, wuf, wdf, wgb, wub, wdb, sems, slot_ref):
    step = pl.program_id(0)

    def fetch(expert, slot):
        return [pltpu.make_async_copy(src.at[expert], dst.at[slot], sems.at[slot])
                for src, dst in ((wg_hbm, wgf), (wu_hbm, wuf), (wd_hbm, wdf))]

    @pl.when(step == 0)
    def _():
        slot_ref[0] = 0
        for cp in fetch(be_ref[0], 0):
            cp.start()

    def one_block(b, rows):
        prev = be_ref[jnp.maximum(b - 1, 0)]

        @pl.when(jnp.logical_or(b == 0, be_ref[b] != prev))
        def _():
            slot = slot_ref[0]
            for cp in fetch(be_ref[b], slot):
                cp.wait()

            @pl.when(nxt_ref[b] >= 0)
            def _():
                for cp in fetch(nxt_ref[b], 1 - slot):
                    cp.start()

            wgb[...] = wgf[slot].astype(BF16)
            wub[...] = wuf[slot].astype(BF16)
            wdb[...] = wdf[slot].astype(BF16)
            slot_ref[0] = 1 - slot

        valid = lax.broadcasted_iota(jnp.int32, (BLK, 1), 0) < nv_ref[b]
        zero = jnp.zeros((), U32)
        x = _unpack_rows(jnp.where(valid, xa_ref[rows, :], zero), jnp.where(valid, xb_ref[rows, :], zero))
        act = _silu(_dot(x, wgb[...])) * _dot(x, wub[...])
        oa_ref[rows, :], ob_ref[rows, :] = _pack_rows(_dot(act.astype(BF16), wdb[...]))

    for j in range(BLOCKS_PER_STEP):
        b = step * BLOCKS_PER_STEP + j
        rows = slice(j * BLK, (j + 1) * BLK)
        if j == 0:
            one_block(b, rows)
        else:
            pl.when(b < nu_ref[0])(functools.partial(one_block, b, rows))


def _experts(blk_e, nvalid, nused, nxt_e, xs_a, xs_b, wg, wu, wd):
    p, dq = xs_a.shape
    d, de = wg.shape[1], wg.shape[2]
    rows_in = pl.BlockSpec((BLOCKS_PER_STEP * BLK, dq), lambda b, be, nv, nx, nu: (b, 0))
    hbm = pl.BlockSpec(memory_space=pl.ANY)
    return pl.pallas_call(
        _experts_kernel,
        grid_spec=pltpu.PrefetchScalarGridSpec(
            num_scalar_prefetch=4,
            grid=((nused + BLOCKS_PER_STEP - 1) // BLOCKS_PER_STEP,),
            in_specs=[rows_in, rows_in, hbm, hbm, hbm],
            out_specs=[rows_in, rows_in],
            scratch_shapes=[pltpu.VMEM((2, d, de), F32), pltpu.VMEM((2, d, de), F32), pltpu.VMEM((2, de, d), F32),
                            pltpu.VMEM((d, de), BF16), pltpu.VMEM((d, de), BF16), pltpu.VMEM((de, d), BF16),
                            pltpu.SemaphoreType.DMA((2,)), pltpu.SMEM((1,), jnp.int32)],
        ),
        out_shape=[jax.ShapeDtypeStruct((p, dq), U32)] * 2,
        compiler_params=_params("arbitrary"),
        name="moe_experts",
    )(blk_e, nvalid, nxt_e, nused.reshape(1), xs_a, xs_b, wg, wu, wd)


def _shared_kernel(ha_ref, hb_ref, sg_ref, su_ref, sd_ref, o_ref):
    hb = _unpack_rows(ha_ref[...], hb_ref[...])
    act = _silu(_dot(hb, sg_ref[...])) * _dot(hb, su_ref[...])
    o_ref[...] = _dot(act.astype(BF16), sd_ref[...]).astype(o_ref.dtype)


def _shared_expert(h2a, h2b, sh_wg, sh_wu, sh_wd, tt=1024):
    n, dp = h2a.shape
    d = sh_wg.shape[0]
    tt = min(tt, n)
    consts = [sh_wg.astype(BF16), sh_wu.astype(BF16), sh_wd.astype(BF16)]
    packed_rows = pl.BlockSpec((tt, dp), lambda i: (i, 0))
    return pl.pallas_call(
        _shared_kernel,
        grid=(n // tt,),
        in_specs=[packed_rows, packed_rows] + [pl.BlockSpec(a.shape, lambda i: (0, 0)) for a in consts],
        out_specs=pl.BlockSpec((tt, d), lambda i: (i, 0)),
        out_shape=jax.ShapeDtypeStruct((n, d), BF16),
        compiler_params=_params("arbitrary"),
        name="shared_expert",
    )(h2a, h2b, *consts)


def _combine_kernel(w_ref, x1_ref, sh_ref, mod_ref, ga_ref, gb_ref, gf_ref, sel_ref, o_ref):
    ffn = sh_ref[...].astype(F32)
    wh, wl = _split2(w_ref[...])
    acc = None
    for kk in range(TOP_K):
        sel = sel_ref[kk]
        wk = _tn(wh, sel) + _tn(wl, sel)
        a_lo, a_hi = _unpack_halves(ga_ref[kk])
        b_lo, b_hi = _unpack_halves(gb_ref[kk])
        parts = [a_lo * wk, b_lo * wk, a_hi * wk, b_hi * wk]
        acc = parts if acc is None else [p + q for p, q in zip(acc, parts)]
    ffn = ffn + jnp.concatenate(acc, axis=1)
    xo = x1_ref[...] + mod_ref[5:6, :] * ffn
    ms = jnp.mean(xo * xo, axis=-1, keepdims=True)
    o_ref[...] = xo * lax.rsqrt(ms + NORM_EPS) * gf_ref[...]


def _combine(wsel, x1, shared, mod, ga, gb, normf_g, tok0, seq, tt=512):
    n, d = x1.shape
    part = ga.shape[1]
    tt = min(tt, seq, part)
    per = seq // tt
    off = tok0 // tt
    dq = ga.shape[2]
    sel = jnp.asarray(np.broadcast_to(np.eye(TOP_K)[:, :, None], (TOP_K, TOP_K, dq)), BF16)
    consts = [normf_g.reshape(1, d), sel]
    const = lambda a: pl.BlockSpec(a.shape, lambda i: (0,) * a.ndim)
    rows = pl.BlockSpec((tt, d), lambda i: (off + i, 0))
    gathered = pl.BlockSpec((TOP_K, tt, dq), lambda i: (0, i, 0))
    return pl.pallas_call(
        _combine_kernel,
        grid=(part // tt,),
        in_specs=[pl.BlockSpec((TOP_K, tt), lambda i: (0, off + i)),
                  rows, rows,
                  pl.BlockSpec((None,) + mod.shape[1:], lambda i: ((off + i) // per, 0, 0)),
                  gathered, gathered] + [const(a) for a in consts],
        out_specs=rows,
        out_shape=jax.ShapeDtypeStruct((n, d), F32),
        input_output_aliases={1: 0},
        compiler_params=_params("arbitrary"),
        name="moe_combine",
    )(wsel, x1, shared, mod, ga, gb, *consts)


COMBINE_PARTS = 4


def _moe(x1, h2a, h2b, mod, eid, wsel, exp_wg, exp_wu, exp_wd, sh_wg, sh_wu, sh_wd, normf_g):
    n = h2a.shape[0]
    e = exp_wg.shape[0]
    nb = (n * TOP_K + e * (BLK - 1)) // BLK
    nb = -(-nb // BLOCKS_PER_STEP) * BLOCKS_PER_STEP
    rank, cnt = _expert_ranks(eid, e)
    dest = _destinations(cnt, eid, rank)
    meta = _block_meta(cnt, nb)
    xs_a = _sc_scatter_rows(h2a, dest, nb * BLK)
    xs_b = _sc_scatter_rows(h2b, dest, nb * BLK)
    shared = _shared_expert(h2a, h2b, sh_wg, sh_wu, sh_wd)
    ys_a, ys_b = _experts(meta[0, :nb], meta[1, :nb], meta[2, 0], meta[3, :nb], xs_a, xs_b,
                          exp_wg, exp_wu, exp_wd)
    bsz, seq, d = x1.shape
    out = x1.reshape(n, d)
    part = n // COMBINE_PARTS
    for j in range(COMBINE_PARTS):
        idx = dest[:, j * part:(j + 1) * part].reshape(1, TOP_K * part)
        ga = _sc_gather_rows(ys_a, idx).reshape(TOP_K, part, -1)
        gb = _sc_gather_rows(ys_b, idx).reshape(TOP_K, part, -1)
        out = _combine(wsel, out, shared, mod, ga, gb, normf_g, j * part, seq)
    return out


def kernel(x, c, norm1_g, norm2_g, normf_g, w_ada, b_ada, w_in, w_out, hy_conv_w, hy_conv_b, hy_pos_w1, hy_pos_b1, hy_pos_w2, hy_pos_b2, hy_pos_w3, hy_sin_freq, hy_skip, rw_mu, rw_w0, rw_w_up, rw_a0, rw_a_up, rw_g_up, rw_k_k, rw_k_a, rw_r_k, rw_ln_w, rw_ln_b, router_w, router_bias, exp_w_gate, exp_w_up, exp_w_down, sh_w_gate, sh_w_up, sh_w_down):
    bsz, seq, d = x.shape
    depth = w_ada.shape[0]
    assert depth == 1, "the final norm is fused into the last kernel of a single layer"
    for l in range(depth):
        mod = _modulation(c, w_ada[l], b_ada[l]).reshape(bsz, -1, d)
        uhy, rkvk, lwa, g, bonus = _projection(
            x, mod, norm1_g[l], w_in[l], hy_conv_w[l], hy_conv_b[l], rw_mu[l], rw_w0[l], rw_w_up[l],
            rw_a0[l], rw_a_up[l], rw_g_up[l], rw_k_k[l], rw_k_a[l], rw_r_k[l])
        k2, ss = _hyena_filters(seq, hy_pos_w1[l], hy_pos_b1[l], hy_pos_w2[l], hy_pos_b2[l],
                                hy_pos_w3[l], hy_sin_freq[l])
        khat = _filter_spectrum(k2, ss, seq)
        z, z_col = uhy, 0
        for order in range(HYENA_ORDER):
            z = _long_conv_gate(z, z_col, uhy, (order + 1) * D_HYENA, khat, hy_skip[l], order)
            z_col = 0
        o_f, o_b = _wkv(rkvk, lwa, rw_k_a[l])
        x1, h2a, h2b, eid, wsel = _mix_out(x, mod, z, o_f, o_b, g, bonus, rw_ln_w[l], rw_ln_b[l], w_out[l],
                                           norm2_g[l], router_w[l], router_bias[l])
        x = _moe(x1, h2a, h2b, mod, eid, wsel, exp_w_gate[l], exp_w_up[l], exp_w_down[l],
                 sh_w_gate[l], sh_w_up[l], sh_w_down[l], normf_g)
        x = x.reshape(bsz, seq, d)
    return x
```

```python
import functools
import math

import jax
import jax.numpy as jnp
import numpy as np
from jax import lax
from jax.experimental import pallas as pl
from jax.experimental.pallas import tpu as pltpu
from jax.experimental.pallas import tpu_sc as plsc

F32 = jnp.float32
BF16 = jnp.bfloat16

LANES = 128
MXU_DIM = 256
VMEM_LIMIT = 56 * 1024 * 1024

D_HYENA = 512
D_RWKV = 512
HEAD = 64
HYENA_ORDER = 2
FILTER_BANDS = 16
DECAY_TARGET = 1e-2
FAST_DECAY_PCT = 0.3
SLOW_DECAY_PCT = 1.5
FILTER_NORM_EPS = 1e-6
DECAY_LORA = 32
ICLR_LORA = 32
GATE_LORA = 96
GN_EPS = 64e-5
NORM_EPS = 1e-6
TOP_K = 8
N_GROUPS = 8
TOPK_GROUPS = 4
ROUTE_SCALE = 2.5


def _params(*sem):
    return pltpu.CompilerParams(dimension_semantics=sem, vmem_limit_bytes=VMEM_LIMIT)


def _split2(a):
    hi = a.astype(BF16)
    lo = (a - hi.astype(F32)).astype(BF16)
    return hi, lo


def _dot(a, b):
    return jnp.dot(a, b, preferred_element_type=F32)


def _dot3(a, b):
    ah, al = _split2(a)
    bh, bl = _split2(b)
    return _dot(ah, bh) + (_dot(ah, bl) + _dot(al, bh))


def _dot_exact_rhs(a, b_bf16):
    ah, al = _split2(a)
    return _dot(ah, b_bf16) + _dot(al, b_bf16)


def _silu(x):
    return x * jax.nn.sigmoid(x)


U32 = jnp.int32


def _pack_halves(x):
    w = x.shape[1] // 2
    return pltpu.pack_elementwise([x[:, :w], x[:, w:]], packed_dtype=BF16)


def _unpack_halves(p):
    lo = pltpu.unpack_elementwise(p, index=0, packed_dtype=BF16, unpacked_dtype=F32)
    hi = pltpu.unpack_elementwise(p, index=1, packed_dtype=BF16, unpacked_dtype=F32)
    return lo, hi


def _pack_rows(x):
    packed = _pack_halves(x)
    half = packed.shape[1] // 2
    return packed[:, :half], packed[:, half:]


def _unpack_rows(a, b):
    a_lo, a_hi = _unpack_halves(a)
    b_lo, b_hi = _unpack_halves(b)
    return jnp.concatenate([a_lo.astype(BF16), b_lo.astype(BF16), a_hi.astype(BF16), b_hi.astype(BF16)], axis=1)


def _mod_kernel(c_ref, w_ref, b_ref, o_ref):
    o_ref[...] = _dot3(_silu(c_ref[...]), w_ref[...]) + b_ref[...]


def _modulation(c, w_ada, b_ada):
    bsz, d = c.shape
    n = w_ada.shape[1]
    blk = 1024
    return pl.pallas_call(
        _mod_kernel,
        grid=(n // blk,),
        in_specs=[
            pl.BlockSpec((bsz, d), lambda j: (0, 0)),
            pl.BlockSpec((d, blk), lambda j: (0, j)),
            pl.BlockSpec((1, blk), lambda j: (0, j)),
        ],
        out_specs=pl.BlockSpec((bsz, blk), lambda j: (0, j)),
        out_shape=jax.ShapeDtypeStruct((bsz, n), F32),
        compiler_params=_params("arbitrary"),
        name="adaln_mod",
    )(c, w_ada, b_ada.reshape(1, n))


def _filter_kernel(band_ref, w1_ref, b1_ref, w2_ref, b2_ref, w3_ref, freq_ref, delta_ref,
                   k_ref, ss_ref, *, seq, rows):
    half = pl.program_id(0)
    i = pl.program_id(1)
    r = lax.broadcasted_iota(jnp.int32, (rows, LANES), 0) + i * rows
    pos = jnp.where(half == 0, r, seq - r).astype(F32)
    tt = pos / float(max(seq - 1, 1))
    lane = lax.broadcasted_iota(jnp.int32, (rows, LANES), 1)
    feats = jnp.where(lane == 0, tt, jnp.sin(pos * band_ref[0:1, :] + band_ref[1:2, :]))
    freq = freq_ref[...]
    hdn = jnp.sin(freq * (_dot3(feats, w1_ref[...]) + b1_ref[...]))
    for j in range(w2_ref.shape[0]):
        hdn = jnp.sin(freq * (_dot3(hdn, w2_ref[j]) + b2_ref[j]))
    filt = _dot3(hdn, w3_ref[...])
    filt = filt * jnp.exp(-tt[:, :1] * delta_ref[...])
    valid = jnp.logical_or(half == 0, r[:, :1] > 0)
    filt = jnp.where(valid, filt, 0.0)
    k_ref[...] = filt

    @pl.when(jnp.logical_and(half == 0, i == 0))
    def _():
        ss_ref[...] = jnp.zeros_like(ss_ref)

    ss_ref[...] += jnp.broadcast_to(jnp.sum(filt * filt, axis=0, keepdims=True), ss_ref.shape)


def _hyena_filters(seq, pw1, pb1, pw2, pb2, pw3, freq):
    width = pw1.shape[1]
    ncol = HYENA_ORDER * D_HYENA
    rows = min(seq, 512)
    bands = np.zeros((2, LANES), np.float64)
    lin = np.linspace(1e-4, FILTER_BANDS - 1, FILTER_BANDS) * (2.0 * math.pi / seq)
    bands[0, 1:1 + FILTER_BANDS] = lin
    bands[1, 1:1 + FILTER_BANDS] = 0.5 * math.pi
    bands[0, 1 + FILTER_BANDS:1 + 2 * FILTER_BANDS] = -lin
    bands = jnp.asarray(bands, F32)
    deltas = np.abs(np.linspace(math.log(DECAY_TARGET) / SLOW_DECAY_PCT,
                                math.log(DECAY_TARGET) / FAST_DECAY_PCT, D_HYENA))
    deltas = jnp.asarray(np.tile(deltas, HYENA_ORDER)[None], F32)
    w1 = jnp.zeros((LANES, width), F32).at[:pw1.shape[0]].set(pw1)
    w3 = pw3.reshape(width, HYENA_ORDER, 2, D_HYENA).transpose(2, 0, 1, 3).reshape(2, width, ncol)
    nt = seq // rows
    full = lambda *shape: pl.BlockSpec(shape, lambda h, i: (0,) * len(shape))
    return pl.pallas_call(
        functools.partial(_filter_kernel, seq=seq, rows=rows),
        grid=(2, nt),
        in_specs=[
            full(2, LANES), full(LANES, width), full(1, width),
            full(pw2.shape[0], width, width), full(pw2.shape[0], 1, width),
            pl.BlockSpec((None, width, ncol), lambda h, i: (h, 0, 0)),
            full(1, width), full(1, ncol),
        ],
        out_specs=[
            pl.BlockSpec((rows, ncol), lambda h, i: (h * nt + i, 0)),
            pl.BlockSpec((8, ncol), lambda h, i: (0, 0)),
        ],
        out_shape=[jax.ShapeDtypeStruct((2 * seq, ncol), F32),
                   jax.ShapeDtypeStruct((8, ncol), F32)],
        compiler_params=_params("arbitrary", "arbitrary"),
        name="hyena_filters",
    )(bands, w1, pb1.reshape(1, width), pw2, pb2.reshape(pw2.shape[0], 1, width), w3,
      freq.reshape(1, width), deltas)


N1 = LANES
UNROLL = 8


def _dft_tables(seq):
    tables = _dft_tables_np(seq)
    return tuple(jnp.asarray(t, BF16) for t in tables[:5]) + tables[5:]


def _dft_tables_np(seq):
    m = 2 * seq
    n2 = m // N1
    n2h = n2 // 2
    n1 = np.arange(N1)[:, None, None]
    f2 = np.arange(n2)[None, :, None]
    k2 = np.arange(n2)[None, None, :]
    th = 2.0 * np.pi * (n1 * f2 / m + (k2 * f2 % n2) / n2)
    fwd_a = np.concatenate([np.cos(th), -np.sin(th)], axis=1)
    tht = np.transpose(th, (0, 2, 1))
    inv_a = np.concatenate([np.cos(tht), -np.sin(tht)], axis=2)[:, :n2h] / m
    a = np.arange(N1)
    ph = 2.0 * np.pi * np.outer(a, a) / N1
    c, s = np.cos(ph), np.sin(ph)
    fwd_b = np.block([[c, s], [-s, c]])
    inv_b = np.block([[c, -s], [s, c]])
    return fwd_a, fwd_a[:, :, :n2h], inv_a, fwd_b, inv_b, n2, n2h


def _stage_a_fwd(x_ref, wa_ref, y_ref, n2, scale=None):
    def body(i, carry):
        trips = [i * UNROLL + j for j in range(UNROLL)]
        xs = [x_ref[pl.ds(n1, wa_ref.shape[2], stride=N1), :] for n1 in trips]
        if scale is not None:
            xs = [x * scale for x in xs]
        prods = [_dot(wa_ref[n1], x.astype(BF16)) for n1, x in zip(trips, xs)]
        for n1, a in zip(trips, prods):
            y_ref[pl.ds(n1, n2, stride=2 * N1), :] = a[:n2]
            y_ref[pl.ds(N1 + n1, n2, stride=2 * N1), :] = a[n2:]
        return carry
    lax.fori_loop(0, N1 // UNROLL, body, 0)


def _filter_fft_kernel(k_ref, ss_ref, wa_ref, fb_ref, o_ref, y_ref, *, n2):
    scale = lax.rsqrt(ss_ref[0:1, :] + FILTER_NORM_EPS)
    _stage_a_fwd(k_ref, wa_ref, y_ref, n2, scale=scale)

    unr = min(UNROLL, n2)

    def body(i, carry):
        trips = [i * unr + j for j in range(unr)]
        ys = [y_ref[pl.ds(pl.multiple_of(f2 * 2 * N1, 2 * N1), 2 * N1), :].astype(BF16) for f2 in trips]
        for j in range(0, unr, 2):
            z = _dot(fb_ref[...], jnp.concatenate(ys[j:j + 2], axis=1))
            o_ref[trips[j]] = z[:, :LANES]
            o_ref[trips[j + 1]] = z[:, LANES:]
        return carry
    lax.fori_loop(0, n2 // unr, body, 0)


def _filter_spectrum(k2, ss, seq):
    fwd_a, _, _, fwd_b, _, n2, _ = _dft_tables(seq)
    ncol = k2.shape[1]
    nblk = ncol // LANES
    return pl.pallas_call(
        functools.partial(_filter_fft_kernel, n2=n2),
        grid=(nblk,),
        in_specs=[
            pl.BlockSpec((2 * seq, LANES), lambda c: (0, c)),
            pl.BlockSpec((8, LANES), lambda c: (0, c)),
            pl.BlockSpec(fwd_a.shape, lambda c: (0, 0, 0)),
            pl.BlockSpec(fwd_b.shape, lambda c: (0, 0)),
        ],
        out_specs=pl.BlockSpec((None, n2, 2 * N1, LANES), lambda c: (c, 0, 0, 0)),
        out_shape=jax.ShapeDtypeStruct((nblk, n2, 2 * N1, LANES), F32),
        scratch_shapes=[pltpu.VMEM((n2 * 2 * N1, LANES), F32)],
        compiler_params=_params("arbitrary"),
        name="hyena_filter_fft",
    )(k2, ss, fwd_a, fwd_b)


TILE = 8
N1_GROUPS = N1 // TILE


def _tile_tables(seq):
    _, fwd_a, inv_a, _, _, n2, n2h = _dft_tables_np(seq)
    eye = np.eye(TILE)
    fa = fwd_a.reshape(N1_GROUPS, TILE, 2 * n2, n2h)
    wa = np.einsum("qjrn,jk->qrjnk", fa, eye).reshape(N1_GROUPS, 2 * n2 * TILE, n2h * TILE)
    ia = inv_a.reshape(N1_GROUPS, TILE, n2h, 2 * n2)
    vc = np.einsum("qjnr,jk->qnjrk", ia, eye).reshape(N1_GROUPS, n2h * TILE, 2 * n2 * TILE)
    return jnp.asarray(wa, BF16), jnp.asarray(vc, BF16)


def _conv_kernel(u_ref, g_ref, skip_ref, fb_ref, ib_ref, kh_hbm, wa_hbm, vc_hbm, o_ref,
                 y_ref, kh_ref, wa_ref, vc_ref, sem, *, n2, n2h, kh_first):
    c_id, b_id = pl.program_id(0), pl.program_id(1)

    @pl.when(jnp.logical_and(c_id == 0, b_id == 0))
    def _():
        for src, dst in ((wa_hbm, wa_ref), (vc_hbm, vc_ref)):
            cp = pltpu.make_async_copy(src, dst, sem)
            cp.start()
            cp.wait()

    @pl.when(b_id == 0)
    def _():
        cp = pltpu.make_async_copy(kh_hbm.at[kh_first + c_id], kh_ref, sem)
        cp.start()
        cp.wait()

    def y_tile(rf, base):
        ri, f2 = divmod(rf, n2)
        return pl.ds(f2 * 2 * N1 + ri * N1 + base, TILE)

    def stage_a(q, carry):
        base = pl.multiple_of(q * TILE, TILE)
        x = jnp.concatenate([u_ref[pl.ds(N1 * m + base, TILE), :] for m in range(n2h)], axis=0)
        r = _dot(wa_ref[q], x.astype(BF16))
        for rf in range(2 * n2):
            y_ref[y_tile(rf, base), :] = r[rf * TILE:(rf + 1) * TILE]
        return carry
    lax.fori_loop(0, N1_GROUPS, stage_a, 0, unroll=8)

    unr = min(2 * UNROLL, n2)

    def mid(i, carry):
        trips = [i * unr + j for j in range(unr)]
        offs = [pl.multiple_of(f2 * 2 * N1, 2 * N1) for f2 in trips]
        wide = lambda blocks: [jnp.concatenate(blocks[j:j + 2], axis=1) for j in range(0, len(blocks), 2)]
        ys = wide([y_ref[pl.ds(off, 2 * N1), :].astype(BF16) for off in offs])
        khs = wide([kh_ref[f2] for f2 in trips])
        zs = [_dot(fb_ref[...], y) for y in ys]
        ps = []
        for z, kh in zip(zs, khs):
            zr, zi = z[:N1], z[N1:]
            kr, ki = kh[:N1], kh[N1:]
            ps.append(jnp.concatenate([zr * kr - zi * ki, zr * ki + zi * kr], axis=0).astype(BF16))
        gs = [_dot(ib_ref[...], p) for p in ps]
        for j, g in enumerate(gs):
            y_ref[pl.ds(offs[2 * j], 2 * N1), :] = g[:, :LANES]
            y_ref[pl.ds(offs[2 * j + 1], 2 * N1), :] = g[:, LANES:]
        return carry
    lax.fori_loop(0, n2 // unr, mid, 0)

    skip = skip_ref[...]

    def stage_c(q, carry):
        base = pl.multiple_of(q * TILE, TILE)
        g = jnp.concatenate([y_ref[y_tile(rf, base), :] for rf in range(2 * n2)], axis=0)
        conv = _dot(vc_ref[q], g.astype(BF16))
        for m in range(n2h):
            rows = pl.ds(N1 * m + base, TILE)
            o_ref[rows, :] = g_ref[rows, :] * (conv[m * TILE:(m + 1) * TILE] + u_ref[rows, :] * skip)
        return carry
    lax.fori_loop(0, N1_GROUPS, stage_c, 0, unroll=8)


def _long_conv_gate(u, u_col, gate, gate_col, khat, skip, order):
    bsz, seq, _ = u.shape
    ch = D_HYENA
    _, _, _, fwd_b, inv_b, n2, n2h = _dft_tables(seq)
    wa, vc = _tile_tables(seq)
    nblk = ch // LANES
    const = lambda a: pl.BlockSpec(a.shape, lambda c, b: (0,) * a.ndim)
    at = lambda col: pl.BlockSpec((None, seq, LANES), lambda c, b: (b, 0, col // LANES + c))
    hbm = pl.BlockSpec(memory_space=pl.ANY)
    return pl.pallas_call(
        functools.partial(_conv_kernel, n2=n2, n2h=n2h, kh_first=order * nblk),
        grid=(nblk, bsz),
        in_specs=[
            at(u_col), at(gate_col),
            pl.BlockSpec((1, LANES), lambda c, b: (0, c)),
            const(fwd_b), const(inv_b), hbm, hbm, hbm,
        ],
        out_specs=at(0),
        out_shape=jax.ShapeDtypeStruct((bsz, seq, ch), F32),
        scratch_shapes=[pltpu.VMEM((n2 * 2 * N1, LANES), F32), pltpu.VMEM(khat.shape[1:], F32),
                        pltpu.VMEM(wa.shape, BF16), pltpu.VMEM(vc.shape, BF16), pltpu.SemaphoreType.DMA(())],
        compiler_params=_params("arbitrary", "arbitrary"),
        name=f"hyena_conv{order}",
    )(u, gate, skip[order].reshape(1, ch), fwd_b, inv_b, khat, wa, vc)


HALO = 8


def _shift_rows(p, k):
    return pltpu.roll(p, k % p.shape[0], axis=0)


def _proj_kernel(xp_ref, x_ref, xn_ref, mod_ref, g1_ref, why_ref, wrkv_ref, wlora_ref,
                 cw_ref, cb_ref, murkv_ref, mulora_ref, w0_ref, a0_ref, wwah_ref, wwal_ref, gup_ref,
                 kk_ref, ka_ref, rk_ref, ones_ref,
                 uhy_ref, rkvk_ref, lwa_ref, g_ref, bonus_ref,
                 *, tt, nt):
    i = pl.program_id(1)
    xe = jnp.concatenate([xp_ref[...], x_ref[...], xn_ref[...]], axis=0)
    ms = jnp.mean(xe * xe, axis=-1, keepdims=True)
    h = xe * lax.rsqrt(ms + NORM_EPS) * g1_ref[...]
    h = h * (1.0 + mod_ref[1:2, :]) + mod_ref[0:1, :]
    row = lax.broadcasted_iota(jnp.int32, (tt + 2 * HALO, 1), 0)
    inside = jnp.logical_and(jnp.logical_or(row >= HALO, i > 0),
                             jnp.logical_or(row < tt + HALO, i < nt - 1))
    hb = jnp.where(inside, h, 0.0).astype(BF16)
    mid = slice(HALO, tt + HALO)

    p = _dot(hb, why_ref[...])
    u = (_shift_rows(p, 1) * cw_ref[0:1, :] + p * cw_ref[1:2, :]
         + _shift_rows(p, -1) * cw_ref[2:3, :] + cb_ref[...])
    uhy_ref[...] = u[mid]

    p = _dot(hb, wrkv_ref[...])
    p = p + murkv_ref[...] * (0.5 * (_shift_rows(p, 1) + _shift_rows(p, -1)) - p)
    p = p[mid]
    c = D_RWKV
    r, k, v = p[:, :c], p[:, c:2 * c], p[:, 2 * c:]
    rkvk_ref[:, :3 * c] = p

    q = _dot(hb, wlora_ref[...])
    q = q + mulora_ref[...] * (0.5 * (_shift_rows(q, 1) + _shift_rows(q, -1)) - q)
    q = q[mid]
    wa = q[:, :LANES]
    lane = lax.broadcasted_iota(jnp.int32, wa.shape, 1)
    wa = jnp.where(lane < 2 * DECAY_LORA, jnp.tanh(wa), wa)
    wah, wal = _split2(wa)
    up = _dot(wah, wwah_ref[...]) + (_dot(wah, wwal_ref[...]) + _dot(wal, wwah_ref[...]))
    lw = -math.exp(-0.5) * jax.nn.sigmoid(w0_ref[...] + up[:, :2 * c])
    a = jax.nn.sigmoid(a0_ref[...] + up[:, 2 * c:])
    for dd in range(2):
        lwa_ref[:, 2 * dd * c:(2 * dd + 1) * c] = lw[:, dd * c:(dd + 1) * c]
        lwa_ref[:, (2 * dd + 1) * c:(2 * dd + 2) * c] = a[:, dd * c:(dd + 1) * c]
    g_ref[...] = _dot3(jax.nn.sigmoid(q[:, LANES:]), gup_ref[...])

    ones = ones_ref[...]
    kk = k * kk_ref[...]
    nrm = jnp.sqrt(_dot_exact_rhs(kk * kk, ones))
    rkvk_ref[:, 3 * c:] = kk / jnp.maximum(nrm, 1e-12)
    ka = ka_ref[...]
    ksum = k * (2.0 + (a[:, :c] + a[:, c:] - 2.0) * ka)
    bonus_ref[...] = _dot_exact_rhs(r * ksum * rk_ref[...], ones) * v


def _head_ones():
    hid = np.arange(D_RWKV) // HEAD
    return jnp.asarray(hid[:, None] == hid[None, :], BF16)


def _projection(x, mod, norm1_g, w_in, hy_conv_w, hy_conv_b, rw_mu, rw_w0, rw_w_up, rw_a0,
                rw_a_up, rw_g_up, rw_k_k, rw_k_a, rw_r_k, tt=512):
    bsz, seq, d = x.shape
    tt = min(tt, seq)
    nt = seq // tt
    c = D_RWKV
    hy = (HYENA_ORDER + 1) * D_HYENA
    nlora = 2 * LANES
    w_hy = w_in[:, :hy].astype(BF16)
    w_rkv = w_in[:, hy:hy + 3 * c].astype(BF16)
    w_lora = jnp.zeros((d, nlora), F32).at[:, :w_in.shape[1] - hy - 3 * c].set(w_in[:, hy + 3 * c:]).astype(BF16)
    mu_rkv = rw_mu[:3 * c].reshape(1, 3 * c)
    mu_lora = jnp.zeros((1, nlora), F32).at[0, :rw_mu.shape[0] - 3 * c].set(rw_mu[3 * c:])
    wwa = jnp.zeros((LANES, 4 * c), F32)
    for dd in range(2):
        wwa = wwa.at[dd * DECAY_LORA:(dd + 1) * DECAY_LORA, dd * c:(dd + 1) * c].set(rw_w_up[dd])
        wwa = wwa.at[2 * DECAY_LORA + dd * ICLR_LORA:2 * DECAY_LORA + (dd + 1) * ICLR_LORA,
                     2 * c + dd * c:2 * c + (dd + 1) * c].set(rw_a_up[dd])
    gup = jnp.zeros((LANES, c), F32).at[:GATE_LORA].set(rw_g_up)
    row = lambda a: a.reshape(1, -1)

    nb8 = seq // HALO
    tb = tt // HALO
    const = lambda a: pl.BlockSpec(a.shape, lambda b, i: (0,) * a.ndim, pipeline_mode=pl.Buffered(1))
    tile = lambda w: pl.BlockSpec((None, tt, w), lambda b, i: (b, i, 0))
    ins = [
        (x, pl.BlockSpec((None, HALO, d), lambda b, i: (b, jnp.maximum(i * tb - 1, 0), 0))),
        (x, pl.BlockSpec((None, tt, d), lambda b, i: (b, i, 0))),
        (x, pl.BlockSpec((None, HALO, d), lambda b, i: (b, jnp.minimum((i + 1) * tb, nb8 - 1), 0))),
        (mod, pl.BlockSpec((None,) + mod.shape[1:], lambda b, i: (b, 0, 0))),
    ]
    consts = [row(norm1_g), w_hy, w_rkv, w_lora, hy_conv_w, row(hy_conv_b), mu_rkv, mu_lora,
              row(rw_w0), row(rw_a0), *_split2(wwa), gup, row(rw_k_k), row(rw_k_a), row(rw_r_k), _head_ones()]
    ins += [(a, const(a)) for a in consts]
    widths = [hy, 4 * c, 4 * c, c, c]
    return pl.pallas_call(
        functools.partial(_proj_kernel, tt=tt, nt=nt),
        grid=(bsz, nt),
        in_specs=[s for _, s in ins],
        out_specs=[tile(w) for w in widths],
        out_shape=[jax.ShapeDtypeStruct((bsz, seq, w), F32) for w in widths],
        compiler_params=_params("arbitrary", "arbitrary"),
        name="input_projection",
    )(*[a for a, _ in ins])


CHUNK = HEAD
GROUP = MXU_DIM // HEAD


def _nt(a, b):
    return lax.dot_general(a, b, (((1,), (1,)), ((), ())), preferred_element_type=F32)


def _tn(a, b):
    return lax.dot_general(a, b, (((0,), (0,)), ((), ())), preferred_element_type=F32)


def _wkv_direction(r, k, v, kk, lw, a, ka, s_ref, reverse):
    c = CHUNK
    ti = lax.broadcasted_iota(jnp.int32, (c, c), 0)
    si = lax.broadcasted_iota(jnp.int32, (c, c), 1)
    tri = (si >= ti) if reverse else (si <= ti)
    cum = _dot_exact_lhs(jnp.where(tri, 1.0, 0.0).astype(BF16), lw)
    tot = jnp.sum(lw, axis=0, keepdims=True)
    w_incl = jnp.exp(cum)
    w_prev = jnp.exp(cum - lw)
    w_inv = jnp.exp(-cum)
    w_end = jnp.exp(tot - cum)
    w_tot = jnp.exp(tot)
    kd = k * (1.0 + (a - 1.0) * ka)
    b = kk * a
    a_w = -kk * w_prev
    r_w = r * w_incl
    b_w = b * w_inv
    k_w = kd * w_inv
    b_e = b * w_end
    k_e = kd * w_end

    m = MXU_DIM
    ri = lax.broadcasted_iota(jnp.int32, (m, m), 0)
    ci = lax.broadcasted_iota(jnp.int32, (m, m), 1)
    head_mask = (ri // HEAD) == (ci // HEAD)
    tl = lax.broadcasted_iota(jnp.int32, (c, m), 0)
    sl = lax.broadcasted_iota(jnp.int32, (c, m), 1) % c
    strict = (sl > tl) if reverse else (sl < tl)
    incl = (sl >= tl) if reverse else (sl <= tl)
    eye = jnp.where(sl == tl, 1.0, 0.0)
    both = lambda top, bot: jnp.concatenate([top, bot], axis=0)

    def stack(xg):
        xb = xg.astype(BF16)
        return jnp.where(head_mask, jnp.concatenate([xb] * GROUP, axis=0), jnp.zeros((), BF16))

    streams = []
    for g in range(D_RWKV // m):
        sl_g = slice(g * m, (g + 1) * m)
        streams.append(dict(
            ar=both(a_w[:, sl_g], r_w[:, sl_g]).astype(BF16),
            b_st=stack(b_w[:, sl_g]), k_st=stack(k_w[:, sl_g]), v_st=stack(v[:, sl_g]),
            v=v[:, sl_g], bk=both(b_e[:, sl_g], k_e[:, sl_g]).astype(BF16),
            w_tot=w_tot[:, sl_g], s_ref=s_ref.at[g],
            strict=strict, incl=incl, eye=eye, head_mask=head_mask, stack=stack))
    return streams


def _wkv_streams_step(streams):
    c = CHUNK
    both = lambda top, bot: jnp.concatenate([top, bot], axis=0)
    for st in streams:
        st["s"] = st["s_ref"][...]
        st["xb"] = _nt(st["ar"], st["b_st"])
        st["xk"] = _nt(st["ar"], st["k_st"])
        st["xs"] = _nt(st["ar"], st["s"].astype(BF16))
    for st in streams:
        m_k = both(jnp.where(st["strict"], st["xk"][:c], 0.0), jnp.where(st["incl"], st["xk"][c:], 0.0))
        st["kv"] = _dot(m_k.astype(BF16), st["v_st"])
        st["rhs"] = st["xs"][:c] + st["kv"][:c]
        st["pw"] = jnp.where(st["strict"], st["xb"][:c], 0.0)
        st["t"] = st["eye"] + st["pw"]
        st["p_st"] = st["stack"](st["pw"])
    levels = int(math.log2(c)) - 1
    for st in streams:
        st["pw"] = _dot(st["pw"].astype(BF16), st["p_st"])
        st["p_st"] = st["stack"](st["pw"])
    for lvl in range(1, levels + 1):
        for st in streams:
            if lvl < levels:
                prod = _dot(both(st["pw"], st["t"]).astype(BF16), st["p_st"])
                st["pw"] = prod[:c]
                st["t"] = st["t"] + prod[c:]
                st["p_st"] = st["stack"](st["pw"])
            else:
                st["t"] = st["t"] + _dot(st["t"].astype(BF16), st["p_st"])
    for st in streams:
        st["u"] = _dot(st["t"].astype(BF16), st["stack"](st["rhs"]))
    outs = []
    for st in streams:
        m_rb = jnp.where(st["incl"], st["xb"][c:], 0.0)
        outs.append(st["xs"][c:] + _dot(m_rb.astype(BF16), st["stack"](st["u"])) + st["kv"][c:])
        uv = both(st["u"], st["v"]).astype(BF16)
        st["s_ref"][...] = st["s"] * st["w_tot"] + jnp.where(st["head_mask"], _tn(uv, st["bk"]), 0.0)
    return outs


def _dot_exact_lhs(tri_bf16, x):
    xh, xl = _split2(x)
    return _dot(tri_bf16, xh) + _dot(tri_bf16, xl)


def _wkv_kernel(rkvk_f, lwa_f, rkvk_b, lwa_b, ka_ref, of_ref, ob_ref, s_ref, *, nch):
    @pl.when(pl.program_id(1) == 0)
    def _():
        s_ref[...] = jnp.zeros_like(s_ref)

    ka = ka_ref[...]
    c = D_RWKV

    def operands(rkvk_ref, lwa_ref, rows):
        x = rkvk_ref[rows, :]
        la = lwa_ref[rows, :]
        return x[:, :c], x[:, c:2 * c], x[:, 2 * c:3 * c], x[:, 3 * c:], la[:, :c], la[:, c:]

    for ci in range(nch):
        rows_f = slice(ci * CHUNK, (ci + 1) * CHUNK)
        rows_b = slice((nch - 1 - ci) * CHUNK, (nch - ci) * CHUNK)
        fwd = _wkv_direction(*operands(rkvk_f, lwa_f, rows_f), ka, s_ref.at[0], False)
        bwd = _wkv_direction(*operands(rkvk_b, lwa_b, rows_b), ka, s_ref.at[1], True)
        outs = _wkv_streams_step(fwd + bwd)
        of_ref[rows_f, :] = jnp.concatenate(outs[:len(fwd)], axis=1)
        ob_ref[rows_b, :] = jnp.concatenate(outs[len(fwd):], axis=1)


WKV_CHUNKS_PER_STEP = 8


def _wkv(rkvk, lwa, rw_k_a):
    bsz, seq, _ = rkvk.shape
    c = D_RWKV
    nch = WKV_CHUNKS_PER_STEP if seq % (WKV_CHUNKS_PER_STEP * CHUNK) == 0 else 1
    rows = nch * CHUNK
    nb = seq // rows
    fwd = lambda w, lane_blk: pl.BlockSpec((None, rows, w), lambda b, j: (b, j, lane_blk))
    bwd = lambda w, lane_blk: pl.BlockSpec((None, rows, w), lambda b, j: (b, nb - 1 - j, lane_blk))
    return pl.pallas_call(
        functools.partial(_wkv_kernel, nch=nch),
        grid=(bsz, nb),
        in_specs=[fwd(4 * c, 0), fwd(2 * c, 0), bwd(4 * c, 0), bwd(2 * c, 1),
                  pl.BlockSpec((1, c), lambda b, j: (0, 0))],
        out_specs=[fwd(c, 0), bwd(c, 0)],
        out_shape=[jax.ShapeDtypeStruct((bsz, seq, c), F32)] * 2,
        scratch_shapes=[pltpu.VMEM((2, c // MXU_DIM, MXU_DIM, MXU_DIM), F32)],
        compiler_params=_params("arbitrary", "arbitrary"),
        name="wkv7_chunked",
    )(rkvk, lwa, rkvk, lwa, rw_k_a.reshape(1, c))


NEG_INF = float("-inf")


def _first_max(vals, idx, size):
    m = jnp.max(vals, axis=0, keepdims=True)
    i = jnp.min(jnp.where(vals == m, idx, size), axis=0, keepdims=True)
    return m, i


def _route(scores, biased):
    e, tt = scores.shape
    per = e // N_GROUPS
    rowl = lax.broadcasted_iota(jnp.int32, (per, tt), 0)
    gs = []
    for g in range(N_GROUPS):
        blk = biased[g * per:(g + 1) * per]
        m1, i1 = _first_max(blk, rowl, per)
        m2 = jnp.max(jnp.where(rowl == i1, NEG_INF, blk), axis=0, keepdims=True)
        gs.append(m1 + m2)
    cur = jnp.concatenate(gs, axis=0)
    growl = lax.broadcasted_iota(jnp.int32, (N_GROUPS, tt), 0)
    gsel = jnp.zeros((N_GROUPS, tt), F32)
    for _ in range(TOPK_GROUPS):
        _, ig = _first_max(cur, growl, N_GROUPS)
        hit = growl == ig
        gsel = jnp.where(hit, 1.0, gsel)
        cur = jnp.where(hit, NEG_INF, cur)
    emask = jnp.concatenate([jnp.broadcast_to(gsel[g:g + 1], (per, tt)) for g in range(N_GROUPS)], axis=0)
    masked = jnp.where(emask > 0.5, biased, NEG_INF)
    row = lax.broadcasted_iota(jnp.int32, (e, tt), 0)
    ids, ws = [], []
    for _ in range(TOP_K):
        _, ie = _first_max(masked, row, e)
        hit = row == ie
        ids.append(ie)
        ws.append(jnp.sum(jnp.where(hit, scores, 0.0), axis=0, keepdims=True))
        masked = jnp.where(hit, NEG_INF, masked)
    w = jnp.concatenate(ws, axis=0)
    w = w / jnp.sum(w, axis=0, keepdims=True) * ROUTE_SCALE
    return jnp.concatenate(ids, axis=0), w


def _mixout_kernel(x_ref, mod_ref, yhy_ref, of_ref, ob_ref, g_ref, bonus_ref, lnw_ref, lnb_ref,
                   ones_ref, wout_ref, g2n_ref, rwth_ref, rwtl_ref, bias_ref,
                   x1_ref, h2a_ref, h2b_ref, eid_ref, wsel_ref):
    ones = ones_ref[...]
    s = of_ref[...] + ob_ref[...]
    mean = _dot_exact_rhs(s, ones) * (1.0 / HEAD)
    dlt = s - mean
    var = _dot_exact_rhs(dlt * dlt, ones) * (1.0 / HEAD)
    sn = dlt * lax.rsqrt(var + GN_EPS) * lnw_ref[...] + lnb_ref[...]
    yrw = (sn + bonus_ref[...]) * g_ref[...]
    ch = yhy_ref.shape[-1]
    mix = _dot(yhy_ref[...].astype(BF16), wout_ref[:ch, :]) + _dot(yrw.astype(BF16), wout_ref[ch:, :])
    x1 = x_ref[...] + mod_ref[2:3, :] * mix
    x1_ref[...] = x1
    ms = jnp.mean(x1 * x1, axis=-1, keepdims=True)
    h2 = x1 * lax.rsqrt(ms + NORM_EPS) * g2n_ref[...]
    h2 = h2 * (1.0 + mod_ref[4:5, :]) + mod_ref[3:4, :]
    h2a_ref[...], h2b_ref[...] = _pack_rows(h2)
    rh, rl = rwth_ref[...], rwtl_ref[...]
    hh, hl = _split2(h2)
    logits = _nt(rh, hh) + (_nt(rh, hl) + _nt(rl, hh))
    scores = jax.nn.sigmoid(logits)
    ids, w = _route(scores, scores + bias_ref[...])
    eid_ref[...] = ids
    wsel_ref[...] = w


def _mix_out(x, mod, yhy, o_f, o_b, g, bonus, ln_w, ln_b, w_out, norm2_g, router_w, router_bias, tt=1024):
    bsz, seq, d = x.shape
    tt = min(tt, seq)
    nt = seq // tt
    n = bsz * seq
    c = D_RWKV
    e = router_w.shape[1]
    row = lambda a: a.reshape(1, -1)
    consts = [row(ln_w), row(ln_b), _head_ones(), w_out.astype(BF16), row(norm2_g), *_split2(router_w.T),
              jnp.broadcast_to(router_bias.reshape(e, 1), (e, tt))]
    const = lambda a: pl.BlockSpec(a.shape, lambda b, i: (0,) * a.ndim, pipeline_mode=pl.Buffered(1))
    tile = lambda w: pl.BlockSpec((None, tt, w), lambda b, i: (b, i, 0))
    flat = lambda rows, dt: jax.ShapeDtypeStruct((rows, n), dt)
    return pl.pallas_call(
        _mixout_kernel,
        grid=(bsz, nt),
        in_specs=[tile(d), pl.BlockSpec((None,) + mod.shape[1:], lambda b, i: (b, 0, 0))]
        + [tile(c)] * 5 + [const(a) for a in consts],
        out_specs=[tile(d), pl.BlockSpec((tt, d // 4), lambda b, i: (b * nt + i, 0)),
                   pl.BlockSpec((tt, d // 4), lambda b, i: (b * nt + i, 0)),
                   pl.BlockSpec((TOP_K, tt), lambda b, i: (0, b * nt + i)),
                   pl.BlockSpec((TOP_K, tt), lambda b, i: (0, b * nt + i))],
        out_shape=[jax.ShapeDtypeStruct((bsz, seq, d), F32), jax.ShapeDtypeStruct((n, d // 4), U32),
                   jax.ShapeDtypeStruct((n, d // 4), U32),
                   flat(TOP_K, jnp.int32), flat(TOP_K, F32)],
        compiler_params=_params("arbitrary", "arbitrary"),
        name="mix_out_router",
    )(x, mod, yhy, o_f, o_b, g, bonus, *consts)


BLK = 512
BLK_SHIFT = 9


def _multi_hot(eid, e):
    row = lax.broadcasted_iota(jnp.int32, (e, eid.shape[1]), 0)
    mh = jnp.zeros((e, eid.shape[1]), F32)
    for kk in range(TOP_K):
        mh = mh + jnp.where(row == eid[kk:kk + 1, :], 1.0, 0.0)
    return row, mh


def _lookup(row, eid, table):
    return jnp.concatenate(
        [jnp.sum(jnp.where(row == eid[kk:kk + 1, :], table, 0.0), axis=0, keepdims=True)
         for kk in range(TOP_K)], axis=0)


def _rank_kernel(eid_ref, rank_ref, cnt_ref, *, e):
    @pl.when(pl.program_id(0) == 0)
    def _():
        cnt_ref[...] = jnp.zeros_like(cnt_ref)

    eid = eid_ref[...]
    tt = eid.shape[1]
    row, mh = _multi_hot(eid, e)
    mhb = mh.astype(BF16)
    si = lax.broadcasted_iota(jnp.int32, (tt, tt), 0)
    ti = lax.broadcasted_iota(jnp.int32, (tt, tt), 1)
    earlier = _dot(mhb, jnp.where(si < ti, 1.0, 0.0).astype(BF16))
    cnt = cnt_ref[...]
    full = earlier + jnp.concatenate([cnt] * (tt // LANES), axis=1)
    rank_ref[...] = _lookup(row, eid, full).astype(jnp.int32)
    cnt_ref[...] = cnt + _dot(mhb, jnp.ones((tt, LANES), BF16))


def _expert_ranks(eid, e, tt=512):
    n = eid.shape[1]
    tt = min(tt, n)
    return pl.pallas_call(
        functools.partial(_rank_kernel, e=e),
        grid=(n // tt,),
        in_specs=[pl.BlockSpec((TOP_K, tt), lambda i: (0, i))],
        out_specs=[pl.BlockSpec((TOP_K, tt), lambda i: (0, i)),
                   pl.BlockSpec((e, LANES), lambda i: (0, 0))],
        out_shape=[jax.ShapeDtypeStruct((TOP_K, n), jnp.int32), jax.ShapeDtypeStruct((e, LANES), F32)],
        compiler_params=_params("arbitrary"),
        name="expert_ranks",
    )(eid)


def _block_offsets(cnt):
    e = cnt.shape[0]
    nblk = ((cnt.astype(jnp.int32) + (BLK - 1)) >> BLK_SHIFT).astype(F32)
    ri = lax.broadcasted_iota(jnp.int32, (e, e), 0)
    ci = lax.broadcasted_iota(jnp.int32, (e, e), 1)
    tril = jnp.where(ci <= ri, 1.0, 0.0).astype(BF16)
    nh, nl = _split2(nblk)
    return nblk, _dot(tril, nh) + _dot(tril, nl)


def _dest_kernel(cnt_ref, eid_ref, rank_ref, dest_ref):
    nblk, end = _block_offsets(cnt_ref[...])
    off = (end - nblk) * float(BLK)
    eid = eid_ref[...]
    tt = eid.shape[1]
    row = lax.broadcasted_iota(jnp.int32, (off.shape[0], tt), 0)
    table = jnp.concatenate([off] * (tt // LANES), axis=1)
    dest_ref[...] = _lookup(row, eid, table).astype(jnp.int32) + rank_ref[...]


def _destinations(cnt, eid, rank, tt=512):
    n = eid.shape[1]
    tt = min(tt, n)
    blk = pl.BlockSpec((TOP_K, tt), lambda i: (0, i))
    return pl.pallas_call(
        _dest_kernel,
        grid=(n // tt,),
        in_specs=[pl.BlockSpec(cnt.shape, lambda i: (0, 0)), blk, blk],
        out_specs=blk,
        out_shape=jax.ShapeDtypeStruct((TOP_K, n), jnp.int32),
        compiler_params=_params("arbitrary"),
        name="expert_destinations",
    )(cnt, eid, rank)


def _meta_kernel(cnt_ref, meta_ref, *, nbp):
    cnt = cnt_ref[...]
    e = cnt.shape[0]
    nblk, end = _block_offsets(cnt)
    rep = lambda a, w: jnp.concatenate([a] * (w // LANES), axis=1)
    b = lax.broadcasted_iota(jnp.int32, (e, nbp), 1).astype(F32)
    blk_e = jnp.minimum(jnp.sum(jnp.where(rep(end, nbp) <= b, 1.0, 0.0), axis=0, keepdims=True), float(e - 1))
    row = lax.broadcasted_iota(jnp.int32, (e, nbp), 0).astype(F32)
    mine = row == blk_e
    left = rep(cnt + (end - nblk) * float(BLK), nbp) - b * float(BLK)
    nvalid = jnp.clip(jnp.sum(jnp.where(mine, left, 0.0), axis=0, keepdims=True), 0.0, float(BLK))
    nused = jnp.max(rep(end, nbp), axis=0, keepdims=True)
    later = jnp.logical_and(row > blk_e, rep(nblk, nbp) > 0.0)
    nxt = jnp.min(jnp.where(later, row, float(e)), axis=0, keepdims=True)
    nxt = jnp.where(nxt >= float(e), -1.0, nxt)
    meta_ref[...] = jnp.concatenate([blk_e, nvalid, nused, nxt, jnp.zeros((4, nbp), F32)],
                                    axis=0).astype(jnp.int32)


def _block_meta(cnt, nb):
    nbp = -(-nb // LANES) * LANES
    return pl.pallas_call(
        functools.partial(_meta_kernel, nbp=nbp),
        out_shape=jax.ShapeDtypeStruct((8, nbp), jnp.int32),
        compiler_params=pltpu.CompilerParams(vmem_limit_bytes=VMEM_LIMIT),
        name="expert_block_meta",
    )(cnt)


SC_WINDOW = 128


def _sc_mesh():
    return plsc.VectorSubcoreMesh(core_axis_name="core", subcore_axis_name="subcore")


def _sc_scatter_rows(rows, idx, nrows):
    n, width = rows.shape

    @pl.kernel(out_type=jax.ShapeDtypeStruct((nrows, width), rows.dtype), mesh=_sc_mesh())
    def scatter(rows_hbm, idx_hbm, out_hbm):
        def body(rows_vmem, idx_vmem):
            pltpu.sync_copy(rows_vmem, out_hbm.at[idx_vmem.at[0]])

        pltpu.emit_pipeline(
            body,
            grid=(n // SC_WINDOW, idx.shape[0]),
            in_specs=[pl.BlockSpec((SC_WINDOW, width), index_map=lambda i, k: (i, 0)),
                      pl.BlockSpec((1, SC_WINDOW), index_map=lambda i, k: (k, i))],
            out_specs=[],
            core_axis_name=("core", "subcore"),
            dimension_semantics=(pltpu.PARALLEL, pltpu.ARBITRARY),
        )(rows_hbm, idx_hbm)

    return scatter(rows, idx)


def _sc_gather_rows(src, idx):
    num = idx.shape[1]
    width = src.shape[1]

    @pl.kernel(out_type=jax.ShapeDtypeStruct((num, width), src.dtype), mesh=_sc_mesh())
    def gather(src_hbm, idx_hbm, out_hbm):
        def body(idx_vmem, out_vmem):
            pltpu.sync_copy(src_hbm.at[idx_vmem.at[0]], out_vmem)

        pltpu.emit_pipeline(
            body,
            grid=(num // SC_WINDOW,),
            in_specs=[pl.BlockSpec((1, SC_WINDOW), index_map=lambda i: (0, i))],
            out_specs=[pl.BlockSpec((SC_WINDOW, width), index_map=lambda i: (i, 0))],
            core_axis_name=("core", "subcore"),
            dimension_semantics=(pltpu.PARALLEL,),
        )(idx_hbm, out_hbm)

    return gather(src, idx)


BLOCKS_PER_STEP = 2


def _experts_kernel(be_ref, nv_ref, nxt_ref, nu_ref, xa_ref, xb_ref, wg_hbm, wu_hbm, wd_hbm, oa_ref, ob_ref,
                    wgf, wuf, wdf, wgb, wub, wdb, sems, slot_ref):
    step = pl.program_id(0)

    def fetch(expert, slot):
        return [pltpu.make_async_copy(src.at[expert], dst.at[slot], sems.at[slot])
                for src, dst in ((wg_hbm, wgf), (wu_hbm, wuf), (wd_hbm, wdf))]

    @pl.when(step == 0)
    def _():
        slot_ref[0] = 0
        for cp in fetch(be_ref[0], 0):
            cp.start()

    def one_block(b, rows):
        prev = be_ref[jnp.maximum(b - 1, 0)]

        @pl.when(jnp.logical_or(b == 0, be_ref[b] != prev))
        def _():
            slot = slot_ref[0]
            for cp in fetch(be_ref[b], slot):
                cp.wait()

            @pl.when(nxt_ref[b] >= 0)
            def _():
                for cp in fetch(nxt_ref[b], 1 - slot):
                    cp.start()

            wgb[...] = wgf[slot].astype(BF16)
            wub[...] = wuf[slot].astype(BF16)
            wdb[...] = wdf[slot].astype(BF16)
            slot_ref[0] = 1 - slot

        valid = lax.broadcasted_iota(jnp.int32, (BLK, 1), 0) < nv_ref[b]
        zero = jnp.zeros((), U32)
        x = _unpack_rows(jnp.where(valid, xa_ref[rows, :], zero), jnp.where(valid, xb_ref[rows, :], zero))
        act = _silu(_dot(x, wgb[...])) * _dot(x, wub[...])
        oa_ref[rows, :], ob_ref[rows, :] = _pack_rows(_dot(act.astype(BF16), wdb[...]))

    for j in range(BLOCKS_PER_STEP):
        b = step * BLOCKS_PER_STEP + j
        rows = slice(j * BLK, (j + 1) * BLK)
        if j == 0:
            one_block(b, rows)
        else:
            pl.when(b < nu_ref[0])(functools.partial(one_block, b, rows))


def _experts(blk_e, nvalid, nused, nxt_e, xs_a, xs_b, wg, wu, wd):
    p, dq = xs_a.shape
    d, de = wg.shape[1], wg.shape[2]
    rows_in = pl.BlockSpec((BLOCKS_PER_STEP * BLK, dq), lambda b, be, nv, nx, nu: (b, 0))
    hbm = pl.BlockSpec(memory_space=pl.ANY)
    return pl.pallas_call(
        _experts_kernel,
        grid_spec=pltpu.PrefetchScalarGridSpec(
            num_scalar_prefetch=4,
            grid=((nused + BLOCKS_PER_STEP - 1) // BLOCKS_PER_STEP,),
            in_specs=[rows_in, rows_in, hbm, hbm, hbm],
            out_specs=[rows_in, rows_in],
            scratch_shapes=[pltpu.VMEM((2, d, de), F32), pltpu.VMEM((2, d, de), F32), pltpu.VMEM((2, de, d), F32),
                            pltpu.VMEM((d, de), BF16), pltpu.VMEM((d, de), BF16), pltpu.VMEM((de, d), BF16),
                            pltpu.SemaphoreType.DMA((2,)), pltpu.SMEM((1,), jnp.int32)],
        ),
        out_shape=[jax.ShapeDtypeStruct((p, dq), U32)] * 2,
        compiler_params=_params("arbitrary"),
        name="moe_experts",
    )(blk_e, nvalid, nxt_e, nused.reshape(1), xs_a, xs_b, wg, wu, wd)


def _shared_kernel(ha_ref, hb_ref, sg_ref, su_ref, sd_ref, o_ref):
    hb = _unpack_rows(ha_ref[...], hb_ref[...])
    act = _silu(_dot(hb, sg_ref[...])) * _dot(hb, su_ref[...])
    o_ref[...] = _dot(act.astype(BF16), sd_ref[...]).astype(o_ref.dtype)


def _shared_expert(h2a, h2b, sh_wg, sh_wu, sh_wd, tt=512):
    n, dp = h2a.shape
    d = sh_wg.shape[0]
    tt = min(tt, n)
    consts = [sh_wg.astype(BF16), sh_wu.astype(BF16), sh_wd.astype(BF16)]
    packed_rows = pl.BlockSpec((tt, dp), lambda i: (i, 0))
    return pl.pallas_call(
        _shared_kernel,
        grid=(n // tt,),
        in_specs=[packed_rows, packed_rows] + [pl.BlockSpec(a.shape, lambda i: (0, 0)) for a in consts],
        out_specs=pl.BlockSpec((tt, d), lambda i: (i, 0)),
        out_shape=jax.ShapeDtypeStruct((n, d), BF16),
        compiler_params=_params("arbitrary"),
        name="shared_expert",
    )(h2a, h2b, *consts)


def _combine_kernel(w_ref, x1_ref, sh_ref, mod_ref, ga_ref, gb_ref, gf_ref, sel_ref, o_ref):
    ffn = sh_ref[...].astype(F32)
    wh, wl = _split2(w_ref[...])
    acc = None
    for kk in range(TOP_K):
        sel = sel_ref[kk]
        wk = _tn(wh, sel) + _tn(wl, sel)
        a_lo, a_hi = _unpack_halves(ga_ref[kk])
        b_lo, b_hi = _unpack_halves(gb_ref[kk])
        parts = [a_lo * wk, b_lo * wk, a_hi * wk, b_hi * wk]
        acc = parts if acc is None else [p + q for p, q in zip(acc, parts)]
    ffn = ffn + jnp.concatenate(acc, axis=1)
    xo = x1_ref[...] + mod_ref[5:6, :] * ffn
    ms = jnp.mean(xo * xo, axis=-1, keepdims=True)
    o_ref[...] = xo * lax.rsqrt(ms + NORM_EPS) * gf_ref[...]


def _combine(wsel, x1, shared, mod, ga, gb, normf_g, tok0, seq, tt=256):
    n, d = x1.shape
    part = ga.shape[1]
    tt = min(tt, seq, part)
    per = seq // tt
    off = tok0 // tt
    dq = ga.shape[2]
    sel = jnp.asarray(np.broadcast_to(np.eye(TOP_K)[:, :, None], (TOP_K, TOP_K, dq)), BF16)
    consts = [normf_g.reshape(1, d), sel]
    const = lambda a: pl.BlockSpec(a.shape, lambda i: (0,) * a.ndim)
    rows = pl.BlockSpec((tt, d), lambda i: (off + i, 0))
    gathered = pl.BlockSpec((TOP_K, tt, dq), lambda i: (0, i, 0))
    return pl.pallas_call(
        _combine_kernel,
        grid=(part // tt,),
        in_specs=[pl.BlockSpec((TOP_K, tt), lambda i: (0, off + i)),
                  rows, rows,
                  pl.BlockSpec((None,) + mod.shape[1:], lambda i: ((off + i) // per, 0, 0)),
                  gathered, gathered] + [const(a) for a in consts],
        out_specs=rows,
        out_shape=jax.ShapeDtypeStruct((n, d), F32),
        input_output_aliases={1: 0},
        compiler_params=_params("arbitrary"),
        name="moe_combine",
    )(wsel, x1, shared, mod, ga, gb, *consts)


COMBINE_PARTS = 8


def _moe(x1, h2a, h2b, mod, eid, wsel, exp_wg, exp_wu, exp_wd, sh_wg, sh_wu, sh_wd, normf_g):
    n = h2a.shape[0]
    e = exp_wg.shape[0]
    nb = (n * TOP_K + e * (BLK - 1)) // BLK
    nb = -(-nb // BLOCKS_PER_STEP) * BLOCKS_PER_STEP
    rank, cnt = _expert_ranks(eid, e)
    dest = _destinations(cnt, eid, rank)
    meta = _block_meta(cnt, nb)
    xs_a = _sc_scatter_rows(h2a, dest, nb * BLK)
    xs_b = _sc_scatter_rows(h2b, dest, nb * BLK)
    shared = _shared_expert(h2a, h2b, sh_wg, sh_wu, sh_wd)
    ys_a, ys_b = _experts(meta[0, :nb], meta[1, :nb], meta[2, 0], meta[3, :nb], xs_a, xs_b,
                          exp_wg, exp_wu, exp_wd)
    bsz, seq, d = x1.shape
    out = x1.reshape(n, d)
    part = n // COMBINE_PARTS
    for j in range(COMBINE_PARTS):
        idx = dest[:, j * part:(j + 1) * part].reshape(1, TOP_K * part)
        ga = _sc_gather_rows(ys_a, idx).reshape(TOP_K, part, -1)
        gb = _sc_gather_rows(ys_b, idx).reshape(TOP_K, part, -1)
        out = _combine(wsel, out, shared, mod, ga, gb, normf_g, j * part, seq)
    return out


def kernel(x, c, norm1_g, norm2_g, normf_g, w_ada, b_ada, w_in, w_out, hy_conv_w, hy_conv_b, hy_pos_w1, hy_pos_b1, hy_pos_w2, hy_pos_b2, hy_pos_w3, hy_sin_freq, hy_skip, rw_mu, rw_w0, rw_w_up, rw_a0, rw_a_up, rw_g_up, rw_k_k, rw_k_a, rw_r_k, rw_ln_w, rw_ln_b, router_w, router_bias, exp_w_gate, exp_w_up, exp_w_down, sh_w_gate, sh_w_up, sh_w_down):
    bsz, seq, d = x.shape
    depth = w_ada.shape[0]
    assert depth == 1, "the final norm is fused into the last kernel of a single layer"
    for l in range(depth):
        mod = _modulation(c, w_ada[l], b_ada[l]).reshape(bsz, -1, d)
        uhy, rkvk, lwa, g, bonus = _projection(
            x, mod, norm1_g[l], w_in[l], hy_conv_w[l], hy_conv_b[l], rw_mu[l], rw_w0[l], rw_w_up[l],
            rw_a0[l], rw_a_up[l], rw_g_up[l], rw_k_k[l], rw_k_a[l], rw_r_k[l])
        k2, ss = _hyena_filters(seq, hy_pos_w1[l], hy_pos_b1[l], hy_pos_w2[l], hy_pos_b2[l],
                                hy_pos_w3[l], hy_sin_freq[l])
        khat = _filter_spectrum(k2, ss, seq)
        z, z_col = uhy, 0
        for order in range(HYENA_ORDER):
            z = _long_conv_gate(z, z_col, uhy, (order + 1) * D_HYENA, khat, hy_skip[l], order)
            z_col = 0
        o_f, o_b = _wkv(rkvk, lwa, rw_k_a[l])
        x1, h2a, h2b, eid, wsel = _mix_out(x, mod, z, o_f, o_b, g, bonus, rw_ln_w[l], rw_ln_b[l], w_out[l],
                                           norm2_g[l], router_w[l], router_bias[l])
        x = _moe(x1, h2a, h2b, mod, eid, wsel, exp_w_gate[l], exp_w_up[l], exp_w_down[l],
                 sh_w_gate[l], sh_w_up[l], sh_w_down[l], normf_g)
        x = x.reshape(bsz, seq, d)
    return x
```
